```python
import jax, jax.numpy as jnp
from jax import lax
import numpy as np

D_MODEL = 1024
BATCH = 8
SEQ = 8192
DEPTH = 1

POOL_WINDOWS = (2, 4, 8, 16)
POOL_WIDTH = D_MODEL // 2
POOL_GROUP = POOL_WIDTH // len(POOL_WINDOWS)
HEAD_DIM = 64
N_HEADS = (D_MODEL // 2) // HEAD_DIM
N_KV_HEADS = 2
ATTN_WIDTH = N_HEADS * HEAD_DIM
KV_WIDTH = N_KV_HEADS * HEAD_DIM
WINDOW = 128
BLOCK = 128
ROPE_THETA = 500000.0
ROT_DIM = HEAD_DIM // 4
MIX_WIDTH = POOL_WIDTH + ATTN_WIDTH
IN_WIDTH = POOL_WIDTH + ATTN_WIDTH + 2 * KV_WIDTH
D_FF = 2816
EPS = 1e-6
NEG_INF = -1e30

kernel_name = "hybrid_pool_swa_macaron_block"


def rms_norm(x, g):
    xf = x.astype(jnp.float32)
    y = xf * lax.rsqrt(jnp.mean(xf * xf, axis=-1, keepdims=True) + EPS)
    return (y * g.astype(jnp.float32)).astype(x.dtype)


def swiglu(h, w_gu, w_down):
    gate, up = jnp.split(h @ w_gu, 2, axis=-1)
    return (jax.nn.silu(gate) * up) @ w_down


def pool_mix(u, w_pool, pool_scale):
    B, S, C = u.shape
    uf = u.astype(jnp.float32)
    cs = jnp.concatenate([jnp.zeros((B, 1, C), jnp.float32), jnp.cumsum(uf, axis=1)], axis=1)
    t = jnp.arange(S)
    outs = []
    for g, w in enumerate(POOL_WINDOWS):
        lo, hi = g * POOL_GROUP, (g + 1) * POOL_GROUP
        start = jnp.maximum(t + 1 - w, 0)
        cnt = (t + 1 - start).astype(jnp.float32)
        csg = cs[:, :, lo:hi]
        mean = (csg[:, 1:] - csg[:, start]) / cnt[None, :, None]
        outs.append(mean - uf[:, :, lo:hi])
    d = jnp.stack(outs, axis=2).astype(u.dtype)
    y = jnp.einsum('bsgc,gcd->bsgd', d, w_pool).reshape(B, S, POOL_WIDTH)
    return y * pool_scale


def apply_partial_rope(x, cos, sin):
    half = ROT_DIM // 2
    x1, x2 = x[..., :half], x[..., half:ROT_DIM]
    rot = jnp.concatenate([x1 * cos - x2 * sin, x2 * cos + x1 * sin], axis=-1)
    return jnp.concatenate([rot, x[..., ROT_DIM:]], axis=-1)


def swa_with_sinks(q, k, v, sinks):
    B, S = q.shape[0], q.shape[1]
    nb = S // BLOCK
    G = N_HEADS // N_KV_HEADS
    qb = q.reshape(B, nb, BLOCK, N_KV_HEADS, G, HEAD_DIM)
    pad = ((0, 0), (BLOCK, 0), (0, 0), (0, 0))
    kp = jnp.pad(k, pad).reshape(B, nb + 1, BLOCK, N_KV_HEADS, HEAD_DIM)
    vp = jnp.pad(v, pad).reshape(B, nb + 1, BLOCK, N_KV_HEADS, HEAD_DIM)
    kb = jnp.concatenate([kp[:, :-1], kp[:, 1:]], axis=2)
    vb = jnp.concatenate([vp[:, :-1], vp[:, 1:]], axis=2)
    s = jnp.einsum('bnqkgd,bnjkd->bnkgqj', qb, kb,
                   preferred_element_type=jnp.float32) * (HEAD_DIM ** -0.5)
    qi = jnp.arange(BLOCK)[:, None]
    kj = jnp.arange(2 * BLOCK)[None, :]
    diff = qi + BLOCK - kj
    band = (diff >= 0) & (diff < WINDOW)
    key_abs = jnp.arange(nb)[:, None] * BLOCK - BLOCK + kj
    valid = band[None] & (key_abs >= 0)[:, None, :]
    s = jnp.where(valid[None, :, None, None], s, NEG_INF)
    sink = jnp.broadcast_to(sinks.astype(jnp.float32).reshape(1, 1, N_KV_HEADS, G, 1, 1),
                            s.shape[:-1] + (1,))
    p = jax.nn.softmax(jnp.concatenate([s, sink], axis=-1), axis=-1)[..., :-1]
    o = jnp.einsum('bnkgqj,bnjkd->bnqkgd', p.astype(v.dtype), vb)
    return o.reshape(B, S, ATTN_WIDTH)


def token_mixer(h, cos, sin, w_in, w_pool, pool_scale, sinks, g_pool, g_attn, w_out):
    B, S, _ = h.shape
    z = h @ w_in
    u = z[..., :POOL_WIDTH]
    q = z[..., POOL_WIDTH:MIX_WIDTH].reshape(B, S, N_HEADS, HEAD_DIM)
    k = z[..., MIX_WIDTH:MIX_WIDTH + KV_WIDTH].reshape(B, S, N_KV_HEADS, HEAD_DIM)
    v = z[..., MIX_WIDTH + KV_WIDTH:].reshape(B, S, N_KV_HEADS, HEAD_DIM)
    pool_out = pool_mix(u, w_pool, pool_scale)
    q = apply_partial_rope(q, cos, sin)
    k = apply_partial_rope(k, cos, sin)
    attn_out = swa_with_sinks(q, k, v, sinks)
    y = jnp.concatenate([rms_norm(pool_out, g_pool), rms_norm(attn_out, g_attn)], axis=-1)
    return y @ w_out


def _fwd_setup_inputs(seed: int = 0) -> dict:
    key = jax.random.key(seed)
    ks = jax.random.split(key, 24)
    f32 = jnp.float32

    def w(k, shape, fan_in):
        return jax.random.normal(k, shape, f32) * fan_in ** -0.5

    def gain(k, n):
        return 1.0 + 0.05 * jax.random.normal(k, (DEPTH, n), f32)

    x = jax.random.normal(ks[0], (BATCH, SEQ, D_MODEL), f32)
    offset = jax.random.randint(ks[1], (BATCH, 1), 0, 4096, jnp.int32)
    positions = offset + jnp.arange(SEQ, dtype=jnp.int32)[None, :]
    return {
        "x": x,
        "positions": positions,
        "ffn1_pre": gain(ks[2], D_MODEL),
        "ffn1_w_gu": w(ks[3], (DEPTH, D_MODEL, 2 * D_FF), D_MODEL),
        "ffn1_w_down": w(ks[4], (DEPTH, D_FF, D_MODEL), D_FF),
        "ffn1_post": gain(ks[5], D_MODEL),
        "mix_pre": gain(ks[6], D_MODEL),
        "w_in": w(ks[7], (DEPTH, D_MODEL, IN_WIDTH), D_MODEL),
        "w_pool": w(ks[8], (DEPTH, len(POOL_WINDOWS), POOL_GROUP, POOL_GROUP), POOL_GROUP),
        "pool_scale": 0.5 + 0.05 * jax.random.normal(ks[9], (DEPTH, POOL_WIDTH), f32),
        "sinks": 0.5 * jax.random.normal(ks[10], (DEPTH, N_HEADS), f32),
        "g_pool": gain(ks[11], POOL_WIDTH),
        "g_attn": gain(ks[12], ATTN_WIDTH),
        "w_out": w(ks[13], (DEPTH, MIX_WIDTH, D_MODEL), MIX_WIDTH),
        "mix_post": gain(ks[14], D_MODEL),
        "ffn2_pre": gain(ks[15], D_MODEL),
        "ffn2_w_gu": w(ks[16], (DEPTH, D_MODEL, 2 * D_FF), D_MODEL),
        "ffn2_w_down": w(ks[17], (DEPTH, D_FF, D_MODEL), D_FF),
        "ffn2_post": gain(ks[18], D_MODEL),
    }


def _fwd_reference(x, positions, ffn1_pre, ffn1_w_gu, ffn1_w_down, ffn1_post,
              mix_pre, w_in, w_pool, pool_scale, sinks, g_pool, g_attn, w_out, mix_post,
              ffn2_pre, ffn2_w_gu, ffn2_w_down, ffn2_post):
    inv_freq = ROPE_THETA ** (-jnp.arange(0, ROT_DIM, 2, dtype=jnp.float32) / ROT_DIM)
    ang = positions.astype(jnp.float32)[..., None] * inv_freq
    cos = jnp.cos(ang)[:, :, None, :].astype(x.dtype)
    sin = jnp.sin(ang)[:, :, None, :].astype(x.dtype)
    for l in range(DEPTH):
        h = swiglu(rms_norm(x, ffn1_pre[l]), ffn1_w_gu[l], ffn1_w_down[l])
        x = x + 0.5 * rms_norm(h, ffn1_post[l])
        h = token_mixer(rms_norm(x, mix_pre[l]), cos, sin, w_in[l], w_pool[l], pool_scale[l],
                        sinks[l], g_pool[l], g_attn[l], w_out[l])
        x = x + rms_norm(h, mix_post[l])
        h = swiglu(rms_norm(x, ffn2_pre[l]), ffn2_w_gu[l], ffn2_w_down[l])
        x = x + 0.5 * rms_norm(h, ffn2_post[l])
    return x


import jax as _jax
import jax.numpy as _jnp

TWIN_FORMAT = 'train_step'
FWD_PARAMS = ['x', 'positions', 'ffn1_pre', 'ffn1_w_gu', 'ffn1_w_down', 'ffn1_post', 'mix_pre', 'w_in', 'w_pool', 'pool_scale', 'sinks', 'g_pool', 'g_attn', 'w_out', 'mix_post', 'ffn2_pre', 'ffn2_w_gu', 'ffn2_w_down', 'ffn2_post']
TWIN_WEIGHTS = ['ffn1_pre', 'ffn1_w_gu', 'ffn1_w_down', 'ffn1_post', 'mix_pre', 'w_in', 'w_pool', 'pool_scale', 'sinks', 'g_pool', 'g_attn', 'w_out', 'mix_post', 'ffn2_pre', 'ffn2_w_gu', 'ffn2_w_down', 'ffn2_post']
TWIN_DIFF_INPUT = 'x'
TWIN_INPUTS = ['x', 'positions', 'ffn1_pre', 'ffn1_w_gu', 'ffn1_w_down', 'ffn1_post', 'mix_pre', 'w_in', 'w_pool', 'pool_scale', 'sinks', 'g_pool', 'g_attn', 'w_out', 'mix_post', 'ffn2_pre', 'ffn2_w_gu', 'ffn2_w_down', 'ffn2_post', 'loss_target', 'm_ffn1_pre', 'm_ffn1_w_gu', 'm_ffn1_w_down', 'm_ffn1_post', 'm_mix_pre', 'm_w_in', 'm_w_pool', 'm_pool_scale', 'm_sinks', 'm_g_pool', 'm_g_attn', 'm_w_out', 'm_mix_post', 'm_ffn2_pre', 'm_ffn2_w_gu', 'm_ffn2_w_down', 'm_ffn2_post', 'v_ffn1_pre', 'v_ffn1_w_gu', 'v_ffn1_w_down', 'v_ffn1_post', 'v_mix_pre', 'v_w_in', 'v_w_pool', 'v_pool_scale', 'v_sinks', 'v_g_pool', 'v_g_attn', 'v_w_out', 'v_mix_post', 'v_ffn2_pre', 'v_ffn2_w_gu', 'v_ffn2_w_down', 'v_ffn2_post']
TWIN_OUTPUTS = ['loss', 'grad_x', 'grad_ffn1_pre', 'grad_ffn1_w_gu', 'grad_ffn1_w_down', 'grad_ffn1_post', 'grad_mix_pre', 'grad_w_in', 'grad_w_pool', 'grad_pool_scale', 'grad_sinks', 'grad_g_pool', 'grad_g_attn', 'grad_w_out', 'grad_mix_post', 'grad_ffn2_pre', 'grad_ffn2_w_gu', 'grad_ffn2_w_down', 'grad_ffn2_post', 'delta_ffn1_pre', 'delta_ffn1_w_gu', 'delta_ffn1_w_down', 'delta_ffn1_post', 'delta_mix_pre', 'delta_w_in', 'delta_w_pool', 'delta_pool_scale', 'delta_sinks', 'delta_g_pool', 'delta_g_attn', 'delta_w_out', 'delta_mix_post', 'delta_ffn2_pre', 'delta_ffn2_w_gu', 'delta_ffn2_w_down', 'delta_ffn2_post', 'new_m_ffn1_pre', 'new_m_ffn1_w_gu', 'new_m_ffn1_w_down', 'new_m_ffn1_post', 'new_m_mix_pre', 'new_m_w_in', 'new_m_w_pool', 'new_m_pool_scale', 'new_m_sinks', 'new_m_g_pool', 'new_m_g_attn', 'new_m_w_out', 'new_m_mix_post', 'new_m_ffn2_pre', 'new_m_ffn2_w_gu', 'new_m_ffn2_w_down', 'new_m_ffn2_post', 'new_v_ffn1_pre', 'new_v_ffn1_w_gu', 'new_v_ffn1_w_down', 'new_v_ffn1_post', 'new_v_mix_pre', 'new_v_w_in', 'new_v_w_pool', 'new_v_pool_scale', 'new_v_sinks', 'new_v_g_pool', 'new_v_g_attn', 'new_v_w_out', 'new_v_mix_post', 'new_v_ffn2_pre', 'new_v_ffn2_w_gu', 'new_v_ffn2_w_down', 'new_v_ffn2_post']
TWIN_LEAF_KINDS = {'loss': 'loss', 'grad_x': 'grad_x', 'grad_ffn1_pre': 'grad_w', 'grad_ffn1_w_gu': 'grad_w', 'grad_ffn1_w_down': 'grad_w', 'grad_ffn1_post': 'grad_w', 'grad_mix_pre': 'grad_w', 'grad_w_in': 'grad_w', 'grad_w_pool': 'grad_w', 'grad_pool_scale': 'grad_w', 'grad_sinks': 'grad_w', 'grad_g_pool': 'grad_w', 'grad_g_attn': 'grad_w', 'grad_w_out': 'grad_w', 'grad_mix_post': 'grad_w', 'grad_ffn2_pre': 'grad_w', 'grad_ffn2_w_gu': 'grad_w', 'grad_ffn2_w_down': 'grad_w', 'grad_ffn2_post': 'grad_w', 'delta_ffn1_pre': 'delta_w', 'delta_ffn1_w_gu': 'delta_w', 'delta_ffn1_w_down': 'delta_w', 'delta_ffn1_post': 'delta_w', 'delta_mix_pre': 'delta_w', 'delta_w_in': 'delta_w', 'delta_w_pool': 'delta_w', 'delta_pool_scale': 'delta_w', 'delta_sinks': 'delta_w', 'delta_g_pool': 'delta_w', 'delta_g_attn': 'delta_w', 'delta_w_out': 'delta_w', 'delta_mix_post': 'delta_w', 'delta_ffn2_pre': 'delta_w', 'delta_ffn2_w_gu': 'delta_w', 'delta_ffn2_w_down': 'delta_w', 'delta_ffn2_post': 'delta_w', 'new_m_ffn1_pre': 'new_m', 'new_m_ffn1_w_gu': 'new_m', 'new_m_ffn1_w_down': 'new_m', 'new_m_ffn1_post': 'new_m', 'new_m_mix_pre': 'new_m', 'new_m_w_in': 'new_m', 'new_m_w_pool': 'new_m', 'new_m_pool_scale': 'new_m', 'new_m_sinks': 'new_m', 'new_m_g_pool': 'new_m', 'new_m_g_attn': 'new_m', 'new_m_w_out': 'new_m', 'new_m_mix_post': 'new_m', 'new_m_ffn2_pre': 'new_m', 'new_m_ffn2_w_gu': 'new_m', 'new_m_ffn2_w_down': 'new_m', 'new_m_ffn2_post': 'new_m', 'new_v_ffn1_pre': 'new_v', 'new_v_ffn1_w_gu': 'new_v', 'new_v_ffn1_w_down': 'new_v', 'new_v_ffn1_post': 'new_v', 'new_v_mix_pre': 'new_v', 'new_v_w_in': 'new_v', 'new_v_w_pool': 'new_v', 'new_v_pool_scale': 'new_v', 'new_v_sinks': 'new_v', 'new_v_g_pool': 'new_v', 'new_v_g_attn': 'new_v', 'new_v_w_out': 'new_v', 'new_v_mix_post': 'new_v', 'new_v_ffn2_pre': 'new_v', 'new_v_ffn2_w_gu': 'new_v', 'new_v_ffn2_w_down': 'new_v', 'new_v_ffn2_post': 'new_v'}


def _forward(args):
    return _fwd_reference(*[args[k] for k in FWD_PARAMS])


def _output_shape():
    def fwd():
        inp = _fwd_setup_inputs(0)
        return _fwd_reference(*[inp[k] for k in FWD_PARAMS])
    out = _jax.eval_shape(fwd)
    return out.shape, out.dtype

N_MICROBATCH = 1
ADAM_LR = 0.001
ADAM_B1 = 0.9
ADAM_B2 = 0.999
ADAM_EPS = 1e-08
ADAM_WD = 0.01
ADAM_STEP = 10
PER_EXAMPLE_BATCH_AXIS = {'x': 0, 'positions': 0, 'loss_target': 0}
SHARED_INPUTS = []
_WEIGHT_DTYPES = {'ffn1_pre': _jnp.float32, 'ffn1_w_gu': _jnp.float32, 'ffn1_w_down': _jnp.float32, 'ffn1_post': _jnp.float32, 'mix_pre': _jnp.float32, 'w_in': _jnp.float32, 'w_pool': _jnp.float32, 'pool_scale': _jnp.float32, 'sinks': _jnp.float32, 'g_pool': _jnp.float32, 'g_attn': _jnp.float32, 'w_out': _jnp.float32, 'mix_post': _jnp.float32, 'ffn2_pre': _jnp.float32, 'ffn2_w_gu': _jnp.float32, 'ffn2_w_down': _jnp.float32, 'ffn2_post': _jnp.float32}
MOMENT_SCALE = {'ffn1_pre': 5.788378e-01, 'ffn1_w_gu': 2.595034e-01, 'ffn1_w_down': 4.863686e-01, 'ffn1_post': 1.586002e+01, 'mix_pre': 1.233245e+00, 'w_in': 1.040314e+00, 'w_pool': 1.625112e+00, 'pool_scale': 3.057612e+00, 'sinks': 2.521996e-01, 'g_pool': 1.528054e+00, 'g_attn': 6.159831e-01, 'w_out': 1.156519e+00, 'mix_post': 6.401002e+01, 'ffn2_pre': 5.529143e-01, 'ffn2_w_gu': 2.230614e-01, 'ffn2_w_down': 4.849902e-01, 'ffn2_post': 1.589072e+01}


def _to_microbatches(a, axis):
    t = _jnp.moveaxis(a, axis, 0)
    t = t.reshape((N_MICROBATCH, t.shape[0] // N_MICROBATCH) + t.shape[1:])
    return _jnp.moveaxis(t, 1, axis + 1)


def setup_inputs(seed: int = 0) -> dict:
    inp = _fwd_setup_inputs(seed)
    key = _jax.random.fold_in(_jax.random.key(seed), 7919)
    shape, _ = _output_shape()
    out = dict(inp)
    out["loss_target"] = _jax.random.normal(_jax.random.fold_in(key, 0), shape, _jnp.float32)
    for i, name in enumerate(TWIN_WEIGHTS):
        w = inp[name].astype(_jnp.float32)
        if MOMENT_SCALE is None:
            s = _jnp.sqrt(_jnp.mean(_jnp.square(w)) + 1e-30)
        else:
            s = MOMENT_SCALE[name]
        km, kv = _jax.random.split(_jax.random.fold_in(key, i + 1))
        out[name] = w
        out["m_" + name] = s * _jax.random.normal(km, w.shape, _jnp.float32)
        out["v_" + name] = (s * s) * _jax.random.uniform(kv, w.shape, _jnp.float32, 0.5, 1.5)
    if N_MICROBATCH > 1:
        for name, axis in PER_EXAMPLE_BATCH_AXIS.items():
            out[name] = _to_microbatches(out[name], axis)
    return {'x': out['x'], 'positions': out['positions'], 'ffn1_pre': out['ffn1_pre'], 'ffn1_w_gu': out['ffn1_w_gu'], 'ffn1_w_down': out['ffn1_w_down'], 'ffn1_post': out['ffn1_post'], 'mix_pre': out['mix_pre'], 'w_in': out['w_in'], 'w_pool': out['w_pool'], 'pool_scale': out['pool_scale'], 'sinks': out['sinks'], 'g_pool': out['g_pool'], 'g_attn': out['g_attn'], 'w_out': out['w_out'], 'mix_post': out['mix_post'], 'ffn2_pre': out['ffn2_pre'], 'ffn2_w_gu': out['ffn2_w_gu'], 'ffn2_w_down': out['ffn2_w_down'], 'ffn2_post': out['ffn2_post'], 'loss_target': out['loss_target'], 'm_ffn1_pre': out['m_ffn1_pre'], 'm_ffn1_w_gu': out['m_ffn1_w_gu'], 'm_ffn1_w_down': out['m_ffn1_w_down'], 'm_ffn1_post': out['m_ffn1_post'], 'm_mix_pre': out['m_mix_pre'], 'm_w_in': out['m_w_in'], 'm_w_pool': out['m_w_pool'], 'm_pool_scale': out['m_pool_scale'], 'm_sinks': out['m_sinks'], 'm_g_pool': out['m_g_pool'], 'm_g_attn': out['m_g_attn'], 'm_w_out': out['m_w_out'], 'm_mix_post': out['m_mix_post'], 'm_ffn2_pre': out['m_ffn2_pre'], 'm_ffn2_w_gu': out['m_ffn2_w_gu'], 'm_ffn2_w_down': out['m_ffn2_w_down'], 'm_ffn2_post': out['m_ffn2_post'], 'v_ffn1_pre': out['v_ffn1_pre'], 'v_ffn1_w_gu': out['v_ffn1_w_gu'], 'v_ffn1_w_down': out['v_ffn1_w_down'], 'v_ffn1_post': out['v_ffn1_post'], 'v_mix_pre': out['v_mix_pre'], 'v_w_in': out['v_w_in'], 'v_w_pool': out['v_w_pool'], 'v_pool_scale': out['v_pool_scale'], 'v_sinks': out['v_sinks'], 'v_g_pool': out['v_g_pool'], 'v_g_attn': out['v_g_attn'], 'v_w_out': out['v_w_out'], 'v_mix_post': out['v_mix_post'], 'v_ffn2_pre': out['v_ffn2_pre'], 'v_ffn2_w_gu': out['v_ffn2_w_gu'], 'v_ffn2_w_down': out['v_ffn2_w_down'], 'v_ffn2_post': out['v_ffn2_post']}


def _loss(weights, diff, rest, loss_target):
    with _jax.named_scope("forward"):
        args = {**rest, TWIN_DIFF_INPUT: diff, **{k: w.astype(_WEIGHT_DTYPES[k]) for k, w in weights.items()}}
        y = _forward(args)
    with _jax.named_scope("loss_head"):
        err = _jnp.square(y.astype(_jnp.float32) - loss_target)
        return 0.5 * _jnp.sum(_jnp.mean(err, axis=-1)) if err.ndim else 0.5 * err


def _adamw(w, g, m, v):
    m = ADAM_B1 * m + (1.0 - ADAM_B1) * g
    v = ADAM_B2 * v + (1.0 - ADAM_B2) * _jnp.square(g)
    m_hat = m / (1.0 - ADAM_B1 ** ADAM_STEP)
    v_hat = v / (1.0 - ADAM_B2 ** ADAM_STEP)
    delta = -ADAM_LR * (m_hat / (_jnp.sqrt(v_hat) + ADAM_EPS) + ADAM_WD * w)
    return delta, m, v


def reference(x, positions, ffn1_pre, ffn1_w_gu, ffn1_w_down, ffn1_post, mix_pre, w_in, w_pool, pool_scale, sinks, g_pool, g_attn, w_out, mix_post, ffn2_pre, ffn2_w_gu, ffn2_w_down, ffn2_post, loss_target, m_ffn1_pre, m_ffn1_w_gu, m_ffn1_w_down, m_ffn1_post, m_mix_pre, m_w_in, m_w_pool, m_pool_scale, m_sinks, m_g_pool, m_g_attn, m_w_out, m_mix_post, m_ffn2_pre, m_ffn2_w_gu, m_ffn2_w_down, m_ffn2_post, v_ffn1_pre, v_ffn1_w_gu, v_ffn1_w_down, v_ffn1_post, v_mix_pre, v_w_in, v_w_pool, v_pool_scale, v_sinks, v_g_pool, v_g_attn, v_w_out, v_mix_post, v_ffn2_pre, v_ffn2_w_gu, v_ffn2_w_down, v_ffn2_post):
    given = dict(x=x, positions=positions, ffn1_pre=ffn1_pre, ffn1_w_gu=ffn1_w_gu, ffn1_w_down=ffn1_w_down, ffn1_post=ffn1_post, mix_pre=mix_pre, w_in=w_in, w_pool=w_pool, pool_scale=pool_scale, sinks=sinks, g_pool=g_pool, g_attn=g_attn, w_out=w_out, mix_post=mix_post, ffn2_pre=ffn2_pre, ffn2_w_gu=ffn2_w_gu, ffn2_w_down=ffn2_w_down, ffn2_post=ffn2_post, loss_target=loss_target, m_ffn1_pre=m_ffn1_pre, m_ffn1_w_gu=m_ffn1_w_gu, m_ffn1_w_down=m_ffn1_w_down, m_ffn1_post=m_ffn1_post, m_mix_pre=m_mix_pre, m_w_in=m_w_in, m_w_pool=m_w_pool, m_pool_scale=m_pool_scale, m_sinks=m_sinks, m_g_pool=m_g_pool, m_g_attn=m_g_attn, m_w_out=m_w_out, m_mix_post=m_mix_post, m_ffn2_pre=m_ffn2_pre, m_ffn2_w_gu=m_ffn2_w_gu, m_ffn2_w_down=m_ffn2_w_down, m_ffn2_post=m_ffn2_post, v_ffn1_pre=v_ffn1_pre, v_ffn1_w_gu=v_ffn1_w_gu, v_ffn1_w_down=v_ffn1_w_down, v_ffn1_post=v_ffn1_post, v_mix_pre=v_mix_pre, v_w_in=v_w_in, v_w_pool=v_w_pool, v_pool_scale=v_pool_scale, v_sinks=v_sinks, v_g_pool=v_g_pool, v_g_attn=v_g_attn, v_w_out=v_w_out, v_mix_post=v_mix_post, v_ffn2_pre=v_ffn2_pre, v_ffn2_w_gu=v_ffn2_w_gu, v_ffn2_w_down=v_ffn2_w_down, v_ffn2_post=v_ffn2_post)
    weights = {n: given[n] for n in TWIN_WEIGHTS}
    shared = {n: given[n] for n in SHARED_INPUTS}
    per_example = {n: given[n] for n in ['x', 'positions']}
    grad_fn = _jax.value_and_grad(_loss, argnums=(0, 1))

    def one_microbatch(ex, loss_target):
        ex = dict(ex)
        diff = ex.pop(TWIN_DIFF_INPUT)
        return grad_fn(weights, diff, {**shared, **ex}, loss_target)

    if N_MICROBATCH == 1:
        loss, (grad_w, grad_x) = one_microbatch(per_example, given["loss_target"])
    else:
        def body(carry, xs):
            loss_sum, grad_sum = carry
            l_k, (gw_k, gx_k) = one_microbatch(xs[0], xs[1])
            with _jax.named_scope("update"):
                return (loss_sum + l_k, _jax.tree.map(_jnp.add, grad_sum, gw_k)), gx_k

        init = (_jnp.zeros((), _jnp.float32), _jax.tree.map(_jnp.zeros_like, weights))
        (loss, grad_w), grad_x = _jax.lax.scan(body, init, (per_example, given["loss_target"]))
    with _jax.named_scope("update"):
        delta_w, new_m, new_v = {}, {}, {}
        for n in TWIN_WEIGHTS:
            delta_w[n], new_m[n], new_v[n] = _adamw(weights[n], grad_w[n], given["m_" + n], given["v_" + n])
    return (loss, grad_x, *[grad_w[n] for n in TWIN_WEIGHTS], *[delta_w[n] for n in TWIN_WEIGHTS],
            *[new_m[n] for n in TWIN_WEIGHTS], *[new_v[n] for n in TWIN_WEIGHTS])
```

```python
import functools

import jax
import jax.numpy as jnp
from jax import lax
from jax.experimental import pallas as pl
from jax.experimental.pallas import tpu as pltpu

F32 = jnp.float32
BF16 = jnp.bfloat16

D_MODEL = 1024
D_FF = 2816
N_SHARD = 4
FF_CHUNK = D_FF // 2
POOL_WINDOWS = (2, 4, 8, 16)
POOL_WIDTH = 512
POOL_GROUP = 128
HALO = 16
HEAD_DIM = 64
N_HEADS = 8
ATTN_WIDTH = 512
KV_WIDTH = 128
IN_WIDTH = 1280
BLOCK = 128
ROT_DIM = 16
ROPE_THETA = 500000.0
EPS = 1e-6
NEG_INF = -1e30
ATTN_SCALE = HEAD_DIM ** -0.5

ADAM_LR = 0.001
ADAM_B1 = 0.9
ADAM_B2 = 0.999
ADAM_EPS = 1e-08
ADAM_WD = 0.01
ADAM_STEP = 10

VMEM_LIMIT = 60 * 1024 * 1024
TOKEN_TILE = 256

MESH = pl.DeviceIdType.MESH


def _params(**kw):
    return pltpu.CompilerParams(vmem_limit_bytes=VMEM_LIMIT, **kw)


def _dot(a, b):
    return jnp.dot(a, b, preferred_element_type=F32)


def _dot_nt(a, b):
    return lax.dot_general(a, b, (((1,), (1,)), ((), ())), preferred_element_type=F32)


def _dot_tn(a, b):
    return lax.dot_general(a, b, (((0,), (0,)), ((), ())), preferred_element_type=F32)


def _rstd(x):
    return lax.rsqrt(jnp.mean(x * x, axis=-1, keepdims=True) + EPS)


def _norm_bwd(dy, xn, r, gain):
    dxn = dy * gain
    return r * (dxn - xn * jnp.mean(dxn * xn, axis=-1, keepdims=True))


def _sigmoid(x):
    return 1.0 / (1.0 + jnp.exp(-x))


def _full(shape):
    return pl.BlockSpec(shape, lambda *_: (0,) * len(shape))


def _rows(tile, width, col=0):
    return pl.BlockSpec((tile, width), lambda i: (i, col))


_ANY = pl.BlockSpec(memory_space=pl.ANY)


def _load_once(pairs, sem):
    @pl.when(pl.program_id(0) == 0)
    def _():
        copies = [pltpu.make_async_copy(src, dst, sem.at[n]) for n, (src, dst) in enumerate(pairs)]
        for cp in copies:
            cp.start()
        for cp in copies:
            cp.wait()


def _ffn_fwd(x, pre, post, wgu, wd, target=None):
    S = x.shape[0]
    tm = TOKEN_TILE
    with_loss = target is not None

    def body(*refs):
        if with_loss:
            (x_ref, pre_ref, post_ref, wgu_hbm, wd_hbm, tgt_ref,
             out_ref, g_ref, u_ref, f_ref, loss_ref, wgu_v, wd_v, sem) = refs
        else:
            (x_ref, pre_ref, post_ref, wgu_hbm, wd_hbm,
             out_ref, g_ref, u_ref, f_ref, wgu_v, wd_v, sem) = refs
        _load_once([(wgu_hbm, wgu_v), (wd_hbm, wd_v)], sem)
        xv = x_ref[...]
        h = ((xv * _rstd(xv)) * pre_ref[...]).astype(BF16)
        facc = jnp.zeros((tm, D_MODEL), F32)
        for c in range(2):
            cols = slice(c * FF_CHUNK, (c + 1) * FF_CHUNK)
            g = _dot(h, wgu_v[c])
            u = _dot(h, wgu_v[2 + c])
            g_ref[:, cols] = g.astype(BF16)
            u_ref[:, cols] = u.astype(BF16)
            a = (g * _sigmoid(g)) * u
            facc = facc + _dot(a.astype(BF16), wd_v[cols, :])
        f_ref[...] = facc
        out = xv + 0.5 * ((facc * _rstd(facc)) * post_ref[...])
        if with_loss:
            diff = out - tgt_ref[...]
            out_ref[...] = diff * (1.0 / D_MODEL)

            @pl.when(pl.program_id(0) == 0)
            def _():
                loss_ref[...] = jnp.zeros_like(loss_ref)
            loss_ref[...] += jnp.sum(diff * diff)
        else:
            out_ref[...] = out

    in_specs = [_rows(tm, D_MODEL), _full((1, D_MODEL)), _full((1, D_MODEL)), _ANY, _ANY]
    args = [x, pre, post, wgu, wd]
    out_shape = [jax.ShapeDtypeStruct((S, D_MODEL), F32), jax.ShapeDtypeStruct((S, D_FF), BF16),
                 jax.ShapeDtypeStruct((S, D_FF), BF16), jax.ShapeDtypeStruct((S, D_MODEL), F32)]
    out_specs = [_rows(tm, D_MODEL), _rows(tm, D_FF), _rows(tm, D_FF), _rows(tm, D_MODEL)]
    if with_loss:
        in_specs.append(_rows(tm, D_MODEL))
        args.append(target)
        out_shape.append(jax.ShapeDtypeStruct((8, 128), F32))
        out_specs.append(_full((8, 128)))
    return pl.pallas_call(
        body, name="ffn_fwd_loss" if with_loss else "ffn_fwd",
        grid=(S // tm,), in_specs=in_specs, out_specs=out_specs, out_shape=out_shape,
        scratch_shapes=[pltpu.VMEM((N_SHARD, D_MODEL, FF_CHUNK), BF16), pltpu.VMEM((D_FF, D_MODEL), BF16),
                        pltpu.SemaphoreType.DMA((2,))],
        compiler_params=_params(dimension_semantics=("arbitrary",)),
    )(*args)


def _ffn_bwd(dout, x, f, g, u, pre, post, wgu, wd):
    S = x.shape[0]
    tm = TOKEN_TILE

    def body(dout_ref, x_ref, f_ref, g_ref, u_ref, pre_ref, post_ref, wgu_hbm, wd_hbm,
             dx_ref, h_ref, dgu_ref, a_ref, df_ref, dpre_ref, dpost_ref, wgu_v, wd_v, sem):
        _load_once([(wgu_hbm, wgu_v), (wd_hbm, wd_v)], sem)

        @pl.when(pl.program_id(0) == 0)
        def _():
            dpre_ref[...] = jnp.zeros_like(dpre_ref)
            dpost_ref[...] = jnp.zeros_like(dpost_ref)

        dout_v = dout_ref[...]
        dn = 0.5 * dout_v
        fv = f_ref[...]
        rf = _rstd(fv)
        fn = fv * rf
        dpost_ref[...] += jnp.sum(dn * fn, axis=0, keepdims=True)
        df = _norm_bwd(dn, fn, rf, post_ref[...]).astype(BF16)
        df_ref[...] = df
        dh = jnp.zeros((tm, D_MODEL), F32)
        for c in range(2):
            cols = slice(c * FF_CHUNK, (c + 1) * FF_CHUNK)
            da = _dot_nt(df, wd_v[cols, :])
            gv = g_ref[:, cols].astype(F32)
            uv = u_ref[:, cols].astype(F32)
            sg = _sigmoid(gv)
            silu = gv * sg
            a_ref[:, cols] = (silu * uv).astype(BF16)
            dg = ((da * uv) * (sg * (1.0 + gv * (1.0 - sg)))).astype(BF16)
            du = (da * silu).astype(BF16)
            dgu_ref[:, cols] = dg
            dgu_ref[:, 2 * FF_CHUNK + c * FF_CHUNK:2 * FF_CHUNK + (c + 1) * FF_CHUNK] = du
            dh = dh + _dot_nt(dg, wgu_v[c]) + _dot_nt(du, wgu_v[2 + c])
        xv = x_ref[...]
        rx = _rstd(xv)
        xn = xv * rx
        h_ref[...] = (xn * pre_ref[...]).astype(BF16)
        dpre_ref[...] += jnp.sum(dh * xn, axis=0, keepdims=True)
        dx_ref[...] = dout_v + _norm_bwd(dh, xn, rx, pre_ref[...])

    return pl.pallas_call(
        body, name="ffn_bwd", grid=(S // tm,),
        in_specs=[_rows(tm, D_MODEL), _rows(tm, D_MODEL), _rows(tm, D_MODEL), _rows(tm, D_FF), _rows(tm, D_FF),
                  _full((1, D_MODEL)), _full((1, D_MODEL)), _ANY, _ANY],
        out_specs=[_rows(tm, D_MODEL), _rows(tm, D_MODEL), _rows(tm, 2 * D_FF), _rows(tm, D_FF), _rows(tm, D_MODEL),
                   _full((1, D_MODEL)), _full((1, D_MODEL))],
        out_shape=[jax.ShapeDtypeStruct((S, D_MODEL), F32), jax.ShapeDtypeStruct((S, D_MODEL), BF16),
                   jax.ShapeDtypeStruct((S, 2 * D_FF), BF16), jax.ShapeDtypeStruct((S, D_FF), BF16),
                   jax.ShapeDtypeStruct((S, D_MODEL), BF16),
                   jax.ShapeDtypeStruct((1, D_MODEL), F32), jax.ShapeDtypeStruct((1, D_MODEL), F32)],
        scratch_shapes=[pltpu.VMEM((N_SHARD, D_MODEL, FF_CHUNK), BF16), pltpu.VMEM((D_FF, D_MODEL), BF16),
                        pltpu.SemaphoreType.DMA((2,))],
        compiler_params=_params(dimension_semantics=("arbitrary",)),
    )(dout, x, f, g, u, pre, post, wgu, wd)


def _wgrad(lhs, rhs, m_block, n_block, name, column_shards=False, tk=512):
    S, M = lhs.shape
    N = rhs.shape[1]
    k_steps = S // tk

    def body(lhs_ref, rhs_ref, out_ref, out16_ref):
        k = pl.program_id(2)

        @pl.when(k == 0)
        def _():
            out_ref[...] = jnp.zeros_like(out_ref)
        out_ref[...] += _dot_tn(lhs_ref[...], rhs_ref[...])

        @pl.when(k == k_steps - 1)
        def _():
            out16_ref[...] = out_ref[...].astype(BF16)

    if column_shards:
        assert N == N_SHARD * n_block
        shape = (N_SHARD, M, n_block)
        out_spec = pl.BlockSpec((None, m_block, n_block), lambda i, j, k: (j, i, 0))
    else:
        shape = (M, N)
        out_spec = pl.BlockSpec((m_block, n_block), lambda i, j, k: (i, j))
    out, out16 = pl.pallas_call(
        body, name=name, grid=(M // m_block, N // n_block, k_steps),
        in_specs=[pl.BlockSpec((tk, m_block), lambda i, j, k: (k, i)),
                  pl.BlockSpec((tk, n_block), lambda i, j, k: (k, j))],
        out_specs=[out_spec, out_spec],
        out_shape=[jax.ShapeDtypeStruct(shape, F32), jax.ShapeDtypeStruct(shape, BF16)],
        compiler_params=_params(dimension_semantics=("arbitrary", "arbitrary", "arbitrary")),
    )(lhs, rhs)
    if not column_shards:
        out = out.reshape(N_SHARD, M // N_SHARD, N)
        out16 = out16.reshape(N_SHARD, M // N_SHARD, N)
    return out, out16


def _rope_tables(pos_ref, invf_ref):
    ang = pos_ref[...].astype(F32) * invf_ref[...]
    cos, sin = jnp.cos(ang), jnp.sin(ang)
    lane = lax.broadcasted_iota(jnp.int32, ang.shape, 1) % HEAD_DIM
    first = lane < ROT_DIM // 2
    second = (lane >= ROT_DIM // 2) & (lane < ROT_DIM)
    c = jnp.where(lane < ROT_DIM, cos, 1.0)
    s_first = jnp.where(first, sin, 0.0)
    s_second = jnp.where(second, sin, 0.0)
    return c, s_first, s_second


_HALF = ROT_DIM // 2
_LANES = 128


def _rope(t, tables):
    c, s_first, s_second = tables
    return t * c - pltpu.roll(t, _LANES - _HALF, axis=1) * s_first + pltpu.roll(t, _HALF, axis=1) * s_second


def _rope_transposed(t, tables):
    c, s_first, s_second = tables
    return t * c - pltpu.roll(t * s_first, _HALF, axis=1) + pltpu.roll(t * s_second, _LANES - _HALF, axis=1)


def _mixer_in_fwd(x, pre, w_in_t, pos, invf):
    S = x.shape[0]
    tm = TOKEN_TILE

    def body(x_ref, pre_ref, w_hbm, pos_ref, invf_ref, u_ref, q_ref, k_ref, v_ref, w_v, sem):
        _load_once([(w_hbm, w_v)], sem)
        xv = x_ref[...]
        h = ((xv * _rstd(xv)) * pre_ref[...]).astype(BF16)
        z = _dot_nt(h, w_v[...])
        tables = _rope_tables(pos_ref, invf_ref)
        u_ref[...] = z[:, :POOL_WIDTH]
        for t in range(ATTN_WIDTH // _LANES):
            lo = POOL_WIDTH + t * _LANES
            q_ref[:, t * _LANES:(t + 1) * _LANES] = _rope(z[:, lo:lo + _LANES], tables).astype(BF16)
        kv = POOL_WIDTH + ATTN_WIDTH
        k_ref[...] = _rope(z[:, kv:kv + KV_WIDTH], tables).astype(BF16)
        v_ref[...] = z[:, kv + KV_WIDTH:].astype(BF16)

    return pl.pallas_call(
        body, name="mixer_in_fwd", grid=(S // tm,),
        in_specs=[_rows(tm, D_MODEL), _full((1, D_MODEL)), _ANY, _rows(tm, 1), _full((1, _LANES))],
        out_specs=[_rows(tm, POOL_WIDTH), _rows(tm, ATTN_WIDTH), _rows(tm, KV_WIDTH), _rows(tm, KV_WIDTH)],
        out_shape=[jax.ShapeDtypeStruct((S, POOL_WIDTH), F32), jax.ShapeDtypeStruct((S, ATTN_WIDTH), BF16),
                   jax.ShapeDtypeStruct((S, KV_WIDTH), BF16), jax.ShapeDtypeStruct((S, KV_WIDTH), BF16)],
        scratch_shapes=[pltpu.VMEM((IN_WIDTH, D_MODEL), BF16), pltpu.SemaphoreType.DMA((1,))],
        compiler_params=_params(dimension_semantics=("arbitrary",)),
    )(x, pre, w_in_t, pos, invf)


def _mixer_in_bwd(dres, x, pre, w_in_t, du, dq, dk, dv, pos, invf):
    S = x.shape[0]
    tm = TOKEN_TILE

    def body(dres_ref, x_ref, pre_ref, w_hbm, du_ref, dq_ref, dk_ref, dv_ref, pos_ref, invf_ref,
             dx_ref, dz_ref, h_ref, dpre_ref, w_v, sem):
        _load_once([(w_hbm, w_v)], sem)

        @pl.when(pl.program_id(0) == 0)
        def _():
            dpre_ref[...] = jnp.zeros_like(dpre_ref)

        tables = _rope_tables(pos_ref, invf_ref)
        dz_ref[:, :POOL_WIDTH] = du_ref[...].astype(BF16)
        for t in range(ATTN_WIDTH // _LANES):
            lo = POOL_WIDTH + t * _LANES
            dz_ref[:, lo:lo + _LANES] = _rope_transposed(dq_ref[:, t * _LANES:(t + 1) * _LANES], tables).astype(BF16)
        kv = POOL_WIDTH + ATTN_WIDTH
        dz_ref[:, kv:kv + KV_WIDTH] = _rope_transposed(dk_ref[...], tables).astype(BF16)
        dz_ref[:, kv + KV_WIDTH:] = dv_ref[...].astype(BF16)
        dh = _dot(dz_ref[...], w_v[...])
        xv = x_ref[...]
        rx = _rstd(xv)
        xn = xv * rx
        h_ref[...] = (xn * pre_ref[...]).astype(BF16)
        dpre_ref[...] += jnp.sum(dh * xn, axis=0, keepdims=True)
        dx_ref[...] = dres_ref[...] + _norm_bwd(dh, xn, rx, pre_ref[...])

    return pl.pallas_call(
        body, name="mixer_in_bwd", grid=(S // tm,),
        in_specs=[_rows(tm, D_MODEL), _rows(tm, D_MODEL), _full((1, D_MODEL)), _ANY,
                  _rows(tm, POOL_WIDTH), _rows(tm, ATTN_WIDTH), _rows(tm, KV_WIDTH), _rows(tm, KV_WIDTH),
                  _rows(tm, 1), _full((1, _LANES))],
        out_specs=[_rows(tm, D_MODEL), _rows(tm, IN_WIDTH), _rows(tm, D_MODEL), _full((1, D_MODEL))],
        out_shape=[jax.ShapeDtypeStruct((S, D_MODEL), F32), jax.ShapeDtypeStruct((S, IN_WIDTH), BF16),
                   jax.ShapeDtypeStruct((S, D_MODEL), BF16), jax.ShapeDtypeStruct((1, D_MODEL), F32)],
        scratch_shapes=[pltpu.VMEM((IN_WIDTH, D_MODEL), BF16), pltpu.SemaphoreType.DMA((1,))],
        compiler_params=_params(dimension_semantics=("arbitrary",)),
    )(dres, x, pre, w_in_t, du, dq, dk, dv, pos, invf)


def _pool_counts(tile_index, tm, width):
    t = tile_index * tm + lax.broadcasted_iota(jnp.int32, (tm, 1), 0)
    return jnp.minimum(t + 1, width).astype(F32)


def _pool_features(ext, u_tile, tile_index, tm):
    ds = []
    for gi, width in enumerate(POOL_WINDOWS):
        lanes = slice(gi * POOL_GROUP, (gi + 1) * POOL_GROUP)
        s = ext[:, lanes]
        shift = 1
        while shift < width:
            s = s + pltpu.roll(s, shift, axis=0)
            shift *= 2
        ds.append(s[HALO:, :] / _pool_counts(tile_index, tm, width) - u_tile[:, lanes])
    return ds


def _pool_fwd(u, w_pool, pool_scale, g_pool):
    S = u.shape[0]
    tm = TOKEN_TILE

    def body(u_ref, w_ref, scale_ref, gain_ref, y_ref, ext_ref):
        i = pl.program_id(0)

        @pl.when(i == 0)
        def _():
            ext_ref[:HALO, :] = jnp.zeros((HALO, POOL_WIDTH), F32)

        u_tile = u_ref[...]
        ext_ref[HALO:, :] = u_tile
        ds = _pool_features(ext_ref[...], u_tile, i, tm)
        ext_ref[:HALO, :] = u_tile[tm - HALO:, :]
        ys = [_dot(ds[gi].astype(BF16), w_ref[gi].astype(BF16)) for gi in range(len(POOL_WINDOWS))]
        po = jnp.concatenate(ys, axis=1) * scale_ref[...]
        y_ref[...] = ((po * _rstd(po)) * gain_ref[...]).astype(BF16)

    return pl.pallas_call(
        body, name="pool_fwd", grid=(S // tm,),
        in_specs=[_rows(tm, POOL_WIDTH), _full((len(POOL_WINDOWS), POOL_GROUP, POOL_GROUP)),
                  _full((1, POOL_WIDTH)), _full((1, POOL_WIDTH))],
        out_specs=_rows(tm, POOL_WIDTH),
        out_shape=jax.ShapeDtypeStruct((S, POOL_WIDTH), BF16),
        scratch_shapes=[pltpu.VMEM((HALO + tm, POOL_WIDTH), F32)],
        compiler_params=_params(dimension_semantics=("arbitrary",)),
    )(u, w_pool, pool_scale, g_pool)


def _pool_bwd(dy, u, w_pool, pool_scale, g_pool):
    S = u.shape[0]
    tm = TOKEN_TILE
    n_tiles = S // tm
    halo_blocks = tm // HALO

    def body(dy_ref, u_ref, uprev_ref, w_ref, scale_ref, gain_ref,
             du_ref, dw_ref, dscale_ref, dgain_ref, ext_ref, nxt_ref):
        i = pl.program_id(0)
        tile = n_tiles - 1 - i

        @pl.when(i == 0)
        def _():
            dw_ref[...] = jnp.zeros_like(dw_ref)
            dscale_ref[...] = jnp.zeros_like(dscale_ref)
            dgain_ref[...] = jnp.zeros_like(dgain_ref)
            nxt_ref[...] = jnp.zeros_like(nxt_ref)

        u_tile = u_ref[...]
        ext_ref[:HALO, :] = jnp.where(tile > 0, uprev_ref[...], 0.0)
        ext_ref[HALO:, :] = u_tile
        ds = _pool_features(ext_ref[...], u_tile, tile, tm)
        dsb = [d.astype(BF16) for d in ds]
        wb = [w_ref[gi].astype(BF16) for gi in range(len(POOL_WINDOWS))]
        yraw = jnp.concatenate([_dot(dsb[gi], wb[gi]) for gi in range(len(POOL_WINDOWS))], axis=1)
        po = yraw * scale_ref[...]
        r = _rstd(po)
        pn = po * r
        dyv = dy_ref[...]
        dgain_ref[...] += jnp.sum(dyv * pn, axis=0, keepdims=True)
        dpo = _norm_bwd(dyv, pn, r, gain_ref[...])
        dscale_ref[...] += jnp.sum(dpo * yraw, axis=0, keepdims=True)
        dyraw = (dpo * scale_ref[...]).astype(BF16)
        for gi, width in enumerate(POOL_WINDOWS):
            lanes = slice(gi * POOL_GROUP, (gi + 1) * POOL_GROUP)
            dw_ref[gi] += _dot_tn(dsb[gi], dyraw[:, lanes])
            dd = _dot_nt(dyraw[:, lanes], wb[gi])
            ddc = dd / _pool_counts(tile, tm, width)
            ext_ref[:tm, lanes] = ddc
            ext_ref[tm:, lanes] = nxt_ref[:, lanes]
            s = ext_ref[:, lanes]
            shift = 1
            while shift < width:
                s = s + pltpu.roll(s, HALO + tm - shift, axis=0)
                shift *= 2
            du_ref[:, lanes] = s[:tm, :] - dd
            nxt_ref[:, lanes] = ddc[:HALO, :]

    return pl.pallas_call(
        body, name="pool_bwd", grid=(n_tiles,),
        in_specs=[pl.BlockSpec((tm, POOL_WIDTH), lambda i: (n_tiles - 1 - i, 0)),
                  pl.BlockSpec((tm, POOL_WIDTH), lambda i: (n_tiles - 1 - i, 0)),
                  pl.BlockSpec((HALO, POOL_WIDTH), lambda i: (jnp.maximum((n_tiles - 1 - i) * halo_blocks - 1, 0), 0)),
                  _full((len(POOL_WINDOWS), POOL_GROUP, POOL_GROUP)), _full((1, POOL_WIDTH)), _full((1, POOL_WIDTH))],
        out_specs=[pl.BlockSpec((tm, POOL_WIDTH), lambda i: (n_tiles - 1 - i, 0)),
                   _full((len(POOL_WINDOWS), POOL_GROUP, POOL_GROUP)), _full((1, POOL_WIDTH)), _full((1, POOL_WIDTH))],
        out_shape=[jax.ShapeDtypeStruct((S, POOL_WIDTH), F32),
                   jax.ShapeDtypeStruct((len(POOL_WINDOWS), POOL_GROUP, POOL_GROUP), F32),
                   jax.ShapeDtypeStruct((1, POOL_WIDTH), F32), jax.ShapeDtypeStruct((1, POOL_WIDTH), F32)],
        scratch_shapes=[pltpu.VMEM((HALO + tm, POOL_WIDTH), F32), pltpu.VMEM((HALO, POOL_WIDTH), F32)],
        compiler_params=_params(dimension_semantics=("arbitrary",)),
    )(dy, u, u, w_pool, pool_scale, g_pool)


def _kv_variants(prev_ref, cur_ref):
    cat = jnp.concatenate([prev_ref[...], cur_ref[...]], axis=0).astype(F32)
    rolled = pltpu.roll(cat, HEAD_DIM, axis=1)
    low = lax.broadcasted_iota(jnp.int32, cat.shape, 1) < HEAD_DIM
    zero = jnp.zeros_like(cat)
    pick = lambda src, keep_low: jnp.where(low if keep_low else ~low, src, zero).astype(BF16)
    return [[pick(cat, True), pick(rolled, False)], [pick(rolled, True), pick(cat, False)]]


def _band_mask(block_index):
    qi = lax.broadcasted_iota(jnp.int32, (BLOCK, 2 * BLOCK), 0)
    kj = lax.broadcasted_iota(jnp.int32, (BLOCK, 2 * BLOCK), 1)
    first_key = jnp.where(block_index > 0, 0, BLOCK)
    return (kj > qi) & (kj <= qi + BLOCK) & (kj >= first_key)


def _attn_probs(q_pair, k_var, valid, sink):
    s = _dot_nt(q_pair, k_var) * ATTN_SCALE
    s = jnp.where(valid, s, NEG_INF)
    m = jnp.maximum(jnp.max(s, axis=-1, keepdims=True), sink)
    p = jnp.exp(s - m)
    p_sink = jnp.exp(sink - m)
    inv = 1.0 / (jnp.sum(p, axis=-1, keepdims=True) + p_sink)
    return p * inv, p_sink * inv


def _attn_fwd(q, k, v, sinks, g_attn):
    S = q.shape[0]
    nb = S // BLOCK

    def body(q_ref, kp_ref, kc_ref, vp_ref, vc_ref, sinks_ref, gain_ref, o_ref, y_ref):
        b = pl.program_id(0)
        kvar = _kv_variants(kp_ref, kc_ref)
        vvar = _kv_variants(vp_ref, vc_ref)
        valid = _band_mask(b)
        pairs = []
        for i in range(N_HEADS // 2):
            q_pair = q_ref[:, i * _LANES:(i + 1) * _LANES]
            acc = jnp.zeros((BLOCK, _LANES), F32)
            for e in range(2):
                head = 2 * i + e
                j = head // (N_HEADS // 2)
                p, _ = _attn_probs(q_pair, kvar[j][e], valid, sinks_ref[0, head])
                acc = acc + _dot(p.astype(BF16), vvar[j][e])
            pairs.append(acc)
        o = jnp.concatenate(pairs, axis=1)
        o_ref[...] = o
        y_ref[...] = ((o * _rstd(o)) * gain_ref[...]).astype(BF16)

    prev = lambda b: (jnp.maximum(b - 1, 0), 0)
    return pl.pallas_call(
        body, name="attn_fwd", grid=(nb,),
        in_specs=[_rows(BLOCK, ATTN_WIDTH),
                  pl.BlockSpec((BLOCK, KV_WIDTH), prev), _rows(BLOCK, KV_WIDTH),
                  pl.BlockSpec((BLOCK, KV_WIDTH), prev), _rows(BLOCK, KV_WIDTH),
                  pl.BlockSpec(memory_space=pltpu.SMEM), _full((1, ATTN_WIDTH))],
        out_specs=[_rows(BLOCK, ATTN_WIDTH), _rows(BLOCK, ATTN_WIDTH)],
        out_shape=[jax.ShapeDtypeStruct((S, ATTN_WIDTH), F32), jax.ShapeDtypeStruct((S, ATTN_WIDTH), BF16)],
        compiler_params=_params(dimension_semantics=("arbitrary",)),
    )(q, k, k, v, v, sinks, g_attn)


def _attn_bwd(dy, o, q, k, v, sinks, g_attn):
    S = q.shape[0]
    nb = S // BLOCK

    def body(dy_ref, o_ref, q_ref, kp_ref, kc_ref, vp_ref, vc_ref, sinks_ref, gain_ref,
             dq_ref, dk_ref, dv_ref, dsink_ref, dgain_ref, kcarry_ref, vcarry_ref):
        b = pl.program_id(0)

        @pl.when(b == 0)
        def _():
            dsink_ref[...] = jnp.zeros_like(dsink_ref)
            dgain_ref[...] = jnp.zeros_like(dgain_ref)
            kcarry_ref[...] = jnp.zeros_like(kcarry_ref)
            vcarry_ref[...] = jnp.zeros_like(vcarry_ref)

        @pl.when(b < nb)
        def _():
            ov = o_ref[...]
            r = _rstd(ov)
            on = ov * r
            dyv = dy_ref[...]
            dgain_ref[...] += jnp.sum(dyv * on, axis=0, keepdims=True)
            do = _norm_bwd(dyv, on, r, gain_ref[...])
            kvar = _kv_variants(kp_ref, kc_ref)
            vvar = _kv_variants(vp_ref, vc_ref)
            valid = _band_mask(b)
            dk_acc = [[jnp.zeros((2 * BLOCK, _LANES), F32) for _ in range(2)] for _ in range(2)]
            dv_acc = [[jnp.zeros((2 * BLOCK, _LANES), F32) for _ in range(2)] for _ in range(2)]
            sink_rows = []
            for i in range(N_HEADS // 2):
                q_pair = q_ref[:, i * _LANES:(i + 1) * _LANES]
                do_pair = do[:, i * _LANES:(i + 1) * _LANES].astype(BF16)
                dq_pair = jnp.zeros((BLOCK, _LANES), F32)
                for e in range(2):
                    head = 2 * i + e
                    j = head // (N_HEADS // 2)
                    p, p_sink = _attn_probs(q_pair, kvar[j][e], valid, sinks_ref[0, head])
                    dp = _dot_nt(do_pair, vvar[j][e])
                    delta = jnp.sum(p * dp, axis=-1, keepdims=True)
                    ds = ((p * (dp - delta)) * ATTN_SCALE).astype(BF16)
                    sink_rows.append(jnp.zeros((1, _LANES), F32) - jnp.sum(p_sink * delta))
                    dq_pair = dq_pair + _dot(ds, kvar[j][e])
                    dk_acc[j][e] = dk_acc[j][e] + _dot_tn(ds, q_pair)
                    dv_acc[j][e] = dv_acc[j][e] + _dot_tn(p.astype(BF16), do_pair)
                dq_ref[:, i * _LANES:(i + 1) * _LANES] = dq_pair
            dsink_ref[...] += jnp.concatenate(sink_rows, axis=0)
            low = lax.broadcasted_iota(jnp.int32, (2 * BLOCK, _LANES), 1) < HEAD_DIM

            def merge(acc):
                return jnp.where(low, acc[0][0] + pltpu.roll(acc[0][1], HEAD_DIM, axis=1),
                                 acc[1][1] + pltpu.roll(acc[1][0], HEAD_DIM, axis=1))
            dk = merge(dk_acc)
            dv = merge(dv_acc)
            dk_ref[...] = kcarry_ref[...] + dk[:BLOCK, :]
            dv_ref[...] = vcarry_ref[...] + dv[:BLOCK, :]
            kcarry_ref[...] = dk[BLOCK:, :]
            vcarry_ref[...] = dv[BLOCK:, :]

        @pl.when(b == nb)
        def _():
            dk_ref[...] = kcarry_ref[...]
            dv_ref[...] = vcarry_ref[...]

    cur = lambda b: (jnp.minimum(b, nb - 1), 0)
    prev = lambda b: (jnp.clip(b - 1, 0, nb - 1), 0)
    return pl.pallas_call(
        body, name="attn_bwd", grid=(nb + 1,),
        in_specs=[pl.BlockSpec((BLOCK, ATTN_WIDTH), lambda b: (jnp.minimum(b, nb - 1), 1)),
                  pl.BlockSpec((BLOCK, ATTN_WIDTH), cur), pl.BlockSpec((BLOCK, ATTN_WIDTH), cur),
                  pl.BlockSpec((BLOCK, KV_WIDTH), prev), pl.BlockSpec((BLOCK, KV_WIDTH), cur),
                  pl.BlockSpec((BLOCK, KV_WIDTH), prev), pl.BlockSpec((BLOCK, KV_WIDTH), cur),
                  pl.BlockSpec(memory_space=pltpu.SMEM), _full((1, ATTN_WIDTH))],
        out_specs=[pl.BlockSpec((BLOCK, ATTN_WIDTH), cur),
                   pl.BlockSpec((BLOCK, KV_WIDTH), prev), pl.BlockSpec((BLOCK, KV_WIDTH), prev),
                   _full((N_HEADS, _LANES)), _full((1, ATTN_WIDTH))],
        out_shape=[jax.ShapeDtypeStruct((S, ATTN_WIDTH), F32), jax.ShapeDtypeStruct((S, KV_WIDTH), F32),
                   jax.ShapeDtypeStruct((S, KV_WIDTH), F32), jax.ShapeDtypeStruct((N_HEADS, _LANES), F32),
                   jax.ShapeDtypeStruct((1, ATTN_WIDTH), F32)],
        scratch_shapes=[pltpu.VMEM((BLOCK, KV_WIDTH), F32), pltpu.VMEM((BLOCK, KV_WIDTH), F32)],
        compiler_params=_params(dimension_semantics=("arbitrary",)),
    )(dy, o, q, k, k, v, v, sinks, g_attn)


def _mixer_out_fwd(y_pool, y_attn, x, w_out, post):
    S = x.shape[0]
    tm = TOKEN_TILE

    def body(yp_ref, ya_ref, x_ref, w_hbm, post_ref, out_ref, m_ref, y_ref, w_v, sem):
        _load_once([(w_hbm, w_v)], sem)
        y_ref[:, :POOL_WIDTH] = yp_ref[...]
        y_ref[:, POOL_WIDTH:] = ya_ref[...]
        m = _dot(y_ref[...], w_v[...])
        m_ref[...] = m
        out_ref[...] = x_ref[...] + (m * _rstd(m)) * post_ref[...]

    return pl.pallas_call(
        body, name="mixer_out_fwd", grid=(S // tm,),
        in_specs=[_rows(tm, POOL_WIDTH), _rows(tm, ATTN_WIDTH), _rows(tm, D_MODEL), _ANY, _full((1, D_MODEL))],
        out_specs=[_rows(tm, D_MODEL), _rows(tm, D_MODEL), _rows(tm, D_MODEL)],
        out_shape=[jax.ShapeDtypeStruct((S, D_MODEL), F32), jax.ShapeDtypeStruct((S, D_MODEL), F32),
                   jax.ShapeDtypeStruct((S, D_MODEL), BF16)],
        scratch_shapes=[pltpu.VMEM((D_MODEL, D_MODEL), BF16), pltpu.SemaphoreType.DMA((1,))],
        compiler_params=_params(dimension_semantics=("arbitrary",)),
    )(y_pool, y_attn, x, w_out, post)


def _mixer_out_bwd(dout, m, w_out, post):
    S = m.shape[0]
    tm = TOKEN_TILE

    def body(dout_ref, m_ref, w_hbm, post_ref, dy_ref, dm_ref, dpost_ref, w_v, sem):
        _load_once([(w_hbm, w_v)], sem)

        @pl.when(pl.program_id(0) == 0)
        def _():
            dpost_ref[...] = jnp.zeros_like(dpost_ref)

        mv = m_ref[...]
        r = _rstd(mv)
        mn = mv * r
        dv = dout_ref[...]
        dpost_ref[...] += jnp.sum(dv * mn, axis=0, keepdims=True)
        dm = _norm_bwd(dv, mn, r, post_ref[...]).astype(BF16)
        dm_ref[...] = dm
        dy_ref[...] = _dot_nt(dm, w_v[...])

    return pl.pallas_call(
        body, name="mixer_out_bwd", grid=(S // tm,),
        in_specs=[_rows(tm, D_MODEL), _rows(tm, D_MODEL), _ANY, _full((1, D_MODEL))],
        out_specs=[_rows(tm, D_MODEL), _rows(tm, D_MODEL), _full((1, D_MODEL))],
        out_shape=[jax.ShapeDtypeStruct((S, D_MODEL), F32), jax.ShapeDtypeStruct((S, D_MODEL), BF16),
                   jax.ShapeDtypeStruct((1, D_MODEL), F32)],
        scratch_shapes=[pltpu.VMEM((D_MODEL, D_MODEL), BF16), pltpu.SemaphoreType.DMA((1,))],
        compiler_params=_params(dimension_semantics=("arbitrary",)),
    )(dout, m, w_out, post)


def _inv_freq_row():
    inv_freq = ROPE_THETA ** (-jnp.arange(0, ROT_DIM, 2, dtype=F32) / ROT_DIM)
    per_head = jnp.concatenate([inv_freq, inv_freq, jnp.zeros((HEAD_DIM - ROT_DIM,), F32)])
    return jnp.tile(per_head, _LANES // HEAD_DIM).reshape(1, _LANES)


def _local_step(x, pos, target, small, wgu1, wd1, w_in_t, w_out, wgu2, wd2):
    invf = _inv_freq_row()
    x1, g1, u1, f1 = _ffn_fwd(x, small["ffn1_pre"], small["ffn1_post"], wgu1, wd1)
    u, q, k, v = _mixer_in_fwd(x1, small["mix_pre"], w_in_t, pos, invf)
    y_pool = _pool_fwd(u, small["w_pool"], small["pool_scale"], small["g_pool"])
    o, y_attn = _attn_fwd(q, k, v, small["sinks"], small["g_attn"])
    x2, m, y = _mixer_out_fwd(y_pool, y_attn, x1, w_out, small["mix_post"])
    dx3, g2, u2, f2, loss_acc = _ffn_fwd(x2, small["ffn2_pre"], small["ffn2_post"], wgu2, wd2, target=target)
    loss = loss_acc[0, 0] * (0.5 / D_MODEL)
    grads = {}
    dx2, h3, dgu2, a2, df2, grads["ffn2_pre"], grads["ffn2_post"] = _ffn_bwd(
        dx3, x2, f2, g2, u2, small["ffn2_pre"], small["ffn2_post"], wgu2, wd2)
    dwgu2 = _wgrad(h3, dgu2, D_MODEL, FF_CHUNK, "wgrad_gu2", column_shards=True)
    dwd2 = _wgrad(a2, df2, FF_CHUNK, D_MODEL, "wgrad_down2")
    dy, dm, grads["mix_post"] = _mixer_out_bwd(dx2, m, w_out, small["mix_post"])
    dw_out = _wgrad(y, dm, D_MODEL, D_MODEL, "wgrad_out")
    dq, dk, dv, dsinks, grads["g_attn"] = _attn_bwd(dy, o, q, k, v, small["sinks"], small["g_attn"])
    grads["sinks"] = dsinks[:, 0].reshape(1, N_HEADS)
    du, grads["w_pool"], grads["pool_scale"], grads["g_pool"] = _pool_bwd(
        dy, u, small["w_pool"], small["pool_scale"], small["g_pool"])
    dx1, dz, h2, grads["mix_pre"] = _mixer_in_bwd(dx2, x1, small["mix_pre"], w_in_t, du, dq, dk, dv, pos, invf)
    dw_in_t = _wgrad(dz, h2, IN_WIDTH, D_MODEL, "wgrad_in")
    dx, h1, dgu1, a1, df1, grads["ffn1_pre"], grads["ffn1_post"] = _ffn_bwd(
        dx1, x, f1, g1, u1, small["ffn1_pre"], small["ffn1_post"], wgu1, wd1)
    dwgu1 = _wgrad(h1, dgu1, D_MODEL, FF_CHUNK, "wgrad_gu1", column_shards=True)
    dwd1 = _wgrad(a1, df1, FF_CHUNK, D_MODEL, "wgrad_down1")
    return loss, dx, (dwgu1, dwd1, dw_in_t, dw_out, dwgu2, dwd2), grads


def _place():
    return lax.axis_index("x"), lax.axis_index("y"), lax.axis_index("c")


def _other_chips(x, y):
    return [(1 - x, y), (x, 1 - y), (1 - x, 1 - y)]


def _hbm_shape(shape, dtype):
    return jax.ShapeDtypeStruct(shape, dtype)


def _all_gather_weights(shards):
    n = len(shards)

    def body(*refs):
        ins, outs = refs[:n], refs[n:2 * n]
        local_sems, send_sems, recv_sems = refs[2 * n:]
        x, y, c = _place()
        mine = 2 * x + y
        chips = _other_chips(x, y)
        local = [pltpu.make_async_copy(ins[w], outs[w].at[mine], local_sems.at[w]) for w in range(n)]
        for cp in local:
            cp.start()
        sends, arrivals = [], []
        for w in range(n):
            for k, (px, py) in enumerate(chips):
                sends.append(pltpu.make_async_remote_copy(
                    src_ref=ins[w], dst_ref=outs[w].at[mine], send_sem=send_sems.at[w, k], recv_sem=recv_sems.at[w, k],
                    device_id=(px, py, c), device_id_type=MESH))
                arrivals.append(pltpu.make_async_remote_copy(
                    src_ref=ins[w], dst_ref=outs[w].at[2 * px + py], send_sem=send_sems.at[w, k],
                    recv_sem=recv_sems.at[w, k], device_id=(px, py, c), device_id_type=MESH))
        for cp in sends:
            cp.start()
        for cp in arrivals:
            cp.wait_recv()
        for cp in sends:
            cp.wait_send()
        for cp in local:
            cp.wait()

    return pl.pallas_call(
        body, name="all_gather_weights",
        in_specs=[_ANY] * n, out_specs=[_ANY] * n,
        out_shape=[_hbm_shape((N_SHARD,) + s.shape, s.dtype) for s in shards],
        scratch_shapes=[pltpu.SemaphoreType.DMA((n,)), pltpu.SemaphoreType.DMA((n, 3)), pltpu.SemaphoreType.DMA((n, 3))],
        compiler_params=pltpu.CompilerParams(has_side_effects=True),
    )(*shards)


def _exchange_grads(grads16, slab):
    n = len(grads16)

    def body(*refs):
        ins, slab_ref = refs[:n], refs[n]
        outs, slab_all = refs[n + 1:2 * n + 1], refs[2 * n + 1]
        local_sem, send_sems, recv_sems, slab_send, slab_recv = refs[2 * n + 2:]
        x, y, c = _place()
        chips = _other_chips(x, y)
        me = 4 * x + 2 * y + c
        own = pltpu.make_async_copy(slab_ref, slab_all.at[me], local_sem)
        own.start()
        copies = []
        for w in range(n):
            for k, (px, py) in enumerate(chips):
                copies.append(pltpu.make_async_remote_copy(
                    src_ref=ins[w].at[2 * px + py], dst_ref=outs[w].at[k], send_sem=send_sems.at[w, k],
                    recv_sem=recv_sems.at[w, k], device_id=(px, py, c), device_id_type=MESH))
        slab_copies, slab_arrivals = [], []
        for k in range(1, 8):
            px, py, pc = x ^ (k >> 2), y ^ ((k >> 1) & 1), c ^ (k & 1)
            slab_copies.append(pltpu.make_async_remote_copy(
                src_ref=slab_ref, dst_ref=slab_all.at[me], send_sem=slab_send.at[k - 1], recv_sem=slab_recv.at[k - 1],
                device_id=(px, py, pc), device_id_type=MESH))
            slab_arrivals.append(pltpu.make_async_remote_copy(
                src_ref=slab_ref, dst_ref=slab_all.at[4 * px + 2 * py + pc], send_sem=slab_send.at[k - 1],
                recv_sem=slab_recv.at[k - 1], device_id=(px, py, pc), device_id_type=MESH))
        for cp in slab_copies + copies:
            cp.start()
        for cp in slab_arrivals + copies:
            cp.wait_recv()
        for cp in slab_copies + copies:
            cp.wait_send()
        own.wait()

    return pl.pallas_call(
        body, name="exchange_grads",
        in_specs=[_ANY] * (n + 1), out_specs=[_ANY] * (n + 1),
        out_shape=[_hbm_shape((3,) + g.shape[1:], g.dtype) for g in grads16] + [_hbm_shape((8,) + slab.shape, slab.dtype)],
        scratch_shapes=[pltpu.SemaphoreType.DMA, pltpu.SemaphoreType.DMA((n, 3)), pltpu.SemaphoreType.DMA((n, 3)),
                        pltpu.SemaphoreType.DMA((7,)), pltpu.SemaphoreType.DMA((7,))],
        compiler_params=pltpu.CompilerParams(has_side_effects=True),
    )(*grads16, slab)


def _exchange_with_sibling(partials):
    n = len(partials)

    def body(*refs):
        ins, outs = refs[:n], refs[n:2 * n]
        local_sems, send_sems, recv_sems = refs[2 * n:]
        x, y, c = _place()
        local = [pltpu.make_async_copy(ins[w], outs[w].at[c], local_sems.at[w]) for w in range(n)]
        sends = [pltpu.make_async_remote_copy(
            src_ref=ins[w], dst_ref=outs[w].at[c], send_sem=send_sems.at[w], recv_sem=recv_sems.at[w],
            device_id=(x, y, 1 - c), device_id_type=MESH) for w in range(n)]
        arrivals = [pltpu.make_async_remote_copy(
            src_ref=ins[w], dst_ref=outs[w].at[1 - c], send_sem=send_sems.at[w], recv_sem=recv_sems.at[w],
            device_id=(x, y, 1 - c), device_id_type=MESH) for w in range(n)]
        for cp in local + sends:
            cp.start()
        for cp in arrivals:
            cp.wait_recv()
        for cp in sends:
            cp.wait_send()
        for cp in local:
            cp.wait()

    return pl.pallas_call(
        body, name="exchange_with_sibling",
        in_specs=[_ANY] * n, out_specs=[_ANY] * n,
        out_shape=[_hbm_shape((2,) + p.shape, p.dtype) for p in partials],
        scratch_shapes=[pltpu.SemaphoreType.DMA((n,)), pltpu.SemaphoreType.DMA((n,)), pltpu.SemaphoreType.DMA((n,))],
        compiler_params=pltpu.CompilerParams(has_side_effects=True),
    )(*partials)


def _row_block(rows, cap):
    best = None
    for cand in range(16, min(rows, cap) + 1, 16):
        if rows % cand == 0:
            best = cand
    assert best is not None, rows
    return best


def _chip_partial(own, received, shard, name):
    _, R, C = own.shape
    rb = _row_block(R, 512)

    def body(shard_ref, own_ref, rec_ref, out_ref):
        acc = own_ref[...]
        for k in range(3):
            acc = acc + rec_ref[k].astype(F32)
        out_ref[...] = acc

    return pl.pallas_call(
        body, name=name,
        grid_spec=pltpu.PrefetchScalarGridSpec(
            num_scalar_prefetch=1, grid=(R // rb,),
            in_specs=[pl.BlockSpec((None, rb, C), lambda i, s: (s[0], i, 0)),
                      pl.BlockSpec((3, rb, C), lambda i, s: (0, i, 0))],
            out_specs=pl.BlockSpec((rb, C), lambda i, s: (i, 0))),
        out_shape=jax.ShapeDtypeStruct((R, C), F32),
        compiler_params=_params(dimension_semantics=("arbitrary",)),
    )(shard, own, received)


def _adamw(w, m, v, g_parts, name):
    R, C = w.shape
    n = g_parts.shape[0]
    rb = _row_block(R, 256) if R % 16 == 0 else R

    def body(w_ref, m_ref, v_ref, g_ref, grad_ref, delta_ref, m_out, v_out):
        g = g_ref[0]
        for k in range(1, n):
            g = g + g_ref[k]
        grad_ref[...] = g
        new_m = ADAM_B1 * m_ref[...] + (1.0 - ADAM_B1) * g
        new_v = ADAM_B2 * v_ref[...] + (1.0 - ADAM_B2) * (g * g)
        m_hat = new_m / (1.0 - ADAM_B1 ** ADAM_STEP)
        v_hat = new_v / (1.0 - ADAM_B2 ** ADAM_STEP)
        delta_ref[...] = -ADAM_LR * (m_hat / (jnp.sqrt(v_hat) + ADAM_EPS) + ADAM_WD * w_ref[...])
        m_out[...] = new_m
        v_out[...] = new_v

    spec = pl.BlockSpec((rb, C), lambda i: (i, 0))
    return pl.pallas_call(
        body, name=name, grid=(R // rb,),
        in_specs=[spec, spec, spec, pl.BlockSpec((n, rb, C), lambda i: (0, i, 0))],
        out_specs=[spec] * 4,
        out_shape=[jax.ShapeDtypeStruct((R, C), F32)] * 4,
        compiler_params=_params(dimension_semantics=("arbitrary",)),
    )(w, m, v, g_parts)


SMALL_NAMES = ("ffn1_pre", "ffn1_post", "mix_pre", "pool_scale", "sinks", "g_pool", "g_attn", "mix_post",
               "ffn2_pre", "ffn2_post", "w_pool")
_SLAB_PART = 8 * _LANES


def _to_slab(parts):
    rows = []
    for name in SMALL_NAMES:
        flat = parts[name].reshape(-1)
        padded = -(-flat.shape[0] // _SLAB_PART) * _SLAB_PART
        rows.append(jnp.pad(flat, (0, padded - flat.shape[0])).reshape(-1, _LANES))
    return jnp.concatenate(rows, axis=0)


def _from_slab(slab, like):
    out, row = {}, 0
    for name in SMALL_NAMES:
        size = like[name].size
        rows = -(-size // _SLAB_PART) * (_SLAB_PART // _LANES)
        out[name] = slab[row:row + rows].reshape(-1)[:size].reshape(like[name].shape)
        row += rows
    return out


BIG_NAMES = ("ffn1_w_gu", "ffn1_w_down", "w_in", "w_out", "ffn2_w_gu", "ffn2_w_down")
WEIGHT_ORDER = ("ffn1_pre", "ffn1_w_gu", "ffn1_w_down", "ffn1_post", "mix_pre", "w_in", "w_pool", "pool_scale",
                "sinks", "g_pool", "g_attn", "w_out", "mix_post", "ffn2_pre", "ffn2_w_gu", "ffn2_w_down", "ffn2_post")


def kernel(x, positions, ffn1_pre, ffn1_w_gu, ffn1_w_down, ffn1_post, mix_pre, w_in, w_pool, pool_scale, sinks, g_pool, g_attn, w_out, mix_post, ffn2_pre, ffn2_w_gu, ffn2_w_down, ffn2_post, loss_target, m_ffn1_pre, m_ffn1_w_gu, m_ffn1_w_down, m_ffn1_post, m_mix_pre, m_w_in, m_w_pool, m_pool_scale, m_sinks, m_g_pool, m_g_attn, m_w_out, m_mix_post, m_ffn2_pre, m_ffn2_w_gu, m_ffn2_w_down, m_ffn2_post, v_ffn1_pre, v_ffn1_w_gu, v_ffn1_w_down, v_ffn1_post, v_mix_pre, v_w_in, v_w_pool, v_pool_scale, v_sinks, v_g_pool, v_g_attn, v_w_out, v_mix_post, v_ffn2_pre, v_ffn2_w_gu, v_ffn2_w_down, v_ffn2_post):
    given = dict(locals())
    weights = {n: given[n][0] for n in WEIGHT_ORDER}
    moments_m = {n: given["m_" + n][0] for n in WEIGHT_ORDER}
    moments_v = {n: given["v_" + n][0] for n in WEIGHT_ORDER}
    S = x.shape[1]
    shard = (2 * lax.axis_index("x") + lax.axis_index("y")).astype(jnp.int32).reshape(1)

    local16 = [weights["ffn1_w_gu"].astype(BF16), weights["ffn1_w_down"].astype(BF16),
               weights["w_in"].T.astype(BF16), weights["w_out"].astype(BF16),
               weights["ffn2_w_gu"].astype(BF16), weights["ffn2_w_down"].astype(BF16)]
    wgu1, wd1, w_in_t, w_out_full, wgu2, wd2 = _all_gather_weights(local16)
    small = {n: (weights[n] if weights[n].ndim > 1 else weights[n].reshape(1, -1)) for n in SMALL_NAMES}
    loss, dx, big, small_grads = _local_step(
        x[0], positions.reshape(S, 1), loss_target[0], small,
        wgu1, wd1.reshape(D_FF, D_MODEL), w_in_t.reshape(IN_WIDTH, D_MODEL), w_out_full.reshape(D_MODEL, D_MODEL),
        wgu2, wd2.reshape(D_FF, D_MODEL))
    loss = lax.psum(loss, ("x", "y", "c"))

    received = _exchange_grads([g16 for _, g16 in big], _to_slab(small_grads))
    slab_all = received[-1]
    partials = [_chip_partial(g32, rec, shard, "chip_partial_" + name)
                for name, (g32, _), rec in zip(BIG_NAMES, big, received[:-1])]
    pairs = _exchange_with_sibling(partials)

    grad, delta, new_m, new_v = {}, {}, {}, {}
    for name, pair in zip(BIG_NAMES, pairs):
        if name == "w_in":
            pair = pair.transpose(0, 2, 1)
        grad[name], delta[name], new_m[name], new_v[name] = _adamw(
            weights[name], moments_m[name], moments_v[name], pair, "adamw_" + name)
    small_like = {n: small[n] for n in SMALL_NAMES}
    slabs = _adamw(_to_slab(small), _to_slab({n: moments_m[n] for n in SMALL_NAMES}),
                   _to_slab({n: moments_v[n] for n in SMALL_NAMES}), slab_all, "adamw_small")
    for store, slab in zip((grad, delta, new_m, new_v), slabs):
        store.update(_from_slab(slab, small_like))

    def out(store):
        return [store[n].reshape(given[n].shape) for n in WEIGHT_ORDER]
    return (loss, dx[None], *out(grad), *out(delta), *out(new_m), *out(new_v))
```

```python
import functools

import jax
import jax.numpy as jnp
from jax import lax
from jax.experimental import pallas as pl
from jax.experimental.pallas import tpu as pltpu

F32 = jnp.float32
BF16 = jnp.bfloat16

D_MODEL = 1024
D_FF = 2816
N_SHARD = 4
FF_CHUNK = D_FF // 2
POOL_WINDOWS = (2, 4, 8, 16)
POOL_WIDTH = 512
POOL_GROUP = 128
HALO = 16
HEAD_DIM = 64
N_HEADS = 8
ATTN_WIDTH = 512
KV_WIDTH = 128
IN_WIDTH = 1280
BLOCK = 128
ROT_DIM = 16
ROPE_THETA = 500000.0
EPS = 1e-6
NEG_INF = -1e30
ATTN_SCALE = HEAD_DIM ** -0.5

ADAM_LR = 0.001
ADAM_B1 = 0.9
ADAM_B2 = 0.999
ADAM_EPS = 1e-08
ADAM_WD = 0.01
ADAM_STEP = 10

VMEM_LIMIT = 60 * 1024 * 1024
TOKEN_TILE = 256

MESH = pl.DeviceIdType.MESH


def _params(**kw):
    return pltpu.CompilerParams(vmem_limit_bytes=VMEM_LIMIT, **kw)


def _dot(a, b):
    return jnp.dot(a, b, preferred_element_type=F32)


def _dot_nt(a, b):
    return lax.dot_general(a, b, (((1,), (1,)), ((), ())), preferred_element_type=F32)


def _dot_tn(a, b):
    return lax.dot_general(a, b, (((0,), (0,)), ((), ())), preferred_element_type=F32)


def _rstd(x):
    return lax.rsqrt(jnp.mean(x * x, axis=-1, keepdims=True) + EPS)


def _norm_bwd(dy, xn, r, gain):
    dxn = dy * gain
    return r * (dxn - xn * jnp.mean(dxn * xn, axis=-1, keepdims=True))


def _sigmoid(x):
    return 1.0 / (1.0 + jnp.exp(-x))


def _full(shape):
    return pl.BlockSpec(shape, lambda *_: (0,) * len(shape))


def _rows(tile, width, col=0):
    return pl.BlockSpec((tile, width), lambda i: (i, col))


_ANY = pl.BlockSpec(memory_space=pl.ANY)


_SMEM = pl.BlockSpec(memory_space=pltpu.SMEM)


def _load_once(pairs, sem):
    @pl.when(pl.program_id(0) == 0)
    def _():
        copies = [pltpu.make_async_copy(src, dst, sem.at[n]) for n, (src, dst) in enumerate(pairs)]
        for cp in copies:
            cp.start()
        for cp in copies:
            cp.wait()


def _gathered(land_ref, own_ref, vmem_ref, mine, rows=None):
    def dst(slot):
        if rows is None:
            return vmem_ref.at[slot]
        return vmem_ref.at[pl.ds(pl.multiple_of(slot * rows, 16), rows), :]
    pairs = [(land_ref.at[(mine + d) % N_SHARD], dst((mine + d) % N_SHARD)) for d in range(1, N_SHARD)]
    return pairs + [(own_ref, dst(mine))]


def _ignoring(body, start, count):
    def wrapped(*refs):
        return body(*refs[:start], *refs[start + count:])
    return wrapped


def _ffn_fwd(x, pre, post, wgu, wd, mine, target=None, deps=()):
    S = x.shape[0]
    tm = TOKEN_TILE
    with_loss = target is not None

    def body(*refs):
        if with_loss:
            (x_ref, pre_ref, post_ref, wgu_land, wgu_own, wd_land, wd_own, mine_ref, tgt_ref,
             out_ref, g_ref, u_ref, f_ref, loss_ref, wgu_v, wd_v, sem) = refs
        else:
            (x_ref, pre_ref, post_ref, wgu_land, wgu_own, wd_land, wd_own, mine_ref,
             out_ref, g_ref, u_ref, f_ref, wgu_v, wd_v, sem) = refs
        _load_once(_gathered(wgu_land, wgu_own, wgu_v, mine_ref[0])
                   + _gathered(wd_land, wd_own, wd_v, mine_ref[0], rows=D_FF // N_SHARD), sem)
        xv = x_ref[...]
        h = ((xv * _rstd(xv)) * pre_ref[...]).astype(BF16)
        facc = jnp.zeros((tm, D_MODEL), F32)
        for c in range(2):
            cols = slice(c * FF_CHUNK, (c + 1) * FF_CHUNK)
            g = _dot(h, wgu_v[c])
            u = _dot(h, wgu_v[2 + c])
            g_ref[:, cols] = g.astype(BF16)
            u_ref[:, cols] = u.astype(BF16)
            a = (g * _sigmoid(g)) * u
            facc = facc + _dot(a.astype(BF16), wd_v[cols, :])
        f_ref[...] = facc
        out = xv + 0.5 * ((facc * _rstd(facc)) * post_ref[...])
        if with_loss:
            diff = out - tgt_ref[...]
            out_ref[...] = diff * (1.0 / D_MODEL)

            @pl.when(pl.program_id(0) == 0)
            def _():
                loss_ref[...] = jnp.zeros_like(loss_ref)
            loss_ref[...] += jnp.sum(diff * diff)
        else:
            out_ref[...] = out

    in_specs = [_rows(tm, D_MODEL), _full((1, D_MODEL)), _full((1, D_MODEL)), _ANY, _ANY, _ANY, _ANY, _SMEM]
    args = [x, pre, post, *wgu, *wd, mine]
    out_shape = [jax.ShapeDtypeStruct((S, D_MODEL), F32), jax.ShapeDtypeStruct((S, D_FF), BF16),
                 jax.ShapeDtypeStruct((S, D_FF), BF16), jax.ShapeDtypeStruct((S, D_MODEL), F32)]
    out_specs = [_rows(tm, D_MODEL), _rows(tm, D_FF), _rows(tm, D_FF), _rows(tm, D_MODEL)]
    if with_loss:
        in_specs.append(_rows(tm, D_MODEL))
        args.append(target)
        out_shape.append(jax.ShapeDtypeStruct((8, 128), F32))
        out_specs.append(_full((8, 128)))
    return pl.pallas_call(
        _ignoring(body, len(args), len(deps)), name="ffn_fwd_loss" if with_loss else "ffn_fwd",
        grid=(S // tm,), in_specs=in_specs + [_ANY] * len(deps), out_specs=out_specs, out_shape=out_shape,
        scratch_shapes=[pltpu.VMEM((N_SHARD, D_MODEL, FF_CHUNK), BF16), pltpu.VMEM((D_FF, D_MODEL), BF16),
                        pltpu.SemaphoreType.DMA((2 * N_SHARD,))],
        compiler_params=_params(dimension_semantics=("arbitrary",)),
    )(*args, *deps)


def _ffn_bwd(dout, x, f, g, u, pre, post, wgu, wd, mine, deps=()):
    S = x.shape[0]
    tm = TOKEN_TILE

    def body(dout_ref, x_ref, f_ref, g_ref, u_ref, pre_ref, post_ref, wgu_land, wgu_own, wd_land, wd_own, mine_ref,
             dx_ref, h_ref, dgu_ref, a_ref, df_ref, dpre_ref, dpost_ref, wgu_v, wd_v, sem):
        _load_once(_gathered(wgu_land, wgu_own, wgu_v, mine_ref[0])
                   + _gathered(wd_land, wd_own, wd_v, mine_ref[0], rows=D_FF // N_SHARD), sem)

        @pl.when(pl.program_id(0) == 0)
        def _():
            dpre_ref[...] = jnp.zeros_like(dpre_ref)
            dpost_ref[...] = jnp.zeros_like(dpost_ref)

        dout_v = dout_ref[...]
        dn = 0.5 * dout_v
        fv = f_ref[...]
        rf = _rstd(fv)
        fn = fv * rf
        dpost_ref[...] += jnp.sum(dn * fn, axis=0, keepdims=True)
        df = _norm_bwd(dn, fn, rf, post_ref[...]).astype(BF16)
        df_ref[...] = df
        dh = jnp.zeros((tm, D_MODEL), F32)
        for c in range(2):
            cols = slice(c * FF_CHUNK, (c + 1) * FF_CHUNK)
            da = _dot_nt(df, wd_v[cols, :])
            gv = g_ref[:, cols].astype(F32)
            uv = u_ref[:, cols].astype(F32)
            sg = _sigmoid(gv)
            silu = gv * sg
            a_ref[:, cols] = (silu * uv).astype(BF16)
            dg = ((da * uv) * (sg * (1.0 + gv * (1.0 - sg)))).astype(BF16)
            du = (da * silu).astype(BF16)
            dgu_ref[:, cols] = dg
            dgu_ref[:, 2 * FF_CHUNK + c * FF_CHUNK:2 * FF_CHUNK + (c + 1) * FF_CHUNK] = du
            dh = dh + _dot_nt(dg, wgu_v[c]) + _dot_nt(du, wgu_v[2 + c])
        xv = x_ref[...]
        rx = _rstd(xv)
        xn = xv * rx
        h_ref[...] = (xn * pre_ref[...]).astype(BF16)
        dpre_ref[...] += jnp.sum(dh * xn, axis=0, keepdims=True)
        dx_ref[...] = dout_v + _norm_bwd(dh, xn, rx, pre_ref[...])

    args = [dout, x, f, g, u, pre, post, *wgu, *wd, mine]
    return pl.pallas_call(
        _ignoring(body, len(args), len(deps)), name="ffn_bwd", grid=(S // tm,),
        in_specs=[_rows(tm, D_MODEL), _rows(tm, D_MODEL), _rows(tm, D_MODEL), _rows(tm, D_FF), _rows(tm, D_FF),
                  _full((1, D_MODEL)), _full((1, D_MODEL)), _ANY, _ANY, _ANY, _ANY, _SMEM] + [_ANY] * len(deps),
        out_specs=[_rows(tm, D_MODEL), _rows(tm, D_MODEL), _rows(tm, 2 * D_FF), _rows(tm, D_FF), _rows(tm, D_MODEL),
                   _full((1, D_MODEL)), _full((1, D_MODEL))],
        out_shape=[jax.ShapeDtypeStruct((S, D_MODEL), F32), jax.ShapeDtypeStruct((S, D_MODEL), BF16),
                   jax.ShapeDtypeStruct((S, 2 * D_FF), BF16), jax.ShapeDtypeStruct((S, D_FF), BF16),
                   jax.ShapeDtypeStruct((S, D_MODEL), BF16),
                   jax.ShapeDtypeStruct((1, D_MODEL), F32), jax.ShapeDtypeStruct((1, D_MODEL), F32)],
        scratch_shapes=[pltpu.VMEM((N_SHARD, D_MODEL, FF_CHUNK), BF16), pltpu.VMEM((D_FF, D_MODEL), BF16),
                        pltpu.SemaphoreType.DMA((2 * N_SHARD,))],
        compiler_params=_params(dimension_semantics=("arbitrary",)),
    )(*args, *deps)


def _wgrad(lhs, rhs, m_block, n_block, name, column_shards=False, tk=512, deps=()):
    S, M = lhs.shape
    N = rhs.shape[1]
    k_steps = S // tk

    def body(lhs_ref, rhs_ref, out_ref, out16_ref):
        k = pl.program_id(2)

        @pl.when(k == 0)
        def _():
            out_ref[...] = jnp.zeros_like(out_ref)
        out_ref[...] += _dot_tn(lhs_ref[...], rhs_ref[...])

        @pl.when(k == k_steps - 1)
        def _():
            out16_ref[...] = out_ref[...].astype(BF16)

    if column_shards:
        assert N == N_SHARD * n_block
        shape = (N_SHARD, M, n_block)
        out_spec = pl.BlockSpec((None, m_block, n_block), lambda i, j, k: (j, i, 0))
    else:
        shape = (M, N)
        out_spec = pl.BlockSpec((m_block, n_block), lambda i, j, k: (i, j))
    out, out16 = pl.pallas_call(
        _ignoring(body, 2, len(deps)), name=name, grid=(M // m_block, N // n_block, k_steps),
        in_specs=[pl.BlockSpec((tk, m_block), lambda i, j, k: (k, i)),
                  pl.BlockSpec((tk, n_block), lambda i, j, k: (k, j))] + [_ANY] * len(deps),
        out_specs=[out_spec, out_spec],
        out_shape=[jax.ShapeDtypeStruct(shape, F32), jax.ShapeDtypeStruct(shape, BF16)],
        compiler_params=_params(dimension_semantics=("arbitrary", "arbitrary", "arbitrary")),
    )(lhs, rhs, *deps)
    if not column_shards:
        out = out.reshape(N_SHARD, M // N_SHARD, N)
        out16 = out16.reshape(N_SHARD, M // N_SHARD, N)
    return out, out16


def _rope_tables(pos_ref, invf_ref):
    ang = pos_ref[...].astype(F32) * invf_ref[...]
    cos, sin = jnp.cos(ang), jnp.sin(ang)
    lane = lax.broadcasted_iota(jnp.int32, ang.shape, 1) % HEAD_DIM
    first = lane < ROT_DIM // 2
    second = (lane >= ROT_DIM // 2) & (lane < ROT_DIM)
    c = jnp.where(lane < ROT_DIM, cos, 1.0)
    s_first = jnp.where(first, sin, 0.0)
    s_second = jnp.where(second, sin, 0.0)
    return c, s_first, s_second


_HALF = ROT_DIM // 2
_LANES = 128


def _rope(t, tables):
    c, s_first, s_second = tables
    return t * c - pltpu.roll(t, _LANES - _HALF, axis=1) * s_first + pltpu.roll(t, _HALF, axis=1) * s_second


def _rope_transposed(t, tables):
    c, s_first, s_second = tables
    return t * c - pltpu.roll(t * s_first, _HALF, axis=1) + pltpu.roll(t * s_second, _LANES - _HALF, axis=1)


def _mixer_in_fwd(x, pre, w_in_t, mine, pos, invf, deps=()):
    S = x.shape[0]
    tm = TOKEN_TILE

    def body(x_ref, pre_ref, w_land, w_own, mine_ref, pos_ref, invf_ref, u_ref, q_ref, k_ref, v_ref, w_v, sem):
        _load_once(_gathered(w_land, w_own, w_v, mine_ref[0], rows=IN_WIDTH // N_SHARD), sem)
        xv = x_ref[...]
        h = ((xv * _rstd(xv)) * pre_ref[...]).astype(BF16)
        z = _dot_nt(h, w_v[...])
        tables = _rope_tables(pos_ref, invf_ref)
        u_ref[...] = z[:, :POOL_WIDTH]
        for t in range(ATTN_WIDTH // _LANES):
            lo = POOL_WIDTH + t * _LANES
            q_ref[:, t * _LANES:(t + 1) * _LANES] = _rope(z[:, lo:lo + _LANES], tables).astype(BF16)
        kv = POOL_WIDTH + ATTN_WIDTH
        k_ref[...] = _rope(z[:, kv:kv + KV_WIDTH], tables).astype(BF16)
        v_ref[...] = z[:, kv + KV_WIDTH:].astype(BF16)

    args = [x, pre, *w_in_t, mine, pos, invf]
    return pl.pallas_call(
        _ignoring(body, len(args), len(deps)), name="mixer_in_fwd", grid=(S // tm,),
        in_specs=[_rows(tm, D_MODEL), _full((1, D_MODEL)), _ANY, _ANY, _SMEM, _rows(tm, 1), _full((1, _LANES))]
        + [_ANY] * len(deps),
        out_specs=[_rows(tm, POOL_WIDTH), _rows(tm, ATTN_WIDTH), _rows(tm, KV_WIDTH), _rows(tm, KV_WIDTH)],
        out_shape=[jax.ShapeDtypeStruct((S, POOL_WIDTH), F32), jax.ShapeDtypeStruct((S, ATTN_WIDTH), BF16),
                   jax.ShapeDtypeStruct((S, KV_WIDTH), BF16), jax.ShapeDtypeStruct((S, KV_WIDTH), BF16)],
        scratch_shapes=[pltpu.VMEM((IN_WIDTH, D_MODEL), BF16), pltpu.SemaphoreType.DMA((N_SHARD,))],
        compiler_params=_params(dimension_semantics=("arbitrary",)),
    )(*args, *deps)


def _mixer_in_bwd(dres, x, pre, w_in_t, mine, du, dq, dk, dv, pos, invf, deps=()):
    S = x.shape[0]
    tm = TOKEN_TILE

    def body(dres_ref, x_ref, pre_ref, w_land, w_own, mine_ref, du_ref, dq_ref, dk_ref, dv_ref, pos_ref, invf_ref,
             dx_ref, dz_ref, h_ref, dpre_ref, w_v, sem):
        _load_once(_gathered(w_land, w_own, w_v, mine_ref[0], rows=IN_WIDTH // N_SHARD), sem)

        @pl.when(pl.program_id(0) == 0)
        def _():
            dpre_ref[...] = jnp.zeros_like(dpre_ref)

        tables = _rope_tables(pos_ref, invf_ref)
        dz_ref[:, :POOL_WIDTH] = du_ref[...].astype(BF16)
        for t in range(ATTN_WIDTH // _LANES):
            lo = POOL_WIDTH + t * _LANES
            dz_ref[:, lo:lo + _LANES] = _rope_transposed(dq_ref[:, t * _LANES:(t + 1) * _LANES], tables).astype(BF16)
        kv = POOL_WIDTH + ATTN_WIDTH
        dz_ref[:, kv:kv + KV_WIDTH] = _rope_transposed(dk_ref[...], tables).astype(BF16)
        dz_ref[:, kv + KV_WIDTH:] = dv_ref[...].astype(BF16)
        dh = _dot(dz_ref[...], w_v[...])
        xv = x_ref[...]
        rx = _rstd(xv)
        xn = xv * rx
        h_ref[...] = (xn * pre_ref[...]).astype(BF16)
        dpre_ref[...] += jnp.sum(dh * xn, axis=0, keepdims=True)
        dx_ref[...] = dres_ref[...] + _norm_bwd(dh, xn, rx, pre_ref[...])

    args = [dres, x, pre, *w_in_t, mine, du, dq, dk, dv, pos, invf]
    return pl.pallas_call(
        _ignoring(body, len(args), len(deps)), name="mixer_in_bwd", grid=(S // tm,),
        in_specs=[_rows(tm, D_MODEL), _rows(tm, D_MODEL), _full((1, D_MODEL)), _ANY, _ANY, _SMEM,
                  _rows(tm, POOL_WIDTH), _rows(tm, ATTN_WIDTH), _rows(tm, KV_WIDTH), _rows(tm, KV_WIDTH),
                  _rows(tm, 1), _full((1, _LANES))] + [_ANY] * len(deps),
        out_specs=[_rows(tm, D_MODEL), _rows(tm, IN_WIDTH), _rows(tm, D_MODEL), _full((1, D_MODEL))],
        out_shape=[jax.ShapeDtypeStruct((S, D_MODEL), F32), jax.ShapeDtypeStruct((S, IN_WIDTH), BF16),
                   jax.ShapeDtypeStruct((S, D_MODEL), BF16), jax.ShapeDtypeStruct((1, D_MODEL), F32)],
        scratch_shapes=[pltpu.VMEM((IN_WIDTH, D_MODEL), BF16), pltpu.SemaphoreType.DMA((N_SHARD,))],
        compiler_params=_params(dimension_semantics=("arbitrary",)),
    )(*args, *deps)


def _pool_counts(tile_index, tm, width):
    t = tile_index * tm + lax.broadcasted_iota(jnp.int32, (tm, 1), 0)
    return jnp.minimum(t + 1, width).astype(F32)


def _pool_features(ext, u_tile, tile_index, tm):
    ds = []
    for gi, width in enumerate(POOL_WINDOWS):
        lanes = slice(gi * POOL_GROUP, (gi + 1) * POOL_GROUP)
        s = ext[:, lanes]
        shift = 1
        while shift < width:
            s = s + pltpu.roll(s, shift, axis=0)
            shift *= 2
        ds.append(s[HALO:, :] / _pool_counts(tile_index, tm, width) - u_tile[:, lanes])
    return ds


def _pool_fwd(u, w_pool, pool_scale, g_pool):
    S = u.shape[0]
    tm = TOKEN_TILE

    def body(u_ref, w_ref, scale_ref, gain_ref, y_ref, ext_ref):
        i = pl.program_id(0)

        @pl.when(i == 0)
        def _():
            ext_ref[:HALO, :] = jnp.zeros((HALO, POOL_WIDTH), F32)

        u_tile = u_ref[...]
        ext_ref[HALO:, :] = u_tile
        ds = _pool_features(ext_ref[...], u_tile, i, tm)
        ext_ref[:HALO, :] = u_tile[tm - HALO:, :]
        ys = [_dot(ds[gi].astype(BF16), w_ref[gi].astype(BF16)) for gi in range(len(POOL_WINDOWS))]
        po = jnp.concatenate(ys, axis=1) * scale_ref[...]
        y_ref[...] = ((po * _rstd(po)) * gain_ref[...]).astype(BF16)

    return pl.pallas_call(
        body, name="pool_fwd", grid=(S // tm,),
        in_specs=[_rows(tm, POOL_WIDTH), _full((len(POOL_WINDOWS), POOL_GROUP, POOL_GROUP)),
                  _full((1, POOL_WIDTH)), _full((1, POOL_WIDTH))],
        out_specs=_rows(tm, POOL_WIDTH),
        out_shape=jax.ShapeDtypeStruct((S, POOL_WIDTH), BF16),
        scratch_shapes=[pltpu.VMEM((HALO + tm, POOL_WIDTH), F32)],
        compiler_params=_params(dimension_semantics=("arbitrary",)),
    )(u, w_pool, pool_scale, g_pool)


def _pool_bwd(dy, u, w_pool, pool_scale, g_pool):
    S = u.shape[0]
    tm = TOKEN_TILE
    n_tiles = S // tm
    halo_blocks = tm // HALO

    def body(dy_ref, u_ref, uprev_ref, w_ref, scale_ref, gain_ref,
             du_ref, dw_ref, dscale_ref, dgain_ref, ext_ref, nxt_ref):
        i = pl.program_id(0)
        tile = n_tiles - 1 - i

        @pl.when(i == 0)
        def _():
            dw_ref[...] = jnp.zeros_like(dw_ref)
            dscale_ref[...] = jnp.zeros_like(dscale_ref)
            dgain_ref[...] = jnp.zeros_like(dgain_ref)
            nxt_ref[...] = jnp.zeros_like(nxt_ref)

        u_tile = u_ref[...]
        ext_ref[:HALO, :] = jnp.where(tile > 0, uprev_ref[...], 0.0)
        ext_ref[HALO:, :] = u_tile
        ds = _pool_features(ext_ref[...], u_tile, tile, tm)
        dsb = [d.astype(BF16) for d in ds]
        wb = [w_ref[gi].astype(BF16) for gi in range(len(POOL_WINDOWS))]
        yraw = jnp.concatenate([_dot(dsb[gi], wb[gi]) for gi in range(len(POOL_WINDOWS))], axis=1)
        po = yraw * scale_ref[...]
        r = _rstd(po)
        pn = po * r
        dyv = dy_ref[...]
        dgain_ref[...] += jnp.sum(dyv * pn, axis=0, keepdims=True)
        dpo = _norm_bwd(dyv, pn, r, gain_ref[...])
        dscale_ref[...] += jnp.sum(dpo * yraw, axis=0, keepdims=True)
        dyraw = (dpo * scale_ref[...]).astype(BF16)
        for gi, width in enumerate(POOL_WINDOWS):
            lanes = slice(gi * POOL_GROUP, (gi + 1) * POOL_GROUP)
            dw_ref[gi] += _dot_tn(dsb[gi], dyraw[:, lanes])
            dd = _dot_nt(dyraw[:, lanes], wb[gi])
            ddc = dd / _pool_counts(tile, tm, width)
            ext_ref[:tm, lanes] = ddc
            ext_ref[tm:, lanes] = nxt_ref[:, lanes]
            s = ext_ref[:, lanes]
            shift = 1
            while shift < width:
                s = s + pltpu.roll(s, HALO + tm - shift, axis=0)
                shift *= 2
            du_ref[:, lanes] = s[:tm, :] - dd
            nxt_ref[:, lanes] = ddc[:HALO, :]

    return pl.pallas_call(
        body, name="pool_bwd", grid=(n_tiles,),
        in_specs=[pl.BlockSpec((tm, POOL_WIDTH), lambda i: (n_tiles - 1 - i, 0)),
                  pl.BlockSpec((tm, POOL_WIDTH), lambda i: (n_tiles - 1 - i, 0)),
                  pl.BlockSpec((HALO, POOL_WIDTH), lambda i: (jnp.maximum((n_tiles - 1 - i) * halo_blocks - 1, 0), 0)),
                  _full((len(POOL_WINDOWS), POOL_GROUP, POOL_GROUP)), _full((1, POOL_WIDTH)), _full((1, POOL_WIDTH))],
        out_specs=[pl.BlockSpec((tm, POOL_WIDTH), lambda i: (n_tiles - 1 - i, 0)),
                   _full((len(POOL_WINDOWS), POOL_GROUP, POOL_GROUP)), _full((1, POOL_WIDTH)), _full((1, POOL_WIDTH))],
        out_shape=[jax.ShapeDtypeStruct((S, POOL_WIDTH), F32),
                   jax.ShapeDtypeStruct((len(POOL_WINDOWS), POOL_GROUP, POOL_GROUP), F32),
                   jax.ShapeDtypeStruct((1, POOL_WIDTH), F32), jax.ShapeDtypeStruct((1, POOL_WIDTH), F32)],
        scratch_shapes=[pltpu.VMEM((HALO + tm, POOL_WIDTH), F32), pltpu.VMEM((HALO, POOL_WIDTH), F32)],
        compiler_params=_params(dimension_semantics=("arbitrary",)),
    )(dy, u, u, w_pool, pool_scale, g_pool)


def _kv_variants(prev_ref, cur_ref):
    cat = jnp.concatenate([prev_ref[...], cur_ref[...]], axis=0).astype(F32)
    rolled = pltpu.roll(cat, HEAD_DIM, axis=1)
    low = lax.broadcasted_iota(jnp.int32, cat.shape, 1) < HEAD_DIM
    zero = jnp.zeros_like(cat)
    pick = lambda src, keep_low: jnp.where(low if keep_low else ~low, src, zero).astype(BF16)
    return [[pick(cat, True), pick(rolled, False)], [pick(rolled, True), pick(cat, False)]]


def _band_mask(block_index):
    qi = lax.broadcasted_iota(jnp.int32, (BLOCK, 2 * BLOCK), 0)
    kj = lax.broadcasted_iota(jnp.int32, (BLOCK, 2 * BLOCK), 1)
    first_key = jnp.where(block_index > 0, 0, BLOCK)
    return (kj > qi) & (kj <= qi + BLOCK) & (kj >= first_key)


def _attn_probs(q_pair, k_var, valid, sink):
    s = _dot_nt(q_pair, k_var) * ATTN_SCALE
    s = jnp.where(valid, s, NEG_INF)
    m = jnp.maximum(jnp.max(s, axis=-1, keepdims=True), sink)
    p = jnp.exp(s - m)
    p_sink = jnp.exp(sink - m)
    inv = 1.0 / (jnp.sum(p, axis=-1, keepdims=True) + p_sink)
    return p * inv, p_sink * inv


def _attn_fwd(q, k, v, sinks, g_attn):
    S = q.shape[0]
    nb = S // BLOCK

    def body(q_ref, kp_ref, kc_ref, vp_ref, vc_ref, sinks_ref, gain_ref, o_ref, y_ref):
        b = pl.program_id(0)
        kvar = _kv_variants(kp_ref, kc_ref)
        vvar = _kv_variants(vp_ref, vc_ref)
        valid = _band_mask(b)
        pairs = []
        for i in range(N_HEADS // 2):
            q_pair = q_ref[:, i * _LANES:(i + 1) * _LANES]
            acc = jnp.zeros((BLOCK, _LANES), F32)
            for e in range(2):
                head = 2 * i + e
                j = head // (N_HEADS // 2)
                p, _ = _attn_probs(q_pair, kvar[j][e], valid, sinks_ref[0, head])
                acc = acc + _dot(p.astype(BF16), vvar[j][e])
            pairs.append(acc)
        o = jnp.concatenate(pairs, axis=1)
        o_ref[...] = o
        y_ref[...] = ((o * _rstd(o)) * gain_ref[...]).astype(BF16)

    prev = lambda b: (jnp.maximum(b - 1, 0), 0)
    return pl.pallas_call(
        body, name="attn_fwd", grid=(nb,),
        in_specs=[_rows(BLOCK, ATTN_WIDTH),
                  pl.BlockSpec((BLOCK, KV_WIDTH), prev), _rows(BLOCK, KV_WIDTH),
                  pl.BlockSpec((BLOCK, KV_WIDTH), prev), _rows(BLOCK, KV_WIDTH),
                  pl.BlockSpec(memory_space=pltpu.SMEM), _full((1, ATTN_WIDTH))],
        out_specs=[_rows(BLOCK, ATTN_WIDTH), _rows(BLOCK, ATTN_WIDTH)],
        out_shape=[jax.ShapeDtypeStruct((S, ATTN_WIDTH), F32), jax.ShapeDtypeStruct((S, ATTN_WIDTH), BF16)],
        compiler_params=_params(dimension_semantics=("arbitrary",)),
    )(q, k, k, v, v, sinks, g_attn)


def _attn_bwd(dy, o, q, k, v, sinks, g_attn):
    S = q.shape[0]
    nb = S // BLOCK

    def body(dy_ref, o_ref, q_ref, kp_ref, kc_ref, vp_ref, vc_ref, sinks_ref, gain_ref,
             dq_ref, dk_ref, dv_ref, dsink_ref, dgain_ref, kcarry_ref, vcarry_ref):
        b = pl.program_id(0)

        @pl.when(b == 0)
        def _():
            dsink_ref[...] = jnp.zeros_like(dsink_ref)
            dgain_ref[...] = jnp.zeros_like(dgain_ref)
            kcarry_ref[...] = jnp.zeros_like(kcarry_ref)
            vcarry_ref[...] = jnp.zeros_like(vcarry_ref)

        @pl.when(b < nb)
        def _():
            ov = o_ref[...]
            r = _rstd(ov)
            on = ov * r
            dyv = dy_ref[...]
            dgain_ref[...] += jnp.sum(dyv * on, axis=0, keepdims=True)
            do = _norm_bwd(dyv, on, r, gain_ref[...])
            kvar = _kv_variants(kp_ref, kc_ref)
            vvar = _kv_variants(vp_ref, vc_ref)
            valid = _band_mask(b)
            dk_acc = [[jnp.zeros((2 * BLOCK, _LANES), F32) for _ in range(2)] for _ in range(2)]
            dv_acc = [[jnp.zeros((2 * BLOCK, _LANES), F32) for _ in range(2)] for _ in range(2)]
            sink_rows = []
            for i in range(N_HEADS // 2):
                q_pair = q_ref[:, i * _LANES:(i + 1) * _LANES]
                do_pair = do[:, i * _LANES:(i + 1) * _LANES].astype(BF16)
                dq_pair = jnp.zeros((BLOCK, _LANES), F32)
                for e in range(2):
                    head = 2 * i + e
                    j = head // (N_HEADS // 2)
                    p, p_sink = _attn_probs(q_pair, kvar[j][e], valid, sinks_ref[0, head])
                    dp = _dot_nt(do_pair, vvar[j][e])
                    delta = jnp.sum(p * dp, axis=-1, keepdims=True)
                    ds = ((p * (dp - delta)) * ATTN_SCALE).astype(BF16)
                    sink_rows.append(jnp.zeros((1, _LANES), F32) - jnp.sum(p_sink * delta))
                    dq_pair = dq_pair + _dot(ds, kvar[j][e])
                    dk_acc[j][e] = dk_acc[j][e] + _dot_tn(ds, q_pair)
                    dv_acc[j][e] = dv_acc[j][e] + _dot_tn(p.astype(BF16), do_pair)
                dq_ref[:, i * _LANES:(i + 1) * _LANES] = dq_pair
            dsink_ref[...] += jnp.concatenate(sink_rows, axis=0)
            low = lax.broadcasted_iota(jnp.int32, (2 * BLOCK, _LANES), 1) < HEAD_DIM

            def merge(acc):
                return jnp.where(low, acc[0][0] + pltpu.roll(acc[0][1], HEAD_DIM, axis=1),
                                 acc[1][1] + pltpu.roll(acc[1][0], HEAD_DIM, axis=1))
            dk = merge(dk_acc)
            dv = merge(dv_acc)
            dk_ref[...] = kcarry_ref[...] + dk[:BLOCK, :]
            dv_ref[...] = vcarry_ref[...] + dv[:BLOCK, :]
            kcarry_ref[...] = dk[BLOCK:, :]
            vcarry_ref[...] = dv[BLOCK:, :]

        @pl.when(b == nb)
        def _():
            dk_ref[...] = kcarry_ref[...]
            dv_ref[...] = vcarry_ref[...]

    cur = lambda b: (jnp.minimum(b, nb - 1), 0)
    prev = lambda b: (jnp.clip(b - 1, 0, nb - 1), 0)
    return pl.pallas_call(
        body, name="attn_bwd", grid=(nb + 1,),
        in_specs=[pl.BlockSpec((BLOCK, ATTN_WIDTH), lambda b: (jnp.minimum(b, nb - 1), 1)),
                  pl.BlockSpec((BLOCK, ATTN_WIDTH), cur), pl.BlockSpec((BLOCK, ATTN_WIDTH), cur),
                  pl.BlockSpec((BLOCK, KV_WIDTH), prev), pl.BlockSpec((BLOCK, KV_WIDTH), cur),
                  pl.BlockSpec((BLOCK, KV_WIDTH), prev), pl.BlockSpec((BLOCK, KV_WIDTH), cur),
                  pl.BlockSpec(memory_space=pltpu.SMEM), _full((1, ATTN_WIDTH))],
        out_specs=[pl.BlockSpec((BLOCK, ATTN_WIDTH), cur),
                   pl.BlockSpec((BLOCK, KV_WIDTH), prev), pl.BlockSpec((BLOCK, KV_WIDTH), prev),
                   _full((N_HEADS, _LANES)), _full((1, ATTN_WIDTH))],
        out_shape=[jax.ShapeDtypeStruct((S, ATTN_WIDTH), F32), jax.ShapeDtypeStruct((S, KV_WIDTH), F32),
                   jax.ShapeDtypeStruct((S, KV_WIDTH), F32), jax.ShapeDtypeStruct((N_HEADS, _LANES), F32),
                   jax.ShapeDtypeStruct((1, ATTN_WIDTH), F32)],
        scratch_shapes=[pltpu.VMEM((BLOCK, KV_WIDTH), F32), pltpu.VMEM((BLOCK, KV_WIDTH), F32)],
        compiler_params=_params(dimension_semantics=("arbitrary",)),
    )(dy, o, q, k, k, v, v, sinks, g_attn)


def _mixer_out_fwd(y_pool, y_attn, x, w_out, mine, post, deps=()):
    S = x.shape[0]
    tm = TOKEN_TILE

    def body(yp_ref, ya_ref, x_ref, w_land, w_own, mine_ref, post_ref, out_ref, m_ref, y_ref, w_v, sem):
        _load_once(_gathered(w_land, w_own, w_v, mine_ref[0], rows=D_MODEL // N_SHARD), sem)
        y_ref[:, :POOL_WIDTH] = yp_ref[...]
        y_ref[:, POOL_WIDTH:] = ya_ref[...]
        m = _dot(y_ref[...], w_v[...])
        m_ref[...] = m
        out_ref[...] = x_ref[...] + (m * _rstd(m)) * post_ref[...]

    args = [y_pool, y_attn, x, *w_out, mine, post]
    return pl.pallas_call(
        _ignoring(body, len(args), len(deps)), name="mixer_out_fwd", grid=(S // tm,),
        in_specs=[_rows(tm, POOL_WIDTH), _rows(tm, ATTN_WIDTH), _rows(tm, D_MODEL), _ANY, _ANY, _SMEM,
                  _full((1, D_MODEL))] + [_ANY] * len(deps),
        out_specs=[_rows(tm, D_MODEL), _rows(tm, D_MODEL), _rows(tm, D_MODEL)],
        out_shape=[jax.ShapeDtypeStruct((S, D_MODEL), F32), jax.ShapeDtypeStruct((S, D_MODEL), F32),
                   jax.ShapeDtypeStruct((S, D_MODEL), BF16)],
        scratch_shapes=[pltpu.VMEM((D_MODEL, D_MODEL), BF16), pltpu.SemaphoreType.DMA((N_SHARD,))],
        compiler_params=_params(dimension_semantics=("arbitrary",)),
    )(*args, *deps)


def _mixer_out_bwd(dout, m, w_out, mine, post, deps=()):
    S = m.shape[0]
    tm = TOKEN_TILE

    def body(dout_ref, m_ref, w_land, w_own, mine_ref, post_ref, dy_ref, dm_ref, dpost_ref, w_v, sem):
        _load_once(_gathered(w_land, w_own, w_v, mine_ref[0], rows=D_MODEL // N_SHARD), sem)

        @pl.when(pl.program_id(0) == 0)
        def _():
            dpost_ref[...] = jnp.zeros_like(dpost_ref)

        mv = m_ref[...]
        r = _rstd(mv)
        mn = mv * r
        dv = dout_ref[...]
        dpost_ref[...] += jnp.sum(dv * mn, axis=0, keepdims=True)
        dm = _norm_bwd(dv, mn, r, post_ref[...]).astype(BF16)
        dm_ref[...] = dm
        dy_ref[...] = _dot_nt(dm, w_v[...])

    args = [dout, m, *w_out, mine, post]
    return pl.pallas_call(
        _ignoring(body, len(args), len(deps)), name="mixer_out_bwd", grid=(S // tm,),
        in_specs=[_rows(tm, D_MODEL), _rows(tm, D_MODEL), _ANY, _ANY, _SMEM, _full((1, D_MODEL))] + [_ANY] * len(deps),
        out_specs=[_rows(tm, D_MODEL), _rows(tm, D_MODEL), _full((1, D_MODEL))],
        out_shape=[jax.ShapeDtypeStruct((S, D_MODEL), F32), jax.ShapeDtypeStruct((S, D_MODEL), BF16),
                   jax.ShapeDtypeStruct((1, D_MODEL), F32)],
        scratch_shapes=[pltpu.VMEM((D_MODEL, D_MODEL), BF16), pltpu.SemaphoreType.DMA((N_SHARD,))],
        compiler_params=_params(dimension_semantics=("arbitrary",)),
    )(*args, *deps)


def _inv_freq_row():
    inv_freq = ROPE_THETA ** (-jnp.arange(0, ROT_DIM, 2, dtype=F32) / ROT_DIM)
    per_head = jnp.concatenate([inv_freq, inv_freq, jnp.zeros((HEAD_DIM - ROT_DIM,), F32)])
    return jnp.tile(per_head, _LANES // HEAD_DIM).reshape(1, _LANES)


def _local_step(x, pos, target, small, mine, weights_of, grads_ready):
    invf = _inv_freq_row()
    wgu1, wd1 = weights_of("ffn1", ())
    x1, g1, u1, f1 = _ffn_fwd(x, small["ffn1_pre"], small["ffn1_post"], wgu1, wd1, mine)
    w_in_t, w_out = weights_of("mixer", (x1,))
    u, q, k, v = _mixer_in_fwd(x1, small["mix_pre"], w_in_t, mine, pos, invf)
    y_pool = _pool_fwd(u, small["w_pool"], small["pool_scale"], small["g_pool"])
    o, y_attn = _attn_fwd(q, k, v, small["sinks"], small["g_attn"])
    x2, m, y = _mixer_out_fwd(y_pool, y_attn, x1, w_out, mine, small["mix_post"])
    wgu2, wd2 = weights_of("ffn2", (x2,))
    dx3, g2, u2, f2, loss_acc = _ffn_fwd(x2, small["ffn2_pre"], small["ffn2_post"], wgu2, wd2, mine, target=target)
    loss = loss_acc[0, 0] * (0.5 / D_MODEL)
    grads = {}
    dx2, h3, dgu2, a2, df2, grads["ffn2_pre"], grads["ffn2_post"] = _ffn_bwd(
        dx3, x2, f2, g2, u2, small["ffn2_pre"], small["ffn2_post"], wgu2, wd2, mine)
    dwgu2 = _wgrad(h3, dgu2, D_MODEL, FF_CHUNK, "wgrad_gu2", column_shards=True)
    dwd2 = _wgrad(a2, df2, FF_CHUNK, D_MODEL, "wgrad_down2")
    deps = grads_ready("ffn2", {"ffn2_w_gu": dwgu2, "ffn2_w_down": dwd2})
    dy, dm, grads["mix_post"] = _mixer_out_bwd(dx2, m, w_out, mine, small["mix_post"], deps=deps)
    dw_out = _wgrad(y, dm, D_MODEL, D_MODEL, "wgrad_out")
    dq, dk, dv, dsinks, grads["g_attn"] = _attn_bwd(dy, o, q, k, v, small["sinks"], small["g_attn"])
    grads["sinks"] = dsinks[:, 0].reshape(1, N_HEADS)
    du, grads["w_pool"], grads["pool_scale"], grads["g_pool"] = _pool_bwd(
        dy, u, small["w_pool"], small["pool_scale"], small["g_pool"])
    dx1, dz, h2, grads["mix_pre"] = _mixer_in_bwd(dx2, x1, small["mix_pre"], w_in_t, mine, du, dq, dk, dv, pos, invf)
    dw_in_t = _wgrad(dz, h2, IN_WIDTH, D_MODEL, "wgrad_in")
    deps = grads_ready("mixer", {"w_in": dw_in_t, "w_out": dw_out})
    dx, h1, dgu1, a1, df1, grads["ffn1_pre"], grads["ffn1_post"] = _ffn_bwd(
        dx1, x, f1, g1, u1, small["ffn1_pre"], small["ffn1_post"], wgu1, wd1, mine, deps=deps)
    dwgu1 = _wgrad(h1, dgu1, D_MODEL, FF_CHUNK, "wgrad_gu1", column_shards=True)
    deps = grads_ready("ffn1_gu", {"ffn1_w_gu": dwgu1})
    dwd1 = _wgrad(a1, df1, FF_CHUNK, D_MODEL, "wgrad_down1", deps=deps)
    grads_ready("ffn1_down", {"ffn1_w_down": dwd1})
    return loss, dx, grads


def _place():
    return lax.axis_index("x"), lax.axis_index("y"), lax.axis_index("c")


def _other_chips(x, y):
    return [(1 - x, y), (x, 1 - y), (1 - x, 1 - y)]


def _hbm_shape(shape, dtype):
    return jax.ShapeDtypeStruct(shape, dtype)


_HBM = pl.BlockSpec(memory_space=pltpu.HBM)
_SEM = pl.BlockSpec(memory_space=pltpu.SEMAPHORE)
_EFFECT = pltpu.SideEffectType.DATAFLOW_SIDE_EFFECTING
GATHER, REDUCE = "gather", "reduce"


def _in_hbm(a):
    return pltpu.with_memory_space_constraint(a, pltpu.HBM)


def _chip_copies(kind, srcs, lands, send_sems, recv_sems):
    x, y, c = _place()
    mine = 2 * x + y
    copies = []
    for w in range(len(srcs)):
        for k, (px, py) in enumerate(_other_chips(x, y)):
            if kind == GATHER:
                src, dst = srcs[w], lands[w].at[mine]
            else:
                src, dst = srcs[w].at[2 * px + py], lands[w].at[k]
            pair = (N_SHARD - 1) * w + k
            copies.append(pltpu.make_async_remote_copy(
                src_ref=src, dst_ref=dst, send_sem=send_sems.at[pair], recv_sem=recv_sems.at[pair],
                device_id=(px, py, c), device_id_type=MESH))
    return copies


def _landing_shape(kind, src):
    return ((N_SHARD,) + src.shape) if kind == GATHER else ((N_SHARD - 1,) + src.shape[1:])


def _exchange_start(kind, groups, name):
    sizes = [len(g) for g in groups]
    flat = [s for g in groups for s in g]
    n, ng = len(flat), len(groups)

    def body(*refs):
        srcs, lands = refs[:n], refs[n:2 * n]
        sems = refs[2 * n:2 * n + 2 * ng]
        token = refs[-1]
        start = 0
        for gi, size in enumerate(sizes):
            for cp in _chip_copies(kind, srcs[start:start + size], lands[start:start + size],
                                   sems[2 * gi], sems[2 * gi + 1]):
                cp.start()
            start += size
        token[...] = jnp.zeros_like(token)

    landings = [lax.empty(_landing_shape(kind, s), s.dtype) for s in flat]
    sem_shapes = [pltpu.SemaphoreType.DMA((size * (N_SHARD - 1),)) for size in sizes for _ in range(2)]
    outs = pl.pallas_call(
        body, name=name,
        in_specs=[_HBM] * (2 * n),
        out_specs=[_SEM] * (2 * ng) + [_HBM] * (2 * n) + [pl.BlockSpec(memory_space=pltpu.VMEM)],
        out_shape=sem_shapes + [pltpu.HBM(a.shape, a.dtype) for a in flat + landings]
        + [jax.ShapeDtypeStruct((8, _LANES), F32)],
        input_output_aliases={i: 2 * ng + i for i in range(2 * n)},
        compiler_params=pltpu.CompilerParams(has_side_effects=_EFFECT),
    )(*[_in_hbm(a) for a in flat + landings])
    sems, srcs, lands, token = outs[:2 * ng], outs[2 * ng:2 * ng + n], outs[2 * ng + n:2 * ng + 2 * n], outs[-1]
    handles, start = [], 0
    for gi, size in enumerate(sizes):
        handles.append((sems[2 * gi], sems[2 * gi + 1], srcs[start:start + size], lands[start:start + size]))
        start += size
    return handles, token


def _exchange_wait(kind, handle, after, name):
    send_sems, recv_sems, srcs, lands = handle
    n = len(srcs)

    def body(*refs):
        copies = _chip_copies(kind, refs[:n], refs[n:2 * n], refs[2 * n], refs[2 * n + 1])
        for cp in copies:
            cp.wait_send()
        for cp in copies:
            cp.wait_recv()

    outs = pl.pallas_call(
        body, name=name,
        in_specs=[_HBM] * (2 * n) + [_SEM, _SEM] + [_ANY] * len(after),
        out_specs=[_HBM] * (2 * n),
        out_shape=[pltpu.HBM(a.shape, a.dtype) for a in list(srcs) + list(lands)],
        input_output_aliases={i: i for i in range(2 * n)},
        compiler_params=pltpu.CompilerParams(has_side_effects=_EFFECT),
    )(*srcs, *lands, send_sems, recv_sems, *after)
    return outs[:n], outs[n:]


def _gather_slab(slab):
    def body(slab_ref, slab_all, local_sem, send_sems, recv_sems):
        x, y, c = _place()
        me = 4 * x + 2 * y + c
        own = pltpu.make_async_copy(slab_ref, slab_all.at[me], local_sem)
        own.start()
        copies, arrivals = [], []
        for k in range(1, 8):
            px, py, pc = x ^ (k >> 2), y ^ ((k >> 1) & 1), c ^ (k & 1)
            copies.append(pltpu.make_async_remote_copy(
                src_ref=slab_ref, dst_ref=slab_all.at[me], send_sem=send_sems.at[k - 1], recv_sem=recv_sems.at[k - 1],
                device_id=(px, py, pc), device_id_type=MESH))
            arrivals.append(pltpu.make_async_remote_copy(
                src_ref=slab_ref, dst_ref=slab_all.at[4 * px + 2 * py + pc], send_sem=send_sems.at[k - 1],
                recv_sem=recv_sems.at[k - 1], device_id=(px, py, pc), device_id_type=MESH))
        for cp in copies:
            cp.start()
        for cp in arrivals:
            cp.wait_recv()
        for cp in copies:
            cp.wait_send()
        own.wait()

    return pl.pallas_call(
        body, name="gather_slab", in_specs=[_ANY], out_specs=_ANY,
        out_shape=_hbm_shape((8,) + slab.shape, slab.dtype),
        scratch_shapes=[pltpu.SemaphoreType.DMA, pltpu.SemaphoreType.DMA((7,)), pltpu.SemaphoreType.DMA((7,))],
        compiler_params=pltpu.CompilerParams(has_side_effects=True),
    )(slab)


def _swap_with_sibling(partials, name):
    n = len(partials)

    def body(*refs):
        ins, outs = refs[:n], refs[n:2 * n]
        send_sems, recv_sems = refs[2 * n:]
        x, y, c = _place()
        sends = [pltpu.make_async_remote_copy(
            src_ref=ins[w], dst_ref=outs[w], send_sem=send_sems.at[w], recv_sem=recv_sems.at[w],
            device_id=(x, y, 1 - c), device_id_type=MESH) for w in range(n)]
        for cp in sends:
            cp.start()
        for cp in sends:
            cp.wait_recv()
        for cp in sends:
            cp.wait_send()

    return pl.pallas_call(
        body, name=name,
        in_specs=[_ANY] * n, out_specs=[_ANY] * n,
        out_shape=[_hbm_shape(p.shape, p.dtype) for p in partials],
        scratch_shapes=[pltpu.SemaphoreType.DMA((n,)), pltpu.SemaphoreType.DMA((n,))],
        compiler_params=pltpu.CompilerParams(has_side_effects=True),
    )(*partials)


def _row_block(rows, cap):
    best = None
    for cand in range(16, min(rows, cap) + 1, 16):
        if rows % cand == 0:
            best = cand
    assert best is not None, rows
    return best


def _chip_partial(own, received, shard, name):
    _, R, C = own.shape
    rb = _row_block(R, 512)

    def body(shard_ref, own_ref, rec_ref, out_ref):
        acc = own_ref[...]
        for k in range(3):
            acc = acc + rec_ref[k].astype(F32)
        out_ref[...] = acc

    return pl.pallas_call(
        body, name=name,
        grid_spec=pltpu.PrefetchScalarGridSpec(
            num_scalar_prefetch=1, grid=(R // rb,),
            in_specs=[pl.BlockSpec((None, rb, C), lambda i, s: (s[0], i, 0)),
                      pl.BlockSpec((3, rb, C), lambda i, s: (0, i, 0))],
            out_specs=pl.BlockSpec((rb, C), lambda i, s: (i, 0))),
        out_shape=jax.ShapeDtypeStruct((R, C), F32),
        compiler_params=_params(dimension_semantics=("arbitrary",)),
    )(shard, own, received)


def _adamw(w, m, v, g_parts, name):
    R, C = w.shape
    stacked = g_parts[0].ndim == 3
    rb = _row_block(R, 256) if R % 16 == 0 else R

    def body(w_ref, m_ref, v_ref, *refs):
        g_refs, (grad_ref, delta_ref, m_out, v_out) = refs[:-4], refs[-4:]
        if stacked:
            g = g_refs[0][0]
            for k in range(1, g_parts[0].shape[0]):
                g = g + g_refs[0][k]
        else:
            g = g_refs[0][...]
            for g_ref in g_refs[1:]:
                g = g + g_ref[...]
        grad_ref[...] = g
        new_m = ADAM_B1 * m_ref[...] + (1.0 - ADAM_B1) * g
        new_v = ADAM_B2 * v_ref[...] + (1.0 - ADAM_B2) * (g * g)
        m_hat = new_m / (1.0 - ADAM_B1 ** ADAM_STEP)
        v_hat = new_v / (1.0 - ADAM_B2 ** ADAM_STEP)
        delta_ref[...] = -ADAM_LR * (m_hat / (jnp.sqrt(v_hat) + ADAM_EPS) + ADAM_WD * w_ref[...])
        m_out[...] = new_m
        v_out[...] = new_v

    spec = pl.BlockSpec((rb, C), lambda i: (i, 0))
    g_specs = [pl.BlockSpec((g_parts[0].shape[0], rb, C), lambda i: (0, i, 0))] if stacked else [spec] * len(g_parts)
    return pl.pallas_call(
        body, name=name, grid=(R // rb,),
        in_specs=[spec, spec, spec] + g_specs,
        out_specs=[spec] * 4,
        out_shape=[jax.ShapeDtypeStruct((R, C), F32)] * 4,
        compiler_params=_params(dimension_semantics=("arbitrary",)),
    )(w, m, v, *g_parts)


SMALL_NAMES = ("ffn1_pre", "ffn1_post", "mix_pre", "pool_scale", "sinks", "g_pool", "g_attn", "mix_post",
               "ffn2_pre", "ffn2_post", "w_pool")
_SLAB_PART = 8 * _LANES


def _to_slab(parts):
    rows = []
    for name in SMALL_NAMES:
        flat = parts[name].reshape(-1)
        padded = -(-flat.shape[0] // _SLAB_PART) * _SLAB_PART
        rows.append(jnp.pad(flat, (0, padded - flat.shape[0])).reshape(-1, _LANES))
    return jnp.concatenate(rows, axis=0)


def _from_slab(slab, like):
    out, row = {}, 0
    for name in SMALL_NAMES:
        size = like[name].size
        rows = -(-size // _SLAB_PART) * (_SLAB_PART // _LANES)
        out[name] = slab[row:row + rows].reshape(-1)[:size].reshape(like[name].shape)
        row += rows
    return out


BIG_NAMES = ("ffn1_w_gu", "ffn1_w_down", "w_in", "w_out", "ffn2_w_gu", "ffn2_w_down")
WEIGHT_ORDER = ("ffn1_pre", "ffn1_w_gu", "ffn1_w_down", "ffn1_post", "mix_pre", "w_in", "w_pool", "pool_scale",
                "sinks", "g_pool", "g_attn", "w_out", "mix_post", "ffn2_pre", "ffn2_w_gu", "ffn2_w_down", "ffn2_post")


def kernel(x, positions, ffn1_pre, ffn1_w_gu, ffn1_w_down, ffn1_post, mix_pre, w_in, w_pool, pool_scale, sinks, g_pool, g_attn, w_out, mix_post, ffn2_pre, ffn2_w_gu, ffn2_w_down, ffn2_post, loss_target, m_ffn1_pre, m_ffn1_w_gu, m_ffn1_w_down, m_ffn1_post, m_mix_pre, m_w_in, m_w_pool, m_pool_scale, m_sinks, m_g_pool, m_g_attn, m_w_out, m_mix_post, m_ffn2_pre, m_ffn2_w_gu, m_ffn2_w_down, m_ffn2_post, v_ffn1_pre, v_ffn1_w_gu, v_ffn1_w_down, v_ffn1_post, v_mix_pre, v_w_in, v_w_pool, v_pool_scale, v_sinks, v_g_pool, v_g_attn, v_w_out, v_mix_post, v_ffn2_pre, v_ffn2_w_gu, v_ffn2_w_down, v_ffn2_post):
    given = dict(locals())
    weights = {n: given[n][0] for n in WEIGHT_ORDER}
    moments_m = {n: given["m_" + n][0] for n in WEIGHT_ORDER}
    moments_v = {n: given["v_" + n][0] for n in WEIGHT_ORDER}
    S = x.shape[1]
    shard = (2 * lax.axis_index("x") + lax.axis_index("y")).astype(jnp.int32).reshape(1)

    local16 = {n: weights[n].astype(BF16) for n in BIG_NAMES if n != "w_in"}
    local16["w_in"] = weights["w_in"].T.astype(BF16)
    gather_groups = {"ffn1": ("ffn1_w_gu", "ffn1_w_down"), "mixer": ("w_in", "w_out"),
                     "ffn2": ("ffn2_w_gu", "ffn2_w_down")}
    handles, _ = _exchange_start(GATHER, [[local16[n] for n in names] for names in gather_groups.values()],
                                 "gather_start")
    gather_handles = dict(zip(gather_groups, handles))

    def weights_of(group, after):
        owns, lands = _exchange_wait(GATHER, gather_handles[group], list(after), "gather_wait_" + group)
        return list(zip(lands, owns))

    pending, last_token = {}, []

    def grads_ready(group, grads):
        names = list(grads)
        (handle,), token = _exchange_start(REDUCE, [[grads[n][1] for n in names]], "reduce_start_" + group)
        pending[group] = (names, handle, [grads[n][0] for n in names])
        last_token[:] = [token]
        return [token]

    small = {n: (weights[n] if weights[n].ndim > 1 else weights[n].reshape(1, -1)) for n in SMALL_NAMES}
    loss, dx, small_grads = _local_step(x[0], positions.reshape(S, 1), loss_target[0], small, shard,
                                        weights_of, grads_ready)
    loss = lax.psum(loss, ("x", "y", "c"))

    grad, delta, new_m, new_v = {}, {}, {}, {}

    def finish(groups, after):
        names, partials = [], []
        for group in groups:
            group_names, handle, own32 = pending[group]
            _, received = _exchange_wait(REDUCE, handle, after, "reduce_wait_" + group)
            names += group_names
            partials += [_chip_partial(g32, rec, shard, "chip_partial_" + n)
                         for n, g32, rec in zip(group_names, own32, received)]
        siblings = _swap_with_sibling(partials, "swap_" + groups[0])
        for name, mine, theirs in zip(names, partials, siblings):
            if name == "w_in":
                mine, theirs = mine.T, theirs.T
            grad[name], delta[name], new_m[name], new_v[name] = _adamw(
                weights[name], moments_m[name], moments_v[name], [mine, theirs], "adamw_" + name)
        return [grad[names[-1]]]

    after = finish(["ffn2"], last_token)
    after = finish(["mixer"], after)
    finish(["ffn1_gu", "ffn1_down"], after)
    slab_all = _gather_slab(_to_slab(small_grads))
    small_like = {n: small[n] for n in SMALL_NAMES}
    slabs = _adamw(_to_slab(small), _to_slab({n: moments_m[n] for n in SMALL_NAMES}),
                   _to_slab({n: moments_v[n] for n in SMALL_NAMES}), [slab_all], "adamw_small")
    for store, slab in zip((grad, delta, new_m, new_v), slabs):
        store.update(_from_slab(slab, small_like))

    def out(store):
        return [store[n].reshape(given[n].shape) for n in WEIGHT_ORDER]
    return (loss, dx[None], *out(grad), *out(delta), *out(new_m), *out(new_v))
```

```python
import functools

import jax
import jax.numpy as jnp
from jax import lax
from jax.experimental import pallas as pl
from jax.experimental.pallas import tpu as pltpu

F32 = jnp.float32
BF16 = jnp.bfloat16

D_MODEL = 1024
D_FF = 2816
N_SHARD = 4
FF_CHUNK = D_FF // 2
POOL_WINDOWS = (2, 4, 8, 16)
POOL_WIDTH = 512
POOL_GROUP = 128
HALO = 16
HEAD_DIM = 64
N_HEADS = 8
ATTN_WIDTH = 512
KV_WIDTH = 128
IN_WIDTH = 1280
BLOCK = 128
ROT_DIM = 16
ROPE_THETA = 500000.0
EPS = 1e-6
NEG_INF = -1e30
ATTN_SCALE = HEAD_DIM ** -0.5

ADAM_LR = 0.001
ADAM_B1 = 0.9
ADAM_B2 = 0.999
ADAM_EPS = 1e-08
ADAM_WD = 0.01
ADAM_STEP = 10

VMEM_LIMIT = 60 * 1024 * 1024
FFN_FWD_TILE = 512
FFN_BWD_TILE = 256
MIXER_TILE = 512

MESH = pl.DeviceIdType.MESH


def _params(**kw):
    return pltpu.CompilerParams(vmem_limit_bytes=VMEM_LIMIT, **kw)


def _dot(a, b):
    return jnp.dot(a, b, preferred_element_type=F32)


def _dot_nt(a, b):
    return lax.dot_general(a, b, (((1,), (1,)), ((), ())), preferred_element_type=F32)


def _dot_tn(a, b):
    return lax.dot_general(a, b, (((0,), (0,)), ((), ())), preferred_element_type=F32)


def _rstd(x):
    return lax.rsqrt(jnp.mean(x * x, axis=-1, keepdims=True) + EPS)


def _norm_bwd(dy, xn, r, gain):
    dxn = dy * gain
    return r * (dxn - xn * jnp.mean(dxn * xn, axis=-1, keepdims=True))


def _sigmoid(x):
    return 1.0 / (1.0 + jnp.exp(-x))


def _full(shape):
    return pl.BlockSpec(shape, lambda *_: (0,) * len(shape))


def _rows(tile, width, col=0):
    return pl.BlockSpec((tile, width), lambda i: (i, col))


_ANY = pl.BlockSpec(memory_space=pl.ANY)


_SMEM = pl.BlockSpec(memory_space=pltpu.SMEM)


def _load_once(pairs, sem):
    @pl.when(pl.program_id(0) == 0)
    def _():
        copies = [pltpu.make_async_copy(src, dst, sem.at[n]) for n, (src, dst) in enumerate(pairs)]
        for cp in copies:
            cp.start()
        for cp in copies:
            cp.wait()


def _gathered(land_ref, own_ref, vmem_ref, mine, rows=None):
    def dst(slot):
        if rows is None:
            return vmem_ref.at[slot]
        return vmem_ref.at[pl.ds(pl.multiple_of(slot * rows, 16), rows), :]
    pairs = [(land_ref.at[(mine + d) % N_SHARD], dst((mine + d) % N_SHARD)) for d in range(1, N_SHARD)]
    return pairs + [(own_ref, dst(mine))]


def _ignoring(body, start, count):
    def wrapped(*refs):
        return body(*refs[:start], *refs[start + count:])
    return wrapped


def _ffn_fwd(x, pre, post, wgu, wd, mine, target=None, deps=()):
    S = x.shape[0]
    tm = FFN_FWD_TILE
    with_loss = target is not None

    def body(*refs):
        if with_loss:
            (x_ref, pre_ref, post_ref, wgu_land, wgu_own, wd_land, wd_own, mine_ref, tgt_ref,
             out_ref, g_ref, u_ref, f_ref, loss_ref, wgu_v, wd_v, sem) = refs
        else:
            (x_ref, pre_ref, post_ref, wgu_land, wgu_own, wd_land, wd_own, mine_ref,
             out_ref, g_ref, u_ref, f_ref, wgu_v, wd_v, sem) = refs
        _load_once(_gathered(wgu_land, wgu_own, wgu_v, mine_ref[0])
                   + _gathered(wd_land, wd_own, wd_v, mine_ref[0], rows=D_FF // N_SHARD), sem)
        xv = x_ref[...]
        h = ((xv * _rstd(xv)) * pre_ref[...]).astype(BF16)
        facc = jnp.zeros((tm, D_MODEL), F32)
        for c in range(2):
            cols = slice(c * FF_CHUNK, (c + 1) * FF_CHUNK)
            g = _dot(h, wgu_v[c])
            u = _dot(h, wgu_v[2 + c])
            g_ref[:, cols] = g.astype(BF16)
            u_ref[:, cols] = u.astype(BF16)
            a = (g * _sigmoid(g)) * u
            facc = facc + _dot(a.astype(BF16), wd_v[cols, :])
        f_ref[...] = facc
        out = xv + 0.5 * ((facc * _rstd(facc)) * post_ref[...])
        if with_loss:
            diff = out - tgt_ref[...]
            out_ref[...] = diff * (1.0 / D_MODEL)

            @pl.when(pl.program_id(0) == 0)
            def _():
                loss_ref[...] = jnp.zeros_like(loss_ref)
            loss_ref[...] += jnp.sum(diff * diff)
        else:
            out_ref[...] = out

    in_specs = [_rows(tm, D_MODEL), _full((1, D_MODEL)), _full((1, D_MODEL)), _ANY, _ANY, _ANY, _ANY, _SMEM]
    args = [x, pre, post, *wgu, *wd, mine]
    out_shape = [jax.ShapeDtypeStruct((S, D_MODEL), F32), jax.ShapeDtypeStruct((S, D_FF), BF16),
                 jax.ShapeDtypeStruct((S, D_FF), BF16), jax.ShapeDtypeStruct((S, D_MODEL), F32)]
    out_specs = [_rows(tm, D_MODEL), _rows(tm, D_FF), _rows(tm, D_FF), _rows(tm, D_MODEL)]
    if with_loss:
        in_specs.append(_rows(tm, D_MODEL))
        args.append(target)
        out_shape.append(jax.ShapeDtypeStruct((8, 128), F32))
        out_specs.append(_full((8, 128)))
    return pl.pallas_call(
        _ignoring(body, len(args), len(deps)), name="ffn_fwd_loss" if with_loss else "ffn_fwd",
        grid=(S // tm,), in_specs=in_specs + [_ANY] * len(deps), out_specs=out_specs, out_shape=out_shape,
        scratch_shapes=[pltpu.VMEM((N_SHARD, D_MODEL, FF_CHUNK), BF16), pltpu.VMEM((D_FF, D_MODEL), BF16),
                        pltpu.SemaphoreType.DMA((2 * N_SHARD,))],
        compiler_params=_params(dimension_semantics=("arbitrary",)),
    )(*args, *deps)


def _ffn_bwd(dout, x, f, g, u, pre, post, wgu, wd, mine, deps=()):
    S = x.shape[0]
    tm = FFN_BWD_TILE

    def body(dout_ref, x_ref, f_ref, g_ref, u_ref, pre_ref, post_ref, wgu_land, wgu_own, wd_land, wd_own, mine_ref,
             dx_ref, h_ref, dgu_ref, a_ref, df_ref, dpre_ref, dpost_ref, wgu_v, wd_v, sem):
        _load_once(_gathered(wgu_land, wgu_own, wgu_v, mine_ref[0])
                   + _gathered(wd_land, wd_own, wd_v, mine_ref[0], rows=D_FF // N_SHARD), sem)

        @pl.when(pl.program_id(0) == 0)
        def _():
            dpre_ref[...] = jnp.zeros_like(dpre_ref)
            dpost_ref[...] = jnp.zeros_like(dpost_ref)

        dout_v = dout_ref[...]
        dn = 0.5 * dout_v
        fv = f_ref[...]
        rf = _rstd(fv)
        fn = fv * rf
        dpost_ref[...] += jnp.sum(dn * fn, axis=0, keepdims=True)
        df = _norm_bwd(dn, fn, rf, post_ref[...]).astype(BF16)
        df_ref[...] = df
        dh = jnp.zeros((tm, D_MODEL), F32)
        for c in range(2):
            cols = slice(c * FF_CHUNK, (c + 1) * FF_CHUNK)
            da = _dot_nt(df, wd_v[cols, :])
            gv = g_ref[:, cols].astype(F32)
            uv = u_ref[:, cols].astype(F32)
            sg = _sigmoid(gv)
            silu = gv * sg
            a_ref[:, cols] = (silu * uv).astype(BF16)
            dg = ((da * uv) * (sg * (1.0 + gv * (1.0 - sg)))).astype(BF16)
            du = (da * silu).astype(BF16)
            dgu_ref[:, cols] = dg
            dgu_ref[:, 2 * FF_CHUNK + c * FF_CHUNK:2 * FF_CHUNK + (c + 1) * FF_CHUNK] = du
            dh = dh + _dot_nt(dg, wgu_v[c]) + _dot_nt(du, wgu_v[2 + c])
        xv = x_ref[...]
        rx = _rstd(xv)
        xn = xv * rx
        h_ref[...] = (xn * pre_ref[...]).astype(BF16)
        dpre_ref[...] += jnp.sum(dh * xn, axis=0, keepdims=True)
        dx_ref[...] = dout_v + _norm_bwd(dh, xn, rx, pre_ref[...])

    args = [dout, x, f, g, u, pre, post, *wgu, *wd, mine]
    return pl.pallas_call(
        _ignoring(body, len(args), len(deps)), name="ffn_bwd", grid=(S // tm,),
        in_specs=[_rows(tm, D_MODEL), _rows(tm, D_MODEL), _rows(tm, D_MODEL), _rows(tm, D_FF), _rows(tm, D_FF),
                  _full((1, D_MODEL)), _full((1, D_MODEL)), _ANY, _ANY, _ANY, _ANY, _SMEM] + [_ANY] * len(deps),
        out_specs=[_rows(tm, D_MODEL), _rows(tm, D_MODEL), _rows(tm, 2 * D_FF), _rows(tm, D_FF), _rows(tm, D_MODEL),
                   _full((1, D_MODEL)), _full((1, D_MODEL))],
        out_shape=[jax.ShapeDtypeStruct((S, D_MODEL), F32), jax.ShapeDtypeStruct((S, D_MODEL), BF16),
                   jax.ShapeDtypeStruct((S, 2 * D_FF), BF16), jax.ShapeDtypeStruct((S, D_FF), BF16),
                   jax.ShapeDtypeStruct((S, D_MODEL), BF16),
                   jax.ShapeDtypeStruct((1, D_MODEL), F32), jax.ShapeDtypeStruct((1, D_MODEL), F32)],
        scratch_shapes=[pltpu.VMEM((N_SHARD, D_MODEL, FF_CHUNK), BF16), pltpu.VMEM((D_FF, D_MODEL), BF16),
                        pltpu.SemaphoreType.DMA((2 * N_SHARD,))],
        compiler_params=_params(dimension_semantics=("arbitrary",)),
    )(*args, *deps)


def _wgrad(lhs, rhs, m_block, n_block, name, column_shards=False, tk=512, deps=()):
    S, M = lhs.shape
    N = rhs.shape[1]
    k_steps = S // tk

    def body(lhs_ref, rhs_ref, out_ref, out16_ref):
        k = pl.program_id(2)

        @pl.when(k == 0)
        def _():
            out_ref[...] = jnp.zeros_like(out_ref)
        out_ref[...] += _dot_tn(lhs_ref[...], rhs_ref[...])

        @pl.when(k == k_steps - 1)
        def _():
            out16_ref[...] = out_ref[...].astype(BF16)

    if column_shards:
        assert N == N_SHARD * n_block
        shape = (N_SHARD, M, n_block)
        out_spec = pl.BlockSpec((None, m_block, n_block), lambda i, j, k: (j, i, 0))
    else:
        shape = (M, N)
        out_spec = pl.BlockSpec((m_block, n_block), lambda i, j, k: (i, j))
    out, out16 = pl.pallas_call(
        _ignoring(body, 2, len(deps)), name=name, grid=(M // m_block, N // n_block, k_steps),
        in_specs=[pl.BlockSpec((tk, m_block), lambda i, j, k: (k, i)),
                  pl.BlockSpec((tk, n_block), lambda i, j, k: (k, j))] + [_ANY] * len(deps),
        out_specs=[out_spec, out_spec],
        out_shape=[jax.ShapeDtypeStruct(shape, F32), jax.ShapeDtypeStruct(shape, BF16)],
        compiler_params=_params(dimension_semantics=("arbitrary", "arbitrary", "arbitrary")),
    )(lhs, rhs, *deps)
    if not column_shards:
        out = out.reshape(N_SHARD, M // N_SHARD, N)
        out16 = out16.reshape(N_SHARD, M // N_SHARD, N)
    return out, out16


def _rope_tables(pos_ref, invf_ref):
    ang = pos_ref[...].astype(F32) * invf_ref[...]
    cos, sin = jnp.cos(ang), jnp.sin(ang)
    lane = lax.broadcasted_iota(jnp.int32, ang.shape, 1) % HEAD_DIM
    first = lane < ROT_DIM // 2
    second = (lane >= ROT_DIM // 2) & (lane < ROT_DIM)
    c = jnp.where(lane < ROT_DIM, cos, 1.0)
    s_first = jnp.where(first, sin, 0.0)
    s_second = jnp.where(second, sin, 0.0)
    return c, s_first, s_second


_HALF = ROT_DIM // 2
_LANES = 128


def _rope(t, tables):
    c, s_first, s_second = tables
    return t * c - pltpu.roll(t, _LANES - _HALF, axis=1) * s_first + pltpu.roll(t, _HALF, axis=1) * s_second


def _rope_transposed(t, tables):
    c, s_first, s_second = tables
    return t * c - pltpu.roll(t * s_first, _HALF, axis=1) + pltpu.roll(t * s_second, _LANES - _HALF, axis=1)


def _mixer_in_fwd(x, pre, w_in_t, mine, pos, invf, deps=()):
    S = x.shape[0]
    tm = MIXER_TILE

    def body(x_ref, pre_ref, w_land, w_own, mine_ref, pos_ref, invf_ref, u_ref, q_ref, k_ref, v_ref, w_v, sem):
        _load_once(_gathered(w_land, w_own, w_v, mine_ref[0], rows=IN_WIDTH // N_SHARD), sem)
        xv = x_ref[...]
        h = ((xv * _rstd(xv)) * pre_ref[...]).astype(BF16)
        z = _dot_nt(h, w_v[...])
        tables = _rope_tables(pos_ref, invf_ref)
        u_ref[...] = z[:, :POOL_WIDTH]
        for t in range(ATTN_WIDTH // _LANES):
            lo = POOL_WIDTH + t * _LANES
            q_ref[:, t * _LANES:(t + 1) * _LANES] = _rope(z[:, lo:lo + _LANES], tables).astype(BF16)
        kv = POOL_WIDTH + ATTN_WIDTH
        k_ref[...] = _rope(z[:, kv:kv + KV_WIDTH], tables).astype(BF16)
        v_ref[...] = z[:, kv + KV_WIDTH:].astype(BF16)

    args = [x, pre, *w_in_t, mine, pos, invf]
    return pl.pallas_call(
        _ignoring(body, len(args), len(deps)), name="mixer_in_fwd", grid=(S // tm,),
        in_specs=[_rows(tm, D_MODEL), _full((1, D_MODEL)), _ANY, _ANY, _SMEM, _rows(tm, 1), _full((1, _LANES))]
        + [_ANY] * len(deps),
        out_specs=[_rows(tm, POOL_WIDTH), _rows(tm, ATTN_WIDTH), _rows(tm, KV_WIDTH), _rows(tm, KV_WIDTH)],
        out_shape=[jax.ShapeDtypeStruct((S, POOL_WIDTH), F32), jax.ShapeDtypeStruct((S, ATTN_WIDTH), BF16),
                   jax.ShapeDtypeStruct((S, KV_WIDTH), BF16), jax.ShapeDtypeStruct((S, KV_WIDTH), BF16)],
        scratch_shapes=[pltpu.VMEM((IN_WIDTH, D_MODEL), BF16), pltpu.SemaphoreType.DMA((N_SHARD,))],
        compiler_params=_params(dimension_semantics=("arbitrary",)),
    )(*args, *deps)


def _mixer_in_bwd(dres, x, pre, w_in_t, mine, du, dq, dk, dv, pos, invf, deps=()):
    S = x.shape[0]
    tm = MIXER_TILE

    def body(dres_ref, x_ref, pre_ref, w_land, w_own, mine_ref, du_ref, dq_ref, dk_ref, dv_ref, pos_ref, invf_ref,
             dx_ref, dz_ref, h_ref, dpre_ref, w_v, sem):
        _load_once(_gathered(w_land, w_own, w_v, mine_ref[0], rows=IN_WIDTH // N_SHARD), sem)

        @pl.when(pl.program_id(0) == 0)
        def _():
            dpre_ref[...] = jnp.zeros_like(dpre_ref)

        tables = _rope_tables(pos_ref, invf_ref)
        dz_ref[:, :POOL_WIDTH] = du_ref[...].astype(BF16)
        for t in range(ATTN_WIDTH // _LANES):
            lo = POOL_WIDTH + t * _LANES
            dz_ref[:, lo:lo + _LANES] = _rope_transposed(dq_ref[:, t * _LANES:(t + 1) * _LANES], tables).astype(BF16)
        kv = POOL_WIDTH + ATTN_WIDTH
        dz_ref[:, kv:kv + KV_WIDTH] = _rope_transposed(dk_ref[...], tables).astype(BF16)
        dz_ref[:, kv + KV_WIDTH:] = dv_ref[...].astype(BF16)
        dh = _dot(dz_ref[...], w_v[...])
        xv = x_ref[...]
        rx = _rstd(xv)
        xn = xv * rx
        h_ref[...] = (xn * pre_ref[...]).astype(BF16)
        dpre_ref[...] += jnp.sum(dh * xn, axis=0, keepdims=True)
        dx_ref[...] = dres_ref[...] + _norm_bwd(dh, xn, rx, pre_ref[...])

    args = [dres, x, pre, *w_in_t, mine, du, dq, dk, dv, pos, invf]
    return pl.pallas_call(
        _ignoring(body, len(args), len(deps)), name="mixer_in_bwd", grid=(S // tm,),
        in_specs=[_rows(tm, D_MODEL), _rows(tm, D_MODEL), _full((1, D_MODEL)), _ANY, _ANY, _SMEM,
                  _rows(tm, POOL_WIDTH), _rows(tm, ATTN_WIDTH), _rows(tm, KV_WIDTH), _rows(tm, KV_WIDTH),
                  _rows(tm, 1), _full((1, _LANES))] + [_ANY] * len(deps),
        out_specs=[_rows(tm, D_MODEL), _rows(tm, IN_WIDTH), _rows(tm, D_MODEL), _full((1, D_MODEL))],
        out_shape=[jax.ShapeDtypeStruct((S, D_MODEL), F32), jax.ShapeDtypeStruct((S, IN_WIDTH), BF16),
                   jax.ShapeDtypeStruct((S, D_MODEL), BF16), jax.ShapeDtypeStruct((1, D_MODEL), F32)],
        scratch_shapes=[pltpu.VMEM((IN_WIDTH, D_MODEL), BF16), pltpu.SemaphoreType.DMA((N_SHARD,))],
        compiler_params=_params(dimension_semantics=("arbitrary",)),
    )(*args, *deps)


def _pool_counts(tile_index, tm, width):
    t = tile_index * tm + lax.broadcasted_iota(jnp.int32, (tm, 1), 0)
    return jnp.minimum(t + 1, width).astype(F32)


def _pool_features(ext, u_tile, tile_index, tm):
    ds = []
    for gi, width in enumerate(POOL_WINDOWS):
        lanes = slice(gi * POOL_GROUP, (gi + 1) * POOL_GROUP)
        s = ext[:, lanes]
        shift = 1
        while shift < width:
            s = s + pltpu.roll(s, shift, axis=0)
            shift *= 2
        ds.append(s[HALO:, :] / _pool_counts(tile_index, tm, width) - u_tile[:, lanes])
    return ds


def _pool_fwd(u, w_pool, pool_scale, g_pool):
    S = u.shape[0]
    tm = MIXER_TILE

    def body(u_ref, w_ref, scale_ref, gain_ref, y_ref, ext_ref):
        i = pl.program_id(0)

        @pl.when(i == 0)
        def _():
            ext_ref[:HALO, :] = jnp.zeros((HALO, POOL_WIDTH), F32)

        u_tile = u_ref[...]
        ext_ref[HALO:, :] = u_tile
        ds = _pool_features(ext_ref[...], u_tile, i, tm)
        ext_ref[:HALO, :] = u_tile[tm - HALO:, :]
        ys = [_dot(ds[gi].astype(BF16), w_ref[gi].astype(BF16)) for gi in range(len(POOL_WINDOWS))]
        po = jnp.concatenate(ys, axis=1) * scale_ref[...]
        y_ref[...] = ((po * _rstd(po)) * gain_ref[...]).astype(BF16)

    return pl.pallas_call(
        body, name="pool_fwd", grid=(S // tm,),
        in_specs=[_rows(tm, POOL_WIDTH), _full((len(POOL_WINDOWS), POOL_GROUP, POOL_GROUP)),
                  _full((1, POOL_WIDTH)), _full((1, POOL_WIDTH))],
        out_specs=_rows(tm, POOL_WIDTH),
        out_shape=jax.ShapeDtypeStruct((S, POOL_WIDTH), BF16),
        scratch_shapes=[pltpu.VMEM((HALO + tm, POOL_WIDTH), F32)],
        compiler_params=_params(dimension_semantics=("arbitrary",)),
    )(u, w_pool, pool_scale, g_pool)


def _pool_bwd(dy, u, w_pool, pool_scale, g_pool):
    S = u.shape[0]
    tm = MIXER_TILE
    n_tiles = S // tm
    halo_blocks = tm // HALO

    def body(dy_ref, u_ref, uprev_ref, w_ref, scale_ref, gain_ref,
             du_ref, dw_ref, dscale_ref, dgain_ref, ext_ref, nxt_ref):
        i = pl.program_id(0)
        tile = n_tiles - 1 - i

        @pl.when(i == 0)
        def _():
            dw_ref[...] = jnp.zeros_like(dw_ref)
            dscale_ref[...] = jnp.zeros_like(dscale_ref)
            dgain_ref[...] = jnp.zeros_like(dgain_ref)
            nxt_ref[...] = jnp.zeros_like(nxt_ref)

        u_tile = u_ref[...]
        ext_ref[:HALO, :] = jnp.where(tile > 0, uprev_ref[...], 0.0)
        ext_ref[HALO:, :] = u_tile
        ds = _pool_features(ext_ref[...], u_tile, tile, tm)
        dsb = [d.astype(BF16) for d in ds]
        wb = [w_ref[gi].astype(BF16) for gi in range(len(POOL_WINDOWS))]
        yraw = jnp.concatenate([_dot(dsb[gi], wb[gi]) for gi in range(len(POOL_WINDOWS))], axis=1)
        po = yraw * scale_ref[...]
        r = _rstd(po)
        pn = po * r
        dyv = dy_ref[...]
        dgain_ref[...] += jnp.sum(dyv * pn, axis=0, keepdims=True)
        dpo = _norm_bwd(dyv, pn, r, gain_ref[...])
        dscale_ref[...] += jnp.sum(dpo * yraw, axis=0, keepdims=True)
        dyraw = (dpo * scale_ref[...]).astype(BF16)
        for gi, width in enumerate(POOL_WINDOWS):
            lanes = slice(gi * POOL_GROUP, (gi + 1) * POOL_GROUP)
            dw_ref[gi] += _dot_tn(dsb[gi], dyraw[:, lanes])
            dd = _dot_nt(dyraw[:, lanes], wb[gi])
            ddc = dd / _pool_counts(tile, tm, width)
            ext_ref[:tm, lanes] = ddc
            ext_ref[tm:, lanes] = nxt_ref[:, lanes]
            s = ext_ref[:, lanes]
            shift = 1
            while shift < width:
                s = s + pltpu.roll(s, HALO + tm - shift, axis=0)
                shift *= 2
            du_ref[:, lanes] = s[:tm, :] - dd
            nxt_ref[:, lanes] = ddc[:HALO, :]

    return pl.pallas_call(
        body, name="pool_bwd", grid=(n_tiles,),
        in_specs=[pl.BlockSpec((tm, POOL_WIDTH), lambda i: (n_tiles - 1 - i, 0)),
                  pl.BlockSpec((tm, POOL_WIDTH), lambda i: (n_tiles - 1 - i, 0)),
                  pl.BlockSpec((HALO, POOL_WIDTH), lambda i: (jnp.maximum((n_tiles - 1 - i) * halo_blocks - 1, 0), 0)),
                  _full((len(POOL_WINDOWS), POOL_GROUP, POOL_GROUP)), _full((1, POOL_WIDTH)), _full((1, POOL_WIDTH))],
        out_specs=[pl.BlockSpec((tm, POOL_WIDTH), lambda i: (n_tiles - 1 - i, 0)),
                   _full((len(POOL_WINDOWS), POOL_GROUP, POOL_GROUP)), _full((1, POOL_WIDTH)), _full((1, POOL_WIDTH))],
        out_shape=[jax.ShapeDtypeStruct((S, POOL_WIDTH), F32),
                   jax.ShapeDtypeStruct((len(POOL_WINDOWS), POOL_GROUP, POOL_GROUP), F32),
                   jax.ShapeDtypeStruct((1, POOL_WIDTH), F32), jax.ShapeDtypeStruct((1, POOL_WIDTH), F32)],
        scratch_shapes=[pltpu.VMEM((HALO + tm, POOL_WIDTH), F32), pltpu.VMEM((HALO, POOL_WIDTH), F32)],
        compiler_params=_params(dimension_semantics=("arbitrary",)),
    )(dy, u, u, w_pool, pool_scale, g_pool)


def _kv_variants(prev_ref, cur_ref):
    cat = jnp.concatenate([prev_ref[...], cur_ref[...]], axis=0).astype(F32)
    rolled = pltpu.roll(cat, HEAD_DIM, axis=1)
    low = lax.broadcasted_iota(jnp.int32, cat.shape, 1) < HEAD_DIM
    zero = jnp.zeros_like(cat)
    pick = lambda src, keep_low: jnp.where(low if keep_low else ~low, src, zero).astype(BF16)
    return [[pick(cat, True), pick(rolled, False)], [pick(rolled, True), pick(cat, False)]]


def _band_mask(block_index):
    qi = lax.broadcasted_iota(jnp.int32, (BLOCK, 2 * BLOCK), 0)
    kj = lax.broadcasted_iota(jnp.int32, (BLOCK, 2 * BLOCK), 1)
    first_key = jnp.where(block_index > 0, 0, BLOCK)
    return (kj > qi) & (kj <= qi + BLOCK) & (kj >= first_key)


def _attn_probs(q_pair, k_var, valid, sink):
    s = _dot_nt(q_pair, k_var) * ATTN_SCALE
    s = jnp.where(valid, s, NEG_INF)
    m = jnp.maximum(jnp.max(s, axis=-1, keepdims=True), sink)
    p = jnp.exp(s - m)
    p_sink = jnp.exp(sink - m)
    inv = 1.0 / (jnp.sum(p, axis=-1, keepdims=True) + p_sink)
    return p * inv, p_sink * inv


def _attn_fwd(q, k, v, sinks, g_attn):
    S = q.shape[0]
    nb = S // BLOCK

    def body(q_ref, kp_ref, kc_ref, vp_ref, vc_ref, sinks_ref, gain_ref, o_ref, y_ref):
        b = pl.program_id(0)
        kvar = _kv_variants(kp_ref, kc_ref)
        vvar = _kv_variants(vp_ref, vc_ref)
        valid = _band_mask(b)
        pairs = []
        for i in range(N_HEADS // 2):
            q_pair = q_ref[:, i * _LANES:(i + 1) * _LANES]
            acc = jnp.zeros((BLOCK, _LANES), F32)
            for e in range(2):
                head = 2 * i + e
                j = head // (N_HEADS // 2)
                p, _ = _attn_probs(q_pair, kvar[j][e], valid, sinks_ref[0, head])
                acc = acc + _dot(p.astype(BF16), vvar[j][e])
            pairs.append(acc)
        o = jnp.concatenate(pairs, axis=1)
        o_ref[...] = o
        y_ref[...] = ((o * _rstd(o)) * gain_ref[...]).astype(BF16)

    prev = lambda b: (jnp.maximum(b - 1, 0), 0)
    return pl.pallas_call(
        body, name="attn_fwd", grid=(nb,),
        in_specs=[_rows(BLOCK, ATTN_WIDTH),
                  pl.BlockSpec((BLOCK, KV_WIDTH), prev), _rows(BLOCK, KV_WIDTH),
                  pl.BlockSpec((BLOCK, KV_WIDTH), prev), _rows(BLOCK, KV_WIDTH),
                  pl.BlockSpec(memory_space=pltpu.SMEM), _full((1, ATTN_WIDTH))],
        out_specs=[_rows(BLOCK, ATTN_WIDTH), _rows(BLOCK, ATTN_WIDTH)],
        out_shape=[jax.ShapeDtypeStruct((S, ATTN_WIDTH), F32), jax.ShapeDtypeStruct((S, ATTN_WIDTH), BF16)],
        compiler_params=_params(dimension_semantics=("arbitrary",)),
    )(q, k, k, v, v, sinks, g_attn)


def _attn_bwd(dy, o, q, k, v, sinks, g_attn):
    S = q.shape[0]
    nb = S // BLOCK

    def body(dy_ref, o_ref, q_ref, kp_ref, kc_ref, vp_ref, vc_ref, sinks_ref, gain_ref,
             dq_ref, dk_ref, dv_ref, dsink_ref, dgain_ref, kcarry_ref, vcarry_ref):
        b = pl.program_id(0)

        @pl.when(b == 0)
        def _():
            dsink_ref[...] = jnp.zeros_like(dsink_ref)
            dgain_ref[...] = jnp.zeros_like(dgain_ref)
            kcarry_ref[...] = jnp.zeros_like(kcarry_ref)
            vcarry_ref[...] = jnp.zeros_like(vcarry_ref)

        @pl.when(b < nb)
        def _():
            ov = o_ref[...]
            r = _rstd(ov)
            on = ov * r
            dyv = dy_ref[...]
            dgain_ref[...] += jnp.sum(dyv * on, axis=0, keepdims=True)
            do = _norm_bwd(dyv, on, r, gain_ref[...])
            kvar = _kv_variants(kp_ref, kc_ref)
            vvar = _kv_variants(vp_ref, vc_ref)
            valid = _band_mask(b)
            dk_acc = [[jnp.zeros((2 * BLOCK, _LANES), F32) for _ in range(2)] for _ in range(2)]
            dv_acc = [[jnp.zeros((2 * BLOCK, _LANES), F32) for _ in range(2)] for _ in range(2)]
            sink_rows = []
            for i in range(N_HEADS // 2):
                q_pair = q_ref[:, i * _LANES:(i + 1) * _LANES]
                do_pair = do[:, i * _LANES:(i + 1) * _LANES].astype(BF16)
                dq_pair = jnp.zeros((BLOCK, _LANES), F32)
                for e in range(2):
                    head = 2 * i + e
                    j = head // (N_HEADS // 2)
                    p, p_sink = _attn_probs(q_pair, kvar[j][e], valid, sinks_ref[0, head])
                    dp = _dot_nt(do_pair, vvar[j][e])
                    delta = jnp.sum(p * dp, axis=-1, keepdims=True)
                    ds = ((p * (dp - delta)) * ATTN_SCALE).astype(BF16)
                    sink_rows.append(jnp.zeros((1, _LANES), F32) - jnp.sum(p_sink * delta))
                    dq_pair = dq_pair + _dot(ds, kvar[j][e])
                    dk_acc[j][e] = dk_acc[j][e] + _dot_tn(ds, q_pair)
                    dv_acc[j][e] = dv_acc[j][e] + _dot_tn(p.astype(BF16), do_pair)
                dq_ref[:, i * _LANES:(i + 1) * _LANES] = dq_pair
            dsink_ref[...] += jnp.concatenate(sink_rows, axis=0)
            low = lax.broadcasted_iota(jnp.int32, (2 * BLOCK, _LANES), 1) < HEAD_DIM

            def merge(acc):
                return jnp.where(low, acc[0][0] + pltpu.roll(acc[0][1], HEAD_DIM, axis=1),
                                 acc[1][1] + pltpu.roll(acc[1][0], HEAD_DIM, axis=1))
            dk = merge(dk_acc)
            dv = merge(dv_acc)
            dk_ref[...] = kcarry_ref[...] + dk[:BLOCK, :]
            dv_ref[...] = vcarry_ref[...] + dv[:BLOCK, :]
            kcarry_ref[...] = dk[BLOCK:, :]
            vcarry_ref[...] = dv[BLOCK:, :]

        @pl.when(b == nb)
        def _():
            dk_ref[...] = kcarry_ref[...]
            dv_ref[...] = vcarry_ref[...]

    cur = lambda b: (jnp.minimum(b, nb - 1), 0)
    prev = lambda b: (jnp.clip(b - 1, 0, nb - 1), 0)
    return pl.pallas_call(
        body, name="attn_bwd", grid=(nb + 1,),
        in_specs=[pl.BlockSpec((BLOCK, ATTN_WIDTH), lambda b: (jnp.minimum(b, nb - 1), 1)),
                  pl.BlockSpec((BLOCK, ATTN_WIDTH), cur), pl.BlockSpec((BLOCK, ATTN_WIDTH), cur),
                  pl.BlockSpec((BLOCK, KV_WIDTH), prev), pl.BlockSpec((BLOCK, KV_WIDTH), cur),
                  pl.BlockSpec((BLOCK, KV_WIDTH), prev), pl.BlockSpec((BLOCK, KV_WIDTH), cur),
                  pl.BlockSpec(memory_space=pltpu.SMEM), _full((1, ATTN_WIDTH))],
        out_specs=[pl.BlockSpec((BLOCK, ATTN_WIDTH), cur),
                   pl.BlockSpec((BLOCK, KV_WIDTH), prev), pl.BlockSpec((BLOCK, KV_WIDTH), prev),
                   _full((N_HEADS, _LANES)), _full((1, ATTN_WIDTH))],
        out_shape=[jax.ShapeDtypeStruct((S, ATTN_WIDTH), F32), jax.ShapeDtypeStruct((S, KV_WIDTH), F32),
                   jax.ShapeDtypeStruct((S, KV_WIDTH), F32), jax.ShapeDtypeStruct((N_HEADS, _LANES), F32),
                   jax.ShapeDtypeStruct((1, ATTN_WIDTH), F32)],
        scratch_shapes=[pltpu.VMEM((BLOCK, KV_WIDTH), F32), pltpu.VMEM((BLOCK, KV_WIDTH), F32)],
        compiler_params=_params(dimension_semantics=("arbitrary",)),
    )(dy, o, q, k, k, v, v, sinks, g_attn)


def _mixer_out_fwd(y_pool, y_attn, x, w_out, mine, post, deps=()):
    S = x.shape[0]
    tm = MIXER_TILE

    def body(yp_ref, ya_ref, x_ref, w_land, w_own, mine_ref, post_ref, out_ref, m_ref, y_ref, w_v, sem):
        _load_once(_gathered(w_land, w_own, w_v, mine_ref[0], rows=D_MODEL // N_SHARD), sem)
        y_ref[:, :POOL_WIDTH] = yp_ref[...]
        y_ref[:, POOL_WIDTH:] = ya_ref[...]
        m = _dot(y_ref[...], w_v[...])
        m_ref[...] = m
        out_ref[...] = x_ref[...] + (m * _rstd(m)) * post_ref[...]

    args = [y_pool, y_attn, x, *w_out, mine, post]
    return pl.pallas_call(
        _ignoring(body, len(args), len(deps)), name="mixer_out_fwd", grid=(S // tm,),
        in_specs=[_rows(tm, POOL_WIDTH), _rows(tm, ATTN_WIDTH), _rows(tm, D_MODEL), _ANY, _ANY, _SMEM,
                  _full((1, D_MODEL))] + [_ANY] * len(deps),
        out_specs=[_rows(tm, D_MODEL), _rows(tm, D_MODEL), _rows(tm, D_MODEL)],
        out_shape=[jax.ShapeDtypeStruct((S, D_MODEL), F32), jax.ShapeDtypeStruct((S, D_MODEL), F32),
                   jax.ShapeDtypeStruct((S, D_MODEL), BF16)],
        scratch_shapes=[pltpu.VMEM((D_MODEL, D_MODEL), BF16), pltpu.SemaphoreType.DMA((N_SHARD,))],
        compiler_params=_params(dimension_semantics=("arbitrary",)),
    )(*args, *deps)


def _mixer_out_bwd(dout, m, w_out, mine, post, deps=()):
    S = m.shape[0]
    tm = MIXER_TILE

    def body(dout_ref, m_ref, w_land, w_own, mine_ref, post_ref, dy_ref, dm_ref, dpost_ref, w_v, sem):
        _load_once(_gathered(w_land, w_own, w_v, mine_ref[0], rows=D_MODEL // N_SHARD), sem)

        @pl.when(pl.program_id(0) == 0)
        def _():
            dpost_ref[...] = jnp.zeros_like(dpost_ref)

        mv = m_ref[...]
        r = _rstd(mv)
        mn = mv * r
        dv = dout_ref[...]
        dpost_ref[...] += jnp.sum(dv * mn, axis=0, keepdims=True)
        dm = _norm_bwd(dv, mn, r, post_ref[...]).astype(BF16)
        dm_ref[...] = dm
        dy_ref[...] = _dot_nt(dm, w_v[...])

    args = [dout, m, *w_out, mine, post]
    return pl.pallas_call(
        _ignoring(body, len(args), len(deps)), name="mixer_out_bwd", grid=(S // tm,),
        in_specs=[_rows(tm, D_MODEL), _rows(tm, D_MODEL), _ANY, _ANY, _SMEM, _full((1, D_MODEL))] + [_ANY] * len(deps),
        out_specs=[_rows(tm, D_MODEL), _rows(tm, D_MODEL), _full((1, D_MODEL))],
        out_shape=[jax.ShapeDtypeStruct((S, D_MODEL), F32), jax.ShapeDtypeStruct((S, D_MODEL), BF16),
                   jax.ShapeDtypeStruct((1, D_MODEL), F32)],
        scratch_shapes=[pltpu.VMEM((D_MODEL, D_MODEL), BF16), pltpu.SemaphoreType.DMA((N_SHARD,))],
        compiler_params=_params(dimension_semantics=("arbitrary",)),
    )(*args, *deps)


def _inv_freq_row():
    inv_freq = ROPE_THETA ** (-jnp.arange(0, ROT_DIM, 2, dtype=F32) / ROT_DIM)
    per_head = jnp.concatenate([inv_freq, inv_freq, jnp.zeros((HEAD_DIM - ROT_DIM,), F32)])
    return jnp.tile(per_head, _LANES // HEAD_DIM).reshape(1, _LANES)


def _local_step(x, pos, target, small, mine, weights_of, grads_ready):
    invf = _inv_freq_row()
    wgu1, wd1 = weights_of("ffn1", ())
    x1, g1, u1, f1 = _ffn_fwd(x, small["ffn1_pre"], small["ffn1_post"], wgu1, wd1, mine)
    w_in_t, w_out = weights_of("mixer", (x1,))
    u, q, k, v = _mixer_in_fwd(x1, small["mix_pre"], w_in_t, mine, pos, invf)
    y_pool = _pool_fwd(u, small["w_pool"], small["pool_scale"], small["g_pool"])
    o, y_attn = _attn_fwd(q, k, v, small["sinks"], small["g_attn"])
    x2, m, y = _mixer_out_fwd(y_pool, y_attn, x1, w_out, mine, small["mix_post"])
    wgu2, wd2 = weights_of("ffn2", (x2,))
    dx3, g2, u2, f2, loss_acc = _ffn_fwd(x2, small["ffn2_pre"], small["ffn2_post"], wgu2, wd2, mine, target=target)
    loss = loss_acc[0, 0] * (0.5 / D_MODEL)
    grads = {}
    dx2, h3, dgu2, a2, df2, grads["ffn2_pre"], grads["ffn2_post"] = _ffn_bwd(
        dx3, x2, f2, g2, u2, small["ffn2_pre"], small["ffn2_post"], wgu2, wd2, mine)
    dwgu2 = _wgrad(h3, dgu2, D_MODEL, FF_CHUNK, "wgrad_gu2", column_shards=True)
    dwd2 = _wgrad(a2, df2, FF_CHUNK, D_MODEL, "wgrad_down2")
    deps = grads_ready("ffn2", {"ffn2_w_gu": dwgu2, "ffn2_w_down": dwd2})
    dy, dm, grads["mix_post"] = _mixer_out_bwd(dx2, m, w_out, mine, small["mix_post"], deps=deps)
    dw_out = _wgrad(y, dm, D_MODEL, D_MODEL, "wgrad_out")
    dq, dk, dv, dsinks, grads["g_attn"] = _attn_bwd(dy, o, q, k, v, small["sinks"], small["g_attn"])
    grads["sinks"] = dsinks[:, 0].reshape(1, N_HEADS)
    du, grads["w_pool"], grads["pool_scale"], grads["g_pool"] = _pool_bwd(
        dy, u, small["w_pool"], small["pool_scale"], small["g_pool"])
    dx1, dz, h2, grads["mix_pre"] = _mixer_in_bwd(dx2, x1, small["mix_pre"], w_in_t, mine, du, dq, dk, dv, pos, invf)
    dw_in_t = _wgrad(dz, h2, IN_WIDTH, D_MODEL, "wgrad_in")
    deps = grads_ready("mixer", {"w_in": dw_in_t, "w_out": dw_out})
    dx, h1, dgu1, a1, df1, grads["ffn1_pre"], grads["ffn1_post"] = _ffn_bwd(
        dx1, x, f1, g1, u1, small["ffn1_pre"], small["ffn1_post"], wgu1, wd1, mine, deps=deps)
    dwgu1 = _wgrad(h1, dgu1, D_MODEL, FF_CHUNK, "wgrad_gu1", column_shards=True)
    deps = grads_ready("ffn1_gu", {"ffn1_w_gu": dwgu1})
    dwd1 = _wgrad(a1, df1, FF_CHUNK, D_MODEL, "wgrad_down1", deps=deps)
    grads_ready("ffn1_down", {"ffn1_w_down": dwd1})
    return loss, dx, grads


def _place():
    return lax.axis_index("x"), lax.axis_index("y"), lax.axis_index("c")


def _other_chips(x, y):
    return [(1 - x, y), (x, 1 - y), (1 - x, 1 - y)]


def _hbm_shape(shape, dtype):
    return jax.ShapeDtypeStruct(shape, dtype)


_HBM = pl.BlockSpec(memory_space=pltpu.HBM)
_SEM = pl.BlockSpec(memory_space=pltpu.SEMAPHORE)
_EFFECT = pltpu.SideEffectType.DATAFLOW_SIDE_EFFECTING
GATHER, GATHER_HALF, REDUCE = "gather", "gather_half", "reduce"


def _in_hbm(a):
    return pltpu.with_memory_space_constraint(a, pltpu.HBM)


def _core_half(rows, c):
    return pl.ds(pl.multiple_of(c * (rows // 2), 16), rows // 2)


def _chip_copies(kind, srcs, lands, send_sems, recv_sems):
    x, y, c = _place()
    mine = 2 * x + y
    copies = []
    for w in range(len(srcs)):
        for k, (px, py) in enumerate(_other_chips(x, y)):
            if kind == GATHER:
                src, dst = srcs[w], lands[w].at[mine]
            elif kind == GATHER_HALF:
                half = _core_half(srcs[w].shape[0], c)
                src, dst = srcs[w].at[half, :], lands[w].at[mine, half, :]
            else:
                src, dst = srcs[w].at[2 * px + py], lands[w].at[k]
            pair = (N_SHARD - 1) * w + k
            copies.append(pltpu.make_async_remote_copy(
                src_ref=src, dst_ref=dst, send_sem=send_sems.at[pair], recv_sem=recv_sems.at[pair],
                device_id=(px, py, c), device_id_type=MESH))
    return copies


def _landing_shape(kind, src):
    return ((N_SHARD - 1,) + src.shape[1:]) if kind == REDUCE else ((N_SHARD,) + src.shape)


def _exchange_start(kinds, groups, name):
    sizes = [len(g) for g in groups]
    flat = [s for g in groups for s in g]
    n, ng = len(flat), len(groups)

    def body(*refs):
        srcs, lands = refs[:n], refs[n:2 * n]
        sems = refs[2 * n:2 * n + 2 * ng]
        token = refs[-1]
        start = 0
        for gi, size in enumerate(sizes):
            for cp in _chip_copies(kinds[gi], srcs[start:start + size], lands[start:start + size],
                                   sems[2 * gi], sems[2 * gi + 1]):
                cp.start()
            start += size
        token[...] = jnp.zeros_like(token)

    landings = [lax.empty(_landing_shape(kind, s), s.dtype) for kind, g in zip(kinds, groups) for s in g]
    sem_shapes = [pltpu.SemaphoreType.DMA((size * (N_SHARD - 1),)) for size in sizes for _ in range(2)]
    outs = pl.pallas_call(
        body, name=name,
        in_specs=[_HBM] * (2 * n),
        out_specs=[_SEM] * (2 * ng) + [_HBM] * (2 * n) + [pl.BlockSpec(memory_space=pltpu.VMEM)],
        out_shape=sem_shapes + [pltpu.HBM(a.shape, a.dtype) for a in flat + landings]
        + [jax.ShapeDtypeStruct((8, _LANES), F32)],
        input_output_aliases={i: 2 * ng + i for i in range(2 * n)},
        compiler_params=pltpu.CompilerParams(has_side_effects=_EFFECT),
    )(*[_in_hbm(a) for a in flat + landings])
    sems, srcs, lands, token = outs[:2 * ng], outs[2 * ng:2 * ng + n], outs[2 * ng + n:2 * ng + 2 * n], outs[-1]
    handles, start = [], 0
    for gi, size in enumerate(sizes):
        handles.append((sems[2 * gi], sems[2 * gi + 1], srcs[start:start + size], lands[start:start + size]))
        start += size
    return handles, token


def _exchange_wait(kind, handle, after, name):
    send_sems, recv_sems, srcs, lands = handle
    n = len(srcs)

    def body(*refs):
        copies = _chip_copies(kind, refs[:n], refs[n:2 * n], refs[2 * n], refs[2 * n + 1])
        for cp in copies:
            cp.wait_send()
        for cp in copies:
            cp.wait_recv()

    outs = pl.pallas_call(
        body, name=name,
        in_specs=[_HBM] * (2 * n) + [_SEM, _SEM] + [_ANY] * len(after),
        out_specs=[_HBM] * (2 * n),
        out_shape=[pltpu.HBM(a.shape, a.dtype) for a in list(srcs) + list(lands)],
        input_output_aliases={i: i for i in range(2 * n)},
        compiler_params=pltpu.CompilerParams(has_side_effects=_EFFECT),
    )(*srcs, *lands, send_sems, recv_sems, *after)
    return outs[:n], outs[n:]


def _swap_gathered_halves(lands, name):
    n = len(lands)

    def body(*refs):
        bufs = refs[n:2 * n]
        send_sems, recv_sems = refs[2 * n:]
        x, y, c = _place()
        mine = 2 * x + y
        sends, arrivals = [], []
        for w in range(n):
            rows = bufs[w].shape[1]
            for d in range(1, N_SHARD):
                slot = (mine + d) % N_SHARD
                sems = dict(send_sem=send_sems.at[(N_SHARD - 1) * w + d - 1],
                            recv_sem=recv_sems.at[(N_SHARD - 1) * w + d - 1],
                            device_id=(x, y, 1 - c), device_id_type=MESH)
                fetched = bufs[w].at[slot, _core_half(rows, c), :]
                missing = bufs[w].at[slot, _core_half(rows, 1 - c), :]
                sends.append(pltpu.make_async_remote_copy(src_ref=fetched, dst_ref=fetched, **sems))
                arrivals.append(pltpu.make_async_remote_copy(src_ref=missing, dst_ref=missing, **sems))
        for cp in sends:
            cp.start()
        for cp in arrivals:
            cp.wait_recv()
        for cp in sends:
            cp.wait_send()

    return pl.pallas_call(
        body, name=name, in_specs=[_ANY] * n, out_specs=[_ANY] * n,
        out_shape=[_hbm_shape(a.shape, a.dtype) for a in lands],
        input_output_aliases={i: i for i in range(n)},
        scratch_shapes=[pltpu.SemaphoreType.DMA((n * (N_SHARD - 1),)), pltpu.SemaphoreType.DMA((n * (N_SHARD - 1),))],
        compiler_params=pltpu.CompilerParams(has_side_effects=True),
    )(*lands)


def _gather_slab(slab):
    def body(slab_ref, slab_all, local_sem, send_sems, recv_sems):
        x, y, c = _place()
        me = 4 * x + 2 * y + c
        own = pltpu.make_async_copy(slab_ref, slab_all.at[me], local_sem)
        own.start()
        copies, arrivals = [], []
        for k in range(1, 8):
            px, py, pc = x ^ (k >> 2), y ^ ((k >> 1) & 1), c ^ (k & 1)
            copies.append(pltpu.make_async_remote_copy(
                src_ref=slab_ref, dst_ref=slab_all.at[me], send_sem=send_sems.at[k - 1], recv_sem=recv_sems.at[k - 1],
                device_id=(px, py, pc), device_id_type=MESH))
            arrivals.append(pltpu.make_async_remote_copy(
                src_ref=slab_ref, dst_ref=slab_all.at[4 * px + 2 * py + pc], send_sem=send_sems.at[k - 1],
                recv_sem=recv_sems.at[k - 1], device_id=(px, py, pc), device_id_type=MESH))
        for cp in copies:
            cp.start()
        for cp in arrivals:
            cp.wait_recv()
        for cp in copies:
            cp.wait_send()
        own.wait()

    return pl.pallas_call(
        body, name="gather_slab", in_specs=[_ANY], out_specs=_ANY,
        out_shape=_hbm_shape((8,) + slab.shape, slab.dtype),
        scratch_shapes=[pltpu.SemaphoreType.DMA, pltpu.SemaphoreType.DMA((7,)), pltpu.SemaphoreType.DMA((7,))],
        compiler_params=pltpu.CompilerParams(has_side_effects=True),
    )(slab)


def _swap_with_sibling(partials, name):
    n = len(partials)

    def body(*refs):
        ins, outs = refs[:n], refs[n:2 * n]
        send_sems, recv_sems = refs[2 * n:]
        x, y, c = _place()
        sends = [pltpu.make_async_remote_copy(
            src_ref=ins[w], dst_ref=outs[w], send_sem=send_sems.at[w], recv_sem=recv_sems.at[w],
            device_id=(x, y, 1 - c), device_id_type=MESH) for w in range(n)]
        for cp in sends:
            cp.start()
        for cp in sends:
            cp.wait_recv()
        for cp in sends:
            cp.wait_send()

    return pl.pallas_call(
        body, name=name,
        in_specs=[_ANY] * n, out_specs=[_ANY] * n,
        out_shape=[_hbm_shape(p.shape, p.dtype) for p in partials],
        scratch_shapes=[pltpu.SemaphoreType.DMA((n,)), pltpu.SemaphoreType.DMA((n,))],
        compiler_params=pltpu.CompilerParams(has_side_effects=True),
    )(*partials)


def _row_block(rows, cap):
    best = None
    for cand in range(16, min(rows, cap) + 1, 16):
        if rows % cand == 0:
            best = cand
    assert best is not None, rows
    return best


def _chip_partial(own, received, shard, name):
    _, R, C = own.shape
    rb = _row_block(R, 512)

    def body(shard_ref, own_ref, rec_ref, out_ref):
        acc = own_ref[...]
        for k in range(3):
            acc = acc + rec_ref[k].astype(F32)
        out_ref[...] = acc

    return pl.pallas_call(
        body, name=name,
        grid_spec=pltpu.PrefetchScalarGridSpec(
            num_scalar_prefetch=1, grid=(R // rb,),
            in_specs=[pl.BlockSpec((None, rb, C), lambda i, s: (s[0], i, 0)),
                      pl.BlockSpec((3, rb, C), lambda i, s: (0, i, 0))],
            out_specs=pl.BlockSpec((rb, C), lambda i, s: (i, 0))),
        out_shape=jax.ShapeDtypeStruct((R, C), F32),
        compiler_params=_params(dimension_semantics=("arbitrary",)),
    )(shard, own, received)


def _adamw(w, m, v, g_parts, name):
    R, C = w.shape
    stacked = g_parts[0].ndim == 3
    rb = _row_block(R, 256) if R % 16 == 0 else R

    def body(w_ref, m_ref, v_ref, *refs):
        g_refs, (grad_ref, delta_ref, m_out, v_out) = refs[:-4], refs[-4:]
        if stacked:
            g = g_refs[0][0]
            for k in range(1, g_parts[0].shape[0]):
                g = g + g_refs[0][k]
        else:
            g = g_refs[0][...]
            for g_ref in g_refs[1:]:
                g = g + g_ref[...]
        grad_ref[...] = g
        new_m = ADAM_B1 * m_ref[...] + (1.0 - ADAM_B1) * g
        new_v = ADAM_B2 * v_ref[...] + (1.0 - ADAM_B2) * (g * g)
        m_hat = new_m / (1.0 - ADAM_B1 ** ADAM_STEP)
        v_hat = new_v / (1.0 - ADAM_B2 ** ADAM_STEP)
        delta_ref[...] = -ADAM_LR * (m_hat / (jnp.sqrt(v_hat) + ADAM_EPS) + ADAM_WD * w_ref[...])
        m_out[...] = new_m
        v_out[...] = new_v

    spec = pl.BlockSpec((rb, C), lambda i: (i, 0))
    g_specs = [pl.BlockSpec((g_parts[0].shape[0], rb, C), lambda i: (0, i, 0))] if stacked else [spec] * len(g_parts)
    return pl.pallas_call(
        body, name=name, grid=(R // rb,),
        in_specs=[spec, spec, spec] + g_specs,
        out_specs=[spec] * 4,
        out_shape=[jax.ShapeDtypeStruct((R, C), F32)] * 4,
        compiler_params=_params(dimension_semantics=("arbitrary",)),
    )(w, m, v, *g_parts)


SMALL_NAMES = ("ffn1_pre", "ffn1_post", "mix_pre", "pool_scale", "sinks", "g_pool", "g_attn", "mix_post",
               "ffn2_pre", "ffn2_post", "w_pool")
_SLAB_PART = 8 * _LANES


def _to_slab(parts):
    rows = []
    for name in SMALL_NAMES:
        flat = parts[name].reshape(-1)
        padded = -(-flat.shape[0] // _SLAB_PART) * _SLAB_PART
        rows.append(jnp.pad(flat, (0, padded - flat.shape[0])).reshape(-1, _LANES))
    return jnp.concatenate(rows, axis=0)


def _from_slab(slab, like):
    out, row = {}, 0
    for name in SMALL_NAMES:
        size = like[name].size
        rows = -(-size // _SLAB_PART) * (_SLAB_PART // _LANES)
        out[name] = slab[row:row + rows].reshape(-1)[:size].reshape(like[name].shape)
        row += rows
    return out


BIG_NAMES = ("ffn1_w_gu", "ffn1_w_down", "w_in", "w_out", "ffn2_w_gu", "ffn2_w_down")
WEIGHT_ORDER = ("ffn1_pre", "ffn1_w_gu", "ffn1_w_down", "ffn1_post", "mix_pre", "w_in", "w_pool", "pool_scale",
                "sinks", "g_pool", "g_attn", "w_out", "mix_post", "ffn2_pre", "ffn2_w_gu", "ffn2_w_down", "ffn2_post")


def kernel(x, positions, ffn1_pre, ffn1_w_gu, ffn1_w_down, ffn1_post, mix_pre, w_in, w_pool, pool_scale, sinks, g_pool, g_attn, w_out, mix_post, ffn2_pre, ffn2_w_gu, ffn2_w_down, ffn2_post, loss_target, m_ffn1_pre, m_ffn1_w_gu, m_ffn1_w_down, m_ffn1_post, m_mix_pre, m_w_in, m_w_pool, m_pool_scale, m_sinks, m_g_pool, m_g_attn, m_w_out, m_mix_post, m_ffn2_pre, m_ffn2_w_gu, m_ffn2_w_down, m_ffn2_post, v_ffn1_pre, v_ffn1_w_gu, v_ffn1_w_down, v_ffn1_post, v_mix_pre, v_w_in, v_w_pool, v_pool_scale, v_sinks, v_g_pool, v_g_attn, v_w_out, v_mix_post, v_ffn2_pre, v_ffn2_w_gu, v_ffn2_w_down, v_ffn2_post):
    given = dict(locals())
    weights = {n: given[n][0] for n in WEIGHT_ORDER}
    moments_m = {n: given["m_" + n][0] for n in WEIGHT_ORDER}
    moments_v = {n: given["v_" + n][0] for n in WEIGHT_ORDER}
    S = x.shape[1]
    shard = (2 * lax.axis_index("x") + lax.axis_index("y")).astype(jnp.int32).reshape(1)

    local16 = {n: weights[n].astype(BF16) for n in BIG_NAMES if n != "w_in"}
    local16["w_in"] = weights["w_in"].T.astype(BF16)
    gather_groups = {"ffn1": ("ffn1_w_gu", "ffn1_w_down"), "mixer": ("w_in", "w_out"),
                     "ffn2": ("ffn2_w_gu", "ffn2_w_down")}
    gather_kinds = {"ffn1": GATHER_HALF, "mixer": GATHER, "ffn2": GATHER}
    handles, _ = _exchange_start(list(gather_kinds.values()),
                                 [[local16[n] for n in names] for names in gather_groups.values()], "gather_start")
    gather_handles = dict(zip(gather_groups, handles))

    def weights_of(group, after):
        kind = gather_kinds[group]
        owns, lands = _exchange_wait(kind, gather_handles[group], list(after), "gather_wait_" + group)
        if kind == GATHER_HALF:
            lands = _swap_gathered_halves(lands, "swap_gathered_" + group)
        return list(zip(lands, owns))

    pending, last_token = {}, []

    def grads_ready(group, grads):
        names = list(grads)
        (handle,), token = _exchange_start([REDUCE], [[grads[n][1] for n in names]], "reduce_start_" + group)
        pending[group] = (names, handle, [grads[n][0] for n in names])
        last_token[:] = [token]
        return [token]

    small = {n: (weights[n] if weights[n].ndim > 1 else weights[n].reshape(1, -1)) for n in SMALL_NAMES}
    loss, dx, small_grads = _local_step(x[0], positions.reshape(S, 1), loss_target[0], small, shard,
                                        weights_of, grads_ready)
    loss = lax.psum(loss, ("x", "y", "c"))

    grad, delta, new_m, new_v = {}, {}, {}, {}

    def finish(groups, after):
        names, partials = [], []
        for group in groups:
            group_names, handle, own32 = pending[group]
            _, received = _exchange_wait(REDUCE, handle, after, "reduce_wait_" + group)
            names += group_names
            partials += [_chip_partial(g32, rec, shard, "chip_partial_" + n)
                         for n, g32, rec in zip(group_names, own32, received)]
        siblings = _swap_with_sibling(partials, "swap_" + groups[0])
        for name, mine, theirs in zip(names, partials, siblings):
            if name == "w_in":
                mine, theirs = mine.T, theirs.T
            grad[name], delta[name], new_m[name], new_v[name] = _adamw(
                weights[name], moments_m[name], moments_v[name], [mine, theirs], "adamw_" + name)
        return [grad[names[-1]]]

    after = finish(["ffn2"], last_token)
    after = finish(["mixer"], after)
    finish(["ffn1_gu", "ffn1_down"], after)
    slab_all = _gather_slab(_to_slab(small_grads))
    small_like = {n: small[n] for n in SMALL_NAMES}
    slabs = _adamw(_to_slab(small), _to_slab({n: moments_m[n] for n in SMALL_NAMES}),
                   _to_slab({n: moments_v[n] for n in SMALL_NAMES}), [slab_all], "adamw_small")
    for store, slab in zip((grad, delta, new_m, new_v), slabs):
        store.update(_from_slab(slab, small_like))

    def out(store):
        return [store[n].reshape(given[n].shape) for n in WEIGHT_ORDER]
    return (loss, dx[None], *out(grad), *out(delta), *out(new_m), *out(new_v))
```

```python
import functools

import jax
import jax.numpy as jnp
from jax import lax
from jax.experimental import pallas as pl
from jax.experimental.pallas import tpu as pltpu

F32 = jnp.float32
BF16 = jnp.bfloat16

D_MODEL = 1024
D_FF = 2816
N_SHARD = 4
FF_CHUNK = D_FF // 2
POOL_WINDOWS = (2, 4, 8, 16)
POOL_WIDTH = 512
POOL_GROUP = 128
HALO = 16
HEAD_DIM = 64
N_HEADS = 8
ATTN_WIDTH = 512
KV_WIDTH = 128
IN_WIDTH = 1280
BLOCK = 128
ROT_DIM = 16
ROPE_THETA = 500000.0
EPS = 1e-6
NEG_INF = -1e30
ATTN_SCALE = HEAD_DIM ** -0.5

ADAM_LR = 0.001
ADAM_B1 = 0.9
ADAM_B2 = 0.999
ADAM_EPS = 1e-08
ADAM_WD = 0.01
ADAM_STEP = 10

VMEM_LIMIT = 60 * 1024 * 1024
FFN_FWD_TILE = 512
FFN_BWD_TILE = 256
MIXER_TILE = 512

MESH = pl.DeviceIdType.MESH


def _params(**kw):
    return pltpu.CompilerParams(vmem_limit_bytes=VMEM_LIMIT, **kw)


def _dot(a, b):
    return jnp.dot(a, b, preferred_element_type=F32)


def _dot_nt(a, b):
    return lax.dot_general(a, b, (((1,), (1,)), ((), ())), preferred_element_type=F32)


def _dot_tn(a, b):
    return lax.dot_general(a, b, (((0,), (0,)), ((), ())), preferred_element_type=F32)


def _rstd(x):
    return lax.rsqrt(jnp.mean(x * x, axis=-1, keepdims=True) + EPS)


def _norm_bwd(dy, xn, r, gain):
    dxn = dy * gain
    return r * (dxn - xn * jnp.mean(dxn * xn, axis=-1, keepdims=True))


def _sigmoid(x):
    return 1.0 / (1.0 + jnp.exp(-x))


def _full(shape):
    return pl.BlockSpec(shape, lambda *_: (0,) * len(shape))


def _rows(tile, width, col=0):
    return pl.BlockSpec((tile, width), lambda i: (i, col))


_ANY = pl.BlockSpec(memory_space=pl.ANY)


_SMEM = pl.BlockSpec(memory_space=pltpu.SMEM)


def _load_once(pairs, sem):
    @pl.when(pl.program_id(0) == 0)
    def _():
        copies = [pltpu.make_async_copy(src, dst, sem.at[n]) for n, (src, dst) in enumerate(pairs)]
        for cp in copies:
            cp.start()
        for cp in copies:
            cp.wait()


def _gathered(land_ref, own_ref, vmem_ref, mine, rows=None):
    def dst(slot):
        if rows is None:
            return vmem_ref.at[slot]
        return vmem_ref.at[pl.ds(pl.multiple_of(slot * rows, 16), rows), :]
    pairs = [(land_ref.at[(mine + d) % N_SHARD], dst((mine + d) % N_SHARD)) for d in range(1, N_SHARD)]
    return pairs + [(own_ref, dst(mine))]


def _ignoring(body, start, count):
    def wrapped(*refs):
        return body(*refs[:start], *refs[start + count:])
    return wrapped


def _ffn_fwd(x, pre, post, wgu, wd, mine, target=None, deps=()):
    S = x.shape[0]
    tm = FFN_FWD_TILE
    with_loss = target is not None

    def body(*refs):
        if with_loss:
            (x_ref, pre_ref, post_ref, wgu_land, wgu_own, wd_land, wd_own, mine_ref, tgt_ref,
             out_ref, g_ref, u_ref, f_ref, loss_ref, wgu_v, wd_v, sem) = refs
        else:
            (x_ref, pre_ref, post_ref, wgu_land, wgu_own, wd_land, wd_own, mine_ref,
             out_ref, g_ref, u_ref, f_ref, wgu_v, wd_v, sem) = refs
        _load_once(_gathered(wgu_land, wgu_own, wgu_v, mine_ref[0])
                   + _gathered(wd_land, wd_own, wd_v, mine_ref[0], rows=D_FF // N_SHARD), sem)
        xv = x_ref[...]
        h = ((xv * _rstd(xv)) * pre_ref[...]).astype(BF16)
        facc = jnp.zeros((tm, D_MODEL), F32)
        for c in range(2):
            cols = slice(c * FF_CHUNK, (c + 1) * FF_CHUNK)
            g = _dot(h, wgu_v[c])
            u = _dot(h, wgu_v[2 + c])
            g_ref[:, cols] = g.astype(BF16)
            u_ref[:, cols] = u.astype(BF16)
            a = (g * _sigmoid(g)) * u
            facc = facc + _dot(a.astype(BF16), wd_v[cols, :])
        f_ref[...] = facc
        out = xv + 0.5 * ((facc * _rstd(facc)) * post_ref[...])
        if with_loss:
            diff = out - tgt_ref[...]
            out_ref[...] = diff * (1.0 / D_MODEL)

            @pl.when(pl.program_id(0) == 0)
            def _():
                loss_ref[...] = jnp.zeros_like(loss_ref)
            loss_ref[...] += jnp.sum(diff * diff)
        else:
            out_ref[...] = out

    in_specs = [_rows(tm, D_MODEL), _full((1, D_MODEL)), _full((1, D_MODEL)), _ANY, _ANY, _ANY, _ANY, _SMEM]
    args = [x, pre, post, *wgu, *wd, mine]
    out_shape = [jax.ShapeDtypeStruct((S, D_MODEL), F32), jax.ShapeDtypeStruct((S, D_FF), BF16),
                 jax.ShapeDtypeStruct((S, D_FF), BF16), jax.ShapeDtypeStruct((S, D_MODEL), F32)]
    out_specs = [_rows(tm, D_MODEL), _rows(tm, D_FF), _rows(tm, D_FF), _rows(tm, D_MODEL)]
    if with_loss:
        in_specs.append(_rows(tm, D_MODEL))
        args.append(target)
        out_shape.append(jax.ShapeDtypeStruct((8, 128), F32))
        out_specs.append(_full((8, 128)))
    return pl.pallas_call(
        _ignoring(body, len(args), len(deps)), name="ffn_fwd_loss" if with_loss else "ffn_fwd",
        grid=(S // tm,), in_specs=in_specs + [_ANY] * len(deps), out_specs=out_specs, out_shape=out_shape,
        scratch_shapes=[pltpu.VMEM((N_SHARD, D_MODEL, FF_CHUNK), BF16), pltpu.VMEM((D_FF, D_MODEL), BF16),
                        pltpu.SemaphoreType.DMA((2 * N_SHARD,))],
        compiler_params=_params(dimension_semantics=("arbitrary",)),
    )(*args, *deps)


def _ffn_bwd(dout, x, f, g, u, pre, post, wgu, wd, mine, deps=()):
    S = x.shape[0]
    tm = FFN_BWD_TILE

    def body(dout_ref, x_ref, f_ref, g_ref, u_ref, pre_ref, post_ref, wgu_land, wgu_own, wd_land, wd_own, mine_ref,
             dx_ref, h_ref, dgu_ref, a_ref, df_ref, dpre_ref, dpost_ref, wgu_v, wd_v, sem):
        _load_once(_gathered(wgu_land, wgu_own, wgu_v, mine_ref[0])
                   + _gathered(wd_land, wd_own, wd_v, mine_ref[0], rows=D_FF // N_SHARD), sem)

        @pl.when(pl.program_id(0) == 0)
        def _():
            dpre_ref[...] = jnp.zeros_like(dpre_ref)
            dpost_ref[...] = jnp.zeros_like(dpost_ref)

        dout_v = dout_ref[...]
        dn = 0.5 * dout_v
        fv = f_ref[...]
        rf = _rstd(fv)
        fn = fv * rf
        dpost_ref[...] += jnp.sum(dn * fn, axis=0, keepdims=True)
        df = _norm_bwd(dn, fn, rf, post_ref[...]).astype(BF16)
        df_ref[...] = df
        dh = jnp.zeros((tm, D_MODEL), F32)
        for c in range(2):
            cols = slice(c * FF_CHUNK, (c + 1) * FF_CHUNK)
            da = _dot_nt(df, wd_v[cols, :])
            gv = g_ref[:, cols].astype(F32)
            uv = u_ref[:, cols].astype(F32)
            sg = _sigmoid(gv)
            silu = gv * sg
            a_ref[:, cols] = (silu * uv).astype(BF16)
            dg = ((da * uv) * (sg * (1.0 + gv * (1.0 - sg)))).astype(BF16)
            du = (da * silu).astype(BF16)
            dgu_ref[:, cols] = dg
            dgu_ref[:, 2 * FF_CHUNK + c * FF_CHUNK:2 * FF_CHUNK + (c + 1) * FF_CHUNK] = du
            dh = dh + _dot_nt(dg, wgu_v[c]) + _dot_nt(du, wgu_v[2 + c])
        xv = x_ref[...]
        rx = _rstd(xv)
        xn = xv * rx
        h_ref[...] = (xn * pre_ref[...]).astype(BF16)
        dpre_ref[...] += jnp.sum(dh * xn, axis=0, keepdims=True)
        dx_ref[...] = dout_v + _norm_bwd(dh, xn, rx, pre_ref[...])

    args = [dout, x, f, g, u, pre, post, *wgu, *wd, mine]
    return pl.pallas_call(
        _ignoring(body, len(args), len(deps)), name="ffn_bwd", grid=(S // tm,),
        in_specs=[_rows(tm, D_MODEL), _rows(tm, D_MODEL), _rows(tm, D_MODEL), _rows(tm, D_FF), _rows(tm, D_FF),
                  _full((1, D_MODEL)), _full((1, D_MODEL)), _ANY, _ANY, _ANY, _ANY, _SMEM] + [_ANY] * len(deps),
        out_specs=[_rows(tm, D_MODEL), _rows(tm, D_MODEL), _rows(tm, 2 * D_FF), _rows(tm, D_FF), _rows(tm, D_MODEL),
                   _full((1, D_MODEL)), _full((1, D_MODEL))],
        out_shape=[jax.ShapeDtypeStruct((S, D_MODEL), F32), jax.ShapeDtypeStruct((S, D_MODEL), BF16),
                   jax.ShapeDtypeStruct((S, 2 * D_FF), BF16), jax.ShapeDtypeStruct((S, D_FF), BF16),
                   jax.ShapeDtypeStruct((S, D_MODEL), BF16),
                   jax.ShapeDtypeStruct((1, D_MODEL), F32), jax.ShapeDtypeStruct((1, D_MODEL), F32)],
        scratch_shapes=[pltpu.VMEM((N_SHARD, D_MODEL, FF_CHUNK), BF16), pltpu.VMEM((D_FF, D_MODEL), BF16),
                        pltpu.SemaphoreType.DMA((2 * N_SHARD,))],
        compiler_params=_params(dimension_semantics=("arbitrary",)),
    )(*args, *deps)


def _wgrad(lhs, rhs, m_block, n_block, name, column_shards=False, tk=2048, deps=()):
    S, M = lhs.shape
    N = rhs.shape[1]
    k_steps = S // tk

    def body(lhs_ref, rhs_ref, out_ref, out16_ref):
        k = pl.program_id(2)

        @pl.when(k == 0)
        def _():
            out_ref[...] = jnp.zeros_like(out_ref)
        out_ref[...] += _dot_tn(lhs_ref[...], rhs_ref[...])

        @pl.when(k == k_steps - 1)
        def _():
            out16_ref[...] = out_ref[...].astype(BF16)

    if column_shards:
        assert N == N_SHARD * n_block
        shape = (N_SHARD, M, n_block)
        out_spec = pl.BlockSpec((None, m_block, n_block), lambda i, j, k: (j, i, 0))
    else:
        shape = (M, N)
        out_spec = pl.BlockSpec((m_block, n_block), lambda i, j, k: (i, j))
    out, out16 = pl.pallas_call(
        _ignoring(body, 2, len(deps)), name=name, grid=(M // m_block, N // n_block, k_steps),
        in_specs=[pl.BlockSpec((tk, m_block), lambda i, j, k: (k, i)),
                  pl.BlockSpec((tk, n_block), lambda i, j, k: (k, j))] + [_ANY] * len(deps),
        out_specs=[out_spec, out_spec],
        out_shape=[jax.ShapeDtypeStruct(shape, F32), jax.ShapeDtypeStruct(shape, BF16)],
        compiler_params=_params(dimension_semantics=("arbitrary", "arbitrary", "arbitrary")),
    )(lhs, rhs, *deps)
    if not column_shards:
        out = out.reshape(N_SHARD, M // N_SHARD, N)
        out16 = out16.reshape(N_SHARD, M // N_SHARD, N)
    return out, out16


def _rope_tables(pos_ref, invf_ref):
    ang = pos_ref[...].astype(F32) * invf_ref[...]
    cos, sin = jnp.cos(ang), jnp.sin(ang)
    lane = lax.broadcasted_iota(jnp.int32, ang.shape, 1) % HEAD_DIM
    first = lane < ROT_DIM // 2
    second = (lane >= ROT_DIM // 2) & (lane < ROT_DIM)
    c = jnp.where(lane < ROT_DIM, cos, 1.0)
    s_first = jnp.where(first, sin, 0.0)
    s_second = jnp.where(second, sin, 0.0)
    return c, s_first, s_second


_HALF = ROT_DIM // 2
_LANES = 128


def _rope(t, tables):
    c, s_first, s_second = tables
    return t * c - pltpu.roll(t, _LANES - _HALF, axis=1) * s_first + pltpu.roll(t, _HALF, axis=1) * s_second


def _rope_transposed(t, tables):
    c, s_first, s_second = tables
    return t * c - pltpu.roll(t * s_first, _HALF, axis=1) + pltpu.roll(t * s_second, _LANES - _HALF, axis=1)


def _mixer_in_fwd(x, pre, w_in_t, mine, pos, invf, deps=()):
    S = x.shape[0]
    tm = MIXER_TILE

    def body(x_ref, pre_ref, w_land, w_own, mine_ref, pos_ref, invf_ref, u_ref, q_ref, k_ref, v_ref, w_v, sem):
        _load_once(_gathered(w_land, w_own, w_v, mine_ref[0], rows=IN_WIDTH // N_SHARD), sem)
        xv = x_ref[...]
        h = ((xv * _rstd(xv)) * pre_ref[...]).astype(BF16)
        z = _dot_nt(h, w_v[...])
        tables = _rope_tables(pos_ref, invf_ref)
        u_ref[...] = z[:, :POOL_WIDTH]
        for t in range(ATTN_WIDTH // _LANES):
            lo = POOL_WIDTH + t * _LANES
            q_ref[:, t * _LANES:(t + 1) * _LANES] = _rope(z[:, lo:lo + _LANES], tables).astype(BF16)
        kv = POOL_WIDTH + ATTN_WIDTH
        k_ref[...] = _rope(z[:, kv:kv + KV_WIDTH], tables).astype(BF16)
        v_ref[...] = z[:, kv + KV_WIDTH:].astype(BF16)

    args = [x, pre, *w_in_t, mine, pos, invf]
    return pl.pallas_call(
        _ignoring(body, len(args), len(deps)), name="mixer_in_fwd", grid=(S // tm,),
        in_specs=[_rows(tm, D_MODEL), _full((1, D_MODEL)), _ANY, _ANY, _SMEM, _rows(tm, 1), _full((1, _LANES))]
        + [_ANY] * len(deps),
        out_specs=[_rows(tm, POOL_WIDTH), _rows(tm, ATTN_WIDTH), _rows(tm, KV_WIDTH), _rows(tm, KV_WIDTH)],
        out_shape=[jax.ShapeDtypeStruct((S, POOL_WIDTH), F32), jax.ShapeDtypeStruct((S, ATTN_WIDTH), BF16),
                   jax.ShapeDtypeStruct((S, KV_WIDTH), BF16), jax.ShapeDtypeStruct((S, KV_WIDTH), BF16)],
        scratch_shapes=[pltpu.VMEM((IN_WIDTH, D_MODEL), BF16), pltpu.SemaphoreType.DMA((N_SHARD,))],
        compiler_params=_params(dimension_semantics=("arbitrary",)),
    )(*args, *deps)


def _mixer_in_bwd(dres, x, pre, w_in_t, mine, du, dq, dk, dv, pos, invf, deps=()):
    S = x.shape[0]
    tm = MIXER_TILE

    def body(dres_ref, x_ref, pre_ref, w_land, w_own, mine_ref, du_ref, dq_ref, dk_ref, dv_ref, pos_ref, invf_ref,
             dx_ref, dz_ref, h_ref, dpre_ref, w_v, sem):
        _load_once(_gathered(w_land, w_own, w_v, mine_ref[0], rows=IN_WIDTH // N_SHARD), sem)

        @pl.when(pl.program_id(0) == 0)
        def _():
            dpre_ref[...] = jnp.zeros_like(dpre_ref)

        tables = _rope_tables(pos_ref, invf_ref)
        dz_ref[:, :POOL_WIDTH] = du_ref[...].astype(BF16)
        for t in range(ATTN_WIDTH // _LANES):
            lo = POOL_WIDTH + t * _LANES
            dz_ref[:, lo:lo + _LANES] = _rope_transposed(dq_ref[:, t * _LANES:(t + 1) * _LANES], tables).astype(BF16)
        kv = POOL_WIDTH + ATTN_WIDTH
        dz_ref[:, kv:kv + KV_WIDTH] = _rope_transposed(dk_ref[...], tables).astype(BF16)
        dz_ref[:, kv + KV_WIDTH:] = dv_ref[...].astype(BF16)
        dh = _dot(dz_ref[...], w_v[...])
        xv = x_ref[...]
        rx = _rstd(xv)
        xn = xv * rx
        h_ref[...] = (xn * pre_ref[...]).astype(BF16)
        dpre_ref[...] += jnp.sum(dh * xn, axis=0, keepdims=True)
        dx_ref[...] = dres_ref[...] + _norm_bwd(dh, xn, rx, pre_ref[...])

    args = [dres, x, pre, *w_in_t, mine, du, dq, dk, dv, pos, invf]
    return pl.pallas_call(
        _ignoring(body, len(args), len(deps)), name="mixer_in_bwd", grid=(S // tm,),
        in_specs=[_rows(tm, D_MODEL), _rows(tm, D_MODEL), _full((1, D_MODEL)), _ANY, _ANY, _SMEM,
                  _rows(tm, POOL_WIDTH), _rows(tm, ATTN_WIDTH), _rows(tm, KV_WIDTH), _rows(tm, KV_WIDTH),
                  _rows(tm, 1), _full((1, _LANES))] + [_ANY] * len(deps),
        out_specs=[_rows(tm, D_MODEL), _rows(tm, IN_WIDTH), _rows(tm, D_MODEL), _full((1, D_MODEL))],
        out_shape=[jax.ShapeDtypeStruct((S, D_MODEL), F32), jax.ShapeDtypeStruct((S, IN_WIDTH), BF16),
                   jax.ShapeDtypeStruct((S, D_MODEL), BF16), jax.ShapeDtypeStruct((1, D_MODEL), F32)],
        scratch_shapes=[pltpu.VMEM((IN_WIDTH, D_MODEL), BF16), pltpu.SemaphoreType.DMA((N_SHARD,))],
        compiler_params=_params(dimension_semantics=("arbitrary",)),
    )(*args, *deps)


def _pool_counts(tile_index, tm, width):
    t = tile_index * tm + lax.broadcasted_iota(jnp.int32, (tm, 1), 0)
    return jnp.minimum(t + 1, width).astype(F32)


def _pool_features(ext, u_tile, tile_index, tm):
    ds = []
    for gi, width in enumerate(POOL_WINDOWS):
        lanes = slice(gi * POOL_GROUP, (gi + 1) * POOL_GROUP)
        s = ext[:, lanes]
        shift = 1
        while shift < width:
            s = s + pltpu.roll(s, shift, axis=0)
            shift *= 2
        ds.append(s[HALO:, :] / _pool_counts(tile_index, tm, width) - u_tile[:, lanes])
    return ds


def _pool_fwd(u, w_pool, pool_scale, g_pool):
    S = u.shape[0]
    tm = MIXER_TILE

    def body(u_ref, w_ref, scale_ref, gain_ref, y_ref, ext_ref):
        i = pl.program_id(0)

        @pl.when(i == 0)
        def _():
            ext_ref[:HALO, :] = jnp.zeros((HALO, POOL_WIDTH), F32)

        u_tile = u_ref[...]
        ext_ref[HALO:, :] = u_tile
        ds = _pool_features(ext_ref[...], u_tile, i, tm)
        ext_ref[:HALO, :] = u_tile[tm - HALO:, :]
        ys = [_dot(ds[gi].astype(BF16), w_ref[gi].astype(BF16)) for gi in range(len(POOL_WINDOWS))]
        po = jnp.concatenate(ys, axis=1) * scale_ref[...]
        y_ref[...] = ((po * _rstd(po)) * gain_ref[...]).astype(BF16)

    return pl.pallas_call(
        body, name="pool_fwd", grid=(S // tm,),
        in_specs=[_rows(tm, POOL_WIDTH), _full((len(POOL_WINDOWS), POOL_GROUP, POOL_GROUP)),
                  _full((1, POOL_WIDTH)), _full((1, POOL_WIDTH))],
        out_specs=_rows(tm, POOL_WIDTH),
        out_shape=jax.ShapeDtypeStruct((S, POOL_WIDTH), BF16),
        scratch_shapes=[pltpu.VMEM((HALO + tm, POOL_WIDTH), F32)],
        compiler_params=_params(dimension_semantics=("arbitrary",)),
    )(u, w_pool, pool_scale, g_pool)


def _pool_bwd(dy, u, w_pool, pool_scale, g_pool):
    S = u.shape[0]
    tm = MIXER_TILE
    n_tiles = S // tm
    halo_blocks = tm // HALO

    def body(dy_ref, u_ref, uprev_ref, w_ref, scale_ref, gain_ref,
             du_ref, dw_ref, dscale_ref, dgain_ref, ext_ref, nxt_ref):
        i = pl.program_id(0)
        tile = n_tiles - 1 - i

        @pl.when(i == 0)
        def _():
            dw_ref[...] = jnp.zeros_like(dw_ref)
            dscale_ref[...] = jnp.zeros_like(dscale_ref)
            dgain_ref[...] = jnp.zeros_like(dgain_ref)
            nxt_ref[...] = jnp.zeros_like(nxt_ref)

        u_tile = u_ref[...]
        ext_ref[:HALO, :] = jnp.where(tile > 0, uprev_ref[...], 0.0)
        ext_ref[HALO:, :] = u_tile
        ds = _pool_features(ext_ref[...], u_tile, tile, tm)
        dsb = [d.astype(BF16) for d in ds]
        wb = [w_ref[gi].astype(BF16) for gi in range(len(POOL_WINDOWS))]
        yraw = jnp.concatenate([_dot(dsb[gi], wb[gi]) for gi in range(len(POOL_WINDOWS))], axis=1)
        po = yraw * scale_ref[...]
        r = _rstd(po)
        pn = po * r
        dyv = dy_ref[...]
        dgain_ref[...] += jnp.sum(dyv * pn, axis=0, keepdims=True)
        dpo = _norm_bwd(dyv, pn, r, gain_ref[...])
        dscale_ref[...] += jnp.sum(dpo * yraw, axis=0, keepdims=True)
        dyraw = (dpo * scale_ref[...]).astype(BF16)
        for gi, width in enumerate(POOL_WINDOWS):
            lanes = slice(gi * POOL_GROUP, (gi + 1) * POOL_GROUP)
            dw_ref[gi] += _dot_tn(dsb[gi], dyraw[:, lanes])
            dd = _dot_nt(dyraw[:, lanes], wb[gi])
            ddc = dd / _pool_counts(tile, tm, width)
            ext_ref[:tm, lanes] = ddc
            ext_ref[tm:, lanes] = nxt_ref[:, lanes]
            s = ext_ref[:, lanes]
            shift = 1
            while shift < width:
                s = s + pltpu.roll(s, HALO + tm - shift, axis=0)
                shift *= 2
            du_ref[:, lanes] = s[:tm, :] - dd
            nxt_ref[:, lanes] = ddc[:HALO, :]

    return pl.pallas_call(
        body, name="pool_bwd", grid=(n_tiles,),
        in_specs=[pl.BlockSpec((tm, POOL_WIDTH), lambda i: (n_tiles - 1 - i, 0)),
                  pl.BlockSpec((tm, POOL_WIDTH), lambda i: (n_tiles - 1 - i, 0)),
                  pl.BlockSpec((HALO, POOL_WIDTH), lambda i: (jnp.maximum((n_tiles - 1 - i) * halo_blocks - 1, 0), 0)),
                  _full((len(POOL_WINDOWS), POOL_GROUP, POOL_GROUP)), _full((1, POOL_WIDTH)), _full((1, POOL_WIDTH))],
        out_specs=[pl.BlockSpec((tm, POOL_WIDTH), lambda i: (n_tiles - 1 - i, 0)),
                   _full((len(POOL_WINDOWS), POOL_GROUP, POOL_GROUP)), _full((1, POOL_WIDTH)), _full((1, POOL_WIDTH))],
        out_shape=[jax.ShapeDtypeStruct((S, POOL_WIDTH), F32),
                   jax.ShapeDtypeStruct((len(POOL_WINDOWS), POOL_GROUP, POOL_GROUP), F32),
                   jax.ShapeDtypeStruct((1, POOL_WIDTH), F32), jax.ShapeDtypeStruct((1, POOL_WIDTH), F32)],
        scratch_shapes=[pltpu.VMEM((HALO + tm, POOL_WIDTH), F32), pltpu.VMEM((HALO, POOL_WIDTH), F32)],
        compiler_params=_params(dimension_semantics=("arbitrary",)),
    )(dy, u, u, w_pool, pool_scale, g_pool)


def _kv_variants(prev_ref, cur_ref):
    cat = jnp.concatenate([prev_ref[...], cur_ref[...]], axis=0).astype(F32)
    rolled = pltpu.roll(cat, HEAD_DIM, axis=1)
    low = lax.broadcasted_iota(jnp.int32, cat.shape, 1) < HEAD_DIM
    zero = jnp.zeros_like(cat)
    pick = lambda src, keep_low: jnp.where(low if keep_low else ~low, src, zero).astype(BF16)
    return [[pick(cat, True), pick(rolled, False)], [pick(rolled, True), pick(cat, False)]]


def _band_mask(block_index):
    qi = lax.broadcasted_iota(jnp.int32, (BLOCK, 2 * BLOCK), 0)
    kj = lax.broadcasted_iota(jnp.int32, (BLOCK, 2 * BLOCK), 1)
    first_key = jnp.where(block_index > 0, 0, BLOCK)
    return (kj > qi) & (kj <= qi + BLOCK) & (kj >= first_key)


def _attn_probs(q_pair, k_var, valid, sink):
    s = _dot_nt(q_pair, k_var) * ATTN_SCALE
    s = jnp.where(valid, s, NEG_INF)
    m = jnp.maximum(jnp.max(s, axis=-1, keepdims=True), sink)
    p = jnp.exp(s - m)
    p_sink = jnp.exp(sink - m)
    inv = 1.0 / (jnp.sum(p, axis=-1, keepdims=True) + p_sink)
    return p * inv, p_sink * inv


def _attn_fwd(q, k, v, sinks, g_attn):
    S = q.shape[0]
    nb = S // BLOCK

    def body(q_ref, kp_ref, kc_ref, vp_ref, vc_ref, sinks_ref, gain_ref, o_ref, y_ref):
        b = pl.program_id(0)
        kvar = _kv_variants(kp_ref, kc_ref)
        vvar = _kv_variants(vp_ref, vc_ref)
        valid = _band_mask(b)
        pairs = []
        for i in range(N_HEADS // 2):
            q_pair = q_ref[:, i * _LANES:(i + 1) * _LANES]
            acc = jnp.zeros((BLOCK, _LANES), F32)
            for e in range(2):
                head = 2 * i + e
                j = head // (N_HEADS // 2)
                p, _ = _attn_probs(q_pair, kvar[j][e], valid, sinks_ref[0, head])
                acc = acc + _dot(p.astype(BF16), vvar[j][e])
            pairs.append(acc)
        o = jnp.concatenate(pairs, axis=1)
        o_ref[...] = o
        y_ref[...] = ((o * _rstd(o)) * gain_ref[...]).astype(BF16)

    prev = lambda b: (jnp.maximum(b - 1, 0), 0)
    return pl.pallas_call(
        body, name="attn_fwd", grid=(nb,),
        in_specs=[_rows(BLOCK, ATTN_WIDTH),
                  pl.BlockSpec((BLOCK, KV_WIDTH), prev), _rows(BLOCK, KV_WIDTH),
                  pl.BlockSpec((BLOCK, KV_WIDTH), prev), _rows(BLOCK, KV_WIDTH),
                  pl.BlockSpec(memory_space=pltpu.SMEM), _full((1, ATTN_WIDTH))],
        out_specs=[_rows(BLOCK, ATTN_WIDTH), _rows(BLOCK, ATTN_WIDTH)],
        out_shape=[jax.ShapeDtypeStruct((S, ATTN_WIDTH), F32), jax.ShapeDtypeStruct((S, ATTN_WIDTH), BF16)],
        compiler_params=_params(dimension_semantics=("arbitrary",)),
    )(q, k, k, v, v, sinks, g_attn)


def _attn_bwd(dy, o, q, k, v, sinks, g_attn):
    S = q.shape[0]
    nb = S // BLOCK

    def body(dy_ref, o_ref, q_ref, kp_ref, kc_ref, vp_ref, vc_ref, sinks_ref, gain_ref,
             dq_ref, dk_ref, dv_ref, dsink_ref, dgain_ref, kcarry_ref, vcarry_ref):
        b = pl.program_id(0)

        @pl.when(b == 0)
        def _():
            dsink_ref[...] = jnp.zeros_like(dsink_ref)
            dgain_ref[...] = jnp.zeros_like(dgain_ref)
            kcarry_ref[...] = jnp.zeros_like(kcarry_ref)
            vcarry_ref[...] = jnp.zeros_like(vcarry_ref)

        @pl.when(b < nb)
        def _():
            ov = o_ref[...]
            r = _rstd(ov)
            on = ov * r
            dyv = dy_ref[...]
            dgain_ref[...] += jnp.sum(dyv * on, axis=0, keepdims=True)
            do = _norm_bwd(dyv, on, r, gain_ref[...])
            kvar = _kv_variants(kp_ref, kc_ref)
            vvar = _kv_variants(vp_ref, vc_ref)
            valid = _band_mask(b)
            dk_acc = [[jnp.zeros((2 * BLOCK, _LANES), F32) for _ in range(2)] for _ in range(2)]
            dv_acc = [[jnp.zeros((2 * BLOCK, _LANES), F32) for _ in range(2)] for _ in range(2)]
            sink_rows = []
            for i in range(N_HEADS // 2):
                q_pair = q_ref[:, i * _LANES:(i + 1) * _LANES]
                do_pair = do[:, i * _LANES:(i + 1) * _LANES].astype(BF16)
                dq_pair = jnp.zeros((BLOCK, _LANES), F32)
                for e in range(2):
                    head = 2 * i + e
                    j = head // (N_HEADS // 2)
                    p, p_sink = _attn_probs(q_pair, kvar[j][e], valid, sinks_ref[0, head])
                    dp = _dot_nt(do_pair, vvar[j][e])
                    delta = jnp.sum(p * dp, axis=-1, keepdims=True)
                    ds = ((p * (dp - delta)) * ATTN_SCALE).astype(BF16)
                    sink_rows.append(jnp.zeros((1, _LANES), F32) - jnp.sum(p_sink * delta))
                    dq_pair = dq_pair + _dot(ds, kvar[j][e])
                    dk_acc[j][e] = dk_acc[j][e] + _dot_tn(ds, q_pair)
                    dv_acc[j][e] = dv_acc[j][e] + _dot_tn(p.astype(BF16), do_pair)
                dq_ref[:, i * _LANES:(i + 1) * _LANES] = dq_pair
            dsink_ref[...] += jnp.concatenate(sink_rows, axis=0)
            low = lax.broadcasted_iota(jnp.int32, (2 * BLOCK, _LANES), 1) < HEAD_DIM

            def merge(acc):
                return jnp.where(low, acc[0][0] + pltpu.roll(acc[0][1], HEAD_DIM, axis=1),
                                 acc[1][1] + pltpu.roll(acc[1][0], HEAD_DIM, axis=1))
            dk = merge(dk_acc)
            dv = merge(dv_acc)
            dk_ref[...] = kcarry_ref[...] + dk[:BLOCK, :]
            dv_ref[...] = vcarry_ref[...] + dv[:BLOCK, :]
            kcarry_ref[...] = dk[BLOCK:, :]
            vcarry_ref[...] = dv[BLOCK:, :]

        @pl.when(b == nb)
        def _():
            dk_ref[...] = kcarry_ref[...]
            dv_ref[...] = vcarry_ref[...]

    cur = lambda b: (jnp.minimum(b, nb - 1), 0)
    prev = lambda b: (jnp.clip(b - 1, 0, nb - 1), 0)
    return pl.pallas_call(
        body, name="attn_bwd", grid=(nb + 1,),
        in_specs=[pl.BlockSpec((BLOCK, ATTN_WIDTH), lambda b: (jnp.minimum(b, nb - 1), 1)),
                  pl.BlockSpec((BLOCK, ATTN_WIDTH), cur), pl.BlockSpec((BLOCK, ATTN_WIDTH), cur),
                  pl.BlockSpec((BLOCK, KV_WIDTH), prev), pl.BlockSpec((BLOCK, KV_WIDTH), cur),
                  pl.BlockSpec((BLOCK, KV_WIDTH), prev), pl.BlockSpec((BLOCK, KV_WIDTH), cur),
                  pl.BlockSpec(memory_space=pltpu.SMEM), _full((1, ATTN_WIDTH))],
        out_specs=[pl.BlockSpec((BLOCK, ATTN_WIDTH), cur),
                   pl.BlockSpec((BLOCK, KV_WIDTH), prev), pl.BlockSpec((BLOCK, KV_WIDTH), prev),
                   _full((N_HEADS, _LANES)), _full((1, ATTN_WIDTH))],
        out_shape=[jax.ShapeDtypeStruct((S, ATTN_WIDTH), F32), jax.ShapeDtypeStruct((S, KV_WIDTH), F32),
                   jax.ShapeDtypeStruct((S, KV_WIDTH), F32), jax.ShapeDtypeStruct((N_HEADS, _LANES), F32),
                   jax.ShapeDtypeStruct((1, ATTN_WIDTH), F32)],
        scratch_shapes=[pltpu.VMEM((BLOCK, KV_WIDTH), F32), pltpu.VMEM((BLOCK, KV_WIDTH), F32)],
        compiler_params=_params(dimension_semantics=("arbitrary",)),
    )(dy, o, q, k, k, v, v, sinks, g_attn)


def _mixer_out_fwd(y_pool, y_attn, x, w_out, mine, post, deps=()):
    S = x.shape[0]
    tm = MIXER_TILE

    def body(yp_ref, ya_ref, x_ref, w_land, w_own, mine_ref, post_ref, out_ref, m_ref, y_ref, w_v, sem):
        _load_once(_gathered(w_land, w_own, w_v, mine_ref[0], rows=D_MODEL // N_SHARD), sem)
        y_ref[:, :POOL_WIDTH] = yp_ref[...]
        y_ref[:, POOL_WIDTH:] = ya_ref[...]
        m = _dot(y_ref[...], w_v[...])
        m_ref[...] = m
        out_ref[...] = x_ref[...] + (m * _rstd(m)) * post_ref[...]

    args = [y_pool, y_attn, x, *w_out, mine, post]
    return pl.pallas_call(
        _ignoring(body, len(args), len(deps)), name="mixer_out_fwd", grid=(S // tm,),
        in_specs=[_rows(tm, POOL_WIDTH), _rows(tm, ATTN_WIDTH), _rows(tm, D_MODEL), _ANY, _ANY, _SMEM,
                  _full((1, D_MODEL))] + [_ANY] * len(deps),
        out_specs=[_rows(tm, D_MODEL), _rows(tm, D_MODEL), _rows(tm, D_MODEL)],
        out_shape=[jax.ShapeDtypeStruct((S, D_MODEL), F32), jax.ShapeDtypeStruct((S, D_MODEL), F32),
                   jax.ShapeDtypeStruct((S, D_MODEL), BF16)],
        scratch_shapes=[pltpu.VMEM((D_MODEL, D_MODEL), BF16), pltpu.SemaphoreType.DMA((N_SHARD,))],
        compiler_params=_params(dimension_semantics=("arbitrary",)),
    )(*args, *deps)


def _mixer_out_bwd(dout, m, w_out, mine, post, deps=()):
    S = m.shape[0]
    tm = MIXER_TILE

    def body(dout_ref, m_ref, w_land, w_own, mine_ref, post_ref, dy_ref, dm_ref, dpost_ref, w_v, sem):
        _load_once(_gathered(w_land, w_own, w_v, mine_ref[0], rows=D_MODEL // N_SHARD), sem)

        @pl.when(pl.program_id(0) == 0)
        def _():
            dpost_ref[...] = jnp.zeros_like(dpost_ref)

        mv = m_ref[...]
        r = _rstd(mv)
        mn = mv * r
        dv = dout_ref[...]
        dpost_ref[...] += jnp.sum(dv * mn, axis=0, keepdims=True)
        dm = _norm_bwd(dv, mn, r, post_ref[...]).astype(BF16)
        dm_ref[...] = dm
        dy_ref[...] = _dot_nt(dm, w_v[...])

    args = [dout, m, *w_out, mine, post]
    return pl.pallas_call(
        _ignoring(body, len(args), len(deps)), name="mixer_out_bwd", grid=(S // tm,),
        in_specs=[_rows(tm, D_MODEL), _rows(tm, D_MODEL), _ANY, _ANY, _SMEM, _full((1, D_MODEL))] + [_ANY] * len(deps),
        out_specs=[_rows(tm, D_MODEL), _rows(tm, D_MODEL), _full((1, D_MODEL))],
        out_shape=[jax.ShapeDtypeStruct((S, D_MODEL), F32), jax.ShapeDtypeStruct((S, D_MODEL), BF16),
                   jax.ShapeDtypeStruct((1, D_MODEL), F32)],
        scratch_shapes=[pltpu.VMEM((D_MODEL, D_MODEL), BF16), pltpu.SemaphoreType.DMA((N_SHARD,))],
        compiler_params=_params(dimension_semantics=("arbitrary",)),
    )(*args, *deps)


def _inv_freq_row():
    inv_freq = ROPE_THETA ** (-jnp.arange(0, ROT_DIM, 2, dtype=F32) / ROT_DIM)
    per_head = jnp.concatenate([inv_freq, inv_freq, jnp.zeros((HEAD_DIM - ROT_DIM,), F32)])
    return jnp.tile(per_head, _LANES // HEAD_DIM).reshape(1, _LANES)


def _local_step(x, pos, target, small, mine, weights_of, grads_ready):
    invf = _inv_freq_row()
    wgu1, wd1 = weights_of("ffn1", ())
    x1, g1, u1, f1 = _ffn_fwd(x, small["ffn1_pre"], small["ffn1_post"], wgu1, wd1, mine)
    w_in_t, w_out = weights_of("mixer", (x1,))
    u, q, k, v = _mixer_in_fwd(x1, small["mix_pre"], w_in_t, mine, pos, invf)
    y_pool = _pool_fwd(u, small["w_pool"], small["pool_scale"], small["g_pool"])
    o, y_attn = _attn_fwd(q, k, v, small["sinks"], small["g_attn"])
    x2, m, y = _mixer_out_fwd(y_pool, y_attn, x1, w_out, mine, small["mix_post"])
    wgu2, wd2 = weights_of("ffn2", (x2,))
    dx3, g2, u2, f2, loss_acc = _ffn_fwd(x2, small["ffn2_pre"], small["ffn2_post"], wgu2, wd2, mine, target=target)
    loss = loss_acc[0, 0] * (0.5 / D_MODEL)
    grads = {}
    dx2, h3, dgu2, a2, df2, grads["ffn2_pre"], grads["ffn2_post"] = _ffn_bwd(
        dx3, x2, f2, g2, u2, small["ffn2_pre"], small["ffn2_post"], wgu2, wd2, mine)
    dwgu2 = _wgrad(h3, dgu2, D_MODEL, FF_CHUNK, "wgrad_gu2", column_shards=True)
    dwd2 = _wgrad(a2, df2, FF_CHUNK, D_MODEL, "wgrad_down2")
    deps = grads_ready("ffn2", {"ffn2_w_gu": dwgu2, "ffn2_w_down": dwd2})
    dy, dm, grads["mix_post"] = _mixer_out_bwd(dx2, m, w_out, mine, small["mix_post"], deps=deps)
    dw_out = _wgrad(y, dm, D_MODEL, D_MODEL, "wgrad_out")
    dq, dk, dv, dsinks, grads["g_attn"] = _attn_bwd(dy, o, q, k, v, small["sinks"], small["g_attn"])
    grads["sinks"] = dsinks[:, 0].reshape(1, N_HEADS)
    du, grads["w_pool"], grads["pool_scale"], grads["g_pool"] = _pool_bwd(
        dy, u, small["w_pool"], small["pool_scale"], small["g_pool"])
    dx1, dz, h2, grads["mix_pre"] = _mixer_in_bwd(dx2, x1, small["mix_pre"], w_in_t, mine, du, dq, dk, dv, pos, invf)
    dw_in_t = _wgrad(dz, h2, IN_WIDTH, D_MODEL, "wgrad_in")
    deps = grads_ready("mixer", {"w_in": dw_in_t, "w_out": dw_out})
    dx, h1, dgu1, a1, df1, grads["ffn1_pre"], grads["ffn1_post"] = _ffn_bwd(
        dx1, x, f1, g1, u1, small["ffn1_pre"], small["ffn1_post"], wgu1, wd1, mine, deps=deps)
    deps = grads_ready("small", grads)
    dwgu1 = _wgrad(h1, dgu1, D_MODEL, FF_CHUNK, "wgrad_gu1", column_shards=True, deps=deps)
    deps = grads_ready("ffn1_gu", {"ffn1_w_gu": dwgu1})
    dwd1 = _wgrad(a1, df1, FF_CHUNK, D_MODEL, "wgrad_down1", deps=deps)
    grads_ready("ffn1_down", {"ffn1_w_down": dwd1})
    return loss, dx, grads


def _place():
    return lax.axis_index("x"), lax.axis_index("y"), lax.axis_index("c")


def _other_chips(x, y):
    return [(1 - x, y), (x, 1 - y), (1 - x, 1 - y)]


def _hbm_shape(shape, dtype):
    return jax.ShapeDtypeStruct(shape, dtype)


_HBM = pl.BlockSpec(memory_space=pltpu.HBM)
_SEM = pl.BlockSpec(memory_space=pltpu.SEMAPHORE)
_EFFECT = pltpu.SideEffectType.DATAFLOW_SIDE_EFFECTING
GATHER, GATHER_HALF, REDUCE, BROADCAST = "gather", "gather_half", "reduce", "broadcast"
N_DEVICES = 8


def _in_hbm(a):
    return pltpu.with_memory_space_constraint(a, pltpu.HBM)


def _core_half(rows, c):
    return pl.ds(pl.multiple_of(c * (rows // 2), 16), rows // 2)


def _chip_copies(kind, srcs, lands, send_sems, recv_sems):
    x, y, c = _place()
    mine = 2 * x + y
    copies = []
    for w in range(len(srcs)):
        if kind == BROADCAST:
            peers = [(x ^ (k >> 2), y ^ ((k >> 1) & 1), c ^ (k & 1)) for k in range(1, N_DEVICES)]
        else:
            peers = [(px, py, c) for px, py in _other_chips(x, y)]
        for k, (px, py, pc) in enumerate(peers):
            if kind == GATHER:
                src, dst = srcs[w], lands[w].at[mine]
            elif kind == GATHER_HALF:
                half = _core_half(srcs[w].shape[0], c)
                src, dst = srcs[w].at[half, :], lands[w].at[mine, half, :]
            elif kind == BROADCAST:
                src, dst = srcs[w], lands[w].at[2 * mine + c]
            else:
                src, dst = srcs[w].at[2 * px + py], lands[w].at[k]
            pair = len(peers) * w + k
            copies.append(pltpu.make_async_remote_copy(
                src_ref=src, dst_ref=dst, send_sem=send_sems.at[pair], recv_sem=recv_sems.at[pair],
                device_id=(px, py, pc), device_id_type=MESH))
    return copies


def _landing_shape(kind, src):
    if kind == REDUCE:
        return (N_SHARD - 1,) + src.shape[1:]
    return ((N_DEVICES if kind == BROADCAST else N_SHARD),) + src.shape


def _peer_count(kind):
    return N_DEVICES - 1 if kind == BROADCAST else N_SHARD - 1


def _exchange_start(kinds, groups, name):
    sizes = [len(g) for g in groups]
    flat = [s for g in groups for s in g]
    n, ng = len(flat), len(groups)

    def body(*refs):
        srcs, lands = refs[:n], refs[n:2 * n]
        sems = refs[2 * n:2 * n + 2 * ng]
        token = refs[-1]
        start = 0
        for gi, size in enumerate(sizes):
            for cp in _chip_copies(kinds[gi], srcs[start:start + size], lands[start:start + size],
                                   sems[2 * gi], sems[2 * gi + 1]):
                cp.start()
            start += size
        token[...] = jnp.zeros_like(token)

    landings = [lax.empty(_landing_shape(kind, s), s.dtype) for kind, g in zip(kinds, groups) for s in g]
    sem_shapes = [pltpu.SemaphoreType.DMA((size * _peer_count(kind),)) for kind, size in zip(kinds, sizes)
                  for _ in range(2)]
    outs = pl.pallas_call(
        body, name=name,
        in_specs=[_HBM] * (2 * n),
        out_specs=[_SEM] * (2 * ng) + [_HBM] * (2 * n) + [pl.BlockSpec(memory_space=pltpu.VMEM)],
        out_shape=sem_shapes + [pltpu.HBM(a.shape, a.dtype) for a in flat + landings]
        + [jax.ShapeDtypeStruct((8, _LANES), F32)],
        input_output_aliases={i: 2 * ng + i for i in range(2 * n)},
        compiler_params=pltpu.CompilerParams(has_side_effects=_EFFECT),
    )(*[_in_hbm(a) for a in flat + landings])
    sems, srcs, lands, token = outs[:2 * ng], outs[2 * ng:2 * ng + n], outs[2 * ng + n:2 * ng + 2 * n], outs[-1]
    handles, start = [], 0
    for gi, size in enumerate(sizes):
        handles.append((sems[2 * gi], sems[2 * gi + 1], srcs[start:start + size], lands[start:start + size]))
        start += size
    return handles, token


def _exchange_wait(kind, handle, after, name):
    send_sems, recv_sems, srcs, lands = handle
    n = len(srcs)

    def body(*refs):
        copies = _chip_copies(kind, refs[:n], refs[n:2 * n], refs[2 * n], refs[2 * n + 1])
        for cp in copies:
            cp.wait_send()
        for cp in copies:
            cp.wait_recv()

    outs = pl.pallas_call(
        body, name=name,
        in_specs=[_HBM] * (2 * n) + [_SEM, _SEM] + [_ANY] * len(after),
        out_specs=[_HBM] * (2 * n),
        out_shape=[pltpu.HBM(a.shape, a.dtype) for a in list(srcs) + list(lands)],
        input_output_aliases={i: i for i in range(2 * n)},
        compiler_params=pltpu.CompilerParams(has_side_effects=_EFFECT),
    )(*srcs, *lands, send_sems, recv_sems, *after)
    return outs[:n], outs[n:]


def _swap_gathered_halves(lands, name):
    n = len(lands)

    def body(*refs):
        bufs = refs[n:2 * n]
        send_sems, recv_sems = refs[2 * n:]
        x, y, c = _place()
        mine = 2 * x + y
        sends, arrivals = [], []
        for w in range(n):
            rows = bufs[w].shape[1]
            for d in range(1, N_SHARD):
                slot = (mine + d) % N_SHARD
                sems = dict(send_sem=send_sems.at[(N_SHARD - 1) * w + d - 1],
                            recv_sem=recv_sems.at[(N_SHARD - 1) * w + d - 1],
                            device_id=(x, y, 1 - c), device_id_type=MESH)
                fetched = bufs[w].at[slot, _core_half(rows, c), :]
                missing = bufs[w].at[slot, _core_half(rows, 1 - c), :]
                sends.append(pltpu.make_async_remote_copy(src_ref=fetched, dst_ref=fetched, **sems))
                arrivals.append(pltpu.make_async_remote_copy(src_ref=missing, dst_ref=missing, **sems))
        for cp in sends:
            cp.start()
        for cp in arrivals:
            cp.wait_recv()
        for cp in sends:
            cp.wait_send()

    return pl.pallas_call(
        body, name=name, in_specs=[_ANY] * n, out_specs=[_ANY] * n,
        out_shape=[_hbm_shape(a.shape, a.dtype) for a in lands],
        input_output_aliases={i: i for i in range(n)},
        scratch_shapes=[pltpu.SemaphoreType.DMA((n * (N_SHARD - 1),)), pltpu.SemaphoreType.DMA((n * (N_SHARD - 1),))],
        compiler_params=pltpu.CompilerParams(has_side_effects=True),
    )(*lands)


def _swap_with_sibling(partials, name):
    n = len(partials)

    def body(*refs):
        ins, outs = refs[:n], refs[n:2 * n]
        send_sems, recv_sems = refs[2 * n:]
        x, y, c = _place()
        sends = [pltpu.make_async_remote_copy(
            src_ref=ins[w], dst_ref=outs[w], send_sem=send_sems.at[w], recv_sem=recv_sems.at[w],
            device_id=(x, y, 1 - c), device_id_type=MESH) for w in range(n)]
        for cp in sends:
            cp.start()
        for cp in sends:
            cp.wait_recv()
        for cp in sends:
            cp.wait_send()

    return pl.pallas_call(
        body, name=name,
        in_specs=[_ANY] * n, out_specs=[_ANY] * n,
        out_shape=[_hbm_shape(p.shape, p.dtype) for p in partials],
        scratch_shapes=[pltpu.SemaphoreType.DMA((n,)), pltpu.SemaphoreType.DMA((n,))],
        compiler_params=pltpu.CompilerParams(has_side_effects=True),
    )(*partials)


def _row_block(rows, cap):
    best = None
    for cand in range(16, min(rows, cap) + 1, 16):
        if rows % cand == 0:
            best = cand
    assert best is not None, rows
    return best


def _chip_partial(own, received, shard, name):
    _, R, C = own.shape
    rb = _row_block(R, 512)

    def body(shard_ref, own_ref, rec_ref, out_ref):
        acc = own_ref[...]
        for k in range(3):
            acc = acc + rec_ref[k].astype(F32)
        out_ref[...] = acc

    return pl.pallas_call(
        body, name=name,
        grid_spec=pltpu.PrefetchScalarGridSpec(
            num_scalar_prefetch=1, grid=(R // rb,),
            in_specs=[pl.BlockSpec((None, rb, C), lambda i, s: (s[0], i, 0)),
                      pl.BlockSpec((3, rb, C), lambda i, s: (0, i, 0))],
            out_specs=pl.BlockSpec((rb, C), lambda i, s: (i, 0))),
        out_shape=jax.ShapeDtypeStruct((R, C), F32),
        compiler_params=_params(dimension_semantics=("arbitrary",)),
    )(shard, own, received)


def _adamw(w, m, v, g_parts, name, slot=None):
    R, C = w.shape
    by_device = slot is not None
    rb = _row_block(R, 256) if R % 16 == 0 else R

    def body(w_ref, m_ref, v_ref, *refs):
        g_refs, (grad_ref, delta_ref, m_out, v_out) = refs[:-4], refs[-4:]
        if by_device:
            own_ref, land_ref, slot_ref = g_refs
            part = lambda d: jnp.where(slot_ref[0] == d, own_ref[...], land_ref[d])
            g = part(0)
            for d in range(1, N_DEVICES):
                g = g + part(d)
        else:
            g = g_refs[0][...]
            for g_ref in g_refs[1:]:
                g = g + g_ref[...]
        grad_ref[...] = g
        new_m = ADAM_B1 * m_ref[...] + (1.0 - ADAM_B1) * g
        new_v = ADAM_B2 * v_ref[...] + (1.0 - ADAM_B2) * (g * g)
        m_hat = new_m / (1.0 - ADAM_B1 ** ADAM_STEP)
        v_hat = new_v / (1.0 - ADAM_B2 ** ADAM_STEP)
        delta_ref[...] = -ADAM_LR * (m_hat / (jnp.sqrt(v_hat) + ADAM_EPS) + ADAM_WD * w_ref[...])
        m_out[...] = new_m
        v_out[...] = new_v

    spec = pl.BlockSpec((rb, C), lambda i: (i, 0))
    if by_device:
        g_specs = [spec, pl.BlockSpec((N_DEVICES, rb, C), lambda i: (0, i, 0)), _SMEM]
        g_parts = list(g_parts) + [slot]
    else:
        g_specs = [spec] * len(g_parts)
    return pl.pallas_call(
        body, name=name, grid=(R // rb,),
        in_specs=[spec, spec, spec] + g_specs,
        out_specs=[spec] * 4,
        out_shape=[jax.ShapeDtypeStruct((R, C), F32)] * 4,
        compiler_params=_params(dimension_semantics=("arbitrary",)),
    )(w, m, v, *g_parts)


SMALL_NAMES = ("ffn1_pre", "ffn1_post", "mix_pre", "pool_scale", "sinks", "g_pool", "g_attn", "mix_post",
               "ffn2_pre", "ffn2_post", "w_pool")
_SLAB_PART = 8 * _LANES


def _to_slab(parts):
    rows = []
    for name in SMALL_NAMES:
        flat = parts[name].reshape(-1)
        padded = -(-flat.shape[0] // _SLAB_PART) * _SLAB_PART
        rows.append(jnp.pad(flat, (0, padded - flat.shape[0])).reshape(-1, _LANES))
    return jnp.concatenate(rows, axis=0)


def _from_slab(slab, like):
    out, row = {}, 0
    for name in SMALL_NAMES:
        size = like[name].size
        rows = -(-size // _SLAB_PART) * (_SLAB_PART // _LANES)
        out[name] = slab[row:row + rows].reshape(-1)[:size].reshape(like[name].shape)
        row += rows
    return out


BIG_NAMES = ("ffn1_w_gu", "ffn1_w_down", "w_in", "w_out", "ffn2_w_gu", "ffn2_w_down")
WEIGHT_ORDER = ("ffn1_pre", "ffn1_w_gu", "ffn1_w_down", "ffn1_post", "mix_pre", "w_in", "w_pool", "pool_scale",
                "sinks", "g_pool", "g_attn", "w_out", "mix_post", "ffn2_pre", "ffn2_w_gu", "ffn2_w_down", "ffn2_post")


def kernel(x, positions, ffn1_pre, ffn1_w_gu, ffn1_w_down, ffn1_post, mix_pre, w_in, w_pool, pool_scale, sinks, g_pool, g_attn, w_out, mix_post, ffn2_pre, ffn2_w_gu, ffn2_w_down, ffn2_post, loss_target, m_ffn1_pre, m_ffn1_w_gu, m_ffn1_w_down, m_ffn1_post, m_mix_pre, m_w_in, m_w_pool, m_pool_scale, m_sinks, m_g_pool, m_g_attn, m_w_out, m_mix_post, m_ffn2_pre, m_ffn2_w_gu, m_ffn2_w_down, m_ffn2_post, v_ffn1_pre, v_ffn1_w_gu, v_ffn1_w_down, v_ffn1_post, v_mix_pre, v_w_in, v_w_pool, v_pool_scale, v_sinks, v_g_pool, v_g_attn, v_w_out, v_mix_post, v_ffn2_pre, v_ffn2_w_gu, v_ffn2_w_down, v_ffn2_post):
    given = dict(locals())
    weights = {n: given[n][0] for n in WEIGHT_ORDER}
    moments_m = {n: given["m_" + n][0] for n in WEIGHT_ORDER}
    moments_v = {n: given["v_" + n][0] for n in WEIGHT_ORDER}
    S = x.shape[1]
    shard = (2 * lax.axis_index("x") + lax.axis_index("y")).astype(jnp.int32).reshape(1)

    local16 = {n: weights[n].astype(BF16) for n in BIG_NAMES if n != "w_in"}
    local16["w_in"] = weights["w_in"].T.astype(BF16)
    gather_groups = {"ffn1": ("ffn1_w_gu", "ffn1_w_down"), "mixer": ("w_in", "w_out"),
                     "ffn2": ("ffn2_w_gu", "ffn2_w_down")}
    gather_kinds = {"ffn1": GATHER_HALF, "mixer": GATHER, "ffn2": GATHER}
    handles, _ = _exchange_start(list(gather_kinds.values()),
                                 [[local16[n] for n in names] for names in gather_groups.values()], "gather_start")
    gather_handles = dict(zip(gather_groups, handles))

    def weights_of(group, after):
        kind = gather_kinds[group]
        owns, lands = _exchange_wait(kind, gather_handles[group], list(after), "gather_wait_" + group)
        if kind == GATHER_HALF:
            lands = _swap_gathered_halves(lands, "swap_gathered_" + group)
        return list(zip(lands, owns))

    pending, last_token = {}, []

    def grads_ready(group, grads):
        if group == "small":
            (handle,), token = _exchange_start([BROADCAST], [[_to_slab(grads)]], "reduce_start_small")
            pending[group] = handle
            return [token]
        names = list(grads)
        (handle,), token = _exchange_start([REDUCE], [[grads[n][1] for n in names]], "reduce_start_" + group)
        pending[group] = (names, handle, [grads[n][0] for n in names])
        last_token[:] = [token]
        return [token]

    small = {n: (weights[n] if weights[n].ndim > 1 else weights[n].reshape(1, -1)) for n in SMALL_NAMES}
    loss, dx, small_grads = _local_step(x[0], positions.reshape(S, 1), loss_target[0], small, shard,
                                        weights_of, grads_ready)
    loss = lax.psum(loss, ("x", "y", "c"))

    grad, delta, new_m, new_v = {}, {}, {}, {}

    def finish(groups, after):
        names, partials = [], []
        for group in groups:
            group_names, handle, own32 = pending[group]
            _, received = _exchange_wait(REDUCE, handle, after, "reduce_wait_" + group)
            names += group_names
            partials += [_chip_partial(g32, rec, shard, "chip_partial_" + n)
                         for n, g32, rec in zip(group_names, own32, received)]
        siblings = _swap_with_sibling(partials, "swap_" + groups[0])
        for name, mine, theirs in zip(names, partials, siblings):
            if name == "w_in":
                mine, theirs = mine.T, theirs.T
            grad[name], delta[name], new_m[name], new_v[name] = _adamw(
                weights[name], moments_m[name], moments_v[name], [mine, theirs], "adamw_" + name)
        return [grad[names[-1]]]

    after = finish(["ffn2"], last_token)
    after = finish(["mixer"], after)
    after = finish(["ffn1_gu", "ffn1_down"], after)
    (own_slab,), (slab_landing,) = _exchange_wait(BROADCAST, pending["small"], after, "reduce_wait_small")
    device = (2 * shard + lax.axis_index("c")).astype(jnp.int32)
    small_like = {n: small[n] for n in SMALL_NAMES}
    slabs = _adamw(_to_slab(small), _to_slab({n: moments_m[n] for n in SMALL_NAMES}),
                   _to_slab({n: moments_v[n] for n in SMALL_NAMES}), [own_slab, slab_landing], "adamw_small",
                   slot=device)
    for store, slab in zip((grad, delta, new_m, new_v), slabs):
        store.update(_from_slab(slab, small_like))

    def out(store):
        return [store[n].reshape(given[n].shape) for n in WEIGHT_ORDER]
    return (loss, dx[None], *out(grad), *out(delta), *out(new_m), *out(new_v))
```

```python
import functools

import jax
import jax.numpy as jnp
from jax import lax
from jax.experimental import pallas as pl
from jax.experimental.pallas import tpu as pltpu

F32 = jnp.float32
BF16 = jnp.bfloat16

D_MODEL = 1024
D_FF = 2816
N_SHARD = 4
FF_CHUNK = D_FF // 2
POOL_WINDOWS = (2, 4, 8, 16)
POOL_WIDTH = 512
POOL_GROUP = 128
HALO = 16
HEAD_DIM = 64
N_HEADS = 8
N_KV_HEADS = 2
ATTN_WIDTH = 512
KV_WIDTH = 128
IN_WIDTH = 1280
BLOCK = 128
ROT_DIM = 16
ROPE_THETA = 500000.0
EPS = 1e-6
NEG_INF = -1e30
ATTN_SCALE = HEAD_DIM ** -0.5

ADAM_LR = 0.001
ADAM_B1 = 0.9
ADAM_B2 = 0.999
ADAM_EPS = 1e-08
ADAM_WD = 0.01
ADAM_STEP = 10

VMEM_LIMIT = 60 * 1024 * 1024
FFN_FWD_TILE = 512
FFN_BWD_TILE = 256
MIXER_TILE = 512

MESH = pl.DeviceIdType.MESH


def _params(**kw):
    return pltpu.CompilerParams(vmem_limit_bytes=VMEM_LIMIT, **kw)


def _dot(a, b):
    return jnp.dot(a, b, preferred_element_type=F32)


def _dot_nt(a, b):
    return lax.dot_general(a, b, (((1,), (1,)), ((), ())), preferred_element_type=F32)


def _dot_tn(a, b):
    return lax.dot_general(a, b, (((0,), (0,)), ((), ())), preferred_element_type=F32)


def _rstd(x):
    return lax.rsqrt(jnp.mean(x * x, axis=-1, keepdims=True) + EPS)


def _norm_bwd(dy, xn, r, gain):
    dxn = dy * gain
    return r * (dxn - xn * jnp.mean(dxn * xn, axis=-1, keepdims=True))


def _sigmoid(x):
    return 1.0 / (1.0 + jnp.exp(-x))


def _full(shape):
    return pl.BlockSpec(shape, lambda *_: (0,) * len(shape))


def _rows(tile, width, col=0):
    return pl.BlockSpec((tile, width), lambda i: (i, col))


_ANY = pl.BlockSpec(memory_space=pl.ANY)


_SMEM = pl.BlockSpec(memory_space=pltpu.SMEM)


def _load_once(pairs, sem):
    @pl.when(pl.program_id(0) == 0)
    def _():
        copies = [pltpu.make_async_copy(src, dst, sem.at[n]) for n, (src, dst) in enumerate(pairs)]
        for cp in copies:
            cp.start()
        for cp in copies:
            cp.wait()


def _gathered(land_ref, own_ref, vmem_ref, mine, rows=None):
    def dst(slot):
        if rows is None:
            return vmem_ref.at[slot]
        return vmem_ref.at[pl.ds(pl.multiple_of(slot * rows, 16), rows), :]
    pairs = [(land_ref.at[(mine + d) % N_SHARD], dst((mine + d) % N_SHARD)) for d in range(1, N_SHARD)]
    return pairs + [(own_ref, dst(mine))]


def _ignoring(body, start, count):
    def wrapped(*refs):
        return body(*refs[:start], *refs[start + count:])
    return wrapped


def _ffn_fwd(x, pre, post, wgu, wd, mine, target=None, deps=()):
    S = x.shape[0]
    tm = FFN_FWD_TILE
    with_loss = target is not None

    def body(*refs):
        if with_loss:
            (x_ref, pre_ref, post_ref, wgu_land, wgu_own, wd_land, wd_own, mine_ref, tgt_ref,
             out_ref, g_ref, u_ref, f_ref, loss_ref, wgu_v, wd_v, sem) = refs
        else:
            (x_ref, pre_ref, post_ref, wgu_land, wgu_own, wd_land, wd_own, mine_ref,
             out_ref, g_ref, u_ref, f_ref, wgu_v, wd_v, sem) = refs
        _load_once(_gathered(wgu_land, wgu_own, wgu_v, mine_ref[0])
                   + _gathered(wd_land, wd_own, wd_v, mine_ref[0], rows=D_FF // N_SHARD), sem)
        xv = x_ref[...]
        h = ((xv * _rstd(xv)) * pre_ref[...]).astype(BF16)
        facc = jnp.zeros((tm, D_MODEL), F32)
        for c in range(2):
            cols = slice(c * FF_CHUNK, (c + 1) * FF_CHUNK)
            g = _dot(h, wgu_v[c])
            u = _dot(h, wgu_v[2 + c])
            g_ref[:, cols] = g.astype(BF16)
            u_ref[:, cols] = u.astype(BF16)
            a = (g * _sigmoid(g)) * u
            facc = facc + _dot(a.astype(BF16), wd_v[cols, :])
        f_ref[...] = facc
        out = xv + 0.5 * ((facc * _rstd(facc)) * post_ref[...])
        if with_loss:
            diff = out - tgt_ref[...]
            out_ref[...] = diff * (1.0 / D_MODEL)

            @pl.when(pl.program_id(0) == 0)
            def _():
                loss_ref[...] = jnp.zeros_like(loss_ref)
            loss_ref[...] += jnp.sum(diff * diff)
        else:
            out_ref[...] = out

    in_specs = [_rows(tm, D_MODEL), _full((1, D_MODEL)), _full((1, D_MODEL)), _ANY, _ANY, _ANY, _ANY, _SMEM]
    args = [x, pre, post, *wgu, *wd, mine]
    out_shape = [jax.ShapeDtypeStruct((S, D_MODEL), F32), jax.ShapeDtypeStruct((S, D_FF), BF16),
                 jax.ShapeDtypeStruct((S, D_FF), BF16), jax.ShapeDtypeStruct((S, D_MODEL), F32)]
    out_specs = [_rows(tm, D_MODEL), _rows(tm, D_FF), _rows(tm, D_FF), _rows(tm, D_MODEL)]
    if with_loss:
        in_specs.append(_rows(tm, D_MODEL))
        args.append(target)
        out_shape.append(jax.ShapeDtypeStruct((8, 128), F32))
        out_specs.append(_full((8, 128)))
    return pl.pallas_call(
        _ignoring(body, len(args), len(deps)), name="ffn_fwd_loss" if with_loss else "ffn_fwd",
        grid=(S // tm,), in_specs=in_specs + [_ANY] * len(deps), out_specs=out_specs, out_shape=out_shape,
        scratch_shapes=[pltpu.VMEM((N_SHARD, D_MODEL, FF_CHUNK), BF16), pltpu.VMEM((D_FF, D_MODEL), BF16),
                        pltpu.SemaphoreType.DMA((2 * N_SHARD,))],
        compiler_params=_params(dimension_semantics=("arbitrary",)),
    )(*args, *deps)


def _ffn_bwd(dout, x, f, g, u, pre, post, wgu, wd, mine, deps=()):
    S = x.shape[0]
    tm = FFN_BWD_TILE

    def body(dout_ref, x_ref, f_ref, g_ref, u_ref, pre_ref, post_ref, wgu_land, wgu_own, wd_land, wd_own, mine_ref,
             dx_ref, h_ref, dgu_ref, a_ref, df_ref, dpre_ref, dpost_ref, wgu_v, wd_v, sem):
        _load_once(_gathered(wgu_land, wgu_own, wgu_v, mine_ref[0])
                   + _gathered(wd_land, wd_own, wd_v, mine_ref[0], rows=D_FF // N_SHARD), sem)

        @pl.when(pl.program_id(0) == 0)
        def _():
            dpre_ref[...] = jnp.zeros_like(dpre_ref)
            dpost_ref[...] = jnp.zeros_like(dpost_ref)

        dout_v = dout_ref[...]
        dn = 0.5 * dout_v
        fv = f_ref[...]
        rf = _rstd(fv)
        fn = fv * rf
        dpost_ref[...] += jnp.sum(dn * fn, axis=0, keepdims=True)
        df = _norm_bwd(dn, fn, rf, post_ref[...]).astype(BF16)
        df_ref[...] = df
        dh = jnp.zeros((tm, D_MODEL), F32)
        for c in range(2):
            cols = slice(c * FF_CHUNK, (c + 1) * FF_CHUNK)
            da = _dot_nt(df, wd_v[cols, :])
            gv = g_ref[:, cols].astype(F32)
            uv = u_ref[:, cols].astype(F32)
            sg = _sigmoid(gv)
            silu = gv * sg
            a_ref[:, cols] = (silu * uv).astype(BF16)
            dg = ((da * uv) * (sg * (1.0 + gv * (1.0 - sg)))).astype(BF16)
            du = (da * silu).astype(BF16)
            dgu_ref[:, cols] = dg
            dgu_ref[:, 2 * FF_CHUNK + c * FF_CHUNK:2 * FF_CHUNK + (c + 1) * FF_CHUNK] = du
            dh = dh + _dot_nt(dg, wgu_v[c]) + _dot_nt(du, wgu_v[2 + c])
        xv = x_ref[...]
        rx = _rstd(xv)
        xn = xv * rx
        h_ref[...] = (xn * pre_ref[...]).astype(BF16)
        dpre_ref[...] += jnp.sum(dh * xn, axis=0, keepdims=True)
        dx_ref[...] = dout_v + _norm_bwd(dh, xn, rx, pre_ref[...])

    args = [dout, x, f, g, u, pre, post, *wgu, *wd, mine]
    return pl.pallas_call(
        _ignoring(body, len(args), len(deps)), name="ffn_bwd", grid=(S // tm,),
        in_specs=[_rows(tm, D_MODEL), _rows(tm, D_MODEL), _rows(tm, D_MODEL), _rows(tm, D_FF), _rows(tm, D_FF),
                  _full((1, D_MODEL)), _full((1, D_MODEL)), _ANY, _ANY, _ANY, _ANY, _SMEM] + [_ANY] * len(deps),
        out_specs=[_rows(tm, D_MODEL), _rows(tm, D_MODEL), _rows(tm, 2 * D_FF), _rows(tm, D_FF), _rows(tm, D_MODEL),
                   _full((1, D_MODEL)), _full((1, D_MODEL))],
        out_shape=[jax.ShapeDtypeStruct((S, D_MODEL), F32), jax.ShapeDtypeStruct((S, D_MODEL), BF16),
                   jax.ShapeDtypeStruct((S, 2 * D_FF), BF16), jax.ShapeDtypeStruct((S, D_FF), BF16),
                   jax.ShapeDtypeStruct((S, D_MODEL), BF16),
                   jax.ShapeDtypeStruct((1, D_MODEL), F32), jax.ShapeDtypeStruct((1, D_MODEL), F32)],
        scratch_shapes=[pltpu.VMEM((N_SHARD, D_MODEL, FF_CHUNK), BF16), pltpu.VMEM((D_FF, D_MODEL), BF16),
                        pltpu.SemaphoreType.DMA((2 * N_SHARD,))],
        compiler_params=_params(dimension_semantics=("arbitrary",)),
    )(*args, *deps)


def _wgrad(lhs, rhs, m_block, n_block, name, column_shards=False, tk=2048, deps=()):
    S, M = lhs.shape
    N = rhs.shape[1]
    k_steps = S // tk

    def body(lhs_ref, rhs_ref, out_ref, out16_ref):
        k = pl.program_id(2)

        @pl.when(k == 0)
        def _():
            out_ref[...] = jnp.zeros_like(out_ref)
        out_ref[...] += _dot_tn(lhs_ref[...], rhs_ref[...])

        @pl.when(k == k_steps - 1)
        def _():
            out16_ref[...] = out_ref[...].astype(BF16)

    if column_shards:
        assert N == N_SHARD * n_block
        shape = (N_SHARD, M, n_block)
        out_spec = pl.BlockSpec((None, m_block, n_block), lambda i, j, k: (j, i, 0))
    else:
        shape = (M, N)
        out_spec = pl.BlockSpec((m_block, n_block), lambda i, j, k: (i, j))
    out, out16 = pl.pallas_call(
        _ignoring(body, 2, len(deps)), name=name, grid=(M // m_block, N // n_block, k_steps),
        in_specs=[pl.BlockSpec((tk, m_block), lambda i, j, k: (k, i)),
                  pl.BlockSpec((tk, n_block), lambda i, j, k: (k, j))] + [_ANY] * len(deps),
        out_specs=[out_spec, out_spec],
        out_shape=[jax.ShapeDtypeStruct(shape, F32), jax.ShapeDtypeStruct(shape, BF16)],
        compiler_params=_params(dimension_semantics=("arbitrary", "arbitrary", "arbitrary")),
    )(lhs, rhs, *deps)
    if not column_shards:
        out = out.reshape(N_SHARD, M // N_SHARD, N)
        out16 = out16.reshape(N_SHARD, M // N_SHARD, N)
    return out, out16


def _rope_tables(pos_ref, invf_ref):
    ang = pos_ref[...].astype(F32) * invf_ref[...]
    cos, sin = jnp.cos(ang), jnp.sin(ang)
    lane = lax.broadcasted_iota(jnp.int32, ang.shape, 1) % HEAD_DIM
    first = lane < ROT_DIM // 2
    second = (lane >= ROT_DIM // 2) & (lane < ROT_DIM)
    c = jnp.where(lane < ROT_DIM, cos, 1.0)
    s_first = jnp.where(first, sin, 0.0)
    s_second = jnp.where(second, sin, 0.0)
    return c, s_first, s_second


_HALF = ROT_DIM // 2
_LANES = 128


def _rope(t, tables):
    c, s_first, s_second = tables
    return t * c - pltpu.roll(t, _LANES - _HALF, axis=1) * s_first + pltpu.roll(t, _HALF, axis=1) * s_second


def _rope_transposed(t, tables):
    c, s_first, s_second = tables
    return t * c - pltpu.roll(t * s_first, _HALF, axis=1) + pltpu.roll(t * s_second, _LANES - _HALF, axis=1)


def _store_head_variants(ref, t):
    rolled = pltpu.roll(t, HEAD_DIM, axis=1)
    low = lax.broadcasted_iota(jnp.int32, t.shape, 1) < HEAD_DIM
    zero = jnp.zeros_like(t)
    ref[0] = jnp.where(low, t, zero).astype(BF16)
    ref[1] = jnp.where(low, zero, rolled).astype(BF16)
    ref[2] = jnp.where(low, rolled, zero).astype(BF16)
    ref[3] = jnp.where(low, zero, t).astype(BF16)


def _mixer_in_fwd(x, pre, w_in_t, mine, pos, invf, deps=()):
    S = x.shape[0]
    tm = MIXER_TILE

    def body(x_ref, pre_ref, w_land, w_own, mine_ref, pos_ref, invf_ref, u_ref, q_ref, k_ref, v_ref, w_v, sem):
        _load_once(_gathered(w_land, w_own, w_v, mine_ref[0], rows=IN_WIDTH // N_SHARD), sem)
        xv = x_ref[...]
        h = ((xv * _rstd(xv)) * pre_ref[...]).astype(BF16)
        z = _dot_nt(h, w_v[...])
        tables = _rope_tables(pos_ref, invf_ref)
        u_ref[...] = z[:, :POOL_WIDTH]
        for t in range(ATTN_WIDTH // _LANES):
            lo = POOL_WIDTH + t * _LANES
            q_ref[:, t * _LANES:(t + 1) * _LANES] = (_rope(z[:, lo:lo + _LANES], tables) * ATTN_SCALE).astype(BF16)
        kv = POOL_WIDTH + ATTN_WIDTH
        _store_head_variants(k_ref, _rope(z[:, kv:kv + KV_WIDTH], tables))
        _store_head_variants(v_ref, z[:, kv + KV_WIDTH:])

    args = [x, pre, *w_in_t, mine, pos, invf]
    variants = pl.BlockSpec((2 * N_KV_HEADS, tm, KV_WIDTH), lambda i: (0, i, 0))
    return pl.pallas_call(
        _ignoring(body, len(args), len(deps)), name="mixer_in_fwd", grid=(S // tm,),
        in_specs=[_rows(tm, D_MODEL), _full((1, D_MODEL)), _ANY, _ANY, _SMEM, _rows(tm, 1), _full((1, _LANES))]
        + [_ANY] * len(deps),
        out_specs=[_rows(tm, POOL_WIDTH), _rows(tm, ATTN_WIDTH), variants, variants],
        out_shape=[jax.ShapeDtypeStruct((S, POOL_WIDTH), F32), jax.ShapeDtypeStruct((S, ATTN_WIDTH), BF16),
                   jax.ShapeDtypeStruct((2 * N_KV_HEADS, S, KV_WIDTH), BF16),
                   jax.ShapeDtypeStruct((2 * N_KV_HEADS, S, KV_WIDTH), BF16)],
        scratch_shapes=[pltpu.VMEM((IN_WIDTH, D_MODEL), BF16), pltpu.SemaphoreType.DMA((N_SHARD,))],
        compiler_params=_params(dimension_semantics=("arbitrary",)),
    )(*args, *deps)


def _mixer_in_bwd(dres, x, pre, w_in_t, mine, du, dq, dk, dv, pos, invf, deps=()):
    S = x.shape[0]
    tm = MIXER_TILE

    def body(dres_ref, x_ref, pre_ref, w_land, w_own, mine_ref, du_ref, dq_ref, dk_ref, dv_ref, pos_ref, invf_ref,
             dx_ref, dz_ref, h_ref, dpre_ref, w_v, sem):
        _load_once(_gathered(w_land, w_own, w_v, mine_ref[0], rows=IN_WIDTH // N_SHARD), sem)

        @pl.when(pl.program_id(0) == 0)
        def _():
            dpre_ref[...] = jnp.zeros_like(dpre_ref)

        tables = _rope_tables(pos_ref, invf_ref)
        dz_ref[:, :POOL_WIDTH] = du_ref[...].astype(BF16)
        for t in range(ATTN_WIDTH // _LANES):
            lo = POOL_WIDTH + t * _LANES
            dz_ref[:, lo:lo + _LANES] = _rope_transposed(dq_ref[:, t * _LANES:(t + 1) * _LANES], tables).astype(BF16)
        kv = POOL_WIDTH + ATTN_WIDTH
        dz_ref[:, kv:kv + KV_WIDTH] = _rope_transposed(dk_ref[...], tables).astype(BF16)
        dz_ref[:, kv + KV_WIDTH:] = dv_ref[...].astype(BF16)
        dh = _dot(dz_ref[...], w_v[...])
        xv = x_ref[...]
        rx = _rstd(xv)
        xn = xv * rx
        h_ref[...] = (xn * pre_ref[...]).astype(BF16)
        dpre_ref[...] += jnp.sum(dh * xn, axis=0, keepdims=True)
        dx_ref[...] = dres_ref[...] + _norm_bwd(dh, xn, rx, pre_ref[...])

    args = [dres, x, pre, *w_in_t, mine, du, dq, dk, dv, pos, invf]
    return pl.pallas_call(
        _ignoring(body, len(args), len(deps)), name="mixer_in_bwd", grid=(S // tm,),
        in_specs=[_rows(tm, D_MODEL), _rows(tm, D_MODEL), _full((1, D_MODEL)), _ANY, _ANY, _SMEM,
                  _rows(tm, POOL_WIDTH), _rows(tm, ATTN_WIDTH), _rows(tm, KV_WIDTH), _rows(tm, KV_WIDTH),
                  _rows(tm, 1), _full((1, _LANES))] + [_ANY] * len(deps),
        out_specs=[_rows(tm, D_MODEL), _rows(tm, IN_WIDTH), _rows(tm, D_MODEL), _full((1, D_MODEL))],
        out_shape=[jax.ShapeDtypeStruct((S, D_MODEL), F32), jax.ShapeDtypeStruct((S, IN_WIDTH), BF16),
                   jax.ShapeDtypeStruct((S, D_MODEL), BF16), jax.ShapeDtypeStruct((1, D_MODEL), F32)],
        scratch_shapes=[pltpu.VMEM((IN_WIDTH, D_MODEL), BF16), pltpu.SemaphoreType.DMA((N_SHARD,))],
        compiler_params=_params(dimension_semantics=("arbitrary",)),
    )(*args, *deps)


def _pool_counts(tile_index, tm, width):
    t = tile_index * tm + lax.broadcasted_iota(jnp.int32, (tm, 1), 0)
    return jnp.minimum(t + 1, width).astype(F32)


def _pool_features(ext, u_tile, tile_index, tm):
    ds = []
    for gi, width in enumerate(POOL_WINDOWS):
        lanes = slice(gi * POOL_GROUP, (gi + 1) * POOL_GROUP)
        s = ext[:, lanes]
        shift = 1
        while shift < width:
            s = s + pltpu.roll(s, shift, axis=0)
            shift *= 2
        ds.append(s[HALO:, :] / _pool_counts(tile_index, tm, width) - u_tile[:, lanes])
    return ds


def _pool_fwd(u, w_pool, pool_scale, g_pool):
    S = u.shape[0]
    tm = MIXER_TILE

    def body(u_ref, w_ref, scale_ref, gain_ref, y_ref, ext_ref):
        i = pl.program_id(0)

        @pl.when(i == 0)
        def _():
            ext_ref[:HALO, :] = jnp.zeros((HALO, POOL_WIDTH), F32)

        u_tile = u_ref[...]
        ext_ref[HALO:, :] = u_tile
        ds = _pool_features(ext_ref[...], u_tile, i, tm)
        ext_ref[:HALO, :] = u_tile[tm - HALO:, :]
        ys = [_dot(ds[gi].astype(BF16), w_ref[gi].astype(BF16)) for gi in range(len(POOL_WINDOWS))]
        po = jnp.concatenate(ys, axis=1) * scale_ref[...]
        y_ref[...] = ((po * _rstd(po)) * gain_ref[...]).astype(BF16)

    return pl.pallas_call(
        body, name="pool_fwd", grid=(S // tm,),
        in_specs=[_rows(tm, POOL_WIDTH), _full((len(POOL_WINDOWS), POOL_GROUP, POOL_GROUP)),
                  _full((1, POOL_WIDTH)), _full((1, POOL_WIDTH))],
        out_specs=_rows(tm, POOL_WIDTH),
        out_shape=jax.ShapeDtypeStruct((S, POOL_WIDTH), BF16),
        scratch_shapes=[pltpu.VMEM((HALO + tm, POOL_WIDTH), F32)],
        compiler_params=_params(dimension_semantics=("arbitrary",)),
    )(u, w_pool, pool_scale, g_pool)


def _pool_bwd(dy, u, w_pool, pool_scale, g_pool):
    S = u.shape[0]
    tm = MIXER_TILE
    n_tiles = S // tm
    halo_blocks = tm // HALO

    def body(dy_ref, u_ref, uprev_ref, w_ref, scale_ref, gain_ref,
             du_ref, dw_ref, dscale_ref, dgain_ref, ext_ref, nxt_ref):
        i = pl.program_id(0)
        tile = n_tiles - 1 - i

        @pl.when(i == 0)
        def _():
            dw_ref[...] = jnp.zeros_like(dw_ref)
            dscale_ref[...] = jnp.zeros_like(dscale_ref)
            dgain_ref[...] = jnp.zeros_like(dgain_ref)
            nxt_ref[...] = jnp.zeros_like(nxt_ref)

        u_tile = u_ref[...]
        ext_ref[:HALO, :] = jnp.where(tile > 0, uprev_ref[...], 0.0)
        ext_ref[HALO:, :] = u_tile
        ds = _pool_features(ext_ref[...], u_tile, tile, tm)
        dsb = [d.astype(BF16) for d in ds]
        wb = [w_ref[gi].astype(BF16) for gi in range(len(POOL_WINDOWS))]
        yraw = jnp.concatenate([_dot(dsb[gi], wb[gi]) for gi in range(len(POOL_WINDOWS))], axis=1)
        po = yraw * scale_ref[...]
        r = _rstd(po)
        pn = po * r
        dyv = dy_ref[...]
        dgain_ref[...] += jnp.sum(dyv * pn, axis=0, keepdims=True)
        dpo = _norm_bwd(dyv, pn, r, gain_ref[...])
        dscale_ref[...] += jnp.sum(dpo * yraw, axis=0, keepdims=True)
        dyraw = (dpo * scale_ref[...]).astype(BF16)
        for gi, width in enumerate(POOL_WINDOWS):
            lanes = slice(gi * POOL_GROUP, (gi + 1) * POOL_GROUP)
            dw_ref[gi] += _dot_tn(dsb[gi], dyraw[:, lanes])
            dd = _dot_nt(dyraw[:, lanes], wb[gi])
            ddc = dd / _pool_counts(tile, tm, width)
            ext_ref[:tm, lanes] = ddc
            ext_ref[tm:, lanes] = nxt_ref[:, lanes]
            s = ext_ref[:, lanes]
            shift = 1
            while shift < width:
                s = s + pltpu.roll(s, HALO + tm - shift, axis=0)
                shift *= 2
            du_ref[:, lanes] = s[:tm, :] - dd
            nxt_ref[:, lanes] = ddc[:HALO, :]

    return pl.pallas_call(
        body, name="pool_bwd", grid=(n_tiles,),
        in_specs=[pl.BlockSpec((tm, POOL_WIDTH), lambda i: (n_tiles - 1 - i, 0)),
                  pl.BlockSpec((tm, POOL_WIDTH), lambda i: (n_tiles - 1 - i, 0)),
                  pl.BlockSpec((HALO, POOL_WIDTH), lambda i: (jnp.maximum((n_tiles - 1 - i) * halo_blocks - 1, 0), 0)),
                  _full((len(POOL_WINDOWS), POOL_GROUP, POOL_GROUP)), _full((1, POOL_WIDTH)), _full((1, POOL_WIDTH))],
        out_specs=[pl.BlockSpec((tm, POOL_WIDTH), lambda i: (n_tiles - 1 - i, 0)),
                   _full((len(POOL_WINDOWS), POOL_GROUP, POOL_GROUP)), _full((1, POOL_WIDTH)), _full((1, POOL_WIDTH))],
        out_shape=[jax.ShapeDtypeStruct((S, POOL_WIDTH), F32),
                   jax.ShapeDtypeStruct((len(POOL_WINDOWS), POOL_GROUP, POOL_GROUP), F32),
                   jax.ShapeDtypeStruct((1, POOL_WIDTH), F32), jax.ShapeDtypeStruct((1, POOL_WIDTH), F32)],
        scratch_shapes=[pltpu.VMEM((HALO + tm, POOL_WIDTH), F32), pltpu.VMEM((HALO, POOL_WIDTH), F32)],
        compiler_params=_params(dimension_semantics=("arbitrary",)),
    )(dy, u, u, w_pool, pool_scale, g_pool)


def _variant(head):
    return 2 * (head // (N_HEADS // N_KV_HEADS)) + head % 2


def _own_block(shape=(BLOCK, BLOCK)):
    r = lax.broadcasted_iota(jnp.int32, shape, 0)
    i = lax.broadcasted_iota(jnp.int32, shape, 1)
    return r <= i


def _fold_band(own, from_own, from_prev):
    return jnp.where(own, from_own, from_prev)


def _scores_by_head(own_ref, prev_ref, q_tiles):
    stacks = [jnp.concatenate(q_tiles[:2], axis=0), jnp.concatenate(q_tiles[2:], axis=0)]
    by_var = [_dot_nt(jnp.concatenate([own_ref[v], prev_ref[v]], axis=0), stacks[v // 2])
              for v in range(2 * N_KV_HEADS)]
    quadrant = lambda h, rows: by_var[_variant(h)][rows * BLOCK:(rows + 1) * BLOCK,
                                                   ((h // 2) % 2) * BLOCK:((h // 2) % 2 + 1) * BLOCK]
    return [quadrant(h, 0) for h in range(N_HEADS)], [quadrant(h, 1) for h in range(N_HEADS)]


def _softmax_t(s, sink):
    m = jnp.maximum(jnp.max(s, axis=0, keepdims=True), sink)
    p = jnp.exp(s - m)
    p_sink = jnp.exp(sink - m)
    inv = 1.0 / (jnp.sum(p, axis=0, keepdims=True) + p_sink)
    return p * inv, p_sink * inv


def _attn_fwd(q, kz, vz, sinks, g_attn):
    S = q.shape[0]
    nb = S // BLOCK

    def body(q_ref, kp_ref, kc_ref, vp_ref, vc_ref, sinks_ref, gain_ref, o_ref, y_ref):
        b = pl.program_id(0)
        own = _own_block()
        no_prev = jnp.where(b > 0, 0.0, NEG_INF)
        q_pairs = [q_ref[:, i * _LANES:(i + 1) * _LANES] for i in range(N_HEADS // 2)]
        s_own, s_prev = _scores_by_head(kc_ref, kp_ref, q_pairs)
        p_own, p_prev = [], []
        zero = jnp.zeros((BLOCK, BLOCK), F32)
        for h in range(N_HEADS):
            p, _ = _softmax_t(_fold_band(own, s_own[h], s_prev[h] + no_prev), sinks_ref[0, h])
            p_own.append(jnp.where(own, p, zero).astype(BF16))
            p_prev.append(jnp.where(own, zero, p).astype(BF16))
        pairs = []
        for i in range(N_HEADS // 2):
            acc = None
            for h in (2 * i, 2 * i + 1):
                part = _dot_tn(p_own[h], vc_ref[_variant(h)]) + _dot_tn(p_prev[h], vp_ref[_variant(h)])
                acc = part if acc is None else acc + part
            pairs.append(acc)
        o = jnp.concatenate(pairs, axis=1)
        o_ref[...] = o
        y_ref[...] = ((o * _rstd(o)) * gain_ref[...]).astype(BF16)

    variants = (2 * N_KV_HEADS, BLOCK, KV_WIDTH)
    prev = pl.BlockSpec(variants, lambda b: (0, jnp.maximum(b - 1, 0), 0))
    cur = pl.BlockSpec(variants, lambda b: (0, b, 0))
    return pl.pallas_call(
        body, name="attn_fwd", grid=(nb,),
        in_specs=[_rows(BLOCK, ATTN_WIDTH), prev, cur, prev, cur,
                  pl.BlockSpec(memory_space=pltpu.SMEM), _full((1, ATTN_WIDTH))],
        out_specs=[_rows(BLOCK, ATTN_WIDTH), _rows(BLOCK, ATTN_WIDTH)],
        out_shape=[jax.ShapeDtypeStruct((S, ATTN_WIDTH), F32), jax.ShapeDtypeStruct((S, ATTN_WIDTH), BF16)],
        compiler_params=_params(dimension_semantics=("arbitrary",)),
    )(q, kz, kz, vz, vz, sinks, g_attn)


def _attn_bwd(dy, o, q, kz, vz, sinks, g_attn):
    S = q.shape[0]
    nb = S // BLOCK

    def body(dy_ref, o_ref, q_ref, kp_ref, kc_ref, vp_ref, vc_ref, sinks_ref, gain_ref,
             dq_ref, dk_ref, dv_ref, dsink_ref, dgain_ref, kcarry_ref, vcarry_ref):
        b = pl.program_id(0)

        @pl.when(b == 0)
        def _():
            dsink_ref[...] = jnp.zeros_like(dsink_ref)
            dgain_ref[...] = jnp.zeros_like(dgain_ref)
            kcarry_ref[...] = jnp.zeros_like(kcarry_ref)
            vcarry_ref[...] = jnp.zeros_like(vcarry_ref)

        @pl.when(b < nb)
        def _():
            ov = o_ref[...]
            r = _rstd(ov)
            on = ov * r
            dyv = dy_ref[...]
            dgain_ref[...] += jnp.sum(dyv * on, axis=0, keepdims=True)
            do = _norm_bwd(dyv, on, r, gain_ref[...])
            own = _own_block()
            no_prev = jnp.where(b > 0, 0.0, NEG_INF)
            q_pairs = [q_ref[:, i * _LANES:(i + 1) * _LANES] for i in range(N_HEADS // 2)]
            do_pairs = [do[:, i * _LANES:(i + 1) * _LANES].astype(BF16) for i in range(N_HEADS // 2)]
            heads = range(N_HEADS)
            s_own, s_prev = _scores_by_head(kc_ref, kp_ref, q_pairs)
            dp_own, dp_prev = _scores_by_head(vc_ref, vp_ref, do_pairs)
            zero = jnp.zeros((BLOCK, BLOCK), F32)
            split = lambda t: (jnp.where(own, t, zero).astype(BF16), jnp.where(own, zero, t).astype(BF16))
            ds_parts, p_parts, sink_rows = [], [], []
            for h in heads:
                p, p_sink = _softmax_t(_fold_band(own, s_own[h], s_prev[h] + no_prev), sinks_ref[0, h])
                dp = _fold_band(own, dp_own[h], dp_prev[h])
                delta = jnp.sum(p * dp, axis=0, keepdims=True)
                ds_parts.append(split(p * (dp - delta)))
                p_parts.append(split(p))
                sink_rows.append(jnp.zeros((1, _LANES), F32) - jnp.sum(p_sink * delta))
            dsink_ref[...] += jnp.concatenate(sink_rows, axis=0)
            n_var = 2 * N_KV_HEADS
            dk_own, dk_prev, dv_own, dv_prev = ([None] * n_var for _ in range(4))
            add = lambda acc, var, t: acc.__setitem__(var, t if acc[var] is None else acc[var] + t)
            for i in range(N_HEADS // 2):
                dq_pair = None
                for h in (2 * i, 2 * i + 1):
                    var = _variant(h)
                    (ds_o, ds_p), (p_o, p_p) = ds_parts[h], p_parts[h]
                    part = _dot_tn(ds_o, kc_ref[var]) + _dot_tn(ds_p, kp_ref[var])
                    dq_pair = part if dq_pair is None else dq_pair + part
                    add(dk_own, var, _dot(ds_o, q_pairs[i]))
                    add(dk_prev, var, _dot(ds_p, q_pairs[i]))
                    add(dv_own, var, _dot(p_o, do_pairs[i]))
                    add(dv_prev, var, _dot(p_p, do_pairs[i]))
                dq_ref[:, i * _LANES:(i + 1) * _LANES] = dq_pair * ATTN_SCALE
            low = lax.broadcasted_iota(jnp.int32, (BLOCK, _LANES), 1) < HEAD_DIM

            def merge(acc):
                return jnp.where(low, acc[0] + pltpu.roll(acc[1], HEAD_DIM, axis=1),
                                 acc[3] + pltpu.roll(acc[2], HEAD_DIM, axis=1))
            dk_ref[...] = kcarry_ref[...] + merge(dk_prev)
            dv_ref[...] = vcarry_ref[...] + merge(dv_prev)
            kcarry_ref[...] = merge(dk_own)
            vcarry_ref[...] = merge(dv_own)

        @pl.when(b == nb)
        def _():
            dk_ref[...] = kcarry_ref[...]
            dv_ref[...] = vcarry_ref[...]

    cur = lambda b: (jnp.minimum(b, nb - 1), 0)
    prev = lambda b: (jnp.clip(b - 1, 0, nb - 1), 0)
    variants = (2 * N_KV_HEADS, BLOCK, KV_WIDTH)
    var_prev = pl.BlockSpec(variants, lambda b: (0, jnp.clip(b - 1, 0, nb - 1), 0))
    var_cur = pl.BlockSpec(variants, lambda b: (0, jnp.minimum(b, nb - 1), 0))
    return pl.pallas_call(
        body, name="attn_bwd", grid=(nb + 1,),
        in_specs=[pl.BlockSpec((BLOCK, ATTN_WIDTH), lambda b: (jnp.minimum(b, nb - 1), 1)),
                  pl.BlockSpec((BLOCK, ATTN_WIDTH), cur), pl.BlockSpec((BLOCK, ATTN_WIDTH), cur),
                  var_prev, var_cur, var_prev, var_cur,
                  pl.BlockSpec(memory_space=pltpu.SMEM), _full((1, ATTN_WIDTH))],
        out_specs=[pl.BlockSpec((BLOCK, ATTN_WIDTH), cur),
                   pl.BlockSpec((BLOCK, KV_WIDTH), prev), pl.BlockSpec((BLOCK, KV_WIDTH), prev),
                   _full((N_HEADS, _LANES)), _full((1, ATTN_WIDTH))],
        out_shape=[jax.ShapeDtypeStruct((S, ATTN_WIDTH), F32), jax.ShapeDtypeStruct((S, KV_WIDTH), F32),
                   jax.ShapeDtypeStruct((S, KV_WIDTH), F32), jax.ShapeDtypeStruct((N_HEADS, _LANES), F32),
                   jax.ShapeDtypeStruct((1, ATTN_WIDTH), F32)],
        scratch_shapes=[pltpu.VMEM((BLOCK, KV_WIDTH), F32), pltpu.VMEM((BLOCK, KV_WIDTH), F32)],
        compiler_params=_params(dimension_semantics=("arbitrary",)),
    )(dy, o, q, kz, kz, vz, vz, sinks, g_attn)


def _mixer_out_fwd(y_pool, y_attn, x, w_out, mine, post, deps=()):
    S = x.shape[0]
    tm = MIXER_TILE

    def body(yp_ref, ya_ref, x_ref, w_land, w_own, mine_ref, post_ref, out_ref, m_ref, y_ref, w_v, sem):
        _load_once(_gathered(w_land, w_own, w_v, mine_ref[0], rows=D_MODEL // N_SHARD), sem)
        y_ref[:, :POOL_WIDTH] = yp_ref[...]
        y_ref[:, POOL_WIDTH:] = ya_ref[...]
        m = _dot(y_ref[...], w_v[...])
        m_ref[...] = m
        out_ref[...] = x_ref[...] + (m * _rstd(m)) * post_ref[...]

    args = [y_pool, y_attn, x, *w_out, mine, post]
    return pl.pallas_call(
        _ignoring(body, len(args), len(deps)), name="mixer_out_fwd", grid=(S // tm,),
        in_specs=[_rows(tm, POOL_WIDTH), _rows(tm, ATTN_WIDTH), _rows(tm, D_MODEL), _ANY, _ANY, _SMEM,
                  _full((1, D_MODEL))] + [_ANY] * len(deps),
        out_specs=[_rows(tm, D_MODEL), _rows(tm, D_MODEL), _rows(tm, D_MODEL)],
        out_shape=[jax.ShapeDtypeStruct((S, D_MODEL), F32), jax.ShapeDtypeStruct((S, D_MODEL), F32),
                   jax.ShapeDtypeStruct((S, D_MODEL), BF16)],
        scratch_shapes=[pltpu.VMEM((D_MODEL, D_MODEL), BF16), pltpu.SemaphoreType.DMA((N_SHARD,))],
        compiler_params=_params(dimension_semantics=("arbitrary",)),
    )(*args, *deps)


def _mixer_out_bwd(dout, m, w_out, mine, post, deps=()):
    S = m.shape[0]
    tm = MIXER_TILE

    def body(dout_ref, m_ref, w_land, w_own, mine_ref, post_ref, dy_ref, dm_ref, dpost_ref, w_v, sem):
        _load_once(_gathered(w_land, w_own, w_v, mine_ref[0], rows=D_MODEL // N_SHARD), sem)

        @pl.when(pl.program_id(0) == 0)
        def _():
            dpost_ref[...] = jnp.zeros_like(dpost_ref)

        mv = m_ref[...]
        r = _rstd(mv)
        mn = mv * r
        dv = dout_ref[...]
        dpost_ref[...] += jnp.sum(dv * mn, axis=0, keepdims=True)
        dm = _norm_bwd(dv, mn, r, post_ref[...]).astype(BF16)
        dm_ref[...] = dm
        dy_ref[...] = _dot_nt(dm, w_v[...])

    args = [dout, m, *w_out, mine, post]
    return pl.pallas_call(
        _ignoring(body, len(args), len(deps)), name="mixer_out_bwd", grid=(S // tm,),
        in_specs=[_rows(tm, D_MODEL), _rows(tm, D_MODEL), _ANY, _ANY, _SMEM, _full((1, D_MODEL))] + [_ANY] * len(deps),
        out_specs=[_rows(tm, D_MODEL), _rows(tm, D_MODEL), _full((1, D_MODEL))],
        out_shape=[jax.ShapeDtypeStruct((S, D_MODEL), F32), jax.ShapeDtypeStruct((S, D_MODEL), BF16),
                   jax.ShapeDtypeStruct((1, D_MODEL), F32)],
        scratch_shapes=[pltpu.VMEM((D_MODEL, D_MODEL), BF16), pltpu.SemaphoreType.DMA((N_SHARD,))],
        compiler_params=_params(dimension_semantics=("arbitrary",)),
    )(*args, *deps)


def _inv_freq_row():
    inv_freq = ROPE_THETA ** (-jnp.arange(0, ROT_DIM, 2, dtype=F32) / ROT_DIM)
    per_head = jnp.concatenate([inv_freq, inv_freq, jnp.zeros((HEAD_DIM - ROT_DIM,), F32)])
    return jnp.tile(per_head, _LANES // HEAD_DIM).reshape(1, _LANES)


def _local_step(x, pos, target, small, mine, weights_of, grads_ready):
    invf = _inv_freq_row()
    wgu1, wd1 = weights_of("ffn1", ())
    x1, g1, u1, f1 = _ffn_fwd(x, small["ffn1_pre"], small["ffn1_post"], wgu1, wd1, mine)
    w_in_t, w_out = weights_of("mixer", (x1,))
    u, q, k, v = _mixer_in_fwd(x1, small["mix_pre"], w_in_t, mine, pos, invf)
    y_pool = _pool_fwd(u, small["w_pool"], small["pool_scale"], small["g_pool"])
    o, y_attn = _attn_fwd(q, k, v, small["sinks"], small["g_attn"])
    x2, m, y = _mixer_out_fwd(y_pool, y_attn, x1, w_out, mine, small["mix_post"])
    wgu2, wd2 = weights_of("ffn2", (x2,))
    dx3, g2, u2, f2, loss_acc = _ffn_fwd(x2, small["ffn2_pre"], small["ffn2_post"], wgu2, wd2, mine, target=target)
    loss = loss_acc[0, 0] * (0.5 / D_MODEL)
    grads = {}
    dx2, h3, dgu2, a2, df2, grads["ffn2_pre"], grads["ffn2_post"] = _ffn_bwd(
        dx3, x2, f2, g2, u2, small["ffn2_pre"], small["ffn2_post"], wgu2, wd2, mine)
    dwgu2 = _wgrad(h3, dgu2, D_MODEL, FF_CHUNK, "wgrad_gu2", column_shards=True)
    dwd2 = _wgrad(a2, df2, FF_CHUNK, D_MODEL, "wgrad_down2")
    deps = grads_ready("ffn2", {"ffn2_w_gu": dwgu2, "ffn2_w_down": dwd2})
    dy, dm, grads["mix_post"] = _mixer_out_bwd(dx2, m, w_out, mine, small["mix_post"], deps=deps)
    dw_out = _wgrad(y, dm, D_MODEL, D_MODEL, "wgrad_out")
    dq, dk, dv, dsinks, grads["g_attn"] = _attn_bwd(dy, o, q, k, v, small["sinks"], small["g_attn"])
    grads["sinks"] = dsinks[:, 0].reshape(1, N_HEADS)
    du, grads["w_pool"], grads["pool_scale"], grads["g_pool"] = _pool_bwd(
        dy, u, small["w_pool"], small["pool_scale"], small["g_pool"])
    dx1, dz, h2, grads["mix_pre"] = _mixer_in_bwd(dx2, x1, small["mix_pre"], w_in_t, mine, du, dq, dk, dv, pos, invf)
    dw_in_t = _wgrad(dz, h2, IN_WIDTH, D_MODEL, "wgrad_in")
    deps = grads_ready("mixer", {"w_in": dw_in_t, "w_out": dw_out})
    dx, h1, dgu1, a1, df1, grads["ffn1_pre"], grads["ffn1_post"] = _ffn_bwd(
        dx1, x, f1, g1, u1, small["ffn1_pre"], small["ffn1_post"], wgu1, wd1, mine, deps=deps)
    deps = grads_ready("small", grads)
    dwgu1 = _wgrad(h1, dgu1, D_MODEL, FF_CHUNK, "wgrad_gu1", column_shards=True, deps=deps)
    deps = grads_ready("ffn1_gu", {"ffn1_w_gu": dwgu1})
    dwd1 = _wgrad(a1, df1, FF_CHUNK, D_MODEL, "wgrad_down1", deps=deps)
    grads_ready("ffn1_down", {"ffn1_w_down": dwd1})
    return loss, dx, grads


def _place():
    return lax.axis_index("x"), lax.axis_index("y"), lax.axis_index("c")


def _other_chips(x, y):
    return [(1 - x, y), (x, 1 - y), (1 - x, 1 - y)]


def _hbm_shape(shape, dtype):
    return jax.ShapeDtypeStruct(shape, dtype)


_HBM = pl.BlockSpec(memory_space=pltpu.HBM)
_SEM = pl.BlockSpec(memory_space=pltpu.SEMAPHORE)
_EFFECT = pltpu.SideEffectType.DATAFLOW_SIDE_EFFECTING
GATHER, GATHER_HALF, REDUCE, BROADCAST = "gather", "gather_half", "reduce", "broadcast"
N_DEVICES = 8


def _in_hbm(a):
    return pltpu.with_memory_space_constraint(a, pltpu.HBM)


def _core_half(rows, c):
    return pl.ds(pl.multiple_of(c * (rows // 2), 16), rows // 2)


def _chip_copies(kind, srcs, lands, send_sems, recv_sems):
    x, y, c = _place()
    mine = 2 * x + y
    copies = []
    for w in range(len(srcs)):
        if kind == BROADCAST:
            peers = [(x ^ (k >> 2), y ^ ((k >> 1) & 1), c ^ (k & 1)) for k in range(1, N_DEVICES)]
        else:
            peers = [(px, py, c) for px, py in _other_chips(x, y)]
        for k, (px, py, pc) in enumerate(peers):
            if kind == GATHER:
                src, dst = srcs[w], lands[w].at[mine]
            elif kind == GATHER_HALF:
                half = _core_half(srcs[w].shape[0], c)
                src, dst = srcs[w].at[half, :], lands[w].at[mine, half, :]
            elif kind == BROADCAST:
                src, dst = srcs[w], lands[w].at[2 * mine + c]
            else:
                src, dst = srcs[w].at[2 * px + py], lands[w].at[k]
            pair = len(peers) * w + k
            copies.append(pltpu.make_async_remote_copy(
                src_ref=src, dst_ref=dst, send_sem=send_sems.at[pair], recv_sem=recv_sems.at[pair],
                device_id=(px, py, pc), device_id_type=MESH))
    return copies


def _landing_shape(kind, src):
    if kind == REDUCE:
        return (N_SHARD - 1,) + src.shape[1:]
    return ((N_DEVICES if kind == BROADCAST else N_SHARD),) + src.shape


def _peer_count(kind):
    return N_DEVICES - 1 if kind == BROADCAST else N_SHARD - 1


def _exchange_start(kinds, groups, name):
    sizes = [len(g) for g in groups]
    flat = [s for g in groups for s in g]
    n, ng = len(flat), len(groups)

    def body(*refs):
        srcs, lands = refs[:n], refs[n:2 * n]
        sems = refs[2 * n:2 * n + 2 * ng]
        token = refs[-1]
        start = 0
        for gi, size in enumerate(sizes):
            for cp in _chip_copies(kinds[gi], srcs[start:start + size], lands[start:start + size],
                                   sems[2 * gi], sems[2 * gi + 1]):
                cp.start()
            start += size
        token[...] = jnp.zeros_like(token)

    landings = [lax.empty(_landing_shape(kind, s), s.dtype) for kind, g in zip(kinds, groups) for s in g]
    sem_shapes = [pltpu.SemaphoreType.DMA((size * _peer_count(kind),)) for kind, size in zip(kinds, sizes)
                  for _ in range(2)]
    outs = pl.pallas_call(
        body, name=name,
        in_specs=[_HBM] * (2 * n),
        out_specs=[_SEM] * (2 * ng) + [_HBM] * (2 * n) + [pl.BlockSpec(memory_space=pltpu.VMEM)],
        out_shape=sem_shapes + [pltpu.HBM(a.shape, a.dtype) for a in flat + landings]
        + [jax.ShapeDtypeStruct((8, _LANES), F32)],
        input_output_aliases={i: 2 * ng + i for i in range(2 * n)},
        compiler_params=pltpu.CompilerParams(has_side_effects=_EFFECT),
    )(*[_in_hbm(a) for a in flat + landings])
    sems, srcs, lands, token = outs[:2 * ng], outs[2 * ng:2 * ng + n], outs[2 * ng + n:2 * ng + 2 * n], outs[-1]
    handles, start = [], 0
    for gi, size in enumerate(sizes):
        handles.append((sems[2 * gi], sems[2 * gi + 1], srcs[start:start + size], lands[start:start + size]))
        start += size
    return handles, token


def _exchange_wait(kind, handle, after, name):
    send_sems, recv_sems, srcs, lands = handle
    n = len(srcs)

    def body(*refs):
        copies = _chip_copies(kind, refs[:n], refs[n:2 * n], refs[2 * n], refs[2 * n + 1])
        for cp in copies:
            cp.wait_send()
        for cp in copies:
            cp.wait_recv()

    outs = pl.pallas_call(
        body, name=name,
        in_specs=[_HBM] * (2 * n) + [_SEM, _SEM] + [_ANY] * len(after),
        out_specs=[_HBM] * (2 * n),
        out_shape=[pltpu.HBM(a.shape, a.dtype) for a in list(srcs) + list(lands)],
        input_output_aliases={i: i for i in range(2 * n)},
        compiler_params=pltpu.CompilerParams(has_side_effects=_EFFECT),
    )(*srcs, *lands, send_sems, recv_sems, *after)
    return outs[:n], outs[n:]


def _swap_gathered_halves(lands, name):
    n = len(lands)

    def body(*refs):
        bufs = refs[n:2 * n]
        send_sems, recv_sems = refs[2 * n:]
        x, y, c = _place()
        mine = 2 * x + y
        sends, arrivals = [], []
        for w in range(n):
            rows = bufs[w].shape[1]
            for d in range(1, N_SHARD):
                slot = (mine + d) % N_SHARD
                sems = dict(send_sem=send_sems.at[(N_SHARD - 1) * w + d - 1],
                            recv_sem=recv_sems.at[(N_SHARD - 1) * w + d - 1],
                            device_id=(x, y, 1 - c), device_id_type=MESH)
                fetched = bufs[w].at[slot, _core_half(rows, c), :]
                missing = bufs[w].at[slot, _core_half(rows, 1 - c), :]
                sends.append(pltpu.make_async_remote_copy(src_ref=fetched, dst_ref=fetched, **sems))
                arrivals.append(pltpu.make_async_remote_copy(src_ref=missing, dst_ref=missing, **sems))
        for cp in sends:
            cp.start()
        for cp in arrivals:
            cp.wait_recv()
        for cp in sends:
            cp.wait_send()

    return pl.pallas_call(
        body, name=name, in_specs=[_ANY] * n, out_specs=[_ANY] * n,
        out_shape=[_hbm_shape(a.shape, a.dtype) for a in lands],
        input_output_aliases={i: i for i in range(n)},
        scratch_shapes=[pltpu.SemaphoreType.DMA((n * (N_SHARD - 1),)), pltpu.SemaphoreType.DMA((n * (N_SHARD - 1),))],
        compiler_params=pltpu.CompilerParams(has_side_effects=True),
    )(*lands)


def _swap_with_sibling(partials, name):
    n = len(partials)

    def body(*refs):
        ins, outs = refs[:n], refs[n:2 * n]
        send_sems, recv_sems = refs[2 * n:]
        x, y, c = _place()
        sends = [pltpu.make_async_remote_copy(
            src_ref=ins[w], dst_ref=outs[w], send_sem=send_sems.at[w], recv_sem=recv_sems.at[w],
            device_id=(x, y, 1 - c), device_id_type=MESH) for w in range(n)]
        for cp in sends:
            cp.start()
        for cp in sends:
            cp.wait_recv()
        for cp in sends:
            cp.wait_send()

    return pl.pallas_call(
        body, name=name,
        in_specs=[_ANY] * n, out_specs=[_ANY] * n,
        out_shape=[_hbm_shape(p.shape, p.dtype) for p in partials],
        scratch_shapes=[pltpu.SemaphoreType.DMA((n,)), pltpu.SemaphoreType.DMA((n,))],
        compiler_params=pltpu.CompilerParams(has_side_effects=True),
    )(*partials)


def _row_block(rows, cap):
    best = None
    for cand in range(16, min(rows, cap) + 1, 16):
        if rows % cand == 0:
            best = cand
    assert best is not None, rows
    return best


def _chip_partial(own, received, shard, name):
    _, R, C = own.shape
    rb = _row_block(R, 512)

    def body(shard_ref, own_ref, rec_ref, out_ref):
        acc = own_ref[...]
        for k in range(3):
            acc = acc + rec_ref[k].astype(F32)
        out_ref[...] = acc

    return pl.pallas_call(
        body, name=name,
        grid_spec=pltpu.PrefetchScalarGridSpec(
            num_scalar_prefetch=1, grid=(R // rb,),
            in_specs=[pl.BlockSpec((None, rb, C), lambda i, s: (s[0], i, 0)),
                      pl.BlockSpec((3, rb, C), lambda i, s: (0, i, 0))],
            out_specs=pl.BlockSpec((rb, C), lambda i, s: (i, 0))),
        out_shape=jax.ShapeDtypeStruct((R, C), F32),
        compiler_params=_params(dimension_semantics=("arbitrary",)),
    )(shard, own, received)


def _adamw(w, m, v, g_parts, name, slot=None):
    R, C = w.shape
    by_device = slot is not None
    rb = _row_block(R, 256) if R % 16 == 0 else R

    def body(w_ref, m_ref, v_ref, *refs):
        g_refs, (grad_ref, delta_ref, m_out, v_out) = refs[:-4], refs[-4:]
        if by_device:
            own_ref, land_ref, slot_ref = g_refs
            part = lambda d: jnp.where(slot_ref[0] == d, own_ref[...], land_ref[d])
            g = part(0)
            for d in range(1, N_DEVICES):
                g = g + part(d)
        else:
            g = g_refs[0][...]
            for g_ref in g_refs[1:]:
                g = g + g_ref[...]
        grad_ref[...] = g
        new_m = ADAM_B1 * m_ref[...] + (1.0 - ADAM_B1) * g
        new_v = ADAM_B2 * v_ref[...] + (1.0 - ADAM_B2) * (g * g)
        m_hat = new_m / (1.0 - ADAM_B1 ** ADAM_STEP)
        v_hat = new_v / (1.0 - ADAM_B2 ** ADAM_STEP)
        delta_ref[...] = -ADAM_LR * (m_hat / (jnp.sqrt(v_hat) + ADAM_EPS) + ADAM_WD * w_ref[...])
        m_out[...] = new_m
        v_out[...] = new_v

    spec = pl.BlockSpec((rb, C), lambda i: (i, 0))
    if by_device:
        g_specs = [spec, pl.BlockSpec((N_DEVICES, rb, C), lambda i: (0, i, 0)), _SMEM]
        g_parts = list(g_parts) + [slot]
    else:
        g_specs = [spec] * len(g_parts)
    return pl.pallas_call(
        body, name=name, grid=(R // rb,),
        in_specs=[spec, spec, spec] + g_specs,
        out_specs=[spec] * 4,
        out_shape=[jax.ShapeDtypeStruct((R, C), F32)] * 4,
        compiler_params=_params(dimension_semantics=("arbitrary",)),
    )(w, m, v, *g_parts)


SMALL_NAMES = ("ffn1_pre", "ffn1_post", "mix_pre", "pool_scale", "sinks", "g_pool", "g_attn", "mix_post",
               "ffn2_pre", "ffn2_post", "w_pool")
_SLAB_PART = 8 * _LANES


def _to_slab(parts):
    rows = []
    for name in SMALL_NAMES:
        flat = parts[name].reshape(-1)
        padded = -(-flat.shape[0] // _SLAB_PART) * _SLAB_PART
        rows.append(jnp.pad(flat, (0, padded - flat.shape[0])).reshape(-1, _LANES))
    return jnp.concatenate(rows, axis=0)


def _from_slab(slab, like):
    out, row = {}, 0
    for name in SMALL_NAMES:
        size = like[name].size
        rows = -(-size // _SLAB_PART) * (_SLAB_PART // _LANES)
        out[name] = slab[row:row + rows].reshape(-1)[:size].reshape(like[name].shape)
        row += rows
    return out


BIG_NAMES = ("ffn1_w_gu", "ffn1_w_down", "w_in", "w_out", "ffn2_w_gu", "ffn2_w_down")
WEIGHT_ORDER = ("ffn1_pre", "ffn1_w_gu", "ffn1_w_down", "ffn1_post", "mix_pre", "w_in", "w_pool", "pool_scale",
                "sinks", "g_pool", "g_attn", "w_out", "mix_post", "ffn2_pre", "ffn2_w_gu", "ffn2_w_down", "ffn2_post")


def kernel(x, positions, ffn1_pre, ffn1_w_gu, ffn1_w_down, ffn1_post, mix_pre, w_in, w_pool, pool_scale, sinks, g_pool, g_attn, w_out, mix_post, ffn2_pre, ffn2_w_gu, ffn2_w_down, ffn2_post, loss_target, m_ffn1_pre, m_ffn1_w_gu, m_ffn1_w_down, m_ffn1_post, m_mix_pre, m_w_in, m_w_pool, m_pool_scale, m_sinks, m_g_pool, m_g_attn, m_w_out, m_mix_post, m_ffn2_pre, m_ffn2_w_gu, m_ffn2_w_down, m_ffn2_post, v_ffn1_pre, v_ffn1_w_gu, v_ffn1_w_down, v_ffn1_post, v_mix_pre, v_w_in, v_w_pool, v_pool_scale, v_sinks, v_g_pool, v_g_attn, v_w_out, v_mix_post, v_ffn2_pre, v_ffn2_w_gu, v_ffn2_w_down, v_ffn2_post):
    given = dict(locals())
    weights = {n: given[n][0] for n in WEIGHT_ORDER}
    moments_m = {n: given["m_" + n][0] for n in WEIGHT_ORDER}
    moments_v = {n: given["v_" + n][0] for n in WEIGHT_ORDER}
    S = x.shape[1]
    shard = (2 * lax.axis_index("x") + lax.axis_index("y")).astype(jnp.int32).reshape(1)

    local16 = {n: weights[n].astype(BF16) for n in BIG_NAMES if n != "w_in"}
    local16["w_in"] = weights["w_in"].T.astype(BF16)
    gather_groups = {"ffn1": ("ffn1_w_gu", "ffn1_w_down"), "mixer": ("w_in", "w_out"),
                     "ffn2": ("ffn2_w_gu", "ffn2_w_down")}
    gather_kinds = {"ffn1": GATHER_HALF, "mixer": GATHER, "ffn2": GATHER}
    handles, _ = _exchange_start(list(gather_kinds.values()),
                                 [[local16[n] for n in names] for names in gather_groups.values()], "gather_start")
    gather_handles = dict(zip(gather_groups, handles))

    def weights_of(group, after):
        kind = gather_kinds[group]
        owns, lands = _exchange_wait(kind, gather_handles[group], list(after), "gather_wait_" + group)
        if kind == GATHER_HALF:
            lands = _swap_gathered_halves(lands, "swap_gathered_" + group)
        return list(zip(lands, owns))

    pending, last_token = {}, []

    def grads_ready(group, grads):
        if group == "small":
            (handle,), token = _exchange_start([BROADCAST], [[_to_slab(grads)]], "reduce_start_small")
            pending[group] = handle
            return [token]
        names = list(grads)
        (handle,), token = _exchange_start([REDUCE], [[grads[n][1] for n in names]], "reduce_start_" + group)
        pending[group] = (names, handle, [grads[n][0] for n in names])
        last_token[:] = [token]
        return [token]

    small = {n: (weights[n] if weights[n].ndim > 1 else weights[n].reshape(1, -1)) for n in SMALL_NAMES}
    loss, dx, small_grads = _local_step(x[0], positions.reshape(S, 1), loss_target[0], small, shard,
                                        weights_of, grads_ready)
    loss = lax.psum(loss, ("x", "y", "c"))

    grad, delta, new_m, new_v = {}, {}, {}, {}

    def finish(groups, after):
        names, partials = [], []
        for group in groups:
            group_names, handle, own32 = pending[group]
            _, received = _exchange_wait(REDUCE, handle, after, "reduce_wait_" + group)
            names += group_names
            partials += [_chip_partial(g32, rec, shard, "chip_partial_" + n)
                         for n, g32, rec in zip(group_names, own32, received)]
        siblings = _swap_with_sibling(partials, "swap_" + groups[0])
        for name, mine, theirs in zip(names, partials, siblings):
            if name == "w_in":
                mine, theirs = mine.T, theirs.T
            grad[name], delta[name], new_m[name], new_v[name] = _adamw(
                weights[name], moments_m[name], moments_v[name], [mine, theirs], "adamw_" + name)
        return [grad[names[-1]]]

    after = finish(["ffn2"], last_token)
    after = finish(["mixer"], after)
    after = finish(["ffn1_gu", "ffn1_down"], after)
    (own_slab,), (slab_landing,) = _exchange_wait(BROADCAST, pending["small"], after, "reduce_wait_small")
    device = (2 * shard + lax.axis_index("c")).astype(jnp.int32)
    small_like = {n: small[n] for n in SMALL_NAMES}
    slabs = _adamw(_to_slab(small), _to_slab({n: moments_m[n] for n in SMALL_NAMES}),
                   _to_slab({n: moments_v[n] for n in SMALL_NAMES}), [own_slab, slab_landing], "adamw_small",
                   slot=device)
    for store, slab in zip((grad, delta, new_m, new_v), slabs):
        store.update(_from_slab(slab, small_like))

    def out(store):
        return [store[n].reshape(given[n].shape) for n in WEIGHT_ORDER]
    return (loss, dx[None], *out(grad), *out(delta), *out(new_m), *out(new_v))
```

```python
import functools

import jax
import jax.numpy as jnp
from jax import lax
from jax.experimental import pallas as pl
from jax.experimental.pallas import tpu as pltpu

F32 = jnp.float32
BF16 = jnp.bfloat16

D_MODEL = 1024
D_FF = 2816
N_SHARD = 4
FF_CHUNK = D_FF // 2
POOL_WINDOWS = (2, 4, 8, 16)
POOL_WIDTH = 512
POOL_GROUP = 128
HALO = 16
HEAD_DIM = 64
N_HEADS = 8
N_KV_HEADS = 2
ATTN_WIDTH = 512
KV_WIDTH = 128
IN_WIDTH = 1280
BLOCK = 128
ROT_DIM = 16
ROPE_THETA = 500000.0
EPS = 1e-6
NEG_INF = -1e30
ATTN_SCALE = HEAD_DIM ** -0.5

ADAM_LR = 0.001
ADAM_B1 = 0.9
ADAM_B2 = 0.999
ADAM_EPS = 1e-08
ADAM_WD = 0.01
ADAM_STEP = 10

VMEM_LIMIT = 60 * 1024 * 1024
FFN_FWD_TILE = 512
FFN_BWD_TILE = 256
MIXER_TILE = 512

MESH = pl.DeviceIdType.MESH


def _params(**kw):
    return pltpu.CompilerParams(vmem_limit_bytes=VMEM_LIMIT, **kw)


def _dot(a, b):
    return jnp.dot(a, b, preferred_element_type=F32)


def _dot_nt(a, b):
    return lax.dot_general(a, b, (((1,), (1,)), ((), ())), preferred_element_type=F32)


def _dot_tn(a, b):
    return lax.dot_general(a, b, (((0,), (0,)), ((), ())), preferred_element_type=F32)


def _rstd(x):
    return lax.rsqrt(jnp.mean(x * x, axis=-1, keepdims=True) + EPS)


def _norm_bwd(dy, xn, r, gain):
    dxn = dy * gain
    return r * (dxn - xn * jnp.mean(dxn * xn, axis=-1, keepdims=True))


def _sigmoid(x):
    return 1.0 / (1.0 + jnp.exp(-x))


def _full(shape):
    return pl.BlockSpec(shape, lambda *_: (0,) * len(shape))


def _rows(tile, width, col=0):
    return pl.BlockSpec((tile, width), lambda i: (i, col))


_ANY = pl.BlockSpec(memory_space=pl.ANY)


_SMEM = pl.BlockSpec(memory_space=pltpu.SMEM)


def _load_once(pairs, sem):
    @pl.when(pl.program_id(0) == 0)
    def _():
        copies = [pltpu.make_async_copy(src, dst, sem.at[n]) for n, (src, dst) in enumerate(pairs)]
        for cp in copies:
            cp.start()
        for cp in copies:
            cp.wait()


def _gathered(land_ref, own_ref, vmem_ref, mine, rows=None):
    def dst(slot):
        if rows is None:
            return vmem_ref.at[slot]
        return vmem_ref.at[pl.ds(pl.multiple_of(slot * rows, 16), rows), :]
    pairs = [(land_ref.at[(mine + d) % N_SHARD], dst((mine + d) % N_SHARD)) for d in range(1, N_SHARD)]
    return pairs + [(own_ref, dst(mine))]


def _ignoring(body, start, count):
    def wrapped(*refs):
        return body(*refs[:start], *refs[start + count:])
    return wrapped


def _ffn_fwd(x, pre, post, wgu, wd, mine, target=None, deps=()):
    S = x.shape[0]
    tm = FFN_FWD_TILE
    with_loss = target is not None

    def body(*refs):
        if with_loss:
            (x_ref, pre_ref, post_ref, wgu_land, wgu_own, wd_land, wd_own, mine_ref, tgt_ref,
             out_ref, g_ref, u_ref, f_ref, loss_ref, wgu_v, wd_v, sem) = refs
        else:
            (x_ref, pre_ref, post_ref, wgu_land, wgu_own, wd_land, wd_own, mine_ref,
             out_ref, g_ref, u_ref, f_ref, wgu_v, wd_v, sem) = refs
        _load_once(_gathered(wgu_land, wgu_own, wgu_v, mine_ref[0])
                   + _gathered(wd_land, wd_own, wd_v, mine_ref[0], rows=D_FF // N_SHARD), sem)
        xv = x_ref[...]
        h = ((xv * _rstd(xv)) * pre_ref[...]).astype(BF16)
        facc = jnp.zeros((tm, D_MODEL), F32)
        for c in range(2):
            cols = slice(c * FF_CHUNK, (c + 1) * FF_CHUNK)
            g = _dot(h, wgu_v[c])
            u = _dot(h, wgu_v[2 + c])
            g_ref[:, cols] = g.astype(BF16)
            u_ref[:, cols] = u.astype(BF16)
            a = (g * _sigmoid(g)) * u
            facc = facc + _dot(a.astype(BF16), wd_v[cols, :])
        f_ref[...] = facc
        out = xv + 0.5 * ((facc * _rstd(facc)) * post_ref[...])
        if with_loss:
            diff = out - tgt_ref[...]
            out_ref[...] = diff * (1.0 / D_MODEL)

            @pl.when(pl.program_id(0) == 0)
            def _():
                loss_ref[...] = jnp.zeros_like(loss_ref)
            loss_ref[...] += jnp.sum(diff * diff)
        else:
            out_ref[...] = out

    in_specs = [_rows(tm, D_MODEL), _full((1, D_MODEL)), _full((1, D_MODEL)), _ANY, _ANY, _ANY, _ANY, _SMEM]
    args = [x, pre, post, *wgu, *wd, mine]
    out_shape = [jax.ShapeDtypeStruct((S, D_MODEL), F32), jax.ShapeDtypeStruct((S, D_FF), BF16),
                 jax.ShapeDtypeStruct((S, D_FF), BF16), jax.ShapeDtypeStruct((S, D_MODEL), F32)]
    out_specs = [_rows(tm, D_MODEL), _rows(tm, D_FF), _rows(tm, D_FF), _rows(tm, D_MODEL)]
    if with_loss:
        in_specs.append(_rows(tm, D_MODEL))
        args.append(target)
        out_shape.append(jax.ShapeDtypeStruct((8, 128), F32))
        out_specs.append(_full((8, 128)))
    return pl.pallas_call(
        _ignoring(body, len(args), len(deps)), name="ffn_fwd_loss" if with_loss else "ffn_fwd",
        grid=(S // tm,), in_specs=in_specs + [_ANY] * len(deps), out_specs=out_specs, out_shape=out_shape,
        scratch_shapes=[pltpu.VMEM((N_SHARD, D_MODEL, FF_CHUNK), BF16), pltpu.VMEM((D_FF, D_MODEL), BF16),
                        pltpu.SemaphoreType.DMA((2 * N_SHARD,))],
        compiler_params=_params(dimension_semantics=("arbitrary",)),
    )(*args, *deps)


def _ffn_bwd(dout, x, f, g, u, pre, post, wgu, wd, mine, deps=()):
    S = x.shape[0]
    tm = FFN_BWD_TILE

    def body(dout_ref, x_ref, f_ref, g_ref, u_ref, pre_ref, post_ref, wgu_land, wgu_own, wd_land, wd_own, mine_ref,
             dx_ref, h_ref, dgu_ref, a_ref, df_ref, dpre_ref, dpost_ref, wgu_v, wd_v, sem):
        _load_once(_gathered(wgu_land, wgu_own, wgu_v, mine_ref[0])
                   + _gathered(wd_land, wd_own, wd_v, mine_ref[0], rows=D_FF // N_SHARD), sem)

        @pl.when(pl.program_id(0) == 0)
        def _():
            dpre_ref[...] = jnp.zeros_like(dpre_ref)
            dpost_ref[...] = jnp.zeros_like(dpost_ref)

        dout_v = dout_ref[...]
        dn = 0.5 * dout_v
        fv = f_ref[...]
        rf = _rstd(fv)
        fn = fv * rf
        dpost_ref[...] += jnp.sum(dn * fn, axis=0, keepdims=True)
        df = _norm_bwd(dn, fn, rf, post_ref[...]).astype(BF16)
        df_ref[...] = df
        dh = jnp.zeros((tm, D_MODEL), F32)
        for c in range(2):
            cols = slice(c * FF_CHUNK, (c + 1) * FF_CHUNK)
            da = _dot_nt(df, wd_v[cols, :])
            gv = g_ref[:, cols].astype(F32)
            uv = u_ref[:, cols].astype(F32)
            sg = _sigmoid(gv)
            silu = gv * sg
            a_ref[:, cols] = (silu * uv).astype(BF16)
            dg = ((da * uv) * (sg * (1.0 + gv * (1.0 - sg)))).astype(BF16)
            du = (da * silu).astype(BF16)
            dgu_ref[:, cols] = dg
            dgu_ref[:, 2 * FF_CHUNK + c * FF_CHUNK:2 * FF_CHUNK + (c + 1) * FF_CHUNK] = du
            dh = dh + _dot_nt(dg, wgu_v[c]) + _dot_nt(du, wgu_v[2 + c])
        xv = x_ref[...]
        rx = _rstd(xv)
        xn = xv * rx
        h_ref[...] = (xn * pre_ref[...]).astype(BF16)
        dpre_ref[...] += jnp.sum(dh * xn, axis=0, keepdims=True)
        dx_ref[...] = dout_v + _norm_bwd(dh, xn, rx, pre_ref[...])

    args = [dout, x, f, g, u, pre, post, *wgu, *wd, mine]
    return pl.pallas_call(
        _ignoring(body, len(args), len(deps)), name="ffn_bwd", grid=(S // tm,),
        in_specs=[_rows(tm, D_MODEL), _rows(tm, D_MODEL), _rows(tm, D_MODEL), _rows(tm, D_FF), _rows(tm, D_FF),
                  _full((1, D_MODEL)), _full((1, D_MODEL)), _ANY, _ANY, _ANY, _ANY, _SMEM] + [_ANY] * len(deps),
        out_specs=[_rows(tm, D_MODEL), _rows(tm, D_MODEL), _rows(tm, 2 * D_FF), _rows(tm, D_FF), _rows(tm, D_MODEL),
                   _full((1, D_MODEL)), _full((1, D_MODEL))],
        out_shape=[jax.ShapeDtypeStruct((S, D_MODEL), F32), jax.ShapeDtypeStruct((S, D_MODEL), BF16),
                   jax.ShapeDtypeStruct((S, 2 * D_FF), BF16), jax.ShapeDtypeStruct((S, D_FF), BF16),
                   jax.ShapeDtypeStruct((S, D_MODEL), BF16),
                   jax.ShapeDtypeStruct((1, D_MODEL), F32), jax.ShapeDtypeStruct((1, D_MODEL), F32)],
        scratch_shapes=[pltpu.VMEM((N_SHARD, D_MODEL, FF_CHUNK), BF16), pltpu.VMEM((D_FF, D_MODEL), BF16),
                        pltpu.SemaphoreType.DMA((2 * N_SHARD,))],
        compiler_params=_params(dimension_semantics=("arbitrary",)),
    )(*args, *deps)


def _wgrad(lhs, rhs, m_block, n_block, name, column_shards=False, tk=2048, deps=()):
    S, M = lhs.shape
    N = rhs.shape[1]
    k_steps = S // tk

    def body(lhs_ref, rhs_ref, out_ref, out16_ref):
        k = pl.program_id(2)

        @pl.when(k == 0)
        def _():
            out_ref[...] = jnp.zeros_like(out_ref)
        out_ref[...] += _dot_tn(lhs_ref[...], rhs_ref[...])

        @pl.when(k == k_steps - 1)
        def _():
            out16_ref[...] = out_ref[...].astype(BF16)

    if column_shards:
        assert N == N_SHARD * n_block
        shape = (N_SHARD, M, n_block)
        out_spec = pl.BlockSpec((None, m_block, n_block), lambda i, j, k: (j, i, 0))
    else:
        shape = (M, N)
        out_spec = pl.BlockSpec((m_block, n_block), lambda i, j, k: (i, j))
    out, out16 = pl.pallas_call(
        _ignoring(body, 2, len(deps)), name=name, grid=(M // m_block, N // n_block, k_steps),
        in_specs=[pl.BlockSpec((tk, m_block), lambda i, j, k: (k, i)),
                  pl.BlockSpec((tk, n_block), lambda i, j, k: (k, j))] + [_ANY] * len(deps),
        out_specs=[out_spec, out_spec],
        out_shape=[jax.ShapeDtypeStruct(shape, F32), jax.ShapeDtypeStruct(shape, BF16)],
        compiler_params=_params(dimension_semantics=("arbitrary", "arbitrary", "arbitrary")),
    )(lhs, rhs, *deps)
    if not column_shards:
        out = out.reshape(N_SHARD, M // N_SHARD, N)
        out16 = out16.reshape(N_SHARD, M // N_SHARD, N)
    return out, out16


def _rope_tables(pos, invf):
    S = pos.shape[0]
    tm = MIXER_TILE

    def body(pos_ref, invf_ref, out_ref):
        ang = pos_ref[...].astype(F32) * invf_ref[...]
        cos, sin = jnp.cos(ang), jnp.sin(ang)
        lane = lax.broadcasted_iota(jnp.int32, ang.shape, 1) % HEAD_DIM
        first = lane < ROT_DIM // 2
        second = (lane >= ROT_DIM // 2) & (lane < ROT_DIM)
        out_ref[0] = jnp.where(lane < ROT_DIM, cos, 1.0)
        out_ref[1] = jnp.where(first, sin, 0.0)
        out_ref[2] = jnp.where(second, sin, 0.0)

    return pl.pallas_call(
        body, name="rope_tables", grid=(S // tm,),
        in_specs=[_rows(tm, 1), _full((1, _LANES))],
        out_specs=pl.BlockSpec((3, tm, _LANES), lambda i: (0, i, 0)),
        out_shape=jax.ShapeDtypeStruct((3, S, _LANES), F32),
        compiler_params=_params(dimension_semantics=("arbitrary",)),
    )(pos, invf)


def _table_spec(tm):
    return pl.BlockSpec((3, tm, _LANES), lambda i: (0, i, 0))


_HALF = ROT_DIM // 2
_LANES = 128


def _rope(t, tables):
    c, s_first, s_second = tables
    return t * c - pltpu.roll(t, _LANES - _HALF, axis=1) * s_first + pltpu.roll(t, _HALF, axis=1) * s_second


def _rope_transposed(t, tables):
    c, s_first, s_second = tables
    return t * c - pltpu.roll(t * s_first, _HALF, axis=1) + pltpu.roll(t * s_second, _LANES - _HALF, axis=1)


def _store_head_variants(ref, t):
    rolled = pltpu.roll(t, HEAD_DIM, axis=1)
    low = lax.broadcasted_iota(jnp.int32, t.shape, 1) < HEAD_DIM
    zero = jnp.zeros_like(t)
    ref[0] = jnp.where(low, t, zero).astype(BF16)
    ref[1] = jnp.where(low, zero, rolled).astype(BF16)
    ref[2] = jnp.where(low, rolled, zero).astype(BF16)
    ref[3] = jnp.where(low, zero, t).astype(BF16)


def _mixer_in_fwd(x, pre, w_in_t, mine, rope, deps=()):
    S = x.shape[0]
    tm = MIXER_TILE

    def body(x_ref, pre_ref, w_land, w_own, mine_ref, rope_ref, u_ref, q_ref, k_ref, v_ref, w_v, sem):
        _load_once(_gathered(w_land, w_own, w_v, mine_ref[0], rows=IN_WIDTH // N_SHARD), sem)
        xv = x_ref[...]
        h = ((xv * _rstd(xv)) * pre_ref[...]).astype(BF16)
        z = _dot_nt(h, w_v[...])
        tables = (rope_ref[0], rope_ref[1], rope_ref[2])
        u_ref[...] = z[:, :POOL_WIDTH]
        for t in range(ATTN_WIDTH // _LANES):
            lo = POOL_WIDTH + t * _LANES
            q_ref[:, t * _LANES:(t + 1) * _LANES] = (_rope(z[:, lo:lo + _LANES], tables) * ATTN_SCALE).astype(BF16)
        kv = POOL_WIDTH + ATTN_WIDTH
        _store_head_variants(k_ref, _rope(z[:, kv:kv + KV_WIDTH], tables))
        _store_head_variants(v_ref, z[:, kv + KV_WIDTH:])

    args = [x, pre, *w_in_t, mine, rope]
    variants = pl.BlockSpec((2 * N_KV_HEADS, tm, KV_WIDTH), lambda i: (0, i, 0))
    return pl.pallas_call(
        _ignoring(body, len(args), len(deps)), name="mixer_in_fwd", grid=(S // tm,),
        in_specs=[_rows(tm, D_MODEL), _full((1, D_MODEL)), _ANY, _ANY, _SMEM, _table_spec(tm)] + [_ANY] * len(deps),
        out_specs=[_rows(tm, POOL_WIDTH), _rows(tm, ATTN_WIDTH), variants, variants],
        out_shape=[jax.ShapeDtypeStruct((S, POOL_WIDTH), F32), jax.ShapeDtypeStruct((S, ATTN_WIDTH), BF16),
                   jax.ShapeDtypeStruct((2 * N_KV_HEADS, S, KV_WIDTH), BF16),
                   jax.ShapeDtypeStruct((2 * N_KV_HEADS, S, KV_WIDTH), BF16)],
        scratch_shapes=[pltpu.VMEM((IN_WIDTH, D_MODEL), BF16), pltpu.SemaphoreType.DMA((N_SHARD,))],
        compiler_params=_params(dimension_semantics=("arbitrary",)),
    )(*args, *deps)


def _mixer_in_bwd(dres, x, pre, w_in_t, mine, du, dq, dk, dv, rope, deps=()):
    S = x.shape[0]
    tm = MIXER_TILE

    def body(dres_ref, x_ref, pre_ref, w_land, w_own, mine_ref, du_ref, dq_ref, dk_ref, dv_ref, rope_ref,
             dx_ref, dz_ref, h_ref, dpre_ref, w_v, sem):
        _load_once(_gathered(w_land, w_own, w_v, mine_ref[0], rows=IN_WIDTH // N_SHARD), sem)

        @pl.when(pl.program_id(0) == 0)
        def _():
            dpre_ref[...] = jnp.zeros_like(dpre_ref)

        tables = (rope_ref[0], rope_ref[1], rope_ref[2])
        dz_ref[:, :POOL_WIDTH] = du_ref[...].astype(BF16)
        for t in range(ATTN_WIDTH // _LANES):
            lo = POOL_WIDTH + t * _LANES
            dz_ref[:, lo:lo + _LANES] = _rope_transposed(dq_ref[:, t * _LANES:(t + 1) * _LANES], tables).astype(BF16)
        kv = POOL_WIDTH + ATTN_WIDTH
        dz_ref[:, kv:kv + KV_WIDTH] = _rope_transposed(dk_ref[...], tables).astype(BF16)
        dz_ref[:, kv + KV_WIDTH:] = dv_ref[...].astype(BF16)
        dh = _dot(dz_ref[...], w_v[...])
        xv = x_ref[...]
        rx = _rstd(xv)
        xn = xv * rx
        h_ref[...] = (xn * pre_ref[...]).astype(BF16)
        dpre_ref[...] += jnp.sum(dh * xn, axis=0, keepdims=True)
        dx_ref[...] = dres_ref[...] + _norm_bwd(dh, xn, rx, pre_ref[...])

    args = [dres, x, pre, *w_in_t, mine, du, dq, dk, dv, rope]
    return pl.pallas_call(
        _ignoring(body, len(args), len(deps)), name="mixer_in_bwd", grid=(S // tm,),
        in_specs=[_rows(tm, D_MODEL), _rows(tm, D_MODEL), _full((1, D_MODEL)), _ANY, _ANY, _SMEM,
                  _rows(tm, POOL_WIDTH), _rows(tm, ATTN_WIDTH), _rows(tm, KV_WIDTH), _rows(tm, KV_WIDTH),
                  _table_spec(tm)] + [_ANY] * len(deps),
        out_specs=[_rows(tm, D_MODEL), _rows(tm, IN_WIDTH), _rows(tm, D_MODEL), _full((1, D_MODEL))],
        out_shape=[jax.ShapeDtypeStruct((S, D_MODEL), F32), jax.ShapeDtypeStruct((S, IN_WIDTH), BF16),
                   jax.ShapeDtypeStruct((S, D_MODEL), BF16), jax.ShapeDtypeStruct((1, D_MODEL), F32)],
        scratch_shapes=[pltpu.VMEM((IN_WIDTH, D_MODEL), BF16), pltpu.SemaphoreType.DMA((N_SHARD,))],
        compiler_params=_params(dimension_semantics=("arbitrary",)),
    )(*args, *deps)


def _pool_counts(tile_index, tm, width):
    t = tile_index * tm + lax.broadcasted_iota(jnp.int32, (tm, 1), 0)
    return jnp.minimum(t + 1, width).astype(F32)


def _pool_features(ext, u_tile, tile_index, tm):
    ds = []
    for gi, width in enumerate(POOL_WINDOWS):
        lanes = slice(gi * POOL_GROUP, (gi + 1) * POOL_GROUP)
        s = ext[:, lanes]
        shift = 1
        while shift < width:
            s = s + pltpu.roll(s, shift, axis=0)
            shift *= 2
        ds.append(s[HALO:, :] / _pool_counts(tile_index, tm, width) - u_tile[:, lanes])
    return ds


def _pool_fwd(u, w_pool, pool_scale, g_pool):
    S = u.shape[0]
    tm = MIXER_TILE

    def body(u_ref, w_ref, scale_ref, gain_ref, y_ref, ext_ref):
        i = pl.program_id(0)

        @pl.when(i == 0)
        def _():
            ext_ref[:HALO, :] = jnp.zeros((HALO, POOL_WIDTH), F32)

        u_tile = u_ref[...]
        ext_ref[HALO:, :] = u_tile
        ds = _pool_features(ext_ref[...], u_tile, i, tm)
        ext_ref[:HALO, :] = u_tile[tm - HALO:, :]
        ys = [_dot(ds[gi].astype(BF16), w_ref[gi].astype(BF16)) for gi in range(len(POOL_WINDOWS))]
        po = jnp.concatenate(ys, axis=1) * scale_ref[...]
        y_ref[...] = ((po * _rstd(po)) * gain_ref[...]).astype(BF16)

    return pl.pallas_call(
        body, name="pool_fwd", grid=(S // tm,),
        in_specs=[_rows(tm, POOL_WIDTH), _full((len(POOL_WINDOWS), POOL_GROUP, POOL_GROUP)),
                  _full((1, POOL_WIDTH)), _full((1, POOL_WIDTH))],
        out_specs=_rows(tm, POOL_WIDTH),
        out_shape=jax.ShapeDtypeStruct((S, POOL_WIDTH), BF16),
        scratch_shapes=[pltpu.VMEM((HALO + tm, POOL_WIDTH), F32)],
        compiler_params=_params(dimension_semantics=("arbitrary",)),
    )(u, w_pool, pool_scale, g_pool)


def _pool_bwd(dy, u, w_pool, pool_scale, g_pool):
    S = u.shape[0]
    tm = MIXER_TILE
    n_tiles = S // tm
    halo_blocks = tm // HALO

    def body(dy_ref, u_ref, uprev_ref, w_ref, scale_ref, gain_ref,
             du_ref, dw_ref, dscale_ref, dgain_ref, ext_ref, nxt_ref):
        i = pl.program_id(0)
        tile = n_tiles - 1 - i

        @pl.when(i == 0)
        def _():
            dw_ref[...] = jnp.zeros_like(dw_ref)
            dscale_ref[...] = jnp.zeros_like(dscale_ref)
            dgain_ref[...] = jnp.zeros_like(dgain_ref)
            nxt_ref[...] = jnp.zeros_like(nxt_ref)

        u_tile = u_ref[...]
        ext_ref[:HALO, :] = jnp.where(tile > 0, uprev_ref[...], 0.0)
        ext_ref[HALO:, :] = u_tile
        ds = _pool_features(ext_ref[...], u_tile, tile, tm)
        dsb = [d.astype(BF16) for d in ds]
        wb = [w_ref[gi].astype(BF16) for gi in range(len(POOL_WINDOWS))]
        yraw = jnp.concatenate([_dot(dsb[gi], wb[gi]) for gi in range(len(POOL_WINDOWS))], axis=1)
        po = yraw * scale_ref[...]
        r = _rstd(po)
        pn = po * r
        dyv = dy_ref[...]
        dgain_ref[...] += jnp.sum(dyv * pn, axis=0, keepdims=True)
        dpo = _norm_bwd(dyv, pn, r, gain_ref[...])
        dscale_ref[...] += jnp.sum(dpo * yraw, axis=0, keepdims=True)
        dyraw = (dpo * scale_ref[...]).astype(BF16)
        for gi, width in enumerate(POOL_WINDOWS):
            lanes = slice(gi * POOL_GROUP, (gi + 1) * POOL_GROUP)
            dw_ref[gi] += _dot_tn(dsb[gi], dyraw[:, lanes])
            dd = _dot_nt(dyraw[:, lanes], wb[gi])
            ddc = dd / _pool_counts(tile, tm, width)
            ext_ref[:tm, lanes] = ddc
            ext_ref[tm:, lanes] = nxt_ref[:, lanes]
            s = ext_ref[:, lanes]
            shift = 1
            while shift < width:
                s = s + pltpu.roll(s, HALO + tm - shift, axis=0)
                shift *= 2
            du_ref[:, lanes] = s[:tm, :] - dd
            nxt_ref[:, lanes] = ddc[:HALO, :]

    return pl.pallas_call(
        body, name="pool_bwd", grid=(n_tiles,),
        in_specs=[pl.BlockSpec((tm, POOL_WIDTH), lambda i: (n_tiles - 1 - i, 0)),
                  pl.BlockSpec((tm, POOL_WIDTH), lambda i: (n_tiles - 1 - i, 0)),
                  pl.BlockSpec((HALO, POOL_WIDTH), lambda i: (jnp.maximum((n_tiles - 1 - i) * halo_blocks - 1, 0), 0)),
                  _full((len(POOL_WINDOWS), POOL_GROUP, POOL_GROUP)), _full((1, POOL_WIDTH)), _full((1, POOL_WIDTH))],
        out_specs=[pl.BlockSpec((tm, POOL_WIDTH), lambda i: (n_tiles - 1 - i, 0)),
                   _full((len(POOL_WINDOWS), POOL_GROUP, POOL_GROUP)), _full((1, POOL_WIDTH)), _full((1, POOL_WIDTH))],
        out_shape=[jax.ShapeDtypeStruct((S, POOL_WIDTH), F32),
                   jax.ShapeDtypeStruct((len(POOL_WINDOWS), POOL_GROUP, POOL_GROUP), F32),
                   jax.ShapeDtypeStruct((1, POOL_WIDTH), F32), jax.ShapeDtypeStruct((1, POOL_WIDTH), F32)],
        scratch_shapes=[pltpu.VMEM((HALO + tm, POOL_WIDTH), F32), pltpu.VMEM((HALO, POOL_WIDTH), F32)],
        compiler_params=_params(dimension_semantics=("arbitrary",)),
    )(dy, u, u, w_pool, pool_scale, g_pool)


def _variant(head):
    return 2 * (head // (N_HEADS // N_KV_HEADS)) + head % 2


def _own_block(shape=(BLOCK, BLOCK)):
    r = lax.broadcasted_iota(jnp.int32, shape, 0)
    i = lax.broadcasted_iota(jnp.int32, shape, 1)
    return r <= i


def _fold_band(own, from_own, from_prev):
    return jnp.where(own, from_own, from_prev)


def _scores_by_head(own_ref, prev_ref, q_tiles):
    stacks = [jnp.concatenate(q_tiles[:2], axis=0), jnp.concatenate(q_tiles[2:], axis=0)]
    by_var = [_dot_nt(jnp.concatenate([own_ref[v], prev_ref[v]], axis=0), stacks[v // 2])
              for v in range(2 * N_KV_HEADS)]
    quadrant = lambda h, rows: by_var[_variant(h)][rows * BLOCK:(rows + 1) * BLOCK,
                                                   ((h // 2) % 2) * BLOCK:((h // 2) % 2 + 1) * BLOCK]
    return [quadrant(h, 0) for h in range(N_HEADS)], [quadrant(h, 1) for h in range(N_HEADS)]


def _softmax_t(s, sink):
    m = jnp.maximum(jnp.max(s, axis=0, keepdims=True), sink)
    p = jnp.exp(s - m)
    p_sink = jnp.exp(sink - m)
    inv = 1.0 / (jnp.sum(p, axis=0, keepdims=True) + p_sink)
    return p * inv, p_sink * inv


def _attn_fwd(q, kz, vz, sinks, g_attn):
    S = q.shape[0]
    nb = S // BLOCK

    def body(q_ref, kp_ref, kc_ref, vp_ref, vc_ref, sinks_ref, gain_ref, o_ref, y_ref):
        b = pl.program_id(0)
        own = _own_block()
        no_prev = jnp.where(b > 0, 0.0, NEG_INF)
        q_pairs = [q_ref[:, i * _LANES:(i + 1) * _LANES] for i in range(N_HEADS // 2)]
        s_own, s_prev = _scores_by_head(kc_ref, kp_ref, q_pairs)
        p_own, p_prev = [], []
        zero = jnp.zeros((BLOCK, BLOCK), F32)
        for h in range(N_HEADS):
            p, _ = _softmax_t(_fold_band(own, s_own[h], s_prev[h] + no_prev), sinks_ref[0, h])
            p_own.append(jnp.where(own, p, zero).astype(BF16))
            p_prev.append(jnp.where(own, zero, p).astype(BF16))
        pairs = []
        for i in range(N_HEADS // 2):
            acc = None
            for h in (2 * i, 2 * i + 1):
                part = _dot_tn(p_own[h], vc_ref[_variant(h)]) + _dot_tn(p_prev[h], vp_ref[_variant(h)])
                acc = part if acc is None else acc + part
            pairs.append(acc)
        o = jnp.concatenate(pairs, axis=1)
        o_ref[...] = o
        y_ref[...] = ((o * _rstd(o)) * gain_ref[...]).astype(BF16)

    variants = (2 * N_KV_HEADS, BLOCK, KV_WIDTH)
    prev = pl.BlockSpec(variants, lambda b: (0, jnp.maximum(b - 1, 0), 0))
    cur = pl.BlockSpec(variants, lambda b: (0, b, 0))
    return pl.pallas_call(
        body, name="attn_fwd", grid=(nb,),
        in_specs=[_rows(BLOCK, ATTN_WIDTH), prev, cur, prev, cur,
                  pl.BlockSpec(memory_space=pltpu.SMEM), _full((1, ATTN_WIDTH))],
        out_specs=[_rows(BLOCK, ATTN_WIDTH), _rows(BLOCK, ATTN_WIDTH)],
        out_shape=[jax.ShapeDtypeStruct((S, ATTN_WIDTH), F32), jax.ShapeDtypeStruct((S, ATTN_WIDTH), BF16)],
        compiler_params=_params(dimension_semantics=("arbitrary",)),
    )(q, kz, kz, vz, vz, sinks, g_attn)


def _attn_bwd(dy, o, q, kz, vz, sinks, g_attn):
    S = q.shape[0]
    nb = S // BLOCK

    def body(dy_ref, o_ref, q_ref, kp_ref, kc_ref, vp_ref, vc_ref, sinks_ref, gain_ref,
             dq_ref, dk_ref, dv_ref, dsink_ref, dgain_ref, kcarry_ref, vcarry_ref):
        b = pl.program_id(0)

        @pl.when(b == 0)
        def _():
            dsink_ref[...] = jnp.zeros_like(dsink_ref)
            dgain_ref[...] = jnp.zeros_like(dgain_ref)
            kcarry_ref[...] = jnp.zeros_like(kcarry_ref)
            vcarry_ref[...] = jnp.zeros_like(vcarry_ref)

        @pl.when(b < nb)
        def _():
            ov = o_ref[...]
            r = _rstd(ov)
            on = ov * r
            dyv = dy_ref[...]
            dgain_ref[...] += jnp.sum(dyv * on, axis=0, keepdims=True)
            do = _norm_bwd(dyv, on, r, gain_ref[...])
            own = _own_block()
            no_prev = jnp.where(b > 0, 0.0, NEG_INF)
            q_pairs = [q_ref[:, i * _LANES:(i + 1) * _LANES] for i in range(N_HEADS // 2)]
            do_pairs = [do[:, i * _LANES:(i + 1) * _LANES].astype(BF16) for i in range(N_HEADS // 2)]
            heads = range(N_HEADS)
            s_own, s_prev = _scores_by_head(kc_ref, kp_ref, q_pairs)
            dp_own, dp_prev = _scores_by_head(vc_ref, vp_ref, do_pairs)
            zero = jnp.zeros((BLOCK, BLOCK), F32)
            split = lambda t: (jnp.where(own, t, zero).astype(BF16), jnp.where(own, zero, t).astype(BF16))
            ds_parts, p_parts, sink_rows = [], [], []
            for h in heads:
                p, p_sink = _softmax_t(_fold_band(own, s_own[h], s_prev[h] + no_prev), sinks_ref[0, h])
                dp = _fold_band(own, dp_own[h], dp_prev[h])
                delta = jnp.sum(p * dp, axis=0, keepdims=True)
                ds_parts.append(split(p * (dp - delta)))
                p_parts.append(split(p))
                sink_rows.append(jnp.zeros((1, _LANES), F32) - jnp.sum(p_sink * delta))
            dsink_ref[...] += jnp.concatenate(sink_rows, axis=0)
            n_var = 2 * N_KV_HEADS
            dk_own, dk_prev, dv_own, dv_prev = ([None] * n_var for _ in range(4))
            add = lambda acc, var, t: acc.__setitem__(var, t if acc[var] is None else acc[var] + t)
            for i in range(N_HEADS // 2):
                dq_pair = None
                for h in (2 * i, 2 * i + 1):
                    var = _variant(h)
                    (ds_o, ds_p), (p_o, p_p) = ds_parts[h], p_parts[h]
                    part = _dot_tn(ds_o, kc_ref[var]) + _dot_tn(ds_p, kp_ref[var])
                    dq_pair = part if dq_pair is None else dq_pair + part
                    add(dk_own, var, _dot(ds_o, q_pairs[i]))
                    add(dk_prev, var, _dot(ds_p, q_pairs[i]))
                    add(dv_own, var, _dot(p_o, do_pairs[i]))
                    add(dv_prev, var, _dot(p_p, do_pairs[i]))
                dq_ref[:, i * _LANES:(i + 1) * _LANES] = dq_pair * ATTN_SCALE
            low = lax.broadcasted_iota(jnp.int32, (BLOCK, _LANES), 1) < HEAD_DIM

            def merge(acc):
                return jnp.where(low, acc[0] + pltpu.roll(acc[1], HEAD_DIM, axis=1),
                                 acc[3] + pltpu.roll(acc[2], HEAD_DIM, axis=1))
            dk_ref[...] = kcarry_ref[...] + merge(dk_prev)
            dv_ref[...] = vcarry_ref[...] + merge(dv_prev)
            kcarry_ref[...] = merge(dk_own)
            vcarry_ref[...] = merge(dv_own)

        @pl.when(b == nb)
        def _():
            dk_ref[...] = kcarry_ref[...]
            dv_ref[...] = vcarry_ref[...]

    cur = lambda b: (jnp.minimum(b, nb - 1), 0)
    prev = lambda b: (jnp.clip(b - 1, 0, nb - 1), 0)
    variants = (2 * N_KV_HEADS, BLOCK, KV_WIDTH)
    var_prev = pl.BlockSpec(variants, lambda b: (0, jnp.clip(b - 1, 0, nb - 1), 0))
    var_cur = pl.BlockSpec(variants, lambda b: (0, jnp.minimum(b, nb - 1), 0))
    return pl.pallas_call(
        body, name="attn_bwd", grid=(nb + 1,),
        in_specs=[pl.BlockSpec((BLOCK, ATTN_WIDTH), lambda b: (jnp.minimum(b, nb - 1), 1)),
                  pl.BlockSpec((BLOCK, ATTN_WIDTH), cur), pl.BlockSpec((BLOCK, ATTN_WIDTH), cur),
                  var_prev, var_cur, var_prev, var_cur,
                  pl.BlockSpec(memory_space=pltpu.SMEM), _full((1, ATTN_WIDTH))],
        out_specs=[pl.BlockSpec((BLOCK, ATTN_WIDTH), cur),
                   pl.BlockSpec((BLOCK, KV_WIDTH), prev), pl.BlockSpec((BLOCK, KV_WIDTH), prev),
                   _full((N_HEADS, _LANES)), _full((1, ATTN_WIDTH))],
        out_shape=[jax.ShapeDtypeStruct((S, ATTN_WIDTH), F32), jax.ShapeDtypeStruct((S, KV_WIDTH), F32),
                   jax.ShapeDtypeStruct((S, KV_WIDTH), F32), jax.ShapeDtypeStruct((N_HEADS, _LANES), F32),
                   jax.ShapeDtypeStruct((1, ATTN_WIDTH), F32)],
        scratch_shapes=[pltpu.VMEM((BLOCK, KV_WIDTH), F32), pltpu.VMEM((BLOCK, KV_WIDTH), F32)],
        compiler_params=_params(dimension_semantics=("arbitrary",)),
    )(dy, o, q, kz, kz, vz, vz, sinks, g_attn)


def _mixer_out_fwd(y_pool, y_attn, x, w_out, mine, post, deps=()):
    S = x.shape[0]
    tm = MIXER_TILE

    def body(yp_ref, ya_ref, x_ref, w_land, w_own, mine_ref, post_ref, out_ref, m_ref, y_ref, w_v, sem):
        _load_once(_gathered(w_land, w_own, w_v, mine_ref[0], rows=D_MODEL // N_SHARD), sem)
        y_ref[:, :POOL_WIDTH] = yp_ref[...]
        y_ref[:, POOL_WIDTH:] = ya_ref[...]
        m = _dot(y_ref[...], w_v[...])
        m_ref[...] = m
        out_ref[...] = x_ref[...] + (m * _rstd(m)) * post_ref[...]

    args = [y_pool, y_attn, x, *w_out, mine, post]
    return pl.pallas_call(
        _ignoring(body, len(args), len(deps)), name="mixer_out_fwd", grid=(S // tm,),
        in_specs=[_rows(tm, POOL_WIDTH), _rows(tm, ATTN_WIDTH), _rows(tm, D_MODEL), _ANY, _ANY, _SMEM,
                  _full((1, D_MODEL))] + [_ANY] * len(deps),
        out_specs=[_rows(tm, D_MODEL), _rows(tm, D_MODEL), _rows(tm, D_MODEL)],
        out_shape=[jax.ShapeDtypeStruct((S, D_MODEL), F32), jax.ShapeDtypeStruct((S, D_MODEL), F32),
                   jax.ShapeDtypeStruct((S, D_MODEL), BF16)],
        scratch_shapes=[pltpu.VMEM((D_MODEL, D_MODEL), BF16), pltpu.SemaphoreType.DMA((N_SHARD,))],
        compiler_params=_params(dimension_semantics=("arbitrary",)),
    )(*args, *deps)


def _mixer_out_bwd(dout, m, w_out, mine, post, deps=()):
    S = m.shape[0]
    tm = MIXER_TILE

    def body(dout_ref, m_ref, w_land, w_own, mine_ref, post_ref, dy_ref, dm_ref, dpost_ref, w_v, sem):
        _load_once(_gathered(w_land, w_own, w_v, mine_ref[0], rows=D_MODEL // N_SHARD), sem)

        @pl.when(pl.program_id(0) == 0)
        def _():
            dpost_ref[...] = jnp.zeros_like(dpost_ref)

        mv = m_ref[...]
        r = _rstd(mv)
        mn = mv * r
        dv = dout_ref[...]
        dpost_ref[...] += jnp.sum(dv * mn, axis=0, keepdims=True)
        dm = _norm_bwd(dv, mn, r, post_ref[...]).astype(BF16)
        dm_ref[...] = dm
        dy_ref[...] = _dot_nt(dm, w_v[...])

    args = [dout, m, *w_out, mine, post]
    return pl.pallas_call(
        _ignoring(body, len(args), len(deps)), name="mixer_out_bwd", grid=(S // tm,),
        in_specs=[_rows(tm, D_MODEL), _rows(tm, D_MODEL), _ANY, _ANY, _SMEM, _full((1, D_MODEL))] + [_ANY] * len(deps),
        out_specs=[_rows(tm, D_MODEL), _rows(tm, D_MODEL), _full((1, D_MODEL))],
        out_shape=[jax.ShapeDtypeStruct((S, D_MODEL), F32), jax.ShapeDtypeStruct((S, D_MODEL), BF16),
                   jax.ShapeDtypeStruct((1, D_MODEL), F32)],
        scratch_shapes=[pltpu.VMEM((D_MODEL, D_MODEL), BF16), pltpu.SemaphoreType.DMA((N_SHARD,))],
        compiler_params=_params(dimension_semantics=("arbitrary",)),
    )(*args, *deps)


def _inv_freq_row():
    inv_freq = ROPE_THETA ** (-jnp.arange(0, ROT_DIM, 2, dtype=F32) / ROT_DIM)
    per_head = jnp.concatenate([inv_freq, inv_freq, jnp.zeros((HEAD_DIM - ROT_DIM,), F32)])
    return jnp.tile(per_head, _LANES // HEAD_DIM).reshape(1, _LANES)


def _local_step(x, pos, target, small, mine, weights_of, grads_ready):
    rope = _rope_tables(pos, _inv_freq_row())
    wgu1, wd1 = weights_of("ffn1", (rope,))
    x1, g1, u1, f1 = _ffn_fwd(x, small["ffn1_pre"], small["ffn1_post"], wgu1, wd1, mine)
    w_in_t, w_out = weights_of("mixer", (x1,))
    u, q, k, v = _mixer_in_fwd(x1, small["mix_pre"], w_in_t, mine, rope)
    y_pool = _pool_fwd(u, small["w_pool"], small["pool_scale"], small["g_pool"])
    o, y_attn = _attn_fwd(q, k, v, small["sinks"], small["g_attn"])
    x2, m, y = _mixer_out_fwd(y_pool, y_attn, x1, w_out, mine, small["mix_post"])
    wgu2, wd2 = weights_of("ffn2", (x2,))
    dx3, g2, u2, f2, loss_acc = _ffn_fwd(x2, small["ffn2_pre"], small["ffn2_post"], wgu2, wd2, mine, target=target)
    grads = {"loss": loss_acc * (0.5 / D_MODEL)}
    dx2, h3, dgu2, a2, df2, grads["ffn2_pre"], grads["ffn2_post"] = _ffn_bwd(
        dx3, x2, f2, g2, u2, small["ffn2_pre"], small["ffn2_post"], wgu2, wd2, mine)
    dwgu2 = _wgrad(h3, dgu2, D_MODEL, FF_CHUNK, "wgrad_gu2", column_shards=True)
    dwd2 = _wgrad(a2, df2, FF_CHUNK, D_MODEL, "wgrad_down2")
    deps = grads_ready("ffn2", {"ffn2_w_gu": dwgu2, "ffn2_w_down": dwd2})
    dy, dm, grads["mix_post"] = _mixer_out_bwd(dx2, m, w_out, mine, small["mix_post"], deps=deps)
    dw_out = _wgrad(y, dm, D_MODEL, D_MODEL, "wgrad_out")
    dq, dk, dv, dsinks, grads["g_attn"] = _attn_bwd(dy, o, q, k, v, small["sinks"], small["g_attn"])
    grads["sinks"] = dsinks[:, 0].reshape(1, N_HEADS)
    du, grads["w_pool"], grads["pool_scale"], grads["g_pool"] = _pool_bwd(
        dy, u, small["w_pool"], small["pool_scale"], small["g_pool"])
    dx1, dz, h2, grads["mix_pre"] = _mixer_in_bwd(dx2, x1, small["mix_pre"], w_in_t, mine, du, dq, dk, dv, rope)
    dw_in_t = _wgrad(dz, h2, IN_WIDTH, D_MODEL, "wgrad_in")
    deps = grads_ready("mixer", {"w_in": dw_in_t, "w_out": dw_out})
    dx, h1, dgu1, a1, df1, grads["ffn1_pre"], grads["ffn1_post"] = _ffn_bwd(
        dx1, x, f1, g1, u1, small["ffn1_pre"], small["ffn1_post"], wgu1, wd1, mine, deps=deps)
    dwgu1 = _wgrad(h1, dgu1, D_MODEL, FF_CHUNK, "wgrad_gu1", column_shards=True)
    deps = grads_ready("ffn1_gu", {"ffn1_w_gu": dwgu1}, small=grads)
    dwd1 = _wgrad(a1, df1, FF_CHUNK, D_MODEL, "wgrad_down1", deps=deps)
    grads_ready("ffn1_down", {"ffn1_w_down": dwd1})
    return dx


def _place():
    return lax.axis_index("x"), lax.axis_index("y"), lax.axis_index("c")


def _other_chips(x, y):
    return [(1 - x, y), (x, 1 - y), (1 - x, 1 - y)]


def _hbm_shape(shape, dtype):
    return jax.ShapeDtypeStruct(shape, dtype)


_HBM = pl.BlockSpec(memory_space=pltpu.HBM)
_SEM = pl.BlockSpec(memory_space=pltpu.SEMAPHORE)
_EFFECT = pltpu.SideEffectType.DATAFLOW_SIDE_EFFECTING
GATHER, GATHER_HALF, REDUCE, BROADCAST = "gather", "gather_half", "reduce", "broadcast"
N_DEVICES = 8


def _in_hbm(a):
    return pltpu.with_memory_space_constraint(a, pltpu.HBM)


def _core_half(rows, c):
    return pl.ds(pl.multiple_of(c * (rows // 2), 16), rows // 2)


def _chip_copies(kind, srcs, lands, send_sems, recv_sems):
    x, y, c = _place()
    mine = 2 * x + y
    copies = []
    for w in range(len(srcs)):
        if kind == BROADCAST:
            peers = [(x ^ (k >> 2), y ^ ((k >> 1) & 1), c ^ (k & 1)) for k in range(1, N_DEVICES)]
        else:
            peers = [(px, py, c) for px, py in _other_chips(x, y)]
        for k, (px, py, pc) in enumerate(peers):
            if kind == GATHER:
                src, dst = srcs[w], lands[w].at[mine]
            elif kind == GATHER_HALF:
                half = _core_half(srcs[w].shape[0], c)
                src, dst = srcs[w].at[half, :], lands[w].at[mine, half, :]
            elif kind == BROADCAST:
                src, dst = srcs[w], lands[w].at[2 * mine + c]
            else:
                src, dst = srcs[w].at[2 * px + py], lands[w].at[k]
            pair = len(peers) * w + k
            copies.append(pltpu.make_async_remote_copy(
                src_ref=src, dst_ref=dst, send_sem=send_sems.at[pair], recv_sem=recv_sems.at[pair],
                device_id=(px, py, pc), device_id_type=MESH))
    return copies


def _landing_shape(kind, src):
    if kind == REDUCE:
        return (N_SHARD - 1,) + src.shape[1:]
    return ((N_DEVICES if kind == BROADCAST else N_SHARD),) + src.shape


def _peer_count(kind):
    return N_DEVICES - 1 if kind == BROADCAST else N_SHARD - 1


def _exchange_start(kinds, groups, name):
    sizes = [len(g) for g in groups]
    flat = [s for g in groups for s in g]
    n, ng = len(flat), len(groups)

    def body(*refs):
        srcs, lands = refs[:n], refs[n:2 * n]
        sems = refs[2 * n:2 * n + 2 * ng]
        token = refs[-1]
        start = 0
        for gi, size in enumerate(sizes):
            for cp in _chip_copies(kinds[gi], srcs[start:start + size], lands[start:start + size],
                                   sems[2 * gi], sems[2 * gi + 1]):
                cp.start()
            start += size
        token[...] = jnp.zeros_like(token)

    landings = [lax.empty(_landing_shape(kind, s), s.dtype) for kind, g in zip(kinds, groups) for s in g]
    sem_shapes = [pltpu.SemaphoreType.DMA((size * _peer_count(kind),)) for kind, size in zip(kinds, sizes)
                  for _ in range(2)]
    outs = pl.pallas_call(
        body, name=name,
        in_specs=[_HBM] * (2 * n),
        out_specs=[_SEM] * (2 * ng) + [_HBM] * (2 * n) + [pl.BlockSpec(memory_space=pltpu.VMEM)],
        out_shape=sem_shapes + [pltpu.HBM(a.shape, a.dtype) for a in flat + landings]
        + [jax.ShapeDtypeStruct((8, _LANES), F32)],
        input_output_aliases={i: 2 * ng + i for i in range(2 * n)},
        compiler_params=pltpu.CompilerParams(has_side_effects=_EFFECT),
    )(*[_in_hbm(a) for a in flat + landings])
    sems, srcs, lands, token = outs[:2 * ng], outs[2 * ng:2 * ng + n], outs[2 * ng + n:2 * ng + 2 * n], outs[-1]
    handles, start = [], 0
    for gi, size in enumerate(sizes):
        handles.append((sems[2 * gi], sems[2 * gi + 1], srcs[start:start + size], lands[start:start + size]))
        start += size
    return handles, token


def _exchange_wait(kind, handle, after, name):
    send_sems, recv_sems, srcs, lands = handle
    n = len(srcs)

    def body(*refs):
        copies = _chip_copies(kind, refs[:n], refs[n:2 * n], refs[2 * n], refs[2 * n + 1])
        for cp in copies:
            cp.wait_send()
        for cp in copies:
            cp.wait_recv()

    outs = pl.pallas_call(
        body, name=name,
        in_specs=[_HBM] * (2 * n) + [_SEM, _SEM] + [_ANY] * len(after),
        out_specs=[_HBM] * (2 * n),
        out_shape=[pltpu.HBM(a.shape, a.dtype) for a in list(srcs) + list(lands)],
        input_output_aliases={i: i for i in range(2 * n)},
        compiler_params=pltpu.CompilerParams(has_side_effects=_EFFECT),
    )(*srcs, *lands, send_sems, recv_sems, *after)
    return outs[:n], outs[n:]


def _swap_gathered_halves(lands, name):
    n = len(lands)

    def body(*refs):
        bufs = refs[n:2 * n]
        send_sems, recv_sems = refs[2 * n:]
        x, y, c = _place()
        mine = 2 * x + y
        sends, arrivals = [], []
        for w in range(n):
            rows = bufs[w].shape[1]
            for d in range(1, N_SHARD):
                slot = (mine + d) % N_SHARD
                sems = dict(send_sem=send_sems.at[(N_SHARD - 1) * w + d - 1],
                            recv_sem=recv_sems.at[(N_SHARD - 1) * w + d - 1],
                            device_id=(x, y, 1 - c), device_id_type=MESH)
                fetched = bufs[w].at[slot, _core_half(rows, c), :]
                missing = bufs[w].at[slot, _core_half(rows, 1 - c), :]
                sends.append(pltpu.make_async_remote_copy(src_ref=fetched, dst_ref=fetched, **sems))
                arrivals.append(pltpu.make_async_remote_copy(src_ref=missing, dst_ref=missing, **sems))
        for cp in sends:
            cp.start()
        for cp in arrivals:
            cp.wait_recv()
        for cp in sends:
            cp.wait_send()

    return pl.pallas_call(
        body, name=name, in_specs=[_ANY] * n, out_specs=[_ANY] * n,
        out_shape=[_hbm_shape(a.shape, a.dtype) for a in lands],
        input_output_aliases={i: i for i in range(n)},
        scratch_shapes=[pltpu.SemaphoreType.DMA((n * (N_SHARD - 1),)), pltpu.SemaphoreType.DMA((n * (N_SHARD - 1),))],
        compiler_params=pltpu.CompilerParams(has_side_effects=True),
    )(*lands)


def _swap_with_sibling(partials, name):
    n = len(partials)

    def body(*refs):
        ins, outs = refs[:n], refs[n:2 * n]
        send_sems, recv_sems = refs[2 * n:]
        x, y, c = _place()
        sends = [pltpu.make_async_remote_copy(
            src_ref=ins[w], dst_ref=outs[w], send_sem=send_sems.at[w], recv_sem=recv_sems.at[w],
            device_id=(x, y, 1 - c), device_id_type=MESH) for w in range(n)]
        for cp in sends:
            cp.start()
        for cp in sends:
            cp.wait_recv()
        for cp in sends:
            cp.wait_send()

    return pl.pallas_call(
        body, name=name,
        in_specs=[_ANY] * n, out_specs=[_ANY] * n,
        out_shape=[_hbm_shape(p.shape, p.dtype) for p in partials],
        scratch_shapes=[pltpu.SemaphoreType.DMA((n,)), pltpu.SemaphoreType.DMA((n,))],
        compiler_params=pltpu.CompilerParams(has_side_effects=True),
    )(*partials)


def _row_block(rows, cap):
    best = None
    for cand in range(16, min(rows, cap) + 1, 16):
        if rows % cand == 0:
            best = cand
    assert best is not None, rows
    return best


def _chip_partial(own, received, shard, name):
    _, R, C = own.shape
    rb = _row_block(R, 512)

    def body(shard_ref, own_ref, rec_ref, out_ref):
        acc = own_ref[...]
        for k in range(3):
            acc = acc + rec_ref[k].astype(F32)
        out_ref[...] = acc

    return pl.pallas_call(
        body, name=name,
        grid_spec=pltpu.PrefetchScalarGridSpec(
            num_scalar_prefetch=1, grid=(R // rb,),
            in_specs=[pl.BlockSpec((None, rb, C), lambda i, s: (s[0], i, 0)),
                      pl.BlockSpec((3, rb, C), lambda i, s: (0, i, 0))],
            out_specs=pl.BlockSpec((rb, C), lambda i, s: (i, 0))),
        out_shape=jax.ShapeDtypeStruct((R, C), F32),
        compiler_params=_params(dimension_semantics=("arbitrary",)),
    )(shard, own, received)


def _adamw(w, m, v, g_parts, name, slot=None):
    R, C = w.shape
    by_device = slot is not None
    rb = _row_block(R, 256) if R % 16 == 0 else R

    def body(w_ref, m_ref, v_ref, *refs):
        g_refs, (grad_ref, delta_ref, m_out, v_out) = refs[:-4], refs[-4:]
        if by_device:
            own_ref, land_ref, slot_ref = g_refs
            part = lambda d: jnp.where(slot_ref[0] == d, own_ref[...], land_ref[d])
            g = part(0)
            for d in range(1, N_DEVICES):
                g = g + part(d)
        else:
            g = g_refs[0][...]
            for g_ref in g_refs[1:]:
                g = g + g_ref[...]
        grad_ref[...] = g
        new_m = ADAM_B1 * m_ref[...] + (1.0 - ADAM_B1) * g
        new_v = ADAM_B2 * v_ref[...] + (1.0 - ADAM_B2) * (g * g)
        m_hat = new_m / (1.0 - ADAM_B1 ** ADAM_STEP)
        v_hat = new_v / (1.0 - ADAM_B2 ** ADAM_STEP)
        delta_ref[...] = -ADAM_LR * (m_hat / (jnp.sqrt(v_hat) + ADAM_EPS) + ADAM_WD * w_ref[...])
        m_out[...] = new_m
        v_out[...] = new_v

    spec = pl.BlockSpec((rb, C), lambda i: (i, 0))
    if by_device:
        g_specs = [spec, pl.BlockSpec((N_DEVICES, rb, C), lambda i: (0, i, 0)), _SMEM]
        g_parts = list(g_parts) + [slot]
    else:
        g_specs = [spec] * len(g_parts)
    return pl.pallas_call(
        body, name=name, grid=(R // rb,),
        in_specs=[spec, spec, spec] + g_specs,
        out_specs=[spec] * 4,
        out_shape=[jax.ShapeDtypeStruct((R, C), F32)] * 4,
        compiler_params=_params(dimension_semantics=("arbitrary",)),
    )(w, m, v, *g_parts)


SMALL_NAMES = ("ffn1_pre", "ffn1_post", "mix_pre", "pool_scale", "sinks", "g_pool", "g_attn", "mix_post",
               "ffn2_pre", "ffn2_post", "w_pool")
_SLAB_PART = 8 * _LANES


SLAB_NAMES = SMALL_NAMES + ("loss",)


def _to_slab(parts):
    rows = []
    for name in SLAB_NAMES:
        flat = parts[name].reshape(-1) if name in parts else jnp.zeros((_SLAB_PART,), F32)
        padded = -(-flat.shape[0] // _SLAB_PART) * _SLAB_PART
        rows.append(jnp.pad(flat, (0, padded - flat.shape[0])).reshape(-1, _LANES))
    return jnp.concatenate(rows, axis=0)


def _from_slab(slab, like):
    out, row = {}, 0
    for name in SLAB_NAMES:
        size = like[name].size
        rows = -(-size // _SLAB_PART) * (_SLAB_PART // _LANES)
        out[name] = slab[row:row + rows].reshape(-1)[:size].reshape(like[name].shape)
        row += rows
    return out


BIG_NAMES = ("ffn1_w_gu", "ffn1_w_down", "w_in", "w_out", "ffn2_w_gu", "ffn2_w_down")
WEIGHT_ORDER = ("ffn1_pre", "ffn1_w_gu", "ffn1_w_down", "ffn1_post", "mix_pre", "w_in", "w_pool", "pool_scale",
                "sinks", "g_pool", "g_attn", "w_out", "mix_post", "ffn2_pre", "ffn2_w_gu", "ffn2_w_down", "ffn2_post")


def kernel(x, positions, ffn1_pre, ffn1_w_gu, ffn1_w_down, ffn1_post, mix_pre, w_in, w_pool, pool_scale, sinks, g_pool, g_attn, w_out, mix_post, ffn2_pre, ffn2_w_gu, ffn2_w_down, ffn2_post, loss_target, m_ffn1_pre, m_ffn1_w_gu, m_ffn1_w_down, m_ffn1_post, m_mix_pre, m_w_in, m_w_pool, m_pool_scale, m_sinks, m_g_pool, m_g_attn, m_w_out, m_mix_post, m_ffn2_pre, m_ffn2_w_gu, m_ffn2_w_down, m_ffn2_post, v_ffn1_pre, v_ffn1_w_gu, v_ffn1_w_down, v_ffn1_post, v_mix_pre, v_w_in, v_w_pool, v_pool_scale, v_sinks, v_g_pool, v_g_attn, v_w_out, v_mix_post, v_ffn2_pre, v_ffn2_w_gu, v_ffn2_w_down, v_ffn2_post):
    given = dict(locals())
    weights = {n: given[n][0] for n in WEIGHT_ORDER}
    moments_m = {n: given["m_" + n][0] for n in WEIGHT_ORDER}
    moments_v = {n: given["v_" + n][0] for n in WEIGHT_ORDER}
    S = x.shape[1]
    shard = (2 * lax.axis_index("x") + lax.axis_index("y")).astype(jnp.int32).reshape(1)

    local16 = {n: weights[n].astype(BF16) for n in BIG_NAMES if n != "w_in"}
    local16["w_in"] = weights["w_in"].T.astype(BF16)
    gather_groups = {"ffn1": ("ffn1_w_gu", "ffn1_w_down"), "mixer": ("w_in", "w_out"),
                     "ffn2": ("ffn2_w_gu", "ffn2_w_down")}
    gather_kinds = {"ffn1": GATHER_HALF, "mixer": GATHER, "ffn2": GATHER}
    handles, _ = _exchange_start(list(gather_kinds.values()),
                                 [[local16[n] for n in names] for names in gather_groups.values()], "gather_start")
    gather_handles = dict(zip(gather_groups, handles))

    def weights_of(group, after):
        kind = gather_kinds[group]
        owns, lands = _exchange_wait(kind, gather_handles[group], list(after), "gather_wait_" + group)
        if kind == GATHER_HALF:
            lands = _swap_gathered_halves(lands, "swap_gathered_" + group)
        return list(zip(lands, owns))

    pending, last_token = {}, []

    def grads_ready(group, grads, small=None):
        names = list(grads)
        kinds, sources = [REDUCE], [[grads[n][1] for n in names]]
        if small is not None:
            kinds, sources = kinds + [BROADCAST], sources + [[_to_slab(small)]]
        handles, token = _exchange_start(kinds, sources, "reduce_start_" + group)
        handle = handles[0]
        if small is not None:
            pending["small"] = handles[1]
        pending[group] = (names, handle, [grads[n][0] for n in names])
        last_token[:] = [token]
        return [token]

    small = {n: (weights[n] if weights[n].ndim > 1 else weights[n].reshape(1, -1)) for n in SMALL_NAMES}
    dx = _local_step(x[0], positions.reshape(S, 1), loss_target[0], small, shard, weights_of, grads_ready)

    grad, delta, new_m, new_v = {}, {}, {}, {}

    def finish(groups, after):
        names, partials = [], []
        for group in groups:
            group_names, handle, own32 = pending[group]
            _, received = _exchange_wait(REDUCE, handle, after, "reduce_wait_" + group)
            names += group_names
            partials += [_chip_partial(g32, rec, shard, "chip_partial_" + n)
                         for n, g32, rec in zip(group_names, own32, received)]
        siblings = _swap_with_sibling(partials, "swap_" + groups[0])
        for name, mine, theirs in zip(names, partials, siblings):
            if name == "w_in":
                mine, theirs = mine.T, theirs.T
            grad[name], delta[name], new_m[name], new_v[name] = _adamw(
                weights[name], moments_m[name], moments_v[name], [mine, theirs], "adamw_" + name)
        return [grad[names[-1]]]

    after = finish(["ffn2"], last_token)
    after = finish(["mixer"], after)
    (own_slab,), (slab_landing,) = _exchange_wait(BROADCAST, pending["small"], after, "reduce_wait_small")
    device = (2 * shard + lax.axis_index("c")).astype(jnp.int32)
    small_like = dict({n: small[n] for n in SMALL_NAMES}, loss=jnp.zeros((8, _LANES), F32))
    slabs = _adamw(_to_slab(small), _to_slab({n: moments_m[n] for n in SMALL_NAMES}),
                   _to_slab({n: moments_v[n] for n in SMALL_NAMES}), [own_slab, slab_landing], "adamw_small",
                   slot=device)
    for store, slab in zip((grad, delta, new_m, new_v), slabs):
        store.update(_from_slab(slab, small_like))
    loss = grad["loss"][0, 0]
    after = finish(["ffn1_gu"], [slabs[0]])
    finish(["ffn1_down"], after)

    def out(store):
        return [store[n].reshape(given[n].shape) for n in WEIGHT_ORDER]
    return (loss, dx[None], *out(grad), *out(delta), *out(new_m), *out(new_v))
```

```python
import functools

import jax
import jax.numpy as jnp
from jax import lax
from jax.experimental import pallas as pl
from jax.experimental.pallas import tpu as pltpu

F32 = jnp.float32
BF16 = jnp.bfloat16

D_MODEL = 1024
D_FF = 2816
N_SHARD = 4
FF_CHUNK = D_FF // 2
POOL_WINDOWS = (2, 4, 8, 16)
POOL_WIDTH = 512
POOL_GROUP = 128
HALO = 16
HEAD_DIM = 64
N_HEADS = 8
N_KV_HEADS = 2
ATTN_WIDTH = 512
KV_WIDTH = 128
IN_WIDTH = 1280
BLOCK = 128
ROT_DIM = 16
ROPE_THETA = 500000.0
EPS = 1e-6
NEG_INF = -1e30
ATTN_SCALE = HEAD_DIM ** -0.5

ADAM_LR = 0.001
ADAM_B1 = 0.9
ADAM_B2 = 0.999
ADAM_EPS = 1e-08
ADAM_WD = 0.01
ADAM_STEP = 10

VMEM_LIMIT = 60 * 1024 * 1024
FFN_FWD_TILE = 512
FFN_BWD_TILE = 256
MIXER_TILE = 512

MESH = pl.DeviceIdType.MESH


def _params(**kw):
    return pltpu.CompilerParams(vmem_limit_bytes=VMEM_LIMIT, **kw)


def _dot(a, b):
    return jnp.dot(a, b, preferred_element_type=F32)


def _dot_nt(a, b):
    return lax.dot_general(a, b, (((1,), (1,)), ((), ())), preferred_element_type=F32)


def _dot_tn(a, b):
    return lax.dot_general(a, b, (((0,), (0,)), ((), ())), preferred_element_type=F32)


def _rstd(x):
    return lax.rsqrt(jnp.mean(x * x, axis=-1, keepdims=True) + EPS)


def _norm_bwd(dy, xn, r, gain):
    dxn = dy * gain
    return r * (dxn - xn * jnp.mean(dxn * xn, axis=-1, keepdims=True))


def _sigmoid(x):
    return 1.0 / (1.0 + jnp.exp(-x))


def _full(shape):
    return pl.BlockSpec(shape, lambda *_: (0,) * len(shape))


def _rows(tile, width, col=0):
    return pl.BlockSpec((tile, width), lambda i: (i, col))


_ANY = pl.BlockSpec(memory_space=pl.ANY)


_SMEM = pl.BlockSpec(memory_space=pltpu.SMEM)


def _load_once(pairs, sem):
    @pl.when(pl.program_id(0) == 0)
    def _():
        copies = [pltpu.make_async_copy(src, dst, sem.at[n]) for n, (src, dst) in enumerate(pairs)]
        for cp in copies:
            cp.start()
        for cp in copies:
            cp.wait()


def _gathered(land_ref, own_ref, vmem_ref, mine, rows=None):
    def dst(slot):
        if rows is None:
            return vmem_ref.at[slot]
        return vmem_ref.at[pl.ds(pl.multiple_of(slot * rows, 16), rows), :]
    pairs = [(land_ref.at[(mine + d) % N_SHARD], dst((mine + d) % N_SHARD)) for d in range(1, N_SHARD)]
    return pairs + [(own_ref, dst(mine))]


def _ignoring(body, start, count):
    def wrapped(*refs):
        return body(*refs[:start], *refs[start + count:])
    return wrapped


def _ffn_fwd(x, pre, post, wgu, wd, mine, target=None, deps=()):
    S = x.shape[0]
    tm = FFN_FWD_TILE
    with_loss = target is not None

    def body(*refs):
        if with_loss:
            (x_ref, pre_ref, post_ref, wgu_land, wgu_own, wd_land, wd_own, mine_ref, tgt_ref,
             out_ref, g_ref, u_ref, f_ref, loss_ref, wgu_v, wd_v, sem) = refs
        else:
            (x_ref, pre_ref, post_ref, wgu_land, wgu_own, wd_land, wd_own, mine_ref,
             out_ref, g_ref, u_ref, f_ref, wgu_v, wd_v, sem) = refs
        _load_once(_gathered(wgu_land, wgu_own, wgu_v, mine_ref[0])
                   + _gathered(wd_land, wd_own, wd_v, mine_ref[0], rows=D_FF // N_SHARD), sem)
        xv = x_ref[...]
        h = ((xv * _rstd(xv)) * pre_ref[...]).astype(BF16)
        facc = jnp.zeros((tm, D_MODEL), F32)
        for c in range(2):
            cols = slice(c * FF_CHUNK, (c + 1) * FF_CHUNK)
            g = _dot(h, wgu_v[c])
            u = _dot(h, wgu_v[2 + c])
            g_ref[:, cols] = g.astype(BF16)
            u_ref[:, cols] = u.astype(BF16)
            a = (g * _sigmoid(g)) * u
            facc = facc + _dot(a.astype(BF16), wd_v[cols, :])
        f_ref[...] = facc
        out = xv + 0.5 * ((facc * _rstd(facc)) * post_ref[...])
        if with_loss:
            diff = out - tgt_ref[...]
            out_ref[...] = diff * (1.0 / D_MODEL)

            @pl.when(pl.program_id(0) == 0)
            def _():
                loss_ref[...] = jnp.zeros_like(loss_ref)
            loss_ref[...] += jnp.sum(diff * diff)
        else:
            out_ref[...] = out

    in_specs = [_rows(tm, D_MODEL), _full((1, D_MODEL)), _full((1, D_MODEL)), _ANY, _ANY, _ANY, _ANY, _SMEM]
    args = [x, pre, post, *wgu, *wd, mine]
    out_shape = [jax.ShapeDtypeStruct((S, D_MODEL), F32), jax.ShapeDtypeStruct((S, D_FF), BF16),
                 jax.ShapeDtypeStruct((S, D_FF), BF16), jax.ShapeDtypeStruct((S, D_MODEL), F32)]
    out_specs = [_rows(tm, D_MODEL), _rows(tm, D_FF), _rows(tm, D_FF), _rows(tm, D_MODEL)]
    if with_loss:
        in_specs.append(_rows(tm, D_MODEL))
        args.append(target)
        out_shape.append(jax.ShapeDtypeStruct((8, 128), F32))
        out_specs.append(_full((8, 128)))
    return pl.pallas_call(
        _ignoring(body, len(args), len(deps)), name="ffn_fwd_loss" if with_loss else "ffn_fwd",
        grid=(S // tm,), in_specs=in_specs + [_ANY] * len(deps), out_specs=out_specs, out_shape=out_shape,
        scratch_shapes=[pltpu.VMEM((N_SHARD, D_MODEL, FF_CHUNK), BF16), pltpu.VMEM((D_FF, D_MODEL), BF16),
                        pltpu.SemaphoreType.DMA((2 * N_SHARD,))],
        compiler_params=_params(dimension_semantics=("arbitrary",)),
    )(*args, *deps)


def _ffn_bwd(dout, x, f, g, u, pre, post, wgu, wd, mine, deps=()):
    S = x.shape[0]
    tm = FFN_BWD_TILE

    def body(dout_ref, x_ref, f_ref, g_ref, u_ref, pre_ref, post_ref, wgu_land, wgu_own, wd_land, wd_own, mine_ref,
             dx_ref, h_ref, dgu_ref, a_ref, df_ref, dpre_ref, dpost_ref, wgu_v, wd_v, sem):
        _load_once(_gathered(wgu_land, wgu_own, wgu_v, mine_ref[0])
                   + _gathered(wd_land, wd_own, wd_v, mine_ref[0], rows=D_FF // N_SHARD), sem)

        @pl.when(pl.program_id(0) == 0)
        def _():
            dpre_ref[...] = jnp.zeros_like(dpre_ref)
            dpost_ref[...] = jnp.zeros_like(dpost_ref)

        dout_v = dout_ref[...]
        dn = 0.5 * dout_v
        fv = f_ref[...]
        rf = _rstd(fv)
        fn = fv * rf
        dpost_ref[...] += jnp.sum(dn * fn, axis=0, keepdims=True)
        df = _norm_bwd(dn, fn, rf, post_ref[...]).astype(BF16)
        df_ref[...] = df
        dh = jnp.zeros((tm, D_MODEL), F32)
        for c in range(2):
            cols = slice(c * FF_CHUNK, (c + 1) * FF_CHUNK)
            da = _dot_nt(df, wd_v[cols, :])
            gv = g_ref[:, cols].astype(F32)
            uv = u_ref[:, cols].astype(F32)
            sg = _sigmoid(gv)
            silu = gv * sg
            a_ref[:, cols] = (silu * uv).astype(BF16)
            dg = ((da * uv) * (sg * (1.0 + gv * (1.0 - sg)))).astype(BF16)
            du = (da * silu).astype(BF16)
            dgu_ref[:, cols] = dg
            dgu_ref[:, 2 * FF_CHUNK + c * FF_CHUNK:2 * FF_CHUNK + (c + 1) * FF_CHUNK] = du
            dh = dh + _dot_nt(dg, wgu_v[c]) + _dot_nt(du, wgu_v[2 + c])
        xv = x_ref[...]
        rx = _rstd(xv)
        xn = xv * rx
        h_ref[...] = (xn * pre_ref[...]).astype(BF16)
        dpre_ref[...] += jnp.sum(dh * xn, axis=0, keepdims=True)
        dx_ref[...] = dout_v + _norm_bwd(dh, xn, rx, pre_ref[...])

    args = [dout, x, f, g, u, pre, post, *wgu, *wd, mine]
    return pl.pallas_call(
        _ignoring(body, len(args), len(deps)), name="ffn_bwd", grid=(S // tm,),
        in_specs=[_rows(tm, D_MODEL), _rows(tm, D_MODEL), _rows(tm, D_MODEL), _rows(tm, D_FF), _rows(tm, D_FF),
                  _full((1, D_MODEL)), _full((1, D_MODEL)), _ANY, _ANY, _ANY, _ANY, _SMEM] + [_ANY] * len(deps),
        out_specs=[_rows(tm, D_MODEL), _rows(tm, D_MODEL), _rows(tm, 2 * D_FF), _rows(tm, D_FF), _rows(tm, D_MODEL),
                   _full((1, D_MODEL)), _full((1, D_MODEL))],
        out_shape=[jax.ShapeDtypeStruct((S, D_MODEL), F32), jax.ShapeDtypeStruct((S, D_MODEL), BF16),
                   jax.ShapeDtypeStruct((S, 2 * D_FF), BF16), jax.ShapeDtypeStruct((S, D_FF), BF16),
                   jax.ShapeDtypeStruct((S, D_MODEL), BF16),
                   jax.ShapeDtypeStruct((1, D_MODEL), F32), jax.ShapeDtypeStruct((1, D_MODEL), F32)],
        scratch_shapes=[pltpu.VMEM((N_SHARD, D_MODEL, FF_CHUNK), BF16), pltpu.VMEM((D_FF, D_MODEL), BF16),
                        pltpu.SemaphoreType.DMA((2 * N_SHARD,))],
        compiler_params=_params(dimension_semantics=("arbitrary",)),
    )(*args, *deps)


def _wgrad(lhs, rhs, m_block, n_block, name, column_shards=False, tk=2048, deps=()):
    S, M = lhs.shape
    N = rhs.shape[1]
    k_steps = S // tk

    def body(lhs_ref, rhs_ref, out_ref, out16_ref):
        k = pl.program_id(2)

        @pl.when(k == 0)
        def _():
            out_ref[...] = jnp.zeros_like(out_ref)
        out_ref[...] += _dot_tn(lhs_ref[...], rhs_ref[...])

        @pl.when(k == k_steps - 1)
        def _():
            out16_ref[...] = out_ref[...].astype(BF16)

    if column_shards:
        assert N == N_SHARD * n_block
        shape = (N_SHARD, M, n_block)
        out_spec = pl.BlockSpec((None, m_block, n_block), lambda i, j, k: (j, i, 0))
    else:
        shape = (M, N)
        out_spec = pl.BlockSpec((m_block, n_block), lambda i, j, k: (i, j))
    out, out16 = pl.pallas_call(
        _ignoring(body, 2, len(deps)), name=name, grid=(M // m_block, N // n_block, k_steps),
        in_specs=[pl.BlockSpec((tk, m_block), lambda i, j, k: (k, i)),
                  pl.BlockSpec((tk, n_block), lambda i, j, k: (k, j))] + [_ANY] * len(deps),
        out_specs=[out_spec, out_spec],
        out_shape=[jax.ShapeDtypeStruct(shape, F32), jax.ShapeDtypeStruct(shape, BF16)],
        compiler_params=_params(dimension_semantics=("arbitrary", "arbitrary", "arbitrary")),
    )(lhs, rhs, *deps)
    if not column_shards:
        out = out.reshape(N_SHARD, M // N_SHARD, N)
        out16 = out16.reshape(N_SHARD, M // N_SHARD, N)
    return out, out16


def _rope_tables(pos, invf):
    S = pos.shape[0]
    tm = MIXER_TILE

    def body(pos_ref, invf_ref, out_ref):
        ang = pos_ref[...].astype(F32) * invf_ref[...]
        cos, sin = jnp.cos(ang), jnp.sin(ang)
        lane = lax.broadcasted_iota(jnp.int32, ang.shape, 1) % HEAD_DIM
        first = lane < ROT_DIM // 2
        second = (lane >= ROT_DIM // 2) & (lane < ROT_DIM)
        out_ref[0] = jnp.where(lane < ROT_DIM, cos, 1.0)
        out_ref[1] = jnp.where(first, sin, 0.0)
        out_ref[2] = jnp.where(second, sin, 0.0)

    return pl.pallas_call(
        body, name="rope_tables", grid=(S // tm,),
        in_specs=[_rows(tm, 1), _full((1, _LANES))],
        out_specs=pl.BlockSpec((3, tm, _LANES), lambda i: (0, i, 0)),
        out_shape=jax.ShapeDtypeStruct((3, S, _LANES), F32),
        compiler_params=_params(dimension_semantics=("arbitrary",)),
    )(pos, invf)


def _table_spec(tm):
    return pl.BlockSpec((3, tm, _LANES), lambda i: (0, i, 0))


_HALF = ROT_DIM // 2
_LANES = 128


def _rope(t, tables):
    c, s_first, s_second = tables
    return t * c - pltpu.roll(t, _LANES - _HALF, axis=1) * s_first + pltpu.roll(t, _HALF, axis=1) * s_second


def _rope_transposed(t, tables):
    c, s_first, s_second = tables
    return t * c - pltpu.roll(t * s_first, _HALF, axis=1) + pltpu.roll(t * s_second, _LANES - _HALF, axis=1)


def _store_head_variants(ref, t):
    rolled = pltpu.roll(t, HEAD_DIM, axis=1)
    low = lax.broadcasted_iota(jnp.int32, t.shape, 1) < HEAD_DIM
    zero = jnp.zeros_like(t)
    ref[0] = jnp.where(low, t, zero).astype(BF16)
    ref[1] = jnp.where(low, zero, rolled).astype(BF16)
    ref[2] = jnp.where(low, rolled, zero).astype(BF16)
    ref[3] = jnp.where(low, zero, t).astype(BF16)


def _mixer_in_fwd(x, pre, w_in_t, mine, rope, deps=()):
    S = x.shape[0]
    tm = MIXER_TILE

    def body(x_ref, pre_ref, w_land, w_own, mine_ref, rope_ref, u_ref, q_ref, k_ref, v_ref, w_v, sem):
        _load_once(_gathered(w_land, w_own, w_v, mine_ref[0], rows=IN_WIDTH // N_SHARD), sem)
        xv = x_ref[...]
        h = ((xv * _rstd(xv)) * pre_ref[...]).astype(BF16)
        z = _dot_nt(h, w_v[...])
        tables = (rope_ref[0], rope_ref[1], rope_ref[2])
        u_ref[...] = z[:, :POOL_WIDTH]
        for t in range(ATTN_WIDTH // _LANES):
            lo = POOL_WIDTH + t * _LANES
            q_ref[:, t * _LANES:(t + 1) * _LANES] = (_rope(z[:, lo:lo + _LANES], tables) * ATTN_SCALE).astype(BF16)
        kv = POOL_WIDTH + ATTN_WIDTH
        _store_head_variants(k_ref, _rope(z[:, kv:kv + KV_WIDTH], tables))
        _store_head_variants(v_ref, z[:, kv + KV_WIDTH:])

    args = [x, pre, *w_in_t, mine, rope]
    variants = pl.BlockSpec((2 * N_KV_HEADS, tm, KV_WIDTH), lambda i: (0, i, 0))
    return pl.pallas_call(
        _ignoring(body, len(args), len(deps)), name="mixer_in_fwd", grid=(S // tm,),
        in_specs=[_rows(tm, D_MODEL), _full((1, D_MODEL)), _ANY, _ANY, _SMEM, _table_spec(tm)] + [_ANY] * len(deps),
        out_specs=[_rows(tm, POOL_WIDTH), _rows(tm, ATTN_WIDTH), variants, variants],
        out_shape=[jax.ShapeDtypeStruct((S, POOL_WIDTH), F32), jax.ShapeDtypeStruct((S, ATTN_WIDTH), BF16),
                   jax.ShapeDtypeStruct((2 * N_KV_HEADS, S, KV_WIDTH), BF16),
                   jax.ShapeDtypeStruct((2 * N_KV_HEADS, S, KV_WIDTH), BF16)],
        scratch_shapes=[pltpu.VMEM((IN_WIDTH, D_MODEL), BF16), pltpu.SemaphoreType.DMA((N_SHARD,))],
        compiler_params=_params(dimension_semantics=("arbitrary",)),
    )(*args, *deps)


def _mixer_in_bwd(dres, x, pre, w_in_t, mine, du, dq, dk, dv, rope, deps=()):
    S = x.shape[0]
    tm = MIXER_TILE

    def body(dres_ref, x_ref, pre_ref, w_land, w_own, mine_ref, du_ref, dq_ref, dk_ref, dv_ref, rope_ref,
             dx_ref, dz_ref, h_ref, dpre_ref, w_v, sem):
        _load_once(_gathered(w_land, w_own, w_v, mine_ref[0], rows=IN_WIDTH // N_SHARD), sem)

        @pl.when(pl.program_id(0) == 0)
        def _():
            dpre_ref[...] = jnp.zeros_like(dpre_ref)

        tables = (rope_ref[0], rope_ref[1], rope_ref[2])
        dz_ref[:, :POOL_WIDTH] = du_ref[...].astype(BF16)
        for t in range(ATTN_WIDTH // _LANES):
            lo = POOL_WIDTH + t * _LANES
            dz_ref[:, lo:lo + _LANES] = _rope_transposed(dq_ref[:, t * _LANES:(t + 1) * _LANES], tables).astype(BF16)
        kv = POOL_WIDTH + ATTN_WIDTH
        dz_ref[:, kv:kv + KV_WIDTH] = _rope_transposed(dk_ref[...], tables).astype(BF16)
        dz_ref[:, kv + KV_WIDTH:] = dv_ref[...].astype(BF16)
        dh = _dot(dz_ref[...], w_v[...])
        xv = x_ref[...]
        rx = _rstd(xv)
        xn = xv * rx
        h_ref[...] = (xn * pre_ref[...]).astype(BF16)
        dpre_ref[...] += jnp.sum(dh * xn, axis=0, keepdims=True)
        dx_ref[...] = dres_ref[...] + _norm_bwd(dh, xn, rx, pre_ref[...])

    args = [dres, x, pre, *w_in_t, mine, du, dq, dk, dv, rope]
    return pl.pallas_call(
        _ignoring(body, len(args), len(deps)), name="mixer_in_bwd", grid=(S // tm,),
        in_specs=[_rows(tm, D_MODEL), _rows(tm, D_MODEL), _full((1, D_MODEL)), _ANY, _ANY, _SMEM,
                  _rows(tm, POOL_WIDTH), _rows(tm, ATTN_WIDTH), _rows(tm, KV_WIDTH), _rows(tm, KV_WIDTH),
                  _table_spec(tm)] + [_ANY] * len(deps),
        out_specs=[_rows(tm, D_MODEL), _rows(tm, IN_WIDTH), _rows(tm, D_MODEL), _full((1, D_MODEL))],
        out_shape=[jax.ShapeDtypeStruct((S, D_MODEL), F32), jax.ShapeDtypeStruct((S, IN_WIDTH), BF16),
                   jax.ShapeDtypeStruct((S, D_MODEL), BF16), jax.ShapeDtypeStruct((1, D_MODEL), F32)],
        scratch_shapes=[pltpu.VMEM((IN_WIDTH, D_MODEL), BF16), pltpu.SemaphoreType.DMA((N_SHARD,))],
        compiler_params=_params(dimension_semantics=("arbitrary",)),
    )(*args, *deps)


def _pool_counts(tile_index, tm, width):
    t = tile_index * tm + lax.broadcasted_iota(jnp.int32, (tm, 1), 0)
    return jnp.minimum(t + 1, width).astype(F32)


def _pool_features(ext, u_tile, tile_index, tm):
    ds = []
    for gi, width in enumerate(POOL_WINDOWS):
        lanes = slice(gi * POOL_GROUP, (gi + 1) * POOL_GROUP)
        s = ext[:, lanes]
        shift = 1
        while shift < width:
            s = s + pltpu.roll(s, shift, axis=0)
            shift *= 2
        ds.append(s[HALO:, :] / _pool_counts(tile_index, tm, width) - u_tile[:, lanes])
    return ds


def _pool_fwd(u, w_pool, pool_scale, g_pool):
    S = u.shape[0]
    tm = MIXER_TILE

    def body(u_ref, w_ref, scale_ref, gain_ref, y_ref, ext_ref):
        i = pl.program_id(0)

        @pl.when(i == 0)
        def _():
            ext_ref[:HALO, :] = jnp.zeros((HALO, POOL_WIDTH), F32)

        u_tile = u_ref[...]
        ext_ref[HALO:, :] = u_tile
        ds = _pool_features(ext_ref[...], u_tile, i, tm)
        ext_ref[:HALO, :] = u_tile[tm - HALO:, :]
        ys = [_dot(ds[gi].astype(BF16), w_ref[gi].astype(BF16)) for gi in range(len(POOL_WINDOWS))]
        po = jnp.concatenate(ys, axis=1) * scale_ref[...]
        y_ref[...] = ((po * _rstd(po)) * gain_ref[...]).astype(BF16)

    return pl.pallas_call(
        body, name="pool_fwd", grid=(S // tm,),
        in_specs=[_rows(tm, POOL_WIDTH), _full((len(POOL_WINDOWS), POOL_GROUP, POOL_GROUP)),
                  _full((1, POOL_WIDTH)), _full((1, POOL_WIDTH))],
        out_specs=_rows(tm, POOL_WIDTH),
        out_shape=jax.ShapeDtypeStruct((S, POOL_WIDTH), BF16),
        scratch_shapes=[pltpu.VMEM((HALO + tm, POOL_WIDTH), F32)],
        compiler_params=_params(dimension_semantics=("arbitrary",)),
    )(u, w_pool, pool_scale, g_pool)


def _pool_bwd(dy, u, w_pool, pool_scale, g_pool):
    S = u.shape[0]
    tm = MIXER_TILE
    n_tiles = S // tm
    halo_blocks = tm // HALO

    def body(dy_ref, u_ref, uprev_ref, w_ref, scale_ref, gain_ref,
             du_ref, dw_ref, dscale_ref, dgain_ref, ext_ref, nxt_ref):
        i = pl.program_id(0)
        tile = n_tiles - 1 - i

        @pl.when(i == 0)
        def _():
            dw_ref[...] = jnp.zeros_like(dw_ref)
            dscale_ref[...] = jnp.zeros_like(dscale_ref)
            dgain_ref[...] = jnp.zeros_like(dgain_ref)
            nxt_ref[...] = jnp.zeros_like(nxt_ref)

        u_tile = u_ref[...]
        ext_ref[:HALO, :] = jnp.where(tile > 0, uprev_ref[...], 0.0)
        ext_ref[HALO:, :] = u_tile
        ds = _pool_features(ext_ref[...], u_tile, tile, tm)
        dsb = [d.astype(BF16) for d in ds]
        wb = [w_ref[gi].astype(BF16) for gi in range(len(POOL_WINDOWS))]
        yraw = jnp.concatenate([_dot(dsb[gi], wb[gi]) for gi in range(len(POOL_WINDOWS))], axis=1)
        po = yraw * scale_ref[...]
        r = _rstd(po)
        pn = po * r
        dyv = dy_ref[...]
        dgain_ref[...] += jnp.sum(dyv * pn, axis=0, keepdims=True)
        dpo = _norm_bwd(dyv, pn, r, gain_ref[...])
        dscale_ref[...] += jnp.sum(dpo * yraw, axis=0, keepdims=True)
        dyraw = (dpo * scale_ref[...]).astype(BF16)
        for gi, width in enumerate(POOL_WINDOWS):
            lanes = slice(gi * POOL_GROUP, (gi + 1) * POOL_GROUP)
            dw_ref[gi] += _dot_tn(dsb[gi], dyraw[:, lanes])
            dd = _dot_nt(dyraw[:, lanes], wb[gi])
            ddc = dd / _pool_counts(tile, tm, width)
            ext_ref[:tm, lanes] = ddc
            ext_ref[tm:, lanes] = nxt_ref[:, lanes]
            s = ext_ref[:, lanes]
            shift = 1
            while shift < width:
                s = s + pltpu.roll(s, HALO + tm - shift, axis=0)
                shift *= 2
            du_ref[:, lanes] = s[:tm, :] - dd
            nxt_ref[:, lanes] = ddc[:HALO, :]

    return pl.pallas_call(
        body, name="pool_bwd", grid=(n_tiles,),
        in_specs=[pl.BlockSpec((tm, POOL_WIDTH), lambda i: (n_tiles - 1 - i, 0)),
                  pl.BlockSpec((tm, POOL_WIDTH), lambda i: (n_tiles - 1 - i, 0)),
                  pl.BlockSpec((HALO, POOL_WIDTH), lambda i: (jnp.maximum((n_tiles - 1 - i) * halo_blocks - 1, 0), 0)),
                  _full((len(POOL_WINDOWS), POOL_GROUP, POOL_GROUP)), _full((1, POOL_WIDTH)), _full((1, POOL_WIDTH))],
        out_specs=[pl.BlockSpec((tm, POOL_WIDTH), lambda i: (n_tiles - 1 - i, 0)),
                   _full((len(POOL_WINDOWS), POOL_GROUP, POOL_GROUP)), _full((1, POOL_WIDTH)), _full((1, POOL_WIDTH))],
        out_shape=[jax.ShapeDtypeStruct((S, POOL_WIDTH), F32),
                   jax.ShapeDtypeStruct((len(POOL_WINDOWS), POOL_GROUP, POOL_GROUP), F32),
                   jax.ShapeDtypeStruct((1, POOL_WIDTH), F32), jax.ShapeDtypeStruct((1, POOL_WIDTH), F32)],
        scratch_shapes=[pltpu.VMEM((HALO + tm, POOL_WIDTH), F32), pltpu.VMEM((HALO, POOL_WIDTH), F32)],
        compiler_params=_params(dimension_semantics=("arbitrary",)),
    )(dy, u, u, w_pool, pool_scale, g_pool)


def _variant(head):
    return 2 * (head // (N_HEADS // N_KV_HEADS)) + head % 2


def _own_block(shape=(BLOCK, BLOCK)):
    r = lax.broadcasted_iota(jnp.int32, shape, 0)
    i = lax.broadcasted_iota(jnp.int32, shape, 1)
    return r <= i


def _fold_band(own, from_own, from_prev):
    return jnp.where(own, from_own, from_prev)


def _scores_by_head(own_ref, prev_ref, q_tiles):
    stacks = [jnp.concatenate(q_tiles[:2], axis=0), jnp.concatenate(q_tiles[2:], axis=0)]
    by_var = [_dot_nt(jnp.concatenate([own_ref[v], prev_ref[v]], axis=0), stacks[v // 2])
              for v in range(2 * N_KV_HEADS)]
    quadrant = lambda h, rows: by_var[_variant(h)][rows * BLOCK:(rows + 1) * BLOCK,
                                                   ((h // 2) % 2) * BLOCK:((h // 2) % 2 + 1) * BLOCK]
    return [quadrant(h, 0) for h in range(N_HEADS)], [quadrant(h, 1) for h in range(N_HEADS)]


def _softmax_t(s, sink):
    m = jnp.maximum(jnp.max(s, axis=0, keepdims=True), sink)
    p = jnp.exp(s - m)
    p_sink = jnp.exp(sink - m)
    inv = 1.0 / (jnp.sum(p, axis=0, keepdims=True) + p_sink)
    return p * inv, p_sink * inv


def _attn_fwd(q, kz, vz, sinks, g_attn, y_pool, x, w_out, mine, post):
    S = q.shape[0]
    nb = S // BLOCK

    def body(q_ref, kp_ref, kc_ref, vp_ref, vc_ref, sinks_ref, gain_ref, yp_ref, x_ref, w_land, w_own, mine_ref,
             post_ref, o_ref, out_ref, m_ref, y_ref, w_v, sem):
        _load_once(_gathered(w_land, w_own, w_v, mine_ref[0], rows=D_MODEL // N_SHARD), sem)
        b = pl.program_id(0)
        own = _own_block()
        no_prev = jnp.where(b > 0, 0.0, NEG_INF)
        q_pairs = [q_ref[:, i * _LANES:(i + 1) * _LANES] for i in range(N_HEADS // 2)]
        s_own, s_prev = _scores_by_head(kc_ref, kp_ref, q_pairs)
        p_own, p_prev = [], []
        zero = jnp.zeros((BLOCK, BLOCK), F32)
        for h in range(N_HEADS):
            p, _ = _softmax_t(_fold_band(own, s_own[h], s_prev[h] + no_prev), sinks_ref[0, h])
            p_own.append(jnp.where(own, p, zero).astype(BF16))
            p_prev.append(jnp.where(own, zero, p).astype(BF16))
        pairs = []
        for i in range(N_HEADS // 2):
            acc = None
            for h in (2 * i, 2 * i + 1):
                part = _dot_tn(p_own[h], vc_ref[_variant(h)]) + _dot_tn(p_prev[h], vp_ref[_variant(h)])
                acc = part if acc is None else acc + part
            pairs.append(acc)
        o = jnp.concatenate(pairs, axis=1)
        o_ref[...] = o
        y_ref[:, :POOL_WIDTH] = yp_ref[...]
        y_ref[:, POOL_WIDTH:] = ((o * _rstd(o)) * gain_ref[...]).astype(BF16)
        m = _dot(y_ref[...], w_v[...])
        m_ref[...] = m
        out_ref[...] = x_ref[...] + (m * _rstd(m)) * post_ref[...]

    variants = (2 * N_KV_HEADS, BLOCK, KV_WIDTH)
    prev = pl.BlockSpec(variants, lambda b: (0, jnp.maximum(b - 1, 0), 0))
    cur = pl.BlockSpec(variants, lambda b: (0, b, 0))
    return pl.pallas_call(
        body, name="attn_fwd", grid=(nb,),
        in_specs=[_rows(BLOCK, ATTN_WIDTH), prev, cur, prev, cur,
                  pl.BlockSpec(memory_space=pltpu.SMEM), _full((1, ATTN_WIDTH)),
                  _rows(BLOCK, POOL_WIDTH), _rows(BLOCK, D_MODEL), _ANY, _ANY, _SMEM, _full((1, D_MODEL))],
        out_specs=[_rows(BLOCK, ATTN_WIDTH), _rows(BLOCK, D_MODEL), _rows(BLOCK, D_MODEL), _rows(BLOCK, D_MODEL)],
        out_shape=[jax.ShapeDtypeStruct((S, ATTN_WIDTH), F32), jax.ShapeDtypeStruct((S, D_MODEL), F32),
                   jax.ShapeDtypeStruct((S, D_MODEL), F32), jax.ShapeDtypeStruct((S, D_MODEL), BF16)],
        scratch_shapes=[pltpu.VMEM((D_MODEL, D_MODEL), BF16), pltpu.SemaphoreType.DMA((N_SHARD,))],
        compiler_params=_params(dimension_semantics=("arbitrary",)),
    )(q, kz, kz, vz, vz, sinks, g_attn, y_pool, x, *w_out, mine, post)


def _attn_bwd(dy, o, q, kz, vz, sinks, g_attn):
    S = q.shape[0]
    nb = S // BLOCK

    def body(dy_ref, o_ref, q_ref, kp_ref, kc_ref, vp_ref, vc_ref, sinks_ref, gain_ref,
             dq_ref, dk_ref, dv_ref, dsink_ref, dgain_ref, kcarry_ref, vcarry_ref):
        b = pl.program_id(0)

        @pl.when(b == 0)
        def _():
            dsink_ref[...] = jnp.zeros_like(dsink_ref)
            dgain_ref[...] = jnp.zeros_like(dgain_ref)
            kcarry_ref[...] = jnp.zeros_like(kcarry_ref)
            vcarry_ref[...] = jnp.zeros_like(vcarry_ref)

        @pl.when(b < nb)
        def _():
            ov = o_ref[...]
            r = _rstd(ov)
            on = ov * r
            dyv = dy_ref[...]
            dgain_ref[...] += jnp.sum(dyv * on, axis=0, keepdims=True)
            do = _norm_bwd(dyv, on, r, gain_ref[...])
            own = _own_block()
            no_prev = jnp.where(b > 0, 0.0, NEG_INF)
            q_pairs = [q_ref[:, i * _LANES:(i + 1) * _LANES] for i in range(N_HEADS // 2)]
            do_pairs = [do[:, i * _LANES:(i + 1) * _LANES].astype(BF16) for i in range(N_HEADS // 2)]
            heads = range(N_HEADS)
            s_own, s_prev = _scores_by_head(kc_ref, kp_ref, q_pairs)
            dp_own, dp_prev = _scores_by_head(vc_ref, vp_ref, do_pairs)
            zero = jnp.zeros((BLOCK, BLOCK), F32)
            split = lambda t: (jnp.where(own, t, zero).astype(BF16), jnp.where(own, zero, t).astype(BF16))
            ds_parts, p_parts, sink_rows = [], [], []
            for h in heads:
                p, p_sink = _softmax_t(_fold_band(own, s_own[h], s_prev[h] + no_prev), sinks_ref[0, h])
                dp = _fold_band(own, dp_own[h], dp_prev[h])
                delta = jnp.sum(p * dp, axis=0, keepdims=True)
                ds_parts.append(split(p * (dp - delta)))
                p_parts.append(split(p))
                sink_rows.append(jnp.zeros((1, _LANES), F32) - jnp.sum(p_sink * delta))
            dsink_ref[...] += jnp.concatenate(sink_rows, axis=0)
            n_var = 2 * N_KV_HEADS
            dk_own, dk_prev, dv_own, dv_prev = ([None] * n_var for _ in range(4))
            add = lambda acc, var, t: acc.__setitem__(var, t if acc[var] is None else acc[var] + t)
            for i in range(N_HEADS // 2):
                dq_pair = None
                for h in (2 * i, 2 * i + 1):
                    var = _variant(h)
                    (ds_o, ds_p), (p_o, p_p) = ds_parts[h], p_parts[h]
                    part = _dot_tn(ds_o, kc_ref[var]) + _dot_tn(ds_p, kp_ref[var])
                    dq_pair = part if dq_pair is None else dq_pair + part
                    add(dk_own, var, _dot(ds_o, q_pairs[i]))
                    add(dk_prev, var, _dot(ds_p, q_pairs[i]))
                    add(dv_own, var, _dot(p_o, do_pairs[i]))
                    add(dv_prev, var, _dot(p_p, do_pairs[i]))
                dq_ref[:, i * _LANES:(i + 1) * _LANES] = dq_pair * ATTN_SCALE
            low = lax.broadcasted_iota(jnp.int32, (BLOCK, _LANES), 1) < HEAD_DIM

            def merge(acc):
                return jnp.where(low, acc[0] + pltpu.roll(acc[1], HEAD_DIM, axis=1),
                                 acc[3] + pltpu.roll(acc[2], HEAD_DIM, axis=1))
            dk_ref[...] = kcarry_ref[...] + merge(dk_prev)
            dv_ref[...] = vcarry_ref[...] + merge(dv_prev)
            kcarry_ref[...] = merge(dk_own)
            vcarry_ref[...] = merge(dv_own)

        @pl.when(b == nb)
        def _():
            dk_ref[...] = kcarry_ref[...]
            dv_ref[...] = vcarry_ref[...]

    cur = lambda b: (jnp.minimum(b, nb - 1), 0)
    prev = lambda b: (jnp.clip(b - 1, 0, nb - 1), 0)
    variants = (2 * N_KV_HEADS, BLOCK, KV_WIDTH)
    var_prev = pl.BlockSpec(variants, lambda b: (0, jnp.clip(b - 1, 0, nb - 1), 0))
    var_cur = pl.BlockSpec(variants, lambda b: (0, jnp.minimum(b, nb - 1), 0))
    return pl.pallas_call(
        body, name="attn_bwd", grid=(nb + 1,),
        in_specs=[pl.BlockSpec((BLOCK, ATTN_WIDTH), lambda b: (jnp.minimum(b, nb - 1), 1)),
                  pl.BlockSpec((BLOCK, ATTN_WIDTH), cur), pl.BlockSpec((BLOCK, ATTN_WIDTH), cur),
                  var_prev, var_cur, var_prev, var_cur,
                  pl.BlockSpec(memory_space=pltpu.SMEM), _full((1, ATTN_WIDTH))],
        out_specs=[pl.BlockSpec((BLOCK, ATTN_WIDTH), cur),
                   pl.BlockSpec((BLOCK, KV_WIDTH), prev), pl.BlockSpec((BLOCK, KV_WIDTH), prev),
                   _full((N_HEADS, _LANES)), _full((1, ATTN_WIDTH))],
        out_shape=[jax.ShapeDtypeStruct((S, ATTN_WIDTH), F32), jax.ShapeDtypeStruct((S, KV_WIDTH), F32),
                   jax.ShapeDtypeStruct((S, KV_WIDTH), F32), jax.ShapeDtypeStruct((N_HEADS, _LANES), F32),
                   jax.ShapeDtypeStruct((1, ATTN_WIDTH), F32)],
        scratch_shapes=[pltpu.VMEM((BLOCK, KV_WIDTH), F32), pltpu.VMEM((BLOCK, KV_WIDTH), F32)],
        compiler_params=_params(dimension_semantics=("arbitrary",)),
    )(dy, o, q, kz, kz, vz, vz, sinks, g_attn)


def _mixer_out_bwd(dout, m, w_out, mine, post, deps=()):
    S = m.shape[0]
    tm = MIXER_TILE

    def body(dout_ref, m_ref, w_land, w_own, mine_ref, post_ref, dy_ref, dm_ref, dpost_ref, w_v, sem):
        _load_once(_gathered(w_land, w_own, w_v, mine_ref[0], rows=D_MODEL // N_SHARD), sem)

        @pl.when(pl.program_id(0) == 0)
        def _():
            dpost_ref[...] = jnp.zeros_like(dpost_ref)

        mv = m_ref[...]
        r = _rstd(mv)
        mn = mv * r
        dv = dout_ref[...]
        dpost_ref[...] += jnp.sum(dv * mn, axis=0, keepdims=True)
        dm = _norm_bwd(dv, mn, r, post_ref[...]).astype(BF16)
        dm_ref[...] = dm
        dy_ref[...] = _dot_nt(dm, w_v[...])

    args = [dout, m, *w_out, mine, post]
    return pl.pallas_call(
        _ignoring(body, len(args), len(deps)), name="mixer_out_bwd", grid=(S // tm,),
        in_specs=[_rows(tm, D_MODEL), _rows(tm, D_MODEL), _ANY, _ANY, _SMEM, _full((1, D_MODEL))] + [_ANY] * len(deps),
        out_specs=[_rows(tm, D_MODEL), _rows(tm, D_MODEL), _full((1, D_MODEL))],
        out_shape=[jax.ShapeDtypeStruct((S, D_MODEL), F32), jax.ShapeDtypeStruct((S, D_MODEL), BF16),
                   jax.ShapeDtypeStruct((1, D_MODEL), F32)],
        scratch_shapes=[pltpu.VMEM((D_MODEL, D_MODEL), BF16), pltpu.SemaphoreType.DMA((N_SHARD,))],
        compiler_params=_params(dimension_semantics=("arbitrary",)),
    )(*args, *deps)


def _inv_freq_row():
    inv_freq = ROPE_THETA ** (-jnp.arange(0, ROT_DIM, 2, dtype=F32) / ROT_DIM)
    per_head = jnp.concatenate([inv_freq, inv_freq, jnp.zeros((HEAD_DIM - ROT_DIM,), F32)])
    return jnp.tile(per_head, _LANES // HEAD_DIM).reshape(1, _LANES)


def _local_step(x, pos, target, small, mine, weights_of, grads_ready):
    rope = _rope_tables(pos, _inv_freq_row())
    wgu1, wd1 = weights_of("ffn1", (rope,))
    x1, g1, u1, f1 = _ffn_fwd(x, small["ffn1_pre"], small["ffn1_post"], wgu1, wd1, mine)
    w_in_t, w_out = weights_of("mixer", (x1,))
    u, q, k, v = _mixer_in_fwd(x1, small["mix_pre"], w_in_t, mine, rope)
    y_pool = _pool_fwd(u, small["w_pool"], small["pool_scale"], small["g_pool"])
    o, x2, m, y = _attn_fwd(q, k, v, small["sinks"], small["g_attn"], y_pool, x1, w_out, mine, small["mix_post"])
    wgu2, wd2 = weights_of("ffn2", (x2,))
    dx3, g2, u2, f2, loss_acc = _ffn_fwd(x2, small["ffn2_pre"], small["ffn2_post"], wgu2, wd2, mine, target=target)
    grads = {"loss": loss_acc * (0.5 / D_MODEL)}
    dx2, h3, dgu2, a2, df2, grads["ffn2_pre"], grads["ffn2_post"] = _ffn_bwd(
        dx3, x2, f2, g2, u2, small["ffn2_pre"], small["ffn2_post"], wgu2, wd2, mine)
    dwgu2 = _wgrad(h3, dgu2, D_MODEL, FF_CHUNK, "wgrad_gu2", column_shards=True)
    dwd2 = _wgrad(a2, df2, FF_CHUNK, D_MODEL, "wgrad_down2")
    deps = grads_ready("ffn2", {"ffn2_w_gu": dwgu2, "ffn2_w_down": dwd2})
    dy, dm, grads["mix_post"] = _mixer_out_bwd(dx2, m, w_out, mine, small["mix_post"], deps=deps)
    dw_out = _wgrad(y, dm, D_MODEL, D_MODEL, "wgrad_out")
    dq, dk, dv, dsinks, grads["g_attn"] = _attn_bwd(dy, o, q, k, v, small["sinks"], small["g_attn"])
    grads["sinks"] = dsinks[:, 0].reshape(1, N_HEADS)
    du, grads["w_pool"], grads["pool_scale"], grads["g_pool"] = _pool_bwd(
        dy, u, small["w_pool"], small["pool_scale"], small["g_pool"])
    dx1, dz, h2, grads["mix_pre"] = _mixer_in_bwd(dx2, x1, small["mix_pre"], w_in_t, mine, du, dq, dk, dv, rope)
    dw_in_t = _wgrad(dz, h2, IN_WIDTH, D_MODEL, "wgrad_in")
    deps = grads_ready("mixer", {"w_in": dw_in_t, "w_out": dw_out})
    dx, h1, dgu1, a1, df1, grads["ffn1_pre"], grads["ffn1_post"] = _ffn_bwd(
        dx1, x, f1, g1, u1, small["ffn1_pre"], small["ffn1_post"], wgu1, wd1, mine, deps=deps)
    dwgu1 = _wgrad(h1, dgu1, D_MODEL, FF_CHUNK, "wgrad_gu1", column_shards=True)
    deps = grads_ready("ffn1_gu", {"ffn1_w_gu": dwgu1}, small=grads)
    dwd1 = _wgrad(a1, df1, FF_CHUNK, D_MODEL, "wgrad_down1", deps=deps)
    grads_ready("ffn1_down", {"ffn1_w_down": dwd1})
    return dx


def _place():
    return lax.axis_index("x"), lax.axis_index("y"), lax.axis_index("c")


def _other_chips(x, y):
    return [(1 - x, y), (x, 1 - y), (1 - x, 1 - y)]


def _hbm_shape(shape, dtype):
    return jax.ShapeDtypeStruct(shape, dtype)


_HBM = pl.BlockSpec(memory_space=pltpu.HBM)
_SEM = pl.BlockSpec(memory_space=pltpu.SEMAPHORE)
_EFFECT = pltpu.SideEffectType.DATAFLOW_SIDE_EFFECTING
GATHER, GATHER_HALF, REDUCE, BROADCAST = "gather", "gather_half", "reduce", "broadcast"
N_DEVICES = 8


def _in_hbm(a):
    return pltpu.with_memory_space_constraint(a, pltpu.HBM)


def _core_half(rows, c):
    return pl.ds(pl.multiple_of(c * (rows // 2), 16), rows // 2)


def _chip_copies(kind, srcs, lands, send_sems, recv_sems):
    x, y, c = _place()
    mine = 2 * x + y
    copies = []
    for w in range(len(srcs)):
        if kind == BROADCAST:
            peers = [(x ^ (k >> 2), y ^ ((k >> 1) & 1), c ^ (k & 1)) for k in range(1, N_DEVICES)]
        else:
            peers = [(px, py, c) for px, py in _other_chips(x, y)]
        for k, (px, py, pc) in enumerate(peers):
            if kind == GATHER:
                src, dst = srcs[w], lands[w].at[mine]
            elif kind == GATHER_HALF:
                half = _core_half(srcs[w].shape[0], c)
                src, dst = srcs[w].at[half, :], lands[w].at[mine, half, :]
            elif kind == BROADCAST:
                src, dst = srcs[w], lands[w].at[2 * mine + c]
            else:
                src, dst = srcs[w].at[2 * px + py], lands[w].at[k]
            pair = len(peers) * w + k
            copies.append(pltpu.make_async_remote_copy(
                src_ref=src, dst_ref=dst, send_sem=send_sems.at[pair], recv_sem=recv_sems.at[pair],
                device_id=(px, py, pc), device_id_type=MESH))
    return copies


def _landing_shape(kind, src):
    if kind == REDUCE:
        return (N_SHARD - 1,) + src.shape[1:]
    return ((N_DEVICES if kind == BROADCAST else N_SHARD),) + src.shape


def _peer_count(kind):
    return N_DEVICES - 1 if kind == BROADCAST else N_SHARD - 1


def _exchange_start(kinds, groups, name):
    sizes = [len(g) for g in groups]
    flat = [s for g in groups for s in g]
    n, ng = len(flat), len(groups)

    def body(*refs):
        srcs, lands = refs[:n], refs[n:2 * n]
        sems = refs[2 * n:2 * n + 2 * ng]
        token = refs[-1]
        start = 0
        for gi, size in enumerate(sizes):
            for cp in _chip_copies(kinds[gi], srcs[start:start + size], lands[start:start + size],
                                   sems[2 * gi], sems[2 * gi + 1]):
                cp.start()
            start += size
        token[...] = jnp.zeros_like(token)

    landings = [lax.empty(_landing_shape(kind, s), s.dtype) for kind, g in zip(kinds, groups) for s in g]
    sem_shapes = [pltpu.SemaphoreType.DMA((size * _peer_count(kind),)) for kind, size in zip(kinds, sizes)
                  for _ in range(2)]
    outs = pl.pallas_call(
        body, name=name,
        in_specs=[_HBM] * (2 * n),
        out_specs=[_SEM] * (2 * ng) + [_HBM] * (2 * n) + [pl.BlockSpec(memory_space=pltpu.VMEM)],
        out_shape=sem_shapes + [pltpu.HBM(a.shape, a.dtype) for a in flat + landings]
        + [jax.ShapeDtypeStruct((8, _LANES), F32)],
        input_output_aliases={i: 2 * ng + i for i in range(2 * n)},
        compiler_params=pltpu.CompilerParams(has_side_effects=_EFFECT),
    )(*[_in_hbm(a) for a in flat + landings])
    sems, srcs, lands, token = outs[:2 * ng], outs[2 * ng:2 * ng + n], outs[2 * ng + n:2 * ng + 2 * n], outs[-1]
    handles, start = [], 0
    for gi, size in enumerate(sizes):
        handles.append((sems[2 * gi], sems[2 * gi + 1], srcs[start:start + size], lands[start:start + size]))
        start += size
    return handles, token


def _exchange_wait(kind, handle, after, name):
    send_sems, recv_sems, srcs, lands = handle
    n = len(srcs)

    def body(*refs):
        copies = _chip_copies(kind, refs[:n], refs[n:2 * n], refs[2 * n], refs[2 * n + 1])
        for cp in copies:
            cp.wait_send()
        for cp in copies:
            cp.wait_recv()

    outs = pl.pallas_call(
        body, name=name,
        in_specs=[_HBM] * (2 * n) + [_SEM, _SEM] + [_ANY] * len(after),
        out_specs=[_HBM] * (2 * n),
        out_shape=[pltpu.HBM(a.shape, a.dtype) for a in list(srcs) + list(lands)],
        input_output_aliases={i: i for i in range(2 * n)},
        compiler_params=pltpu.CompilerParams(has_side_effects=_EFFECT),
    )(*srcs, *lands, send_sems, recv_sems, *after)
    return outs[:n], outs[n:]


def _swap_gathered_halves(lands, name):
    n = len(lands)

    def body(*refs):
        bufs = refs[n:2 * n]
        send_sems, recv_sems = refs[2 * n:]
        x, y, c = _place()
        mine = 2 * x + y
        sends, arrivals = [], []
        for w in range(n):
            rows = bufs[w].shape[1]
            for d in range(1, N_SHARD):
                slot = (mine + d) % N_SHARD
                sems = dict(send_sem=send_sems.at[(N_SHARD - 1) * w + d - 1],
                            recv_sem=recv_sems.at[(N_SHARD - 1) * w + d - 1],
                            device_id=(x, y, 1 - c), device_id_type=MESH)
                fetched = bufs[w].at[slot, _core_half(rows, c), :]
                missing = bufs[w].at[slot, _core_half(rows, 1 - c), :]
                sends.append(pltpu.make_async_remote_copy(src_ref=fetched, dst_ref=fetched, **sems))
                arrivals.append(pltpu.make_async_remote_copy(src_ref=missing, dst_ref=missing, **sems))
        for cp in sends:
            cp.start()
        for cp in arrivals:
            cp.wait_recv()
        for cp in sends:
            cp.wait_send()

    return pl.pallas_call(
        body, name=name, in_specs=[_ANY] * n, out_specs=[_ANY] * n,
        out_shape=[_hbm_shape(a.shape, a.dtype) for a in lands],
        input_output_aliases={i: i for i in range(n)},
        scratch_shapes=[pltpu.SemaphoreType.DMA((n * (N_SHARD - 1),)), pltpu.SemaphoreType.DMA((n * (N_SHARD - 1),))],
        compiler_params=pltpu.CompilerParams(has_side_effects=True),
    )(*lands)


def _swap_with_sibling(partials, name):
    n = len(partials)

    def body(*refs):
        ins, outs = refs[:n], refs[n:2 * n]
        send_sems, recv_sems = refs[2 * n:]
        x, y, c = _place()
        sends = [pltpu.make_async_remote_copy(
            src_ref=ins[w], dst_ref=outs[w], send_sem=send_sems.at[w], recv_sem=recv_sems.at[w],
            device_id=(x, y, 1 - c), device_id_type=MESH) for w in range(n)]
        for cp in sends:
            cp.start()
        for cp in sends:
            cp.wait_recv()
        for cp in sends:
            cp.wait_send()

    return pl.pallas_call(
        body, name=name,
        in_specs=[_ANY] * n, out_specs=[_ANY] * n,
        out_shape=[_hbm_shape(p.shape, p.dtype) for p in partials],
        scratch_shapes=[pltpu.SemaphoreType.DMA((n,)), pltpu.SemaphoreType.DMA((n,))],
        compiler_params=pltpu.CompilerParams(has_side_effects=True),
    )(*partials)


def _row_block(rows, cap):
    best = None
    for cand in range(16, min(rows, cap) + 1, 16):
        if rows % cand == 0:
            best = cand
    assert best is not None, rows
    return best


def _chip_partial(own, received, shard, name):
    _, R, C = own.shape
    rb = _row_block(R, 512)

    def body(shard_ref, own_ref, rec_ref, out_ref):
        acc = own_ref[...]
        for k in range(3):
            acc = acc + rec_ref[k].astype(F32)
        out_ref[...] = acc.astype(BF16)

    return pl.pallas_call(
        body, name=name,
        grid_spec=pltpu.PrefetchScalarGridSpec(
            num_scalar_prefetch=1, grid=(R // rb,),
            in_specs=[pl.BlockSpec((None, rb, C), lambda i, s: (s[0], i, 0)),
                      pl.BlockSpec((3, rb, C), lambda i, s: (0, i, 0))],
            out_specs=pl.BlockSpec((rb, C), lambda i, s: (i, 0))),
        out_shape=jax.ShapeDtypeStruct((R, C), BF16),
        compiler_params=_params(dimension_semantics=("arbitrary",)),
    )(shard, own, received)


def _adamw(w, m, v, g_parts, name, slot=None):
    R, C = w.shape
    by_device = slot is not None
    rb = _row_block(R, 256) if R % 16 == 0 else R

    def body(w_ref, m_ref, v_ref, *refs):
        g_refs, (grad_ref, delta_ref, m_out, v_out) = refs[:-4], refs[-4:]
        if by_device:
            own_ref, land_ref, slot_ref = g_refs
            part = lambda d: jnp.where(slot_ref[0] == d, own_ref[...], land_ref[d])
            g = part(0)
            for d in range(1, N_DEVICES):
                g = g + part(d)
        else:
            g = g_refs[0][...].astype(F32)
            for g_ref in g_refs[1:]:
                g = g + g_ref[...].astype(F32)
        grad_ref[...] = g
        new_m = ADAM_B1 * m_ref[...] + (1.0 - ADAM_B1) * g
        new_v = ADAM_B2 * v_ref[...] + (1.0 - ADAM_B2) * (g * g)
        m_hat = new_m / (1.0 - ADAM_B1 ** ADAM_STEP)
        v_hat = new_v / (1.0 - ADAM_B2 ** ADAM_STEP)
        delta_ref[...] = -ADAM_LR * (m_hat / (jnp.sqrt(v_hat) + ADAM_EPS) + ADAM_WD * w_ref[...])
        m_out[...] = new_m
        v_out[...] = new_v

    spec = pl.BlockSpec((rb, C), lambda i: (i, 0))
    if by_device:
        g_specs = [spec, pl.BlockSpec((N_DEVICES, rb, C), lambda i: (0, i, 0)), _SMEM]
        g_parts = list(g_parts) + [slot]
    else:
        g_specs = [spec] * len(g_parts)
    return pl.pallas_call(
        body, name=name, grid=(R // rb,),
        in_specs=[spec, spec, spec] + g_specs,
        out_specs=[spec] * 4,
        out_shape=[jax.ShapeDtypeStruct((R, C), F32)] * 4,
        compiler_params=_params(dimension_semantics=("arbitrary",)),
    )(w, m, v, *g_parts)


SMALL_NAMES = ("ffn1_pre", "ffn1_post", "mix_pre", "pool_scale", "sinks", "g_pool", "g_attn", "mix_post",
               "ffn2_pre", "ffn2_post", "w_pool")
_SLAB_PART = 8 * _LANES


SLAB_NAMES = SMALL_NAMES + ("loss",)


def _to_slab(parts):
    rows = []
    for name in SLAB_NAMES:
        flat = parts[name].reshape(-1) if name in parts else jnp.zeros((_SLAB_PART,), F32)
        padded = -(-flat.shape[0] // _SLAB_PART) * _SLAB_PART
        rows.append(jnp.pad(flat, (0, padded - flat.shape[0])).reshape(-1, _LANES))
    return jnp.concatenate(rows, axis=0)


def _from_slab(slab, like):
    out, row = {}, 0
    for name in SLAB_NAMES:
        size = like[name].size
        rows = -(-size // _SLAB_PART) * (_SLAB_PART // _LANES)
        out[name] = slab[row:row + rows].reshape(-1)[:size].reshape(like[name].shape)
        row += rows
    return out


BIG_NAMES = ("ffn1_w_gu", "ffn1_w_down", "w_in", "w_out", "ffn2_w_gu", "ffn2_w_down")
WEIGHT_ORDER = ("ffn1_pre", "ffn1_w_gu", "ffn1_w_down", "ffn1_post", "mix_pre", "w_in", "w_pool", "pool_scale",
                "sinks", "g_pool", "g_attn", "w_out", "mix_post", "ffn2_pre", "ffn2_w_gu", "ffn2_w_down", "ffn2_post")


def kernel(x, positions, ffn1_pre, ffn1_w_gu, ffn1_w_down, ffn1_post, mix_pre, w_in, w_pool, pool_scale, sinks, g_pool, g_attn, w_out, mix_post, ffn2_pre, ffn2_w_gu, ffn2_w_down, ffn2_post, loss_target, m_ffn1_pre, m_ffn1_w_gu, m_ffn1_w_down, m_ffn1_post, m_mix_pre, m_w_in, m_w_pool, m_pool_scale, m_sinks, m_g_pool, m_g_attn, m_w_out, m_mix_post, m_ffn2_pre, m_ffn2_w_gu, m_ffn2_w_down, m_ffn2_post, v_ffn1_pre, v_ffn1_w_gu, v_ffn1_w_down, v_ffn1_post, v_mix_pre, v_w_in, v_w_pool, v_pool_scale, v_sinks, v_g_pool, v_g_attn, v_w_out, v_mix_post, v_ffn2_pre, v_ffn2_w_gu, v_ffn2_w_down, v_ffn2_post):
    given = dict(locals())
    weights = {n: given[n][0] for n in WEIGHT_ORDER}
    moments_m = {n: given["m_" + n][0] for n in WEIGHT_ORDER}
    moments_v = {n: given["v_" + n][0] for n in WEIGHT_ORDER}
    S = x.shape[1]
    shard = (2 * lax.axis_index("x") + lax.axis_index("y")).astype(jnp.int32).reshape(1)

    local16 = {n: weights[n].astype(BF16) for n in BIG_NAMES if n != "w_in"}
    local16["w_in"] = weights["w_in"].T.astype(BF16)
    gather_groups = {"ffn1": ("ffn1_w_gu", "ffn1_w_down"), "mixer": ("w_in", "w_out"),
                     "ffn2": ("ffn2_w_gu", "ffn2_w_down")}
    gather_kinds = {"ffn1": GATHER_HALF, "mixer": GATHER, "ffn2": GATHER}
    handles, _ = _exchange_start(list(gather_kinds.values()),
                                 [[local16[n] for n in names] for names in gather_groups.values()], "gather_start")
    gather_handles = dict(zip(gather_groups, handles))

    def weights_of(group, after):
        kind = gather_kinds[group]
        owns, lands = _exchange_wait(kind, gather_handles[group], list(after), "gather_wait_" + group)
        if kind == GATHER_HALF:
            lands = _swap_gathered_halves(lands, "swap_gathered_" + group)
        return list(zip(lands, owns))

    pending, last_token = {}, []

    def grads_ready(group, grads, small=None):
        names = list(grads)
        kinds, sources = [REDUCE], [[grads[n][1] for n in names]]
        if small is not None:
            kinds, sources = kinds + [BROADCAST], sources + [[_to_slab(small)]]
        handles, token = _exchange_start(kinds, sources, "reduce_start_" + group)
        handle = handles[0]
        if small is not None:
            pending["small"] = handles[1]
        pending[group] = (names, handle, [grads[n][0] for n in names])
        last_token[:] = [token]
        return [token]

    small = {n: (weights[n] if weights[n].ndim > 1 else weights[n].reshape(1, -1)) for n in SMALL_NAMES}
    dx = _local_step(x[0], positions.reshape(S, 1), loss_target[0], small, shard, weights_of, grads_ready)

    grad, delta, new_m, new_v = {}, {}, {}, {}

    def finish(groups, after):
        names, partials = [], []
        for group in groups:
            group_names, handle, own32 = pending[group]
            _, received = _exchange_wait(REDUCE, handle, after, "reduce_wait_" + group)
            names += group_names
            partials += [_chip_partial(g32, rec, shard, "chip_partial_" + n)
                         for n, g32, rec in zip(group_names, own32, received)]
        siblings = _swap_with_sibling(partials, "swap_" + groups[0])
        for name, mine, theirs in zip(names, partials, siblings):
            if name == "w_in":
                mine, theirs = mine.T, theirs.T
            grad[name], delta[name], new_m[name], new_v[name] = _adamw(
                weights[name], moments_m[name], moments_v[name], [mine, theirs], "adamw_" + name)
        return [grad[names[-1]]]

    after = finish(["ffn2"], last_token)
    after = finish(["mixer"], after)
    (own_slab,), (slab_landing,) = _exchange_wait(BROADCAST, pending["small"], after, "reduce_wait_small")
    device = (2 * shard + lax.axis_index("c")).astype(jnp.int32)
    small_like = dict({n: small[n] for n in SMALL_NAMES}, loss=jnp.zeros((8, _LANES), F32))
    slabs = _adamw(_to_slab(small), _to_slab({n: moments_m[n] for n in SMALL_NAMES}),
                   _to_slab({n: moments_v[n] for n in SMALL_NAMES}), [own_slab, slab_landing], "adamw_small",
                   slot=device)
    for store, slab in zip((grad, delta, new_m, new_v), slabs):
        store.update(_from_slab(slab, small_like))
    loss = grad["loss"][0, 0]
    after = finish(["ffn1_gu"], [slabs[0]])
    finish(["ffn1_down"], after)

    def out(store):
        return [store[n].reshape(given[n].shape) for n in WEIGHT_ORDER]
    return (loss, dx[None], *out(grad), *out(delta), *out(new_m), *out(new_v))
```

```python
import functools

import jax
import jax.numpy as jnp
from jax import lax
from jax.experimental import pallas as pl
from jax.experimental.pallas import tpu as pltpu

F32 = jnp.float32
BF16 = jnp.bfloat16

D_MODEL = 1024
D_FF = 2816
N_SHARD = 4
FF_CHUNK = D_FF // 2
POOL_WINDOWS = (2, 4, 8, 16)
POOL_WIDTH = 512
POOL_GROUP = 128
HALO = 16
HEAD_DIM = 64
N_HEADS = 8
N_KV_HEADS = 2
ATTN_WIDTH = 512
KV_WIDTH = 128
IN_WIDTH = 1280
BLOCK = 128
ATTN_BLOCKS = 4
ROT_DIM = 16
ROPE_THETA = 500000.0
EPS = 1e-6
NEG_INF = -1e30
ATTN_SCALE = HEAD_DIM ** -0.5

ADAM_LR = 0.001
ADAM_B1 = 0.9
ADAM_B2 = 0.999
ADAM_EPS = 1e-08
ADAM_WD = 0.01
ADAM_STEP = 10

VMEM_LIMIT = 60 * 1024 * 1024
FFN_FWD_TILE = 512
FFN_BWD_TILE = 256
MIXER_TILE = 512

MESH = pl.DeviceIdType.MESH


def _params(**kw):
    return pltpu.CompilerParams(vmem_limit_bytes=VMEM_LIMIT, **kw)


def _dot(a, b):
    return jnp.dot(a, b, preferred_element_type=F32)


def _dot_nt(a, b):
    return lax.dot_general(a, b, (((1,), (1,)), ((), ())), preferred_element_type=F32)


def _dot_tn(a, b):
    return lax.dot_general(a, b, (((0,), (0,)), ((), ())), preferred_element_type=F32)


def _rstd(x):
    return lax.rsqrt(jnp.mean(x * x, axis=-1, keepdims=True) + EPS)


def _norm_bwd(dy, xn, r, gain):
    dxn = dy * gain
    return r * (dxn - xn * jnp.mean(dxn * xn, axis=-1, keepdims=True))


def _sigmoid(x):
    return 1.0 / (1.0 + jnp.exp(-x))


def _full(shape):
    return pl.BlockSpec(shape, lambda *_: (0,) * len(shape))


def _rows(tile, width, col=0):
    return pl.BlockSpec((tile, width), lambda i: (i, col))


_ANY = pl.BlockSpec(memory_space=pl.ANY)


_SMEM = pl.BlockSpec(memory_space=pltpu.SMEM)


def _load_once(pairs, sem):
    @pl.when(pl.program_id(0) == 0)
    def _():
        copies = [pltpu.make_async_copy(src, dst, sem.at[n]) for n, (src, dst) in enumerate(pairs)]
        for cp in copies:
            cp.start()
        for cp in copies:
            cp.wait()


def _gathered(land_ref, own_ref, vmem_ref, mine, rows=None):
    def dst(slot):
        if rows is None:
            return vmem_ref.at[slot]
        return vmem_ref.at[pl.ds(pl.multiple_of(slot * rows, 16), rows), :]
    pairs = [(land_ref.at[(mine + d) % N_SHARD], dst((mine + d) % N_SHARD)) for d in range(1, N_SHARD)]
    return pairs + [(own_ref, dst(mine))]


def _ignoring(body, start, count):
    def wrapped(*refs):
        return body(*refs[:start], *refs[start + count:])
    return wrapped


def _ffn_fwd(x, pre, post, wgu, wd, mine, target=None, deps=()):
    S = x.shape[0]
    tm = FFN_FWD_TILE
    with_loss = target is not None

    def body(*refs):
        if with_loss:
            (x_ref, pre_ref, post_ref, wgu_land, wgu_own, wd_land, wd_own, mine_ref, tgt_ref,
             out_ref, g_ref, u_ref, f_ref, loss_ref, wgu_v, wd_v, sem) = refs
        else:
            (x_ref, pre_ref, post_ref, wgu_land, wgu_own, wd_land, wd_own, mine_ref,
             out_ref, g_ref, u_ref, f_ref, wgu_v, wd_v, sem) = refs
        _load_once(_gathered(wgu_land, wgu_own, wgu_v, mine_ref[0])
                   + _gathered(wd_land, wd_own, wd_v, mine_ref[0], rows=D_FF // N_SHARD), sem)
        xv = x_ref[...]
        h = ((xv * _rstd(xv)) * pre_ref[...]).astype(BF16)
        facc = jnp.zeros((tm, D_MODEL), F32)
        for c in range(2):
            cols = slice(c * FF_CHUNK, (c + 1) * FF_CHUNK)
            g = _dot(h, wgu_v[c])
            u = _dot(h, wgu_v[2 + c])
            g_ref[:, cols] = g.astype(BF16)
            u_ref[:, cols] = u.astype(BF16)
            a = (g * _sigmoid(g)) * u
            facc = facc + _dot(a.astype(BF16), wd_v[cols, :])
        f_ref[...] = facc
        out = xv + 0.5 * ((facc * _rstd(facc)) * post_ref[...])
        if with_loss:
            diff = out - tgt_ref[...]
            out_ref[...] = diff * (1.0 / D_MODEL)

            @pl.when(pl.program_id(0) == 0)
            def _():
                loss_ref[...] = jnp.zeros_like(loss_ref)
            loss_ref[...] += jnp.sum(diff * diff)
        else:
            out_ref[...] = out

    in_specs = [_rows(tm, D_MODEL), _full((1, D_MODEL)), _full((1, D_MODEL)), _ANY, _ANY, _ANY, _ANY, _SMEM]
    args = [x, pre, post, *wgu, *wd, mine]
    out_shape = [jax.ShapeDtypeStruct((S, D_MODEL), F32), jax.ShapeDtypeStruct((S, D_FF), BF16),
                 jax.ShapeDtypeStruct((S, D_FF), BF16), jax.ShapeDtypeStruct((S, D_MODEL), F32)]
    out_specs = [_rows(tm, D_MODEL), _rows(tm, D_FF), _rows(tm, D_FF), _rows(tm, D_MODEL)]
    if with_loss:
        in_specs.append(_rows(tm, D_MODEL))
        args.append(target)
        out_shape.append(jax.ShapeDtypeStruct((8, 128), F32))
        out_specs.append(_full((8, 128)))
    return pl.pallas_call(
        _ignoring(body, len(args), len(deps)), name="ffn_fwd_loss" if with_loss else "ffn_fwd",
        grid=(S // tm,), in_specs=in_specs + [_ANY] * len(deps), out_specs=out_specs, out_shape=out_shape,
        scratch_shapes=[pltpu.VMEM((N_SHARD, D_MODEL, FF_CHUNK), BF16), pltpu.VMEM((D_FF, D_MODEL), BF16),
                        pltpu.SemaphoreType.DMA((2 * N_SHARD,))],
        compiler_params=_params(dimension_semantics=("arbitrary",)),
    )(*args, *deps)


def _ffn_bwd(dout, x, f, g, u, pre, post, wgu, wd, mine, deps=()):
    S = x.shape[0]
    tm = FFN_BWD_TILE

    def body(dout_ref, x_ref, f_ref, g_ref, u_ref, pre_ref, post_ref, wgu_land, wgu_own, wd_land, wd_own, mine_ref,
             dx_ref, h_ref, dgu_ref, a_ref, df_ref, dpre_ref, dpost_ref, wgu_v, wd_v, sem):
        _load_once(_gathered(wgu_land, wgu_own, wgu_v, mine_ref[0])
                   + _gathered(wd_land, wd_own, wd_v, mine_ref[0], rows=D_FF // N_SHARD), sem)

        @pl.when(pl.program_id(0) == 0)
        def _():
            dpre_ref[...] = jnp.zeros_like(dpre_ref)
            dpost_ref[...] = jnp.zeros_like(dpost_ref)

        dout_v = dout_ref[...]
        dn = 0.5 * dout_v
        fv = f_ref[...]
        rf = _rstd(fv)
        fn = fv * rf
        dpost_ref[...] += jnp.sum(dn * fn, axis=0, keepdims=True)
        df = _norm_bwd(dn, fn, rf, post_ref[...]).astype(BF16)
        df_ref[...] = df
        dh = jnp.zeros((tm, D_MODEL), F32)
        for c in range(2):
            cols = slice(c * FF_CHUNK, (c + 1) * FF_CHUNK)
            da = _dot_nt(df, wd_v[cols, :])
            gv = g_ref[:, cols].astype(F32)
            uv = u_ref[:, cols].astype(F32)
            sg = _sigmoid(gv)
            silu = gv * sg
            a_ref[:, cols] = (silu * uv).astype(BF16)
            dg = ((da * uv) * (sg * (1.0 + gv * (1.0 - sg)))).astype(BF16)
            du = (da * silu).astype(BF16)
            dgu_ref[:, cols] = dg
            dgu_ref[:, 2 * FF_CHUNK + c * FF_CHUNK:2 * FF_CHUNK + (c + 1) * FF_CHUNK] = du
            dh = dh + _dot_nt(dg, wgu_v[c]) + _dot_nt(du, wgu_v[2 + c])
        xv = x_ref[...]
        rx = _rstd(xv)
        xn = xv * rx
        h_ref[...] = (xn * pre_ref[...]).astype(BF16)
        dpre_ref[...] += jnp.sum(dh * xn, axis=0, keepdims=True)
        dx_ref[...] = dout_v + _norm_bwd(dh, xn, rx, pre_ref[...])

    args = [dout, x, f, g, u, pre, post, *wgu, *wd, mine]
    return pl.pallas_call(
        _ignoring(body, len(args), len(deps)), name="ffn_bwd", grid=(S // tm,),
        in_specs=[_rows(tm, D_MODEL), _rows(tm, D_MODEL), _rows(tm, D_MODEL), _rows(tm, D_FF), _rows(tm, D_FF),
                  _full((1, D_MODEL)), _full((1, D_MODEL)), _ANY, _ANY, _ANY, _ANY, _SMEM] + [_ANY] * len(deps),
        out_specs=[_rows(tm, D_MODEL), _rows(tm, D_MODEL), _rows(tm, 2 * D_FF), _rows(tm, D_FF), _rows(tm, D_MODEL),
                   _full((1, D_MODEL)), _full((1, D_MODEL))],
        out_shape=[jax.ShapeDtypeStruct((S, D_MODEL), F32), jax.ShapeDtypeStruct((S, D_MODEL), BF16),
                   jax.ShapeDtypeStruct((S, 2 * D_FF), BF16), jax.ShapeDtypeStruct((S, D_FF), BF16),
                   jax.ShapeDtypeStruct((S, D_MODEL), BF16),
                   jax.ShapeDtypeStruct((1, D_MODEL), F32), jax.ShapeDtypeStruct((1, D_MODEL), F32)],
        scratch_shapes=[pltpu.VMEM((N_SHARD, D_MODEL, FF_CHUNK), BF16), pltpu.VMEM((D_FF, D_MODEL), BF16),
                        pltpu.SemaphoreType.DMA((2 * N_SHARD,))],
        compiler_params=_params(dimension_semantics=("arbitrary",)),
    )(*args, *deps)


def _wgrad(lhs, rhs, m_block, n_block, name, column_shards=False, tk=2048, deps=()):
    S, M = lhs.shape
    N = rhs.shape[1]
    k_steps = S // tk

    def body(lhs_ref, rhs_ref, out_ref, out16_ref):
        k = pl.program_id(2)

        @pl.when(k == 0)
        def _():
            out_ref[...] = jnp.zeros_like(out_ref)
        out_ref[...] += _dot_tn(lhs_ref[...], rhs_ref[...])

        @pl.when(k == k_steps - 1)
        def _():
            out16_ref[...] = out_ref[...].astype(BF16)

    if column_shards:
        assert N == N_SHARD * n_block
        shape = (N_SHARD, M, n_block)
        out_spec = pl.BlockSpec((None, m_block, n_block), lambda i, j, k: (j, i, 0))
    else:
        shape = (M, N)
        out_spec = pl.BlockSpec((m_block, n_block), lambda i, j, k: (i, j))
    out, out16 = pl.pallas_call(
        _ignoring(body, 2, len(deps)), name=name, grid=(M // m_block, N // n_block, k_steps),
        in_specs=[pl.BlockSpec((tk, m_block), lambda i, j, k: (k, i)),
                  pl.BlockSpec((tk, n_block), lambda i, j, k: (k, j))] + [_ANY] * len(deps),
        out_specs=[out_spec, out_spec],
        out_shape=[jax.ShapeDtypeStruct(shape, F32), jax.ShapeDtypeStruct(shape, BF16)],
        compiler_params=_params(dimension_semantics=("arbitrary", "arbitrary", "arbitrary")),
    )(lhs, rhs, *deps)
    if not column_shards:
        out = out.reshape(N_SHARD, M // N_SHARD, N)
        out16 = out16.reshape(N_SHARD, M // N_SHARD, N)
    return out, out16


def _rope_tables(pos, invf):
    S = pos.shape[0]
    tm = MIXER_TILE

    def body(pos_ref, invf_ref, out_ref):
        ang = pos_ref[...].astype(F32) * invf_ref[...]
        cos, sin = jnp.cos(ang), jnp.sin(ang)
        lane = lax.broadcasted_iota(jnp.int32, ang.shape, 1) % HEAD_DIM
        first = lane < ROT_DIM // 2
        second = (lane >= ROT_DIM // 2) & (lane < ROT_DIM)
        out_ref[0] = jnp.where(lane < ROT_DIM, cos, 1.0)
        out_ref[1] = jnp.where(first, sin, 0.0)
        out_ref[2] = jnp.where(second, sin, 0.0)

    return pl.pallas_call(
        body, name="rope_tables", grid=(S // tm,),
        in_specs=[_rows(tm, 1), _full((1, _LANES))],
        out_specs=pl.BlockSpec((3, tm, _LANES), lambda i: (0, i, 0)),
        out_shape=jax.ShapeDtypeStruct((3, S, _LANES), F32),
        compiler_params=_params(dimension_semantics=("arbitrary",)),
    )(pos, invf)


def _table_spec(tm):
    return pl.BlockSpec((3, tm, _LANES), lambda i: (0, i, 0))


_HALF = ROT_DIM // 2
_LANES = 128


def _rope(t, tables):
    c, s_first, s_second = tables
    return t * c - pltpu.roll(t, _LANES - _HALF, axis=1) * s_first + pltpu.roll(t, _HALF, axis=1) * s_second


def _rope_transposed(t, tables):
    c, s_first, s_second = tables
    return t * c - pltpu.roll(t * s_first, _HALF, axis=1) + pltpu.roll(t * s_second, _LANES - _HALF, axis=1)


def _store_head_variants(ref, t):
    rolled = pltpu.roll(t, HEAD_DIM, axis=1)
    low = lax.broadcasted_iota(jnp.int32, t.shape, 1) < HEAD_DIM
    zero = jnp.zeros_like(t)
    ref[0] = jnp.where(low, t, zero).astype(BF16)
    ref[1] = jnp.where(low, zero, rolled).astype(BF16)
    ref[2] = jnp.where(low, rolled, zero).astype(BF16)
    ref[3] = jnp.where(low, zero, t).astype(BF16)


def _mixer_in_fwd(x, pre, w_in_t, mine, rope, deps=()):
    S = x.shape[0]
    tm = MIXER_TILE

    def body(x_ref, pre_ref, w_land, w_own, mine_ref, rope_ref, u_ref, q_ref, k_ref, v_ref, w_v, sem):
        _load_once(_gathered(w_land, w_own, w_v, mine_ref[0], rows=IN_WIDTH // N_SHARD), sem)
        xv = x_ref[...]
        h = ((xv * _rstd(xv)) * pre_ref[...]).astype(BF16)
        z = _dot_nt(h, w_v[...])
        tables = (rope_ref[0], rope_ref[1], rope_ref[2])
        u_ref[...] = z[:, :POOL_WIDTH]
        for t in range(ATTN_WIDTH // _LANES):
            lo = POOL_WIDTH + t * _LANES
            q_ref[:, t * _LANES:(t + 1) * _LANES] = (_rope(z[:, lo:lo + _LANES], tables) * ATTN_SCALE).astype(BF16)
        kv = POOL_WIDTH + ATTN_WIDTH
        _store_head_variants(k_ref, _rope(z[:, kv:kv + KV_WIDTH], tables))
        _store_head_variants(v_ref, z[:, kv + KV_WIDTH:])

    args = [x, pre, *w_in_t, mine, rope]
    variants = pl.BlockSpec((2 * N_KV_HEADS, tm, KV_WIDTH), lambda i: (0, i, 0))
    return pl.pallas_call(
        _ignoring(body, len(args), len(deps)), name="mixer_in_fwd", grid=(S // tm,),
        in_specs=[_rows(tm, D_MODEL), _full((1, D_MODEL)), _ANY, _ANY, _SMEM, _table_spec(tm)] + [_ANY] * len(deps),
        out_specs=[_rows(tm, POOL_WIDTH), _rows(tm, ATTN_WIDTH), variants, variants],
        out_shape=[jax.ShapeDtypeStruct((S, POOL_WIDTH), F32), jax.ShapeDtypeStruct((S, ATTN_WIDTH), BF16),
                   jax.ShapeDtypeStruct((2 * N_KV_HEADS, S, KV_WIDTH), BF16),
                   jax.ShapeDtypeStruct((2 * N_KV_HEADS, S, KV_WIDTH), BF16)],
        scratch_shapes=[pltpu.VMEM((IN_WIDTH, D_MODEL), BF16), pltpu.SemaphoreType.DMA((N_SHARD,))],
        compiler_params=_params(dimension_semantics=("arbitrary",)),
    )(*args, *deps)


def _mixer_in_bwd(dres, x, pre, w_in_t, mine, du, dq, dk, dv, dk_next, dv_next, rope, deps=()):
    S = x.shape[0]
    tm = MIXER_TILE

    def body(dres_ref, x_ref, pre_ref, w_land, w_own, mine_ref, du_ref, dq_ref, dk_ref, dv_ref, dkx_ref, dvx_ref,
             rope_ref,
             dx_ref, dz_ref, h_ref, dpre_ref, w_v, sem):
        _load_once(_gathered(w_land, w_own, w_v, mine_ref[0], rows=IN_WIDTH // N_SHARD), sem)

        @pl.when(pl.program_id(0) == 0)
        def _():
            dpre_ref[...] = jnp.zeros_like(dpre_ref)

        tables = (rope_ref[0], rope_ref[1], rope_ref[2])
        dz_ref[:, :POOL_WIDTH] = du_ref[...].astype(BF16)
        for t in range(ATTN_WIDTH // _LANES):
            lo = POOL_WIDTH + t * _LANES
            dz_ref[:, lo:lo + _LANES] = _rope_transposed(dq_ref[:, t * _LANES:(t + 1) * _LANES], tables).astype(BF16)
        kv = POOL_WIDTH + ATTN_WIDTH
        has_next = pl.program_id(0) + 1 < steps
        pad = jnp.zeros((tm - BLOCK, KV_WIDTH), F32)
        dk_tile = dk_ref[...] + jnp.concatenate([pad, jnp.where(has_next, dkx_ref[...], 0.0)], axis=0)
        dv_tile = dv_ref[...] + jnp.concatenate([pad, jnp.where(has_next, dvx_ref[...], 0.0)], axis=0)
        dz_ref[:, kv:kv + KV_WIDTH] = _rope_transposed(dk_tile, tables).astype(BF16)
        dz_ref[:, kv + KV_WIDTH:] = dv_tile.astype(BF16)
        dh = _dot(dz_ref[...], w_v[...])
        xv = x_ref[...]
        rx = _rstd(xv)
        xn = xv * rx
        h_ref[...] = (xn * pre_ref[...]).astype(BF16)
        dpre_ref[...] += jnp.sum(dh * xn, axis=0, keepdims=True)
        dx_ref[...] = dres_ref[...] + _norm_bwd(dh, xn, rx, pre_ref[...])

    assert tm == ATTN_BLOCKS * BLOCK
    steps = S // tm
    nxt = pl.BlockSpec((None, BLOCK, KV_WIDTH), lambda i: (jnp.minimum(i + 1, steps - 1), 0, 0))
    args = [dres, x, pre, *w_in_t, mine, du, dq, dk, dv, dk_next, dv_next, rope]
    return pl.pallas_call(
        _ignoring(body, len(args), len(deps)), name="mixer_in_bwd", grid=(S // tm,),
        in_specs=[_rows(tm, D_MODEL), _rows(tm, D_MODEL), _full((1, D_MODEL)), _ANY, _ANY, _SMEM,
                  _rows(tm, POOL_WIDTH), _rows(tm, ATTN_WIDTH), _rows(tm, KV_WIDTH), _rows(tm, KV_WIDTH), nxt, nxt,
                  _table_spec(tm)] + [_ANY] * len(deps),
        out_specs=[_rows(tm, D_MODEL), _rows(tm, IN_WIDTH), _rows(tm, D_MODEL), _full((1, D_MODEL))],
        out_shape=[jax.ShapeDtypeStruct((S, D_MODEL), F32), jax.ShapeDtypeStruct((S, IN_WIDTH), BF16),
                   jax.ShapeDtypeStruct((S, D_MODEL), BF16), jax.ShapeDtypeStruct((1, D_MODEL), F32)],
        scratch_shapes=[pltpu.VMEM((IN_WIDTH, D_MODEL), BF16), pltpu.SemaphoreType.DMA((N_SHARD,))],
        compiler_params=_params(dimension_semantics=("arbitrary",)),
    )(*args, *deps)


def _pool_counts(tile_index, tm, width):
    t = tile_index * tm + lax.broadcasted_iota(jnp.int32, (tm, 1), 0)
    return jnp.minimum(t + 1, width).astype(F32)


def _pool_features(ext, u_tile, tile_index, tm):
    ds = []
    for gi, width in enumerate(POOL_WINDOWS):
        lanes = slice(gi * POOL_GROUP, (gi + 1) * POOL_GROUP)
        s = ext[:, lanes]
        shift = 1
        while shift < width:
            s = s + pltpu.roll(s, shift, axis=0)
            shift *= 2
        ds.append(s[HALO:, :] / _pool_counts(tile_index, tm, width) - u_tile[:, lanes])
    return ds


def _pool_fwd(u, w_pool, pool_scale, g_pool):
    S = u.shape[0]
    tm = MIXER_TILE

    def body(u_ref, w_ref, scale_ref, gain_ref, y_ref, ext_ref):
        i = pl.program_id(0)

        @pl.when(i == 0)
        def _():
            ext_ref[:HALO, :] = jnp.zeros((HALO, POOL_WIDTH), F32)

        u_tile = u_ref[...]
        ext_ref[HALO:, :] = u_tile
        ds = _pool_features(ext_ref[...], u_tile, i, tm)
        ext_ref[:HALO, :] = u_tile[tm - HALO:, :]
        ys = [_dot(ds[gi].astype(BF16), w_ref[gi].astype(BF16)) for gi in range(len(POOL_WINDOWS))]
        po = jnp.concatenate(ys, axis=1) * scale_ref[...]
        y_ref[...] = ((po * _rstd(po)) * gain_ref[...]).astype(BF16)

    return pl.pallas_call(
        body, name="pool_fwd", grid=(S // tm,),
        in_specs=[_rows(tm, POOL_WIDTH), _full((len(POOL_WINDOWS), POOL_GROUP, POOL_GROUP)),
                  _full((1, POOL_WIDTH)), _full((1, POOL_WIDTH))],
        out_specs=_rows(tm, POOL_WIDTH),
        out_shape=jax.ShapeDtypeStruct((S, POOL_WIDTH), BF16),
        scratch_shapes=[pltpu.VMEM((HALO + tm, POOL_WIDTH), F32)],
        compiler_params=_params(dimension_semantics=("arbitrary",)),
    )(u, w_pool, pool_scale, g_pool)


def _pool_bwd(dy, u, w_pool, pool_scale, g_pool):
    S = u.shape[0]
    tm = MIXER_TILE
    n_tiles = S // tm
    halo_blocks = tm // HALO

    def body(dy_ref, u_ref, uprev_ref, w_ref, scale_ref, gain_ref,
             du_ref, dw_ref, dscale_ref, dgain_ref, ext_ref, nxt_ref):
        i = pl.program_id(0)
        tile = n_tiles - 1 - i

        @pl.when(i == 0)
        def _():
            dw_ref[...] = jnp.zeros_like(dw_ref)
            dscale_ref[...] = jnp.zeros_like(dscale_ref)
            dgain_ref[...] = jnp.zeros_like(dgain_ref)
            nxt_ref[...] = jnp.zeros_like(nxt_ref)

        u_tile = u_ref[...]
        ext_ref[:HALO, :] = jnp.where(tile > 0, uprev_ref[...], 0.0)
        ext_ref[HALO:, :] = u_tile
        ds = _pool_features(ext_ref[...], u_tile, tile, tm)
        dsb = [d.astype(BF16) for d in ds]
        wb = [w_ref[gi].astype(BF16) for gi in range(len(POOL_WINDOWS))]
        yraw = jnp.concatenate([_dot(dsb[gi], wb[gi]) for gi in range(len(POOL_WINDOWS))], axis=1)
        po = yraw * scale_ref[...]
        r = _rstd(po)
        pn = po * r
        dyv = dy_ref[...]
        dgain_ref[...] += jnp.sum(dyv * pn, axis=0, keepdims=True)
        dpo = _norm_bwd(dyv, pn, r, gain_ref[...])
        dscale_ref[...] += jnp.sum(dpo * yraw, axis=0, keepdims=True)
        dyraw = (dpo * scale_ref[...]).astype(BF16)
        for gi, width in enumerate(POOL_WINDOWS):
            lanes = slice(gi * POOL_GROUP, (gi + 1) * POOL_GROUP)
            dw_ref[gi] += _dot_tn(dsb[gi], dyraw[:, lanes])
            dd = _dot_nt(dyraw[:, lanes], wb[gi])
            ddc = dd / _pool_counts(tile, tm, width)
            ext_ref[:tm, lanes] = ddc
            ext_ref[tm:, lanes] = nxt_ref[:, lanes]
            s = ext_ref[:, lanes]
            shift = 1
            while shift < width:
                s = s + pltpu.roll(s, HALO + tm - shift, axis=0)
                shift *= 2
            du_ref[:, lanes] = s[:tm, :] - dd
            nxt_ref[:, lanes] = ddc[:HALO, :]

    return pl.pallas_call(
        body, name="pool_bwd", grid=(n_tiles,),
        in_specs=[pl.BlockSpec((tm, POOL_WIDTH), lambda i: (n_tiles - 1 - i, 0)),
                  pl.BlockSpec((tm, POOL_WIDTH), lambda i: (n_tiles - 1 - i, 0)),
                  pl.BlockSpec((HALO, POOL_WIDTH), lambda i: (jnp.maximum((n_tiles - 1 - i) * halo_blocks - 1, 0), 0)),
                  _full((len(POOL_WINDOWS), POOL_GROUP, POOL_GROUP)), _full((1, POOL_WIDTH)), _full((1, POOL_WIDTH))],
        out_specs=[pl.BlockSpec((tm, POOL_WIDTH), lambda i: (n_tiles - 1 - i, 0)),
                   _full((len(POOL_WINDOWS), POOL_GROUP, POOL_GROUP)), _full((1, POOL_WIDTH)), _full((1, POOL_WIDTH))],
        out_shape=[jax.ShapeDtypeStruct((S, POOL_WIDTH), F32),
                   jax.ShapeDtypeStruct((len(POOL_WINDOWS), POOL_GROUP, POOL_GROUP), F32),
                   jax.ShapeDtypeStruct((1, POOL_WIDTH), F32), jax.ShapeDtypeStruct((1, POOL_WIDTH), F32)],
        scratch_shapes=[pltpu.VMEM((HALO + tm, POOL_WIDTH), F32), pltpu.VMEM((HALO, POOL_WIDTH), F32)],
        compiler_params=_params(dimension_semantics=("arbitrary",)),
    )(dy, u, u, w_pool, pool_scale, g_pool)


def _variant(head):
    return 2 * (head // (N_HEADS // N_KV_HEADS)) + head % 2


def _own_block(shape=(BLOCK, BLOCK)):
    r = lax.broadcasted_iota(jnp.int32, shape, 0)
    i = lax.broadcasted_iota(jnp.int32, shape, 1)
    return r <= i


def _fold_band(own, from_own, from_prev):
    return jnp.where(own, from_own, from_prev)


def _scores_by_head(own_tiles, prev_tiles, q_tiles):
    stacks = [jnp.concatenate(q_tiles[:2], axis=0), jnp.concatenate(q_tiles[2:], axis=0)]
    by_var = [_dot_nt(jnp.concatenate([own_tiles[v], prev_tiles[v]], axis=0), stacks[v // 2])
              for v in range(2 * N_KV_HEADS)]
    quadrant = lambda h, rows: by_var[_variant(h)][rows * BLOCK:(rows + 1) * BLOCK,
                                                   ((h // 2) % 2) * BLOCK:((h // 2) % 2 + 1) * BLOCK]
    return [quadrant(h, 0) for h in range(N_HEADS)], [quadrant(h, 1) for h in range(N_HEADS)]


def _softmax_t(s, sink):
    m = jnp.maximum(jnp.max(s, axis=0, keepdims=True), sink)
    p = jnp.exp(s - m)
    p_sink = jnp.exp(sink - m)
    inv = 1.0 / (jnp.sum(p, axis=0, keepdims=True) + p_sink)
    return p * inv, p_sink * inv


def _attn_fwd(q, kz, vz, sinks, g_attn, y_pool, x, w_out, mine, post):
    S = q.shape[0]
    tq = ATTN_BLOCKS * BLOCK
    n_var = 2 * N_KV_HEADS

    def body(q_ref, kp_ref, kc_ref, vp_ref, vc_ref, sinks_ref, gain_ref, yp_ref, x_ref, w_land, w_own, mine_ref,
             post_ref, o_ref, out_ref, m_ref, y_ref, w_v, sem):
        _load_once(_gathered(w_land, w_own, w_v, mine_ref[0], rows=D_MODEL // N_SHARD), sem)
        step = pl.program_id(0)
        own = _own_block()
        zero = jnp.zeros((BLOCK, BLOCK), F32)

        def tiles(cur_ref, prev_ref, j):
            rows = lambda jj: slice(jj * BLOCK, (jj + 1) * BLOCK)
            return ([cur_ref[v, rows(j), :] for v in range(n_var)],
                    [prev_ref[v] if j == 0 else cur_ref[v, rows(j - 1), :] for v in range(n_var)])

        scores = []
        for j in range(ATTN_BLOCKS):
            q_pairs = [q_ref[j * BLOCK:(j + 1) * BLOCK, i * _LANES:(i + 1) * _LANES] for i in range(N_HEADS // 2)]
            scores.append(_scores_by_head(*tiles(kc_ref, kp_ref, j), q_pairs))
        probs = []
        for j in range(ATTN_BLOCKS):
            s_own, s_prev = scores[j]
            no_prev = jnp.where(step > 0, 0.0, NEG_INF) if j == 0 else 0.0
            p_own, p_prev = [], []
            for h in range(N_HEADS):
                p, _ = _softmax_t(_fold_band(own, s_own[h], s_prev[h] + no_prev), sinks_ref[0, h])
                p_own.append(jnp.where(own, p, zero).astype(BF16))
                p_prev.append(jnp.where(own, zero, p).astype(BF16))
            probs.append((p_own, p_prev))
        blocks = []
        for j in range(ATTN_BLOCKS):
            p_own, p_prev = probs[j]
            v_own, v_prev = tiles(vc_ref, vp_ref, j)
            pairs = []
            for i in range(N_HEADS // 2):
                acc = None
                for h in (2 * i, 2 * i + 1):
                    part = _dot_tn(p_own[h], v_own[_variant(h)]) + _dot_tn(p_prev[h], v_prev[_variant(h)])
                    acc = part if acc is None else acc + part
                pairs.append(acc)
            blocks.append(jnp.concatenate(pairs, axis=1))
        o = jnp.concatenate(blocks, axis=0)
        o_ref[...] = o
        y_ref[:, :POOL_WIDTH] = yp_ref[...]
        y_ref[:, POOL_WIDTH:] = ((o * _rstd(o)) * gain_ref[...]).astype(BF16)
        m = _dot(y_ref[...], w_v[...])
        m_ref[...] = m
        out_ref[...] = x_ref[...] + (m * _rstd(m)) * post_ref[...]

    prev = pl.BlockSpec((n_var, BLOCK, KV_WIDTH), lambda g: (0, jnp.maximum(g * ATTN_BLOCKS - 1, 0), 0))
    cur = pl.BlockSpec((n_var, tq, KV_WIDTH), lambda g: (0, g, 0))
    return pl.pallas_call(
        body, name="attn_fwd", grid=(S // tq,),
        in_specs=[_rows(tq, ATTN_WIDTH), prev, cur, prev, cur,
                  pl.BlockSpec(memory_space=pltpu.SMEM), _full((1, ATTN_WIDTH)),
                  _rows(tq, POOL_WIDTH), _rows(tq, D_MODEL), _ANY, _ANY, _SMEM, _full((1, D_MODEL))],
        out_specs=[_rows(tq, ATTN_WIDTH), _rows(tq, D_MODEL), _rows(tq, D_MODEL), _rows(tq, D_MODEL)],
        out_shape=[jax.ShapeDtypeStruct((S, ATTN_WIDTH), F32), jax.ShapeDtypeStruct((S, D_MODEL), F32),
                   jax.ShapeDtypeStruct((S, D_MODEL), F32), jax.ShapeDtypeStruct((S, D_MODEL), BF16)],
        scratch_shapes=[pltpu.VMEM((D_MODEL, D_MODEL), BF16), pltpu.SemaphoreType.DMA((N_SHARD,))],
        compiler_params=_params(dimension_semantics=("arbitrary",)),
    )(q, kz, kz, vz, vz, sinks, g_attn, y_pool, x, *w_out, mine, post)


def _attn_bwd(dy, o, q, kz, vz, sinks, g_attn):
    S = q.shape[0]
    tq = ATTN_BLOCKS * BLOCK
    n_var = 2 * N_KV_HEADS

    def body(dy_ref, o_ref, q_ref, kp_ref, kc_ref, vp_ref, vc_ref, sinks_ref, gain_ref,
             dq_ref, dk_ref, dv_ref, dkx_ref, dvx_ref, dsink_ref, dgain_ref):
        step = pl.program_id(0)

        @pl.when(step == 0)
        def _():
            dsink_ref[...] = jnp.zeros_like(dsink_ref)
            dgain_ref[...] = jnp.zeros_like(dgain_ref)

        ov = o_ref[...]
        r = _rstd(ov)
        on = ov * r
        dyv = dy_ref[...]
        dgain_ref[...] += jnp.sum(dyv * on, axis=0, keepdims=True)
        do = _norm_bwd(dyv, on, r, gain_ref[...]).astype(BF16)
        own = _own_block()
        zero = jnp.zeros((BLOCK, BLOCK), F32)
        split = lambda t: (jnp.where(own, t, zero).astype(BF16), jnp.where(own, zero, t).astype(BF16))
        rows = lambda j: slice(j * BLOCK, (j + 1) * BLOCK)
        heads = range(N_HEADS)

        def tiles(cur_ref, prev_ref, j):
            return ([cur_ref[v, rows(j), :] for v in range(n_var)],
                    [prev_ref[v] if j == 0 else cur_ref[v, rows(j - 1), :] for v in range(n_var)])

        q_pairs = [[q_ref[rows(j), i * _LANES:(i + 1) * _LANES] for i in range(N_HEADS // 2)] for j in range(ATTN_BLOCKS)]
        do_pairs = [[do[rows(j), i * _LANES:(i + 1) * _LANES] for i in range(N_HEADS // 2)] for j in range(ATTN_BLOCKS)]
        scores = [(_scores_by_head(*tiles(kc_ref, kp_ref, j), q_pairs[j]),
                   _scores_by_head(*tiles(vc_ref, vp_ref, j), do_pairs[j])) for j in range(ATTN_BLOCKS)]
        parts, sink_sum = [], None
        for j in range(ATTN_BLOCKS):
            (s_own, s_prev), (dp_own, dp_prev) = scores[j]
            no_prev = jnp.where(step > 0, 0.0, NEG_INF) if j == 0 else 0.0
            ds_parts, p_parts, sink_rows = [], [], []
            for h in heads:
                p, p_sink = _softmax_t(_fold_band(own, s_own[h], s_prev[h] + no_prev), sinks_ref[0, h])
                dp = _fold_band(own, dp_own[h], dp_prev[h])
                delta = jnp.sum(p * dp, axis=0, keepdims=True)
                ds_parts.append(split(p * (dp - delta)))
                p_parts.append(split(p))
                sink_rows.append(jnp.zeros((1, _LANES), F32) - jnp.sum(p_sink * delta))
            block_sinks = jnp.concatenate(sink_rows, axis=0)
            sink_sum = block_sinks if sink_sum is None else sink_sum + block_sinks
            parts.append((ds_parts, p_parts))
        dsink_ref[...] += sink_sum
        low = lax.broadcasted_iota(jnp.int32, (BLOCK, _LANES), 1) < HEAD_DIM

        def merge(acc):
            return jnp.where(low, acc[0] + pltpu.roll(acc[1], HEAD_DIM, axis=1),
                             acc[3] + pltpu.roll(acc[2], HEAD_DIM, axis=1))
        add = lambda acc, var, t: acc.__setitem__(var, t if acc[var] is None else acc[var] + t)
        k_own, k_prev, v_own, v_prev = [], [], [], []
        for j in range(ATTN_BLOCKS):
            ds_parts, p_parts = parts[j]
            kt_own, kt_prev = tiles(kc_ref, kp_ref, j)
            dk_own, dk_prev, dv_own, dv_prev = ([None] * n_var for _ in range(4))
            for i in range(N_HEADS // 2):
                dq_pair = None
                for h in (2 * i, 2 * i + 1):
                    var = _variant(h)
                    (ds_o, ds_p), (p_o, p_p) = ds_parts[h], p_parts[h]
                    part = _dot_tn(ds_o, kt_own[var]) + _dot_tn(ds_p, kt_prev[var])
                    dq_pair = part if dq_pair is None else dq_pair + part
                    add(dk_own, var, _dot(ds_o, q_pairs[j][i]))
                    add(dk_prev, var, _dot(ds_p, q_pairs[j][i]))
                    add(dv_own, var, _dot(p_o, do_pairs[j][i]))
                    add(dv_prev, var, _dot(p_p, do_pairs[j][i]))
                dq_ref[rows(j), i * _LANES:(i + 1) * _LANES] = dq_pair * ATTN_SCALE
            k_own.append(merge(dk_own))
            k_prev.append(merge(dk_prev))
            v_own.append(merge(dv_own))
            v_prev.append(merge(dv_prev))
        for j in range(ATTN_BLOCKS):
            last = j == ATTN_BLOCKS - 1
            dk_ref[rows(j), :] = k_own[j] if last else k_own[j] + k_prev[j + 1]
            dv_ref[rows(j), :] = v_own[j] if last else v_own[j] + v_prev[j + 1]
        dkx_ref[...] = k_prev[0]
        dvx_ref[...] = v_prev[0]

    steps = S // tq
    prev = pl.BlockSpec((n_var, BLOCK, KV_WIDTH), lambda g: (0, jnp.maximum(g * ATTN_BLOCKS - 1, 0), 0))
    cur = pl.BlockSpec((n_var, tq, KV_WIDTH), lambda g: (0, g, 0))
    nxt = pl.BlockSpec((None, BLOCK, KV_WIDTH), lambda g: (g, 0, 0))
    return pl.pallas_call(
        body, name="attn_bwd", grid=(steps,),
        in_specs=[_rows(tq, ATTN_WIDTH, col=1), _rows(tq, ATTN_WIDTH), _rows(tq, ATTN_WIDTH), prev, cur, prev, cur,
                  pl.BlockSpec(memory_space=pltpu.SMEM), _full((1, ATTN_WIDTH))],
        out_specs=[_rows(tq, ATTN_WIDTH), _rows(tq, KV_WIDTH), _rows(tq, KV_WIDTH), nxt, nxt,
                   _full((N_HEADS, _LANES)), _full((1, ATTN_WIDTH))],
        out_shape=[jax.ShapeDtypeStruct((S, ATTN_WIDTH), F32), jax.ShapeDtypeStruct((S, KV_WIDTH), F32),
                   jax.ShapeDtypeStruct((S, KV_WIDTH), F32),
                   jax.ShapeDtypeStruct((steps, BLOCK, KV_WIDTH), F32), jax.ShapeDtypeStruct((steps, BLOCK, KV_WIDTH), F32),
                   jax.ShapeDtypeStruct((N_HEADS, _LANES), F32), jax.ShapeDtypeStruct((1, ATTN_WIDTH), F32)],
        compiler_params=_params(dimension_semantics=("arbitrary",)),
    )(dy, o, q, kz, kz, vz, vz, sinks, g_attn)


def _mixer_out_bwd(dout, m, w_out, mine, post, deps=()):
    S = m.shape[0]
    tm = MIXER_TILE

    def body(dout_ref, m_ref, w_land, w_own, mine_ref, post_ref, dy_ref, dm_ref, dpost_ref, w_v, sem):
        _load_once(_gathered(w_land, w_own, w_v, mine_ref[0], rows=D_MODEL // N_SHARD), sem)

        @pl.when(pl.program_id(0) == 0)
        def _():
            dpost_ref[...] = jnp.zeros_like(dpost_ref)

        mv = m_ref[...]
        r = _rstd(mv)
        mn = mv * r
        dv = dout_ref[...]
        dpost_ref[...] += jnp.sum(dv * mn, axis=0, keepdims=True)
        dm = _norm_bwd(dv, mn, r, post_ref[...]).astype(BF16)
        dm_ref[...] = dm
        dy_ref[...] = _dot_nt(dm, w_v[...])

    args = [dout, m, *w_out, mine, post]
    return pl.pallas_call(
        _ignoring(body, len(args), len(deps)), name="mixer_out_bwd", grid=(S // tm,),
        in_specs=[_rows(tm, D_MODEL), _rows(tm, D_MODEL), _ANY, _ANY, _SMEM, _full((1, D_MODEL))] + [_ANY] * len(deps),
        out_specs=[_rows(tm, D_MODEL), _rows(tm, D_MODEL), _full((1, D_MODEL))],
        out_shape=[jax.ShapeDtypeStruct((S, D_MODEL), F32), jax.ShapeDtypeStruct((S, D_MODEL), BF16),
                   jax.ShapeDtypeStruct((1, D_MODEL), F32)],
        scratch_shapes=[pltpu.VMEM((D_MODEL, D_MODEL), BF16), pltpu.SemaphoreType.DMA((N_SHARD,))],
        compiler_params=_params(dimension_semantics=("arbitrary",)),
    )(*args, *deps)


def _inv_freq_row():
    inv_freq = ROPE_THETA ** (-jnp.arange(0, ROT_DIM, 2, dtype=F32) / ROT_DIM)
    per_head = jnp.concatenate([inv_freq, inv_freq, jnp.zeros((HEAD_DIM - ROT_DIM,), F32)])
    return jnp.tile(per_head, _LANES // HEAD_DIM).reshape(1, _LANES)


def _local_step(x, pos, target, small, mine, weights_of, grads_ready):
    rope = _rope_tables(pos, _inv_freq_row())
    wgu1, wd1 = weights_of("ffn1", (rope,))
    x1, g1, u1, f1 = _ffn_fwd(x, small["ffn1_pre"], small["ffn1_post"], wgu1, wd1, mine)
    w_in_t, w_out = weights_of("mixer", (x1,))
    u, q, k, v = _mixer_in_fwd(x1, small["mix_pre"], w_in_t, mine, rope)
    y_pool = _pool_fwd(u, small["w_pool"], small["pool_scale"], small["g_pool"])
    o, x2, m, y = _attn_fwd(q, k, v, small["sinks"], small["g_attn"], y_pool, x1, w_out, mine, small["mix_post"])
    wgu2, wd2 = weights_of("ffn2", (x2,))
    dx3, g2, u2, f2, loss_acc = _ffn_fwd(x2, small["ffn2_pre"], small["ffn2_post"], wgu2, wd2, mine, target=target)
    grads = {"loss": loss_acc * (0.5 / D_MODEL)}
    dx2, h3, dgu2, a2, df2, grads["ffn2_pre"], grads["ffn2_post"] = _ffn_bwd(
        dx3, x2, f2, g2, u2, small["ffn2_pre"], small["ffn2_post"], wgu2, wd2, mine)
    dwgu2 = _wgrad(h3, dgu2, D_MODEL, FF_CHUNK, "wgrad_gu2", column_shards=True)
    dwd2 = _wgrad(a2, df2, FF_CHUNK, D_MODEL, "wgrad_down2")
    deps = grads_ready("ffn2", {"ffn2_w_gu": dwgu2, "ffn2_w_down": dwd2})
    dy, dm, grads["mix_post"] = _mixer_out_bwd(dx2, m, w_out, mine, small["mix_post"], deps=deps)
    dw_out = _wgrad(y, dm, D_MODEL, D_MODEL, "wgrad_out")
    dq, dk, dv, dk_next, dv_next, dsinks, grads["g_attn"] = _attn_bwd(dy, o, q, k, v, small["sinks"], small["g_attn"])
    grads["sinks"] = dsinks[:, 0].reshape(1, N_HEADS)
    du, grads["w_pool"], grads["pool_scale"], grads["g_pool"] = _pool_bwd(
        dy, u, small["w_pool"], small["pool_scale"], small["g_pool"])
    dx1, dz, h2, grads["mix_pre"] = _mixer_in_bwd(dx2, x1, small["mix_pre"], w_in_t, mine, du, dq, dk, dv, dk_next, dv_next, rope)
    dw_in_t = _wgrad(dz, h2, IN_WIDTH, D_MODEL, "wgrad_in")
    deps = grads_ready("mixer", {"w_in": dw_in_t, "w_out": dw_out})
    dx, h1, dgu1, a1, df1, grads["ffn1_pre"], grads["ffn1_post"] = _ffn_bwd(
        dx1, x, f1, g1, u1, small["ffn1_pre"], small["ffn1_post"], wgu1, wd1, mine, deps=deps)
    dwgu1 = _wgrad(h1, dgu1, D_MODEL, FF_CHUNK, "wgrad_gu1", column_shards=True)
    deps = grads_ready("ffn1_gu", {"ffn1_w_gu": dwgu1}, small=grads)
    dwd1 = _wgrad(a1, df1, FF_CHUNK, D_MODEL, "wgrad_down1", deps=deps)
    grads_ready("ffn1_down", {"ffn1_w_down": dwd1})
    return dx


def _place():
    return lax.axis_index("x"), lax.axis_index("y"), lax.axis_index("c")


def _other_chips(x, y):
    return [(1 - x, y), (x, 1 - y), (1 - x, 1 - y)]


def _hbm_shape(shape, dtype):
    return jax.ShapeDtypeStruct(shape, dtype)


_HBM = pl.BlockSpec(memory_space=pltpu.HBM)
_SEM = pl.BlockSpec(memory_space=pltpu.SEMAPHORE)
_EFFECT = pltpu.SideEffectType.DATAFLOW_SIDE_EFFECTING
GATHER, GATHER_HALF, REDUCE, BROADCAST = "gather", "gather_half", "reduce", "broadcast"
N_DEVICES = 8


def _in_hbm(a):
    return pltpu.with_memory_space_constraint(a, pltpu.HBM)


def _core_half(rows, c):
    return pl.ds(pl.multiple_of(c * (rows // 2), 16), rows // 2)


def _chip_copies(kind, srcs, lands, send_sems, recv_sems):
    x, y, c = _place()
    mine = 2 * x + y
    copies = []
    for w in range(len(srcs)):
        if kind == BROADCAST:
            peers = [(x ^ (k >> 2), y ^ ((k >> 1) & 1), c ^ (k & 1)) for k in range(1, N_DEVICES)]
        else:
            peers = [(px, py, c) for px, py in _other_chips(x, y)]
        for k, (px, py, pc) in enumerate(peers):
            if kind == GATHER:
                src, dst = srcs[w], lands[w].at[mine]
            elif kind == GATHER_HALF:
                half = _core_half(srcs[w].shape[0], c)
                src, dst = srcs[w].at[half, :], lands[w].at[mine, half, :]
            elif kind == BROADCAST:
                src, dst = srcs[w], lands[w].at[2 * mine + c]
            else:
                src, dst = srcs[w].at[2 * px + py], lands[w].at[k]
            pair = len(peers) * w + k
            copies.append(pltpu.make_async_remote_copy(
                src_ref=src, dst_ref=dst, send_sem=send_sems.at[pair], recv_sem=recv_sems.at[pair],
                device_id=(px, py, pc), device_id_type=MESH))
    return copies


def _landing_shape(kind, src):
    if kind == REDUCE:
        return (N_SHARD - 1,) + src.shape[1:]
    return ((N_DEVICES if kind == BROADCAST else N_SHARD),) + src.shape


def _peer_count(kind):
    return N_DEVICES - 1 if kind == BROADCAST else N_SHARD - 1


def _exchange_start(kinds, groups, name):
    sizes = [len(g) for g in groups]
    flat = [s for g in groups for s in g]
    n, ng = len(flat), len(groups)

    def body(*refs):
        srcs, lands = refs[:n], refs[n:2 * n]
        sems = refs[2 * n:2 * n + 2 * ng]
        token = refs[-1]
        start = 0
        for gi, size in enumerate(sizes):
            for cp in _chip_copies(kinds[gi], srcs[start:start + size], lands[start:start + size],
                                   sems[2 * gi], sems[2 * gi + 1]):
                cp.start()
            start += size
        token[...] = jnp.zeros_like(token)

    landings = [lax.empty(_landing_shape(kind, s), s.dtype) for kind, g in zip(kinds, groups) for s in g]
    sem_shapes = [pltpu.SemaphoreType.DMA((size * _peer_count(kind),)) for kind, size in zip(kinds, sizes)
                  for _ in range(2)]
    outs = pl.pallas_call(
        body, name=name,
        in_specs=[_HBM] * (2 * n),
        out_specs=[_SEM] * (2 * ng) + [_HBM] * (2 * n) + [pl.BlockSpec(memory_space=pltpu.VMEM)],
        out_shape=sem_shapes + [pltpu.HBM(a.shape, a.dtype) for a in flat + landings]
        + [jax.ShapeDtypeStruct((8, _LANES), F32)],
        input_output_aliases={i: 2 * ng + i for i in range(2 * n)},
        compiler_params=pltpu.CompilerParams(has_side_effects=_EFFECT),
    )(*[_in_hbm(a) for a in flat + landings])
    sems, srcs, lands, token = outs[:2 * ng], outs[2 * ng:2 * ng + n], outs[2 * ng + n:2 * ng + 2 * n], outs[-1]
    handles, start = [], 0
    for gi, size in enumerate(sizes):
        handles.append((sems[2 * gi], sems[2 * gi + 1], srcs[start:start + size], lands[start:start + size]))
        start += size
    return handles, token


def _exchange_wait(kind, handle, after, name):
    send_sems, recv_sems, srcs, lands = handle
    n = len(srcs)

    def body(*refs):
        copies = _chip_copies(kind, refs[:n], refs[n:2 * n], refs[2 * n], refs[2 * n + 1])
        for cp in copies:
            cp.wait_send()
        for cp in copies:
            cp.wait_recv()

    outs = pl.pallas_call(
        body, name=name,
        in_specs=[_HBM] * (2 * n) + [_SEM, _SEM] + [_ANY] * len(after),
        out_specs=[_HBM] * (2 * n),
        out_shape=[pltpu.HBM(a.shape, a.dtype) for a in list(srcs) + list(lands)],
        input_output_aliases={i: i for i in range(2 * n)},
        compiler_params=pltpu.CompilerParams(has_side_effects=_EFFECT),
    )(*srcs, *lands, send_sems, recv_sems, *after)
    return outs[:n], outs[n:]


def _swap_gathered_halves(lands, name):
    n = len(lands)

    def body(*refs):
        bufs = refs[n:2 * n]
        send_sems, recv_sems = refs[2 * n:]
        x, y, c = _place()
        mine = 2 * x + y
        sends, arrivals = [], []
        for w in range(n):
            rows = bufs[w].shape[1]
            for d in range(1, N_SHARD):
                slot = (mine + d) % N_SHARD
                sems = dict(send_sem=send_sems.at[(N_SHARD - 1) * w + d - 1],
                            recv_sem=recv_sems.at[(N_SHARD - 1) * w + d - 1],
                            device_id=(x, y, 1 - c), device_id_type=MESH)
                fetched = bufs[w].at[slot, _core_half(rows, c), :]
                missing = bufs[w].at[slot, _core_half(rows, 1 - c), :]
                sends.append(pltpu.make_async_remote_copy(src_ref=fetched, dst_ref=fetched, **sems))
                arrivals.append(pltpu.make_async_remote_copy(src_ref=missing, dst_ref=missing, **sems))
        for cp in sends:
            cp.start()
        for cp in arrivals:
            cp.wait_recv()
        for cp in sends:
            cp.wait_send()

    return pl.pallas_call(
        body, name=name, in_specs=[_ANY] * n, out_specs=[_ANY] * n,
        out_shape=[_hbm_shape(a.shape, a.dtype) for a in lands],
        input_output_aliases={i: i for i in range(n)},
        scratch_shapes=[pltpu.SemaphoreType.DMA((n * (N_SHARD - 1),)), pltpu.SemaphoreType.DMA((n * (N_SHARD - 1),))],
        compiler_params=pltpu.CompilerParams(has_side_effects=True),
    )(*lands)


def _swap_with_sibling(partials, name):
    n = len(partials)

    def body(*refs):
        ins, outs = refs[:n], refs[n:2 * n]
        send_sems, recv_sems = refs[2 * n:]
        x, y, c = _place()
        sends = [pltpu.make_async_remote_copy(
            src_ref=ins[w], dst_ref=outs[w], send_sem=send_sems.at[w], recv_sem=recv_sems.at[w],
            device_id=(x, y, 1 - c), device_id_type=MESH) for w in range(n)]
        for cp in sends:
            cp.start()
        for cp in sends:
            cp.wait_recv()
        for cp in sends:
            cp.wait_send()

    return pl.pallas_call(
        body, name=name,
        in_specs=[_ANY] * n, out_specs=[_ANY] * n,
        out_shape=[_hbm_shape(p.shape, p.dtype) for p in partials],
        scratch_shapes=[pltpu.SemaphoreType.DMA((n,)), pltpu.SemaphoreType.DMA((n,))],
        compiler_params=pltpu.CompilerParams(has_side_effects=True),
    )(*partials)


def _row_block(rows, cap):
    best = None
    for cand in range(16, min(rows, cap) + 1, 16):
        if rows % cand == 0:
            best = cand
    assert best is not None, rows
    return best


def _chip_partial(own, received, shard, name):
    _, R, C = own.shape
    rb = _row_block(R, 512)

    def body(shard_ref, own_ref, rec_ref, out_ref):
        acc = own_ref[...]
        for k in range(3):
            acc = acc + rec_ref[k].astype(F32)
        out_ref[...] = acc.astype(BF16)

    return pl.pallas_call(
        body, name=name,
        grid_spec=pltpu.PrefetchScalarGridSpec(
            num_scalar_prefetch=1, grid=(R // rb,),
            in_specs=[pl.BlockSpec((None, rb, C), lambda i, s: (s[0], i, 0)),
                      pl.BlockSpec((3, rb, C), lambda i, s: (0, i, 0))],
            out_specs=pl.BlockSpec((rb, C), lambda i, s: (i, 0))),
        out_shape=jax.ShapeDtypeStruct((R, C), BF16),
        compiler_params=_params(dimension_semantics=("arbitrary",)),
    )(shard, own, received)


def _adamw(w, m, v, g_parts, name, slot=None):
    R, C = w.shape
    by_device = slot is not None
    rb = _row_block(R, 256) if R % 16 == 0 else R

    def body(w_ref, m_ref, v_ref, *refs):
        g_refs, (grad_ref, delta_ref, m_out, v_out) = refs[:-4], refs[-4:]
        if by_device:
            own_ref, land_ref, slot_ref = g_refs
            part = lambda d: jnp.where(slot_ref[0] == d, own_ref[...], land_ref[d])
            g = part(0)
            for d in range(1, N_DEVICES):
                g = g + part(d)
        else:
            g = g_refs[0][...].astype(F32)
            for g_ref in g_refs[1:]:
                g = g + g_ref[...].astype(F32)
        grad_ref[...] = g
        new_m = ADAM_B1 * m_ref[...] + (1.0 - ADAM_B1) * g
        new_v = ADAM_B2 * v_ref[...] + (1.0 - ADAM_B2) * (g * g)
        m_hat = new_m / (1.0 - ADAM_B1 ** ADAM_STEP)
        v_hat = new_v / (1.0 - ADAM_B2 ** ADAM_STEP)
        delta_ref[...] = -ADAM_LR * (m_hat / (jnp.sqrt(v_hat) + ADAM_EPS) + ADAM_WD * w_ref[...])
        m_out[...] = new_m
        v_out[...] = new_v

    spec = pl.BlockSpec((rb, C), lambda i: (i, 0))
    if by_device:
        g_specs = [spec, pl.BlockSpec((N_DEVICES, rb, C), lambda i: (0, i, 0)), _SMEM]
        g_parts = list(g_parts) + [slot]
    else:
        g_specs = [spec] * len(g_parts)
    return pl.pallas_call(
        body, name=name, grid=(R // rb,),
        in_specs=[spec, spec, spec] + g_specs,
        out_specs=[spec] * 4,
        out_shape=[jax.ShapeDtypeStruct((R, C), F32)] * 4,
        compiler_params=_params(dimension_semantics=("arbitrary",)),
    )(w, m, v, *g_parts)


SMALL_NAMES = ("ffn1_pre", "ffn1_post", "mix_pre", "pool_scale", "sinks", "g_pool", "g_attn", "mix_post",
               "ffn2_pre", "ffn2_post", "w_pool")
_SLAB_PART = 8 * _LANES


SLAB_NAMES = SMALL_NAMES + ("loss",)


def _to_slab(parts):
    rows = []
    for name in SLAB_NAMES:
        flat = parts[name].reshape(-1) if name in parts else jnp.zeros((_SLAB_PART,), F32)
        padded = -(-flat.shape[0] // _SLAB_PART) * _SLAB_PART
        rows.append(jnp.pad(flat, (0, padded - flat.shape[0])).reshape(-1, _LANES))
    return jnp.concatenate(rows, axis=0)


def _from_slab(slab, like):
    out, row = {}, 0
    for name in SLAB_NAMES:
        size = like[name].size
        rows = -(-size // _SLAB_PART) * (_SLAB_PART // _LANES)
        out[name] = slab[row:row + rows].reshape(-1)[:size].reshape(like[name].shape)
        row += rows
    return out


BIG_NAMES = ("ffn1_w_gu", "ffn1_w_down", "w_in", "w_out", "ffn2_w_gu", "ffn2_w_down")
WEIGHT_ORDER = ("ffn1_pre", "ffn1_w_gu", "ffn1_w_down", "ffn1_post", "mix_pre", "w_in", "w_pool", "pool_scale",
                "sinks", "g_pool", "g_attn", "w_out", "mix_post", "ffn2_pre", "ffn2_w_gu", "ffn2_w_down", "ffn2_post")


def kernel(x, positions, ffn1_pre, ffn1_w_gu, ffn1_w_down, ffn1_post, mix_pre, w_in, w_pool, pool_scale, sinks, g_pool, g_attn, w_out, mix_post, ffn2_pre, ffn2_w_gu, ffn2_w_down, ffn2_post, loss_target, m_ffn1_pre, m_ffn1_w_gu, m_ffn1_w_down, m_ffn1_post, m_mix_pre, m_w_in, m_w_pool, m_pool_scale, m_sinks, m_g_pool, m_g_attn, m_w_out, m_mix_post, m_ffn2_pre, m_ffn2_w_gu, m_ffn2_w_down, m_ffn2_post, v_ffn1_pre, v_ffn1_w_gu, v_ffn1_w_down, v_ffn1_post, v_mix_pre, v_w_in, v_w_pool, v_pool_scale, v_sinks, v_g_pool, v_g_attn, v_w_out, v_mix_post, v_ffn2_pre, v_ffn2_w_gu, v_ffn2_w_down, v_ffn2_post):
    given = dict(locals())
    weights = {n: given[n][0] for n in WEIGHT_ORDER}
    moments_m = {n: given["m_" + n][0] for n in WEIGHT_ORDER}
    moments_v = {n: given["v_" + n][0] for n in WEIGHT_ORDER}
    S = x.shape[1]
    shard = (2 * lax.axis_index("x") + lax.axis_index("y")).astype(jnp.int32).reshape(1)

    local16 = {n: weights[n].astype(BF16) for n in BIG_NAMES if n != "w_in"}
    local16["w_in"] = weights["w_in"].T.astype(BF16)
    gather_groups = {"ffn1": ("ffn1_w_gu", "ffn1_w_down"), "mixer": ("w_in", "w_out"),
                     "ffn2": ("ffn2_w_gu", "ffn2_w_down")}
    gather_kinds = {"ffn1": GATHER_HALF, "mixer": GATHER, "ffn2": GATHER}
    handles, _ = _exchange_start(list(gather_kinds.values()),
                                 [[local16[n] for n in names] for names in gather_groups.values()], "gather_start")
    gather_handles = dict(zip(gather_groups, handles))

    def weights_of(group, after):
        kind = gather_kinds[group]
        owns, lands = _exchange_wait(kind, gather_handles[group], list(after), "gather_wait_" + group)
        if kind == GATHER_HALF:
            lands = _swap_gathered_halves(lands, "swap_gathered_" + group)
        return list(zip(lands, owns))

    pending, last_token = {}, []

    def grads_ready(group, grads, small=None):
        names = list(grads)
        kinds, sources = [REDUCE], [[grads[n][1] for n in names]]
        if small is not None:
            kinds, sources = kinds + [BROADCAST], sources + [[_to_slab(small)]]
        handles, token = _exchange_start(kinds, sources, "reduce_start_" + group)
        handle = handles[0]
        if small is not None:
            pending["small"] = handles[1]
        pending[group] = (names, handle, [grads[n][0] for n in names])
        last_token[:] = [token]
        return [token]

    small = {n: (weights[n] if weights[n].ndim > 1 else weights[n].reshape(1, -1)) for n in SMALL_NAMES}
    dx = _local_step(x[0], positions.reshape(S, 1), loss_target[0], small, shard, weights_of, grads_ready)

    grad, delta, new_m, new_v = {}, {}, {}, {}

    def finish(groups, after):
        names, partials = [], []
        for group in groups:
            group_names, handle, own32 = pending[group]
            _, received = _exchange_wait(REDUCE, handle, after, "reduce_wait_" + group)
            names += group_names
            partials += [_chip_partial(g32, rec, shard, "chip_partial_" + n)
                         for n, g32, rec in zip(group_names, own32, received)]
        siblings = _swap_with_sibling(partials, "swap_" + groups[0])
        for name, mine, theirs in zip(names, partials, siblings):
            if name == "w_in":
                mine, theirs = mine.T, theirs.T
            grad[name], delta[name], new_m[name], new_v[name] = _adamw(
                weights[name], moments_m[name], moments_v[name], [mine, theirs], "adamw_" + name)
        return [grad[names[-1]]]

    after = finish(["ffn2"], last_token)
    after = finish(["mixer"], after)
    (own_slab,), (slab_landing,) = _exchange_wait(BROADCAST, pending["small"], after, "reduce_wait_small")
    device = (2 * shard + lax.axis_index("c")).astype(jnp.int32)
    small_like = dict({n: small[n] for n in SMALL_NAMES}, loss=jnp.zeros((8, _LANES), F32))
    slabs = _adamw(_to_slab(small), _to_slab({n: moments_m[n] for n in SMALL_NAMES}),
                   _to_slab({n: moments_v[n] for n in SMALL_NAMES}), [own_slab, slab_landing], "adamw_small",
                   slot=device)
    for store, slab in zip((grad, delta, new_m, new_v), slabs):
        store.update(_from_slab(slab, small_like))
    loss = grad["loss"][0, 0]
    after = finish(["ffn1_gu"], [slabs[0]])
    finish(["ffn1_down"], after)

    def out(store):
        return [store[n].reshape(given[n].shape) for n in WEIGHT_ORDER]
    return (loss, dx[None], *out(grad), *out(delta), *out(new_m), *out(new_v))
```

```python
import functools

import jax
import jax.numpy as jnp
from jax import lax
from jax.experimental import pallas as pl
from jax.experimental.pallas import tpu as pltpu

F32 = jnp.float32
BF16 = jnp.bfloat16

D_MODEL = 1024
D_FF = 2816
N_SHARD = 4
FF_CHUNK = D_FF // 2
POOL_WINDOWS = (2, 4, 8, 16)
POOL_WIDTH = 512
POOL_GROUP = 128
HALO = 16
HEAD_DIM = 64
N_HEADS = 8
N_KV_HEADS = 2
ATTN_WIDTH = 512
KV_WIDTH = 128
IN_WIDTH = 1280
BLOCK = 128
ATTN_BLOCKS = 4
ROT_DIM = 16
ROPE_THETA = 500000.0
EPS = 1e-6
NEG_INF = -1e30
ATTN_SCALE = HEAD_DIM ** -0.5

ADAM_LR = 0.001
ADAM_B1 = 0.9
ADAM_B2 = 0.999
ADAM_EPS = 1e-08
ADAM_WD = 0.01
ADAM_STEP = 10

VMEM_LIMIT = 60 * 1024 * 1024
FFN_FWD_TILE = 512
FFN_BWD_TILE = 256
MIXER_TILE = 512

MESH = pl.DeviceIdType.MESH


def _params(**kw):
    return pltpu.CompilerParams(vmem_limit_bytes=VMEM_LIMIT, **kw)


def _dot(a, b):
    return jnp.dot(a, b, preferred_element_type=F32)


def _dot_nt(a, b):
    return lax.dot_general(a, b, (((1,), (1,)), ((), ())), preferred_element_type=F32)


def _dot_tn(a, b):
    return lax.dot_general(a, b, (((0,), (0,)), ((), ())), preferred_element_type=F32)


def _rstd(x):
    return lax.rsqrt(jnp.mean(x * x, axis=-1, keepdims=True) + EPS)


def _norm_bwd(dy, xn, r, gain):
    dxn = dy * gain
    return r * (dxn - xn * jnp.mean(dxn * xn, axis=-1, keepdims=True))


def _sigmoid(x):
    return 1.0 / (1.0 + jnp.exp(-x))


def _full(shape):
    return pl.BlockSpec(shape, lambda *_: (0,) * len(shape))


def _rows(tile, width, col=0):
    return pl.BlockSpec((tile, width), lambda i: (i, col))


_ANY = pl.BlockSpec(memory_space=pl.ANY)


_SMEM = pl.BlockSpec(memory_space=pltpu.SMEM)


def _load_once(pairs, sem):
    @pl.when(pl.program_id(0) == 0)
    def _():
        copies = [pltpu.make_async_copy(src, dst, sem.at[n]) for n, (src, dst) in enumerate(pairs)]
        for cp in copies:
            cp.start()
        for cp in copies:
            cp.wait()


def _gathered(land_ref, own_ref, vmem_ref, mine, rows=None):
    def dst(slot):
        if rows is None:
            return vmem_ref.at[slot]
        return vmem_ref.at[pl.ds(pl.multiple_of(slot * rows, 16), rows), :]
    pairs = [(land_ref.at[(mine + d) % N_SHARD], dst((mine + d) % N_SHARD)) for d in range(1, N_SHARD)]
    return pairs + [(own_ref, dst(mine))]


def _ignoring(body, start, count):
    def wrapped(*refs):
        return body(*refs[:start], *refs[start + count:])
    return wrapped


def _ffn_fwd(x, pre, post, wgu, wd, mine, target=None, deps=()):
    S = x.shape[0]
    tm = FFN_FWD_TILE
    with_loss = target is not None

    def body(*refs):
        if with_loss:
            (x_ref, pre_ref, post_ref, wgu_land, wgu_own, wd_land, wd_own, mine_ref, tgt_ref,
             out_ref, g_ref, u_ref, f_ref, loss_ref, wgu_v, wd_v, sem) = refs
        else:
            (x_ref, pre_ref, post_ref, wgu_land, wgu_own, wd_land, wd_own, mine_ref,
             out_ref, g_ref, u_ref, f_ref, wgu_v, wd_v, sem) = refs
        _load_once(_gathered(wgu_land, wgu_own, wgu_v, mine_ref[0])
                   + _gathered(wd_land, wd_own, wd_v, mine_ref[0], rows=D_FF // N_SHARD), sem)
        xv = x_ref[...]
        h = ((xv * _rstd(xv)) * pre_ref[...]).astype(BF16)
        facc = jnp.zeros((tm, D_MODEL), F32)
        for c in range(2):
            cols = slice(c * FF_CHUNK, (c + 1) * FF_CHUNK)
            g = _dot(h, wgu_v[c])
            u = _dot(h, wgu_v[2 + c])
            g_ref[:, cols] = g.astype(BF16)
            u_ref[:, cols] = u.astype(BF16)
            a = (g * _sigmoid(g)) * u
            facc = facc + _dot(a.astype(BF16), wd_v[cols, :])
        f_ref[...] = facc
        out = xv + 0.5 * ((facc * _rstd(facc)) * post_ref[...])
        if with_loss:
            diff = out - tgt_ref[...]
            out_ref[...] = diff * (1.0 / D_MODEL)

            @pl.when(pl.program_id(0) == 0)
            def _():
                loss_ref[...] = jnp.zeros_like(loss_ref)
            loss_ref[...] += jnp.sum(diff * diff)
        else:
            out_ref[...] = out

    in_specs = [_rows(tm, D_MODEL), _full((1, D_MODEL)), _full((1, D_MODEL)), _ANY, _ANY, _ANY, _ANY, _SMEM]
    args = [x, pre, post, *wgu, *wd, mine]
    out_shape = [jax.ShapeDtypeStruct((S, D_MODEL), F32), jax.ShapeDtypeStruct((S, D_FF), BF16),
                 jax.ShapeDtypeStruct((S, D_FF), BF16), jax.ShapeDtypeStruct((S, D_MODEL), F32)]
    out_specs = [_rows(tm, D_MODEL), _rows(tm, D_FF), _rows(tm, D_FF), _rows(tm, D_MODEL)]
    if with_loss:
        in_specs.append(_rows(tm, D_MODEL))
        args.append(target)
        out_shape.append(jax.ShapeDtypeStruct((8, 128), F32))
        out_specs.append(_full((8, 128)))
    return pl.pallas_call(
        _ignoring(body, len(args), len(deps)), name="ffn_fwd_loss" if with_loss else "ffn_fwd",
        grid=(S // tm,), in_specs=in_specs + [_ANY] * len(deps), out_specs=out_specs, out_shape=out_shape,
        scratch_shapes=[pltpu.VMEM((N_SHARD, D_MODEL, FF_CHUNK), BF16), pltpu.VMEM((D_FF, D_MODEL), BF16),
                        pltpu.SemaphoreType.DMA((2 * N_SHARD,))],
        compiler_params=_params(dimension_semantics=("arbitrary",)),
    )(*args, *deps)


def _ffn_bwd(dout, x, f, g, u, pre, post, wgu, wd, mine, deps=()):
    S = x.shape[0]
    tm = FFN_BWD_TILE

    def body(dout_ref, x_ref, f_ref, g_ref, u_ref, pre_ref, post_ref, wgu_land, wgu_own, wd_land, wd_own, mine_ref,
             dx_ref, h_ref, dgu_ref, a_ref, df_ref, dpre_ref, dpost_ref, wgu_v, wd_v, sem):
        _load_once(_gathered(wgu_land, wgu_own, wgu_v, mine_ref[0])
                   + _gathered(wd_land, wd_own, wd_v, mine_ref[0], rows=D_FF // N_SHARD), sem)

        @pl.when(pl.program_id(0) == 0)
        def _():
            dpre_ref[...] = jnp.zeros_like(dpre_ref)
            dpost_ref[...] = jnp.zeros_like(dpost_ref)

        dout_v = dout_ref[...]
        dn = 0.5 * dout_v
        fv = f_ref[...]
        rf = _rstd(fv)
        fn = fv * rf
        dpost_ref[...] += jnp.sum(dn * fn, axis=0, keepdims=True)
        df = _norm_bwd(dn, fn, rf, post_ref[...]).astype(BF16)
        df_ref[...] = df
        dh = jnp.zeros((tm, D_MODEL), F32)
        for c in range(2):
            cols = slice(c * FF_CHUNK, (c + 1) * FF_CHUNK)
            da = _dot_nt(df, wd_v[cols, :])
            gv = g_ref[:, cols].astype(F32)
            uv = u_ref[:, cols].astype(F32)
            sg = _sigmoid(gv)
            silu = gv * sg
            a_ref[:, cols] = (silu * uv).astype(BF16)
            dg = ((da * uv) * (sg * (1.0 + gv * (1.0 - sg)))).astype(BF16)
            du = (da * silu).astype(BF16)
            dgu_ref[:, cols] = dg
            dgu_ref[:, 2 * FF_CHUNK + c * FF_CHUNK:2 * FF_CHUNK + (c + 1) * FF_CHUNK] = du
            dh = dh + _dot_nt(dg, wgu_v[c]) + _dot_nt(du, wgu_v[2 + c])
        xv = x_ref[...]
        rx = _rstd(xv)
        xn = xv * rx
        h_ref[...] = (xn * pre_ref[...]).astype(BF16)
        dpre_ref[...] += jnp.sum(dh * xn, axis=0, keepdims=True)
        dx_ref[...] = dout_v + _norm_bwd(dh, xn, rx, pre_ref[...])

    args = [dout, x, f, g, u, pre, post, *wgu, *wd, mine]
    return pl.pallas_call(
        _ignoring(body, len(args), len(deps)), name="ffn_bwd", grid=(S // tm,),
        in_specs=[_rows(tm, D_MODEL), _rows(tm, D_MODEL), _rows(tm, D_MODEL), _rows(tm, D_FF), _rows(tm, D_FF),
                  _full((1, D_MODEL)), _full((1, D_MODEL)), _ANY, _ANY, _ANY, _ANY, _SMEM] + [_ANY] * len(deps),
        out_specs=[_rows(tm, D_MODEL), _rows(tm, D_MODEL), _rows(tm, 2 * D_FF), _rows(tm, D_FF), _rows(tm, D_MODEL),
                   _full((1, D_MODEL)), _full((1, D_MODEL))],
        out_shape=[jax.ShapeDtypeStruct((S, D_MODEL), F32), jax.ShapeDtypeStruct((S, D_MODEL), BF16),
                   jax.ShapeDtypeStruct((S, 2 * D_FF), BF16), jax.ShapeDtypeStruct((S, D_FF), BF16),
                   jax.ShapeDtypeStruct((S, D_MODEL), BF16),
                   jax.ShapeDtypeStruct((1, D_MODEL), F32), jax.ShapeDtypeStruct((1, D_MODEL), F32)],
        scratch_shapes=[pltpu.VMEM((N_SHARD, D_MODEL, FF_CHUNK), BF16), pltpu.VMEM((D_FF, D_MODEL), BF16),
                        pltpu.SemaphoreType.DMA((2 * N_SHARD,))],
        compiler_params=_params(dimension_semantics=("arbitrary",)),
    )(*args, *deps)


def _wgrad(lhs, rhs, m_block, n_block, name, column_shards=False, tk=2048, deps=()):
    S, M = lhs.shape
    N = rhs.shape[1]
    k_steps = S // tk

    def body(lhs_ref, rhs_ref, out_ref, out16_ref):
        k = pl.program_id(2)

        @pl.when(k == 0)
        def _():
            out_ref[...] = jnp.zeros_like(out_ref)
        out_ref[...] += _dot_tn(lhs_ref[...], rhs_ref[...])

        @pl.when(k == k_steps - 1)
        def _():
            out16_ref[...] = out_ref[...].astype(BF16)

    if column_shards:
        assert N == N_SHARD * n_block
        shape = (N_SHARD, M, n_block)
        out_spec = pl.BlockSpec((None, m_block, n_block), lambda i, j, k: (j, i, 0))
    else:
        shape = (M, N)
        out_spec = pl.BlockSpec((m_block, n_block), lambda i, j, k: (i, j))
    out, out16 = pl.pallas_call(
        _ignoring(body, 2, len(deps)), name=name, grid=(M // m_block, N // n_block, k_steps),
        in_specs=[pl.BlockSpec((tk, m_block), lambda i, j, k: (k, i)),
                  pl.BlockSpec((tk, n_block), lambda i, j, k: (k, j))] + [_ANY] * len(deps),
        out_specs=[out_spec, out_spec],
        out_shape=[jax.ShapeDtypeStruct(shape, F32), jax.ShapeDtypeStruct(shape, BF16)],
        compiler_params=_params(dimension_semantics=("arbitrary", "arbitrary", "arbitrary")),
    )(lhs, rhs, *deps)
    if not column_shards:
        out = out.reshape(N_SHARD, M // N_SHARD, N)
        out16 = out16.reshape(N_SHARD, M // N_SHARD, N)
    return out, out16


def _rope_tables(pos, invf):
    S = pos.shape[0]
    tm = MIXER_TILE

    def body(pos_ref, invf_ref, out_ref):
        ang = pos_ref[...].astype(F32) * invf_ref[...]
        cos, sin = jnp.cos(ang), jnp.sin(ang)
        lane = lax.broadcasted_iota(jnp.int32, ang.shape, 1) % HEAD_DIM
        first = lane < ROT_DIM // 2
        second = (lane >= ROT_DIM // 2) & (lane < ROT_DIM)
        out_ref[0] = jnp.where(lane < ROT_DIM, cos, 1.0)
        out_ref[1] = jnp.where(first, sin, 0.0)
        out_ref[2] = jnp.where(second, sin, 0.0)

    return pl.pallas_call(
        body, name="rope_tables", grid=(S // tm,),
        in_specs=[_rows(tm, 1), _full((1, _LANES))],
        out_specs=pl.BlockSpec((3, tm, _LANES), lambda i: (0, i, 0)),
        out_shape=jax.ShapeDtypeStruct((3, S, _LANES), F32),
        compiler_params=_params(dimension_semantics=("arbitrary",)),
    )(pos, invf)


def _table_spec(tm):
    return pl.BlockSpec((3, tm, _LANES), lambda i: (0, i, 0))


_HALF = ROT_DIM // 2
_LANES = 128


def _rope(t, tables):
    c, s_first, s_second = tables
    return t * c - pltpu.roll(t, _LANES - _HALF, axis=1) * s_first + pltpu.roll(t, _HALF, axis=1) * s_second


def _rope_transposed(t, tables):
    c, s_first, s_second = tables
    return t * c - pltpu.roll(t * s_first, _HALF, axis=1) + pltpu.roll(t * s_second, _LANES - _HALF, axis=1)


def _store_head_variants(ref, t):
    rolled = pltpu.roll(t, HEAD_DIM, axis=1)
    low = lax.broadcasted_iota(jnp.int32, t.shape, 1) < HEAD_DIM
    zero = jnp.zeros_like(t)
    ref[0] = jnp.where(low, t, zero).astype(BF16)
    ref[1] = jnp.where(low, zero, rolled).astype(BF16)
    ref[2] = jnp.where(low, rolled, zero).astype(BF16)
    ref[3] = jnp.where(low, zero, t).astype(BF16)


def _mixer_in_fwd(x, pre, w_in_t, mine, rope, deps=()):
    S = x.shape[0]
    tm = MIXER_TILE

    def body(x_ref, pre_ref, w_land, w_own, mine_ref, rope_ref, u_ref, q_ref, k_ref, v_ref, w_v, sem):
        _load_once(_gathered(w_land, w_own, w_v, mine_ref[0], rows=IN_WIDTH // N_SHARD), sem)
        xv = x_ref[...]
        h = ((xv * _rstd(xv)) * pre_ref[...]).astype(BF16)
        z = _dot_nt(h, w_v[...])
        tables = (rope_ref[0], rope_ref[1], rope_ref[2])
        u_ref[...] = z[:, :POOL_WIDTH]
        for t in range(ATTN_WIDTH // _LANES):
            lo = POOL_WIDTH + t * _LANES
            q_ref[:, t * _LANES:(t + 1) * _LANES] = (_rope(z[:, lo:lo + _LANES], tables) * ATTN_SCALE).astype(BF16)
        kv = POOL_WIDTH + ATTN_WIDTH
        _store_head_variants(k_ref, _rope(z[:, kv:kv + KV_WIDTH], tables))
        _store_head_variants(v_ref, z[:, kv + KV_WIDTH:])

    args = [x, pre, *w_in_t, mine, rope]
    variants = pl.BlockSpec((2 * N_KV_HEADS, tm, KV_WIDTH), lambda i: (0, i, 0))
    return pl.pallas_call(
        _ignoring(body, len(args), len(deps)), name="mixer_in_fwd", grid=(S // tm,),
        in_specs=[_rows(tm, D_MODEL), _full((1, D_MODEL)), _ANY, _ANY, _SMEM, _table_spec(tm)] + [_ANY] * len(deps),
        out_specs=[_rows(tm, POOL_WIDTH), _rows(tm, ATTN_WIDTH), variants, variants],
        out_shape=[jax.ShapeDtypeStruct((S, POOL_WIDTH), F32), jax.ShapeDtypeStruct((S, ATTN_WIDTH), BF16),
                   jax.ShapeDtypeStruct((2 * N_KV_HEADS, S, KV_WIDTH), BF16),
                   jax.ShapeDtypeStruct((2 * N_KV_HEADS, S, KV_WIDTH), BF16)],
        scratch_shapes=[pltpu.VMEM((IN_WIDTH, D_MODEL), BF16), pltpu.SemaphoreType.DMA((N_SHARD,))],
        compiler_params=_params(dimension_semantics=("arbitrary",)),
    )(*args, *deps)


def _mixer_in_bwd(dres, x, pre, w_in_t, mine, du, dq, dk, dv, dk_next, dv_next, rope, deps=()):
    S = x.shape[0]
    tm = MIXER_TILE

    def body(dres_ref, x_ref, pre_ref, w_land, w_own, mine_ref, du_ref, dq_ref, dk_ref, dv_ref, dkx_ref, dvx_ref,
             rope_ref,
             dx_ref, dz_ref, h_ref, dpre_ref, w_v, sem):
        _load_once(_gathered(w_land, w_own, w_v, mine_ref[0], rows=IN_WIDTH // N_SHARD), sem)

        @pl.when(pl.program_id(0) == 0)
        def _():
            dpre_ref[...] = jnp.zeros_like(dpre_ref)

        tables = (rope_ref[0], rope_ref[1], rope_ref[2])
        dz_ref[:, :POOL_WIDTH] = du_ref[...].astype(BF16)
        for t in range(ATTN_WIDTH // _LANES):
            lo = POOL_WIDTH + t * _LANES
            dz_ref[:, lo:lo + _LANES] = _rope_transposed(dq_ref[:, t * _LANES:(t + 1) * _LANES], tables).astype(BF16)
        kv = POOL_WIDTH + ATTN_WIDTH
        has_next = pl.program_id(0) + 1 < steps
        pad = jnp.zeros((tm - BLOCK, KV_WIDTH), F32)
        dk_tile = dk_ref[...] + jnp.concatenate([pad, jnp.where(has_next, dkx_ref[...], 0.0)], axis=0)
        dv_tile = dv_ref[...] + jnp.concatenate([pad, jnp.where(has_next, dvx_ref[...], 0.0)], axis=0)
        dz_ref[:, kv:kv + KV_WIDTH] = _rope_transposed(dk_tile, tables).astype(BF16)
        dz_ref[:, kv + KV_WIDTH:] = dv_tile.astype(BF16)
        dh = _dot(dz_ref[...], w_v[...])
        xv = x_ref[...]
        rx = _rstd(xv)
        xn = xv * rx
        h_ref[...] = (xn * pre_ref[...]).astype(BF16)
        dpre_ref[...] += jnp.sum(dh * xn, axis=0, keepdims=True)
        dx_ref[...] = dres_ref[...] + _norm_bwd(dh, xn, rx, pre_ref[...])

    assert tm == ATTN_BLOCKS * BLOCK
    steps = S // tm
    nxt = pl.BlockSpec((None, BLOCK, KV_WIDTH), lambda i: (jnp.minimum(i + 1, steps - 1), 0, 0))
    args = [dres, x, pre, *w_in_t, mine, du, dq, dk, dv, dk_next, dv_next, rope]
    return pl.pallas_call(
        _ignoring(body, len(args), len(deps)), name="mixer_in_bwd", grid=(S // tm,),
        in_specs=[_rows(tm, D_MODEL), _rows(tm, D_MODEL), _full((1, D_MODEL)), _ANY, _ANY, _SMEM,
                  _rows(tm, POOL_WIDTH), _rows(tm, ATTN_WIDTH), _rows(tm, KV_WIDTH), _rows(tm, KV_WIDTH), nxt, nxt,
                  _table_spec(tm)] + [_ANY] * len(deps),
        out_specs=[_rows(tm, D_MODEL), _rows(tm, IN_WIDTH), _rows(tm, D_MODEL), _full((1, D_MODEL))],
        out_shape=[jax.ShapeDtypeStruct((S, D_MODEL), F32), jax.ShapeDtypeStruct((S, IN_WIDTH), BF16),
                   jax.ShapeDtypeStruct((S, D_MODEL), BF16), jax.ShapeDtypeStruct((1, D_MODEL), F32)],
        scratch_shapes=[pltpu.VMEM((IN_WIDTH, D_MODEL), BF16), pltpu.SemaphoreType.DMA((N_SHARD,))],
        compiler_params=_params(dimension_semantics=("arbitrary",)),
    )(*args, *deps)


def _pool_counts(tile_index, tm, width):
    t = tile_index * tm + lax.broadcasted_iota(jnp.int32, (tm, 1), 0)
    return jnp.minimum(t + 1, width).astype(F32)


def _pool_features(ext, u_tile, tile_index, tm):
    ds = []
    for gi, width in enumerate(POOL_WINDOWS):
        lanes = slice(gi * POOL_GROUP, (gi + 1) * POOL_GROUP)
        s = ext[:, lanes]
        shift = 1
        while shift < width:
            s = s + pltpu.roll(s, shift, axis=0)
            shift *= 2
        ds.append(s[HALO:, :] / _pool_counts(tile_index, tm, width) - u_tile[:, lanes])
    return ds


def _pool_fwd(u, w_pool, pool_scale, g_pool):
    S = u.shape[0]
    tm = MIXER_TILE

    def body(u_ref, w_ref, scale_ref, gain_ref, y_ref, ext_ref):
        i = pl.program_id(0)

        @pl.when(i == 0)
        def _():
            ext_ref[:HALO, :] = jnp.zeros((HALO, POOL_WIDTH), F32)

        u_tile = u_ref[...]
        ext_ref[HALO:, :] = u_tile
        ds = _pool_features(ext_ref[...], u_tile, i, tm)
        ext_ref[:HALO, :] = u_tile[tm - HALO:, :]
        ys = [_dot(ds[gi].astype(BF16), w_ref[gi].astype(BF16)) for gi in range(len(POOL_WINDOWS))]
        po = jnp.concatenate(ys, axis=1) * scale_ref[...]
        y_ref[...] = ((po * _rstd(po)) * gain_ref[...]).astype(BF16)

    return pl.pallas_call(
        body, name="pool_fwd", grid=(S // tm,),
        in_specs=[_rows(tm, POOL_WIDTH), _full((len(POOL_WINDOWS), POOL_GROUP, POOL_GROUP)),
                  _full((1, POOL_WIDTH)), _full((1, POOL_WIDTH))],
        out_specs=_rows(tm, POOL_WIDTH),
        out_shape=jax.ShapeDtypeStruct((S, POOL_WIDTH), BF16),
        scratch_shapes=[pltpu.VMEM((HALO + tm, POOL_WIDTH), F32)],
        compiler_params=_params(dimension_semantics=("arbitrary",)),
    )(u, w_pool, pool_scale, g_pool)


def _pool_bwd(dy, u, w_pool, pool_scale, g_pool):
    S = u.shape[0]
    tm = MIXER_TILE
    n_tiles = S // tm
    halo_blocks = tm // HALO

    def body(dy_ref, u_ref, uprev_ref, w_ref, scale_ref, gain_ref,
             du_ref, dw_ref, dscale_ref, dgain_ref, ext_ref, nxt_ref):
        i = pl.program_id(0)
        tile = n_tiles - 1 - i

        @pl.when(i == 0)
        def _():
            dw_ref[...] = jnp.zeros_like(dw_ref)
            dscale_ref[...] = jnp.zeros_like(dscale_ref)
            dgain_ref[...] = jnp.zeros_like(dgain_ref)
            nxt_ref[...] = jnp.zeros_like(nxt_ref)

        u_tile = u_ref[...]
        ext_ref[:HALO, :] = jnp.where(tile > 0, uprev_ref[...], 0.0)
        ext_ref[HALO:, :] = u_tile
        ds = _pool_features(ext_ref[...], u_tile, tile, tm)
        dsb = [d.astype(BF16) for d in ds]
        wb = [w_ref[gi].astype(BF16) for gi in range(len(POOL_WINDOWS))]
        yraw = jnp.concatenate([_dot(dsb[gi], wb[gi]) for gi in range(len(POOL_WINDOWS))], axis=1)
        po = yraw * scale_ref[...]
        r = _rstd(po)
        pn = po * r
        dyv = dy_ref[...]
        dgain_ref[...] += jnp.sum(dyv * pn, axis=0, keepdims=True)
        dpo = _norm_bwd(dyv, pn, r, gain_ref[...])
        dscale_ref[...] += jnp.sum(dpo * yraw, axis=0, keepdims=True)
        dyraw = (dpo * scale_ref[...]).astype(BF16)
        for gi, width in enumerate(POOL_WINDOWS):
            lanes = slice(gi * POOL_GROUP, (gi + 1) * POOL_GROUP)
            dw_ref[gi] += _dot_tn(dsb[gi], dyraw[:, lanes])
            dd = _dot_nt(dyraw[:, lanes], wb[gi])
            ddc = dd / _pool_counts(tile, tm, width)
            ext_ref[:tm, lanes] = ddc
            ext_ref[tm:, lanes] = nxt_ref[:, lanes]
            s = ext_ref[:, lanes]
            shift = 1
            while shift < width:
                s = s + pltpu.roll(s, HALO + tm - shift, axis=0)
                shift *= 2
            du_ref[:, lanes] = s[:tm, :] - dd
            nxt_ref[:, lanes] = ddc[:HALO, :]

    return pl.pallas_call(
        body, name="pool_bwd", grid=(n_tiles,),
        in_specs=[pl.BlockSpec((tm, POOL_WIDTH), lambda i: (n_tiles - 1 - i, 0)),
                  pl.BlockSpec((tm, POOL_WIDTH), lambda i: (n_tiles - 1 - i, 0)),
                  pl.BlockSpec((HALO, POOL_WIDTH), lambda i: (jnp.maximum((n_tiles - 1 - i) * halo_blocks - 1, 0), 0)),
                  _full((len(POOL_WINDOWS), POOL_GROUP, POOL_GROUP)), _full((1, POOL_WIDTH)), _full((1, POOL_WIDTH))],
        out_specs=[pl.BlockSpec((tm, POOL_WIDTH), lambda i: (n_tiles - 1 - i, 0)),
                   _full((len(POOL_WINDOWS), POOL_GROUP, POOL_GROUP)), _full((1, POOL_WIDTH)), _full((1, POOL_WIDTH))],
        out_shape=[jax.ShapeDtypeStruct((S, POOL_WIDTH), F32),
                   jax.ShapeDtypeStruct((len(POOL_WINDOWS), POOL_GROUP, POOL_GROUP), F32),
                   jax.ShapeDtypeStruct((1, POOL_WIDTH), F32), jax.ShapeDtypeStruct((1, POOL_WIDTH), F32)],
        scratch_shapes=[pltpu.VMEM((HALO + tm, POOL_WIDTH), F32), pltpu.VMEM((HALO, POOL_WIDTH), F32)],
        compiler_params=_params(dimension_semantics=("arbitrary",)),
    )(dy, u, u, w_pool, pool_scale, g_pool)


def _variant(head):
    return 2 * (head // (N_HEADS // N_KV_HEADS)) + head % 2


def _own_block(shape=(BLOCK, BLOCK)):
    r = lax.broadcasted_iota(jnp.int32, shape, 0)
    i = lax.broadcasted_iota(jnp.int32, shape, 1)
    return r <= i


def _fold_band(own, from_own, from_prev):
    return jnp.where(own, from_own, from_prev)


def _scores_by_head(own_tiles, prev_tiles, q_tiles):
    stacks = [jnp.concatenate(q_tiles[:2], axis=0), jnp.concatenate(q_tiles[2:], axis=0)]
    by_var = [_dot_nt(jnp.concatenate([own_tiles[v], prev_tiles[v]], axis=0), stacks[v // 2])
              for v in range(2 * N_KV_HEADS)]
    quadrant = lambda h, rows: by_var[_variant(h)][rows * BLOCK:(rows + 1) * BLOCK,
                                                   ((h // 2) % 2) * BLOCK:((h // 2) % 2 + 1) * BLOCK]
    return [quadrant(h, 0) for h in range(N_HEADS)], [quadrant(h, 1) for h in range(N_HEADS)]


def _softmax_t(s, sink):
    m = jnp.maximum(jnp.max(s, axis=0, keepdims=True), sink)
    p = jnp.exp(s - m)
    p_sink = jnp.exp(sink - m)
    inv = 1.0 / (jnp.sum(p, axis=0, keepdims=True) + p_sink)
    return p * inv, p_sink * inv


def _attn_fwd(q, kz, vz, sinks, g_attn, y_pool, x, w_out, mine, post):
    S = q.shape[0]
    tq = ATTN_BLOCKS * BLOCK
    n_var = 2 * N_KV_HEADS

    def body(q_ref, kp_ref, kc_ref, vp_ref, vc_ref, sinks_ref, gain_ref, yp_ref, x_ref, w_land, w_own, mine_ref,
             post_ref, o_ref, out_ref, m_ref, y_ref, w_v, sem):
        _load_once(_gathered(w_land, w_own, w_v, mine_ref[0], rows=D_MODEL // N_SHARD), sem)
        step = pl.program_id(0)
        own = _own_block()
        zero = jnp.zeros((BLOCK, BLOCK), F32)

        def tiles(cur_ref, prev_ref, j):
            rows = lambda jj: slice(jj * BLOCK, (jj + 1) * BLOCK)
            return ([cur_ref[v, rows(j), :] for v in range(n_var)],
                    [prev_ref[v] if j == 0 else cur_ref[v, rows(j - 1), :] for v in range(n_var)])

        scores = []
        for j in range(ATTN_BLOCKS):
            q_pairs = [q_ref[j * BLOCK:(j + 1) * BLOCK, i * _LANES:(i + 1) * _LANES] for i in range(N_HEADS // 2)]
            scores.append(_scores_by_head(*tiles(kc_ref, kp_ref, j), q_pairs))
        probs = []
        for j in range(ATTN_BLOCKS):
            s_own, s_prev = scores[j]
            no_prev = jnp.where(step > 0, 0.0, NEG_INF) if j == 0 else 0.0
            p_own, p_prev = [], []
            for h in range(N_HEADS):
                p, _ = _softmax_t(_fold_band(own, s_own[h], s_prev[h] + no_prev), sinks_ref[0, h])
                p_own.append(jnp.where(own, p, zero).astype(BF16))
                p_prev.append(jnp.where(own, zero, p).astype(BF16))
            probs.append((p_own, p_prev))
        blocks = []
        for j in range(ATTN_BLOCKS):
            p_own, p_prev = probs[j]
            v_own, v_prev = tiles(vc_ref, vp_ref, j)
            pairs = []
            for i in range(N_HEADS // 2):
                acc = None
                for h in (2 * i, 2 * i + 1):
                    part = _dot_tn(p_own[h], v_own[_variant(h)]) + _dot_tn(p_prev[h], v_prev[_variant(h)])
                    acc = part if acc is None else acc + part
                pairs.append(acc)
            blocks.append(jnp.concatenate(pairs, axis=1))
        o = jnp.concatenate(blocks, axis=0)
        o_ref[...] = o
        y_ref[:, :POOL_WIDTH] = yp_ref[...]
        y_ref[:, POOL_WIDTH:] = ((o * _rstd(o)) * gain_ref[...]).astype(BF16)
        m = _dot(y_ref[...], w_v[...])
        m_ref[...] = m
        out_ref[...] = x_ref[...] + (m * _rstd(m)) * post_ref[...]

    prev = pl.BlockSpec((n_var, BLOCK, KV_WIDTH), lambda g: (0, jnp.maximum(g * ATTN_BLOCKS - 1, 0), 0))
    cur = pl.BlockSpec((n_var, tq, KV_WIDTH), lambda g: (0, g, 0))
    return pl.pallas_call(
        body, name="attn_fwd", grid=(S // tq,),
        in_specs=[_rows(tq, ATTN_WIDTH), prev, cur, prev, cur,
                  pl.BlockSpec(memory_space=pltpu.SMEM), _full((1, ATTN_WIDTH)),
                  _rows(tq, POOL_WIDTH), _rows(tq, D_MODEL), _ANY, _ANY, _SMEM, _full((1, D_MODEL))],
        out_specs=[_rows(tq, ATTN_WIDTH), _rows(tq, D_MODEL), _rows(tq, D_MODEL), _rows(tq, D_MODEL)],
        out_shape=[jax.ShapeDtypeStruct((S, ATTN_WIDTH), F32), jax.ShapeDtypeStruct((S, D_MODEL), F32),
                   jax.ShapeDtypeStruct((S, D_MODEL), F32), jax.ShapeDtypeStruct((S, D_MODEL), BF16)],
        scratch_shapes=[pltpu.VMEM((D_MODEL, D_MODEL), BF16), pltpu.SemaphoreType.DMA((N_SHARD,))],
        compiler_params=_params(dimension_semantics=("arbitrary",)),
    )(q, kz, kz, vz, vz, sinks, g_attn, y_pool, x, *w_out, mine, post)


def _attn_bwd(dout, m, w_out, mine, post, o, q, kz, vz, sinks, g_attn, deps=()):
    S = q.shape[0]
    tq = ATTN_BLOCKS * BLOCK
    n_var = 2 * N_KV_HEADS

    def body(dout_ref, m_ref, w_land, w_own, mine_ref, post_ref, o_ref, q_ref, kp_ref, kc_ref, vp_ref, vc_ref,
             sinks_ref, gain_ref, dyp_ref, dm_ref, dpost_ref, dq_ref, dk_ref, dv_ref, dkx_ref, dvx_ref, dsink_ref,
             dgain_ref, w_v, sem):
        _load_once(_gathered(w_land, w_own, w_v, mine_ref[0], rows=D_MODEL // N_SHARD), sem)
        step = pl.program_id(0)

        @pl.when(step == 0)
        def _():
            dpost_ref[...] = jnp.zeros_like(dpost_ref)
            dsink_ref[...] = jnp.zeros_like(dsink_ref)
            dgain_ref[...] = jnp.zeros_like(dgain_ref)

        mv = m_ref[...]
        rm = _rstd(mv)
        mn = mv * rm
        dres = dout_ref[...]
        dpost_ref[...] += jnp.sum(dres * mn, axis=0, keepdims=True)
        dm = _norm_bwd(dres, mn, rm, post_ref[...]).astype(BF16)
        dm_ref[...] = dm
        dy = _dot_nt(dm, w_v[...])
        dyp_ref[...] = dy[:, :POOL_WIDTH]
        ov = o_ref[...]
        r = _rstd(ov)
        on = ov * r
        dyv = dy[:, POOL_WIDTH:]
        dgain_ref[...] += jnp.sum(dyv * on, axis=0, keepdims=True)
        do = _norm_bwd(dyv, on, r, gain_ref[...]).astype(BF16)
        own = _own_block()
        zero = jnp.zeros((BLOCK, BLOCK), F32)
        split = lambda t: (jnp.where(own, t, zero).astype(BF16), jnp.where(own, zero, t).astype(BF16))
        rows = lambda j: slice(j * BLOCK, (j + 1) * BLOCK)
        heads = range(N_HEADS)

        def tiles(cur_ref, prev_ref, j):
            return ([cur_ref[v, rows(j), :] for v in range(n_var)],
                    [prev_ref[v] if j == 0 else cur_ref[v, rows(j - 1), :] for v in range(n_var)])

        q_pairs = [[q_ref[rows(j), i * _LANES:(i + 1) * _LANES] for i in range(N_HEADS // 2)] for j in range(ATTN_BLOCKS)]
        do_pairs = [[do[rows(j), i * _LANES:(i + 1) * _LANES] for i in range(N_HEADS // 2)] for j in range(ATTN_BLOCKS)]
        scores = [(_scores_by_head(*tiles(kc_ref, kp_ref, j), q_pairs[j]),
                   _scores_by_head(*tiles(vc_ref, vp_ref, j), do_pairs[j])) for j in range(ATTN_BLOCKS)]
        parts, sink_sum = [], None
        for j in range(ATTN_BLOCKS):
            (s_own, s_prev), (dp_own, dp_prev) = scores[j]
            no_prev = jnp.where(step > 0, 0.0, NEG_INF) if j == 0 else 0.0
            ds_parts, p_parts, sink_rows = [], [], []
            for h in heads:
                p, p_sink = _softmax_t(_fold_band(own, s_own[h], s_prev[h] + no_prev), sinks_ref[0, h])
                dp = _fold_band(own, dp_own[h], dp_prev[h])
                delta = jnp.sum(p * dp, axis=0, keepdims=True)
                ds_parts.append(split(p * (dp - delta)))
                p_parts.append(split(p))
                sink_rows.append(jnp.zeros((1, _LANES), F32) - jnp.sum(p_sink * delta))
            block_sinks = jnp.concatenate(sink_rows, axis=0)
            sink_sum = block_sinks if sink_sum is None else sink_sum + block_sinks
            parts.append((ds_parts, p_parts))
        dsink_ref[...] += sink_sum
        low = lax.broadcasted_iota(jnp.int32, (BLOCK, _LANES), 1) < HEAD_DIM

        def merge(acc):
            return jnp.where(low, acc[0] + pltpu.roll(acc[1], HEAD_DIM, axis=1),
                             acc[3] + pltpu.roll(acc[2], HEAD_DIM, axis=1))
        add = lambda acc, var, t: acc.__setitem__(var, t if acc[var] is None else acc[var] + t)
        k_own, k_prev, v_own, v_prev = [], [], [], []
        for j in range(ATTN_BLOCKS):
            ds_parts, p_parts = parts[j]
            kt_own, kt_prev = tiles(kc_ref, kp_ref, j)
            dk_own, dk_prev, dv_own, dv_prev = ([None] * n_var for _ in range(4))
            for i in range(N_HEADS // 2):
                dq_pair = None
                for h in (2 * i, 2 * i + 1):
                    var = _variant(h)
                    (ds_o, ds_p), (p_o, p_p) = ds_parts[h], p_parts[h]
                    part = _dot_tn(ds_o, kt_own[var]) + _dot_tn(ds_p, kt_prev[var])
                    dq_pair = part if dq_pair is None else dq_pair + part
                    add(dk_own, var, _dot(ds_o, q_pairs[j][i]))
                    add(dk_prev, var, _dot(ds_p, q_pairs[j][i]))
                    add(dv_own, var, _dot(p_o, do_pairs[j][i]))
                    add(dv_prev, var, _dot(p_p, do_pairs[j][i]))
                dq_ref[rows(j), i * _LANES:(i + 1) * _LANES] = dq_pair * ATTN_SCALE
            k_own.append(merge(dk_own))
            k_prev.append(merge(dk_prev))
            v_own.append(merge(dv_own))
            v_prev.append(merge(dv_prev))
        for j in range(ATTN_BLOCKS):
            last = j == ATTN_BLOCKS - 1
            dk_ref[rows(j), :] = k_own[j] if last else k_own[j] + k_prev[j + 1]
            dv_ref[rows(j), :] = v_own[j] if last else v_own[j] + v_prev[j + 1]
        dkx_ref[...] = k_prev[0]
        dvx_ref[...] = v_prev[0]

    steps = S // tq
    prev = pl.BlockSpec((n_var, BLOCK, KV_WIDTH), lambda g: (0, jnp.maximum(g * ATTN_BLOCKS - 1, 0), 0))
    cur = pl.BlockSpec((n_var, tq, KV_WIDTH), lambda g: (0, g, 0))
    nxt = pl.BlockSpec((None, BLOCK, KV_WIDTH), lambda g: (g, 0, 0))
    args = [dout, m, *w_out, mine, post, o, q, kz, kz, vz, vz, sinks, g_attn]
    return pl.pallas_call(
        _ignoring(body, len(args), len(deps)), name="attn_bwd", grid=(steps,),
        in_specs=[_rows(tq, D_MODEL), _rows(tq, D_MODEL), _ANY, _ANY, _SMEM, _full((1, D_MODEL)),
                  _rows(tq, ATTN_WIDTH), _rows(tq, ATTN_WIDTH), prev, cur, prev, cur,
                  pl.BlockSpec(memory_space=pltpu.SMEM), _full((1, ATTN_WIDTH))] + [_ANY] * len(deps),
        out_specs=[_rows(tq, POOL_WIDTH), _rows(tq, D_MODEL), _full((1, D_MODEL)),
                   _rows(tq, ATTN_WIDTH), _rows(tq, KV_WIDTH), _rows(tq, KV_WIDTH), nxt, nxt,
                   _full((N_HEADS, _LANES)), _full((1, ATTN_WIDTH))],
        out_shape=[jax.ShapeDtypeStruct((S, POOL_WIDTH), F32), jax.ShapeDtypeStruct((S, D_MODEL), BF16),
                   jax.ShapeDtypeStruct((1, D_MODEL), F32),
                   jax.ShapeDtypeStruct((S, ATTN_WIDTH), F32), jax.ShapeDtypeStruct((S, KV_WIDTH), F32),
                   jax.ShapeDtypeStruct((S, KV_WIDTH), F32),
                   jax.ShapeDtypeStruct((steps, BLOCK, KV_WIDTH), F32), jax.ShapeDtypeStruct((steps, BLOCK, KV_WIDTH), F32),
                   jax.ShapeDtypeStruct((N_HEADS, _LANES), F32), jax.ShapeDtypeStruct((1, ATTN_WIDTH), F32)],
        scratch_shapes=[pltpu.VMEM((D_MODEL, D_MODEL), BF16), pltpu.SemaphoreType.DMA((N_SHARD,))],
        compiler_params=_params(dimension_semantics=("arbitrary",)),
    )(*args, *deps)


def _inv_freq_row():
    inv_freq = ROPE_THETA ** (-jnp.arange(0, ROT_DIM, 2, dtype=F32) / ROT_DIM)
    per_head = jnp.concatenate([inv_freq, inv_freq, jnp.zeros((HEAD_DIM - ROT_DIM,), F32)])
    return jnp.tile(per_head, _LANES // HEAD_DIM).reshape(1, _LANES)


def _local_step(x, pos, target, small, mine, weights_of, grads_ready):
    rope = _rope_tables(pos, _inv_freq_row())
    wgu1, wd1 = weights_of("ffn1", (rope,))
    x1, g1, u1, f1 = _ffn_fwd(x, small["ffn1_pre"], small["ffn1_post"], wgu1, wd1, mine)
    w_in_t, w_out = weights_of("mixer", (x1,))
    u, q, k, v = _mixer_in_fwd(x1, small["mix_pre"], w_in_t, mine, rope)
    y_pool = _pool_fwd(u, small["w_pool"], small["pool_scale"], small["g_pool"])
    o, x2, m, y = _attn_fwd(q, k, v, small["sinks"], small["g_attn"], y_pool, x1, w_out, mine, small["mix_post"])
    wgu2, wd2 = weights_of("ffn2", (x2,))
    dx3, g2, u2, f2, loss_acc = _ffn_fwd(x2, small["ffn2_pre"], small["ffn2_post"], wgu2, wd2, mine, target=target)
    grads = {"loss": loss_acc * (0.5 / D_MODEL)}
    dx2, h3, dgu2, a2, df2, grads["ffn2_pre"], grads["ffn2_post"] = _ffn_bwd(
        dx3, x2, f2, g2, u2, small["ffn2_pre"], small["ffn2_post"], wgu2, wd2, mine)
    dwgu2 = _wgrad(h3, dgu2, D_MODEL, FF_CHUNK, "wgrad_gu2", column_shards=True)
    dwd2 = _wgrad(a2, df2, FF_CHUNK, D_MODEL, "wgrad_down2")
    deps = grads_ready("ffn2", {"ffn2_w_gu": dwgu2, "ffn2_w_down": dwd2})
    dy_pool, dm, grads["mix_post"], dq, dk, dv, dk_next, dv_next, dsinks, grads["g_attn"] = _attn_bwd(
        dx2, m, w_out, mine, small["mix_post"], o, q, k, v, small["sinks"], small["g_attn"], deps=deps)
    dw_out = _wgrad(y, dm, D_MODEL, D_MODEL, "wgrad_out")
    grads["sinks"] = dsinks[:, 0].reshape(1, N_HEADS)
    du, grads["w_pool"], grads["pool_scale"], grads["g_pool"] = _pool_bwd(
        dy_pool, u, small["w_pool"], small["pool_scale"], small["g_pool"])
    dx1, dz, h2, grads["mix_pre"] = _mixer_in_bwd(dx2, x1, small["mix_pre"], w_in_t, mine, du, dq, dk, dv, dk_next, dv_next, rope)
    dw_in_t = _wgrad(dz, h2, IN_WIDTH, D_MODEL, "wgrad_in")
    deps = grads_ready("mixer", {"w_in": dw_in_t, "w_out": dw_out})
    dx, h1, dgu1, a1, df1, grads["ffn1_pre"], grads["ffn1_post"] = _ffn_bwd(
        dx1, x, f1, g1, u1, small["ffn1_pre"], small["ffn1_post"], wgu1, wd1, mine, deps=deps)
    dwgu1 = _wgrad(h1, dgu1, D_MODEL, FF_CHUNK, "wgrad_gu1", column_shards=True)
    deps = grads_ready("ffn1_gu", {"ffn1_w_gu": dwgu1}, small=grads)
    dwd1 = _wgrad(a1, df1, FF_CHUNK, D_MODEL, "wgrad_down1", deps=deps)
    grads_ready("ffn1_down", {"ffn1_w_down": dwd1})
    return dx


def _place():
    return lax.axis_index("x"), lax.axis_index("y"), lax.axis_index("c")


def _other_chips(x, y):
    return [(1 - x, y), (x, 1 - y), (1 - x, 1 - y)]


def _hbm_shape(shape, dtype):
    return jax.ShapeDtypeStruct(shape, dtype)


_HBM = pl.BlockSpec(memory_space=pltpu.HBM)
_SEM = pl.BlockSpec(memory_space=pltpu.SEMAPHORE)
_EFFECT = pltpu.SideEffectType.DATAFLOW_SIDE_EFFECTING
GATHER, GATHER_HALF, REDUCE, BROADCAST = "gather", "gather_half", "reduce", "broadcast"
N_DEVICES = 8


def _in_hbm(a):
    return pltpu.with_memory_space_constraint(a, pltpu.HBM)


def _core_half(rows, c):
    return pl.ds(pl.multiple_of(c * (rows // 2), 16), rows // 2)


def _chip_copies(kind, srcs, lands, send_sems, recv_sems):
    x, y, c = _place()
    mine = 2 * x + y
    copies = []
    for w in range(len(srcs)):
        if kind == BROADCAST:
            peers = [(x ^ (k >> 2), y ^ ((k >> 1) & 1), c ^ (k & 1)) for k in range(1, N_DEVICES)]
        else:
            peers = [(px, py, c) for px, py in _other_chips(x, y)]
        for k, (px, py, pc) in enumerate(peers):
            if kind == GATHER:
                src, dst = srcs[w], lands[w].at[mine]
            elif kind == GATHER_HALF:
                half = _core_half(srcs[w].shape[0], c)
                src, dst = srcs[w].at[half, :], lands[w].at[mine, half, :]
            elif kind == BROADCAST:
                src, dst = srcs[w], lands[w].at[2 * mine + c]
            else:
                src, dst = srcs[w].at[2 * px + py], lands[w].at[k]
            pair = len(peers) * w + k
            copies.append(pltpu.make_async_remote_copy(
                src_ref=src, dst_ref=dst, send_sem=send_sems.at[pair], recv_sem=recv_sems.at[pair],
                device_id=(px, py, pc), device_id_type=MESH))
    return copies


def _landing_shape(kind, src):
    if kind == REDUCE:
        return (N_SHARD - 1,) + src.shape[1:]
    return ((N_DEVICES if kind == BROADCAST else N_SHARD),) + src.shape


def _peer_count(kind):
    return N_DEVICES - 1 if kind == BROADCAST else N_SHARD - 1


def _exchange_start(kinds, groups, name):
    sizes = [len(g) for g in groups]
    flat = [s for g in groups for s in g]
    n, ng = len(flat), len(groups)

    def body(*refs):
        srcs, lands = refs[:n], refs[n:2 * n]
        sems = refs[2 * n:2 * n + 2 * ng]
        token = refs[-1]
        start = 0
        for gi, size in enumerate(sizes):
            for cp in _chip_copies(kinds[gi], srcs[start:start + size], lands[start:start + size],
                                   sems[2 * gi], sems[2 * gi + 1]):
                cp.start()
            start += size
        token[...] = jnp.zeros_like(token)

    landings = [lax.empty(_landing_shape(kind, s), s.dtype) for kind, g in zip(kinds, groups) for s in g]
    sem_shapes = [pltpu.SemaphoreType.DMA((size * _peer_count(kind),)) for kind, size in zip(kinds, sizes)
                  for _ in range(2)]
    outs = pl.pallas_call(
        body, name=name,
        in_specs=[_HBM] * (2 * n),
        out_specs=[_SEM] * (2 * ng) + [_HBM] * (2 * n) + [pl.BlockSpec(memory_space=pltpu.VMEM)],
        out_shape=sem_shapes + [pltpu.HBM(a.shape, a.dtype) for a in flat + landings]
        + [jax.ShapeDtypeStruct((8, _LANES), F32)],
        input_output_aliases={i: 2 * ng + i for i in range(2 * n)},
        compiler_params=pltpu.CompilerParams(has_side_effects=_EFFECT),
    )(*[_in_hbm(a) for a in flat + landings])
    sems, srcs, lands, token = outs[:2 * ng], outs[2 * ng:2 * ng + n], outs[2 * ng + n:2 * ng + 2 * n], outs[-1]
    handles, start = [], 0
    for gi, size in enumerate(sizes):
        handles.append((sems[2 * gi], sems[2 * gi + 1], srcs[start:start + size], lands[start:start + size]))
        start += size
    return handles, token


def _exchange_wait(kind, handle, after, name):
    send_sems, recv_sems, srcs, lands = handle
    n = len(srcs)

    def body(*refs):
        copies = _chip_copies(kind, refs[:n], refs[n:2 * n], refs[2 * n], refs[2 * n + 1])
        for cp in copies:
            cp.wait_send()
        for cp in copies:
            cp.wait_recv()

    outs = pl.pallas_call(
        body, name=name,
        in_specs=[_HBM] * (2 * n) + [_SEM, _SEM] + [_ANY] * len(after),
        out_specs=[_HBM] * (2 * n),
        out_shape=[pltpu.HBM(a.shape, a.dtype) for a in list(srcs) + list(lands)],
        input_output_aliases={i: i for i in range(2 * n)},
        compiler_params=pltpu.CompilerParams(has_side_effects=_EFFECT),
    )(*srcs, *lands, send_sems, recv_sems, *after)
    return outs[:n], outs[n:]


def _swap_gathered_halves(lands, name):
    n = len(lands)

    def body(*refs):
        bufs = refs[n:2 * n]
        send_sems, recv_sems = refs[2 * n:]
        x, y, c = _place()
        mine = 2 * x + y
        sends, arrivals = [], []
        for w in range(n):
            rows = bufs[w].shape[1]
            for d in range(1, N_SHARD):
                slot = (mine + d) % N_SHARD
                sems = dict(send_sem=send_sems.at[(N_SHARD - 1) * w + d - 1],
                            recv_sem=recv_sems.at[(N_SHARD - 1) * w + d - 1],
                            device_id=(x, y, 1 - c), device_id_type=MESH)
                fetched = bufs[w].at[slot, _core_half(rows, c), :]
                missing = bufs[w].at[slot, _core_half(rows, 1 - c), :]
                sends.append(pltpu.make_async_remote_copy(src_ref=fetched, dst_ref=fetched, **sems))
                arrivals.append(pltpu.make_async_remote_copy(src_ref=missing, dst_ref=missing, **sems))
        for cp in sends:
            cp.start()
        for cp in arrivals:
            cp.wait_recv()
        for cp in sends:
            cp.wait_send()

    return pl.pallas_call(
        body, name=name, in_specs=[_ANY] * n, out_specs=[_ANY] * n,
        out_shape=[_hbm_shape(a.shape, a.dtype) for a in lands],
        input_output_aliases={i: i for i in range(n)},
        scratch_shapes=[pltpu.SemaphoreType.DMA((n * (N_SHARD - 1),)), pltpu.SemaphoreType.DMA((n * (N_SHARD - 1),))],
        compiler_params=pltpu.CompilerParams(has_side_effects=True),
    )(*lands)


def _swap_with_sibling(partials, name):
    n = len(partials)

    def body(*refs):
        ins, outs = refs[:n], refs[n:2 * n]
        send_sems, recv_sems = refs[2 * n:]
        x, y, c = _place()
        sends = [pltpu.make_async_remote_copy(
            src_ref=ins[w], dst_ref=outs[w], send_sem=send_sems.at[w], recv_sem=recv_sems.at[w],
            device_id=(x, y, 1 - c), device_id_type=MESH) for w in range(n)]
        for cp in sends:
            cp.start()
        for cp in sends:
            cp.wait_recv()
        for cp in sends:
            cp.wait_send()

    return pl.pallas_call(
        body, name=name,
        in_specs=[_ANY] * n, out_specs=[_ANY] * n,
        out_shape=[_hbm_shape(p.shape, p.dtype) for p in partials],
        scratch_shapes=[pltpu.SemaphoreType.DMA((n,)), pltpu.SemaphoreType.DMA((n,))],
        compiler_params=pltpu.CompilerParams(has_side_effects=True),
    )(*partials)


def _row_block(rows, cap):
    best = None
    for cand in range(16, min(rows, cap) + 1, 16):
        if rows % cand == 0:
            best = cand
    assert best is not None, rows
    return best


def _chip_partial(own, received, shard, name):
    _, R, C = own.shape
    rb = _row_block(R, 512)

    def body(shard_ref, own_ref, rec_ref, out_ref):
        acc = own_ref[...]
        for k in range(3):
            acc = acc + rec_ref[k].astype(F32)
        out_ref[...] = acc.astype(BF16)

    return pl.pallas_call(
        body, name=name,
        grid_spec=pltpu.PrefetchScalarGridSpec(
            num_scalar_prefetch=1, grid=(R // rb,),
            in_specs=[pl.BlockSpec((None, rb, C), lambda i, s: (s[0], i, 0)),
                      pl.BlockSpec((3, rb, C), lambda i, s: (0, i, 0))],
            out_specs=pl.BlockSpec((rb, C), lambda i, s: (i, 0))),
        out_shape=jax.ShapeDtypeStruct((R, C), BF16),
        compiler_params=_params(dimension_semantics=("arbitrary",)),
    )(shard, own, received)


def _adamw(w, m, v, g_parts, name, slot=None):
    R, C = w.shape
    by_device = slot is not None
    rb = _row_block(R, 256) if R % 16 == 0 else R

    def body(w_ref, m_ref, v_ref, *refs):
        g_refs, (grad_ref, delta_ref, m_out, v_out) = refs[:-4], refs[-4:]
        if by_device:
            own_ref, land_ref, slot_ref = g_refs
            part = lambda d: jnp.where(slot_ref[0] == d, own_ref[...], land_ref[d])
            g = part(0)
            for d in range(1, N_DEVICES):
                g = g + part(d)
        else:
            g = g_refs[0][...].astype(F32)
            for g_ref in g_refs[1:]:
                g = g + g_ref[...].astype(F32)
        grad_ref[...] = g
        new_m = ADAM_B1 * m_ref[...] + (1.0 - ADAM_B1) * g
        new_v = ADAM_B2 * v_ref[...] + (1.0 - ADAM_B2) * (g * g)
        m_hat = new_m / (1.0 - ADAM_B1 ** ADAM_STEP)
        v_hat = new_v / (1.0 - ADAM_B2 ** ADAM_STEP)
        delta_ref[...] = -ADAM_LR * (m_hat / (jnp.sqrt(v_hat) + ADAM_EPS) + ADAM_WD * w_ref[...])
        m_out[...] = new_m
        v_out[...] = new_v

    spec = pl.BlockSpec((rb, C), lambda i: (i, 0))
    if by_device:
        g_specs = [spec, pl.BlockSpec((N_DEVICES, rb, C), lambda i: (0, i, 0)), _SMEM]
        g_parts = list(g_parts) + [slot]
    else:
        g_specs = [spec] * len(g_parts)
    return pl.pallas_call(
        body, name=name, grid=(R // rb,),
        in_specs=[spec, spec, spec] + g_specs,
        out_specs=[spec] * 4,
        out_shape=[jax.ShapeDtypeStruct((R, C), F32)] * 4,
        compiler_params=_params(dimension_semantics=("arbitrary",)),
    )(w, m, v, *g_parts)


SMALL_NAMES = ("ffn1_pre", "ffn1_post", "mix_pre", "pool_scale", "sinks", "g_pool", "g_attn", "mix_post",
               "ffn2_pre", "ffn2_post", "w_pool")
_SLAB_PART = 8 * _LANES


SLAB_NAMES = SMALL_NAMES + ("loss",)


def _to_slab(parts):
    rows = []
    for name in SLAB_NAMES:
        flat = parts[name].reshape(-1) if name in parts else jnp.zeros((_SLAB_PART,), F32)
        padded = -(-flat.shape[0] // _SLAB_PART) * _SLAB_PART
        rows.append(jnp.pad(flat, (0, padded - flat.shape[0])).reshape(-1, _LANES))
    return jnp.concatenate(rows, axis=0)


def _from_slab(slab, like):
    out, row = {}, 0
    for name in SLAB_NAMES:
        size = like[name].size
        rows = -(-size // _SLAB_PART) * (_SLAB_PART // _LANES)
        out[name] = slab[row:row + rows].reshape(-1)[:size].reshape(like[name].shape)
        row += rows
    return out


BIG_NAMES = ("ffn1_w_gu", "ffn1_w_down", "w_in", "w_out", "ffn2_w_gu", "ffn2_w_down")
WEIGHT_ORDER = ("ffn1_pre", "ffn1_w_gu", "ffn1_w_down", "ffn1_post", "mix_pre", "w_in", "w_pool", "pool_scale",
                "sinks", "g_pool", "g_attn", "w_out", "mix_post", "ffn2_pre", "ffn2_w_gu", "ffn2_w_down", "ffn2_post")


def kernel(x, positions, ffn1_pre, ffn1_w_gu, ffn1_w_down, ffn1_post, mix_pre, w_in, w_pool, pool_scale, sinks, g_pool, g_attn, w_out, mix_post, ffn2_pre, ffn2_w_gu, ffn2_w_down, ffn2_post, loss_target, m_ffn1_pre, m_ffn1_w_gu, m_ffn1_w_down, m_ffn1_post, m_mix_pre, m_w_in, m_w_pool, m_pool_scale, m_sinks, m_g_pool, m_g_attn, m_w_out, m_mix_post, m_ffn2_pre, m_ffn2_w_gu, m_ffn2_w_down, m_ffn2_post, v_ffn1_pre, v_ffn1_w_gu, v_ffn1_w_down, v_ffn1_post, v_mix_pre, v_w_in, v_w_pool, v_pool_scale, v_sinks, v_g_pool, v_g_attn, v_w_out, v_mix_post, v_ffn2_pre, v_ffn2_w_gu, v_ffn2_w_down, v_ffn2_post):
    given = dict(locals())
    weights = {n: given[n][0] for n in WEIGHT_ORDER}
    moments_m = {n: given["m_" + n][0] for n in WEIGHT_ORDER}
    moments_v = {n: given["v_" + n][0] for n in WEIGHT_ORDER}
    S = x.shape[1]
    shard = (2 * lax.axis_index("x") + lax.axis_index("y")).astype(jnp.int32).reshape(1)

    local16 = {n: weights[n].astype(BF16) for n in BIG_NAMES if n != "w_in"}
    local16["w_in"] = weights["w_in"].T.astype(BF16)
    gather_groups = {"ffn1": ("ffn1_w_gu", "ffn1_w_down"), "mixer": ("w_in", "w_out"),
                     "ffn2": ("ffn2_w_gu", "ffn2_w_down")}
    gather_kinds = {"ffn1": GATHER_HALF, "mixer": GATHER, "ffn2": GATHER}
    handles, _ = _exchange_start(list(gather_kinds.values()),
                                 [[local16[n] for n in names] for names in gather_groups.values()], "gather_start")
    gather_handles = dict(zip(gather_groups, handles))

    def weights_of(group, after):
        kind = gather_kinds[group]
        owns, lands = _exchange_wait(kind, gather_handles[group], list(after), "gather_wait_" + group)
        if kind == GATHER_HALF:
            lands = _swap_gathered_halves(lands, "swap_gathered_" + group)
        return list(zip(lands, owns))

    pending, last_token = {}, []

    def grads_ready(group, grads, small=None):
        names = list(grads)
        kinds, sources = [REDUCE], [[grads[n][1] for n in names]]
        if small is not None:
            kinds, sources = kinds + [BROADCAST], sources + [[_to_slab(small)]]
        handles, token = _exchange_start(kinds, sources, "reduce_start_" + group)
        handle = handles[0]
        if small is not None:
            pending["small"] = handles[1]
        pending[group] = (names, handle, [grads[n][0] for n in names])
        last_token[:] = [token]
        return [token]

    small = {n: (weights[n] if weights[n].ndim > 1 else weights[n].reshape(1, -1)) for n in SMALL_NAMES}
    dx = _local_step(x[0], positions.reshape(S, 1), loss_target[0], small, shard, weights_of, grads_ready)

    grad, delta, new_m, new_v = {}, {}, {}, {}

    def finish(groups, after):
        names, partials = [], []
        for group in groups:
            group_names, handle, own32 = pending[group]
            _, received = _exchange_wait(REDUCE, handle, after, "reduce_wait_" + group)
            names += group_names
            partials += [_chip_partial(g32, rec, shard, "chip_partial_" + n)
                         for n, g32, rec in zip(group_names, own32, received)]
        siblings = _swap_with_sibling(partials, "swap_" + groups[0])
        for name, mine, theirs in zip(names, partials, siblings):
            if name == "w_in":
                mine, theirs = mine.T, theirs.T
            grad[name], delta[name], new_m[name], new_v[name] = _adamw(
                weights[name], moments_m[name], moments_v[name], [mine, theirs], "adamw_" + name)
        return [grad[names[-1]]]

    after = finish(["ffn2"], last_token)
    after = finish(["mixer"], after)
    (own_slab,), (slab_landing,) = _exchange_wait(BROADCAST, pending["small"], after, "reduce_wait_small")
    device = (2 * shard + lax.axis_index("c")).astype(jnp.int32)
    small_like = dict({n: small[n] for n in SMALL_NAMES}, loss=jnp.zeros((8, _LANES), F32))
    slabs = _adamw(_to_slab(small), _to_slab({n: moments_m[n] for n in SMALL_NAMES}),
                   _to_slab({n: moments_v[n] for n in SMALL_NAMES}), [own_slab, slab_landing], "adamw_small",
                   slot=device)
    for store, slab in zip((grad, delta, new_m, new_v), slabs):
        store.update(_from_slab(slab, small_like))
    loss = grad["loss"][0, 0]
    after = finish(["ffn1_gu"], [slabs[0]])
    finish(["ffn1_down"], after)

    def out(store):
        return [store[n].reshape(given[n].shape) for n in WEIGHT_ORDER]
    return (loss, dx[None], *out(grad), *out(delta), *out(new_m), *out(new_v))
```

```python
import jax
import jax.numpy as jnp
from jax import lax
from jax.experimental import pallas as pl
from jax.experimental.pallas import tpu as pltpu

F32 = jnp.float32
BF16 = jnp.bfloat16

D_MODEL = 1024
D_FF = 2816
N_SHARD = 4
FF_CHUNK = D_FF // 2
POOL_WINDOWS = (2, 4, 8, 16)
POOL_WIDTH = 512
POOL_GROUP = 128
HALO = 16
HEAD_DIM = 64
N_HEADS = 8
N_KV_HEADS = 2
ATTN_WIDTH = 512
KV_WIDTH = 128
IN_WIDTH = 1280
BLOCK = 128
ATTN_BLOCKS = 4
ROT_DIM = 16
ROPE_THETA = 500000.0
EPS = 1e-6
NEG_INF = -1e30
ATTN_SCALE = HEAD_DIM ** -0.5

ADAM_LR = 0.001
ADAM_B1 = 0.9
ADAM_B2 = 0.999
ADAM_EPS = 1e-08
ADAM_WD = 0.01
ADAM_STEP = 10

VMEM_LIMIT = 60 * 1024 * 1024
FFN_FWD_TILE = 512
FFN_BWD_TILE = 256
MIXER_TILE = 512

MESH = pl.DeviceIdType.MESH


def _params(**kw):
    return pltpu.CompilerParams(vmem_limit_bytes=VMEM_LIMIT, **kw)


def _dot(a, b):
    return jnp.dot(a, b, preferred_element_type=F32)


def _dot_nt(a, b):
    return lax.dot_general(a, b, (((1,), (1,)), ((), ())), preferred_element_type=F32)


def _dot_tn(a, b):
    return lax.dot_general(a, b, (((0,), (0,)), ((), ())), preferred_element_type=F32)


def _rstd(x):
    return lax.rsqrt(jnp.mean(x * x, axis=-1, keepdims=True) + EPS)


def _norm_bwd(dy, xn, r, gain):
    dxn = dy * gain
    return r * (dxn - xn * jnp.mean(dxn * xn, axis=-1, keepdims=True))


def _sigmoid(x):
    return 1.0 / (1.0 + jnp.exp(-x))


def _full(shape):
    return pl.BlockSpec(shape, lambda *_: (0,) * len(shape))


def _rows(tile, width):
    return pl.BlockSpec((tile, width), lambda i: (i, 0))


_ANY = pl.BlockSpec(memory_space=pl.ANY)


_SMEM = pl.BlockSpec(memory_space=pltpu.SMEM)


def _load_once(pairs, sem):
    @pl.when(pl.program_id(0) == 0)
    def _():
        copies = [pltpu.make_async_copy(src, dst, sem.at[n]) for n, (src, dst) in enumerate(pairs)]
        for cp in copies:
            cp.start()
        for cp in copies:
            cp.wait()


def _gathered(land_ref, own_ref, vmem_ref, mine, rows=None):
    def dst(slot):
        if rows is None:
            return vmem_ref.at[slot]
        return vmem_ref.at[pl.ds(pl.multiple_of(slot * rows, 16), rows), :]
    pairs = [(land_ref.at[(mine + d) % N_SHARD], dst((mine + d) % N_SHARD)) for d in range(1, N_SHARD)]
    return pairs + [(own_ref, dst(mine))]


def _ignoring(body, start, count):
    def wrapped(*refs):
        return body(*refs[:start], *refs[start + count:])
    return wrapped


def _ffn_fwd(x, pre, post, wgu, wd, mine, target=None, deps=()):
    S = x.shape[0]
    tm = FFN_FWD_TILE
    with_loss = target is not None

    def body(*refs):
        if with_loss:
            (x_ref, pre_ref, post_ref, wgu_land, wgu_own, wd_land, wd_own, mine_ref, tgt_ref,
             out_ref, g_ref, u_ref, f_ref, loss_ref, wgu_v, wd_v, sem) = refs
        else:
            (x_ref, pre_ref, post_ref, wgu_land, wgu_own, wd_land, wd_own, mine_ref,
             out_ref, g_ref, u_ref, f_ref, wgu_v, wd_v, sem) = refs
        _load_once(_gathered(wgu_land, wgu_own, wgu_v, mine_ref[0])
                   + _gathered(wd_land, wd_own, wd_v, mine_ref[0], rows=D_FF // N_SHARD), sem)
        xv = x_ref[...]
        h = ((xv * _rstd(xv)) * pre_ref[...]).astype(BF16)
        facc = jnp.zeros((tm, D_MODEL), F32)
        for c in range(2):
            cols = slice(c * FF_CHUNK, (c + 1) * FF_CHUNK)
            g = _dot(h, wgu_v[c])
            u = _dot(h, wgu_v[2 + c])
            g_ref[:, cols] = g.astype(BF16)
            u_ref[:, cols] = u.astype(BF16)
            a = (g * _sigmoid(g)) * u
            facc = facc + _dot(a.astype(BF16), wd_v[cols, :])
        f_ref[...] = facc
        out = xv + 0.5 * ((facc * _rstd(facc)) * post_ref[...])
        if with_loss:
            diff = out - tgt_ref[...]
            out_ref[...] = diff * (1.0 / D_MODEL)

            @pl.when(pl.program_id(0) == 0)
            def _():
                loss_ref[...] = jnp.zeros_like(loss_ref)
            loss_ref[...] += jnp.sum(diff * diff)
        else:
            out_ref[...] = out

    in_specs = [_rows(tm, D_MODEL), _full((1, D_MODEL)), _full((1, D_MODEL)), _ANY, _ANY, _ANY, _ANY, _SMEM]
    args = [x, pre, post, *wgu, *wd, mine]
    out_shape = [jax.ShapeDtypeStruct((S, D_MODEL), F32), jax.ShapeDtypeStruct((S, D_FF), BF16),
                 jax.ShapeDtypeStruct((S, D_FF), BF16), jax.ShapeDtypeStruct((S, D_MODEL), F32)]
    out_specs = [_rows(tm, D_MODEL), _rows(tm, D_FF), _rows(tm, D_FF), _rows(tm, D_MODEL)]
    if with_loss:
        in_specs.append(_rows(tm, D_MODEL))
        args.append(target)
        out_shape.append(jax.ShapeDtypeStruct((8, 128), F32))
        out_specs.append(_full((8, 128)))
    return pl.pallas_call(
        _ignoring(body, len(args), len(deps)), name="ffn_fwd_loss" if with_loss else "ffn_fwd",
        grid=(S // tm,), in_specs=in_specs + [_ANY] * len(deps), out_specs=out_specs, out_shape=out_shape,
        scratch_shapes=[pltpu.VMEM((N_SHARD, D_MODEL, FF_CHUNK), BF16), pltpu.VMEM((D_FF, D_MODEL), BF16),
                        pltpu.SemaphoreType.DMA((2 * N_SHARD,))],
        compiler_params=_params(dimension_semantics=("arbitrary",)),
    )(*args, *deps)


def _ffn_bwd(dout, x, f, g, u, pre, post, wgu, wd, mine, deps=()):
    S = x.shape[0]
    tm = FFN_BWD_TILE

    def body(dout_ref, x_ref, f_ref, g_ref, u_ref, pre_ref, post_ref, wgu_land, wgu_own, wd_land, wd_own, mine_ref,
             dx_ref, h_ref, dgu_ref, a_ref, df_ref, dpre_ref, dpost_ref, wgu_v, wd_v, sem):
        _load_once(_gathered(wgu_land, wgu_own, wgu_v, mine_ref[0])
                   + _gathered(wd_land, wd_own, wd_v, mine_ref[0], rows=D_FF // N_SHARD), sem)

        @pl.when(pl.program_id(0) == 0)
        def _():
            dpre_ref[...] = jnp.zeros_like(dpre_ref)
            dpost_ref[...] = jnp.zeros_like(dpost_ref)

        dout_v = dout_ref[...]
        dn = 0.5 * dout_v
        fv = f_ref[...]
        rf = _rstd(fv)
        fn = fv * rf
        dpost_ref[...] += jnp.sum(dn * fn, axis=0, keepdims=True)
        df = _norm_bwd(dn, fn, rf, post_ref[...]).astype(BF16)
        df_ref[...] = df
        dh = jnp.zeros((tm, D_MODEL), F32)
        for c in range(2):
            cols = slice(c * FF_CHUNK, (c + 1) * FF_CHUNK)
            da = _dot_nt(df, wd_v[cols, :])
            gv = g_ref[:, cols].astype(F32)
            uv = u_ref[:, cols].astype(F32)
            sg = _sigmoid(gv)
            silu = gv * sg
            a_ref[:, cols] = (silu * uv).astype(BF16)
            dg = ((da * uv) * (sg * (1.0 + gv * (1.0 - sg)))).astype(BF16)
            du = (da * silu).astype(BF16)
            dgu_ref[:, cols] = dg
            dgu_ref[:, 2 * FF_CHUNK + c * FF_CHUNK:2 * FF_CHUNK + (c + 1) * FF_CHUNK] = du
            dh = dh + _dot_nt(dg, wgu_v[c]) + _dot_nt(du, wgu_v[2 + c])
        xv = x_ref[...]
        rx = _rstd(xv)
        xn = xv * rx
        h_ref[...] = (xn * pre_ref[...]).astype(BF16)
        dpre_ref[...] += jnp.sum(dh * xn, axis=0, keepdims=True)
        dx_ref[...] = dout_v + _norm_bwd(dh, xn, rx, pre_ref[...])

    args = [dout, x, f, g, u, pre, post, *wgu, *wd, mine]
    return pl.pallas_call(
        _ignoring(body, len(args), len(deps)), name="ffn_bwd", grid=(S // tm,),
        in_specs=[_rows(tm, D_MODEL), _rows(tm, D_MODEL), _rows(tm, D_MODEL), _rows(tm, D_FF), _rows(tm, D_FF),
                  _full((1, D_MODEL)), _full((1, D_MODEL)), _ANY, _ANY, _ANY, _ANY, _SMEM] + [_ANY] * len(deps),
        out_specs=[_rows(tm, D_MODEL), _rows(tm, D_MODEL), _rows(tm, 2 * D_FF), _rows(tm, D_FF), _rows(tm, D_MODEL),
                   _full((1, D_MODEL)), _full((1, D_MODEL))],
        out_shape=[jax.ShapeDtypeStruct((S, D_MODEL), F32), jax.ShapeDtypeStruct((S, D_MODEL), BF16),
                   jax.ShapeDtypeStruct((S, 2 * D_FF), BF16), jax.ShapeDtypeStruct((S, D_FF), BF16),
                   jax.ShapeDtypeStruct((S, D_MODEL), BF16),
                   jax.ShapeDtypeStruct((1, D_MODEL), F32), jax.ShapeDtypeStruct((1, D_MODEL), F32)],
        scratch_shapes=[pltpu.VMEM((N_SHARD, D_MODEL, FF_CHUNK), BF16), pltpu.VMEM((D_FF, D_MODEL), BF16),
                        pltpu.SemaphoreType.DMA((2 * N_SHARD,))],
        compiler_params=_params(dimension_semantics=("arbitrary",)),
    )(*args, *deps)


def _wgrad(lhs, rhs, m_block, n_block, name, column_shards=False, tk=2048, deps=()):
    S, M = lhs.shape
    N = rhs.shape[1]
    k_steps = S // tk

    def body(lhs_ref, rhs_ref, out_ref, out16_ref):
        k = pl.program_id(2)

        @pl.when(k == 0)
        def _():
            out_ref[...] = jnp.zeros_like(out_ref)
        out_ref[...] += _dot_tn(lhs_ref[...], rhs_ref[...])

        @pl.when(k == k_steps - 1)
        def _():
            out16_ref[...] = out_ref[...].astype(BF16)

    if column_shards:
        assert N == N_SHARD * n_block
        shape = (N_SHARD, M, n_block)
        out_spec = pl.BlockSpec((None, m_block, n_block), lambda i, j, k: (j, i, 0))
    else:
        shape = (M, N)
        out_spec = pl.BlockSpec((m_block, n_block), lambda i, j, k: (i, j))
    out, out16 = pl.pallas_call(
        _ignoring(body, 2, len(deps)), name=name, grid=(M // m_block, N // n_block, k_steps),
        in_specs=[pl.BlockSpec((tk, m_block), lambda i, j, k: (k, i)),
                  pl.BlockSpec((tk, n_block), lambda i, j, k: (k, j))] + [_ANY] * len(deps),
        out_specs=[out_spec, out_spec],
        out_shape=[jax.ShapeDtypeStruct(shape, F32), jax.ShapeDtypeStruct(shape, BF16)],
        compiler_params=_params(dimension_semantics=("arbitrary", "arbitrary", "arbitrary")),
    )(lhs, rhs, *deps)
    if not column_shards:
        out = out.reshape(N_SHARD, M // N_SHARD, N)
        out16 = out16.reshape(N_SHARD, M // N_SHARD, N)
    return out, out16


def _rope_tables(pos, invf):
    S = pos.shape[0]
    tm = MIXER_TILE

    def body(pos_ref, invf_ref, out_ref):
        ang = pos_ref[...].astype(F32) * invf_ref[...]
        cos, sin = jnp.cos(ang), jnp.sin(ang)
        lane = lax.broadcasted_iota(jnp.int32, ang.shape, 1) % HEAD_DIM
        first = lane < ROT_DIM // 2
        second = (lane >= ROT_DIM // 2) & (lane < ROT_DIM)
        out_ref[0] = jnp.where(lane < ROT_DIM, cos, 1.0)
        out_ref[1] = jnp.where(first, sin, 0.0)
        out_ref[2] = jnp.where(second, sin, 0.0)

    return pl.pallas_call(
        body, name="rope_tables", grid=(S // tm,),
        in_specs=[_rows(tm, 1), _full((1, _LANES))],
        out_specs=pl.BlockSpec((3, tm, _LANES), lambda i: (0, i, 0)),
        out_shape=jax.ShapeDtypeStruct((3, S, _LANES), F32),
        compiler_params=_params(dimension_semantics=("arbitrary",)),
    )(pos, invf)


def _table_spec(tm):
    return pl.BlockSpec((3, tm, _LANES), lambda i: (0, i, 0))


_HALF = ROT_DIM // 2
_LANES = 128


def _rope(t, tables):
    c, s_first, s_second = tables
    return t * c - pltpu.roll(t, _LANES - _HALF, axis=1) * s_first + pltpu.roll(t, _HALF, axis=1) * s_second


def _rope_transposed(t, tables):
    c, s_first, s_second = tables
    return t * c - pltpu.roll(t * s_first, _HALF, axis=1) + pltpu.roll(t * s_second, _LANES - _HALF, axis=1)


def _store_head_variants(ref, t):
    rolled = pltpu.roll(t, HEAD_DIM, axis=1)
    low = lax.broadcasted_iota(jnp.int32, t.shape, 1) < HEAD_DIM
    zero = jnp.zeros_like(t)
    ref[0] = jnp.where(low, t, zero).astype(BF16)
    ref[1] = jnp.where(low, zero, rolled).astype(BF16)
    ref[2] = jnp.where(low, rolled, zero).astype(BF16)
    ref[3] = jnp.where(low, zero, t).astype(BF16)


def _mixer_in_fwd(x, pre, w_in_t, mine, rope, deps=()):
    S = x.shape[0]
    tm = MIXER_TILE

    def body(x_ref, pre_ref, w_land, w_own, mine_ref, rope_ref, u_ref, q_ref, k_ref, v_ref, w_v, sem):
        _load_once(_gathered(w_land, w_own, w_v, mine_ref[0], rows=IN_WIDTH // N_SHARD), sem)
        xv = x_ref[...]
        h = ((xv * _rstd(xv)) * pre_ref[...]).astype(BF16)
        z = _dot_nt(h, w_v[...])
        tables = (rope_ref[0], rope_ref[1], rope_ref[2])
        u_ref[...] = z[:, :POOL_WIDTH]
        for t in range(ATTN_WIDTH // _LANES):
            lo = POOL_WIDTH + t * _LANES
            q_ref[:, t * _LANES:(t + 1) * _LANES] = (_rope(z[:, lo:lo + _LANES], tables) * ATTN_SCALE).astype(BF16)
        kv = POOL_WIDTH + ATTN_WIDTH
        _store_head_variants(k_ref, _rope(z[:, kv:kv + KV_WIDTH], tables))
        _store_head_variants(v_ref, z[:, kv + KV_WIDTH:])

    args = [x, pre, *w_in_t, mine, rope]
    variants = pl.BlockSpec((2 * N_KV_HEADS, tm, KV_WIDTH), lambda i: (0, i, 0))
    return pl.pallas_call(
        _ignoring(body, len(args), len(deps)), name="mixer_in_fwd", grid=(S // tm,),
        in_specs=[_rows(tm, D_MODEL), _full((1, D_MODEL)), _ANY, _ANY, _SMEM, _table_spec(tm)] + [_ANY] * len(deps),
        out_specs=[_rows(tm, POOL_WIDTH), _rows(tm, ATTN_WIDTH), variants, variants],
        out_shape=[jax.ShapeDtypeStruct((S, POOL_WIDTH), F32), jax.ShapeDtypeStruct((S, ATTN_WIDTH), BF16),
                   jax.ShapeDtypeStruct((2 * N_KV_HEADS, S, KV_WIDTH), BF16),
                   jax.ShapeDtypeStruct((2 * N_KV_HEADS, S, KV_WIDTH), BF16)],
        scratch_shapes=[pltpu.VMEM((IN_WIDTH, D_MODEL), BF16), pltpu.SemaphoreType.DMA((N_SHARD,))],
        compiler_params=_params(dimension_semantics=("arbitrary",)),
    )(*args, *deps)


def _mixer_in_bwd(dres, x, pre, w_in_t, mine, du, dq, dk, dv, dk_next, dv_next, rope, deps=()):
    S = x.shape[0]
    tm = MIXER_TILE

    def body(dres_ref, x_ref, pre_ref, w_land, w_own, mine_ref, du_ref, dq_ref, dk_ref, dv_ref, dkx_ref, dvx_ref,
             rope_ref,
             dx_ref, dz_ref, h_ref, dpre_ref, w_v, sem):
        _load_once(_gathered(w_land, w_own, w_v, mine_ref[0], rows=IN_WIDTH // N_SHARD), sem)

        @pl.when(pl.program_id(0) == 0)
        def _():
            dpre_ref[...] = jnp.zeros_like(dpre_ref)

        tables = (rope_ref[0], rope_ref[1], rope_ref[2])
        dz_ref[:, :POOL_WIDTH] = du_ref[...]
        for t in range(ATTN_WIDTH // _LANES):
            lo = POOL_WIDTH + t * _LANES
            dz_ref[:, lo:lo + _LANES] = _rope_transposed(dq_ref[:, t * _LANES:(t + 1) * _LANES], tables).astype(BF16)
        kv = POOL_WIDTH + ATTN_WIDTH
        has_next = pl.program_id(0) + 1 < steps
        pad = jnp.zeros((tm - BLOCK, KV_WIDTH), F32)
        dk_tile = dk_ref[...] + jnp.concatenate([pad, jnp.where(has_next, dkx_ref[...], 0.0)], axis=0)
        dv_tile = dv_ref[...] + jnp.concatenate([pad, jnp.where(has_next, dvx_ref[...], 0.0)], axis=0)
        dz_ref[:, kv:kv + KV_WIDTH] = _rope_transposed(dk_tile, tables).astype(BF16)
        dz_ref[:, kv + KV_WIDTH:] = dv_tile.astype(BF16)
        dh = _dot(dz_ref[...], w_v[...])
        xv = x_ref[...]
        rx = _rstd(xv)
        xn = xv * rx
        h_ref[...] = (xn * pre_ref[...]).astype(BF16)
        dpre_ref[...] += jnp.sum(dh * xn, axis=0, keepdims=True)
        dx_ref[...] = dres_ref[...] + _norm_bwd(dh, xn, rx, pre_ref[...])

    assert tm == ATTN_BLOCKS * BLOCK
    steps = S // tm
    nxt = pl.BlockSpec((None, BLOCK, KV_WIDTH), lambda i: (jnp.minimum(i + 1, steps - 1), 0, 0))
    args = [dres, x, pre, *w_in_t, mine, du, dq, dk, dv, dk_next, dv_next, rope]
    return pl.pallas_call(
        _ignoring(body, len(args), len(deps)), name="mixer_in_bwd", grid=(S // tm,),
        in_specs=[_rows(tm, D_MODEL), _rows(tm, D_MODEL), _full((1, D_MODEL)), _ANY, _ANY, _SMEM,
                  _rows(tm, POOL_WIDTH), _rows(tm, ATTN_WIDTH), _rows(tm, KV_WIDTH), _rows(tm, KV_WIDTH), nxt, nxt,
                  _table_spec(tm)] + [_ANY] * len(deps),
        out_specs=[_rows(tm, D_MODEL), _rows(tm, IN_WIDTH), _rows(tm, D_MODEL), _full((1, D_MODEL))],
        out_shape=[jax.ShapeDtypeStruct((S, D_MODEL), F32), jax.ShapeDtypeStruct((S, IN_WIDTH), BF16),
                   jax.ShapeDtypeStruct((S, D_MODEL), BF16), jax.ShapeDtypeStruct((1, D_MODEL), F32)],
        scratch_shapes=[pltpu.VMEM((IN_WIDTH, D_MODEL), BF16), pltpu.SemaphoreType.DMA((N_SHARD,))],
        compiler_params=_params(dimension_semantics=("arbitrary",)),
    )(*args, *deps)


def _pool_counts(tile_index, tm, width):
    t = tile_index * tm + lax.broadcasted_iota(jnp.int32, (tm, 1), 0)
    return jnp.minimum(t + 1, width).astype(F32)


def _pool_features(ext, u_tile, tile_index, tm):
    ds = []
    for gi, width in enumerate(POOL_WINDOWS):
        lanes = slice(gi * POOL_GROUP, (gi + 1) * POOL_GROUP)
        s = ext[:, lanes]
        shift = 1
        while shift < width:
            s = s + pltpu.roll(s, shift, axis=0)
            shift *= 2
        ds.append(s[HALO:, :] / _pool_counts(tile_index, tm, width) - u_tile[:, lanes])
    return ds


def _pool_fwd(u, w_pool, pool_scale, g_pool):
    S = u.shape[0]
    tm = MIXER_TILE

    def body(u_ref, w_ref, scale_ref, gain_ref, y_ref, ext_ref):
        i = pl.program_id(0)

        @pl.when(i == 0)
        def _():
            ext_ref[:HALO, :] = jnp.zeros((HALO, POOL_WIDTH), F32)

        u_tile = u_ref[...]
        ext_ref[HALO:, :] = u_tile
        ds = _pool_features(ext_ref[...], u_tile, i, tm)
        ext_ref[:HALO, :] = u_tile[tm - HALO:, :]
        ys = [_dot(ds[gi].astype(BF16), w_ref[gi].astype(BF16)) for gi in range(len(POOL_WINDOWS))]
        po = jnp.concatenate(ys, axis=1) * scale_ref[...]
        y_ref[...] = ((po * _rstd(po)) * gain_ref[...]).astype(BF16)

    return pl.pallas_call(
        body, name="pool_fwd", grid=(S // tm,),
        in_specs=[_rows(tm, POOL_WIDTH), _full((len(POOL_WINDOWS), POOL_GROUP, POOL_GROUP)),
                  _full((1, POOL_WIDTH)), _full((1, POOL_WIDTH))],
        out_specs=_rows(tm, POOL_WIDTH),
        out_shape=jax.ShapeDtypeStruct((S, POOL_WIDTH), BF16),
        scratch_shapes=[pltpu.VMEM((HALO + tm, POOL_WIDTH), F32)],
        compiler_params=_params(dimension_semantics=("arbitrary",)),
    )(u, w_pool, pool_scale, g_pool)


def _pool_bwd(dy, u, w_pool, pool_scale, g_pool):
    S = u.shape[0]
    tm = MIXER_TILE
    n_tiles = S // tm
    halo_blocks = tm // HALO

    def body(dy_ref, u_ref, uprev_ref, w_ref, scale_ref, gain_ref,
             du_ref, dw_ref, dscale_ref, dgain_ref, ext_ref, nxt_ref):
        i = pl.program_id(0)
        tile = n_tiles - 1 - i

        @pl.when(i == 0)
        def _():
            dw_ref[...] = jnp.zeros_like(dw_ref)
            dscale_ref[...] = jnp.zeros_like(dscale_ref)
            dgain_ref[...] = jnp.zeros_like(dgain_ref)
            nxt_ref[...] = jnp.zeros_like(nxt_ref)

        u_tile = u_ref[...]
        ext_ref[:HALO, :] = jnp.where(tile > 0, uprev_ref[...], 0.0)
        ext_ref[HALO:, :] = u_tile
        ds = _pool_features(ext_ref[...], u_tile, tile, tm)
        dsb = [d.astype(BF16) for d in ds]
        wb = [w_ref[gi].astype(BF16) for gi in range(len(POOL_WINDOWS))]
        yraw = jnp.concatenate([_dot(dsb[gi], wb[gi]) for gi in range(len(POOL_WINDOWS))], axis=1)
        po = yraw * scale_ref[...]
        r = _rstd(po)
        pn = po * r
        dyv = dy_ref[...]
        dgain_ref[...] += jnp.sum(dyv * pn, axis=0, keepdims=True)
        dpo = _norm_bwd(dyv, pn, r, gain_ref[...])
        dscale_ref[...] += jnp.sum(dpo * yraw, axis=0, keepdims=True)
        dyraw = (dpo * scale_ref[...]).astype(BF16)
        for gi, width in enumerate(POOL_WINDOWS):
            lanes = slice(gi * POOL_GROUP, (gi + 1) * POOL_GROUP)
            dw_ref[gi] += _dot_tn(dsb[gi], dyraw[:, lanes])
            dd = _dot_nt(dyraw[:, lanes], wb[gi])
            ddc = dd / _pool_counts(tile, tm, width)
            ext_ref[:tm, lanes] = ddc
            ext_ref[tm:, lanes] = nxt_ref[:, lanes]
            s = ext_ref[:, lanes]
            shift = 1
            while shift < width:
                s = s + pltpu.roll(s, HALO + tm - shift, axis=0)
                shift *= 2
            du_ref[:, lanes] = (s[:tm, :] - dd).astype(BF16)
            nxt_ref[:, lanes] = ddc[:HALO, :]

    return pl.pallas_call(
        body, name="pool_bwd", grid=(n_tiles,),
        in_specs=[pl.BlockSpec((tm, POOL_WIDTH), lambda i: (n_tiles - 1 - i, 0)),
                  pl.BlockSpec((tm, POOL_WIDTH), lambda i: (n_tiles - 1 - i, 0)),
                  pl.BlockSpec((HALO, POOL_WIDTH), lambda i: (jnp.maximum((n_tiles - 1 - i) * halo_blocks - 1, 0), 0)),
                  _full((len(POOL_WINDOWS), POOL_GROUP, POOL_GROUP)), _full((1, POOL_WIDTH)), _full((1, POOL_WIDTH))],
        out_specs=[pl.BlockSpec((tm, POOL_WIDTH), lambda i: (n_tiles - 1 - i, 0)),
                   _full((len(POOL_WINDOWS), POOL_GROUP, POOL_GROUP)), _full((1, POOL_WIDTH)), _full((1, POOL_WIDTH))],
        out_shape=[jax.ShapeDtypeStruct((S, POOL_WIDTH), BF16),
                   jax.ShapeDtypeStruct((len(POOL_WINDOWS), POOL_GROUP, POOL_GROUP), F32),
                   jax.ShapeDtypeStruct((1, POOL_WIDTH), F32), jax.ShapeDtypeStruct((1, POOL_WIDTH), F32)],
        scratch_shapes=[pltpu.VMEM((HALO + tm, POOL_WIDTH), F32), pltpu.VMEM((HALO, POOL_WIDTH), F32)],
        compiler_params=_params(dimension_semantics=("arbitrary",)),
    )(dy, u, u, w_pool, pool_scale, g_pool)


def _variant(head):
    return 2 * (head // (N_HEADS // N_KV_HEADS)) + head % 2


def _own_block(shape=(BLOCK, BLOCK)):
    r = lax.broadcasted_iota(jnp.int32, shape, 0)
    i = lax.broadcasted_iota(jnp.int32, shape, 1)
    return r <= i


def _fold_band(own, from_own, from_prev):
    return jnp.where(own, from_own, from_prev)


def _scores_by_head(own_tiles, prev_tiles, q_tiles):
    stacks = [jnp.concatenate(q_tiles[:2], axis=0), jnp.concatenate(q_tiles[2:], axis=0)]
    by_var = [_dot_nt(jnp.concatenate([own_tiles[v], prev_tiles[v]], axis=0), stacks[v // 2])
              for v in range(2 * N_KV_HEADS)]
    quadrant = lambda h, rows: by_var[_variant(h)][rows * BLOCK:(rows + 1) * BLOCK,
                                                   ((h // 2) % 2) * BLOCK:((h // 2) % 2 + 1) * BLOCK]
    return [quadrant(h, 0) for h in range(N_HEADS)], [quadrant(h, 1) for h in range(N_HEADS)]


def _softmax_t(s, sink):
    m = jnp.maximum(jnp.max(s, axis=0, keepdims=True), sink)
    p = jnp.exp(s - m)
    p_sink = jnp.exp(sink - m)
    inv = 1.0 / (jnp.sum(p, axis=0, keepdims=True) + p_sink)
    return p * inv, p_sink * inv


def _attn_fwd(q, kz, vz, sinks, g_attn, y_pool, x, w_out, mine, post):
    S = q.shape[0]
    tq = ATTN_BLOCKS * BLOCK
    n_var = 2 * N_KV_HEADS

    def body(q_ref, kp_ref, kc_ref, vp_ref, vc_ref, sinks_ref, gain_ref, yp_ref, x_ref, w_land, w_own, mine_ref,
             post_ref, o_ref, out_ref, m_ref, y_ref, w_v, sem):
        _load_once(_gathered(w_land, w_own, w_v, mine_ref[0], rows=D_MODEL // N_SHARD), sem)
        step = pl.program_id(0)
        own = _own_block()
        zero = jnp.zeros((BLOCK, BLOCK), F32)

        def tiles(cur_ref, prev_ref, j):
            rows = lambda jj: slice(jj * BLOCK, (jj + 1) * BLOCK)
            return ([cur_ref[v, rows(j), :] for v in range(n_var)],
                    [prev_ref[v] if j == 0 else cur_ref[v, rows(j - 1), :] for v in range(n_var)])

        scores = []
        for j in range(ATTN_BLOCKS):
            q_pairs = [q_ref[j * BLOCK:(j + 1) * BLOCK, i * _LANES:(i + 1) * _LANES] for i in range(N_HEADS // 2)]
            scores.append(_scores_by_head(*tiles(kc_ref, kp_ref, j), q_pairs))
        probs = []
        for j in range(ATTN_BLOCKS):
            s_own, s_prev = scores[j]
            no_prev = jnp.where(step > 0, 0.0, NEG_INF) if j == 0 else 0.0
            p_own, p_prev = [], []
            for h in range(N_HEADS):
                p, _ = _softmax_t(_fold_band(own, s_own[h], s_prev[h] + no_prev), sinks_ref[0, h])
                p_own.append(jnp.where(own, p, zero).astype(BF16))
                p_prev.append(jnp.where(own, zero, p).astype(BF16))
            probs.append((p_own, p_prev))
        blocks = []
        for j in range(ATTN_BLOCKS):
            p_own, p_prev = probs[j]
            v_own, v_prev = tiles(vc_ref, vp_ref, j)
            pairs = []
            for i in range(N_HEADS // 2):
                acc = None
                for h in (2 * i, 2 * i + 1):
                    part = _dot_tn(p_own[h], v_own[_variant(h)]) + _dot_tn(p_prev[h], v_prev[_variant(h)])
                    acc = part if acc is None else acc + part
                pairs.append(acc)
            blocks.append(jnp.concatenate(pairs, axis=1))
        o = jnp.concatenate(blocks, axis=0)
        o_ref[...] = o
        y_ref[:, :POOL_WIDTH] = yp_ref[...]
        y_ref[:, POOL_WIDTH:] = ((o * _rstd(o)) * gain_ref[...]).astype(BF16)
        m = _dot(y_ref[...], w_v[...])
        m_ref[...] = m
        out_ref[...] = x_ref[...] + (m * _rstd(m)) * post_ref[...]

    prev = pl.BlockSpec((n_var, BLOCK, KV_WIDTH), lambda g: (0, jnp.maximum(g * ATTN_BLOCKS - 1, 0), 0))
    cur = pl.BlockSpec((n_var, tq, KV_WIDTH), lambda g: (0, g, 0))
    return pl.pallas_call(
        body, name="attn_fwd", grid=(S // tq,),
        in_specs=[_rows(tq, ATTN_WIDTH), prev, cur, prev, cur,
                  pl.BlockSpec(memory_space=pltpu.SMEM), _full((1, ATTN_WIDTH)),
                  _rows(tq, POOL_WIDTH), _rows(tq, D_MODEL), _ANY, _ANY, _SMEM, _full((1, D_MODEL))],
        out_specs=[_rows(tq, ATTN_WIDTH), _rows(tq, D_MODEL), _rows(tq, D_MODEL), _rows(tq, D_MODEL)],
        out_shape=[jax.ShapeDtypeStruct((S, ATTN_WIDTH), F32), jax.ShapeDtypeStruct((S, D_MODEL), F32),
                   jax.ShapeDtypeStruct((S, D_MODEL), F32), jax.ShapeDtypeStruct((S, D_MODEL), BF16)],
        scratch_shapes=[pltpu.VMEM((D_MODEL, D_MODEL), BF16), pltpu.SemaphoreType.DMA((N_SHARD,))],
        compiler_params=_params(dimension_semantics=("arbitrary",)),
    )(q, kz, kz, vz, vz, sinks, g_attn, y_pool, x, *w_out, mine, post)


def _attn_bwd(dout, m, w_out, mine, post, o, q, kz, vz, sinks, g_attn, deps=()):
    S = q.shape[0]
    tq = ATTN_BLOCKS * BLOCK
    n_var = 2 * N_KV_HEADS

    def body(dout_ref, m_ref, w_land, w_own, mine_ref, post_ref, o_ref, q_ref, kp_ref, kc_ref, vp_ref, vc_ref,
             sinks_ref, gain_ref, dyp_ref, dm_ref, dpost_ref, dq_ref, dk_ref, dv_ref, dkx_ref, dvx_ref, dsink_ref,
             dgain_ref, w_v, sem):
        _load_once(_gathered(w_land, w_own, w_v, mine_ref[0], rows=D_MODEL // N_SHARD), sem)
        step = pl.program_id(0)

        @pl.when(step == 0)
        def _():
            dpost_ref[...] = jnp.zeros_like(dpost_ref)
            dsink_ref[...] = jnp.zeros_like(dsink_ref)
            dgain_ref[...] = jnp.zeros_like(dgain_ref)

        mv = m_ref[...]
        rm = _rstd(mv)
        mn = mv * rm
        dres = dout_ref[...]
        dpost_ref[...] += jnp.sum(dres * mn, axis=0, keepdims=True)
        dm = _norm_bwd(dres, mn, rm, post_ref[...]).astype(BF16)
        dm_ref[...] = dm
        dy = _dot_nt(dm, w_v[...])
        dyp_ref[...] = dy[:, :POOL_WIDTH]
        ov = o_ref[...]
        r = _rstd(ov)
        on = ov * r
        dyv = dy[:, POOL_WIDTH:]
        dgain_ref[...] += jnp.sum(dyv * on, axis=0, keepdims=True)
        do = _norm_bwd(dyv, on, r, gain_ref[...]).astype(BF16)
        own = _own_block()
        zero = jnp.zeros((BLOCK, BLOCK), F32)
        split = lambda t: (jnp.where(own, t, zero).astype(BF16), jnp.where(own, zero, t).astype(BF16))
        rows = lambda j: slice(j * BLOCK, (j + 1) * BLOCK)
        heads = range(N_HEADS)

        def tiles(cur_ref, prev_ref, j):
            return ([cur_ref[v, rows(j), :] for v in range(n_var)],
                    [prev_ref[v] if j == 0 else cur_ref[v, rows(j - 1), :] for v in range(n_var)])

        q_pairs = [[q_ref[rows(j), i * _LANES:(i + 1) * _LANES] for i in range(N_HEADS // 2)] for j in range(ATTN_BLOCKS)]
        do_pairs = [[do[rows(j), i * _LANES:(i + 1) * _LANES] for i in range(N_HEADS // 2)] for j in range(ATTN_BLOCKS)]
        scores = [(_scores_by_head(*tiles(kc_ref, kp_ref, j), q_pairs[j]),
                   _scores_by_head(*tiles(vc_ref, vp_ref, j), do_pairs[j])) for j in range(ATTN_BLOCKS)]
        parts, sink_sum = [], None
        for j in range(ATTN_BLOCKS):
            (s_own, s_prev), (dp_own, dp_prev) = scores[j]
            no_prev = jnp.where(step > 0, 0.0, NEG_INF) if j == 0 else 0.0
            ds_parts, p_parts, sink_rows = [], [], []
            for h in heads:
                p, p_sink = _softmax_t(_fold_band(own, s_own[h], s_prev[h] + no_prev), sinks_ref[0, h])
                dp = _fold_band(own, dp_own[h], dp_prev[h])
                delta = jnp.sum(p * dp, axis=0, keepdims=True)
                ds_parts.append(split(p * (dp - delta)))
                p_parts.append(split(p))
                sink_rows.append(jnp.zeros((1, _LANES), F32) - jnp.sum(p_sink * delta))
            block_sinks = jnp.concatenate(sink_rows, axis=0)
            sink_sum = block_sinks if sink_sum is None else sink_sum + block_sinks
            parts.append((ds_parts, p_parts))
        dsink_ref[...] += sink_sum
        low = lax.broadcasted_iota(jnp.int32, (BLOCK, _LANES), 1) < HEAD_DIM

        def merge(acc):
            return jnp.where(low, acc[0] + pltpu.roll(acc[1], HEAD_DIM, axis=1),
                             acc[3] + pltpu.roll(acc[2], HEAD_DIM, axis=1))
        add = lambda acc, var, t: acc.__setitem__(var, t if acc[var] is None else acc[var] + t)
        k_own, k_prev, v_own, v_prev = [], [], [], []
        for j in range(ATTN_BLOCKS):
            ds_parts, p_parts = parts[j]
            kt_own, kt_prev = tiles(kc_ref, kp_ref, j)
            dk_own, dk_prev, dv_own, dv_prev = ([None] * n_var for _ in range(4))
            for i in range(N_HEADS // 2):
                dq_pair = None
                for h in (2 * i, 2 * i + 1):
                    var = _variant(h)
                    (ds_o, ds_p), (p_o, p_p) = ds_parts[h], p_parts[h]
                    part = _dot_tn(ds_o, kt_own[var]) + _dot_tn(ds_p, kt_prev[var])
                    dq_pair = part if dq_pair is None else dq_pair + part
                    add(dk_own, var, _dot(ds_o, q_pairs[j][i]))
                    add(dk_prev, var, _dot(ds_p, q_pairs[j][i]))
                    add(dv_own, var, _dot(p_o, do_pairs[j][i]))
                    add(dv_prev, var, _dot(p_p, do_pairs[j][i]))
                dq_ref[rows(j), i * _LANES:(i + 1) * _LANES] = dq_pair * ATTN_SCALE
            k_own.append(merge(dk_own))
            k_prev.append(merge(dk_prev))
            v_own.append(merge(dv_own))
            v_prev.append(merge(dv_prev))
        for j in range(ATTN_BLOCKS):
            last = j == ATTN_BLOCKS - 1
            dk_ref[rows(j), :] = k_own[j] if last else k_own[j] + k_prev[j + 1]
            dv_ref[rows(j), :] = v_own[j] if last else v_own[j] + v_prev[j + 1]
        dkx_ref[...] = k_prev[0]
        dvx_ref[...] = v_prev[0]

    steps = S // tq
    prev = pl.BlockSpec((n_var, BLOCK, KV_WIDTH), lambda g: (0, jnp.maximum(g * ATTN_BLOCKS - 1, 0), 0))
    cur = pl.BlockSpec((n_var, tq, KV_WIDTH), lambda g: (0, g, 0))
    nxt = pl.BlockSpec((None, BLOCK, KV_WIDTH), lambda g: (g, 0, 0))
    args = [dout, m, *w_out, mine, post, o, q, kz, kz, vz, vz, sinks, g_attn]
    return pl.pallas_call(
        _ignoring(body, len(args), len(deps)), name="attn_bwd", grid=(steps,),
        in_specs=[_rows(tq, D_MODEL), _rows(tq, D_MODEL), _ANY, _ANY, _SMEM, _full((1, D_MODEL)),
                  _rows(tq, ATTN_WIDTH), _rows(tq, ATTN_WIDTH), prev, cur, prev, cur,
                  pl.BlockSpec(memory_space=pltpu.SMEM), _full((1, ATTN_WIDTH))] + [_ANY] * len(deps),
        out_specs=[_rows(tq, POOL_WIDTH), _rows(tq, D_MODEL), _full((1, D_MODEL)),
                   _rows(tq, ATTN_WIDTH), _rows(tq, KV_WIDTH), _rows(tq, KV_WIDTH), nxt, nxt,
                   _full((N_HEADS, _LANES)), _full((1, ATTN_WIDTH))],
        out_shape=[jax.ShapeDtypeStruct((S, POOL_WIDTH), F32), jax.ShapeDtypeStruct((S, D_MODEL), BF16),
                   jax.ShapeDtypeStruct((1, D_MODEL), F32),
                   jax.ShapeDtypeStruct((S, ATTN_WIDTH), F32), jax.ShapeDtypeStruct((S, KV_WIDTH), F32),
                   jax.ShapeDtypeStruct((S, KV_WIDTH), F32),
                   jax.ShapeDtypeStruct((steps, BLOCK, KV_WIDTH), F32), jax.ShapeDtypeStruct((steps, BLOCK, KV_WIDTH), F32),
                   jax.ShapeDtypeStruct((N_HEADS, _LANES), F32), jax.ShapeDtypeStruct((1, ATTN_WIDTH), F32)],
        scratch_shapes=[pltpu.VMEM((D_MODEL, D_MODEL), BF16), pltpu.SemaphoreType.DMA((N_SHARD,))],
        compiler_params=_params(dimension_semantics=("arbitrary",)),
    )(*args, *deps)


def _inv_freq_row():
    inv_freq = ROPE_THETA ** (-jnp.arange(0, ROT_DIM, 2, dtype=F32) / ROT_DIM)
    per_head = jnp.concatenate([inv_freq, inv_freq, jnp.zeros((HEAD_DIM - ROT_DIM,), F32)])
    return jnp.tile(per_head, _LANES // HEAD_DIM).reshape(1, _LANES)


def _local_step(x, pos, target, small, mine, weights_of, grads_ready):
    rope = _rope_tables(pos, _inv_freq_row())
    wgu1, wd1 = weights_of("ffn1", (rope,))
    x1, g1, u1, f1 = _ffn_fwd(x, small["ffn1_pre"], small["ffn1_post"], wgu1, wd1, mine)
    w_in_t, w_out = weights_of("mixer", (x1,))
    u, q, k, v = _mixer_in_fwd(x1, small["mix_pre"], w_in_t, mine, rope)
    y_pool = _pool_fwd(u, small["w_pool"], small["pool_scale"], small["g_pool"])
    o, x2, m, y = _attn_fwd(q, k, v, small["sinks"], small["g_attn"], y_pool, x1, w_out, mine, small["mix_post"])
    wgu2, wd2 = weights_of("ffn2", (x2,))
    dx3, g2, u2, f2, loss_acc = _ffn_fwd(x2, small["ffn2_pre"], small["ffn2_post"], wgu2, wd2, mine, target=target)
    grads = {"loss": loss_acc * (0.5 / D_MODEL)}
    dx2, h3, dgu2, a2, df2, grads["ffn2_pre"], grads["ffn2_post"] = _ffn_bwd(
        dx3, x2, f2, g2, u2, small["ffn2_pre"], small["ffn2_post"], wgu2, wd2, mine)
    dwgu2 = _wgrad(h3, dgu2, D_MODEL, FF_CHUNK, "wgrad_gu2", column_shards=True)
    dwd2 = _wgrad(a2, df2, FF_CHUNK, D_MODEL, "wgrad_down2")
    deps = grads_ready("ffn2", {"ffn2_w_gu": dwgu2, "ffn2_w_down": dwd2})
    dy_pool, dm, grads["mix_post"], dq, dk, dv, dk_next, dv_next, dsinks, grads["g_attn"] = _attn_bwd(
        dx2, m, w_out, mine, small["mix_post"], o, q, k, v, small["sinks"], small["g_attn"], deps=deps)
    dw_out = _wgrad(y, dm, D_MODEL, D_MODEL, "wgrad_out")
    grads["sinks"] = dsinks[:, 0].reshape(1, N_HEADS)
    du, grads["w_pool"], grads["pool_scale"], grads["g_pool"] = _pool_bwd(
        dy_pool, u, small["w_pool"], small["pool_scale"], small["g_pool"])
    dx1, dz, h2, grads["mix_pre"] = _mixer_in_bwd(dx2, x1, small["mix_pre"], w_in_t, mine, du, dq, dk, dv, dk_next, dv_next, rope)
    dw_in_t = _wgrad(dz, h2, IN_WIDTH, D_MODEL, "wgrad_in")
    deps = grads_ready("mixer", {"w_in": dw_in_t, "w_out": dw_out})
    dx, h1, dgu1, a1, df1, grads["ffn1_pre"], grads["ffn1_post"] = _ffn_bwd(
        dx1, x, f1, g1, u1, small["ffn1_pre"], small["ffn1_post"], wgu1, wd1, mine, deps=deps)
    dwgu1 = _wgrad(h1, dgu1, D_MODEL, FF_CHUNK, "wgrad_gu1", column_shards=True)
    deps = grads_ready("ffn1_gu", {"ffn1_w_gu": dwgu1}, small=grads)
    dwd1 = _wgrad(a1, df1, FF_CHUNK, D_MODEL, "wgrad_down1", deps=deps)
    grads_ready("ffn1_down", {"ffn1_w_down": dwd1})
    return dx


def _place():
    return lax.axis_index("x"), lax.axis_index("y"), lax.axis_index("c")


def _other_chips(x, y):
    return [(1 - x, y), (x, 1 - y), (1 - x, 1 - y)]


_HBM = pl.BlockSpec(memory_space=pltpu.HBM)
_SEM = pl.BlockSpec(memory_space=pltpu.SEMAPHORE)
_EFFECT = pltpu.SideEffectType.DATAFLOW_SIDE_EFFECTING
GATHER, GATHER_HALF, REDUCE, BROADCAST = "gather", "gather_half", "reduce", "broadcast"
N_DEVICES = 8


def _in_hbm(a):
    return pltpu.with_memory_space_constraint(a, pltpu.HBM)


def _core_half(rows, c):
    return pl.ds(pl.multiple_of(c * (rows // 2), 16), rows // 2)


def _chip_copies(kind, srcs, lands, send_sems, recv_sems):
    x, y, c = _place()
    mine = 2 * x + y
    copies = []
    for w in range(len(srcs)):
        if kind == BROADCAST:
            peers = [(x ^ (k >> 2), y ^ ((k >> 1) & 1), c ^ (k & 1)) for k in range(1, N_DEVICES)]
        else:
            peers = [(px, py, c) for px, py in _other_chips(x, y)]
        for k, (px, py, pc) in enumerate(peers):
            if kind == GATHER:
                src, dst = srcs[w], lands[w].at[mine]
            elif kind == GATHER_HALF:
                half = _core_half(srcs[w].shape[0], c)
                src, dst = srcs[w].at[half, :], lands[w].at[mine, half, :]
            elif kind == BROADCAST:
                src, dst = srcs[w], lands[w].at[2 * mine + c]
            else:
                src, dst = srcs[w].at[2 * px + py], lands[w].at[k]
            pair = len(peers) * w + k
            copies.append(pltpu.make_async_remote_copy(
                src_ref=src, dst_ref=dst, send_sem=send_sems.at[pair], recv_sem=recv_sems.at[pair],
                device_id=(px, py, pc), device_id_type=MESH))
    return copies


def _landing_shape(kind, src):
    if kind == REDUCE:
        return (N_SHARD - 1,) + src.shape[1:]
    return ((N_DEVICES if kind == BROADCAST else N_SHARD),) + src.shape


def _peer_count(kind):
    return N_DEVICES - 1 if kind == BROADCAST else N_SHARD - 1


def _exchange_start(kinds, groups, name):
    sizes = [len(g) for g in groups]
    flat = [s for g in groups for s in g]
    n, ng = len(flat), len(groups)

    def body(*refs):
        srcs, lands = refs[:n], refs[n:2 * n]
        sems = refs[2 * n:2 * n + 2 * ng]
        token = refs[-1]
        start = 0
        for gi, size in enumerate(sizes):
            for cp in _chip_copies(kinds[gi], srcs[start:start + size], lands[start:start + size],
                                   sems[2 * gi], sems[2 * gi + 1]):
                cp.start()
            start += size
        token[...] = jnp.zeros_like(token)

    landings = [lax.empty(_landing_shape(kind, s), s.dtype) for kind, g in zip(kinds, groups) for s in g]
    sem_shapes = [pltpu.SemaphoreType.DMA((size * _peer_count(kind),)) for kind, size in zip(kinds, sizes)
                  for _ in range(2)]
    outs = pl.pallas_call(
        body, name=name,
        in_specs=[_HBM] * (2 * n),
        out_specs=[_SEM] * (2 * ng) + [_HBM] * (2 * n) + [pl.BlockSpec(memory_space=pltpu.VMEM)],
        out_shape=sem_shapes + [pltpu.HBM(a.shape, a.dtype) for a in flat + landings]
        + [jax.ShapeDtypeStruct((8, _LANES), F32)],
        input_output_aliases={i: 2 * ng + i for i in range(2 * n)},
        compiler_params=pltpu.CompilerParams(has_side_effects=_EFFECT),
    )(*[_in_hbm(a) for a in flat + landings])
    sems, srcs, lands, token = outs[:2 * ng], outs[2 * ng:2 * ng + n], outs[2 * ng + n:2 * ng + 2 * n], outs[-1]
    handles, start = [], 0
    for gi, size in enumerate(sizes):
        handles.append((sems[2 * gi], sems[2 * gi + 1], srcs[start:start + size], lands[start:start + size]))
        start += size
    return handles, token


def _exchange_wait(kind, handle, after, name):
    send_sems, recv_sems, srcs, lands = handle
    n = len(srcs)

    def body(*refs):
        copies = _chip_copies(kind, refs[:n], refs[n:2 * n], refs[2 * n], refs[2 * n + 1])
        for cp in copies:
            cp.wait_send()
        for cp in copies:
            cp.wait_recv()

    outs = pl.pallas_call(
        body, name=name,
        in_specs=[_HBM] * (2 * n) + [_SEM, _SEM] + [_ANY] * len(after),
        out_specs=[_HBM] * (2 * n),
        out_shape=[pltpu.HBM(a.shape, a.dtype) for a in list(srcs) + list(lands)],
        input_output_aliases={i: i for i in range(2 * n)},
        compiler_params=pltpu.CompilerParams(has_side_effects=_EFFECT),
    )(*srcs, *lands, send_sems, recv_sems, *after)
    return outs[:n], outs[n:]


def _swap_gathered_halves(lands, name):
    n = len(lands)

    def body(*refs):
        bufs = refs[n:2 * n]
        send_sems, recv_sems = refs[2 * n:]
        x, y, c = _place()
        mine = 2 * x + y
        sends, arrivals = [], []
        for w in range(n):
            rows = bufs[w].shape[1]
            for d in range(1, N_SHARD):
                slot = (mine + d) % N_SHARD
                sems = dict(send_sem=send_sems.at[(N_SHARD - 1) * w + d - 1],
                            recv_sem=recv_sems.at[(N_SHARD - 1) * w + d - 1],
                            device_id=(x, y, 1 - c), device_id_type=MESH)
                fetched = bufs[w].at[slot, _core_half(rows, c), :]
                missing = bufs[w].at[slot, _core_half(rows, 1 - c), :]
                sends.append(pltpu.make_async_remote_copy(src_ref=fetched, dst_ref=fetched, **sems))
                arrivals.append(pltpu.make_async_remote_copy(src_ref=missing, dst_ref=missing, **sems))
        for cp in sends:
            cp.start()
        for cp in arrivals:
            cp.wait_recv()
        for cp in sends:
            cp.wait_send()

    return pl.pallas_call(
        body, name=name, in_specs=[_ANY] * n, out_specs=[_ANY] * n,
        out_shape=[jax.ShapeDtypeStruct(a.shape, a.dtype) for a in lands],
        input_output_aliases={i: i for i in range(n)},
        scratch_shapes=[pltpu.SemaphoreType.DMA((n * (N_SHARD - 1),)), pltpu.SemaphoreType.DMA((n * (N_SHARD - 1),))],
        compiler_params=pltpu.CompilerParams(has_side_effects=True),
    )(*lands)


def _swap_with_sibling(partials, name):
    n = len(partials)

    def body(*refs):
        ins, outs = refs[:n], refs[n:2 * n]
        send_sems, recv_sems = refs[2 * n:]
        x, y, c = _place()
        sends = [pltpu.make_async_remote_copy(
            src_ref=ins[w], dst_ref=outs[w], send_sem=send_sems.at[w], recv_sem=recv_sems.at[w],
            device_id=(x, y, 1 - c), device_id_type=MESH) for w in range(n)]
        for cp in sends:
            cp.start()
        for cp in sends:
            cp.wait_recv()
        for cp in sends:
            cp.wait_send()

    return pl.pallas_call(
        body, name=name,
        in_specs=[_ANY] * n, out_specs=[_ANY] * n,
        out_shape=[jax.ShapeDtypeStruct(p.shape, p.dtype) for p in partials],
        scratch_shapes=[pltpu.SemaphoreType.DMA((n,)), pltpu.SemaphoreType.DMA((n,))],
        compiler_params=pltpu.CompilerParams(has_side_effects=True),
    )(*partials)


def _row_block(rows, cap):
    best = None
    for cand in range(16, min(rows, cap) + 1, 16):
        if rows % cand == 0:
            best = cand
    assert best is not None, rows
    return best


def _chip_partial(own, received, shard, name):
    _, R, C = own.shape
    rb = _row_block(R, 512)

    def body(shard_ref, own_ref, rec_ref, out_ref):
        acc = own_ref[...]
        for k in range(3):
            acc = acc + rec_ref[k].astype(F32)
        out_ref[...] = acc.astype(BF16)

    return pl.pallas_call(
        body, name=name,
        grid_spec=pltpu.PrefetchScalarGridSpec(
            num_scalar_prefetch=1, grid=(R // rb,),
            in_specs=[pl.BlockSpec((None, rb, C), lambda i, s: (s[0], i, 0)),
                      pl.BlockSpec((3, rb, C), lambda i, s: (0, i, 0))],
            out_specs=pl.BlockSpec((rb, C), lambda i, s: (i, 0))),
        out_shape=jax.ShapeDtypeStruct((R, C), BF16),
        compiler_params=_params(dimension_semantics=("arbitrary",)),
    )(shard, own, received)


def _adamw(w, m, v, g_parts, name, slot=None):
    R, C = w.shape
    by_device = slot is not None
    rb = _row_block(R, 256) if R % 16 == 0 else R

    def body(w_ref, m_ref, v_ref, *refs):
        g_refs, (grad_ref, delta_ref, m_out, v_out) = refs[:-4], refs[-4:]
        if by_device:
            own_ref, land_ref, slot_ref = g_refs
            part = lambda d: jnp.where(slot_ref[0] == d, own_ref[...], land_ref[d])
            g = part(0)
            for d in range(1, N_DEVICES):
                g = g + part(d)
        else:
            g = g_refs[0][...].astype(F32)
            for g_ref in g_refs[1:]:
                g = g + g_ref[...].astype(F32)
        grad_ref[...] = g
        new_m = ADAM_B1 * m_ref[...] + (1.0 - ADAM_B1) * g
        new_v = ADAM_B2 * v_ref[...] + (1.0 - ADAM_B2) * (g * g)
        m_hat = new_m / (1.0 - ADAM_B1 ** ADAM_STEP)
        v_hat = new_v / (1.0 - ADAM_B2 ** ADAM_STEP)
        delta_ref[...] = -ADAM_LR * (m_hat / (jnp.sqrt(v_hat) + ADAM_EPS) + ADAM_WD * w_ref[...])
        m_out[...] = new_m
        v_out[...] = new_v

    spec = pl.BlockSpec((rb, C), lambda i: (i, 0))
    if by_device:
        g_specs = [spec, pl.BlockSpec((N_DEVICES, rb, C), lambda i: (0, i, 0)), _SMEM]
        g_parts = list(g_parts) + [slot]
    else:
        g_specs = [spec] * len(g_parts)
    return pl.pallas_call(
        body, name=name, grid=(R // rb,),
        in_specs=[spec, spec, spec] + g_specs,
        out_specs=[spec] * 4,
        out_shape=[jax.ShapeDtypeStruct((R, C), F32)] * 4,
        compiler_params=_params(dimension_semantics=("arbitrary",)),
    )(w, m, v, *g_parts)


SMALL_NAMES = ("ffn1_pre", "ffn1_post", "mix_pre", "pool_scale", "sinks", "g_pool", "g_attn", "mix_post",
               "ffn2_pre", "ffn2_post", "w_pool")
_SLAB_PART = 8 * _LANES


SLAB_NAMES = SMALL_NAMES + ("loss",)


def _to_slab(parts):
    rows = []
    for name in SLAB_NAMES:
        flat = parts[name].reshape(-1) if name in parts else jnp.zeros((_SLAB_PART,), F32)
        padded = -(-flat.shape[0] // _SLAB_PART) * _SLAB_PART
        rows.append(jnp.pad(flat, (0, padded - flat.shape[0])).reshape(-1, _LANES))
    return jnp.concatenate(rows, axis=0)


def _from_slab(slab, like):
    out, row = {}, 0
    for name in SLAB_NAMES:
        size = like[name].size
        rows = -(-size // _SLAB_PART) * (_SLAB_PART // _LANES)
        out[name] = slab[row:row + rows].reshape(-1)[:size].reshape(like[name].shape)
        row += rows
    return out


BIG_NAMES = ("ffn1_w_gu", "ffn1_w_down", "w_in", "w_out", "ffn2_w_gu", "ffn2_w_down")
WEIGHT_ORDER = ("ffn1_pre", "ffn1_w_gu", "ffn1_w_down", "ffn1_post", "mix_pre", "w_in", "w_pool", "pool_scale",
                "sinks", "g_pool", "g_attn", "w_out", "mix_post", "ffn2_pre", "ffn2_w_gu", "ffn2_w_down", "ffn2_post")


def kernel(x, positions, ffn1_pre, ffn1_w_gu, ffn1_w_down, ffn1_post, mix_pre, w_in, w_pool, pool_scale, sinks, g_pool, g_attn, w_out, mix_post, ffn2_pre, ffn2_w_gu, ffn2_w_down, ffn2_post, loss_target, m_ffn1_pre, m_ffn1_w_gu, m_ffn1_w_down, m_ffn1_post, m_mix_pre, m_w_in, m_w_pool, m_pool_scale, m_sinks, m_g_pool, m_g_attn, m_w_out, m_mix_post, m_ffn2_pre, m_ffn2_w_gu, m_ffn2_w_down, m_ffn2_post, v_ffn1_pre, v_ffn1_w_gu, v_ffn1_w_down, v_ffn1_post, v_mix_pre, v_w_in, v_w_pool, v_pool_scale, v_sinks, v_g_pool, v_g_attn, v_w_out, v_mix_post, v_ffn2_pre, v_ffn2_w_gu, v_ffn2_w_down, v_ffn2_post):
    given = dict(locals())
    weights = {n: given[n][0] for n in WEIGHT_ORDER}
    moments_m = {n: given["m_" + n][0] for n in WEIGHT_ORDER}
    moments_v = {n: given["v_" + n][0] for n in WEIGHT_ORDER}
    S = x.shape[1]
    shard = (2 * lax.axis_index("x") + lax.axis_index("y")).astype(jnp.int32).reshape(1)

    local16 = {n: weights[n].astype(BF16) for n in BIG_NAMES if n != "w_in"}
    local16["w_in"] = weights["w_in"].T.astype(BF16)
    gather_groups = {"ffn1": ("ffn1_w_gu", "ffn1_w_down"), "mixer": ("w_in", "w_out"),
                     "ffn2": ("ffn2_w_gu", "ffn2_w_down")}
    gather_kinds = {"ffn1": GATHER_HALF, "mixer": GATHER, "ffn2": GATHER}
    handles, _ = _exchange_start(list(gather_kinds.values()),
                                 [[local16[n] for n in names] for names in gather_groups.values()], "gather_start")
    gather_handles = dict(zip(gather_groups, handles))

    def weights_of(group, after):
        kind = gather_kinds[group]
        owns, lands = _exchange_wait(kind, gather_handles[group], list(after), "gather_wait_" + group)
        if kind == GATHER_HALF:
            lands = _swap_gathered_halves(lands, "swap_gathered_" + group)
        return list(zip(lands, owns))

    pending, last_token = {}, []

    def grads_ready(group, grads, small=None):
        names = list(grads)
        kinds, sources = [REDUCE], [[grads[n][1] for n in names]]
        if small is not None:
            kinds, sources = kinds + [BROADCAST], sources + [[_to_slab(small)]]
        handles, token = _exchange_start(kinds, sources, "reduce_start_" + group)
        handle = handles[0]
        if small is not None:
            pending["small"] = handles[1]
        pending[group] = (names, handle, [grads[n][0] for n in names])
        last_token[:] = [token]
        return [token]

    small = {n: (weights[n] if weights[n].ndim > 1 else weights[n].reshape(1, -1)) for n in SMALL_NAMES}
    dx = _local_step(x[0], positions.reshape(S, 1), loss_target[0], small, shard, weights_of, grads_ready)

    grad, delta, new_m, new_v = {}, {}, {}, {}

    def finish(groups, after):
        names, partials = [], []
        for group in groups:
            group_names, handle, own32 = pending[group]
            _, received = _exchange_wait(REDUCE, handle, after, "reduce_wait_" + group)
            names += group_names
            partials += [_chip_partial(g32, rec, shard, "chip_partial_" + n)
                         for n, g32, rec in zip(group_names, own32, received)]
        siblings = _swap_with_sibling(partials, "swap_" + groups[0])
        for name, mine, theirs in zip(names, partials, siblings):
            if name == "w_in":
                mine, theirs = mine.T, theirs.T
            grad[name], delta[name], new_m[name], new_v[name] = _adamw(
                weights[name], moments_m[name], moments_v[name], [mine, theirs], "adamw_" + name)
        return [grad[names[-1]]]

    after = finish(["ffn2"], last_token)
    after = finish(["mixer"], after)
    (own_slab,), (slab_landing,) = _exchange_wait(BROADCAST, pending["small"], after, "reduce_wait_small")
    device = (2 * shard + lax.axis_index("c")).astype(jnp.int32)
    small_like = dict({n: small[n] for n in SMALL_NAMES}, loss=jnp.zeros((8, _LANES), F32))
    slabs = _adamw(_to_slab(small), _to_slab({n: moments_m[n] for n in SMALL_NAMES}),
                   _to_slab({n: moments_v[n] for n in SMALL_NAMES}), [own_slab, slab_landing], "adamw_small",
                   slot=device)
    for store, slab in zip((grad, delta, new_m, new_v), slabs):
        store.update(_from_slab(slab, small_like))
    loss = grad["loss"][0, 0]
    after = finish(["ffn1_gu"], [slabs[0]])
    finish(["ffn1_down"], after)

    def out(store):
        return [store[n].reshape(given[n].shape) for n in WEIGHT_ORDER]
    return (loss, dx[None], *out(grad), *out(delta), *out(new_m), *out(new_v))
```

```python
import jax
import jax.numpy as jnp
from jax import lax
from jax.experimental import pallas as pl
from jax.experimental.pallas import tpu as pltpu

F32 = jnp.float32
BF16 = jnp.bfloat16

D_MODEL = 1024
D_FF = 2816
N_SHARD = 4
FF_CHUNK = D_FF // 2
POOL_WINDOWS = (2, 4, 8, 16)
POOL_WIDTH = 512
POOL_GROUP = 128
HALO = 16
HEAD_DIM = 64
N_HEADS = 8
N_KV_HEADS = 2
ATTN_WIDTH = 512
KV_WIDTH = 128
IN_WIDTH = 1280
BLOCK = 128
ATTN_BLOCKS = 4
ROT_DIM = 16
ROPE_THETA = 500000.0
EPS = 1e-6
NEG_INF = -1e30
ATTN_SCALE = HEAD_DIM ** -0.5

ADAM_LR = 0.001
ADAM_B1 = 0.9
ADAM_B2 = 0.999
ADAM_EPS = 1e-08
ADAM_WD = 0.01
ADAM_STEP = 10

VMEM_LIMIT = 60 * 1024 * 1024
FFN_FWD_TILE = 512
FFN_BWD_TILE = 256
MIXER_TILE = 512

MESH = pl.DeviceIdType.MESH


def _params(**kw):
    return pltpu.CompilerParams(vmem_limit_bytes=VMEM_LIMIT, **kw)


def _dot(a, b):
    return jnp.dot(a, b, preferred_element_type=F32)


def _dot_nt(a, b):
    return lax.dot_general(a, b, (((1,), (1,)), ((), ())), preferred_element_type=F32)


def _dot_tn(a, b):
    return lax.dot_general(a, b, (((0,), (0,)), ((), ())), preferred_element_type=F32)


def _rstd(x):
    return lax.rsqrt(jnp.mean(x * x, axis=-1, keepdims=True) + EPS)


def _norm_bwd(dy, xn, r, gain):
    dxn = dy * gain
    return r * (dxn - xn * jnp.mean(dxn * xn, axis=-1, keepdims=True))


def _sigmoid(x):
    return 1.0 / (1.0 + jnp.exp(-x))


def _full(shape):
    return pl.BlockSpec(shape, lambda *_: (0,) * len(shape))


def _rows(tile, width):
    return pl.BlockSpec((tile, width), lambda i: (i, 0))


_ANY = pl.BlockSpec(memory_space=pl.ANY)


_SMEM = pl.BlockSpec(memory_space=pltpu.SMEM)


def _load_once(pairs, sem):
    @pl.when(pl.program_id(0) == 0)
    def _():
        copies = [pltpu.make_async_copy(src, dst, sem.at[n]) for n, (src, dst) in enumerate(pairs)]
        for cp in copies:
            cp.start()
        for cp in copies:
            cp.wait()


def _gathered(land_ref, own_ref, vmem_ref, mine, rows=None):
    def dst(slot):
        if rows is None:
            return vmem_ref.at[slot]
        return vmem_ref.at[pl.ds(pl.multiple_of(slot * rows, 16), rows), :]
    pairs = [(land_ref.at[(mine + d) % N_SHARD], dst((mine + d) % N_SHARD)) for d in range(1, N_SHARD)]
    return pairs + [(own_ref, dst(mine))]


def _ignoring(body, start, count):
    def wrapped(*refs):
        return body(*refs[:start], *refs[start + count:])
    return wrapped


def _ffn_fwd(x, pre, post, wgu, wd, mine, target=None, deps=()):
    S = x.shape[0]
    tm = FFN_FWD_TILE
    with_loss = target is not None

    def body(*refs):
        if with_loss:
            (x_ref, pre_ref, post_ref, wgu_land, wgu_own, wd_land, wd_own, mine_ref, tgt_ref,
             out_ref, g_ref, u_ref, f_ref, loss_ref, wgu_v, wd_v, sem) = refs
        else:
            (x_ref, pre_ref, post_ref, wgu_land, wgu_own, wd_land, wd_own, mine_ref,
             out_ref, g_ref, u_ref, f_ref, wgu_v, wd_v, sem) = refs
        _load_once(_gathered(wgu_land, wgu_own, wgu_v, mine_ref[0])
                   + _gathered(wd_land, wd_own, wd_v, mine_ref[0], rows=D_FF // N_SHARD), sem)
        xv = x_ref[...]
        h = ((xv * _rstd(xv)) * pre_ref[...]).astype(BF16)
        facc = jnp.zeros((tm, D_MODEL), F32)
        for c in range(2):
            cols = slice(c * FF_CHUNK, (c + 1) * FF_CHUNK)
            g = _dot(h, wgu_v[c])
            u = _dot(h, wgu_v[2 + c])
            g_ref[:, cols] = g.astype(BF16)
            u_ref[:, cols] = u.astype(BF16)
            a = (g * _sigmoid(g)) * u
            facc = facc + _dot(a.astype(BF16), wd_v[cols, :])
        f_ref[...] = facc
        out = xv + 0.5 * ((facc * _rstd(facc)) * post_ref[...])
        if with_loss:
            diff = out - tgt_ref[...]
            out_ref[...] = diff * (1.0 / D_MODEL)

            @pl.when(pl.program_id(0) == 0)
            def _():
                loss_ref[...] = jnp.zeros_like(loss_ref)
            loss_ref[...] += jnp.sum(diff * diff)
        else:
            out_ref[...] = out

    in_specs = [_rows(tm, D_MODEL), _full((1, D_MODEL)), _full((1, D_MODEL)), _ANY, _ANY, _ANY, _ANY, _SMEM]
    args = [x, pre, post, *wgu, *wd, mine]
    out_shape = [jax.ShapeDtypeStruct((S, D_MODEL), F32), jax.ShapeDtypeStruct((S, D_FF), BF16),
                 jax.ShapeDtypeStruct((S, D_FF), BF16), jax.ShapeDtypeStruct((S, D_MODEL), F32)]
    out_specs = [_rows(tm, D_MODEL), _rows(tm, D_FF), _rows(tm, D_FF), _rows(tm, D_MODEL)]
    if with_loss:
        in_specs.append(_rows(tm, D_MODEL))
        args.append(target)
        out_shape.append(jax.ShapeDtypeStruct((8, 128), F32))
        out_specs.append(_full((8, 128)))
    return pl.pallas_call(
        _ignoring(body, len(args), len(deps)), name="ffn_fwd_loss" if with_loss else "ffn_fwd",
        grid=(S // tm,), in_specs=in_specs + [_ANY] * len(deps), out_specs=out_specs, out_shape=out_shape,
        scratch_shapes=[pltpu.VMEM((N_SHARD, D_MODEL, FF_CHUNK), BF16), pltpu.VMEM((D_FF, D_MODEL), BF16),
                        pltpu.SemaphoreType.DMA((2 * N_SHARD,))],
        compiler_params=_params(dimension_semantics=("arbitrary",)),
    )(*args, *deps)


def _ffn_bwd(dout, x, f, g, u, pre, post, wgu, wd, mine, deps=()):
    S = x.shape[0]
    tm = FFN_BWD_TILE

    def body(dout_ref, x_ref, f_ref, g_ref, u_ref, pre_ref, post_ref, wgu_land, wgu_own, wd_land, wd_own, mine_ref,
             dx_ref, h_ref, dgu_ref, a_ref, df_ref, dpre_ref, dpost_ref, wgu_v, wd_v, sem):
        _load_once(_gathered(wgu_land, wgu_own, wgu_v, mine_ref[0])
                   + _gathered(wd_land, wd_own, wd_v, mine_ref[0], rows=D_FF // N_SHARD), sem)

        @pl.when(pl.program_id(0) == 0)
        def _():
            dpre_ref[...] = jnp.zeros_like(dpre_ref)
            dpost_ref[...] = jnp.zeros_like(dpost_ref)

        dout_v = dout_ref[...]
        dn = 0.5 * dout_v
        fv = f_ref[...]
        rf = _rstd(fv)
        fn = fv * rf
        dpost_ref[...] += jnp.sum(dn * fn, axis=0, keepdims=True)
        df = _norm_bwd(dn, fn, rf, post_ref[...]).astype(BF16)
        df_ref[...] = df
        dh = jnp.zeros((tm, D_MODEL), F32)
        for c in range(2):
            cols = slice(c * FF_CHUNK, (c + 1) * FF_CHUNK)
            da = _dot_nt(df, wd_v[cols, :])
            gv = g_ref[:, cols].astype(F32)
            uv = u_ref[:, cols].astype(F32)
            sg = _sigmoid(gv)
            silu = gv * sg
            a_ref[:, cols] = (silu * uv).astype(BF16)
            dg = ((da * uv) * (sg * (1.0 + gv * (1.0 - sg)))).astype(BF16)
            du = (da * silu).astype(BF16)
            dgu_ref[:, cols] = dg
            dgu_ref[:, 2 * FF_CHUNK + c * FF_CHUNK:2 * FF_CHUNK + (c + 1) * FF_CHUNK] = du
            dh = dh + _dot_nt(dg, wgu_v[c]) + _dot_nt(du, wgu_v[2 + c])
        xv = x_ref[...]
        rx = _rstd(xv)
        xn = xv * rx
        h_ref[...] = (xn * pre_ref[...]).astype(BF16)
        dpre_ref[...] += jnp.sum(dh * xn, axis=0, keepdims=True)
        dx_ref[...] = dout_v + _norm_bwd(dh, xn, rx, pre_ref[...])

    args = [dout, x, f, g, u, pre, post, *wgu, *wd, mine]
    return pl.pallas_call(
        _ignoring(body, len(args), len(deps)), name="ffn_bwd", grid=(S // tm,),
        in_specs=[_rows(tm, D_MODEL), _rows(tm, D_MODEL), _rows(tm, D_MODEL), _rows(tm, D_FF), _rows(tm, D_FF),
                  _full((1, D_MODEL)), _full((1, D_MODEL)), _ANY, _ANY, _ANY, _ANY, _SMEM] + [_ANY] * len(deps),
        out_specs=[_rows(tm, D_MODEL), _rows(tm, D_MODEL), _rows(tm, 2 * D_FF), _rows(tm, D_FF), _rows(tm, D_MODEL),
                   _full((1, D_MODEL)), _full((1, D_MODEL))],
        out_shape=[jax.ShapeDtypeStruct((S, D_MODEL), F32), jax.ShapeDtypeStruct((S, D_MODEL), BF16),
                   jax.ShapeDtypeStruct((S, 2 * D_FF), BF16), jax.ShapeDtypeStruct((S, D_FF), BF16),
                   jax.ShapeDtypeStruct((S, D_MODEL), BF16),
                   jax.ShapeDtypeStruct((1, D_MODEL), F32), jax.ShapeDtypeStruct((1, D_MODEL), F32)],
        scratch_shapes=[pltpu.VMEM((N_SHARD, D_MODEL, FF_CHUNK), BF16), pltpu.VMEM((D_FF, D_MODEL), BF16),
                        pltpu.SemaphoreType.DMA((2 * N_SHARD,))],
        compiler_params=_params(dimension_semantics=("arbitrary",)),
    )(*args, *deps)


def _wgrad(lhs, rhs, m_block, n_block, name, column_shards=False, tk=2048, deps=()):
    S, M = lhs.shape
    N = rhs.shape[1]
    k_steps = S // tk

    def body(lhs_ref, rhs_ref, out_ref, out16_ref):
        k = pl.program_id(2)

        @pl.when(k == 0)
        def _():
            out_ref[...] = jnp.zeros_like(out_ref)
        out_ref[...] += _dot_tn(lhs_ref[...], rhs_ref[...])

        @pl.when(k == k_steps - 1)
        def _():
            out16_ref[...] = out_ref[...].astype(BF16)

    if column_shards:
        assert N == N_SHARD * n_block
        shape = (N_SHARD, M, n_block)
        out_spec = pl.BlockSpec((None, m_block, n_block), lambda i, j, k: (j, i, 0))
    else:
        shape = (M, N)
        out_spec = pl.BlockSpec((m_block, n_block), lambda i, j, k: (i, j))
    out, out16 = pl.pallas_call(
        _ignoring(body, 2, len(deps)), name=name, grid=(M // m_block, N // n_block, k_steps),
        in_specs=[pl.BlockSpec((tk, m_block), lambda i, j, k: (k, i)),
                  pl.BlockSpec((tk, n_block), lambda i, j, k: (k, j))] + [_ANY] * len(deps),
        out_specs=[out_spec, out_spec],
        out_shape=[jax.ShapeDtypeStruct(shape, F32), jax.ShapeDtypeStruct(shape, BF16)],
        compiler_params=_params(dimension_semantics=("arbitrary", "arbitrary", "arbitrary")),
    )(lhs, rhs, *deps)
    if not column_shards:
        out = out.reshape(N_SHARD, M // N_SHARD, N)
        out16 = out16.reshape(N_SHARD, M // N_SHARD, N)
    return out, out16


def _rope_tables(pos, invf):
    S = pos.shape[0]
    tm = MIXER_TILE

    def body(pos_ref, invf_ref, out_ref):
        ang = pos_ref[...].astype(F32) * invf_ref[...]
        cos, sin = jnp.cos(ang), jnp.sin(ang)
        lane = lax.broadcasted_iota(jnp.int32, ang.shape, 1) % HEAD_DIM
        first = lane < ROT_DIM // 2
        second = (lane >= ROT_DIM // 2) & (lane < ROT_DIM)
        out_ref[0] = jnp.where(lane < ROT_DIM, cos, 1.0)
        out_ref[1] = jnp.where(first, sin, 0.0)
        out_ref[2] = jnp.where(second, sin, 0.0)

    return pl.pallas_call(
        body, name="rope_tables", grid=(S // tm,),
        in_specs=[_rows(tm, 1), _full((1, _LANES))],
        out_specs=pl.BlockSpec((3, tm, _LANES), lambda i: (0, i, 0)),
        out_shape=jax.ShapeDtypeStruct((3, S, _LANES), F32),
        compiler_params=_params(dimension_semantics=("arbitrary",)),
    )(pos, invf)


def _table_spec(tm):
    return pl.BlockSpec((3, tm, _LANES), lambda i: (0, i, 0))


_HALF = ROT_DIM // 2
_LANES = 128


def _rope(t, tables):
    c, s_first, s_second = tables
    return t * c - pltpu.roll(t, _LANES - _HALF, axis=1) * s_first + pltpu.roll(t, _HALF, axis=1) * s_second


def _rope_transposed(t, tables):
    c, s_first, s_second = tables
    return t * c - pltpu.roll(t * s_first, _HALF, axis=1) + pltpu.roll(t * s_second, _LANES - _HALF, axis=1)


def _store_head_variants(ref, t):
    rolled = pltpu.roll(t, HEAD_DIM, axis=1)
    low = lax.broadcasted_iota(jnp.int32, t.shape, 1) < HEAD_DIM
    zero = jnp.zeros_like(t)
    ref[0] = jnp.where(low, t, zero).astype(BF16)
    ref[1] = jnp.where(low, zero, rolled).astype(BF16)
    ref[2] = jnp.where(low, rolled, zero).astype(BF16)
    ref[3] = jnp.where(low, zero, t).astype(BF16)


def _mixer_in_fwd(x, pre, w_in_t, mine, rope, deps=()):
    S = x.shape[0]
    tm = MIXER_TILE

    def body(x_ref, pre_ref, w_land, w_own, mine_ref, rope_ref, u_ref, q_ref, k_ref, v_ref, w_v, sem):
        _load_once(_gathered(w_land, w_own, w_v, mine_ref[0], rows=IN_WIDTH // N_SHARD), sem)
        xv = x_ref[...]
        h = ((xv * _rstd(xv)) * pre_ref[...]).astype(BF16)
        z = _dot_nt(h, w_v[...])
        tables = (rope_ref[0], rope_ref[1], rope_ref[2])
        u_ref[...] = z[:, :POOL_WIDTH]
        for t in range(ATTN_WIDTH // _LANES):
            lo = POOL_WIDTH + t * _LANES
            q_ref[:, t * _LANES:(t + 1) * _LANES] = (_rope(z[:, lo:lo + _LANES], tables) * ATTN_SCALE).astype(BF16)
        kv = POOL_WIDTH + ATTN_WIDTH
        _store_head_variants(k_ref, _rope(z[:, kv:kv + KV_WIDTH], tables))
        _store_head_variants(v_ref, z[:, kv + KV_WIDTH:])

    args = [x, pre, *w_in_t, mine, rope]
    variants = pl.BlockSpec((2 * N_KV_HEADS, tm, KV_WIDTH), lambda i: (0, i, 0))
    return pl.pallas_call(
        _ignoring(body, len(args), len(deps)), name="mixer_in_fwd", grid=(S // tm,),
        in_specs=[_rows(tm, D_MODEL), _full((1, D_MODEL)), _ANY, _ANY, _SMEM, _table_spec(tm)] + [_ANY] * len(deps),
        out_specs=[_rows(tm, POOL_WIDTH), _rows(tm, ATTN_WIDTH), variants, variants],
        out_shape=[jax.ShapeDtypeStruct((S, POOL_WIDTH), F32), jax.ShapeDtypeStruct((S, ATTN_WIDTH), BF16),
                   jax.ShapeDtypeStruct((2 * N_KV_HEADS, S, KV_WIDTH), BF16),
                   jax.ShapeDtypeStruct((2 * N_KV_HEADS, S, KV_WIDTH), BF16)],
        scratch_shapes=[pltpu.VMEM((IN_WIDTH, D_MODEL), BF16), pltpu.SemaphoreType.DMA((N_SHARD,))],
        compiler_params=_params(dimension_semantics=("arbitrary",)),
    )(*args, *deps)


def _mixer_in_bwd(dres, x, pre, w_in_t, mine, du, dq, dk, dv, dk_next, dv_next, rope, deps=()):
    S = x.shape[0]
    tm = MIXER_TILE

    def body(dres_ref, x_ref, pre_ref, w_land, w_own, mine_ref, du_ref, dq_ref, dk_ref, dv_ref, dkx_ref, dvx_ref,
             rope_ref,
             dx_ref, dz_ref, h_ref, dpre_ref, w_v, sem):
        _load_once(_gathered(w_land, w_own, w_v, mine_ref[0], rows=IN_WIDTH // N_SHARD), sem)

        @pl.when(pl.program_id(0) == 0)
        def _():
            dpre_ref[...] = jnp.zeros_like(dpre_ref)

        tables = (rope_ref[0], rope_ref[1], rope_ref[2])
        dz_ref[:, :POOL_WIDTH] = du_ref[...]
        for t in range(ATTN_WIDTH // _LANES):
            lo = POOL_WIDTH + t * _LANES
            dz_ref[:, lo:lo + _LANES] = _rope_transposed(dq_ref[:, t * _LANES:(t + 1) * _LANES], tables).astype(BF16)
        kv = POOL_WIDTH + ATTN_WIDTH
        has_next = pl.program_id(0) + 1 < steps
        pad = jnp.zeros((tm - BLOCK, KV_WIDTH), F32)
        dk_tile = dk_ref[...] + jnp.concatenate([pad, jnp.where(has_next, dkx_ref[...], 0.0)], axis=0)
        dv_tile = dv_ref[...] + jnp.concatenate([pad, jnp.where(has_next, dvx_ref[...], 0.0)], axis=0)
        dz_ref[:, kv:kv + KV_WIDTH] = _rope_transposed(dk_tile, tables).astype(BF16)
        dz_ref[:, kv + KV_WIDTH:] = dv_tile.astype(BF16)
        dh = _dot(dz_ref[...], w_v[...])
        xv = x_ref[...]
        rx = _rstd(xv)
        xn = xv * rx
        h_ref[...] = (xn * pre_ref[...]).astype(BF16)
        dpre_ref[...] += jnp.sum(dh * xn, axis=0, keepdims=True)
        dx_ref[...] = dres_ref[...] + _norm_bwd(dh, xn, rx, pre_ref[...])

    assert tm == ATTN_BLOCKS * BLOCK
    steps = S // tm
    nxt = pl.BlockSpec((None, BLOCK, KV_WIDTH), lambda i: (jnp.minimum(i + 1, steps - 1), 0, 0))
    args = [dres, x, pre, *w_in_t, mine, du, dq, dk, dv, dk_next, dv_next, rope]
    return pl.pallas_call(
        _ignoring(body, len(args), len(deps)), name="mixer_in_bwd", grid=(S // tm,),
        in_specs=[_rows(tm, D_MODEL), _rows(tm, D_MODEL), _full((1, D_MODEL)), _ANY, _ANY, _SMEM,
                  _rows(tm, POOL_WIDTH), _rows(tm, ATTN_WIDTH), _rows(tm, KV_WIDTH), _rows(tm, KV_WIDTH), nxt, nxt,
                  _table_spec(tm)] + [_ANY] * len(deps),
        out_specs=[_rows(tm, D_MODEL), _rows(tm, IN_WIDTH), _rows(tm, D_MODEL), _full((1, D_MODEL))],
        out_shape=[jax.ShapeDtypeStruct((S, D_MODEL), F32), jax.ShapeDtypeStruct((S, IN_WIDTH), BF16),
                   jax.ShapeDtypeStruct((S, D_MODEL), BF16), jax.ShapeDtypeStruct((1, D_MODEL), F32)],
        scratch_shapes=[pltpu.VMEM((IN_WIDTH, D_MODEL), BF16), pltpu.SemaphoreType.DMA((N_SHARD,))],
        compiler_params=_params(dimension_semantics=("arbitrary",)),
    )(*args, *deps)


def _pool_counts(tile_index, tm, width):
    t = tile_index * tm + lax.broadcasted_iota(jnp.int32, (tm, 1), 0)
    return jnp.minimum(t + 1, width).astype(F32)


def _pool_features(ext, u_tile, tile_index, tm):
    ds = []
    for gi, width in enumerate(POOL_WINDOWS):
        lanes = slice(gi * POOL_GROUP, (gi + 1) * POOL_GROUP)
        s = ext[:, lanes]
        shift = 1
        while shift < width:
            s = s + pltpu.roll(s, shift, axis=0)
            shift *= 2
        ds.append(s[HALO:, :] / _pool_counts(tile_index, tm, width) - u_tile[:, lanes])
    return ds


def _pool_fwd(u, w_pool, pool_scale, g_pool):
    S = u.shape[0]
    tm = MIXER_TILE

    def body(u_ref, w_ref, scale_ref, gain_ref, y_ref, ext_ref):
        i = pl.program_id(0)

        @pl.when(i == 0)
        def _():
            ext_ref[:HALO, :] = jnp.zeros((HALO, POOL_WIDTH), F32)

        u_tile = u_ref[...]
        ext_ref[HALO:, :] = u_tile
        ds = _pool_features(ext_ref[...], u_tile, i, tm)
        ext_ref[:HALO, :] = u_tile[tm - HALO:, :]
        ys = [_dot(ds[gi].astype(BF16), w_ref[gi].astype(BF16)) for gi in range(len(POOL_WINDOWS))]
        po = jnp.concatenate(ys, axis=1) * scale_ref[...]
        y_ref[...] = ((po * _rstd(po)) * gain_ref[...]).astype(BF16)

    return pl.pallas_call(
        body, name="pool_fwd", grid=(S // tm,),
        in_specs=[_rows(tm, POOL_WIDTH), _full((len(POOL_WINDOWS), POOL_GROUP, POOL_GROUP)),
                  _full((1, POOL_WIDTH)), _full((1, POOL_WIDTH))],
        out_specs=_rows(tm, POOL_WIDTH),
        out_shape=jax.ShapeDtypeStruct((S, POOL_WIDTH), BF16),
        scratch_shapes=[pltpu.VMEM((HALO + tm, POOL_WIDTH), F32)],
        compiler_params=_params(dimension_semantics=("arbitrary",)),
    )(u, w_pool, pool_scale, g_pool)


def _pool_bwd(dy, u, w_pool, pool_scale, g_pool):
    S = u.shape[0]
    tm = MIXER_TILE
    n_tiles = S // tm
    halo_blocks = tm // HALO

    def body(dy_ref, u_ref, uprev_ref, w_ref, scale_ref, gain_ref,
             du_ref, dw_ref, dscale_ref, dgain_ref, ext_ref, nxt_ref):
        i = pl.program_id(0)
        tile = n_tiles - 1 - i

        @pl.when(i == 0)
        def _():
            dw_ref[...] = jnp.zeros_like(dw_ref)
            dscale_ref[...] = jnp.zeros_like(dscale_ref)
            dgain_ref[...] = jnp.zeros_like(dgain_ref)
            nxt_ref[...] = jnp.zeros_like(nxt_ref)

        u_tile = u_ref[...]
        ext_ref[:HALO, :] = jnp.where(tile > 0, uprev_ref[...], 0.0)
        ext_ref[HALO:, :] = u_tile
        ds = _pool_features(ext_ref[...], u_tile, tile, tm)
        dsb = [d.astype(BF16) for d in ds]
        wb = [w_ref[gi].astype(BF16) for gi in range(len(POOL_WINDOWS))]
        yraw = jnp.concatenate([_dot(dsb[gi], wb[gi]) for gi in range(len(POOL_WINDOWS))], axis=1)
        po = yraw * scale_ref[...]
        r = _rstd(po)
        pn = po * r
        dyv = dy_ref[...]
        dgain_ref[...] += jnp.sum(dyv * pn, axis=0, keepdims=True)
        dpo = _norm_bwd(dyv, pn, r, gain_ref[...])
        dscale_ref[...] += jnp.sum(dpo * yraw, axis=0, keepdims=True)
        dyraw = (dpo * scale_ref[...]).astype(BF16)
        for gi, width in enumerate(POOL_WINDOWS):
            lanes = slice(gi * POOL_GROUP, (gi + 1) * POOL_GROUP)
            dw_ref[gi] += _dot_tn(dsb[gi], dyraw[:, lanes])
            dd = _dot_nt(dyraw[:, lanes], wb[gi])
            ddc = dd / _pool_counts(tile, tm, width)
            ext_ref[:tm, lanes] = ddc
            ext_ref[tm:, lanes] = nxt_ref[:, lanes]
            s = ext_ref[:, lanes]
            shift = 1
            while shift < width:
                s = s + pltpu.roll(s, HALO + tm - shift, axis=0)
                shift *= 2
            du_ref[:, lanes] = (s[:tm, :] - dd).astype(BF16)
            nxt_ref[:, lanes] = ddc[:HALO, :]

    return pl.pallas_call(
        body, name="pool_bwd", grid=(n_tiles,),
        in_specs=[pl.BlockSpec((tm, POOL_WIDTH), lambda i: (n_tiles - 1 - i, 0)),
                  pl.BlockSpec((tm, POOL_WIDTH), lambda i: (n_tiles - 1 - i, 0)),
                  pl.BlockSpec((HALO, POOL_WIDTH), lambda i: (jnp.maximum((n_tiles - 1 - i) * halo_blocks - 1, 0), 0)),
                  _full((len(POOL_WINDOWS), POOL_GROUP, POOL_GROUP)), _full((1, POOL_WIDTH)), _full((1, POOL_WIDTH))],
        out_specs=[pl.BlockSpec((tm, POOL_WIDTH), lambda i: (n_tiles - 1 - i, 0)),
                   _full((len(POOL_WINDOWS), POOL_GROUP, POOL_GROUP)), _full((1, POOL_WIDTH)), _full((1, POOL_WIDTH))],
        out_shape=[jax.ShapeDtypeStruct((S, POOL_WIDTH), BF16),
                   jax.ShapeDtypeStruct((len(POOL_WINDOWS), POOL_GROUP, POOL_GROUP), F32),
                   jax.ShapeDtypeStruct((1, POOL_WIDTH), F32), jax.ShapeDtypeStruct((1, POOL_WIDTH), F32)],
        scratch_shapes=[pltpu.VMEM((HALO + tm, POOL_WIDTH), F32), pltpu.VMEM((HALO, POOL_WIDTH), F32)],
        compiler_params=_params(dimension_semantics=("arbitrary",)),
    )(dy, u, u, w_pool, pool_scale, g_pool)


def _variant(head):
    return 2 * (head // (N_HEADS // N_KV_HEADS)) + head % 2


def _own_block(shape=(BLOCK, BLOCK)):
    r = lax.broadcasted_iota(jnp.int32, shape, 0)
    i = lax.broadcasted_iota(jnp.int32, shape, 1)
    return r <= i


def _fold_band(own, from_own, from_prev):
    return jnp.where(own, from_own, from_prev)


def _scores_by_head(own_tiles, prev_tiles, q_tiles):
    stacks = [jnp.concatenate(q_tiles[:2], axis=0), jnp.concatenate(q_tiles[2:], axis=0)]
    by_var = [_dot_nt(jnp.concatenate([own_tiles[v], prev_tiles[v]], axis=0), stacks[v // 2])
              for v in range(2 * N_KV_HEADS)]
    quadrant = lambda h, rows: by_var[_variant(h)][rows * BLOCK:(rows + 1) * BLOCK,
                                                   ((h // 2) % 2) * BLOCK:((h // 2) % 2 + 1) * BLOCK]
    return [quadrant(h, 0) for h in range(N_HEADS)], [quadrant(h, 1) for h in range(N_HEADS)]


def _softmax_t(s, sink):
    m = jnp.maximum(jnp.max(s, axis=0, keepdims=True), sink)
    p = jnp.exp(s - m)
    p_sink = jnp.exp(sink - m)
    inv = 1.0 / (jnp.sum(p, axis=0, keepdims=True) + p_sink)
    return p * inv, p_sink * inv


def _attn_fwd(q, kz, vz, sinks, g_attn, y_pool, x, w_out, mine, post):
    S = q.shape[0]
    tq = ATTN_BLOCKS * BLOCK
    n_var = 2 * N_KV_HEADS

    def body(q_ref, kp_ref, kc_ref, vp_ref, vc_ref, sinks_ref, gain_ref, yp_ref, x_ref, w_land, w_own, mine_ref,
             post_ref, o_ref, out_ref, m_ref, y_ref, w_v, sem):
        _load_once(_gathered(w_land, w_own, w_v, mine_ref[0], rows=D_MODEL // N_SHARD), sem)
        step = pl.program_id(0)
        own = _own_block()
        zero = jnp.zeros((BLOCK, BLOCK), F32)

        def tiles(cur_ref, prev_ref, j):
            rows = lambda jj: slice(jj * BLOCK, (jj + 1) * BLOCK)
            return ([cur_ref[v, rows(j), :] for v in range(n_var)],
                    [prev_ref[v] if j == 0 else cur_ref[v, rows(j - 1), :] for v in range(n_var)])

        scores = []
        for j in range(ATTN_BLOCKS):
            q_pairs = [q_ref[j * BLOCK:(j + 1) * BLOCK, i * _LANES:(i + 1) * _LANES] for i in range(N_HEADS // 2)]
            scores.append(_scores_by_head(*tiles(kc_ref, kp_ref, j), q_pairs))
        probs = []
        for j in range(ATTN_BLOCKS):
            s_own, s_prev = scores[j]
            no_prev = jnp.where(step > 0, 0.0, NEG_INF) if j == 0 else 0.0
            p_own, p_prev = [], []
            for h in range(N_HEADS):
                p, _ = _softmax_t(_fold_band(own, s_own[h], s_prev[h] + no_prev), sinks_ref[0, h])
                p_own.append(jnp.where(own, p, zero).astype(BF16))
                p_prev.append(jnp.where(own, zero, p).astype(BF16))
            probs.append((p_own, p_prev))
        blocks = []
        for j in range(ATTN_BLOCKS):
            p_own, p_prev = probs[j]
            v_own, v_prev = tiles(vc_ref, vp_ref, j)
            pairs = []
            for i in range(N_HEADS // 2):
                acc = None
                for h in (2 * i, 2 * i + 1):
                    part = _dot_tn(p_own[h], v_own[_variant(h)]) + _dot_tn(p_prev[h], v_prev[_variant(h)])
                    acc = part if acc is None else acc + part
                pairs.append(acc)
            blocks.append(jnp.concatenate(pairs, axis=1))
        o = jnp.concatenate(blocks, axis=0)
        o_ref[...] = o
        y_ref[:, :POOL_WIDTH] = yp_ref[...]
        y_ref[:, POOL_WIDTH:] = ((o * _rstd(o)) * gain_ref[...]).astype(BF16)
        m = _dot(y_ref[...], w_v[...])
        m_ref[...] = m
        out_ref[...] = x_ref[...] + (m * _rstd(m)) * post_ref[...]

    prev = pl.BlockSpec((n_var, BLOCK, KV_WIDTH), lambda g: (0, jnp.maximum(g * ATTN_BLOCKS - 1, 0), 0))
    cur = pl.BlockSpec((n_var, tq, KV_WIDTH), lambda g: (0, g, 0))
    return pl.pallas_call(
        body, name="attn_fwd", grid=(S // tq,),
        in_specs=[_rows(tq, ATTN_WIDTH), prev, cur, prev, cur,
                  pl.BlockSpec(memory_space=pltpu.SMEM), _full((1, ATTN_WIDTH)),
                  _rows(tq, POOL_WIDTH), _rows(tq, D_MODEL), _ANY, _ANY, _SMEM, _full((1, D_MODEL))],
        out_specs=[_rows(tq, ATTN_WIDTH), _rows(tq, D_MODEL), _rows(tq, D_MODEL), _rows(tq, D_MODEL)],
        out_shape=[jax.ShapeDtypeStruct((S, ATTN_WIDTH), F32), jax.ShapeDtypeStruct((S, D_MODEL), F32),
                   jax.ShapeDtypeStruct((S, D_MODEL), F32), jax.ShapeDtypeStruct((S, D_MODEL), BF16)],
        scratch_shapes=[pltpu.VMEM((D_MODEL, D_MODEL), BF16), pltpu.SemaphoreType.DMA((N_SHARD,))],
        compiler_params=_params(dimension_semantics=("arbitrary",)),
    )(q, kz, kz, vz, vz, sinks, g_attn, y_pool, x, *w_out, mine, post)


def _attn_bwd(dout, m, w_out, mine, post, o, q, kz, vz, sinks, g_attn, deps=()):
    S = q.shape[0]
    tq = ATTN_BLOCKS * BLOCK
    n_var = 2 * N_KV_HEADS

    def body(dout_ref, m_ref, w_land, w_own, mine_ref, post_ref, o_ref, q_ref, kp_ref, kc_ref, vp_ref, vc_ref,
             sinks_ref, gain_ref, dyp_ref, dm_ref, dpost_ref, dq_ref, dk_ref, dv_ref, dkx_ref, dvx_ref, dsink_ref,
             dgain_ref, w_v, sem):
        _load_once(_gathered(w_land, w_own, w_v, mine_ref[0], rows=D_MODEL // N_SHARD), sem)
        step = pl.program_id(0)

        @pl.when(step == 0)
        def _():
            dpost_ref[...] = jnp.zeros_like(dpost_ref)
            dsink_ref[...] = jnp.zeros_like(dsink_ref)
            dgain_ref[...] = jnp.zeros_like(dgain_ref)

        mv = m_ref[...]
        rm = _rstd(mv)
        mn = mv * rm
        dres = dout_ref[...]
        dpost_ref[...] += jnp.sum(dres * mn, axis=0, keepdims=True)
        dm = _norm_bwd(dres, mn, rm, post_ref[...]).astype(BF16)
        dm_ref[...] = dm
        dy = _dot_nt(dm, w_v[...])
        dyp_ref[...] = dy[:, :POOL_WIDTH]
        ov = o_ref[...]
        r = _rstd(ov)
        on = ov * r
        dyv = dy[:, POOL_WIDTH:]
        dgain_ref[...] += jnp.sum(dyv * on, axis=0, keepdims=True)
        do = _norm_bwd(dyv, on, r, gain_ref[...]).astype(BF16)
        own = _own_block()
        zero = jnp.zeros((BLOCK, BLOCK), F32)
        split = lambda t: (jnp.where(own, t, zero).astype(BF16), jnp.where(own, zero, t).astype(BF16))
        rows = lambda j: slice(j * BLOCK, (j + 1) * BLOCK)
        heads = range(N_HEADS)

        def tiles(cur_ref, prev_ref, j):
            return ([cur_ref[v, rows(j), :] for v in range(n_var)],
                    [prev_ref[v] if j == 0 else cur_ref[v, rows(j - 1), :] for v in range(n_var)])

        q_pairs = [[q_ref[rows(j), i * _LANES:(i + 1) * _LANES] for i in range(N_HEADS // 2)] for j in range(ATTN_BLOCKS)]
        do_pairs = [[do[rows(j), i * _LANES:(i + 1) * _LANES] for i in range(N_HEADS // 2)] for j in range(ATTN_BLOCKS)]
        scores = [(_scores_by_head(*tiles(kc_ref, kp_ref, j), q_pairs[j]),
                   _scores_by_head(*tiles(vc_ref, vp_ref, j), do_pairs[j])) for j in range(ATTN_BLOCKS)]
        parts, sink_sum = [], None
        for j in range(ATTN_BLOCKS):
            (s_own, s_prev), (dp_own, dp_prev) = scores[j]
            no_prev = jnp.where(step > 0, 0.0, NEG_INF) if j == 0 else 0.0
            ds_parts, p_parts, sink_rows = [], [], []
            for h in heads:
                p, p_sink = _softmax_t(_fold_band(own, s_own[h], s_prev[h] + no_prev), sinks_ref[0, h])
                dp = _fold_band(own, dp_own[h], dp_prev[h])
                delta = jnp.sum(p * dp, axis=0, keepdims=True)
                ds_parts.append(split(p * (dp - delta)))
                p_parts.append(split(p))
                sink_rows.append(jnp.zeros((1, _LANES), F32) - jnp.sum(p_sink * delta))
            block_sinks = jnp.concatenate(sink_rows, axis=0)
            sink_sum = block_sinks if sink_sum is None else sink_sum + block_sinks
            parts.append((ds_parts, p_parts))
        dsink_ref[...] += sink_sum
        low = lax.broadcasted_iota(jnp.int32, (BLOCK, _LANES), 1) < HEAD_DIM

        def merge(acc):
            return jnp.where(low, acc[0] + pltpu.roll(acc[1], HEAD_DIM, axis=1),
                             acc[3] + pltpu.roll(acc[2], HEAD_DIM, axis=1))
        add = lambda acc, var, t: acc.__setitem__(var, t if acc[var] is None else acc[var] + t)
        k_own, k_prev, v_own, v_prev = [], [], [], []
        for j in range(ATTN_BLOCKS):
            ds_parts, p_parts = parts[j]
            kt_own, kt_prev = tiles(kc_ref, kp_ref, j)
            dk_own, dk_prev, dv_own, dv_prev = ([None] * n_var for _ in range(4))
            for i in range(N_HEADS // 2):
                dq_pair = None
                for h in (2 * i, 2 * i + 1):
                    var = _variant(h)
                    (ds_o, ds_p), (p_o, p_p) = ds_parts[h], p_parts[h]
                    part = _dot_tn(ds_o, kt_own[var]) + _dot_tn(ds_p, kt_prev[var])
                    dq_pair = part if dq_pair is None else dq_pair + part
                    add(dk_own, var, _dot(ds_o, q_pairs[j][i]))
                    add(dk_prev, var, _dot(ds_p, q_pairs[j][i]))
                    add(dv_own, var, _dot(p_o, do_pairs[j][i]))
                    add(dv_prev, var, _dot(p_p, do_pairs[j][i]))
                dq_ref[rows(j), i * _LANES:(i + 1) * _LANES] = dq_pair * ATTN_SCALE
            k_own.append(merge(dk_own))
            k_prev.append(merge(dk_prev))
            v_own.append(merge(dv_own))
            v_prev.append(merge(dv_prev))
        for j in range(ATTN_BLOCKS):
            last = j == ATTN_BLOCKS - 1
            dk_ref[rows(j), :] = k_own[j] if last else k_own[j] + k_prev[j + 1]
            dv_ref[rows(j), :] = v_own[j] if last else v_own[j] + v_prev[j + 1]
        dkx_ref[...] = k_prev[0]
        dvx_ref[...] = v_prev[0]

    steps = S // tq
    prev = pl.BlockSpec((n_var, BLOCK, KV_WIDTH), lambda g: (0, jnp.maximum(g * ATTN_BLOCKS - 1, 0), 0))
    cur = pl.BlockSpec((n_var, tq, KV_WIDTH), lambda g: (0, g, 0))
    nxt = pl.BlockSpec((None, BLOCK, KV_WIDTH), lambda g: (g, 0, 0))
    args = [dout, m, *w_out, mine, post, o, q, kz, kz, vz, vz, sinks, g_attn]
    return pl.pallas_call(
        _ignoring(body, len(args), len(deps)), name="attn_bwd", grid=(steps,),
        in_specs=[_rows(tq, D_MODEL), _rows(tq, D_MODEL), _ANY, _ANY, _SMEM, _full((1, D_MODEL)),
                  _rows(tq, ATTN_WIDTH), _rows(tq, ATTN_WIDTH), prev, cur, prev, cur,
                  pl.BlockSpec(memory_space=pltpu.SMEM), _full((1, ATTN_WIDTH))] + [_ANY] * len(deps),
        out_specs=[_rows(tq, POOL_WIDTH), _rows(tq, D_MODEL), _full((1, D_MODEL)),
                   _rows(tq, ATTN_WIDTH), _rows(tq, KV_WIDTH), _rows(tq, KV_WIDTH), nxt, nxt,
                   _full((N_HEADS, _LANES)), _full((1, ATTN_WIDTH))],
        out_shape=[jax.ShapeDtypeStruct((S, POOL_WIDTH), F32), jax.ShapeDtypeStruct((S, D_MODEL), BF16),
                   jax.ShapeDtypeStruct((1, D_MODEL), F32),
                   jax.ShapeDtypeStruct((S, ATTN_WIDTH), F32), jax.ShapeDtypeStruct((S, KV_WIDTH), F32),
                   jax.ShapeDtypeStruct((S, KV_WIDTH), F32),
                   jax.ShapeDtypeStruct((steps, BLOCK, KV_WIDTH), F32), jax.ShapeDtypeStruct((steps, BLOCK, KV_WIDTH), F32),
                   jax.ShapeDtypeStruct((N_HEADS, _LANES), F32), jax.ShapeDtypeStruct((1, ATTN_WIDTH), F32)],
        scratch_shapes=[pltpu.VMEM((D_MODEL, D_MODEL), BF16), pltpu.SemaphoreType.DMA((N_SHARD,))],
        compiler_params=_params(dimension_semantics=("arbitrary",)),
    )(*args, *deps)


def _inv_freq_row():
    inv_freq = ROPE_THETA ** (-jnp.arange(0, ROT_DIM, 2, dtype=F32) / ROT_DIM)
    per_head = jnp.concatenate([inv_freq, inv_freq, jnp.zeros((HEAD_DIM - ROT_DIM,), F32)])
    return jnp.tile(per_head, _LANES // HEAD_DIM).reshape(1, _LANES)


def _local_step(x, pos, target, small, mine, weights_of, grads_ready):
    rope = _rope_tables(pos, _inv_freq_row())
    wgu1, wd1 = weights_of("ffn1", (rope,))
    x1, g1, u1, f1 = _ffn_fwd(x, small["ffn1_pre"], small["ffn1_post"], wgu1, wd1, mine)
    w_in_t, w_out = weights_of("mixer", (x1,))
    u, q, kz, vz = _mixer_in_fwd(x1, small["mix_pre"], w_in_t, mine, rope)
    y_pool = _pool_fwd(u, small["w_pool"], small["pool_scale"], small["g_pool"])
    o, x2, m, y = _attn_fwd(q, kz, vz, small["sinks"], small["g_attn"], y_pool, x1, w_out, mine, small["mix_post"])
    wgu2, wd2 = weights_of("ffn2", (x2,))
    dx3, g2, u2, f2, loss_acc = _ffn_fwd(x2, small["ffn2_pre"], small["ffn2_post"], wgu2, wd2, mine, target=target)
    grads = {"loss": loss_acc * (0.5 / D_MODEL)}
    dx2, h3, dgu2, a2, df2, grads["ffn2_pre"], grads["ffn2_post"] = _ffn_bwd(
        dx3, x2, f2, g2, u2, small["ffn2_pre"], small["ffn2_post"], wgu2, wd2, mine)
    dwgu2 = _wgrad(h3, dgu2, D_MODEL, FF_CHUNK, "wgrad_gu2", column_shards=True)
    dwd2 = _wgrad(a2, df2, FF_CHUNK, D_MODEL, "wgrad_down2")
    deps = grads_ready("ffn2", {"ffn2_w_gu": dwgu2, "ffn2_w_down": dwd2})
    dy_pool, dm, grads["mix_post"], dq, dk, dv, dk_next, dv_next, dsinks, grads["g_attn"] = _attn_bwd(
        dx2, m, w_out, mine, small["mix_post"], o, q, kz, vz, small["sinks"], small["g_attn"], deps=deps)
    dw_out = _wgrad(y, dm, D_MODEL, D_MODEL, "wgrad_out")
    grads["sinks"] = dsinks[:, 0].reshape(1, N_HEADS)
    du, grads["w_pool"], grads["pool_scale"], grads["g_pool"] = _pool_bwd(
        dy_pool, u, small["w_pool"], small["pool_scale"], small["g_pool"])
    dx1, dz, h2, grads["mix_pre"] = _mixer_in_bwd(dx2, x1, small["mix_pre"], w_in_t, mine, du, dq, dk, dv, dk_next, dv_next, rope)
    dw_in_t = _wgrad(dz, h2, IN_WIDTH, D_MODEL, "wgrad_in")
    deps = grads_ready("mixer", {"w_in": dw_in_t, "w_out": dw_out})
    dx, h1, dgu1, a1, df1, grads["ffn1_pre"], grads["ffn1_post"] = _ffn_bwd(
        dx1, x, f1, g1, u1, small["ffn1_pre"], small["ffn1_post"], wgu1, wd1, mine, deps=deps)
    dwgu1 = _wgrad(h1, dgu1, D_MODEL, FF_CHUNK, "wgrad_gu1", column_shards=True)
    deps = grads_ready("ffn1_gu", {"ffn1_w_gu": dwgu1}, small=grads)
    dwd1 = _wgrad(a1, df1, FF_CHUNK, D_MODEL, "wgrad_down1", deps=deps)
    grads_ready("ffn1_down", {"ffn1_w_down": dwd1})
    return dx


def _place():
    return lax.axis_index("x"), lax.axis_index("y"), lax.axis_index("c")


def _other_chips(x, y):
    return [(1 - x, y), (x, 1 - y), (1 - x, 1 - y)]


_HBM = pl.BlockSpec(memory_space=pltpu.HBM)
_SEM = pl.BlockSpec(memory_space=pltpu.SEMAPHORE)
_EFFECT = pltpu.SideEffectType.DATAFLOW_SIDE_EFFECTING
GATHER, GATHER_HALF, REDUCE, BROADCAST = "gather", "gather_half", "reduce", "broadcast"
N_DEVICES = 8


def _in_hbm(a):
    return pltpu.with_memory_space_constraint(a, pltpu.HBM)


def _core_half(rows, c):
    return pl.ds(pl.multiple_of(c * (rows // 2), 16), rows // 2)


def _chip_copies(kind, srcs, lands, send_sems, recv_sems):
    x, y, c = _place()
    mine = 2 * x + y
    copies = []
    for w in range(len(srcs)):
        if kind == BROADCAST:
            peers = [(x ^ (k >> 2), y ^ ((k >> 1) & 1), c ^ (k & 1)) for k in range(1, N_DEVICES)]
        else:
            peers = [(px, py, c) for px, py in _other_chips(x, y)]
        for k, (px, py, pc) in enumerate(peers):
            if kind == GATHER:
                src, dst = srcs[w], lands[w].at[mine]
            elif kind == GATHER_HALF:
                half = _core_half(srcs[w].shape[0], c)
                src, dst = srcs[w].at[half, :], lands[w].at[mine, half, :]
            elif kind == BROADCAST:
                src, dst = srcs[w], lands[w].at[2 * mine + c]
            else:
                src, dst = srcs[w].at[2 * px + py], lands[w].at[k]
            pair = len(peers) * w + k
            copies.append(pltpu.make_async_remote_copy(
                src_ref=src, dst_ref=dst, send_sem=send_sems.at[pair], recv_sem=recv_sems.at[pair],
                device_id=(px, py, pc), device_id_type=MESH))
    return copies


def _landing_shape(kind, src):
    if kind == REDUCE:
        return (N_SHARD - 1,) + src.shape[1:]
    return ((N_DEVICES if kind == BROADCAST else N_SHARD),) + src.shape


def _peer_count(kind):
    return N_DEVICES - 1 if kind == BROADCAST else N_SHARD - 1


def _exchange_start(kinds, groups, name):
    sizes = [len(g) for g in groups]
    flat = [s for g in groups for s in g]
    n, ng = len(flat), len(groups)

    def body(*refs):
        srcs, lands = refs[:n], refs[n:2 * n]
        sems = refs[2 * n:2 * n + 2 * ng]
        token = refs[-1]
        start = 0
        for gi, size in enumerate(sizes):
            for cp in _chip_copies(kinds[gi], srcs[start:start + size], lands[start:start + size],
                                   sems[2 * gi], sems[2 * gi + 1]):
                cp.start()
            start += size
        token[...] = jnp.zeros_like(token)

    landings = [lax.empty(_landing_shape(kind, s), s.dtype) for kind, g in zip(kinds, groups) for s in g]
    sem_shapes = [pltpu.SemaphoreType.DMA((size * _peer_count(kind),)) for kind, size in zip(kinds, sizes)
                  for _ in range(2)]
    outs = pl.pallas_call(
        body, name=name,
        in_specs=[_HBM] * (2 * n),
        out_specs=[_SEM] * (2 * ng) + [_HBM] * (2 * n) + [pl.BlockSpec(memory_space=pltpu.VMEM)],
        out_shape=sem_shapes + [pltpu.HBM(a.shape, a.dtype) for a in flat + landings]
        + [jax.ShapeDtypeStruct((8, _LANES), F32)],
        input_output_aliases={i: 2 * ng + i for i in range(2 * n)},
        compiler_params=pltpu.CompilerParams(has_side_effects=_EFFECT),
    )(*[_in_hbm(a) for a in flat + landings])
    sems, srcs, lands, token = outs[:2 * ng], outs[2 * ng:2 * ng + n], outs[2 * ng + n:2 * ng + 2 * n], outs[-1]
    handles, start = [], 0
    for gi, size in enumerate(sizes):
        handles.append((sems[2 * gi], sems[2 * gi + 1], srcs[start:start + size], lands[start:start + size]))
        start += size
    return handles, token


def _exchange_wait(kind, handle, after, name):
    send_sems, recv_sems, srcs, lands = handle
    n = len(srcs)

    def body(*refs):
        copies = _chip_copies(kind, refs[:n], refs[n:2 * n], refs[2 * n], refs[2 * n + 1])
        for cp in copies:
            cp.wait_send()
        for cp in copies:
            cp.wait_recv()

    outs = pl.pallas_call(
        body, name=name,
        in_specs=[_HBM] * (2 * n) + [_SEM, _SEM] + [_ANY] * len(after),
        out_specs=[_HBM] * (2 * n),
        out_shape=[pltpu.HBM(a.shape, a.dtype) for a in list(srcs) + list(lands)],
        input_output_aliases={i: i for i in range(2 * n)},
        compiler_params=pltpu.CompilerParams(has_side_effects=_EFFECT),
    )(*srcs, *lands, send_sems, recv_sems, *after)
    return outs[:n], outs[n:]


def _swap_gathered_halves(lands, name):
    n = len(lands)

    def body(*refs):
        bufs = refs[n:2 * n]
        send_sems, recv_sems = refs[2 * n:]
        x, y, c = _place()
        mine = 2 * x + y
        sends, arrivals = [], []
        for w in range(n):
            rows = bufs[w].shape[1]
            for d in range(1, N_SHARD):
                slot = (mine + d) % N_SHARD
                sems = dict(send_sem=send_sems.at[(N_SHARD - 1) * w + d - 1],
                            recv_sem=recv_sems.at[(N_SHARD - 1) * w + d - 1],
                            device_id=(x, y, 1 - c), device_id_type=MESH)
                fetched = bufs[w].at[slot, _core_half(rows, c), :]
                missing = bufs[w].at[slot, _core_half(rows, 1 - c), :]
                sends.append(pltpu.make_async_remote_copy(src_ref=fetched, dst_ref=fetched, **sems))
                arrivals.append(pltpu.make_async_remote_copy(src_ref=missing, dst_ref=missing, **sems))
        for cp in sends:
            cp.start()
        for cp in arrivals:
            cp.wait_recv()
        for cp in sends:
            cp.wait_send()

    return pl.pallas_call(
        body, name=name, in_specs=[_ANY] * n, out_specs=[_ANY] * n,
        out_shape=[jax.ShapeDtypeStruct(a.shape, a.dtype) for a in lands],
        input_output_aliases={i: i for i in range(n)},
        scratch_shapes=[pltpu.SemaphoreType.DMA((n * (N_SHARD - 1),)), pltpu.SemaphoreType.DMA((n * (N_SHARD - 1),))],
        compiler_params=pltpu.CompilerParams(has_side_effects=True),
    )(*lands)


def _swap_with_sibling(partials, name):
    n = len(partials)

    def body(*refs):
        ins, outs = refs[:n], refs[n:2 * n]
        send_sems, recv_sems = refs[2 * n:]
        x, y, c = _place()
        sends = [pltpu.make_async_remote_copy(
            src_ref=ins[w], dst_ref=outs[w], send_sem=send_sems.at[w], recv_sem=recv_sems.at[w],
            device_id=(x, y, 1 - c), device_id_type=MESH) for w in range(n)]
        for cp in sends:
            cp.start()
        for cp in sends:
            cp.wait_recv()
        for cp in sends:
            cp.wait_send()

    return pl.pallas_call(
        body, name=name,
        in_specs=[_ANY] * n, out_specs=[_ANY] * n,
        out_shape=[jax.ShapeDtypeStruct(p.shape, p.dtype) for p in partials],
        scratch_shapes=[pltpu.SemaphoreType.DMA((n,)), pltpu.SemaphoreType.DMA((n,))],
        compiler_params=pltpu.CompilerParams(has_side_effects=True),
    )(*partials)


def _row_block(rows, cap):
    best = None
    for cand in range(16, min(rows, cap) + 1, 16):
        if rows % cand == 0:
            best = cand
    assert best is not None, rows
    return best


def _chip_partial(own, received, shard, name):
    _, R, C = own.shape
    rb = _row_block(R, 512)

    def body(shard_ref, own_ref, rec_ref, out_ref):
        acc = own_ref[...]
        for k in range(3):
            acc = acc + rec_ref[k].astype(F32)
        out_ref[...] = acc.astype(BF16)

    return pl.pallas_call(
        body, name=name,
        grid_spec=pltpu.PrefetchScalarGridSpec(
            num_scalar_prefetch=1, grid=(R // rb,),
            in_specs=[pl.BlockSpec((None, rb, C), lambda i, s: (s[0], i, 0)),
                      pl.BlockSpec((3, rb, C), lambda i, s: (0, i, 0))],
            out_specs=pl.BlockSpec((rb, C), lambda i, s: (i, 0))),
        out_shape=jax.ShapeDtypeStruct((R, C), BF16),
        compiler_params=_params(dimension_semantics=("arbitrary",)),
    )(shard, own, received)


def _adamw(w, m, v, g_parts, name, slot=None):
    R, C = w.shape
    by_device = slot is not None
    rb = _row_block(R, 512) if R % 16 == 0 else R

    def body(w_ref, m_ref, v_ref, *refs):
        g_refs, (grad_ref, delta_ref, m_out, v_out) = refs[:-4], refs[-4:]
        if by_device:
            own_ref, land_ref, slot_ref = g_refs
            part = lambda d: jnp.where(slot_ref[0] == d, own_ref[...], land_ref[d])
            g = part(0)
            for d in range(1, N_DEVICES):
                g = g + part(d)
        else:
            g = g_refs[0][...].astype(F32)
            for g_ref in g_refs[1:]:
                g = g + g_ref[...].astype(F32)
        grad_ref[...] = g
        new_m = ADAM_B1 * m_ref[...] + (1.0 - ADAM_B1) * g
        new_v = ADAM_B2 * v_ref[...] + (1.0 - ADAM_B2) * (g * g)
        m_hat = new_m / (1.0 - ADAM_B1 ** ADAM_STEP)
        v_hat = new_v / (1.0 - ADAM_B2 ** ADAM_STEP)
        delta_ref[...] = -ADAM_LR * (m_hat / (jnp.sqrt(v_hat) + ADAM_EPS) + ADAM_WD * w_ref[...])
        m_out[...] = new_m
        v_out[...] = new_v

    spec = pl.BlockSpec((rb, C), lambda i: (i, 0))
    if by_device:
        g_specs = [spec, pl.BlockSpec((N_DEVICES, rb, C), lambda i: (0, i, 0)), _SMEM]
        g_parts = list(g_parts) + [slot]
    else:
        g_specs = [spec] * len(g_parts)
    return pl.pallas_call(
        body, name=name, grid=(R // rb,),
        in_specs=[spec, spec, spec] + g_specs,
        out_specs=[spec] * 4,
        out_shape=[jax.ShapeDtypeStruct((R, C), F32)] * 4,
        compiler_params=_params(dimension_semantics=("arbitrary",)),
    )(w, m, v, *g_parts)


SMALL_NAMES = ("ffn1_pre", "ffn1_post", "mix_pre", "pool_scale", "sinks", "g_pool", "g_attn", "mix_post",
               "ffn2_pre", "ffn2_post", "w_pool")
_SLAB_PART = 8 * _LANES


SLAB_NAMES = SMALL_NAMES + ("loss",)


def _to_slab(parts):
    rows = []
    for name in SLAB_NAMES:
        flat = parts[name].reshape(-1) if name in parts else jnp.zeros((_SLAB_PART,), F32)
        padded = -(-flat.shape[0] // _SLAB_PART) * _SLAB_PART
        rows.append(jnp.pad(flat, (0, padded - flat.shape[0])).reshape(-1, _LANES))
    return jnp.concatenate(rows, axis=0)


def _from_slab(slab, like):
    out, row = {}, 0
    for name in SLAB_NAMES:
        size = like[name].size
        rows = -(-size // _SLAB_PART) * (_SLAB_PART // _LANES)
        out[name] = slab[row:row + rows].reshape(-1)[:size].reshape(like[name].shape)
        row += rows
    return out


BIG_NAMES = ("ffn1_w_gu", "ffn1_w_down", "w_in", "w_out", "ffn2_w_gu", "ffn2_w_down")
WEIGHT_ORDER = ("ffn1_pre", "ffn1_w_gu", "ffn1_w_down", "ffn1_post", "mix_pre", "w_in", "w_pool", "pool_scale",
                "sinks", "g_pool", "g_attn", "w_out", "mix_post", "ffn2_pre", "ffn2_w_gu", "ffn2_w_down", "ffn2_post")


def kernel(x, positions, ffn1_pre, ffn1_w_gu, ffn1_w_down, ffn1_post, mix_pre, w_in, w_pool, pool_scale, sinks, g_pool, g_attn, w_out, mix_post, ffn2_pre, ffn2_w_gu, ffn2_w_down, ffn2_post, loss_target, m_ffn1_pre, m_ffn1_w_gu, m_ffn1_w_down, m_ffn1_post, m_mix_pre, m_w_in, m_w_pool, m_pool_scale, m_sinks, m_g_pool, m_g_attn, m_w_out, m_mix_post, m_ffn2_pre, m_ffn2_w_gu, m_ffn2_w_down, m_ffn2_post, v_ffn1_pre, v_ffn1_w_gu, v_ffn1_w_down, v_ffn1_post, v_mix_pre, v_w_in, v_w_pool, v_pool_scale, v_sinks, v_g_pool, v_g_attn, v_w_out, v_mix_post, v_ffn2_pre, v_ffn2_w_gu, v_ffn2_w_down, v_ffn2_post):
    given = dict(locals())
    weights = {n: given[n][0] for n in WEIGHT_ORDER}
    moments_m = {n: given["m_" + n][0] for n in WEIGHT_ORDER}
    moments_v = {n: given["v_" + n][0] for n in WEIGHT_ORDER}
    S = x.shape[1]
    shard = (2 * lax.axis_index("x") + lax.axis_index("y")).astype(jnp.int32).reshape(1)

    local16 = {n: weights[n].astype(BF16) for n in BIG_NAMES if n != "w_in"}
    local16["w_in"] = weights["w_in"].T.astype(BF16)
    gather_groups = {"ffn1": ("ffn1_w_gu", "ffn1_w_down"), "mixer": ("w_in", "w_out"),
                     "ffn2": ("ffn2_w_gu", "ffn2_w_down")}
    gather_kinds = {"ffn1": GATHER_HALF, "mixer": GATHER, "ffn2": GATHER}
    handles, _ = _exchange_start(list(gather_kinds.values()),
                                 [[local16[n] for n in names] for names in gather_groups.values()], "gather_start")
    gather_handles = dict(zip(gather_groups, handles))

    def weights_of(group, after):
        kind = gather_kinds[group]
        owns, lands = _exchange_wait(kind, gather_handles[group], list(after), "gather_wait_" + group)
        if kind == GATHER_HALF:
            lands = _swap_gathered_halves(lands, "swap_gathered_" + group)
        return list(zip(lands, owns))

    pending, last_token = {}, []

    def grads_ready(group, grads, small=None):
        names = list(grads)
        kinds, sources = [REDUCE], [[grads[n][1] for n in names]]
        if small is not None:
            kinds, sources = kinds + [BROADCAST], sources + [[_to_slab(small)]]
        handles, token = _exchange_start(kinds, sources, "reduce_start_" + group)
        handle = handles[0]
        if small is not None:
            pending["small"] = handles[1]
        pending[group] = (names, handle, [grads[n][0] for n in names])
        last_token[:] = [token]
        return [token]

    small = {n: (weights[n] if weights[n].ndim > 1 else weights[n].reshape(1, -1)) for n in SMALL_NAMES}
    dx = _local_step(x[0], positions.reshape(S, 1), loss_target[0], small, shard, weights_of, grads_ready)

    grad, delta, new_m, new_v = {}, {}, {}, {}

    def finish(groups, after):
        names, partials = [], []
        for group in groups:
            group_names, handle, own32 = pending[group]
            _, received = _exchange_wait(REDUCE, handle, after, "reduce_wait_" + group)
            names += group_names
            partials += [_chip_partial(g32, rec, shard, "chip_partial_" + n)
                         for n, g32, rec in zip(group_names, own32, received)]
        siblings = _swap_with_sibling(partials, "swap_" + groups[0])
        for name, mine, theirs in zip(names, partials, siblings):
            if name == "w_in":
                mine, theirs = mine.T, theirs.T
            grad[name], delta[name], new_m[name], new_v[name] = _adamw(
                weights[name], moments_m[name], moments_v[name], [mine, theirs], "adamw_" + name)
        return [grad[names[-1]]]

    after = finish(["ffn2"], last_token)
    after = finish(["mixer"], after)
    (own_slab,), (slab_landing,) = _exchange_wait(BROADCAST, pending["small"], after, "reduce_wait_small")
    device = (2 * shard + lax.axis_index("c")).astype(jnp.int32)
    small_like = dict({n: small[n] for n in SMALL_NAMES}, loss=jnp.zeros((8, _LANES), F32))
    slabs = _adamw(_to_slab(small), _to_slab({n: moments_m[n] for n in SMALL_NAMES}),
                   _to_slab({n: moments_v[n] for n in SMALL_NAMES}), [own_slab, slab_landing], "adamw_small",
                   slot=device)
    for store, slab in zip((grad, delta, new_m, new_v), slabs):
        store.update(_from_slab(slab, small_like))
    loss = grad["loss"][0, 0]
    after = finish(["ffn1_gu"], [slabs[0]])
    finish(["ffn1_down"], after)

    def out(store):
        return [store[n].reshape(given[n].shape) for n in WEIGHT_ORDER]
    return (loss, dx[None], *out(grad), *out(delta), *out(new_m), *out(new_v))
```

```python
import jax
import jax.numpy as jnp
from jax import lax
from jax.experimental import pallas as pl
from jax.experimental.pallas import tpu as pltpu

F32 = jnp.float32
BF16 = jnp.bfloat16

D_MODEL = 1024
D_FF = 2816
N_SHARD = 4
FF_CHUNK = D_FF // 2
POOL_WINDOWS = (2, 4, 8, 16)
POOL_WIDTH = 512
POOL_GROUP = 128
HALO = 16
HEAD_DIM = 64
N_HEADS = 8
N_KV_HEADS = 2
ATTN_WIDTH = 512
KV_WIDTH = 128
IN_WIDTH = 1280
BLOCK = 128
ATTN_BLOCKS = 4
ROT_DIM = 16
ROPE_THETA = 500000.0
EPS = 1e-6
NEG_INF = -1e30
ATTN_SCALE = HEAD_DIM ** -0.5

ADAM_LR = 0.001
ADAM_B1 = 0.9
ADAM_B2 = 0.999
ADAM_EPS = 1e-08
ADAM_WD = 0.01
ADAM_STEP = 10

VMEM_LIMIT = 60 * 1024 * 1024
FFN_FWD_TILE = 512
FFN_BWD_TILE = 256
MIXER_TILE = 512

MESH = pl.DeviceIdType.MESH


def _params(**kw):
    return pltpu.CompilerParams(vmem_limit_bytes=VMEM_LIMIT, **kw)


def _dot(a, b):
    return jnp.dot(a, b, preferred_element_type=F32)


def _dot_nt(a, b):
    return lax.dot_general(a, b, (((1,), (1,)), ((), ())), preferred_element_type=F32)


def _dot_tn(a, b):
    return lax.dot_general(a, b, (((0,), (0,)), ((), ())), preferred_element_type=F32)


def _rstd(x):
    return lax.rsqrt(jnp.mean(x * x, axis=-1, keepdims=True) + EPS)


def _norm_bwd(dy, xn, r, gain):
    dxn = dy * gain
    return r * (dxn - xn * jnp.mean(dxn * xn, axis=-1, keepdims=True))


def _sigmoid(x):
    return 1.0 / (1.0 + jnp.exp(-x))


def _full(shape):
    return pl.BlockSpec(shape, lambda *_: (0,) * len(shape))


def _rows(tile, width):
    return pl.BlockSpec((tile, width), lambda i: (i, 0))


_ANY = pl.BlockSpec(memory_space=pl.ANY)


_SMEM = pl.BlockSpec(memory_space=pltpu.SMEM)


def _load_once(pairs, sem):
    @pl.when(pl.program_id(0) == 0)
    def _():
        copies = [pltpu.make_async_copy(src, dst, sem.at[n]) for n, (src, dst) in enumerate(pairs)]
        for cp in copies:
            cp.start()
        for cp in copies:
            cp.wait()


def _gathered(land_ref, own_ref, vmem_ref, mine, rows=None):
    def dst(slot):
        if rows is None:
            return vmem_ref.at[slot]
        return vmem_ref.at[pl.ds(pl.multiple_of(slot * rows, 16), rows), :]
    pairs = [(land_ref.at[(mine + d) % N_SHARD], dst((mine + d) % N_SHARD)) for d in range(1, N_SHARD)]
    return pairs + [(own_ref, dst(mine))]


def _ignoring(body, start, count):
    def wrapped(*refs):
        return body(*refs[:start], *refs[start + count:])
    return wrapped


def _ffn_fwd(x, pre, post, wgu, wd, mine, target=None, deps=()):
    S = x.shape[0]
    tm = FFN_FWD_TILE
    with_loss = target is not None

    def body(*refs):
        if with_loss:
            (x_ref, pre_ref, post_ref, wgu_land, wgu_own, wd_land, wd_own, mine_ref, tgt_ref,
             out_ref, g_ref, u_ref, f_ref, loss_ref, dpost_ref, wgu_v, wd_v, sem) = refs
        else:
            (x_ref, pre_ref, post_ref, wgu_land, wgu_own, wd_land, wd_own, mine_ref,
             out_ref, g_ref, u_ref, f_ref, wgu_v, wd_v, sem) = refs
        _load_once(_gathered(wgu_land, wgu_own, wgu_v, mine_ref[0])
                   + _gathered(wd_land, wd_own, wd_v, mine_ref[0], rows=D_FF // N_SHARD), sem)
        xv = x_ref[...]
        h = ((xv * _rstd(xv)) * pre_ref[...]).astype(BF16)
        facc = jnp.zeros((tm, D_MODEL), F32)
        for c in range(2):
            cols = slice(c * FF_CHUNK, (c + 1) * FF_CHUNK)
            g = _dot(h, wgu_v[c])
            u = _dot(h, wgu_v[2 + c])
            g_ref[:, cols] = g.astype(BF16)
            u_ref[:, cols] = u.astype(BF16)
            a = (g * _sigmoid(g)) * u
            facc = facc + _dot(a.astype(BF16), wd_v[cols, :])
        rf = _rstd(facc)
        fn = facc * rf
        out = xv + 0.5 * (fn * post_ref[...])
        if with_loss:
            diff = out - tgt_ref[...]
            dout = diff * (1.0 / D_MODEL)
            out_ref[...] = dout

            @pl.when(pl.program_id(0) == 0)
            def _():
                loss_ref[...] = jnp.zeros_like(loss_ref)
                dpost_ref[...] = jnp.zeros_like(dpost_ref)
            loss_ref[...] += jnp.sum(diff * diff)
            dn = 0.5 * dout
            dpost_ref[...] += jnp.sum(dn * fn, axis=0, keepdims=True)
            f_ref[...] = _norm_bwd(dn, fn, rf, post_ref[...]).astype(BF16)
        else:
            f_ref[...] = facc
            out_ref[...] = out

    in_specs = [_rows(tm, D_MODEL), _full((1, D_MODEL)), _full((1, D_MODEL)), _ANY, _ANY, _ANY, _ANY, _SMEM]
    args = [x, pre, post, *wgu, *wd, mine]
    out_shape = [jax.ShapeDtypeStruct((S, D_MODEL), F32), jax.ShapeDtypeStruct((S, D_FF), BF16),
                 jax.ShapeDtypeStruct((S, D_FF), BF16), jax.ShapeDtypeStruct((S, D_MODEL), F32)]
    out_specs = [_rows(tm, D_MODEL), _rows(tm, D_FF), _rows(tm, D_FF), _rows(tm, D_MODEL)]
    if with_loss:
        in_specs.append(_rows(tm, D_MODEL))
        args.append(target)
        out_shape[3] = jax.ShapeDtypeStruct((S, D_MODEL), BF16)
        out_shape += [jax.ShapeDtypeStruct((8, 128), F32), jax.ShapeDtypeStruct((1, D_MODEL), F32)]
        out_specs += [_full((8, 128)), _full((1, D_MODEL))]
    return pl.pallas_call(
        _ignoring(body, len(args), len(deps)), name="ffn_fwd_loss" if with_loss else "ffn_fwd",
        grid=(S // tm,), in_specs=in_specs + [_ANY] * len(deps), out_specs=out_specs, out_shape=out_shape,
        scratch_shapes=[pltpu.VMEM((N_SHARD, D_MODEL, FF_CHUNK), BF16), pltpu.VMEM((D_FF, D_MODEL), BF16),
                        pltpu.SemaphoreType.DMA((2 * N_SHARD,))],
        compiler_params=_params(dimension_semantics=("arbitrary",)),
    )(*args, *deps)


def _ffn_bwd(dout, x, f, g, u, pre, post, wgu, wd, mine, df_known=False, deps=()):
    S = x.shape[0]
    tm = FFN_BWD_TILE

    def body(*refs):
        if df_known:
            (dout_ref, x_ref, f_ref, g_ref, u_ref, pre_ref, post_ref, wgu_land, wgu_own, wd_land, wd_own, mine_ref,
             dx_ref, h_ref, dgu_ref, a_ref, dpre_ref, wgu_v, wd_v, sem) = refs
        else:
            (dout_ref, x_ref, f_ref, g_ref, u_ref, pre_ref, post_ref, wgu_land, wgu_own, wd_land, wd_own, mine_ref,
             dx_ref, h_ref, dgu_ref, a_ref, df_ref, dpre_ref, dpost_ref, wgu_v, wd_v, sem) = refs
        _load_once(_gathered(wgu_land, wgu_own, wgu_v, mine_ref[0])
                   + _gathered(wd_land, wd_own, wd_v, mine_ref[0], rows=D_FF // N_SHARD), sem)

        @pl.when(pl.program_id(0) == 0)
        def _():
            dpre_ref[...] = jnp.zeros_like(dpre_ref)
            if not df_known:
                dpost_ref[...] = jnp.zeros_like(dpost_ref)

        dout_v = dout_ref[...]
        if df_known:
            df = f_ref[...]
        else:
            dn = 0.5 * dout_v
            fv = f_ref[...]
            rf = _rstd(fv)
            fn = fv * rf
            dpost_ref[...] += jnp.sum(dn * fn, axis=0, keepdims=True)
            df = _norm_bwd(dn, fn, rf, post_ref[...]).astype(BF16)
            df_ref[...] = df
        dh = jnp.zeros((tm, D_MODEL), F32)
        for c in range(2):
            cols = slice(c * FF_CHUNK, (c + 1) * FF_CHUNK)
            da = _dot_nt(df, wd_v[cols, :])
            gv = g_ref[:, cols].astype(F32)
            uv = u_ref[:, cols].astype(F32)
            sg = _sigmoid(gv)
            silu = gv * sg
            a_ref[:, cols] = (silu * uv).astype(BF16)
            dg = ((da * uv) * (sg * (1.0 + gv * (1.0 - sg)))).astype(BF16)
            du = (da * silu).astype(BF16)
            dgu_ref[:, cols] = dg
            dgu_ref[:, 2 * FF_CHUNK + c * FF_CHUNK:2 * FF_CHUNK + (c + 1) * FF_CHUNK] = du
            dh = dh + _dot_nt(dg, wgu_v[c]) + _dot_nt(du, wgu_v[2 + c])
        xv = x_ref[...]
        rx = _rstd(xv)
        xn = xv * rx
        h_ref[...] = (xn * pre_ref[...]).astype(BF16)
        dpre_ref[...] += jnp.sum(dh * xn, axis=0, keepdims=True)
        dx_ref[...] = dout_v + _norm_bwd(dh, xn, rx, pre_ref[...])

    args = [dout, x, f, g, u, pre, post, *wgu, *wd, mine]
    out_specs = [_rows(tm, D_MODEL), _rows(tm, D_MODEL), _rows(tm, 2 * D_FF), _rows(tm, D_FF), _rows(tm, D_MODEL),
                 _full((1, D_MODEL)), _full((1, D_MODEL))]
    out_shape = [jax.ShapeDtypeStruct((S, D_MODEL), F32), jax.ShapeDtypeStruct((S, D_MODEL), BF16),
                 jax.ShapeDtypeStruct((S, 2 * D_FF), BF16), jax.ShapeDtypeStruct((S, D_FF), BF16),
                 jax.ShapeDtypeStruct((S, D_MODEL), BF16),
                 jax.ShapeDtypeStruct((1, D_MODEL), F32), jax.ShapeDtypeStruct((1, D_MODEL), F32)]
    if df_known:
        out_specs = out_specs[:4] + out_specs[5:6]
        out_shape = out_shape[:4] + out_shape[5:6]
    return pl.pallas_call(
        _ignoring(body, len(args), len(deps)), name="ffn_bwd_from_df" if df_known else "ffn_bwd", grid=(S // tm,),
        in_specs=[_rows(tm, D_MODEL), _rows(tm, D_MODEL), _rows(tm, D_MODEL), _rows(tm, D_FF), _rows(tm, D_FF),
                  _full((1, D_MODEL)), _full((1, D_MODEL)), _ANY, _ANY, _ANY, _ANY, _SMEM] + [_ANY] * len(deps),
        out_specs=out_specs, out_shape=out_shape,
        scratch_shapes=[pltpu.VMEM((N_SHARD, D_MODEL, FF_CHUNK), BF16), pltpu.VMEM((D_FF, D_MODEL), BF16),
                        pltpu.SemaphoreType.DMA((2 * N_SHARD,))],
        compiler_params=_params(dimension_semantics=("arbitrary",)),
    )(*args, *deps)


def _wgrad(lhs, rhs, m_block, n_block, name, column_shards=False, tk=2048, deps=()):
    S, M = lhs.shape
    N = rhs.shape[1]
    k_steps = S // tk

    def body(lhs_ref, rhs_ref, out_ref, out16_ref):
        k = pl.program_id(2)

        @pl.when(k == 0)
        def _():
            out_ref[...] = jnp.zeros_like(out_ref)
        out_ref[...] += _dot_tn(lhs_ref[...], rhs_ref[...])

        @pl.when(k == k_steps - 1)
        def _():
            out16_ref[...] = out_ref[...].astype(BF16)

    if column_shards:
        assert N == N_SHARD * n_block
        shape = (N_SHARD, M, n_block)
        out_spec = pl.BlockSpec((None, m_block, n_block), lambda i, j, k: (j, i, 0))
    else:
        shape = (M, N)
        out_spec = pl.BlockSpec((m_block, n_block), lambda i, j, k: (i, j))
    out, out16 = pl.pallas_call(
        _ignoring(body, 2, len(deps)), name=name, grid=(M // m_block, N // n_block, k_steps),
        in_specs=[pl.BlockSpec((tk, m_block), lambda i, j, k: (k, i)),
                  pl.BlockSpec((tk, n_block), lambda i, j, k: (k, j))] + [_ANY] * len(deps),
        out_specs=[out_spec, out_spec],
        out_shape=[jax.ShapeDtypeStruct(shape, F32), jax.ShapeDtypeStruct(shape, BF16)],
        compiler_params=_params(dimension_semantics=("arbitrary", "arbitrary", "arbitrary")),
    )(lhs, rhs, *deps)
    if not column_shards:
        out = out.reshape(N_SHARD, M // N_SHARD, N)
        out16 = out16.reshape(N_SHARD, M // N_SHARD, N)
    return out, out16


def _rope_tables(pos, invf):
    S = pos.shape[0]
    tm = MIXER_TILE

    def body(pos_ref, invf_ref, out_ref):
        ang = pos_ref[...].astype(F32) * invf_ref[...]
        cos, sin = jnp.cos(ang), jnp.sin(ang)
        lane = lax.broadcasted_iota(jnp.int32, ang.shape, 1) % HEAD_DIM
        first = lane < ROT_DIM // 2
        second = (lane >= ROT_DIM // 2) & (lane < ROT_DIM)
        out_ref[0] = jnp.where(lane < ROT_DIM, cos, 1.0)
        out_ref[1] = jnp.where(first, sin, 0.0)
        out_ref[2] = jnp.where(second, sin, 0.0)

    return pl.pallas_call(
        body, name="rope_tables", grid=(S // tm,),
        in_specs=[_rows(tm, 1), _full((1, _LANES))],
        out_specs=pl.BlockSpec((3, tm, _LANES), lambda i: (0, i, 0)),
        out_shape=jax.ShapeDtypeStruct((3, S, _LANES), F32),
        compiler_params=_params(dimension_semantics=("arbitrary",)),
    )(pos, invf)


def _table_spec(tm):
    return pl.BlockSpec((3, tm, _LANES), lambda i: (0, i, 0))


_HALF = ROT_DIM // 2
_LANES = 128


def _rope(t, tables):
    c, s_first, s_second = tables
    return t * c - pltpu.roll(t, _LANES - _HALF, axis=1) * s_first + pltpu.roll(t, _HALF, axis=1) * s_second


def _rope_transposed(t, tables):
    c, s_first, s_second = tables
    return t * c - pltpu.roll(t * s_first, _HALF, axis=1) + pltpu.roll(t * s_second, _LANES - _HALF, axis=1)


def _store_head_variants(ref, t):
    rolled = pltpu.roll(t, HEAD_DIM, axis=1)
    low = lax.broadcasted_iota(jnp.int32, t.shape, 1) < HEAD_DIM
    zero = jnp.zeros_like(t)
    ref[0] = jnp.where(low, t, zero).astype(BF16)
    ref[1] = jnp.where(low, zero, rolled).astype(BF16)
    ref[2] = jnp.where(low, rolled, zero).astype(BF16)
    ref[3] = jnp.where(low, zero, t).astype(BF16)


def _mixer_in_fwd(x, pre, w_in_t, mine, rope, deps=()):
    S = x.shape[0]
    tm = MIXER_TILE

    def body(x_ref, pre_ref, w_land, w_own, mine_ref, rope_ref, u_ref, q_ref, k_ref, v_ref, w_v, sem):
        _load_once(_gathered(w_land, w_own, w_v, mine_ref[0], rows=IN_WIDTH // N_SHARD), sem)
        xv = x_ref[...]
        h = ((xv * _rstd(xv)) * pre_ref[...]).astype(BF16)
        z = _dot_nt(h, w_v[...])
        tables = (rope_ref[0], rope_ref[1], rope_ref[2])
        u_ref[...] = z[:, :POOL_WIDTH]
        for t in range(ATTN_WIDTH // _LANES):
            lo = POOL_WIDTH + t * _LANES
            q_ref[:, t * _LANES:(t + 1) * _LANES] = (_rope(z[:, lo:lo + _LANES], tables) * ATTN_SCALE).astype(BF16)
        kv = POOL_WIDTH + ATTN_WIDTH
        _store_head_variants(k_ref, _rope(z[:, kv:kv + KV_WIDTH], tables))
        _store_head_variants(v_ref, z[:, kv + KV_WIDTH:])

    args = [x, pre, *w_in_t, mine, rope]
    variants = pl.BlockSpec((2 * N_KV_HEADS, tm, KV_WIDTH), lambda i: (0, i, 0))
    return pl.pallas_call(
        _ignoring(body, len(args), len(deps)), name="mixer_in_fwd", grid=(S // tm,),
        in_specs=[_rows(tm, D_MODEL), _full((1, D_MODEL)), _ANY, _ANY, _SMEM, _table_spec(tm)] + [_ANY] * len(deps),
        out_specs=[_rows(tm, POOL_WIDTH), _rows(tm, ATTN_WIDTH), variants, variants],
        out_shape=[jax.ShapeDtypeStruct((S, POOL_WIDTH), F32), jax.ShapeDtypeStruct((S, ATTN_WIDTH), BF16),
                   jax.ShapeDtypeStruct((2 * N_KV_HEADS, S, KV_WIDTH), BF16),
                   jax.ShapeDtypeStruct((2 * N_KV_HEADS, S, KV_WIDTH), BF16)],
        scratch_shapes=[pltpu.VMEM((IN_WIDTH, D_MODEL), BF16), pltpu.SemaphoreType.DMA((N_SHARD,))],
        compiler_params=_params(dimension_semantics=("arbitrary",)),
    )(*args, *deps)


def _mixer_in_bwd(dres, x, pre, w_in_t, mine, du, dq, dk, dv, dk_next, dv_next, rope, deps=()):
    S = x.shape[0]
    tm = MIXER_TILE

    def body(dres_ref, x_ref, pre_ref, w_land, w_own, mine_ref, du_ref, dq_ref, dk_ref, dv_ref, dkx_ref, dvx_ref,
             rope_ref,
             dx_ref, dz_ref, h_ref, dpre_ref, w_v, sem):
        _load_once(_gathered(w_land, w_own, w_v, mine_ref[0], rows=IN_WIDTH // N_SHARD), sem)

        @pl.when(pl.program_id(0) == 0)
        def _():
            dpre_ref[...] = jnp.zeros_like(dpre_ref)

        tables = (rope_ref[0], rope_ref[1], rope_ref[2])
        dz_ref[:, :POOL_WIDTH] = du_ref[...]
        for t in range(ATTN_WIDTH // _LANES):
            lo = POOL_WIDTH + t * _LANES
            dz_ref[:, lo:lo + _LANES] = _rope_transposed(dq_ref[:, t * _LANES:(t + 1) * _LANES], tables).astype(BF16)
        kv = POOL_WIDTH + ATTN_WIDTH
        has_next = pl.program_id(0) + 1 < steps
        pad = jnp.zeros((tm - BLOCK, KV_WIDTH), F32)
        dk_tile = dk_ref[...] + jnp.concatenate([pad, jnp.where(has_next, dkx_ref[...], 0.0)], axis=0)
        dv_tile = dv_ref[...] + jnp.concatenate([pad, jnp.where(has_next, dvx_ref[...], 0.0)], axis=0)
        dz_ref[:, kv:kv + KV_WIDTH] = _rope_transposed(dk_tile, tables).astype(BF16)
        dz_ref[:, kv + KV_WIDTH:] = dv_tile.astype(BF16)
        dh = _dot(dz_ref[...], w_v[...])
        xv = x_ref[...]
        rx = _rstd(xv)
        xn = xv * rx
        h_ref[...] = (xn * pre_ref[...]).astype(BF16)
        dpre_ref[...] += jnp.sum(dh * xn, axis=0, keepdims=True)
        dx_ref[...] = dres_ref[...] + _norm_bwd(dh, xn, rx, pre_ref[...])

    assert tm == ATTN_BLOCKS * BLOCK
    steps = S // tm
    nxt = pl.BlockSpec((None, BLOCK, KV_WIDTH), lambda i: (jnp.minimum(i + 1, steps - 1), 0, 0))
    args = [dres, x, pre, *w_in_t, mine, du, dq, dk, dv, dk_next, dv_next, rope]
    return pl.pallas_call(
        _ignoring(body, len(args), len(deps)), name="mixer_in_bwd", grid=(S // tm,),
        in_specs=[_rows(tm, D_MODEL), _rows(tm, D_MODEL), _full((1, D_MODEL)), _ANY, _ANY, _SMEM,
                  _rows(tm, POOL_WIDTH), _rows(tm, ATTN_WIDTH), _rows(tm, KV_WIDTH), _rows(tm, KV_WIDTH), nxt, nxt,
                  _table_spec(tm)] + [_ANY] * len(deps),
        out_specs=[_rows(tm, D_MODEL), _rows(tm, IN_WIDTH), _rows(tm, D_MODEL), _full((1, D_MODEL))],
        out_shape=[jax.ShapeDtypeStruct((S, D_MODEL), F32), jax.ShapeDtypeStruct((S, IN_WIDTH), BF16),
                   jax.ShapeDtypeStruct((S, D_MODEL), BF16), jax.ShapeDtypeStruct((1, D_MODEL), F32)],
        scratch_shapes=[pltpu.VMEM((IN_WIDTH, D_MODEL), BF16), pltpu.SemaphoreType.DMA((N_SHARD,))],
        compiler_params=_params(dimension_semantics=("arbitrary",)),
    )(*args, *deps)


def _pool_counts(tile_index, tm, width):
    t = tile_index * tm + lax.broadcasted_iota(jnp.int32, (tm, 1), 0)
    return jnp.minimum(t + 1, width).astype(F32)


def _pool_features(ext, u_tile, tile_index, tm):
    ds = []
    for gi, width in enumerate(POOL_WINDOWS):
        lanes = slice(gi * POOL_GROUP, (gi + 1) * POOL_GROUP)
        s = ext[:, lanes]
        shift = 1
        while shift < width:
            s = s + pltpu.roll(s, shift, axis=0)
            shift *= 2
        ds.append(s[HALO:, :] / _pool_counts(tile_index, tm, width) - u_tile[:, lanes])
    return ds


def _pool_fwd(u, w_pool, pool_scale, g_pool):
    S = u.shape[0]
    tm = MIXER_TILE

    def body(u_ref, w_ref, scale_ref, gain_ref, y_ref, ext_ref):
        i = pl.program_id(0)

        @pl.when(i == 0)
        def _():
            ext_ref[:HALO, :] = jnp.zeros((HALO, POOL_WIDTH), F32)

        u_tile = u_ref[...]
        ext_ref[HALO:, :] = u_tile
        ds = _pool_features(ext_ref[...], u_tile, i, tm)
        ext_ref[:HALO, :] = u_tile[tm - HALO:, :]
        ys = [_dot(ds[gi].astype(BF16), w_ref[gi].astype(BF16)) for gi in range(len(POOL_WINDOWS))]
        po = jnp.concatenate(ys, axis=1) * scale_ref[...]
        y_ref[...] = ((po * _rstd(po)) * gain_ref[...]).astype(BF16)

    return pl.pallas_call(
        body, name="pool_fwd", grid=(S // tm,),
        in_specs=[_rows(tm, POOL_WIDTH), _full((len(POOL_WINDOWS), POOL_GROUP, POOL_GROUP)),
                  _full((1, POOL_WIDTH)), _full((1, POOL_WIDTH))],
        out_specs=_rows(tm, POOL_WIDTH),
        out_shape=jax.ShapeDtypeStruct((S, POOL_WIDTH), BF16),
        scratch_shapes=[pltpu.VMEM((HALO + tm, POOL_WIDTH), F32)],
        compiler_params=_params(dimension_semantics=("arbitrary",)),
    )(u, w_pool, pool_scale, g_pool)


def _pool_bwd(dy, u, w_pool, pool_scale, g_pool):
    S = u.shape[0]
    tm = MIXER_TILE
    n_tiles = S // tm
    halo_blocks = tm // HALO

    def body(dy_ref, u_ref, uprev_ref, w_ref, scale_ref, gain_ref,
             du_ref, dw_ref, dscale_ref, dgain_ref, ext_ref, nxt_ref):
        i = pl.program_id(0)
        tile = n_tiles - 1 - i

        @pl.when(i == 0)
        def _():
            dw_ref[...] = jnp.zeros_like(dw_ref)
            dscale_ref[...] = jnp.zeros_like(dscale_ref)
            dgain_ref[...] = jnp.zeros_like(dgain_ref)
            nxt_ref[...] = jnp.zeros_like(nxt_ref)

        u_tile = u_ref[...]
        ext_ref[:HALO, :] = jnp.where(tile > 0, uprev_ref[...], 0.0)
        ext_ref[HALO:, :] = u_tile
        ds = _pool_features(ext_ref[...], u_tile, tile, tm)
        dsb = [d.astype(BF16) for d in ds]
        wb = [w_ref[gi].astype(BF16) for gi in range(len(POOL_WINDOWS))]
        yraw = jnp.concatenate([_dot(dsb[gi], wb[gi]) for gi in range(len(POOL_WINDOWS))], axis=1)
        po = yraw * scale_ref[...]
        r = _rstd(po)
        pn = po * r
        dyv = dy_ref[...]
        dgain_ref[...] += jnp.sum(dyv * pn, axis=0, keepdims=True)
        dpo = _norm_bwd(dyv, pn, r, gain_ref[...])
        dscale_ref[...] += jnp.sum(dpo * yraw, axis=0, keepdims=True)
        dyraw = (dpo * scale_ref[...]).astype(BF16)
        for gi, width in enumerate(POOL_WINDOWS):
            lanes = slice(gi * POOL_GROUP, (gi + 1) * POOL_GROUP)
            dw_ref[gi] += _dot_tn(dsb[gi], dyraw[:, lanes])
            dd = _dot_nt(dyraw[:, lanes], wb[gi])
            ddc = dd / _pool_counts(tile, tm, width)
            ext_ref[:tm, lanes] = ddc
            ext_ref[tm:, lanes] = nxt_ref[:, lanes]
            s = ext_ref[:, lanes]
            shift = 1
            while shift < width:
                s = s + pltpu.roll(s, HALO + tm - shift, axis=0)
                shift *= 2
            du_ref[:, lanes] = (s[:tm, :] - dd).astype(BF16)
            nxt_ref[:, lanes] = ddc[:HALO, :]

    return pl.pallas_call(
        body, name="pool_bwd", grid=(n_tiles,),
        in_specs=[pl.BlockSpec((tm, POOL_WIDTH), lambda i: (n_tiles - 1 - i, 0)),
                  pl.BlockSpec((tm, POOL_WIDTH), lambda i: (n_tiles - 1 - i, 0)),
                  pl.BlockSpec((HALO, POOL_WIDTH), lambda i: (jnp.maximum((n_tiles - 1 - i) * halo_blocks - 1, 0), 0)),
                  _full((len(POOL_WINDOWS), POOL_GROUP, POOL_GROUP)), _full((1, POOL_WIDTH)), _full((1, POOL_WIDTH))],
        out_specs=[pl.BlockSpec((tm, POOL_WIDTH), lambda i: (n_tiles - 1 - i, 0)),
                   _full((len(POOL_WINDOWS), POOL_GROUP, POOL_GROUP)), _full((1, POOL_WIDTH)), _full((1, POOL_WIDTH))],
        out_shape=[jax.ShapeDtypeStruct((S, POOL_WIDTH), BF16),
                   jax.ShapeDtypeStruct((len(POOL_WINDOWS), POOL_GROUP, POOL_GROUP), F32),
                   jax.ShapeDtypeStruct((1, POOL_WIDTH), F32), jax.ShapeDtypeStruct((1, POOL_WIDTH), F32)],
        scratch_shapes=[pltpu.VMEM((HALO + tm, POOL_WIDTH), F32), pltpu.VMEM((HALO, POOL_WIDTH), F32)],
        compiler_params=_params(dimension_semantics=("arbitrary",)),
    )(dy, u, u, w_pool, pool_scale, g_pool)


def _variant(head):
    return 2 * (head // (N_HEADS // N_KV_HEADS)) + head % 2


def _own_block(shape=(BLOCK, BLOCK)):
    r = lax.broadcasted_iota(jnp.int32, shape, 0)
    i = lax.broadcasted_iota(jnp.int32, shape, 1)
    return r <= i


def _fold_band(own, from_own, from_prev):
    return jnp.where(own, from_own, from_prev)


def _scores_by_head(own_tiles, prev_tiles, q_tiles):
    stacks = [jnp.concatenate(q_tiles[:2], axis=0), jnp.concatenate(q_tiles[2:], axis=0)]
    by_var = [_dot_nt(jnp.concatenate([own_tiles[v], prev_tiles[v]], axis=0), stacks[v // 2])
              for v in range(2 * N_KV_HEADS)]
    quadrant = lambda h, rows: by_var[_variant(h)][rows * BLOCK:(rows + 1) * BLOCK,
                                                   ((h // 2) % 2) * BLOCK:((h // 2) % 2 + 1) * BLOCK]
    return [quadrant(h, 0) for h in range(N_HEADS)], [quadrant(h, 1) for h in range(N_HEADS)]


def _softmax_t(s, sink):
    m = jnp.maximum(jnp.max(s, axis=0, keepdims=True), sink)
    p = jnp.exp(s - m)
    p_sink = jnp.exp(sink - m)
    inv = 1.0 / (jnp.sum(p, axis=0, keepdims=True) + p_sink)
    return p * inv, p_sink * inv


def _attn_fwd(q, kz, vz, sinks, g_attn, y_pool, x, w_out, mine, post):
    S = q.shape[0]
    tq = ATTN_BLOCKS * BLOCK
    n_var = 2 * N_KV_HEADS

    def body(q_ref, kp_ref, kc_ref, vp_ref, vc_ref, sinks_ref, gain_ref, yp_ref, x_ref, w_land, w_own, mine_ref,
             post_ref, o_ref, out_ref, m_ref, y_ref, w_v, sem):
        _load_once(_gathered(w_land, w_own, w_v, mine_ref[0], rows=D_MODEL // N_SHARD), sem)
        step = pl.program_id(0)
        own = _own_block()
        zero = jnp.zeros((BLOCK, BLOCK), F32)

        def tiles(cur_ref, prev_ref, j):
            rows = lambda jj: slice(jj * BLOCK, (jj + 1) * BLOCK)
            return ([cur_ref[v, rows(j), :] for v in range(n_var)],
                    [prev_ref[v] if j == 0 else cur_ref[v, rows(j - 1), :] for v in range(n_var)])

        scores = []
        for j in range(ATTN_BLOCKS):
            q_pairs = [q_ref[j * BLOCK:(j + 1) * BLOCK, i * _LANES:(i + 1) * _LANES] for i in range(N_HEADS // 2)]
            scores.append(_scores_by_head(*tiles(kc_ref, kp_ref, j), q_pairs))
        probs = []
        for j in range(ATTN_BLOCKS):
            s_own, s_prev = scores[j]
            no_prev = jnp.where(step > 0, 0.0, NEG_INF) if j == 0 else 0.0
            p_own, p_prev = [], []
            for h in range(N_HEADS):
                p, _ = _softmax_t(_fold_band(own, s_own[h], s_prev[h] + no_prev), sinks_ref[0, h])
                p_own.append(jnp.where(own, p, zero).astype(BF16))
                p_prev.append(jnp.where(own, zero, p).astype(BF16))
            probs.append((p_own, p_prev))
        blocks = []
        for j in range(ATTN_BLOCKS):
            p_own, p_prev = probs[j]
            v_own, v_prev = tiles(vc_ref, vp_ref, j)
            pairs = []
            for i in range(N_HEADS // 2):
                acc = None
                for h in (2 * i, 2 * i + 1):
                    part = _dot_tn(p_own[h], v_own[_variant(h)]) + _dot_tn(p_prev[h], v_prev[_variant(h)])
                    acc = part if acc is None else acc + part
                pairs.append(acc)
            blocks.append(jnp.concatenate(pairs, axis=1))
        o = jnp.concatenate(blocks, axis=0)
        o_ref[...] = o
        y_ref[:, :POOL_WIDTH] = yp_ref[...]
        y_ref[:, POOL_WIDTH:] = ((o * _rstd(o)) * gain_ref[...]).astype(BF16)
        m = _dot(y_ref[...], w_v[...])
        m_ref[...] = m
        out_ref[...] = x_ref[...] + (m * _rstd(m)) * post_ref[...]

    prev = pl.BlockSpec((n_var, BLOCK, KV_WIDTH), lambda g: (0, jnp.maximum(g * ATTN_BLOCKS - 1, 0), 0))
    cur = pl.BlockSpec((n_var, tq, KV_WIDTH), lambda g: (0, g, 0))
    return pl.pallas_call(
        body, name="attn_fwd", grid=(S // tq,),
        in_specs=[_rows(tq, ATTN_WIDTH), prev, cur, prev, cur,
                  pl.BlockSpec(memory_space=pltpu.SMEM), _full((1, ATTN_WIDTH)),
                  _rows(tq, POOL_WIDTH), _rows(tq, D_MODEL), _ANY, _ANY, _SMEM, _full((1, D_MODEL))],
        out_specs=[_rows(tq, ATTN_WIDTH), _rows(tq, D_MODEL), _rows(tq, D_MODEL), _rows(tq, D_MODEL)],
        out_shape=[jax.ShapeDtypeStruct((S, ATTN_WIDTH), F32), jax.ShapeDtypeStruct((S, D_MODEL), F32),
                   jax.ShapeDtypeStruct((S, D_MODEL), F32), jax.ShapeDtypeStruct((S, D_MODEL), BF16)],
        scratch_shapes=[pltpu.VMEM((D_MODEL, D_MODEL), BF16), pltpu.SemaphoreType.DMA((N_SHARD,))],
        compiler_params=_params(dimension_semantics=("arbitrary",)),
    )(q, kz, kz, vz, vz, sinks, g_attn, y_pool, x, *w_out, mine, post)


def _attn_bwd(dout, m, w_out, mine, post, o, q, kz, vz, sinks, g_attn, deps=()):
    S = q.shape[0]
    tq = ATTN_BLOCKS * BLOCK
    n_var = 2 * N_KV_HEADS

    def body(dout_ref, m_ref, w_land, w_own, mine_ref, post_ref, o_ref, q_ref, kp_ref, kc_ref, vp_ref, vc_ref,
             sinks_ref, gain_ref, dyp_ref, dm_ref, dpost_ref, dq_ref, dk_ref, dv_ref, dkx_ref, dvx_ref, dsink_ref,
             dgain_ref, w_v, sem):
        _load_once(_gathered(w_land, w_own, w_v, mine_ref[0], rows=D_MODEL // N_SHARD), sem)
        step = pl.program_id(0)

        @pl.when(step == 0)
        def _():
            dpost_ref[...] = jnp.zeros_like(dpost_ref)
            dsink_ref[...] = jnp.zeros_like(dsink_ref)
            dgain_ref[...] = jnp.zeros_like(dgain_ref)

        mv = m_ref[...]
        rm = _rstd(mv)
        mn = mv * rm
        dres = dout_ref[...]
        dpost_ref[...] += jnp.sum(dres * mn, axis=0, keepdims=True)
        dm = _norm_bwd(dres, mn, rm, post_ref[...]).astype(BF16)
        dm_ref[...] = dm
        dy = _dot_nt(dm, w_v[...])
        dyp_ref[...] = dy[:, :POOL_WIDTH]
        ov = o_ref[...]
        r = _rstd(ov)
        on = ov * r
        dyv = dy[:, POOL_WIDTH:]
        dgain_ref[...] += jnp.sum(dyv * on, axis=0, keepdims=True)
        do = _norm_bwd(dyv, on, r, gain_ref[...]).astype(BF16)
        own = _own_block()
        zero = jnp.zeros((BLOCK, BLOCK), F32)
        split = lambda t: (jnp.where(own, t, zero).astype(BF16), jnp.where(own, zero, t).astype(BF16))
        rows = lambda j: slice(j * BLOCK, (j + 1) * BLOCK)
        heads = range(N_HEADS)

        def tiles(cur_ref, prev_ref, j):
            return ([cur_ref[v, rows(j), :] for v in range(n_var)],
                    [prev_ref[v] if j == 0 else cur_ref[v, rows(j - 1), :] for v in range(n_var)])

        q_pairs = [[q_ref[rows(j), i * _LANES:(i + 1) * _LANES] for i in range(N_HEADS // 2)] for j in range(ATTN_BLOCKS)]
        do_pairs = [[do[rows(j), i * _LANES:(i + 1) * _LANES] for i in range(N_HEADS // 2)] for j in range(ATTN_BLOCKS)]
        scores = [(_scores_by_head(*tiles(kc_ref, kp_ref, j), q_pairs[j]),
                   _scores_by_head(*tiles(vc_ref, vp_ref, j), do_pairs[j])) for j in range(ATTN_BLOCKS)]
        parts, sink_sum = [], None
        for j in range(ATTN_BLOCKS):
            (s_own, s_prev), (dp_own, dp_prev) = scores[j]
            no_prev = jnp.where(step > 0, 0.0, NEG_INF) if j == 0 else 0.0
            ds_parts, p_parts, sink_rows = [], [], []
            for h in heads:
                p, p_sink = _softmax_t(_fold_band(own, s_own[h], s_prev[h] + no_prev), sinks_ref[0, h])
                dp = _fold_band(own, dp_own[h], dp_prev[h])
                delta = jnp.sum(p * dp, axis=0, keepdims=True)
                ds_parts.append(split(p * (dp - delta)))
                p_parts.append(split(p))
                sink_rows.append(jnp.zeros((1, _LANES), F32) - jnp.sum(p_sink * delta))
            block_sinks = jnp.concatenate(sink_rows, axis=0)
            sink_sum = block_sinks if sink_sum is None else sink_sum + block_sinks
            parts.append((ds_parts, p_parts))
        dsink_ref[...] += sink_sum
        low = lax.broadcasted_iota(jnp.int32, (BLOCK, _LANES), 1) < HEAD_DIM

        def merge(acc):
            return jnp.where(low, acc[0] + pltpu.roll(acc[1], HEAD_DIM, axis=1),
                             acc[3] + pltpu.roll(acc[2], HEAD_DIM, axis=1))
        add = lambda acc, var, t: acc.__setitem__(var, t if acc[var] is None else acc[var] + t)
        k_own, k_prev, v_own, v_prev = [], [], [], []
        for j in range(ATTN_BLOCKS):
            ds_parts, p_parts = parts[j]
            kt_own, kt_prev = tiles(kc_ref, kp_ref, j)
            dk_own, dk_prev, dv_own, dv_prev = ([None] * n_var for _ in range(4))
            for i in range(N_HEADS // 2):
                dq_pair = None
                for h in (2 * i, 2 * i + 1):
                    var = _variant(h)
                    (ds_o, ds_p), (p_o, p_p) = ds_parts[h], p_parts[h]
                    part = _dot_tn(ds_o, kt_own[var]) + _dot_tn(ds_p, kt_prev[var])
                    dq_pair = part if dq_pair is None else dq_pair + part
                    add(dk_own, var, _dot(ds_o, q_pairs[j][i]))
                    add(dk_prev, var, _dot(ds_p, q_pairs[j][i]))
                    add(dv_own, var, _dot(p_o, do_pairs[j][i]))
                    add(dv_prev, var, _dot(p_p, do_pairs[j][i]))
                dq_ref[rows(j), i * _LANES:(i + 1) * _LANES] = dq_pair * ATTN_SCALE
            k_own.append(merge(dk_own))
            k_prev.append(merge(dk_prev))
            v_own.append(merge(dv_own))
            v_prev.append(merge(dv_prev))
        for j in range(ATTN_BLOCKS):
            last = j == ATTN_BLOCKS - 1
            dk_ref[rows(j), :] = k_own[j] if last else k_own[j] + k_prev[j + 1]
            dv_ref[rows(j), :] = v_own[j] if last else v_own[j] + v_prev[j + 1]
        dkx_ref[...] = k_prev[0]
        dvx_ref[...] = v_prev[0]

    steps = S // tq
    prev = pl.BlockSpec((n_var, BLOCK, KV_WIDTH), lambda g: (0, jnp.maximum(g * ATTN_BLOCKS - 1, 0), 0))
    cur = pl.BlockSpec((n_var, tq, KV_WIDTH), lambda g: (0, g, 0))
    nxt = pl.BlockSpec((None, BLOCK, KV_WIDTH), lambda g: (g, 0, 0))
    args = [dout, m, *w_out, mine, post, o, q, kz, kz, vz, vz, sinks, g_attn]
    return pl.pallas_call(
        _ignoring(body, len(args), len(deps)), name="attn_bwd", grid=(steps,),
        in_specs=[_rows(tq, D_MODEL), _rows(tq, D_MODEL), _ANY, _ANY, _SMEM, _full((1, D_MODEL)),
                  _rows(tq, ATTN_WIDTH), _rows(tq, ATTN_WIDTH), prev, cur, prev, cur,
                  pl.BlockSpec(memory_space=pltpu.SMEM), _full((1, ATTN_WIDTH))] + [_ANY] * len(deps),
        out_specs=[_rows(tq, POOL_WIDTH), _rows(tq, D_MODEL), _full((1, D_MODEL)),
                   _rows(tq, ATTN_WIDTH), _rows(tq, KV_WIDTH), _rows(tq, KV_WIDTH), nxt, nxt,
                   _full((N_HEADS, _LANES)), _full((1, ATTN_WIDTH))],
        out_shape=[jax.ShapeDtypeStruct((S, POOL_WIDTH), F32), jax.ShapeDtypeStruct((S, D_MODEL), BF16),
                   jax.ShapeDtypeStruct((1, D_MODEL), F32),
                   jax.ShapeDtypeStruct((S, ATTN_WIDTH), F32), jax.ShapeDtypeStruct((S, KV_WIDTH), F32),
                   jax.ShapeDtypeStruct((S, KV_WIDTH), F32),
                   jax.ShapeDtypeStruct((steps, BLOCK, KV_WIDTH), F32), jax.ShapeDtypeStruct((steps, BLOCK, KV_WIDTH), F32),
                   jax.ShapeDtypeStruct((N_HEADS, _LANES), F32), jax.ShapeDtypeStruct((1, ATTN_WIDTH), F32)],
        scratch_shapes=[pltpu.VMEM((D_MODEL, D_MODEL), BF16), pltpu.SemaphoreType.DMA((N_SHARD,))],
        compiler_params=_params(dimension_semantics=("arbitrary",)),
    )(*args, *deps)


def _inv_freq_row():
    inv_freq = ROPE_THETA ** (-jnp.arange(0, ROT_DIM, 2, dtype=F32) / ROT_DIM)
    per_head = jnp.concatenate([inv_freq, inv_freq, jnp.zeros((HEAD_DIM - ROT_DIM,), F32)])
    return jnp.tile(per_head, _LANES // HEAD_DIM).reshape(1, _LANES)


def _local_step(x, pos, target, small, mine, weights_of, grads_ready):
    rope = _rope_tables(pos, _inv_freq_row())
    wgu1, wd1 = weights_of("ffn1", (rope,))
    x1, g1, u1, f1 = _ffn_fwd(x, small["ffn1_pre"], small["ffn1_post"], wgu1, wd1, mine)
    w_in_t, w_out = weights_of("mixer", (x1,))
    u, q, kz, vz = _mixer_in_fwd(x1, small["mix_pre"], w_in_t, mine, rope)
    y_pool = _pool_fwd(u, small["w_pool"], small["pool_scale"], small["g_pool"])
    o, x2, m, y = _attn_fwd(q, kz, vz, small["sinks"], small["g_attn"], y_pool, x1, w_out, mine, small["mix_post"])
    wgu2, wd2 = weights_of("ffn2", (x2,))
    dx3, g2, u2, df2, loss_acc, dpost2 = _ffn_fwd(
        x2, small["ffn2_pre"], small["ffn2_post"], wgu2, wd2, mine, target=target)
    grads = {"loss": loss_acc * (0.5 / D_MODEL), "ffn2_post": dpost2}
    dx2, h3, dgu2, a2, grads["ffn2_pre"] = _ffn_bwd(
        dx3, x2, df2, g2, u2, small["ffn2_pre"], small["ffn2_post"], wgu2, wd2, mine, df_known=True)
    dwgu2 = _wgrad(h3, dgu2, D_MODEL, FF_CHUNK, "wgrad_gu2", column_shards=True)
    dwd2 = _wgrad(a2, df2, FF_CHUNK, D_MODEL, "wgrad_down2")
    deps = grads_ready("ffn2", {"ffn2_w_gu": dwgu2, "ffn2_w_down": dwd2})
    dy_pool, dm, grads["mix_post"], dq, dk, dv, dk_next, dv_next, dsinks, grads["g_attn"] = _attn_bwd(
        dx2, m, w_out, mine, small["mix_post"], o, q, kz, vz, small["sinks"], small["g_attn"], deps=deps)
    dw_out = _wgrad(y, dm, D_MODEL, D_MODEL, "wgrad_out")
    grads["sinks"] = dsinks[:, 0].reshape(1, N_HEADS)
    du, grads["w_pool"], grads["pool_scale"], grads["g_pool"] = _pool_bwd(
        dy_pool, u, small["w_pool"], small["pool_scale"], small["g_pool"])
    dx1, dz, h2, grads["mix_pre"] = _mixer_in_bwd(dx2, x1, small["mix_pre"], w_in_t, mine, du, dq, dk, dv, dk_next, dv_next, rope)
    dw_in_t = _wgrad(dz, h2, IN_WIDTH, D_MODEL, "wgrad_in")
    deps = grads_ready("mixer", {"w_in": dw_in_t, "w_out": dw_out})
    dx, h1, dgu1, a1, df1, grads["ffn1_pre"], grads["ffn1_post"] = _ffn_bwd(
        dx1, x, f1, g1, u1, small["ffn1_pre"], small["ffn1_post"], wgu1, wd1, mine, deps=deps)
    dwgu1 = _wgrad(h1, dgu1, D_MODEL, FF_CHUNK, "wgrad_gu1", column_shards=True)
    deps = grads_ready("ffn1_gu", {"ffn1_w_gu": dwgu1}, small=grads)
    dwd1 = _wgrad(a1, df1, FF_CHUNK, D_MODEL, "wgrad_down1", deps=deps)
    grads_ready("ffn1_down", {"ffn1_w_down": dwd1})
    return dx


def _place():
    return lax.axis_index("x"), lax.axis_index("y"), lax.axis_index("c")


def _other_chips(x, y):
    return [(1 - x, y), (x, 1 - y), (1 - x, 1 - y)]


_HBM = pl.BlockSpec(memory_space=pltpu.HBM)
_SEM = pl.BlockSpec(memory_space=pltpu.SEMAPHORE)
_EFFECT = pltpu.SideEffectType.DATAFLOW_SIDE_EFFECTING
GATHER, GATHER_HALF, REDUCE, BROADCAST = "gather", "gather_half", "reduce", "broadcast"
N_DEVICES = 8


def _in_hbm(a):
    return pltpu.with_memory_space_constraint(a, pltpu.HBM)


def _core_half(rows, c):
    return pl.ds(pl.multiple_of(c * (rows // 2), 16), rows // 2)


def _chip_copies(kind, srcs, lands, send_sems, recv_sems):
    x, y, c = _place()
    mine = 2 * x + y
    copies = []
    for w in range(len(srcs)):
        if kind == BROADCAST:
            peers = [(x ^ (k >> 2), y ^ ((k >> 1) & 1), c ^ (k & 1)) for k in range(1, N_DEVICES)]
        else:
            peers = [(px, py, c) for px, py in _other_chips(x, y)]
        for k, (px, py, pc) in enumerate(peers):
            if kind == GATHER:
                src, dst = srcs[w], lands[w].at[mine]
            elif kind == GATHER_HALF:
                half = _core_half(srcs[w].shape[0], c)
                src, dst = srcs[w].at[half, :], lands[w].at[mine, half, :]
            elif kind == BROADCAST:
                src, dst = srcs[w], lands[w].at[2 * mine + c]
            else:
                src, dst = srcs[w].at[2 * px + py], lands[w].at[k]
            pair = len(peers) * w + k
            copies.append(pltpu.make_async_remote_copy(
                src_ref=src, dst_ref=dst, send_sem=send_sems.at[pair], recv_sem=recv_sems.at[pair],
                device_id=(px, py, pc), device_id_type=MESH))
    return copies


def _landing_shape(kind, src):
    if kind == REDUCE:
        return (N_SHARD - 1,) + src.shape[1:]
    return ((N_DEVICES if kind == BROADCAST else N_SHARD),) + src.shape


def _peer_count(kind):
    return N_DEVICES - 1 if kind == BROADCAST else N_SHARD - 1


def _exchange_start(kinds, groups, name):
    sizes = [len(g) for g in groups]
    flat = [s for g in groups for s in g]
    n, ng = len(flat), len(groups)

    def body(*refs):
        srcs, lands = refs[:n], refs[n:2 * n]
        sems = refs[2 * n:2 * n + 2 * ng]
        token = refs[-1]
        start = 0
        for gi, size in enumerate(sizes):
            for cp in _chip_copies(kinds[gi], srcs[start:start + size], lands[start:start + size],
                                   sems[2 * gi], sems[2 * gi + 1]):
                cp.start()
            start += size
        token[...] = jnp.zeros_like(token)

    landings = [lax.empty(_landing_shape(kind, s), s.dtype) for kind, g in zip(kinds, groups) for s in g]
    sem_shapes = [pltpu.SemaphoreType.DMA((size * _peer_count(kind),)) for kind, size in zip(kinds, sizes)
                  for _ in range(2)]
    outs = pl.pallas_call(
        body, name=name,
        in_specs=[_HBM] * (2 * n),
        out_specs=[_SEM] * (2 * ng) + [_HBM] * (2 * n) + [pl.BlockSpec(memory_space=pltpu.VMEM)],
        out_shape=sem_shapes + [pltpu.HBM(a.shape, a.dtype) for a in flat + landings]
        + [jax.ShapeDtypeStruct((8, _LANES), F32)],
        input_output_aliases={i: 2 * ng + i for i in range(2 * n)},
        compiler_params=pltpu.CompilerParams(has_side_effects=_EFFECT),
    )(*[_in_hbm(a) for a in flat + landings])
    sems, srcs, lands, token = outs[:2 * ng], outs[2 * ng:2 * ng + n], outs[2 * ng + n:2 * ng + 2 * n], outs[-1]
    handles, start = [], 0
    for gi, size in enumerate(sizes):
        handles.append((sems[2 * gi], sems[2 * gi + 1], srcs[start:start + size], lands[start:start + size]))
        start += size
    return handles, token


def _exchange_wait(kind, handle, after, name):
    send_sems, recv_sems, srcs, lands = handle
    n = len(srcs)

    def body(*refs):
        copies = _chip_copies(kind, refs[:n], refs[n:2 * n], refs[2 * n], refs[2 * n + 1])
        for cp in copies:
            cp.wait_send()
        for cp in copies:
            cp.wait_recv()

    outs = pl.pallas_call(
        body, name=name,
        in_specs=[_HBM] * (2 * n) + [_SEM, _SEM] + [_ANY] * len(after),
        out_specs=[_HBM] * (2 * n),
        out_shape=[pltpu.HBM(a.shape, a.dtype) for a in list(srcs) + list(lands)],
        input_output_aliases={i: i for i in range(2 * n)},
        compiler_params=pltpu.CompilerParams(has_side_effects=_EFFECT),
    )(*srcs, *lands, send_sems, recv_sems, *after)
    return outs[:n], outs[n:]


def _swap_gathered_halves(lands, name):
    n = len(lands)

    def body(*refs):
        bufs = refs[n:2 * n]
        send_sems, recv_sems = refs[2 * n:]
        x, y, c = _place()
        mine = 2 * x + y
        sends, arrivals = [], []
        for w in range(n):
            rows = bufs[w].shape[1]
            for d in range(1, N_SHARD):
                slot = (mine + d) % N_SHARD
                sems = dict(send_sem=send_sems.at[(N_SHARD - 1) * w + d - 1],
                            recv_sem=recv_sems.at[(N_SHARD - 1) * w + d - 1],
                            device_id=(x, y, 1 - c), device_id_type=MESH)
                fetched = bufs[w].at[slot, _core_half(rows, c), :]
                missing = bufs[w].at[slot, _core_half(rows, 1 - c), :]
                sends.append(pltpu.make_async_remote_copy(src_ref=fetched, dst_ref=fetched, **sems))
                arrivals.append(pltpu.make_async_remote_copy(src_ref=missing, dst_ref=missing, **sems))
        for cp in sends:
            cp.start()
        for cp in arrivals:
            cp.wait_recv()
        for cp in sends:
            cp.wait_send()

    return pl.pallas_call(
        body, name=name, in_specs=[_ANY] * n, out_specs=[_ANY] * n,
        out_shape=[jax.ShapeDtypeStruct(a.shape, a.dtype) for a in lands],
        input_output_aliases={i: i for i in range(n)},
        scratch_shapes=[pltpu.SemaphoreType.DMA((n * (N_SHARD - 1),)), pltpu.SemaphoreType.DMA((n * (N_SHARD - 1),))],
        compiler_params=pltpu.CompilerParams(has_side_effects=True),
    )(*lands)


def _swap_with_sibling(partials, name):
    n = len(partials)

    def body(*refs):
        ins, outs = refs[:n], refs[n:2 * n]
        send_sems, recv_sems = refs[2 * n:]
        x, y, c = _place()
        sends = [pltpu.make_async_remote_copy(
            src_ref=ins[w], dst_ref=outs[w], send_sem=send_sems.at[w], recv_sem=recv_sems.at[w],
            device_id=(x, y, 1 - c), device_id_type=MESH) for w in range(n)]
        for cp in sends:
            cp.start()
        for cp in sends:
            cp.wait_recv()
        for cp in sends:
            cp.wait_send()

    return pl.pallas_call(
        body, name=name,
        in_specs=[_ANY] * n, out_specs=[_ANY] * n,
        out_shape=[jax.ShapeDtypeStruct(p.shape, p.dtype) for p in partials],
        scratch_shapes=[pltpu.SemaphoreType.DMA((n,)), pltpu.SemaphoreType.DMA((n,))],
        compiler_params=pltpu.CompilerParams(has_side_effects=True),
    )(*partials)


def _row_block(rows, cap):
    best = None
    for cand in range(16, min(rows, cap) + 1, 16):
        if rows % cand == 0:
            best = cand
    assert best is not None, rows
    return best


def _chip_partial(own, received, shard, name):
    _, R, C = own.shape
    rb = _row_block(R, 512)

    def body(shard_ref, own_ref, rec_ref, out_ref):
        acc = own_ref[...]
        for k in range(3):
            acc = acc + rec_ref[k].astype(F32)
        out_ref[...] = acc.astype(BF16)

    return pl.pallas_call(
        body, name=name,
        grid_spec=pltpu.PrefetchScalarGridSpec(
            num_scalar_prefetch=1, grid=(R // rb,),
            in_specs=[pl.BlockSpec((None, rb, C), lambda i, s: (s[0], i, 0)),
                      pl.BlockSpec((3, rb, C), lambda i, s: (0, i, 0))],
            out_specs=pl.BlockSpec((rb, C), lambda i, s: (i, 0))),
        out_shape=jax.ShapeDtypeStruct((R, C), BF16),
        compiler_params=_params(dimension_semantics=("arbitrary",)),
    )(shard, own, received)


def _adamw(w, m, v, g_parts, name, slot=None):
    R, C = w.shape
    by_device = slot is not None
    rb = _row_block(R, 512) if R % 16 == 0 else R

    def body(w_ref, m_ref, v_ref, *refs):
        g_refs, (grad_ref, delta_ref, m_out, v_out) = refs[:-4], refs[-4:]
        if by_device:
            own_ref, land_ref, slot_ref = g_refs
            part = lambda d: jnp.where(slot_ref[0] == d, own_ref[...], land_ref[d])
            g = part(0)
            for d in range(1, N_DEVICES):
                g = g + part(d)
        else:
            g = g_refs[0][...].astype(F32)
            for g_ref in g_refs[1:]:
                g = g + g_ref[...].astype(F32)
        grad_ref[...] = g
        new_m = ADAM_B1 * m_ref[...] + (1.0 - ADAM_B1) * g
        new_v = ADAM_B2 * v_ref[...] + (1.0 - ADAM_B2) * (g * g)
        m_hat = new_m / (1.0 - ADAM_B1 ** ADAM_STEP)
        v_hat = new_v / (1.0 - ADAM_B2 ** ADAM_STEP)
        delta_ref[...] = -ADAM_LR * (m_hat / (jnp.sqrt(v_hat) + ADAM_EPS) + ADAM_WD * w_ref[...])
        m_out[...] = new_m
        v_out[...] = new_v

    spec = pl.BlockSpec((rb, C), lambda i: (i, 0))
    if by_device:
        g_specs = [spec, pl.BlockSpec((N_DEVICES, rb, C), lambda i: (0, i, 0)), _SMEM]
        g_parts = list(g_parts) + [slot]
    else:
        g_specs = [spec] * len(g_parts)
    return pl.pallas_call(
        body, name=name, grid=(R // rb,),
        in_specs=[spec, spec, spec] + g_specs,
        out_specs=[spec] * 4,
        out_shape=[jax.ShapeDtypeStruct((R, C), F32)] * 4,
        compiler_params=_params(dimension_semantics=("arbitrary",)),
    )(w, m, v, *g_parts)


SMALL_NAMES = ("ffn1_pre", "ffn1_post", "mix_pre", "pool_scale", "sinks", "g_pool", "g_attn", "mix_post",
               "ffn2_pre", "ffn2_post", "w_pool")
_SLAB_PART = 8 * _LANES


SLAB_NAMES = SMALL_NAMES + ("loss",)


def _to_slab(parts):
    rows = []
    for name in SLAB_NAMES:
        flat = parts[name].reshape(-1) if name in parts else jnp.zeros((_SLAB_PART,), F32)
        padded = -(-flat.shape[0] // _SLAB_PART) * _SLAB_PART
        rows.append(jnp.pad(flat, (0, padded - flat.shape[0])).reshape(-1, _LANES))
    return jnp.concatenate(rows, axis=0)


def _from_slab(slab, like):
    out, row = {}, 0
    for name in SLAB_NAMES:
        size = like[name].size
        rows = -(-size // _SLAB_PART) * (_SLAB_PART // _LANES)
        out[name] = slab[row:row + rows].reshape(-1)[:size].reshape(like[name].shape)
        row += rows
    return out


BIG_NAMES = ("ffn1_w_gu", "ffn1_w_down", "w_in", "w_out", "ffn2_w_gu", "ffn2_w_down")
WEIGHT_ORDER = ("ffn1_pre", "ffn1_w_gu", "ffn1_w_down", "ffn1_post", "mix_pre", "w_in", "w_pool", "pool_scale",
                "sinks", "g_pool", "g_attn", "w_out", "mix_post", "ffn2_pre", "ffn2_w_gu", "ffn2_w_down", "ffn2_post")


def kernel(x, positions, ffn1_pre, ffn1_w_gu, ffn1_w_down, ffn1_post, mix_pre, w_in, w_pool, pool_scale, sinks, g_pool, g_attn, w_out, mix_post, ffn2_pre, ffn2_w_gu, ffn2_w_down, ffn2_post, loss_target, m_ffn1_pre, m_ffn1_w_gu, m_ffn1_w_down, m_ffn1_post, m_mix_pre, m_w_in, m_w_pool, m_pool_scale, m_sinks, m_g_pool, m_g_attn, m_w_out, m_mix_post, m_ffn2_pre, m_ffn2_w_gu, m_ffn2_w_down, m_ffn2_post, v_ffn1_pre, v_ffn1_w_gu, v_ffn1_w_down, v_ffn1_post, v_mix_pre, v_w_in, v_w_pool, v_pool_scale, v_sinks, v_g_pool, v_g_attn, v_w_out, v_mix_post, v_ffn2_pre, v_ffn2_w_gu, v_ffn2_w_down, v_ffn2_post):
    given = dict(locals())
    weights = {n: given[n][0] for n in WEIGHT_ORDER}
    moments_m = {n: given["m_" + n][0] for n in WEIGHT_ORDER}
    moments_v = {n: given["v_" + n][0] for n in WEIGHT_ORDER}
    S = x.shape[1]
    shard = (2 * lax.axis_index("x") + lax.axis_index("y")).astype(jnp.int32).reshape(1)

    local16 = {n: weights[n].astype(BF16) for n in BIG_NAMES if n != "w_in"}
    local16["w_in"] = weights["w_in"].T.astype(BF16)
    gather_groups = {"ffn1": ("ffn1_w_gu", "ffn1_w_down"), "mixer": ("w_in", "w_out"),
                     "ffn2": ("ffn2_w_gu", "ffn2_w_down")}
    gather_kinds = {"ffn1": GATHER_HALF, "mixer": GATHER, "ffn2": GATHER}
    handles, _ = _exchange_start(list(gather_kinds.values()),
                                 [[local16[n] for n in names] for names in gather_groups.values()], "gather_start")
    gather_handles = dict(zip(gather_groups, handles))

    def weights_of(group, after):
        kind = gather_kinds[group]
        owns, lands = _exchange_wait(kind, gather_handles[group], list(after), "gather_wait_" + group)
        if kind == GATHER_HALF:
            lands = _swap_gathered_halves(lands, "swap_gathered_" + group)
        return list(zip(lands, owns))

    pending, last_token = {}, []

    def grads_ready(group, grads, small=None):
        names = list(grads)
        kinds, sources = [REDUCE], [[grads[n][1] for n in names]]
        if small is not None:
            kinds, sources = kinds + [BROADCAST], sources + [[_to_slab(small)]]
        handles, token = _exchange_start(kinds, sources, "reduce_start_" + group)
        handle = handles[0]
        if small is not None:
            pending["small"] = handles[1]
        pending[group] = (names, handle, [grads[n][0] for n in names])
        last_token[:] = [token]
        return [token]

    small = {n: (weights[n] if weights[n].ndim > 1 else weights[n].reshape(1, -1)) for n in SMALL_NAMES}
    dx = _local_step(x[0], positions.reshape(S, 1), loss_target[0], small, shard, weights_of, grads_ready)

    grad, delta, new_m, new_v = {}, {}, {}, {}

    def finish(groups, after):
        names, partials = [], []
        for group in groups:
            group_names, handle, own32 = pending[group]
            _, received = _exchange_wait(REDUCE, handle, after, "reduce_wait_" + group)
            names += group_names
            partials += [_chip_partial(g32, rec, shard, "chip_partial_" + n)
                         for n, g32, rec in zip(group_names, own32, received)]
        siblings = _swap_with_sibling(partials, "swap_" + groups[0])
        for name, mine, theirs in zip(names, partials, siblings):
            if name == "w_in":
                mine, theirs = mine.T, theirs.T
            grad[name], delta[name], new_m[name], new_v[name] = _adamw(
                weights[name], moments_m[name], moments_v[name], [mine, theirs], "adamw_" + name)
        return [grad[names[-1]]]

    after = finish(["ffn2"], last_token)
    after = finish(["mixer"], after)
    (own_slab,), (slab_landing,) = _exchange_wait(BROADCAST, pending["small"], after, "reduce_wait_small")
    device = (2 * shard + lax.axis_index("c")).astype(jnp.int32)
    small_like = dict({n: small[n] for n in SMALL_NAMES}, loss=jnp.zeros((8, _LANES), F32))
    slabs = _adamw(_to_slab(small), _to_slab({n: moments_m[n] for n in SMALL_NAMES}),
                   _to_slab({n: moments_v[n] for n in SMALL_NAMES}), [own_slab, slab_landing], "adamw_small",
                   slot=device)
    for store, slab in zip((grad, delta, new_m, new_v), slabs):
        store.update(_from_slab(slab, small_like))
    loss = grad["loss"][0, 0]
    after = finish(["ffn1_gu"], [slabs[0]])
    finish(["ffn1_down"], after)

    def out(store):
        return [store[n].reshape(given[n].shape) for n in WEIGHT_ORDER]
    return (loss, dx[None], *out(grad), *out(delta), *out(new_m), *out(new_v))
```

```python
import jax
import jax.numpy as jnp
from jax import lax
from jax.experimental import pallas as pl
from jax.experimental.pallas import tpu as pltpu

F32 = jnp.float32
BF16 = jnp.bfloat16

D_MODEL = 1024
D_FF = 2816
N_SHARD = 4
FF_CHUNK = D_FF // 2
POOL_WINDOWS = (2, 4, 8, 16)
POOL_WIDTH = 512
POOL_GROUP = 128
HALO = 16
HEAD_DIM = 64
N_HEADS = 8
N_KV_HEADS = 2
ATTN_WIDTH = 512
KV_WIDTH = 128
IN_WIDTH = 1280
BLOCK = 128
ATTN_BLOCKS = 4
ROT_DIM = 16
ROPE_THETA = 500000.0
EPS = 1e-6
NEG_INF = -1e30
ATTN_SCALE = HEAD_DIM ** -0.5

ADAM_LR = 0.001
ADAM_B1 = 0.9
ADAM_B2 = 0.999
ADAM_EPS = 1e-08
ADAM_WD = 0.01
ADAM_STEP = 10

VMEM_LIMIT = 60 * 1024 * 1024
FFN_FWD_TILE = 512
FFN_BWD_TILE = 256
MIXER_TILE = 512

MESH = pl.DeviceIdType.MESH


def _params(**kw):
    return pltpu.CompilerParams(vmem_limit_bytes=VMEM_LIMIT, **kw)


def _dot(a, b):
    return jnp.dot(a, b, preferred_element_type=F32)


def _dot_nt(a, b):
    return lax.dot_general(a, b, (((1,), (1,)), ((), ())), preferred_element_type=F32)


def _dot_tn(a, b):
    return lax.dot_general(a, b, (((0,), (0,)), ((), ())), preferred_element_type=F32)


def _rstd(x):
    return lax.rsqrt(jnp.mean(x * x, axis=-1, keepdims=True) + EPS)


def _norm_bwd(dy, xn, r, gain):
    dxn = dy * gain
    return r * (dxn - xn * jnp.mean(dxn * xn, axis=-1, keepdims=True))


def _sigmoid(x):
    return 1.0 / (1.0 + jnp.exp(-x))


def _full(shape):
    return pl.BlockSpec(shape, lambda *_: (0,) * len(shape))


def _rows(tile, width):
    return pl.BlockSpec((tile, width), lambda i: (i, 0))


_ANY = pl.BlockSpec(memory_space=pl.ANY)


_SMEM = pl.BlockSpec(memory_space=pltpu.SMEM)


def _load_once(pairs, sem):
    @pl.when(pl.program_id(0) == 0)
    def _():
        copies = [pltpu.make_async_copy(src, dst, sem.at[n]) for n, (src, dst) in enumerate(pairs)]
        for cp in copies:
            cp.start()
        for cp in copies:
            cp.wait()


def _gathered(land_ref, own_ref, vmem_ref, mine, rows=None):
    def dst(slot):
        if rows is None:
            return vmem_ref.at[slot]
        return vmem_ref.at[pl.ds(pl.multiple_of(slot * rows, 16), rows), :]
    pairs = [(land_ref.at[(mine + d) % N_SHARD], dst((mine + d) % N_SHARD)) for d in range(1, N_SHARD)]
    return pairs + [(own_ref, dst(mine))]


def _ignoring(body, start, count):
    def wrapped(*refs):
        return body(*refs[:start], *refs[start + count:])
    return wrapped


def _ffn_up(x, pre, wgu, mine, deps=()):
    S = x.shape[0]
    tm = FFN_FWD_TILE

    def body(x_ref, pre_ref, wgu_land, wgu_own, mine_ref, g_ref, u_ref, a_ref, wgu_v, sem):
        _load_once(_gathered(wgu_land, wgu_own, wgu_v, mine_ref[0]), sem)
        xv = x_ref[...]
        h = ((xv * _rstd(xv)) * pre_ref[...]).astype(BF16)
        for c in range(2):
            cols = slice(c * FF_CHUNK, (c + 1) * FF_CHUNK)
            g = _dot(h, wgu_v[c])
            u = _dot(h, wgu_v[2 + c])
            g_ref[:, cols] = g.astype(BF16)
            u_ref[:, cols] = u.astype(BF16)
            a_ref[:, cols] = ((g * _sigmoid(g)) * u).astype(BF16)

    args = [x, pre, *wgu, mine]
    return pl.pallas_call(
        _ignoring(body, len(args), len(deps)), name="ffn_up", grid=(S // tm,),
        in_specs=[_rows(tm, D_MODEL), _full((1, D_MODEL)), _ANY, _ANY, _SMEM] + [_ANY] * len(deps),
        out_specs=[_rows(tm, D_FF)] * 3,
        out_shape=[jax.ShapeDtypeStruct((S, D_FF), BF16)] * 3,
        scratch_shapes=[pltpu.VMEM((N_SHARD, D_MODEL, FF_CHUNK), BF16), pltpu.SemaphoreType.DMA((N_SHARD,))],
        compiler_params=_params(dimension_semantics=("arbitrary",)),
    )(*args, *deps)


def _ffn_down(x, a, post, wd, mine, target=None, deps=()):
    S = x.shape[0]
    tm = FFN_FWD_TILE
    with_loss = target is not None

    def body(*refs):
        if with_loss:
            (x_ref, a_ref, post_ref, wd_land, wd_own, mine_ref, tgt_ref,
             out_ref, f_ref, loss_ref, dpost_ref, wd_v, sem) = refs
        else:
            x_ref, a_ref, post_ref, wd_land, wd_own, mine_ref, out_ref, f_ref, wd_v, sem = refs
        _load_once(_gathered(wd_land, wd_own, wd_v, mine_ref[0], rows=D_FF // N_SHARD), sem)
        xv = x_ref[...]
        facc = _dot(a_ref[...], wd_v[...])
        rf = _rstd(facc)
        fn = facc * rf
        out = xv + 0.5 * (fn * post_ref[...])
        if with_loss:
            diff = out - tgt_ref[...]
            dout = diff * (1.0 / D_MODEL)
            out_ref[...] = dout

            @pl.when(pl.program_id(0) == 0)
            def _():
                loss_ref[...] = jnp.zeros_like(loss_ref)
                dpost_ref[...] = jnp.zeros_like(dpost_ref)
            loss_ref[...] += jnp.sum(diff * diff)
            dn = 0.5 * dout
            dpost_ref[...] += jnp.sum(dn * fn, axis=0, keepdims=True)
            f_ref[...] = _norm_bwd(dn, fn, rf, post_ref[...]).astype(BF16)
        else:
            f_ref[...] = facc
            out_ref[...] = out

    in_specs = [_rows(tm, D_MODEL), _rows(tm, D_FF), _full((1, D_MODEL)), _ANY, _ANY, _SMEM]
    args = [x, a, post, *wd, mine]
    out_shape = [jax.ShapeDtypeStruct((S, D_MODEL), F32), jax.ShapeDtypeStruct((S, D_MODEL), F32)]
    out_specs = [_rows(tm, D_MODEL), _rows(tm, D_MODEL)]
    if with_loss:
        in_specs.append(_rows(tm, D_MODEL))
        args.append(target)
        out_shape[1] = jax.ShapeDtypeStruct((S, D_MODEL), BF16)
        out_shape += [jax.ShapeDtypeStruct((8, 128), F32), jax.ShapeDtypeStruct((1, D_MODEL), F32)]
        out_specs += [_full((8, 128)), _full((1, D_MODEL))]
    return pl.pallas_call(
        _ignoring(body, len(args), len(deps)), name="ffn_down_loss" if with_loss else "ffn_down",
        grid=(S // tm,), in_specs=in_specs + [_ANY] * len(deps), out_specs=out_specs, out_shape=out_shape,
        scratch_shapes=[pltpu.VMEM((D_FF, D_MODEL), BF16), pltpu.SemaphoreType.DMA((N_SHARD,))],
        compiler_params=_params(dimension_semantics=("arbitrary",)),
    )(*args, *deps)


def _ffn_bwd(dout, x, f, g, u, pre, post, wgu, wd, mine, df_known=False, deps=()):
    S = x.shape[0]
    tm = FFN_BWD_TILE

    def body(*refs):
        if df_known:
            (dout_ref, x_ref, f_ref, g_ref, u_ref, pre_ref, post_ref, wgu_land, wgu_own, wd_land, wd_own, mine_ref,
             dx_ref, h_ref, dgu_ref, dpre_ref, wgu_v, wd_v, sem) = refs
        else:
            (dout_ref, x_ref, f_ref, g_ref, u_ref, pre_ref, post_ref, wgu_land, wgu_own, wd_land, wd_own, mine_ref,
             dx_ref, h_ref, dgu_ref, df_ref, dpre_ref, dpost_ref, wgu_v, wd_v, sem) = refs
        _load_once(_gathered(wgu_land, wgu_own, wgu_v, mine_ref[0])
                   + _gathered(wd_land, wd_own, wd_v, mine_ref[0], rows=D_FF // N_SHARD), sem)

        @pl.when(pl.program_id(0) == 0)
        def _():
            dpre_ref[...] = jnp.zeros_like(dpre_ref)
            if not df_known:
                dpost_ref[...] = jnp.zeros_like(dpost_ref)

        dout_v = dout_ref[...]
        if df_known:
            df = f_ref[...]
        else:
            dn = 0.5 * dout_v
            fv = f_ref[...]
            rf = _rstd(fv)
            fn = fv * rf
            dpost_ref[...] += jnp.sum(dn * fn, axis=0, keepdims=True)
            df = _norm_bwd(dn, fn, rf, post_ref[...]).astype(BF16)
            df_ref[...] = df
        dh = jnp.zeros((tm, D_MODEL), F32)
        for c in range(2):
            cols = slice(c * FF_CHUNK, (c + 1) * FF_CHUNK)
            da = _dot_nt(df, wd_v[cols, :])
            gv = g_ref[:, cols].astype(F32)
            uv = u_ref[:, cols].astype(F32)
            sg = _sigmoid(gv)
            silu = gv * sg
            dg = ((da * uv) * (sg * (1.0 + gv * (1.0 - sg)))).astype(BF16)
            du = (da * silu).astype(BF16)
            dgu_ref[:, cols] = dg
            dgu_ref[:, 2 * FF_CHUNK + c * FF_CHUNK:2 * FF_CHUNK + (c + 1) * FF_CHUNK] = du
            dh = dh + _dot_nt(dg, wgu_v[c]) + _dot_nt(du, wgu_v[2 + c])
        xv = x_ref[...]
        rx = _rstd(xv)
        xn = xv * rx
        h_ref[...] = (xn * pre_ref[...]).astype(BF16)
        dpre_ref[...] += jnp.sum(dh * xn, axis=0, keepdims=True)
        dx_ref[...] = dout_v + _norm_bwd(dh, xn, rx, pre_ref[...])

    args = [dout, x, f, g, u, pre, post, *wgu, *wd, mine]
    out_specs = [_rows(tm, D_MODEL), _rows(tm, D_MODEL), _rows(tm, 2 * D_FF), _rows(tm, D_MODEL),
                 _full((1, D_MODEL)), _full((1, D_MODEL))]
    out_shape = [jax.ShapeDtypeStruct((S, D_MODEL), F32), jax.ShapeDtypeStruct((S, D_MODEL), BF16),
                 jax.ShapeDtypeStruct((S, 2 * D_FF), BF16), jax.ShapeDtypeStruct((S, D_MODEL), BF16),
                 jax.ShapeDtypeStruct((1, D_MODEL), F32), jax.ShapeDtypeStruct((1, D_MODEL), F32)]
    if df_known:
        out_specs = out_specs[:3] + out_specs[4:5]
        out_shape = out_shape[:3] + out_shape[4:5]
    return pl.pallas_call(
        _ignoring(body, len(args), len(deps)), name="ffn_bwd_from_df" if df_known else "ffn_bwd", grid=(S // tm,),
        in_specs=[_rows(tm, D_MODEL), _rows(tm, D_MODEL), _rows(tm, D_MODEL), _rows(tm, D_FF), _rows(tm, D_FF),
                  _full((1, D_MODEL)), _full((1, D_MODEL)), _ANY, _ANY, _ANY, _ANY, _SMEM] + [_ANY] * len(deps),
        out_specs=out_specs, out_shape=out_shape,
        scratch_shapes=[pltpu.VMEM((N_SHARD, D_MODEL, FF_CHUNK), BF16), pltpu.VMEM((D_FF, D_MODEL), BF16),
                        pltpu.SemaphoreType.DMA((2 * N_SHARD,))],
        compiler_params=_params(dimension_semantics=("arbitrary",)),
    )(*args, *deps)


def _wgrad(lhs, rhs, m_block, n_block, name, column_shards=False, tk=2048, deps=()):
    S, M = lhs.shape
    N = rhs.shape[1]
    k_steps = S // tk

    def body(lhs_ref, rhs_ref, out_ref, out16_ref):
        k = pl.program_id(2)

        @pl.when(k == 0)
        def _():
            out_ref[...] = jnp.zeros_like(out_ref)
        out_ref[...] += _dot_tn(lhs_ref[...], rhs_ref[...])

        @pl.when(k == k_steps - 1)
        def _():
            out16_ref[...] = out_ref[...].astype(BF16)

    if column_shards:
        assert N == N_SHARD * n_block
        shape = (N_SHARD, M, n_block)
        out_spec = pl.BlockSpec((None, m_block, n_block), lambda i, j, k: (j, i, 0))
    else:
        shape = (M, N)
        out_spec = pl.BlockSpec((m_block, n_block), lambda i, j, k: (i, j))
    out, out16 = pl.pallas_call(
        _ignoring(body, 2, len(deps)), name=name, grid=(M // m_block, N // n_block, k_steps),
        in_specs=[pl.BlockSpec((tk, m_block), lambda i, j, k: (k, i)),
                  pl.BlockSpec((tk, n_block), lambda i, j, k: (k, j))] + [_ANY] * len(deps),
        out_specs=[out_spec, out_spec],
        out_shape=[jax.ShapeDtypeStruct(shape, F32), jax.ShapeDtypeStruct(shape, BF16)],
        compiler_params=_params(dimension_semantics=("arbitrary", "arbitrary", "arbitrary")),
    )(lhs, rhs, *deps)
    if not column_shards:
        out = out.reshape(N_SHARD, M // N_SHARD, N)
        out16 = out16.reshape(N_SHARD, M // N_SHARD, N)
    return out, out16


def _rope_tables(pos, invf):
    S = pos.shape[0]
    tm = MIXER_TILE

    def body(pos_ref, invf_ref, out_ref):
        ang = pos_ref[...].astype(F32) * invf_ref[...]
        cos, sin = jnp.cos(ang), jnp.sin(ang)
        lane = lax.broadcasted_iota(jnp.int32, ang.shape, 1) % HEAD_DIM
        first = lane < ROT_DIM // 2
        second = (lane >= ROT_DIM // 2) & (lane < ROT_DIM)
        out_ref[0] = jnp.where(lane < ROT_DIM, cos, 1.0)
        out_ref[1] = jnp.where(first, sin, 0.0)
        out_ref[2] = jnp.where(second, sin, 0.0)

    return pl.pallas_call(
        body, name="rope_tables", grid=(S // tm,),
        in_specs=[_rows(tm, 1), _full((1, _LANES))],
        out_specs=pl.BlockSpec((3, tm, _LANES), lambda i: (0, i, 0)),
        out_shape=jax.ShapeDtypeStruct((3, S, _LANES), F32),
        compiler_params=_params(dimension_semantics=("arbitrary",)),
    )(pos, invf)


def _table_spec(tm):
    return pl.BlockSpec((3, tm, _LANES), lambda i: (0, i, 0))


_HALF = ROT_DIM // 2
_LANES = 128


def _rope(t, tables):
    c, s_first, s_second = tables
    return t * c - pltpu.roll(t, _LANES - _HALF, axis=1) * s_first + pltpu.roll(t, _HALF, axis=1) * s_second


def _rope_transposed(t, tables):
    c, s_first, s_second = tables
    return t * c - pltpu.roll(t * s_first, _HALF, axis=1) + pltpu.roll(t * s_second, _LANES - _HALF, axis=1)


def _store_head_variants(ref, t):
    rolled = pltpu.roll(t, HEAD_DIM, axis=1)
    low = lax.broadcasted_iota(jnp.int32, t.shape, 1) < HEAD_DIM
    zero = jnp.zeros_like(t)
    ref[0] = jnp.where(low, t, zero).astype(BF16)
    ref[1] = jnp.where(low, zero, rolled).astype(BF16)
    ref[2] = jnp.where(low, rolled, zero).astype(BF16)
    ref[3] = jnp.where(low, zero, t).astype(BF16)


def _mixer_in_fwd(x, pre, w_in_t, mine, rope, deps=()):
    S = x.shape[0]
    tm = MIXER_TILE

    def body(x_ref, pre_ref, w_land, w_own, mine_ref, rope_ref, u_ref, q_ref, k_ref, v_ref, w_v, sem):
        _load_once(_gathered(w_land, w_own, w_v, mine_ref[0], rows=IN_WIDTH // N_SHARD), sem)
        xv = x_ref[...]
        h = ((xv * _rstd(xv)) * pre_ref[...]).astype(BF16)
        z = _dot_nt(h, w_v[...])
        tables = (rope_ref[0], rope_ref[1], rope_ref[2])
        u_ref[...] = z[:, :POOL_WIDTH]
        for t in range(ATTN_WIDTH // _LANES):
            lo = POOL_WIDTH + t * _LANES
            q_ref[:, t * _LANES:(t + 1) * _LANES] = (_rope(z[:, lo:lo + _LANES], tables) * ATTN_SCALE).astype(BF16)
        kv = POOL_WIDTH + ATTN_WIDTH
        _store_head_variants(k_ref, _rope(z[:, kv:kv + KV_WIDTH], tables))
        _store_head_variants(v_ref, z[:, kv + KV_WIDTH:])

    args = [x, pre, *w_in_t, mine, rope]
    variants = pl.BlockSpec((2 * N_KV_HEADS, tm, KV_WIDTH), lambda i: (0, i, 0))
    return pl.pallas_call(
        _ignoring(body, len(args), len(deps)), name="mixer_in_fwd", grid=(S // tm,),
        in_specs=[_rows(tm, D_MODEL), _full((1, D_MODEL)), _ANY, _ANY, _SMEM, _table_spec(tm)] + [_ANY] * len(deps),
        out_specs=[_rows(tm, POOL_WIDTH), _rows(tm, ATTN_WIDTH), variants, variants],
        out_shape=[jax.ShapeDtypeStruct((S, POOL_WIDTH), F32), jax.ShapeDtypeStruct((S, ATTN_WIDTH), BF16),
                   jax.ShapeDtypeStruct((2 * N_KV_HEADS, S, KV_WIDTH), BF16),
                   jax.ShapeDtypeStruct((2 * N_KV_HEADS, S, KV_WIDTH), BF16)],
        scratch_shapes=[pltpu.VMEM((IN_WIDTH, D_MODEL), BF16), pltpu.SemaphoreType.DMA((N_SHARD,))],
        compiler_params=_params(dimension_semantics=("arbitrary",)),
    )(*args, *deps)


def _mixer_in_bwd(dres, x, pre, w_in_t, mine, du, dq, dk, dv, dk_next, dv_next, rope, deps=()):
    S = x.shape[0]
    tm = MIXER_TILE

    def body(dres_ref, x_ref, pre_ref, w_land, w_own, mine_ref, du_ref, dq_ref, dk_ref, dv_ref, dkx_ref, dvx_ref,
             rope_ref,
             dx_ref, dz_ref, h_ref, dpre_ref, w_v, sem):
        _load_once(_gathered(w_land, w_own, w_v, mine_ref[0], rows=IN_WIDTH // N_SHARD), sem)

        @pl.when(pl.program_id(0) == 0)
        def _():
            dpre_ref[...] = jnp.zeros_like(dpre_ref)

        tables = (rope_ref[0], rope_ref[1], rope_ref[2])
        dz_ref[:, :POOL_WIDTH] = du_ref[...]
        for t in range(ATTN_WIDTH // _LANES):
            lo = POOL_WIDTH + t * _LANES
            dz_ref[:, lo:lo + _LANES] = _rope_transposed(dq_ref[:, t * _LANES:(t + 1) * _LANES], tables).astype(BF16)
        kv = POOL_WIDTH + ATTN_WIDTH
        has_next = pl.program_id(0) + 1 < steps
        pad = jnp.zeros((tm - BLOCK, KV_WIDTH), F32)
        dk_tile = dk_ref[...] + jnp.concatenate([pad, jnp.where(has_next, dkx_ref[...], 0.0)], axis=0)
        dv_tile = dv_ref[...] + jnp.concatenate([pad, jnp.where(has_next, dvx_ref[...], 0.0)], axis=0)
        dz_ref[:, kv:kv + KV_WIDTH] = _rope_transposed(dk_tile, tables).astype(BF16)
        dz_ref[:, kv + KV_WIDTH:] = dv_tile.astype(BF16)
        dh = _dot(dz_ref[...], w_v[...])
        xv = x_ref[...]
        rx = _rstd(xv)
        xn = xv * rx
        h_ref[...] = (xn * pre_ref[...]).astype(BF16)
        dpre_ref[...] += jnp.sum(dh * xn, axis=0, keepdims=True)
        dx_ref[...] = dres_ref[...] + _norm_bwd(dh, xn, rx, pre_ref[...])

    assert tm == ATTN_BLOCKS * BLOCK
    steps = S // tm
    nxt = pl.BlockSpec((None, BLOCK, KV_WIDTH), lambda i: (jnp.minimum(i + 1, steps - 1), 0, 0))
    args = [dres, x, pre, *w_in_t, mine, du, dq, dk, dv, dk_next, dv_next, rope]
    return pl.pallas_call(
        _ignoring(body, len(args), len(deps)), name="mixer_in_bwd", grid=(S // tm,),
        in_specs=[_rows(tm, D_MODEL), _rows(tm, D_MODEL), _full((1, D_MODEL)), _ANY, _ANY, _SMEM,
                  _rows(tm, POOL_WIDTH), _rows(tm, ATTN_WIDTH), _rows(tm, KV_WIDTH), _rows(tm, KV_WIDTH), nxt, nxt,
                  _table_spec(tm)] + [_ANY] * len(deps),
        out_specs=[_rows(tm, D_MODEL), _rows(tm, IN_WIDTH), _rows(tm, D_MODEL), _full((1, D_MODEL))],
        out_shape=[jax.ShapeDtypeStruct((S, D_MODEL), F32), jax.ShapeDtypeStruct((S, IN_WIDTH), BF16),
                   jax.ShapeDtypeStruct((S, D_MODEL), BF16), jax.ShapeDtypeStruct((1, D_MODEL), F32)],
        scratch_shapes=[pltpu.VMEM((IN_WIDTH, D_MODEL), BF16), pltpu.SemaphoreType.DMA((N_SHARD,))],
        compiler_params=_params(dimension_semantics=("arbitrary",)),
    )(*args, *deps)


def _pool_counts(tile_index, tm, width):
    t = tile_index * tm + lax.broadcasted_iota(jnp.int32, (tm, 1), 0)
    return jnp.minimum(t + 1, width).astype(F32)


def _pool_features(ext, u_tile, tile_index, tm):
    ds = []
    for gi, width in enumerate(POOL_WINDOWS):
        lanes = slice(gi * POOL_GROUP, (gi + 1) * POOL_GROUP)
        s = ext[:, lanes]
        shift = 1
        while shift < width:
            s = s + pltpu.roll(s, shift, axis=0)
            shift *= 2
        ds.append(s[HALO:, :] / _pool_counts(tile_index, tm, width) - u_tile[:, lanes])
    return ds


def _pool_fwd(u, w_pool, pool_scale, g_pool):
    S = u.shape[0]
    tm = MIXER_TILE

    def body(u_ref, w_ref, scale_ref, gain_ref, y_ref, ext_ref):
        i = pl.program_id(0)

        @pl.when(i == 0)
        def _():
            ext_ref[:HALO, :] = jnp.zeros((HALO, POOL_WIDTH), F32)

        u_tile = u_ref[...]
        ext_ref[HALO:, :] = u_tile
        ds = _pool_features(ext_ref[...], u_tile, i, tm)
        ext_ref[:HALO, :] = u_tile[tm - HALO:, :]
        ys = [_dot(ds[gi].astype(BF16), w_ref[gi].astype(BF16)) for gi in range(len(POOL_WINDOWS))]
        po = jnp.concatenate(ys, axis=1) * scale_ref[...]
        y_ref[...] = ((po * _rstd(po)) * gain_ref[...]).astype(BF16)

    return pl.pallas_call(
        body, name="pool_fwd", grid=(S // tm,),
        in_specs=[_rows(tm, POOL_WIDTH), _full((len(POOL_WINDOWS), POOL_GROUP, POOL_GROUP)),
                  _full((1, POOL_WIDTH)), _full((1, POOL_WIDTH))],
        out_specs=_rows(tm, POOL_WIDTH),
        out_shape=jax.ShapeDtypeStruct((S, POOL_WIDTH), BF16),
        scratch_shapes=[pltpu.VMEM((HALO + tm, POOL_WIDTH), F32)],
        compiler_params=_params(dimension_semantics=("arbitrary",)),
    )(u, w_pool, pool_scale, g_pool)


def _pool_bwd(dy, u, w_pool, pool_scale, g_pool):
    S = u.shape[0]
    tm = MIXER_TILE
    n_tiles = S // tm
    halo_blocks = tm // HALO

    def body(dy_ref, u_ref, uprev_ref, w_ref, scale_ref, gain_ref,
             du_ref, dw_ref, dscale_ref, dgain_ref, ext_ref, nxt_ref):
        i = pl.program_id(0)
        tile = n_tiles - 1 - i

        @pl.when(i == 0)
        def _():
            dw_ref[...] = jnp.zeros_like(dw_ref)
            dscale_ref[...] = jnp.zeros_like(dscale_ref)
            dgain_ref[...] = jnp.zeros_like(dgain_ref)
            nxt_ref[...] = jnp.zeros_like(nxt_ref)

        u_tile = u_ref[...]
        ext_ref[:HALO, :] = jnp.where(tile > 0, uprev_ref[...], 0.0)
        ext_ref[HALO:, :] = u_tile
        ds = _pool_features(ext_ref[...], u_tile, tile, tm)
        dsb = [d.astype(BF16) for d in ds]
        wb = [w_ref[gi].astype(BF16) for gi in range(len(POOL_WINDOWS))]
        yraw = jnp.concatenate([_dot(dsb[gi], wb[gi]) for gi in range(len(POOL_WINDOWS))], axis=1)
        po = yraw * scale_ref[...]
        r = _rstd(po)
        pn = po * r
        dyv = dy_ref[...]
        dgain_ref[...] += jnp.sum(dyv * pn, axis=0, keepdims=True)
        dpo = _norm_bwd(dyv, pn, r, gain_ref[...])
        dscale_ref[...] += jnp.sum(dpo * yraw, axis=0, keepdims=True)
        dyraw = (dpo * scale_ref[...]).astype(BF16)
        for gi, width in enumerate(POOL_WINDOWS):
            lanes = slice(gi * POOL_GROUP, (gi + 1) * POOL_GROUP)
            dw_ref[gi] += _dot_tn(dsb[gi], dyraw[:, lanes])
            dd = _dot_nt(dyraw[:, lanes], wb[gi])
            ddc = dd / _pool_counts(tile, tm, width)
            ext_ref[:tm, lanes] = ddc
            ext_ref[tm:, lanes] = nxt_ref[:, lanes]
            s = ext_ref[:, lanes]
            shift = 1
            while shift < width:
                s = s + pltpu.roll(s, HALO + tm - shift, axis=0)
                shift *= 2
            du_ref[:, lanes] = (s[:tm, :] - dd).astype(BF16)
            nxt_ref[:, lanes] = ddc[:HALO, :]

    return pl.pallas_call(
        body, name="pool_bwd", grid=(n_tiles,),
        in_specs=[pl.BlockSpec((tm, POOL_WIDTH), lambda i: (n_tiles - 1 - i, 0)),
                  pl.BlockSpec((tm, POOL_WIDTH), lambda i: (n_tiles - 1 - i, 0)),
                  pl.BlockSpec((HALO, POOL_WIDTH), lambda i: (jnp.maximum((n_tiles - 1 - i) * halo_blocks - 1, 0), 0)),
                  _full((len(POOL_WINDOWS), POOL_GROUP, POOL_GROUP)), _full((1, POOL_WIDTH)), _full((1, POOL_WIDTH))],
        out_specs=[pl.BlockSpec((tm, POOL_WIDTH), lambda i: (n_tiles - 1 - i, 0)),
                   _full((len(POOL_WINDOWS), POOL_GROUP, POOL_GROUP)), _full((1, POOL_WIDTH)), _full((1, POOL_WIDTH))],
        out_shape=[jax.ShapeDtypeStruct((S, POOL_WIDTH), BF16),
                   jax.ShapeDtypeStruct((len(POOL_WINDOWS), POOL_GROUP, POOL_GROUP), F32),
                   jax.ShapeDtypeStruct((1, POOL_WIDTH), F32), jax.ShapeDtypeStruct((1, POOL_WIDTH), F32)],
        scratch_shapes=[pltpu.VMEM((HALO + tm, POOL_WIDTH), F32), pltpu.VMEM((HALO, POOL_WIDTH), F32)],
        compiler_params=_params(dimension_semantics=("arbitrary",)),
    )(dy, u, u, w_pool, pool_scale, g_pool)


def _variant(head):
    return 2 * (head // (N_HEADS // N_KV_HEADS)) + head % 2


def _own_block(shape=(BLOCK, BLOCK)):
    r = lax.broadcasted_iota(jnp.int32, shape, 0)
    i = lax.broadcasted_iota(jnp.int32, shape, 1)
    return r <= i


def _fold_band(own, from_own, from_prev):
    return jnp.where(own, from_own, from_prev)


def _scores_by_head(own_tiles, prev_tiles, q_tiles):
    stacks = [jnp.concatenate(q_tiles[:2], axis=0), jnp.concatenate(q_tiles[2:], axis=0)]
    by_var = [_dot_nt(jnp.concatenate([own_tiles[v], prev_tiles[v]], axis=0), stacks[v // 2])
              for v in range(2 * N_KV_HEADS)]
    quadrant = lambda h, rows: by_var[_variant(h)][rows * BLOCK:(rows + 1) * BLOCK,
                                                   ((h // 2) % 2) * BLOCK:((h // 2) % 2 + 1) * BLOCK]
    return [quadrant(h, 0) for h in range(N_HEADS)], [quadrant(h, 1) for h in range(N_HEADS)]


def _softmax_t(s, sink):
    m = jnp.maximum(jnp.max(s, axis=0, keepdims=True), sink)
    p = jnp.exp(s - m)
    p_sink = jnp.exp(sink - m)
    inv = 1.0 / (jnp.sum(p, axis=0, keepdims=True) + p_sink)
    return p * inv, p_sink * inv


def _attn_fwd(q, kz, vz, sinks, g_attn, y_pool, x, w_out, mine, post):
    S = q.shape[0]
    tq = ATTN_BLOCKS * BLOCK
    n_var = 2 * N_KV_HEADS

    def body(q_ref, kp_ref, kc_ref, vp_ref, vc_ref, sinks_ref, gain_ref, yp_ref, x_ref, w_land, w_own, mine_ref,
             post_ref, o_ref, out_ref, m_ref, y_ref, w_v, sem):
        _load_once(_gathered(w_land, w_own, w_v, mine_ref[0], rows=D_MODEL // N_SHARD), sem)
        step = pl.program_id(0)
        own = _own_block()
        zero = jnp.zeros((BLOCK, BLOCK), F32)

        def tiles(cur_ref, prev_ref, j):
            rows = lambda jj: slice(jj * BLOCK, (jj + 1) * BLOCK)
            return ([cur_ref[v, rows(j), :] for v in range(n_var)],
                    [prev_ref[v] if j == 0 else cur_ref[v, rows(j - 1), :] for v in range(n_var)])

        scores = []
        for j in range(ATTN_BLOCKS):
            q_pairs = [q_ref[j * BLOCK:(j + 1) * BLOCK, i * _LANES:(i + 1) * _LANES] for i in range(N_HEADS // 2)]
            scores.append(_scores_by_head(*tiles(kc_ref, kp_ref, j), q_pairs))
        probs = []
        for j in range(ATTN_BLOCKS):
            s_own, s_prev = scores[j]
            no_prev = jnp.where(step > 0, 0.0, NEG_INF) if j == 0 else 0.0
            p_own, p_prev = [], []
            for h in range(N_HEADS):
                p, _ = _softmax_t(_fold_band(own, s_own[h], s_prev[h] + no_prev), sinks_ref[0, h])
                p_own.append(jnp.where(own, p, zero).astype(BF16))
                p_prev.append(jnp.where(own, zero, p).astype(BF16))
            probs.append((p_own, p_prev))
        blocks = []
        for j in range(ATTN_BLOCKS):
            p_own, p_prev = probs[j]
            v_own, v_prev = tiles(vc_ref, vp_ref, j)
            pairs = []
            for i in range(N_HEADS // 2):
                acc = None
                for h in (2 * i, 2 * i + 1):
                    part = _dot_tn(p_own[h], v_own[_variant(h)]) + _dot_tn(p_prev[h], v_prev[_variant(h)])
                    acc = part if acc is None else acc + part
                pairs.append(acc)
            blocks.append(jnp.concatenate(pairs, axis=1))
        o = jnp.concatenate(blocks, axis=0)
        o_ref[...] = o
        y_ref[:, :POOL_WIDTH] = yp_ref[...]
        y_ref[:, POOL_WIDTH:] = ((o * _rstd(o)) * gain_ref[...]).astype(BF16)
        m = _dot(y_ref[...], w_v[...])
        m_ref[...] = m
        out_ref[...] = x_ref[...] + (m * _rstd(m)) * post_ref[...]

    prev = pl.BlockSpec((n_var, BLOCK, KV_WIDTH), lambda g: (0, jnp.maximum(g * ATTN_BLOCKS - 1, 0), 0))
    cur = pl.BlockSpec((n_var, tq, KV_WIDTH), lambda g: (0, g, 0))
    return pl.pallas_call(
        body, name="attn_fwd", grid=(S // tq,),
        in_specs=[_rows(tq, ATTN_WIDTH), prev, cur, prev, cur,
                  pl.BlockSpec(memory_space=pltpu.SMEM), _full((1, ATTN_WIDTH)),
                  _rows(tq, POOL_WIDTH), _rows(tq, D_MODEL), _ANY, _ANY, _SMEM, _full((1, D_MODEL))],
        out_specs=[_rows(tq, ATTN_WIDTH), _rows(tq, D_MODEL), _rows(tq, D_MODEL), _rows(tq, D_MODEL)],
        out_shape=[jax.ShapeDtypeStruct((S, ATTN_WIDTH), F32), jax.ShapeDtypeStruct((S, D_MODEL), F32),
                   jax.ShapeDtypeStruct((S, D_MODEL), F32), jax.ShapeDtypeStruct((S, D_MODEL), BF16)],
        scratch_shapes=[pltpu.VMEM((D_MODEL, D_MODEL), BF16), pltpu.SemaphoreType.DMA((N_SHARD,))],
        compiler_params=_params(dimension_semantics=("arbitrary",)),
    )(q, kz, kz, vz, vz, sinks, g_attn, y_pool, x, *w_out, mine, post)


def _attn_bwd(dout, m, w_out, mine, post, o, q, kz, vz, sinks, g_attn, deps=()):
    S = q.shape[0]
    tq = ATTN_BLOCKS * BLOCK
    n_var = 2 * N_KV_HEADS

    def body(dout_ref, m_ref, w_land, w_own, mine_ref, post_ref, o_ref, q_ref, kp_ref, kc_ref, vp_ref, vc_ref,
             sinks_ref, gain_ref, dyp_ref, dm_ref, dpost_ref, dq_ref, dk_ref, dv_ref, dkx_ref, dvx_ref, dsink_ref,
             dgain_ref, w_v, sem):
        _load_once(_gathered(w_land, w_own, w_v, mine_ref[0], rows=D_MODEL // N_SHARD), sem)
        step = pl.program_id(0)

        @pl.when(step == 0)
        def _():
            dpost_ref[...] = jnp.zeros_like(dpost_ref)
            dsink_ref[...] = jnp.zeros_like(dsink_ref)
            dgain_ref[...] = jnp.zeros_like(dgain_ref)

        mv = m_ref[...]
        rm = _rstd(mv)
        mn = mv * rm
        dres = dout_ref[...]
        dpost_ref[...] += jnp.sum(dres * mn, axis=0, keepdims=True)
        dm = _norm_bwd(dres, mn, rm, post_ref[...]).astype(BF16)
        dm_ref[...] = dm
        dy = _dot_nt(dm, w_v[...])
        dyp_ref[...] = dy[:, :POOL_WIDTH]
        ov = o_ref[...]
        r = _rstd(ov)
        on = ov * r
        dyv = dy[:, POOL_WIDTH:]
        dgain_ref[...] += jnp.sum(dyv * on, axis=0, keepdims=True)
        do = _norm_bwd(dyv, on, r, gain_ref[...]).astype(BF16)
        own = _own_block()
        zero = jnp.zeros((BLOCK, BLOCK), F32)
        split = lambda t: (jnp.where(own, t, zero).astype(BF16), jnp.where(own, zero, t).astype(BF16))
        rows = lambda j: slice(j * BLOCK, (j + 1) * BLOCK)
        heads = range(N_HEADS)

        def tiles(cur_ref, prev_ref, j):
            return ([cur_ref[v, rows(j), :] for v in range(n_var)],
                    [prev_ref[v] if j == 0 else cur_ref[v, rows(j - 1), :] for v in range(n_var)])

        q_pairs = [[q_ref[rows(j), i * _LANES:(i + 1) * _LANES] for i in range(N_HEADS // 2)] for j in range(ATTN_BLOCKS)]
        do_pairs = [[do[rows(j), i * _LANES:(i + 1) * _LANES] for i in range(N_HEADS // 2)] for j in range(ATTN_BLOCKS)]
        scores = [(_scores_by_head(*tiles(kc_ref, kp_ref, j), q_pairs[j]),
                   _scores_by_head(*tiles(vc_ref, vp_ref, j), do_pairs[j])) for j in range(ATTN_BLOCKS)]
        parts, sink_sum = [], None
        for j in range(ATTN_BLOCKS):
            (s_own, s_prev), (dp_own, dp_prev) = scores[j]
            no_prev = jnp.where(step > 0, 0.0, NEG_INF) if j == 0 else 0.0
            ds_parts, p_parts, sink_rows = [], [], []
            for h in heads:
                p, p_sink = _softmax_t(_fold_band(own, s_own[h], s_prev[h] + no_prev), sinks_ref[0, h])
                dp = _fold_band(own, dp_own[h], dp_prev[h])
                delta = jnp.sum(p * dp, axis=0, keepdims=True)
                ds_parts.append(split(p * (dp - delta)))
                p_parts.append(split(p))
                sink_rows.append(jnp.zeros((1, _LANES), F32) - jnp.sum(p_sink * delta))
            block_sinks = jnp.concatenate(sink_rows, axis=0)
            sink_sum = block_sinks if sink_sum is None else sink_sum + block_sinks
            parts.append((ds_parts, p_parts))
        dsink_ref[...] += sink_sum
        low = lax.broadcasted_iota(jnp.int32, (BLOCK, _LANES), 1) < HEAD_DIM

        def merge(acc):
            return jnp.where(low, acc[0] + pltpu.roll(acc[1], HEAD_DIM, axis=1),
                             acc[3] + pltpu.roll(acc[2], HEAD_DIM, axis=1))
        add = lambda acc, var, t: acc.__setitem__(var, t if acc[var] is None else acc[var] + t)
        k_own, k_prev, v_own, v_prev = [], [], [], []
        for j in range(ATTN_BLOCKS):
            ds_parts, p_parts = parts[j]
            kt_own, kt_prev = tiles(kc_ref, kp_ref, j)
            dk_own, dk_prev, dv_own, dv_prev = ([None] * n_var for _ in range(4))
            for i in range(N_HEADS // 2):
                dq_pair = None
                for h in (2 * i, 2 * i + 1):
                    var = _variant(h)
                    (ds_o, ds_p), (p_o, p_p) = ds_parts[h], p_parts[h]
                    part = _dot_tn(ds_o, kt_own[var]) + _dot_tn(ds_p, kt_prev[var])
                    dq_pair = part if dq_pair is None else dq_pair + part
                    add(dk_own, var, _dot(ds_o, q_pairs[j][i]))
                    add(dk_prev, var, _dot(ds_p, q_pairs[j][i]))
                    add(dv_own, var, _dot(p_o, do_pairs[j][i]))
                    add(dv_prev, var, _dot(p_p, do_pairs[j][i]))
                dq_ref[rows(j), i * _LANES:(i + 1) * _LANES] = dq_pair * ATTN_SCALE
            k_own.append(merge(dk_own))
            k_prev.append(merge(dk_prev))
            v_own.append(merge(dv_own))
            v_prev.append(merge(dv_prev))
        for j in range(ATTN_BLOCKS):
            last = j == ATTN_BLOCKS - 1
            dk_ref[rows(j), :] = k_own[j] if last else k_own[j] + k_prev[j + 1]
            dv_ref[rows(j), :] = v_own[j] if last else v_own[j] + v_prev[j + 1]
        dkx_ref[...] = k_prev[0]
        dvx_ref[...] = v_prev[0]

    steps = S // tq
    prev = pl.BlockSpec((n_var, BLOCK, KV_WIDTH), lambda g: (0, jnp.maximum(g * ATTN_BLOCKS - 1, 0), 0))
    cur = pl.BlockSpec((n_var, tq, KV_WIDTH), lambda g: (0, g, 0))
    nxt = pl.BlockSpec((None, BLOCK, KV_WIDTH), lambda g: (g, 0, 0))
    args = [dout, m, *w_out, mine, post, o, q, kz, kz, vz, vz, sinks, g_attn]
    return pl.pallas_call(
        _ignoring(body, len(args), len(deps)), name="attn_bwd", grid=(steps,),
        in_specs=[_rows(tq, D_MODEL), _rows(tq, D_MODEL), _ANY, _ANY, _SMEM, _full((1, D_MODEL)),
                  _rows(tq, ATTN_WIDTH), _rows(tq, ATTN_WIDTH), prev, cur, prev, cur,
                  pl.BlockSpec(memory_space=pltpu.SMEM), _full((1, ATTN_WIDTH))] + [_ANY] * len(deps),
        out_specs=[_rows(tq, POOL_WIDTH), _rows(tq, D_MODEL), _full((1, D_MODEL)),
                   _rows(tq, ATTN_WIDTH), _rows(tq, KV_WIDTH), _rows(tq, KV_WIDTH), nxt, nxt,
                   _full((N_HEADS, _LANES)), _full((1, ATTN_WIDTH))],
        out_shape=[jax.ShapeDtypeStruct((S, POOL_WIDTH), F32), jax.ShapeDtypeStruct((S, D_MODEL), BF16),
                   jax.ShapeDtypeStruct((1, D_MODEL), F32),
                   jax.ShapeDtypeStruct((S, ATTN_WIDTH), F32), jax.ShapeDtypeStruct((S, KV_WIDTH), F32),
                   jax.ShapeDtypeStruct((S, KV_WIDTH), F32),
                   jax.ShapeDtypeStruct((steps, BLOCK, KV_WIDTH), F32), jax.ShapeDtypeStruct((steps, BLOCK, KV_WIDTH), F32),
                   jax.ShapeDtypeStruct((N_HEADS, _LANES), F32), jax.ShapeDtypeStruct((1, ATTN_WIDTH), F32)],
        scratch_shapes=[pltpu.VMEM((D_MODEL, D_MODEL), BF16), pltpu.SemaphoreType.DMA((N_SHARD,))],
        compiler_params=_params(dimension_semantics=("arbitrary",)),
    )(*args, *deps)


def _inv_freq_row():
    inv_freq = ROPE_THETA ** (-jnp.arange(0, ROT_DIM, 2, dtype=F32) / ROT_DIM)
    per_head = jnp.concatenate([inv_freq, inv_freq, jnp.zeros((HEAD_DIM - ROT_DIM,), F32)])
    return jnp.tile(per_head, _LANES // HEAD_DIM).reshape(1, _LANES)


def _local_step(x, pos, target, small, mine, weights_of, grads_ready):
    rope = _rope_tables(pos, _inv_freq_row())
    (wgu1,) = weights_of("ffn1_up", (rope,))
    g1, u1, a1 = _ffn_up(x, small["ffn1_pre"], wgu1, mine)
    (wd1,) = weights_of("ffn1_down", (a1,))
    x1, f1 = _ffn_down(x, a1, small["ffn1_post"], wd1, mine)
    w_in_t, w_out = weights_of("mixer", (x1,))
    u, q, kz, vz = _mixer_in_fwd(x1, small["mix_pre"], w_in_t, mine, rope)
    y_pool = _pool_fwd(u, small["w_pool"], small["pool_scale"], small["g_pool"])
    o, x2, m, y = _attn_fwd(q, kz, vz, small["sinks"], small["g_attn"], y_pool, x1, w_out, mine, small["mix_post"])
    wgu2, wd2 = weights_of("ffn2", (x2,))
    g2, u2, a2 = _ffn_up(x2, small["ffn2_pre"], wgu2, mine)
    dx3, df2, loss_acc, dpost2 = _ffn_down(x2, a2, small["ffn2_post"], wd2, mine, target=target)
    grads = {"loss": loss_acc * (0.5 / D_MODEL), "ffn2_post": dpost2}
    dx2, h3, dgu2, grads["ffn2_pre"] = _ffn_bwd(
        dx3, x2, df2, g2, u2, small["ffn2_pre"], small["ffn2_post"], wgu2, wd2, mine, df_known=True)
    dwgu2 = _wgrad(h3, dgu2, D_MODEL, FF_CHUNK, "wgrad_gu2", column_shards=True)
    dwd2 = _wgrad(a2, df2, FF_CHUNK, D_MODEL, "wgrad_down2")
    deps = grads_ready("ffn2", {"ffn2_w_gu": dwgu2, "ffn2_w_down": dwd2})
    dy_pool, dm, grads["mix_post"], dq, dk, dv, dk_next, dv_next, dsinks, grads["g_attn"] = _attn_bwd(
        dx2, m, w_out, mine, small["mix_post"], o, q, kz, vz, small["sinks"], small["g_attn"], deps=deps)
    dw_out = _wgrad(y, dm, D_MODEL, D_MODEL, "wgrad_out")
    grads["sinks"] = dsinks[:, 0].reshape(1, N_HEADS)
    du, grads["w_pool"], grads["pool_scale"], grads["g_pool"] = _pool_bwd(
        dy_pool, u, small["w_pool"], small["pool_scale"], small["g_pool"])
    dx1, dz, h2, grads["mix_pre"] = _mixer_in_bwd(dx2, x1, small["mix_pre"], w_in_t, mine, du, dq, dk, dv, dk_next, dv_next, rope)
    dw_in_t = _wgrad(dz, h2, IN_WIDTH, D_MODEL, "wgrad_in")
    deps = grads_ready("mixer", {"w_in": dw_in_t, "w_out": dw_out})
    dx, h1, dgu1, df1, grads["ffn1_pre"], grads["ffn1_post"] = _ffn_bwd(
        dx1, x, f1, g1, u1, small["ffn1_pre"], small["ffn1_post"], wgu1, wd1, mine, deps=deps)
    dwgu1 = _wgrad(h1, dgu1, D_MODEL, FF_CHUNK, "wgrad_gu1", column_shards=True)
    deps = grads_ready("ffn1_gu", {"ffn1_w_gu": dwgu1}, small=grads)
    dwd1 = _wgrad(a1, df1, FF_CHUNK, D_MODEL, "wgrad_down1", deps=deps)
    grads_ready("ffn1_down", {"ffn1_w_down": dwd1})
    return dx


def _place():
    return lax.axis_index("x"), lax.axis_index("y"), lax.axis_index("c")


def _other_chips(x, y):
    return [(1 - x, y), (x, 1 - y), (1 - x, 1 - y)]


_HBM = pl.BlockSpec(memory_space=pltpu.HBM)
_SEM = pl.BlockSpec(memory_space=pltpu.SEMAPHORE)
_EFFECT = pltpu.SideEffectType.DATAFLOW_SIDE_EFFECTING
GATHER, GATHER_HALF, REDUCE, BROADCAST = "gather", "gather_half", "reduce", "broadcast"
N_DEVICES = 8


def _in_hbm(a):
    return pltpu.with_memory_space_constraint(a, pltpu.HBM)


def _core_half(rows, c):
    return pl.ds(pl.multiple_of(c * (rows // 2), 16), rows // 2)


def _chip_copies(kind, srcs, lands, send_sems, recv_sems):
    x, y, c = _place()
    mine = 2 * x + y
    copies = []
    for w in range(len(srcs)):
        if kind == BROADCAST:
            peers = [(x ^ (k >> 2), y ^ ((k >> 1) & 1), c ^ (k & 1)) for k in range(1, N_DEVICES)]
        else:
            peers = [(px, py, c) for px, py in _other_chips(x, y)]
        for k, (px, py, pc) in enumerate(peers):
            if kind == GATHER:
                src, dst = srcs[w], lands[w].at[mine]
            elif kind == GATHER_HALF:
                half = _core_half(srcs[w].shape[0], c)
                src, dst = srcs[w].at[half, :], lands[w].at[mine, half, :]
            elif kind == BROADCAST:
                src, dst = srcs[w], lands[w].at[2 * mine + c]
            else:
                src, dst = srcs[w].at[2 * px + py], lands[w].at[k]
            pair = len(peers) * w + k
            copies.append(pltpu.make_async_remote_copy(
                src_ref=src, dst_ref=dst, send_sem=send_sems.at[pair], recv_sem=recv_sems.at[pair],
                device_id=(px, py, pc), device_id_type=MESH))
    return copies


def _landing_shape(kind, src):
    if kind == REDUCE:
        return (N_SHARD - 1,) + src.shape[1:]
    return ((N_DEVICES if kind == BROADCAST else N_SHARD),) + src.shape


def _peer_count(kind):
    return N_DEVICES - 1 if kind == BROADCAST else N_SHARD - 1


def _exchange_start(kinds, groups, name):
    sizes = [len(g) for g in groups]
    flat = [s for g in groups for s in g]
    n, ng = len(flat), len(groups)

    def body(*refs):
        srcs, lands = refs[:n], refs[n:2 * n]
        sems = refs[2 * n:2 * n + 2 * ng]
        token = refs[-1]
        start = 0
        for gi, size in enumerate(sizes):
            for cp in _chip_copies(kinds[gi], srcs[start:start + size], lands[start:start + size],
                                   sems[2 * gi], sems[2 * gi + 1]):
                cp.start()
            start += size
        token[...] = jnp.zeros_like(token)

    landings = [lax.empty(_landing_shape(kind, s), s.dtype) for kind, g in zip(kinds, groups) for s in g]
    sem_shapes = [pltpu.SemaphoreType.DMA((size * _peer_count(kind),)) for kind, size in zip(kinds, sizes)
                  for _ in range(2)]
    outs = pl.pallas_call(
        body, name=name,
        in_specs=[_HBM] * (2 * n),
        out_specs=[_SEM] * (2 * ng) + [_HBM] * (2 * n) + [pl.BlockSpec(memory_space=pltpu.VMEM)],
        out_shape=sem_shapes + [pltpu.HBM(a.shape, a.dtype) for a in flat + landings]
        + [jax.ShapeDtypeStruct((8, _LANES), F32)],
        input_output_aliases={i: 2 * ng + i for i in range(2 * n)},
        compiler_params=pltpu.CompilerParams(has_side_effects=_EFFECT),
    )(*[_in_hbm(a) for a in flat + landings])
    sems, srcs, lands, token = outs[:2 * ng], outs[2 * ng:2 * ng + n], outs[2 * ng + n:2 * ng + 2 * n], outs[-1]
    handles, start = [], 0
    for gi, size in enumerate(sizes):
        handles.append((sems[2 * gi], sems[2 * gi + 1], srcs[start:start + size], lands[start:start + size]))
        start += size
    return handles, token


def _exchange_wait(kind, handle, after, name):
    send_sems, recv_sems, srcs, lands = handle
    n = len(srcs)

    def body(*refs):
        copies = _chip_copies(kind, refs[:n], refs[n:2 * n], refs[2 * n], refs[2 * n + 1])
        for cp in copies:
            cp.wait_send()
        for cp in copies:
            cp.wait_recv()

    outs = pl.pallas_call(
        body, name=name,
        in_specs=[_HBM] * (2 * n) + [_SEM, _SEM] + [_ANY] * len(after),
        out_specs=[_HBM] * (2 * n),
        out_shape=[pltpu.HBM(a.shape, a.dtype) for a in list(srcs) + list(lands)],
        input_output_aliases={i: i for i in range(2 * n)},
        compiler_params=pltpu.CompilerParams(has_side_effects=_EFFECT),
    )(*srcs, *lands, send_sems, recv_sems, *after)
    return outs[:n], outs[n:]


def _swap_gathered_halves(lands, name):
    n = len(lands)

    def body(*refs):
        bufs = refs[n:2 * n]
        send_sems, recv_sems = refs[2 * n:]
        x, y, c = _place()
        mine = 2 * x + y
        sends, arrivals = [], []
        for w in range(n):
            rows = bufs[w].shape[1]
            for d in range(1, N_SHARD):
                slot = (mine + d) % N_SHARD
                sems = dict(send_sem=send_sems.at[(N_SHARD - 1) * w + d - 1],
                            recv_sem=recv_sems.at[(N_SHARD - 1) * w + d - 1],
                            device_id=(x, y, 1 - c), device_id_type=MESH)
                fetched = bufs[w].at[slot, _core_half(rows, c), :]
                missing = bufs[w].at[slot, _core_half(rows, 1 - c), :]
                sends.append(pltpu.make_async_remote_copy(src_ref=fetched, dst_ref=fetched, **sems))
                arrivals.append(pltpu.make_async_remote_copy(src_ref=missing, dst_ref=missing, **sems))
        for cp in sends:
            cp.start()
        for cp in arrivals:
            cp.wait_recv()
        for cp in sends:
            cp.wait_send()

    return pl.pallas_call(
        body, name=name, in_specs=[_ANY] * n, out_specs=[_ANY] * n,
        out_shape=[jax.ShapeDtypeStruct(a.shape, a.dtype) for a in lands],
        input_output_aliases={i: i for i in range(n)},
        scratch_shapes=[pltpu.SemaphoreType.DMA((n * (N_SHARD - 1),)), pltpu.SemaphoreType.DMA((n * (N_SHARD - 1),))],
        compiler_params=pltpu.CompilerParams(has_side_effects=True),
    )(*lands)


def _swap_with_sibling(partials, name):
    n = len(partials)

    def body(*refs):
        ins, outs = refs[:n], refs[n:2 * n]
        send_sems, recv_sems = refs[2 * n:]
        x, y, c = _place()
        sends = [pltpu.make_async_remote_copy(
            src_ref=ins[w], dst_ref=outs[w], send_sem=send_sems.at[w], recv_sem=recv_sems.at[w],
            device_id=(x, y, 1 - c), device_id_type=MESH) for w in range(n)]
        for cp in sends:
            cp.start()
        for cp in sends:
            cp.wait_recv()
        for cp in sends:
            cp.wait_send()

    return pl.pallas_call(
        body, name=name,
        in_specs=[_ANY] * n, out_specs=[_ANY] * n,
        out_shape=[jax.ShapeDtypeStruct(p.shape, p.dtype) for p in partials],
        scratch_shapes=[pltpu.SemaphoreType.DMA((n,)), pltpu.SemaphoreType.DMA((n,))],
        compiler_params=pltpu.CompilerParams(has_side_effects=True),
    )(*partials)


def _row_block(rows, cap):
    best = None
    for cand in range(16, min(rows, cap) + 1, 16):
        if rows % cand == 0:
            best = cand
    assert best is not None, rows
    return best


def _chip_partial(own, received, shard, name):
    _, R, C = own.shape
    rb = _row_block(R, 512)

    def body(shard_ref, own_ref, rec_ref, out_ref):
        acc = own_ref[...]
        for k in range(3):
            acc = acc + rec_ref[k].astype(F32)
        out_ref[...] = acc.astype(BF16)

    return pl.pallas_call(
        body, name=name,
        grid_spec=pltpu.PrefetchScalarGridSpec(
            num_scalar_prefetch=1, grid=(R // rb,),
            in_specs=[pl.BlockSpec((None, rb, C), lambda i, s: (s[0], i, 0)),
                      pl.BlockSpec((3, rb, C), lambda i, s: (0, i, 0))],
            out_specs=pl.BlockSpec((rb, C), lambda i, s: (i, 0))),
        out_shape=jax.ShapeDtypeStruct((R, C), BF16),
        compiler_params=_params(dimension_semantics=("arbitrary",)),
    )(shard, own, received)


def _adamw(w, m, v, g_parts, name, slot=None):
    R, C = w.shape
    by_device = slot is not None
    rb = _row_block(R, 512) if R % 16 == 0 else R

    def body(w_ref, m_ref, v_ref, *refs):
        g_refs, (grad_ref, delta_ref, m_out, v_out) = refs[:-4], refs[-4:]
        if by_device:
            own_ref, land_ref, slot_ref = g_refs
            part = lambda d: jnp.where(slot_ref[0] == d, own_ref[...], land_ref[d])
            g = part(0)
            for d in range(1, N_DEVICES):
                g = g + part(d)
        else:
            g = g_refs[0][...].astype(F32)
            for g_ref in g_refs[1:]:
                g = g + g_ref[...].astype(F32)
        grad_ref[...] = g
        new_m = ADAM_B1 * m_ref[...] + (1.0 - ADAM_B1) * g
        new_v = ADAM_B2 * v_ref[...] + (1.0 - ADAM_B2) * (g * g)
        m_hat = new_m / (1.0 - ADAM_B1 ** ADAM_STEP)
        v_hat = new_v / (1.0 - ADAM_B2 ** ADAM_STEP)
        delta_ref[...] = -ADAM_LR * (m_hat / (jnp.sqrt(v_hat) + ADAM_EPS) + ADAM_WD * w_ref[...])
        m_out[...] = new_m
        v_out[...] = new_v

    spec = pl.BlockSpec((rb, C), lambda i: (i, 0))
    if by_device:
        g_specs = [spec, pl.BlockSpec((N_DEVICES, rb, C), lambda i: (0, i, 0)), _SMEM]
        g_parts = list(g_parts) + [slot]
    else:
        g_specs = [spec] * len(g_parts)
    return pl.pallas_call(
        body, name=name, grid=(R // rb,),
        in_specs=[spec, spec, spec] + g_specs,
        out_specs=[spec] * 4,
        out_shape=[jax.ShapeDtypeStruct((R, C), F32)] * 4,
        compiler_params=_params(dimension_semantics=("arbitrary",)),
    )(w, m, v, *g_parts)


SMALL_NAMES = ("ffn1_pre", "ffn1_post", "mix_pre", "pool_scale", "sinks", "g_pool", "g_attn", "mix_post",
               "ffn2_pre", "ffn2_post", "w_pool")
_SLAB_PART = 8 * _LANES


SLAB_NAMES = SMALL_NAMES + ("loss",)


def _to_slab(parts):
    rows = []
    for name in SLAB_NAMES:
        flat = parts[name].reshape(-1) if name in parts else jnp.zeros((_SLAB_PART,), F32)
        padded = -(-flat.shape[0] // _SLAB_PART) * _SLAB_PART
        rows.append(jnp.pad(flat, (0, padded - flat.shape[0])).reshape(-1, _LANES))
    return jnp.concatenate(rows, axis=0)


def _from_slab(slab, like):
    out, row = {}, 0
    for name in SLAB_NAMES:
        size = like[name].size
        rows = -(-size // _SLAB_PART) * (_SLAB_PART // _LANES)
        out[name] = slab[row:row + rows].reshape(-1)[:size].reshape(like[name].shape)
        row += rows
    return out


BIG_NAMES = ("ffn1_w_gu", "ffn1_w_down", "w_in", "w_out", "ffn2_w_gu", "ffn2_w_down")
WEIGHT_ORDER = ("ffn1_pre", "ffn1_w_gu", "ffn1_w_down", "ffn1_post", "mix_pre", "w_in", "w_pool", "pool_scale",
                "sinks", "g_pool", "g_attn", "w_out", "mix_post", "ffn2_pre", "ffn2_w_gu", "ffn2_w_down", "ffn2_post")


def kernel(x, positions, ffn1_pre, ffn1_w_gu, ffn1_w_down, ffn1_post, mix_pre, w_in, w_pool, pool_scale, sinks, g_pool, g_attn, w_out, mix_post, ffn2_pre, ffn2_w_gu, ffn2_w_down, ffn2_post, loss_target, m_ffn1_pre, m_ffn1_w_gu, m_ffn1_w_down, m_ffn1_post, m_mix_pre, m_w_in, m_w_pool, m_pool_scale, m_sinks, m_g_pool, m_g_attn, m_w_out, m_mix_post, m_ffn2_pre, m_ffn2_w_gu, m_ffn2_w_down, m_ffn2_post, v_ffn1_pre, v_ffn1_w_gu, v_ffn1_w_down, v_ffn1_post, v_mix_pre, v_w_in, v_w_pool, v_pool_scale, v_sinks, v_g_pool, v_g_attn, v_w_out, v_mix_post, v_ffn2_pre, v_ffn2_w_gu, v_ffn2_w_down, v_ffn2_post):
    given = dict(locals())
    weights = {n: given[n][0] for n in WEIGHT_ORDER}
    moments_m = {n: given["m_" + n][0] for n in WEIGHT_ORDER}
    moments_v = {n: given["v_" + n][0] for n in WEIGHT_ORDER}
    S = x.shape[1]
    shard = (2 * lax.axis_index("x") + lax.axis_index("y")).astype(jnp.int32).reshape(1)

    local16 = {n: weights[n].astype(BF16) for n in BIG_NAMES if n != "w_in"}
    local16["w_in"] = weights["w_in"].T.astype(BF16)
    gather_groups = {"ffn1_up": ("ffn1_w_gu",), "ffn1_down": ("ffn1_w_down",), "mixer": ("w_in", "w_out"),
                     "ffn2": ("ffn2_w_gu", "ffn2_w_down")}
    gather_kinds = {"ffn1_up": GATHER_HALF, "ffn1_down": GATHER_HALF, "mixer": GATHER, "ffn2": GATHER}
    handles, _ = _exchange_start(list(gather_kinds.values()),
                                 [[local16[n] for n in names] for names in gather_groups.values()], "gather_start")
    gather_handles = dict(zip(gather_groups, handles))

    def weights_of(group, after):
        kind = gather_kinds[group]
        owns, lands = _exchange_wait(kind, gather_handles[group], list(after), "gather_wait_" + group)
        if kind == GATHER_HALF:
            lands = _swap_gathered_halves(lands, "swap_gathered_" + group)
        return list(zip(lands, owns))

    pending, last_token = {}, []

    def grads_ready(group, grads, small=None):
        names = list(grads)
        kinds, sources = [REDUCE], [[grads[n][1] for n in names]]
        if small is not None:
            kinds, sources = kinds + [BROADCAST], sources + [[_to_slab(small)]]
        handles, token = _exchange_start(kinds, sources, "reduce_start_" + group)
        handle = handles[0]
        if small is not None:
            pending["small"] = handles[1]
        pending[group] = (names, handle, [grads[n][0] for n in names])
        last_token[:] = [token]
        return [token]

    small = {n: (weights[n] if weights[n].ndim > 1 else weights[n].reshape(1, -1)) for n in SMALL_NAMES}
    dx = _local_step(x[0], positions.reshape(S, 1), loss_target[0], small, shard, weights_of, grads_ready)

    grad, delta, new_m, new_v = {}, {}, {}, {}

    def finish(groups, after):
        names, partials = [], []
        for group in groups:
            group_names, handle, own32 = pending[group]
            _, received = _exchange_wait(REDUCE, handle, after, "reduce_wait_" + group)
            names += group_names
            partials += [_chip_partial(g32, rec, shard, "chip_partial_" + n)
                         for n, g32, rec in zip(group_names, own32, received)]
        siblings = _swap_with_sibling(partials, "swap_" + groups[0])
        for name, mine, theirs in zip(names, partials, siblings):
            if name == "w_in":
                mine, theirs = mine.T, theirs.T
            grad[name], delta[name], new_m[name], new_v[name] = _adamw(
                weights[name], moments_m[name], moments_v[name], [mine, theirs], "adamw_" + name)
        return [grad[names[-1]]]

    after = finish(["ffn2"], last_token)
    after = finish(["mixer"], after)
    (own_slab,), (slab_landing,) = _exchange_wait(BROADCAST, pending["small"], after, "reduce_wait_small")
    device = (2 * shard + lax.axis_index("c")).astype(jnp.int32)
    small_like = dict({n: small[n] for n in SMALL_NAMES}, loss=jnp.zeros((8, _LANES), F32))
    slabs = _adamw(_to_slab(small), _to_slab({n: moments_m[n] for n in SMALL_NAMES}),
                   _to_slab({n: moments_v[n] for n in SMALL_NAMES}), [own_slab, slab_landing], "adamw_small",
                   slot=device)
    for store, slab in zip((grad, delta, new_m, new_v), slabs):
        store.update(_from_slab(slab, small_like))
    loss = grad["loss"][0, 0]
    after = finish(["ffn1_gu"], [slabs[0]])
    finish(["ffn1_down"], after)

    def out(store):
        return [store[n].reshape(given[n].shape) for n in WEIGHT_ORDER]
    return (loss, dx[None], *out(grad), *out(delta), *out(new_m), *out(new_v))
```

```python
import jax
import jax.numpy as jnp
from jax import lax
from jax.experimental import pallas as pl
from jax.experimental.pallas import tpu as pltpu

F32 = jnp.float32
BF16 = jnp.bfloat16

D_MODEL = 1024
D_FF = 2816
N_SHARD = 4
FF_CHUNK = D_FF // 2
POOL_WINDOWS = (2, 4, 8, 16)
POOL_WIDTH = 512
POOL_GROUP = 128
HALO = 16
HEAD_DIM = 64
N_HEADS = 8
N_KV_HEADS = 2
ATTN_WIDTH = 512
KV_WIDTH = 128
IN_WIDTH = 1280
BLOCK = 128
ATTN_BLOCKS = 4
ROT_DIM = 16
ROPE_THETA = 500000.0
EPS = 1e-6
NEG_INF = -1e30
ATTN_SCALE = HEAD_DIM ** -0.5

ADAM_LR = 0.001
ADAM_B1 = 0.9
ADAM_B2 = 0.999
ADAM_EPS = 1e-08
ADAM_WD = 0.01
ADAM_STEP = 10

VMEM_LIMIT = 60 * 1024 * 1024
FFN_UP_TILE = 512
FFN_DOWN_TILE = 1024
FFN_BWD_TILE = 256
MIXER_TILE = 512

MESH = pl.DeviceIdType.MESH


def _params(**kw):
    return pltpu.CompilerParams(vmem_limit_bytes=VMEM_LIMIT, **kw)


def _dot(a, b):
    return jnp.dot(a, b, preferred_element_type=F32)


def _dot_nt(a, b):
    return lax.dot_general(a, b, (((1,), (1,)), ((), ())), preferred_element_type=F32)


def _dot_tn(a, b):
    return lax.dot_general(a, b, (((0,), (0,)), ((), ())), preferred_element_type=F32)


def _rstd(x):
    return lax.rsqrt(jnp.mean(x * x, axis=-1, keepdims=True) + EPS)


def _norm_bwd(dy, xn, r, gain):
    dxn = dy * gain
    return r * (dxn - xn * jnp.mean(dxn * xn, axis=-1, keepdims=True))


def _sigmoid(x):
    return 1.0 / (1.0 + jnp.exp(-x))


def _full(shape):
    return pl.BlockSpec(shape, lambda *_: (0,) * len(shape))


def _rows(tile, width):
    return pl.BlockSpec((tile, width), lambda i: (i, 0))


_ANY = pl.BlockSpec(memory_space=pl.ANY)


_SMEM = pl.BlockSpec(memory_space=pltpu.SMEM)


def _load_once(pairs, sem):
    @pl.when(pl.program_id(0) == 0)
    def _():
        copies = [pltpu.make_async_copy(src, dst, sem.at[n]) for n, (src, dst) in enumerate(pairs)]
        for cp in copies:
            cp.start()
        for cp in copies:
            cp.wait()


def _gathered(land_ref, own_ref, vmem_ref, mine, rows=None):
    def dst(slot):
        if rows is None:
            return vmem_ref.at[slot]
        return vmem_ref.at[pl.ds(pl.multiple_of(slot * rows, 16), rows), :]
    pairs = [(land_ref.at[(mine + d) % N_SHARD], dst((mine + d) % N_SHARD)) for d in range(1, N_SHARD)]
    return pairs + [(own_ref, dst(mine))]


def _ignoring(body, start, count):
    def wrapped(*refs):
        return body(*refs[:start], *refs[start + count:])
    return wrapped


def _ffn_up(x, pre, wgu, mine, deps=()):
    S = x.shape[0]
    tm = FFN_UP_TILE

    def body(x_ref, pre_ref, wgu_land, wgu_own, mine_ref, g_ref, u_ref, a_ref, wgu_v, sem):
        _load_once(_gathered(wgu_land, wgu_own, wgu_v, mine_ref[0]), sem)
        xv = x_ref[...]
        h = ((xv * _rstd(xv)) * pre_ref[...]).astype(BF16)
        for c in range(2):
            cols = slice(c * FF_CHUNK, (c + 1) * FF_CHUNK)
            g = _dot(h, wgu_v[c])
            u = _dot(h, wgu_v[2 + c])
            g_ref[:, cols] = g.astype(BF16)
            u_ref[:, cols] = u.astype(BF16)
            a_ref[:, cols] = ((g * _sigmoid(g)) * u).astype(BF16)

    args = [x, pre, *wgu, mine]
    return pl.pallas_call(
        _ignoring(body, len(args), len(deps)), name="ffn_up", grid=(S // tm,),
        in_specs=[_rows(tm, D_MODEL), _full((1, D_MODEL)), _ANY, _ANY, _SMEM] + [_ANY] * len(deps),
        out_specs=[_rows(tm, D_FF)] * 3,
        out_shape=[jax.ShapeDtypeStruct((S, D_FF), BF16)] * 3,
        scratch_shapes=[pltpu.VMEM((N_SHARD, D_MODEL, FF_CHUNK), BF16), pltpu.SemaphoreType.DMA((N_SHARD,))],
        compiler_params=_params(dimension_semantics=("arbitrary",)),
    )(*args, *deps)


def _ffn_down(x, a, post, wd, mine, target=None, deps=()):
    S = x.shape[0]
    tm = FFN_DOWN_TILE
    with_loss = target is not None

    def body(*refs):
        if with_loss:
            (x_ref, a_ref, post_ref, wd_land, wd_own, mine_ref, tgt_ref,
             out_ref, f_ref, loss_ref, dpost_ref, wd_v, sem) = refs
        else:
            x_ref, a_ref, post_ref, wd_land, wd_own, mine_ref, out_ref, f_ref, wd_v, sem = refs
        _load_once(_gathered(wd_land, wd_own, wd_v, mine_ref[0], rows=D_FF // N_SHARD), sem)
        xv = x_ref[...]
        facc = _dot(a_ref[...], wd_v[...])
        rf = _rstd(facc)
        fn = facc * rf
        out = xv + 0.5 * (fn * post_ref[...])
        if with_loss:
            diff = out - tgt_ref[...]
            dout = diff * (1.0 / D_MODEL)
            out_ref[...] = dout

            @pl.when(pl.program_id(0) == 0)
            def _():
                loss_ref[...] = jnp.zeros_like(loss_ref)
                dpost_ref[...] = jnp.zeros_like(dpost_ref)
            loss_ref[...] += jnp.sum(diff * diff)
            dn = 0.5 * dout
            dpost_ref[...] += jnp.sum(dn * fn, axis=0, keepdims=True)
            f_ref[...] = _norm_bwd(dn, fn, rf, post_ref[...]).astype(BF16)
        else:
            f_ref[...] = facc
            out_ref[...] = out

    in_specs = [_rows(tm, D_MODEL), _rows(tm, D_FF), _full((1, D_MODEL)), _ANY, _ANY, _SMEM]
    args = [x, a, post, *wd, mine]
    out_shape = [jax.ShapeDtypeStruct((S, D_MODEL), F32), jax.ShapeDtypeStruct((S, D_MODEL), F32)]
    out_specs = [_rows(tm, D_MODEL), _rows(tm, D_MODEL)]
    if with_loss:
        in_specs.append(_rows(tm, D_MODEL))
        args.append(target)
        out_shape[1] = jax.ShapeDtypeStruct((S, D_MODEL), BF16)
        out_shape += [jax.ShapeDtypeStruct((8, 128), F32), jax.ShapeDtypeStruct((1, D_MODEL), F32)]
        out_specs += [_full((8, 128)), _full((1, D_MODEL))]
    return pl.pallas_call(
        _ignoring(body, len(args), len(deps)), name="ffn_down_loss" if with_loss else "ffn_down",
        grid=(S // tm,), in_specs=in_specs + [_ANY] * len(deps), out_specs=out_specs, out_shape=out_shape,
        scratch_shapes=[pltpu.VMEM((D_FF, D_MODEL), BF16), pltpu.SemaphoreType.DMA((N_SHARD,))],
        compiler_params=_params(dimension_semantics=("arbitrary",)),
    )(*args, *deps)


def _ffn_bwd(dout, x, f, g, u, pre, post, wgu, wd, mine, df_known=False, deps=()):
    S = x.shape[0]
    tm = FFN_BWD_TILE

    def body(*refs):
        if df_known:
            (dout_ref, x_ref, f_ref, g_ref, u_ref, pre_ref, post_ref, wgu_land, wgu_own, wd_land, wd_own, mine_ref,
             dx_ref, h_ref, dgu_ref, dpre_ref, wgu_v, wd_v, sem) = refs
        else:
            (dout_ref, x_ref, f_ref, g_ref, u_ref, pre_ref, post_ref, wgu_land, wgu_own, wd_land, wd_own, mine_ref,
             dx_ref, h_ref, dgu_ref, df_ref, dpre_ref, dpost_ref, wgu_v, wd_v, sem) = refs
        _load_once(_gathered(wgu_land, wgu_own, wgu_v, mine_ref[0])
                   + _gathered(wd_land, wd_own, wd_v, mine_ref[0], rows=D_FF // N_SHARD), sem)

        @pl.when(pl.program_id(0) == 0)
        def _():
            dpre_ref[...] = jnp.zeros_like(dpre_ref)
            if not df_known:
                dpost_ref[...] = jnp.zeros_like(dpost_ref)

        dout_v = dout_ref[...]
        if df_known:
            df = f_ref[...]
        else:
            dn = 0.5 * dout_v
            fv = f_ref[...]
            rf = _rstd(fv)
            fn = fv * rf
            dpost_ref[...] += jnp.sum(dn * fn, axis=0, keepdims=True)
            df = _norm_bwd(dn, fn, rf, post_ref[...]).astype(BF16)
            df_ref[...] = df
        dh = jnp.zeros((tm, D_MODEL), F32)
        for c in range(2):
            cols = slice(c * FF_CHUNK, (c + 1) * FF_CHUNK)
            da = _dot_nt(df, wd_v[cols, :])
            gv = g_ref[:, cols].astype(F32)
            uv = u_ref[:, cols].astype(F32)
            sg = _sigmoid(gv)
            silu = gv * sg
            dg = ((da * uv) * (sg * (1.0 + gv * (1.0 - sg)))).astype(BF16)
            du = (da * silu).astype(BF16)
            dgu_ref[:, cols] = dg
            dgu_ref[:, 2 * FF_CHUNK + c * FF_CHUNK:2 * FF_CHUNK + (c + 1) * FF_CHUNK] = du
            dh = dh + _dot_nt(dg, wgu_v[c]) + _dot_nt(du, wgu_v[2 + c])
        xv = x_ref[...]
        rx = _rstd(xv)
        xn = xv * rx
        h_ref[...] = (xn * pre_ref[...]).astype(BF16)
        dpre_ref[...] += jnp.sum(dh * xn, axis=0, keepdims=True)
        dx_ref[...] = dout_v + _norm_bwd(dh, xn, rx, pre_ref[...])

    args = [dout, x, f, g, u, pre, post, *wgu, *wd, mine]
    out_specs = [_rows(tm, D_MODEL), _rows(tm, D_MODEL), _rows(tm, 2 * D_FF), _rows(tm, D_MODEL),
                 _full((1, D_MODEL)), _full((1, D_MODEL))]
    out_shape = [jax.ShapeDtypeStruct((S, D_MODEL), F32), jax.ShapeDtypeStruct((S, D_MODEL), BF16),
                 jax.ShapeDtypeStruct((S, 2 * D_FF), BF16), jax.ShapeDtypeStruct((S, D_MODEL), BF16),
                 jax.ShapeDtypeStruct((1, D_MODEL), F32), jax.ShapeDtypeStruct((1, D_MODEL), F32)]
    if df_known:
        out_specs = out_specs[:3] + out_specs[4:5]
        out_shape = out_shape[:3] + out_shape[4:5]
    return pl.pallas_call(
        _ignoring(body, len(args), len(deps)), name="ffn_bwd_from_df" if df_known else "ffn_bwd", grid=(S // tm,),
        in_specs=[_rows(tm, D_MODEL), _rows(tm, D_MODEL), _rows(tm, D_MODEL), _rows(tm, D_FF), _rows(tm, D_FF),
                  _full((1, D_MODEL)), _full((1, D_MODEL)), _ANY, _ANY, _ANY, _ANY, _SMEM] + [_ANY] * len(deps),
        out_specs=out_specs, out_shape=out_shape,
        scratch_shapes=[pltpu.VMEM((N_SHARD, D_MODEL, FF_CHUNK), BF16), pltpu.VMEM((D_FF, D_MODEL), BF16),
                        pltpu.SemaphoreType.DMA((2 * N_SHARD,))],
        compiler_params=_params(dimension_semantics=("arbitrary",)),
    )(*args, *deps)


def _wgrad(lhs, rhs, m_block, n_block, name, column_shards=False, tk=2048, deps=()):
    S, M = lhs.shape
    N = rhs.shape[1]
    k_steps = S // tk

    def body(lhs_ref, rhs_ref, out_ref, out16_ref):
        k = pl.program_id(2)

        @pl.when(k == 0)
        def _():
            out_ref[...] = jnp.zeros_like(out_ref)
        out_ref[...] += _dot_tn(lhs_ref[...], rhs_ref[...])

        @pl.when(k == k_steps - 1)
        def _():
            out16_ref[...] = out_ref[...].astype(BF16)

    if column_shards:
        assert N == N_SHARD * n_block
        shape = (N_SHARD, M, n_block)
        out_spec = pl.BlockSpec((None, m_block, n_block), lambda i, j, k: (j, i, 0))
    else:
        shape = (M, N)
        out_spec = pl.BlockSpec((m_block, n_block), lambda i, j, k: (i, j))
    out, out16 = pl.pallas_call(
        _ignoring(body, 2, len(deps)), name=name, grid=(M // m_block, N // n_block, k_steps),
        in_specs=[pl.BlockSpec((tk, m_block), lambda i, j, k: (k, i)),
                  pl.BlockSpec((tk, n_block), lambda i, j, k: (k, j))] + [_ANY] * len(deps),
        out_specs=[out_spec, out_spec],
        out_shape=[jax.ShapeDtypeStruct(shape, F32), jax.ShapeDtypeStruct(shape, BF16)],
        compiler_params=_params(dimension_semantics=("arbitrary", "arbitrary", "arbitrary")),
    )(lhs, rhs, *deps)
    if not column_shards:
        out = out.reshape(N_SHARD, M // N_SHARD, N)
        out16 = out16.reshape(N_SHARD, M // N_SHARD, N)
    return out, out16


def _rope_tables(pos, invf):
    S = pos.shape[0]
    tm = MIXER_TILE

    def body(pos_ref, invf_ref, out_ref):
        ang = pos_ref[...].astype(F32) * invf_ref[...]
        cos, sin = jnp.cos(ang), jnp.sin(ang)
        lane = lax.broadcasted_iota(jnp.int32, ang.shape, 1) % HEAD_DIM
        first = lane < ROT_DIM // 2
        second = (lane >= ROT_DIM // 2) & (lane < ROT_DIM)
        out_ref[0] = jnp.where(lane < ROT_DIM, cos, 1.0)
        out_ref[1] = jnp.where(first, sin, 0.0)
        out_ref[2] = jnp.where(second, sin, 0.0)

    return pl.pallas_call(
        body, name="rope_tables", grid=(S // tm,),
        in_specs=[_rows(tm, 1), _full((1, _LANES))],
        out_specs=pl.BlockSpec((3, tm, _LANES), lambda i: (0, i, 0)),
        out_shape=jax.ShapeDtypeStruct((3, S, _LANES), F32),
        compiler_params=_params(dimension_semantics=("arbitrary",)),
    )(pos, invf)


def _table_spec(tm):
    return pl.BlockSpec((3, tm, _LANES), lambda i: (0, i, 0))


_HALF = ROT_DIM // 2
_LANES = 128


def _rope(t, tables):
    c, s_first, s_second = tables
    return t * c - pltpu.roll(t, _LANES - _HALF, axis=1) * s_first + pltpu.roll(t, _HALF, axis=1) * s_second


def _rope_transposed(t, tables):
    c, s_first, s_second = tables
    return t * c - pltpu.roll(t * s_first, _HALF, axis=1) + pltpu.roll(t * s_second, _LANES - _HALF, axis=1)


def _store_head_variants(ref, t):
    rolled = pltpu.roll(t, HEAD_DIM, axis=1)
    low = lax.broadcasted_iota(jnp.int32, t.shape, 1) < HEAD_DIM
    zero = jnp.zeros_like(t)
    ref[0] = jnp.where(low, t, zero).astype(BF16)
    ref[1] = jnp.where(low, zero, rolled).astype(BF16)
    ref[2] = jnp.where(low, rolled, zero).astype(BF16)
    ref[3] = jnp.where(low, zero, t).astype(BF16)


def _mixer_in_fwd(x, pre, w_in_t, mine, rope, deps=()):
    S = x.shape[0]
    tm = MIXER_TILE

    def body(x_ref, pre_ref, w_land, w_own, mine_ref, rope_ref, u_ref, q_ref, k_ref, v_ref, w_v, sem):
        _load_once(_gathered(w_land, w_own, w_v, mine_ref[0], rows=IN_WIDTH // N_SHARD), sem)
        xv = x_ref[...]
        h = ((xv * _rstd(xv)) * pre_ref[...]).astype(BF16)
        z = _dot_nt(h, w_v[...])
        tables = (rope_ref[0], rope_ref[1], rope_ref[2])
        u_ref[...] = z[:, :POOL_WIDTH]
        for t in range(ATTN_WIDTH // _LANES):
            lo = POOL_WIDTH + t * _LANES
            q_ref[:, t * _LANES:(t + 1) * _LANES] = (_rope(z[:, lo:lo + _LANES], tables) * ATTN_SCALE).astype(BF16)
        kv = POOL_WIDTH + ATTN_WIDTH
        _store_head_variants(k_ref, _rope(z[:, kv:kv + KV_WIDTH], tables))
        _store_head_variants(v_ref, z[:, kv + KV_WIDTH:])

    args = [x, pre, *w_in_t, mine, rope]
    variants = pl.BlockSpec((2 * N_KV_HEADS, tm, KV_WIDTH), lambda i: (0, i, 0))
    return pl.pallas_call(
        _ignoring(body, len(args), len(deps)), name="mixer_in_fwd", grid=(S // tm,),
        in_specs=[_rows(tm, D_MODEL), _full((1, D_MODEL)), _ANY, _ANY, _SMEM, _table_spec(tm)] + [_ANY] * len(deps),
        out_specs=[_rows(tm, POOL_WIDTH), _rows(tm, ATTN_WIDTH), variants, variants],
        out_shape=[jax.ShapeDtypeStruct((S, POOL_WIDTH), F32), jax.ShapeDtypeStruct((S, ATTN_WIDTH), BF16),
                   jax.ShapeDtypeStruct((2 * N_KV_HEADS, S, KV_WIDTH), BF16),
                   jax.ShapeDtypeStruct((2 * N_KV_HEADS, S, KV_WIDTH), BF16)],
        scratch_shapes=[pltpu.VMEM((IN_WIDTH, D_MODEL), BF16), pltpu.SemaphoreType.DMA((N_SHARD,))],
        compiler_params=_params(dimension_semantics=("arbitrary",)),
    )(*args, *deps)


def _mixer_in_bwd(dres, x, pre, w_in_t, mine, du, dq, dk, dv, dk_next, dv_next, rope, deps=()):
    S = x.shape[0]
    tm = MIXER_TILE

    def body(dres_ref, x_ref, pre_ref, w_land, w_own, mine_ref, du_ref, dq_ref, dk_ref, dv_ref, dkx_ref, dvx_ref,
             rope_ref,
             dx_ref, dz_ref, h_ref, dpre_ref, w_v, sem):
        _load_once(_gathered(w_land, w_own, w_v, mine_ref[0], rows=IN_WIDTH // N_SHARD), sem)

        @pl.when(pl.program_id(0) == 0)
        def _():
            dpre_ref[...] = jnp.zeros_like(dpre_ref)

        tables = (rope_ref[0], rope_ref[1], rope_ref[2])
        dz_ref[:, :POOL_WIDTH] = du_ref[...]
        for t in range(ATTN_WIDTH // _LANES):
            lo = POOL_WIDTH + t * _LANES
            dz_ref[:, lo:lo + _LANES] = _rope_transposed(dq_ref[:, t * _LANES:(t + 1) * _LANES], tables).astype(BF16)
        kv = POOL_WIDTH + ATTN_WIDTH
        has_next = pl.program_id(0) + 1 < steps
        pad = jnp.zeros((tm - BLOCK, KV_WIDTH), F32)
        dk_tile = dk_ref[...] + jnp.concatenate([pad, jnp.where(has_next, dkx_ref[...], 0.0)], axis=0)
        dv_tile = dv_ref[...] + jnp.concatenate([pad, jnp.where(has_next, dvx_ref[...], 0.0)], axis=0)
        dz_ref[:, kv:kv + KV_WIDTH] = _rope_transposed(dk_tile, tables).astype(BF16)
        dz_ref[:, kv + KV_WIDTH:] = dv_tile.astype(BF16)
        dh = _dot(dz_ref[...], w_v[...])
        xv = x_ref[...]
        rx = _rstd(xv)
        xn = xv * rx
        h_ref[...] = (xn * pre_ref[...]).astype(BF16)
        dpre_ref[...] += jnp.sum(dh * xn, axis=0, keepdims=True)
        dx_ref[...] = dres_ref[...] + _norm_bwd(dh, xn, rx, pre_ref[...])

    assert tm == ATTN_BLOCKS * BLOCK
    steps = S // tm
    nxt = pl.BlockSpec((None, BLOCK, KV_WIDTH), lambda i: (jnp.minimum(i + 1, steps - 1), 0, 0))
    args = [dres, x, pre, *w_in_t, mine, du, dq, dk, dv, dk_next, dv_next, rope]
    return pl.pallas_call(
        _ignoring(body, len(args), len(deps)), name="mixer_in_bwd", grid=(S // tm,),
        in_specs=[_rows(tm, D_MODEL), _rows(tm, D_MODEL), _full((1, D_MODEL)), _ANY, _ANY, _SMEM,
                  _rows(tm, POOL_WIDTH), _rows(tm, ATTN_WIDTH), _rows(tm, KV_WIDTH), _rows(tm, KV_WIDTH), nxt, nxt,
                  _table_spec(tm)] + [_ANY] * len(deps),
        out_specs=[_rows(tm, D_MODEL), _rows(tm, IN_WIDTH), _rows(tm, D_MODEL), _full((1, D_MODEL))],
        out_shape=[jax.ShapeDtypeStruct((S, D_MODEL), F32), jax.ShapeDtypeStruct((S, IN_WIDTH), BF16),
                   jax.ShapeDtypeStruct((S, D_MODEL), BF16), jax.ShapeDtypeStruct((1, D_MODEL), F32)],
        scratch_shapes=[pltpu.VMEM((IN_WIDTH, D_MODEL), BF16), pltpu.SemaphoreType.DMA((N_SHARD,))],
        compiler_params=_params(dimension_semantics=("arbitrary",)),
    )(*args, *deps)


def _pool_counts(tile_index, tm, width):
    t = tile_index * tm + lax.broadcasted_iota(jnp.int32, (tm, 1), 0)
    return jnp.minimum(t + 1, width).astype(F32)


def _pool_features(ext, u_tile, tile_index, tm):
    ds = []
    for gi, width in enumerate(POOL_WINDOWS):
        lanes = slice(gi * POOL_GROUP, (gi + 1) * POOL_GROUP)
        s = ext[:, lanes]
        shift = 1
        while shift < width:
            s = s + pltpu.roll(s, shift, axis=0)
            shift *= 2
        ds.append(s[HALO:, :] / _pool_counts(tile_index, tm, width) - u_tile[:, lanes])
    return ds


def _pool_fwd(u, w_pool, pool_scale, g_pool):
    S = u.shape[0]
    tm = MIXER_TILE

    def body(u_ref, w_ref, scale_ref, gain_ref, y_ref, ext_ref):
        i = pl.program_id(0)

        @pl.when(i == 0)
        def _():
            ext_ref[:HALO, :] = jnp.zeros((HALO, POOL_WIDTH), F32)

        u_tile = u_ref[...]
        ext_ref[HALO:, :] = u_tile
        ds = _pool_features(ext_ref[...], u_tile, i, tm)
        ext_ref[:HALO, :] = u_tile[tm - HALO:, :]
        ys = [_dot(ds[gi].astype(BF16), w_ref[gi].astype(BF16)) for gi in range(len(POOL_WINDOWS))]
        po = jnp.concatenate(ys, axis=1) * scale_ref[...]
        y_ref[...] = ((po * _rstd(po)) * gain_ref[...]).astype(BF16)

    return pl.pallas_call(
        body, name="pool_fwd", grid=(S // tm,),
        in_specs=[_rows(tm, POOL_WIDTH), _full((len(POOL_WINDOWS), POOL_GROUP, POOL_GROUP)),
                  _full((1, POOL_WIDTH)), _full((1, POOL_WIDTH))],
        out_specs=_rows(tm, POOL_WIDTH),
        out_shape=jax.ShapeDtypeStruct((S, POOL_WIDTH), BF16),
        scratch_shapes=[pltpu.VMEM((HALO + tm, POOL_WIDTH), F32)],
        compiler_params=_params(dimension_semantics=("arbitrary",)),
    )(u, w_pool, pool_scale, g_pool)


def _pool_bwd(dy, u, w_pool, pool_scale, g_pool):
    S = u.shape[0]
    tm = MIXER_TILE
    n_tiles = S // tm
    halo_blocks = tm // HALO

    def body(dy_ref, u_ref, uprev_ref, w_ref, scale_ref, gain_ref,
             du_ref, dw_ref, dscale_ref, dgain_ref, ext_ref, nxt_ref):
        i = pl.program_id(0)
        tile = n_tiles - 1 - i

        @pl.when(i == 0)
        def _():
            dw_ref[...] = jnp.zeros_like(dw_ref)
            dscale_ref[...] = jnp.zeros_like(dscale_ref)
            dgain_ref[...] = jnp.zeros_like(dgain_ref)
            nxt_ref[...] = jnp.zeros_like(nxt_ref)

        u_tile = u_ref[...]
        ext_ref[:HALO, :] = jnp.where(tile > 0, uprev_ref[...], 0.0)
        ext_ref[HALO:, :] = u_tile
        ds = _pool_features(ext_ref[...], u_tile, tile, tm)
        dsb = [d.astype(BF16) for d in ds]
        wb = [w_ref[gi].astype(BF16) for gi in range(len(POOL_WINDOWS))]
        yraw = jnp.concatenate([_dot(dsb[gi], wb[gi]) for gi in range(len(POOL_WINDOWS))], axis=1)
        po = yraw * scale_ref[...]
        r = _rstd(po)
        pn = po * r
        dyv = dy_ref[...]
        dgain_ref[...] += jnp.sum(dyv * pn, axis=0, keepdims=True)
        dpo = _norm_bwd(dyv, pn, r, gain_ref[...])
        dscale_ref[...] += jnp.sum(dpo * yraw, axis=0, keepdims=True)
        dyraw = (dpo * scale_ref[...]).astype(BF16)
        for gi, width in enumerate(POOL_WINDOWS):
            lanes = slice(gi * POOL_GROUP, (gi + 1) * POOL_GROUP)
            dw_ref[gi] += _dot_tn(dsb[gi], dyraw[:, lanes])
            dd = _dot_nt(dyraw[:, lanes], wb[gi])
            ddc = dd / _pool_counts(tile, tm, width)
            ext_ref[:tm, lanes] = ddc
            ext_ref[tm:, lanes] = nxt_ref[:, lanes]
            s = ext_ref[:, lanes]
            shift = 1
            while shift < width:
                s = s + pltpu.roll(s, HALO + tm - shift, axis=0)
                shift *= 2
            du_ref[:, lanes] = (s[:tm, :] - dd).astype(BF16)
            nxt_ref[:, lanes] = ddc[:HALO, :]

    return pl.pallas_call(
        body, name="pool_bwd", grid=(n_tiles,),
        in_specs=[pl.BlockSpec((tm, POOL_WIDTH), lambda i: (n_tiles - 1 - i, 0)),
                  pl.BlockSpec((tm, POOL_WIDTH), lambda i: (n_tiles - 1 - i, 0)),
                  pl.BlockSpec((HALO, POOL_WIDTH), lambda i: (jnp.maximum((n_tiles - 1 - i) * halo_blocks - 1, 0), 0)),
                  _full((len(POOL_WINDOWS), POOL_GROUP, POOL_GROUP)), _full((1, POOL_WIDTH)), _full((1, POOL_WIDTH))],
        out_specs=[pl.BlockSpec((tm, POOL_WIDTH), lambda i: (n_tiles - 1 - i, 0)),
                   _full((len(POOL_WINDOWS), POOL_GROUP, POOL_GROUP)), _full((1, POOL_WIDTH)), _full((1, POOL_WIDTH))],
        out_shape=[jax.ShapeDtypeStruct((S, POOL_WIDTH), BF16),
                   jax.ShapeDtypeStruct((len(POOL_WINDOWS), POOL_GROUP, POOL_GROUP), F32),
                   jax.ShapeDtypeStruct((1, POOL_WIDTH), F32), jax.ShapeDtypeStruct((1, POOL_WIDTH), F32)],
        scratch_shapes=[pltpu.VMEM((HALO + tm, POOL_WIDTH), F32), pltpu.VMEM((HALO, POOL_WIDTH), F32)],
        compiler_params=_params(dimension_semantics=("arbitrary",)),
    )(dy, u, u, w_pool, pool_scale, g_pool)


def _variant(head):
    return 2 * (head // (N_HEADS // N_KV_HEADS)) + head % 2


def _own_block(shape=(BLOCK, BLOCK)):
    r = lax.broadcasted_iota(jnp.int32, shape, 0)
    i = lax.broadcasted_iota(jnp.int32, shape, 1)
    return r <= i


def _fold_band(own, from_own, from_prev):
    return jnp.where(own, from_own, from_prev)


def _scores_by_head(own_tiles, prev_tiles, q_tiles):
    stacks = [jnp.concatenate(q_tiles[:2], axis=0), jnp.concatenate(q_tiles[2:], axis=0)]
    by_var = [_dot_nt(jnp.concatenate([own_tiles[v], prev_tiles[v]], axis=0), stacks[v // 2])
              for v in range(2 * N_KV_HEADS)]
    quadrant = lambda h, rows: by_var[_variant(h)][rows * BLOCK:(rows + 1) * BLOCK,
                                                   ((h // 2) % 2) * BLOCK:((h // 2) % 2 + 1) * BLOCK]
    return [quadrant(h, 0) for h in range(N_HEADS)], [quadrant(h, 1) for h in range(N_HEADS)]


def _softmax_t(s, sink):
    m = jnp.maximum(jnp.max(s, axis=0, keepdims=True), sink)
    p = jnp.exp(s - m)
    p_sink = jnp.exp(sink - m)
    inv = 1.0 / (jnp.sum(p, axis=0, keepdims=True) + p_sink)
    return p * inv, p_sink * inv


def _attn_fwd(q, kz, vz, sinks, g_attn, y_pool, x, w_out, mine, post):
    S = q.shape[0]
    tq = ATTN_BLOCKS * BLOCK
    n_var = 2 * N_KV_HEADS

    def body(q_ref, kp_ref, kc_ref, vp_ref, vc_ref, sinks_ref, gain_ref, yp_ref, x_ref, w_land, w_own, mine_ref,
             post_ref, o_ref, out_ref, m_ref, y_ref, w_v, sem):
        _load_once(_gathered(w_land, w_own, w_v, mine_ref[0], rows=D_MODEL // N_SHARD), sem)
        step = pl.program_id(0)
        own = _own_block()
        zero = jnp.zeros((BLOCK, BLOCK), F32)

        def tiles(cur_ref, prev_ref, j):
            rows = lambda jj: slice(jj * BLOCK, (jj + 1) * BLOCK)
            return ([cur_ref[v, rows(j), :] for v in range(n_var)],
                    [prev_ref[v] if j == 0 else cur_ref[v, rows(j - 1), :] for v in range(n_var)])

        scores = []
        for j in range(ATTN_BLOCKS):
            q_pairs = [q_ref[j * BLOCK:(j + 1) * BLOCK, i * _LANES:(i + 1) * _LANES] for i in range(N_HEADS // 2)]
            scores.append(_scores_by_head(*tiles(kc_ref, kp_ref, j), q_pairs))
        probs = []
        for j in range(ATTN_BLOCKS):
            s_own, s_prev = scores[j]
            no_prev = jnp.where(step > 0, 0.0, NEG_INF) if j == 0 else 0.0
            p_own, p_prev = [], []
            for h in range(N_HEADS):
                p, _ = _softmax_t(_fold_band(own, s_own[h], s_prev[h] + no_prev), sinks_ref[0, h])
                p_own.append(jnp.where(own, p, zero).astype(BF16))
                p_prev.append(jnp.where(own, zero, p).astype(BF16))
            probs.append((p_own, p_prev))
        blocks = []
        for j in range(ATTN_BLOCKS):
            p_own, p_prev = probs[j]
            v_own, v_prev = tiles(vc_ref, vp_ref, j)
            pairs = []
            for i in range(N_HEADS // 2):
                acc = None
                for h in (2 * i, 2 * i + 1):
                    part = _dot_tn(p_own[h], v_own[_variant(h)]) + _dot_tn(p_prev[h], v_prev[_variant(h)])
                    acc = part if acc is None else acc + part
                pairs.append(acc)
            blocks.append(jnp.concatenate(pairs, axis=1))
        o = jnp.concatenate(blocks, axis=0)
        o_ref[...] = o
        y_ref[:, :POOL_WIDTH] = yp_ref[...]
        y_ref[:, POOL_WIDTH:] = ((o * _rstd(o)) * gain_ref[...]).astype(BF16)
        m = _dot(y_ref[...], w_v[...])
        m_ref[...] = m
        out_ref[...] = x_ref[...] + (m * _rstd(m)) * post_ref[...]

    prev = pl.BlockSpec((n_var, BLOCK, KV_WIDTH), lambda g: (0, jnp.maximum(g * ATTN_BLOCKS - 1, 0), 0))
    cur = pl.BlockSpec((n_var, tq, KV_WIDTH), lambda g: (0, g, 0))
    return pl.pallas_call(
        body, name="attn_fwd", grid=(S // tq,),
        in_specs=[_rows(tq, ATTN_WIDTH), prev, cur, prev, cur,
                  pl.BlockSpec(memory_space=pltpu.SMEM), _full((1, ATTN_WIDTH)),
                  _rows(tq, POOL_WIDTH), _rows(tq, D_MODEL), _ANY, _ANY, _SMEM, _full((1, D_MODEL))],
        out_specs=[_rows(tq, ATTN_WIDTH), _rows(tq, D_MODEL), _rows(tq, D_MODEL), _rows(tq, D_MODEL)],
        out_shape=[jax.ShapeDtypeStruct((S, ATTN_WIDTH), F32), jax.ShapeDtypeStruct((S, D_MODEL), F32),
                   jax.ShapeDtypeStruct((S, D_MODEL), F32), jax.ShapeDtypeStruct((S, D_MODEL), BF16)],
        scratch_shapes=[pltpu.VMEM((D_MODEL, D_MODEL), BF16), pltpu.SemaphoreType.DMA((N_SHARD,))],
        compiler_params=_params(dimension_semantics=("arbitrary",)),
    )(q, kz, kz, vz, vz, sinks, g_attn, y_pool, x, *w_out, mine, post)


def _attn_bwd(dout, m, w_out, mine, post, o, q, kz, vz, sinks, g_attn, deps=()):
    S = q.shape[0]
    tq = ATTN_BLOCKS * BLOCK
    n_var = 2 * N_KV_HEADS

    def body(dout_ref, m_ref, w_land, w_own, mine_ref, post_ref, o_ref, q_ref, kp_ref, kc_ref, vp_ref, vc_ref,
             sinks_ref, gain_ref, dyp_ref, dm_ref, dpost_ref, dq_ref, dk_ref, dv_ref, dkx_ref, dvx_ref, dsink_ref,
             dgain_ref, w_v, sem):
        _load_once(_gathered(w_land, w_own, w_v, mine_ref[0], rows=D_MODEL // N_SHARD), sem)
        step = pl.program_id(0)

        @pl.when(step == 0)
        def _():
            dpost_ref[...] = jnp.zeros_like(dpost_ref)
            dsink_ref[...] = jnp.zeros_like(dsink_ref)
            dgain_ref[...] = jnp.zeros_like(dgain_ref)

        mv = m_ref[...]
        rm = _rstd(mv)
        mn = mv * rm
        dres = dout_ref[...]
        dpost_ref[...] += jnp.sum(dres * mn, axis=0, keepdims=True)
        dm = _norm_bwd(dres, mn, rm, post_ref[...]).astype(BF16)
        dm_ref[...] = dm
        dy = _dot_nt(dm, w_v[...])
        dyp_ref[...] = dy[:, :POOL_WIDTH]
        ov = o_ref[...]
        r = _rstd(ov)
        on = ov * r
        dyv = dy[:, POOL_WIDTH:]
        dgain_ref[...] += jnp.sum(dyv * on, axis=0, keepdims=True)
        do = _norm_bwd(dyv, on, r, gain_ref[...]).astype(BF16)
        own = _own_block()
        zero = jnp.zeros((BLOCK, BLOCK), F32)
        split = lambda t: (jnp.where(own, t, zero).astype(BF16), jnp.where(own, zero, t).astype(BF16))
        rows = lambda j: slice(j * BLOCK, (j + 1) * BLOCK)
        heads = range(N_HEADS)

        def tiles(cur_ref, prev_ref, j):
            return ([cur_ref[v, rows(j), :] for v in range(n_var)],
                    [prev_ref[v] if j == 0 else cur_ref[v, rows(j - 1), :] for v in range(n_var)])

        q_pairs = [[q_ref[rows(j), i * _LANES:(i + 1) * _LANES] for i in range(N_HEADS // 2)] for j in range(ATTN_BLOCKS)]
        do_pairs = [[do[rows(j), i * _LANES:(i + 1) * _LANES] for i in range(N_HEADS // 2)] for j in range(ATTN_BLOCKS)]
        scores = [(_scores_by_head(*tiles(kc_ref, kp_ref, j), q_pairs[j]),
                   _scores_by_head(*tiles(vc_ref, vp_ref, j), do_pairs[j])) for j in range(ATTN_BLOCKS)]
        parts, sink_sum = [], None
        for j in range(ATTN_BLOCKS):
            (s_own, s_prev), (dp_own, dp_prev) = scores[j]
            no_prev = jnp.where(step > 0, 0.0, NEG_INF) if j == 0 else 0.0
            ds_parts, p_parts, sink_rows = [], [], []
            for h in heads:
                p, p_sink = _softmax_t(_fold_band(own, s_own[h], s_prev[h] + no_prev), sinks_ref[0, h])
                dp = _fold_band(own, dp_own[h], dp_prev[h])
                delta = jnp.sum(p * dp, axis=0, keepdims=True)
                ds_parts.append(split(p * (dp - delta)))
                p_parts.append(split(p))
                sink_rows.append(jnp.zeros((1, _LANES), F32) - jnp.sum(p_sink * delta))
            block_sinks = jnp.concatenate(sink_rows, axis=0)
            sink_sum = block_sinks if sink_sum is None else sink_sum + block_sinks
            parts.append((ds_parts, p_parts))
        dsink_ref[...] += sink_sum
        low = lax.broadcasted_iota(jnp.int32, (BLOCK, _LANES), 1) < HEAD_DIM

        def merge(acc):
            return jnp.where(low, acc[0] + pltpu.roll(acc[1], HEAD_DIM, axis=1),
                             acc[3] + pltpu.roll(acc[2], HEAD_DIM, axis=1))
        add = lambda acc, var, t: acc.__setitem__(var, t if acc[var] is None else acc[var] + t)
        k_own, k_prev, v_own, v_prev = [], [], [], []
        for j in range(ATTN_BLOCKS):
            ds_parts, p_parts = parts[j]
            kt_own, kt_prev = tiles(kc_ref, kp_ref, j)
            dk_own, dk_prev, dv_own, dv_prev = ([None] * n_var for _ in range(4))
            for i in range(N_HEADS // 2):
                dq_pair = None
                for h in (2 * i, 2 * i + 1):
                    var = _variant(h)
                    (ds_o, ds_p), (p_o, p_p) = ds_parts[h], p_parts[h]
                    part = _dot_tn(ds_o, kt_own[var]) + _dot_tn(ds_p, kt_prev[var])
                    dq_pair = part if dq_pair is None else dq_pair + part
                    add(dk_own, var, _dot(ds_o, q_pairs[j][i]))
                    add(dk_prev, var, _dot(ds_p, q_pairs[j][i]))
                    add(dv_own, var, _dot(p_o, do_pairs[j][i]))
                    add(dv_prev, var, _dot(p_p, do_pairs[j][i]))
                dq_ref[rows(j), i * _LANES:(i + 1) * _LANES] = dq_pair * ATTN_SCALE
            k_own.append(merge(dk_own))
            k_prev.append(merge(dk_prev))
            v_own.append(merge(dv_own))
            v_prev.append(merge(dv_prev))
        for j in range(ATTN_BLOCKS):
            last = j == ATTN_BLOCKS - 1
            dk_ref[rows(j), :] = k_own[j] if last else k_own[j] + k_prev[j + 1]
            dv_ref[rows(j), :] = v_own[j] if last else v_own[j] + v_prev[j + 1]
        dkx_ref[...] = k_prev[0]
        dvx_ref[...] = v_prev[0]

    steps = S // tq
    prev = pl.BlockSpec((n_var, BLOCK, KV_WIDTH), lambda g: (0, jnp.maximum(g * ATTN_BLOCKS - 1, 0), 0))
    cur = pl.BlockSpec((n_var, tq, KV_WIDTH), lambda g: (0, g, 0))
    nxt = pl.BlockSpec((None, BLOCK, KV_WIDTH), lambda g: (g, 0, 0))
    args = [dout, m, *w_out, mine, post, o, q, kz, kz, vz, vz, sinks, g_attn]
    return pl.pallas_call(
        _ignoring(body, len(args), len(deps)), name="attn_bwd", grid=(steps,),
        in_specs=[_rows(tq, D_MODEL), _rows(tq, D_MODEL), _ANY, _ANY, _SMEM, _full((1, D_MODEL)),
                  _rows(tq, ATTN_WIDTH), _rows(tq, ATTN_WIDTH), prev, cur, prev, cur,
                  pl.BlockSpec(memory_space=pltpu.SMEM), _full((1, ATTN_WIDTH))] + [_ANY] * len(deps),
        out_specs=[_rows(tq, POOL_WIDTH), _rows(tq, D_MODEL), _full((1, D_MODEL)),
                   _rows(tq, ATTN_WIDTH), _rows(tq, KV_WIDTH), _rows(tq, KV_WIDTH), nxt, nxt,
                   _full((N_HEADS, _LANES)), _full((1, ATTN_WIDTH))],
        out_shape=[jax.ShapeDtypeStruct((S, POOL_WIDTH), F32), jax.ShapeDtypeStruct((S, D_MODEL), BF16),
                   jax.ShapeDtypeStruct((1, D_MODEL), F32),
                   jax.ShapeDtypeStruct((S, ATTN_WIDTH), F32), jax.ShapeDtypeStruct((S, KV_WIDTH), F32),
                   jax.ShapeDtypeStruct((S, KV_WIDTH), F32),
                   jax.ShapeDtypeStruct((steps, BLOCK, KV_WIDTH), F32), jax.ShapeDtypeStruct((steps, BLOCK, KV_WIDTH), F32),
                   jax.ShapeDtypeStruct((N_HEADS, _LANES), F32), jax.ShapeDtypeStruct((1, ATTN_WIDTH), F32)],
        scratch_shapes=[pltpu.VMEM((D_MODEL, D_MODEL), BF16), pltpu.SemaphoreType.DMA((N_SHARD,))],
        compiler_params=_params(dimension_semantics=("arbitrary",)),
    )(*args, *deps)


def _inv_freq_row():
    inv_freq = ROPE_THETA ** (-jnp.arange(0, ROT_DIM, 2, dtype=F32) / ROT_DIM)
    per_head = jnp.concatenate([inv_freq, inv_freq, jnp.zeros((HEAD_DIM - ROT_DIM,), F32)])
    return jnp.tile(per_head, _LANES // HEAD_DIM).reshape(1, _LANES)


def _local_step(x, pos, target, small, mine, weights_of, grads_ready):
    rope = _rope_tables(pos, _inv_freq_row())
    (wgu1,) = weights_of("ffn1_up", (rope,))
    g1, u1, a1 = _ffn_up(x, small["ffn1_pre"], wgu1, mine)
    (wd1,) = weights_of("ffn1_down", (a1,))
    x1, f1 = _ffn_down(x, a1, small["ffn1_post"], wd1, mine)
    w_in_t, w_out = weights_of("mixer", (x1,))
    u, q, kz, vz = _mixer_in_fwd(x1, small["mix_pre"], w_in_t, mine, rope)
    y_pool = _pool_fwd(u, small["w_pool"], small["pool_scale"], small["g_pool"])
    o, x2, m, y = _attn_fwd(q, kz, vz, small["sinks"], small["g_attn"], y_pool, x1, w_out, mine, small["mix_post"])
    wgu2, wd2 = weights_of("ffn2", (x2,))
    g2, u2, a2 = _ffn_up(x2, small["ffn2_pre"], wgu2, mine)
    dx3, df2, loss_acc, dpost2 = _ffn_down(x2, a2, small["ffn2_post"], wd2, mine, target=target)
    grads = {"loss": loss_acc * (0.5 / D_MODEL), "ffn2_post": dpost2}
    dx2, h3, dgu2, grads["ffn2_pre"] = _ffn_bwd(
        dx3, x2, df2, g2, u2, small["ffn2_pre"], small["ffn2_post"], wgu2, wd2, mine, df_known=True)
    dwgu2 = _wgrad(h3, dgu2, D_MODEL, FF_CHUNK, "wgrad_gu2", column_shards=True)
    dwd2 = _wgrad(a2, df2, FF_CHUNK, D_MODEL, "wgrad_down2")
    deps = grads_ready("ffn2", {"ffn2_w_gu": dwgu2, "ffn2_w_down": dwd2})
    dy_pool, dm, grads["mix_post"], dq, dk, dv, dk_next, dv_next, dsinks, grads["g_attn"] = _attn_bwd(
        dx2, m, w_out, mine, small["mix_post"], o, q, kz, vz, small["sinks"], small["g_attn"], deps=deps)
    dw_out = _wgrad(y, dm, D_MODEL, D_MODEL, "wgrad_out")
    grads["sinks"] = dsinks[:, 0].reshape(1, N_HEADS)
    du, grads["w_pool"], grads["pool_scale"], grads["g_pool"] = _pool_bwd(
        dy_pool, u, small["w_pool"], small["pool_scale"], small["g_pool"])
    dx1, dz, h2, grads["mix_pre"] = _mixer_in_bwd(dx2, x1, small["mix_pre"], w_in_t, mine, du, dq, dk, dv, dk_next, dv_next, rope)
    dw_in_t = _wgrad(dz, h2, IN_WIDTH, D_MODEL, "wgrad_in")
    deps = grads_ready("mixer", {"w_in": dw_in_t, "w_out": dw_out})
    dx, h1, dgu1, df1, grads["ffn1_pre"], grads["ffn1_post"] = _ffn_bwd(
        dx1, x, f1, g1, u1, small["ffn1_pre"], small["ffn1_post"], wgu1, wd1, mine, deps=deps)
    dwgu1 = _wgrad(h1, dgu1, D_MODEL, FF_CHUNK, "wgrad_gu1", column_shards=True)
    deps = grads_ready("ffn1_gu", {"ffn1_w_gu": dwgu1}, small=grads)
    dwd1 = _wgrad(a1, df1, FF_CHUNK, D_MODEL, "wgrad_down1", deps=deps)
    grads_ready("ffn1_down", {"ffn1_w_down": dwd1})
    return dx


def _place():
    return lax.axis_index("x"), lax.axis_index("y"), lax.axis_index("c")


def _other_chips(x, y):
    return [(1 - x, y), (x, 1 - y), (1 - x, 1 - y)]


_HBM = pl.BlockSpec(memory_space=pltpu.HBM)
_SEM = pl.BlockSpec(memory_space=pltpu.SEMAPHORE)
_EFFECT = pltpu.SideEffectType.DATAFLOW_SIDE_EFFECTING
GATHER, GATHER_HALF, REDUCE, BROADCAST = "gather", "gather_half", "reduce", "broadcast"
N_DEVICES = 8


def _in_hbm(a):
    return pltpu.with_memory_space_constraint(a, pltpu.HBM)


def _core_half(rows, c):
    return pl.ds(pl.multiple_of(c * (rows // 2), 16), rows // 2)


def _chip_copies(kind, srcs, lands, send_sems, recv_sems):
    x, y, c = _place()
    mine = 2 * x + y
    copies = []
    for w in range(len(srcs)):
        if kind == BROADCAST:
            peers = [(x ^ (k >> 2), y ^ ((k >> 1) & 1), c ^ (k & 1)) for k in range(1, N_DEVICES)]
        else:
            peers = [(px, py, c) for px, py in _other_chips(x, y)]
        for k, (px, py, pc) in enumerate(peers):
            if kind == GATHER:
                src, dst = srcs[w], lands[w].at[mine]
            elif kind == GATHER_HALF:
                half = _core_half(srcs[w].shape[0], c)
                src, dst = srcs[w].at[half, :], lands[w].at[mine, half, :]
            elif kind == BROADCAST:
                src, dst = srcs[w], lands[w].at[2 * mine + c]
            else:
                src, dst = srcs[w].at[2 * px + py], lands[w].at[k]
            pair = len(peers) * w + k
            copies.append(pltpu.make_async_remote_copy(
                src_ref=src, dst_ref=dst, send_sem=send_sems.at[pair], recv_sem=recv_sems.at[pair],
                device_id=(px, py, pc), device_id_type=MESH))
    return copies


def _landing_shape(kind, src):
    if kind == REDUCE:
        return (N_SHARD - 1,) + src.shape[1:]
    return ((N_DEVICES if kind == BROADCAST else N_SHARD),) + src.shape


def _peer_count(kind):
    return N_DEVICES - 1 if kind == BROADCAST else N_SHARD - 1


def _exchange_start(kinds, groups, name):
    sizes = [len(g) for g in groups]
    flat = [s for g in groups for s in g]
    n, ng = len(flat), len(groups)

    def body(*refs):
        srcs, lands = refs[:n], refs[n:2 * n]
        sems = refs[2 * n:2 * n + 2 * ng]
        token = refs[-1]
        start = 0
        for gi, size in enumerate(sizes):
            for cp in _chip_copies(kinds[gi], srcs[start:start + size], lands[start:start + size],
                                   sems[2 * gi], sems[2 * gi + 1]):
                cp.start()
            start += size
        token[...] = jnp.zeros_like(token)

    landings = [lax.empty(_landing_shape(kind, s), s.dtype) for kind, g in zip(kinds, groups) for s in g]
    sem_shapes = [pltpu.SemaphoreType.DMA((size * _peer_count(kind),)) for kind, size in zip(kinds, sizes)
                  for _ in range(2)]
    outs = pl.pallas_call(
        body, name=name,
        in_specs=[_HBM] * (2 * n),
        out_specs=[_SEM] * (2 * ng) + [_HBM] * (2 * n) + [pl.BlockSpec(memory_space=pltpu.VMEM)],
        out_shape=sem_shapes + [pltpu.HBM(a.shape, a.dtype) for a in flat + landings]
        + [jax.ShapeDtypeStruct((8, _LANES), F32)],
        input_output_aliases={i: 2 * ng + i for i in range(2 * n)},
        compiler_params=pltpu.CompilerParams(has_side_effects=_EFFECT),
    )(*[_in_hbm(a) for a in flat + landings])
    sems, srcs, lands, token = outs[:2 * ng], outs[2 * ng:2 * ng + n], outs[2 * ng + n:2 * ng + 2 * n], outs[-1]
    handles, start = [], 0
    for gi, size in enumerate(sizes):
        handles.append((sems[2 * gi], sems[2 * gi + 1], srcs[start:start + size], lands[start:start + size]))
        start += size
    return handles, token


def _exchange_wait(kind, handle, after, name):
    send_sems, recv_sems, srcs, lands = handle
    n = len(srcs)

    def body(*refs):
        copies = _chip_copies(kind, refs[:n], refs[n:2 * n], refs[2 * n], refs[2 * n + 1])
        for cp in copies:
            cp.wait_send()
        for cp in copies:
            cp.wait_recv()

    outs = pl.pallas_call(
        body, name=name,
        in_specs=[_HBM] * (2 * n) + [_SEM, _SEM] + [_ANY] * len(after),
        out_specs=[_HBM] * (2 * n),
        out_shape=[pltpu.HBM(a.shape, a.dtype) for a in list(srcs) + list(lands)],
        input_output_aliases={i: i for i in range(2 * n)},
        compiler_params=pltpu.CompilerParams(has_side_effects=_EFFECT),
    )(*srcs, *lands, send_sems, recv_sems, *after)
    return outs[:n], outs[n:]


def _swap_gathered_halves(lands, name):
    n = len(lands)

    def body(*refs):
        bufs = refs[n:2 * n]
        send_sems, recv_sems = refs[2 * n:]
        x, y, c = _place()
        mine = 2 * x + y
        sends, arrivals = [], []
        for w in range(n):
            rows = bufs[w].shape[1]
            for d in range(1, N_SHARD):
                slot = (mine + d) % N_SHARD
                sems = dict(send_sem=send_sems.at[(N_SHARD - 1) * w + d - 1],
                            recv_sem=recv_sems.at[(N_SHARD - 1) * w + d - 1],
                            device_id=(x, y, 1 - c), device_id_type=MESH)
                fetched = bufs[w].at[slot, _core_half(rows, c), :]
                missing = bufs[w].at[slot, _core_half(rows, 1 - c), :]
                sends.append(pltpu.make_async_remote_copy(src_ref=fetched, dst_ref=fetched, **sems))
                arrivals.append(pltpu.make_async_remote_copy(src_ref=missing, dst_ref=missing, **sems))
        for cp in sends:
            cp.start()
        for cp in arrivals:
            cp.wait_recv()
        for cp in sends:
            cp.wait_send()

    return pl.pallas_call(
        body, name=name, in_specs=[_ANY] * n, out_specs=[_ANY] * n,
        out_shape=[jax.ShapeDtypeStruct(a.shape, a.dtype) for a in lands],
        input_output_aliases={i: i for i in range(n)},
        scratch_shapes=[pltpu.SemaphoreType.DMA((n * (N_SHARD - 1),)), pltpu.SemaphoreType.DMA((n * (N_SHARD - 1),))],
        compiler_params=pltpu.CompilerParams(has_side_effects=True),
    )(*lands)


def _swap_with_sibling(partials, name):
    n = len(partials)

    def body(*refs):
        ins, outs = refs[:n], refs[n:2 * n]
        send_sems, recv_sems = refs[2 * n:]
        x, y, c = _place()
        sends = [pltpu.make_async_remote_copy(
            src_ref=ins[w], dst_ref=outs[w], send_sem=send_sems.at[w], recv_sem=recv_sems.at[w],
            device_id=(x, y, 1 - c), device_id_type=MESH) for w in range(n)]
        for cp in sends:
            cp.start()
        for cp in sends:
            cp.wait_recv()
        for cp in sends:
            cp.wait_send()

    return pl.pallas_call(
        body, name=name,
        in_specs=[_ANY] * n, out_specs=[_ANY] * n,
        out_shape=[jax.ShapeDtypeStruct(p.shape, p.dtype) for p in partials],
        scratch_shapes=[pltpu.SemaphoreType.DMA((n,)), pltpu.SemaphoreType.DMA((n,))],
        compiler_params=pltpu.CompilerParams(has_side_effects=True),
    )(*partials)


def _row_block(rows, cap):
    best = None
    for cand in range(16, min(rows, cap) + 1, 16):
        if rows % cand == 0:
            best = cand
    assert best is not None, rows
    return best


def _chip_partial(own, received, shard, name):
    _, R, C = own.shape
    rb = _row_block(R, 512)

    def body(shard_ref, own_ref, rec_ref, out_ref):
        acc = own_ref[...]
        for k in range(3):
            acc = acc + rec_ref[k].astype(F32)
        out_ref[...] = acc.astype(BF16)

    return pl.pallas_call(
        body, name=name,
        grid_spec=pltpu.PrefetchScalarGridSpec(
            num_scalar_prefetch=1, grid=(R // rb,),
            in_specs=[pl.BlockSpec((None, rb, C), lambda i, s: (s[0], i, 0)),
                      pl.BlockSpec((3, rb, C), lambda i, s: (0, i, 0))],
            out_specs=pl.BlockSpec((rb, C), lambda i, s: (i, 0))),
        out_shape=jax.ShapeDtypeStruct((R, C), BF16),
        compiler_params=_params(dimension_semantics=("arbitrary",)),
    )(shard, own, received)


def _adamw(w, m, v, g_parts, name, slot=None):
    R, C = w.shape
    by_device = slot is not None
    rb = _row_block(R, 512) if R % 16 == 0 else R

    def body(w_ref, m_ref, v_ref, *refs):
        g_refs, (grad_ref, delta_ref, m_out, v_out) = refs[:-4], refs[-4:]
        if by_device:
            own_ref, land_ref, slot_ref = g_refs
            part = lambda d: jnp.where(slot_ref[0] == d, own_ref[...], land_ref[d])
            g = part(0)
            for d in range(1, N_DEVICES):
                g = g + part(d)
        else:
            g = g_refs[0][...].astype(F32)
            for g_ref in g_refs[1:]:
                g = g + g_ref[...].astype(F32)
        grad_ref[...] = g
        new_m = ADAM_B1 * m_ref[...] + (1.0 - ADAM_B1) * g
        new_v = ADAM_B2 * v_ref[...] + (1.0 - ADAM_B2) * (g * g)
        m_hat = new_m / (1.0 - ADAM_B1 ** ADAM_STEP)
        v_hat = new_v / (1.0 - ADAM_B2 ** ADAM_STEP)
        delta_ref[...] = -ADAM_LR * (m_hat / (jnp.sqrt(v_hat) + ADAM_EPS) + ADAM_WD * w_ref[...])
        m_out[...] = new_m
        v_out[...] = new_v

    spec = pl.BlockSpec((rb, C), lambda i: (i, 0))
    if by_device:
        g_specs = [spec, pl.BlockSpec((N_DEVICES, rb, C), lambda i: (0, i, 0)), _SMEM]
        g_parts = list(g_parts) + [slot]
    else:
        g_specs = [spec] * len(g_parts)
    return pl.pallas_call(
        body, name=name, grid=(R // rb,),
        in_specs=[spec, spec, spec] + g_specs,
        out_specs=[spec] * 4,
        out_shape=[jax.ShapeDtypeStruct((R, C), F32)] * 4,
        compiler_params=_params(dimension_semantics=("arbitrary",)),
    )(w, m, v, *g_parts)


SMALL_NAMES = ("ffn1_pre", "ffn1_post", "mix_pre", "pool_scale", "sinks", "g_pool", "g_attn", "mix_post",
               "ffn2_pre", "ffn2_post", "w_pool")
_SLAB_PART = 8 * _LANES


SLAB_NAMES = SMALL_NAMES + ("loss",)


def _to_slab(parts):
    rows = []
    for name in SLAB_NAMES:
        flat = parts[name].reshape(-1) if name in parts else jnp.zeros((_SLAB_PART,), F32)
        padded = -(-flat.shape[0] // _SLAB_PART) * _SLAB_PART
        rows.append(jnp.pad(flat, (0, padded - flat.shape[0])).reshape(-1, _LANES))
    return jnp.concatenate(rows, axis=0)


def _from_slab(slab, like):
    out, row = {}, 0
    for name in SLAB_NAMES:
        size = like[name].size
        rows = -(-size // _SLAB_PART) * (_SLAB_PART // _LANES)
        out[name] = slab[row:row + rows].reshape(-1)[:size].reshape(like[name].shape)
        row += rows
    return out


BIG_NAMES = ("ffn1_w_gu", "ffn1_w_down", "w_in", "w_out", "ffn2_w_gu", "ffn2_w_down")
WEIGHT_ORDER = ("ffn1_pre", "ffn1_w_gu", "ffn1_w_down", "ffn1_post", "mix_pre", "w_in", "w_pool", "pool_scale",
                "sinks", "g_pool", "g_attn", "w_out", "mix_post", "ffn2_pre", "ffn2_w_gu", "ffn2_w_down", "ffn2_post")


def kernel(x, positions, ffn1_pre, ffn1_w_gu, ffn1_w_down, ffn1_post, mix_pre, w_in, w_pool, pool_scale, sinks, g_pool, g_attn, w_out, mix_post, ffn2_pre, ffn2_w_gu, ffn2_w_down, ffn2_post, loss_target, m_ffn1_pre, m_ffn1_w_gu, m_ffn1_w_down, m_ffn1_post, m_mix_pre, m_w_in, m_w_pool, m_pool_scale, m_sinks, m_g_pool, m_g_attn, m_w_out, m_mix_post, m_ffn2_pre, m_ffn2_w_gu, m_ffn2_w_down, m_ffn2_post, v_ffn1_pre, v_ffn1_w_gu, v_ffn1_w_down, v_ffn1_post, v_mix_pre, v_w_in, v_w_pool, v_pool_scale, v_sinks, v_g_pool, v_g_attn, v_w_out, v_mix_post, v_ffn2_pre, v_ffn2_w_gu, v_ffn2_w_down, v_ffn2_post):
    given = dict(locals())
    weights = {n: given[n][0] for n in WEIGHT_ORDER}
    moments_m = {n: given["m_" + n][0] for n in WEIGHT_ORDER}
    moments_v = {n: given["v_" + n][0] for n in WEIGHT_ORDER}
    S = x.shape[1]
    shard = (2 * lax.axis_index("x") + lax.axis_index("y")).astype(jnp.int32).reshape(1)

    local16 = {n: weights[n].astype(BF16) for n in BIG_NAMES if n != "w_in"}
    local16["w_in"] = weights["w_in"].T.astype(BF16)
    gather_groups = {"ffn1_up": ("ffn1_w_gu",), "ffn1_down": ("ffn1_w_down",), "mixer": ("w_in", "w_out"),
                     "ffn2": ("ffn2_w_gu", "ffn2_w_down")}
    gather_kinds = {"ffn1_up": GATHER_HALF, "ffn1_down": GATHER_HALF, "mixer": GATHER, "ffn2": GATHER}
    handles, _ = _exchange_start(list(gather_kinds.values()),
                                 [[local16[n] for n in names] for names in gather_groups.values()], "gather_start")
    gather_handles = dict(zip(gather_groups, handles))

    def weights_of(group, after):
        kind = gather_kinds[group]
        owns, lands = _exchange_wait(kind, gather_handles[group], list(after), "gather_wait_" + group)
        if kind == GATHER_HALF:
            lands = _swap_gathered_halves(lands, "swap_gathered_" + group)
        return list(zip(lands, owns))

    pending, last_token = {}, []

    def grads_ready(group, grads, small=None):
        names = list(grads)
        kinds, sources = [REDUCE], [[grads[n][1] for n in names]]
        if small is not None:
            kinds, sources = kinds + [BROADCAST], sources + [[_to_slab(small)]]
        handles, token = _exchange_start(kinds, sources, "reduce_start_" + group)
        handle = handles[0]
        if small is not None:
            pending["small"] = handles[1]
        pending[group] = (names, handle, [grads[n][0] for n in names])
        last_token[:] = [token]
        return [token]

    small = {n: (weights[n] if weights[n].ndim > 1 else weights[n].reshape(1, -1)) for n in SMALL_NAMES}
    dx = _local_step(x[0], positions.reshape(S, 1), loss_target[0], small, shard, weights_of, grads_ready)

    grad, delta, new_m, new_v = {}, {}, {}, {}

    def finish(groups, after):
        names, partials = [], []
        for group in groups:
            group_names, handle, own32 = pending[group]
            _, received = _exchange_wait(REDUCE, handle, after, "reduce_wait_" + group)
            names += group_names
            partials += [_chip_partial(g32, rec, shard, "chip_partial_" + n)
                         for n, g32, rec in zip(group_names, own32, received)]
        siblings = _swap_with_sibling(partials, "swap_" + groups[0])
        for name, mine, theirs in zip(names, partials, siblings):
            if name == "w_in":
                mine, theirs = mine.T, theirs.T
            grad[name], delta[name], new_m[name], new_v[name] = _adamw(
                weights[name], moments_m[name], moments_v[name], [mine, theirs], "adamw_" + name)
        return [grad[names[-1]]]

    after = finish(["ffn2"], last_token)
    after = finish(["mixer"], after)
    (own_slab,), (slab_landing,) = _exchange_wait(BROADCAST, pending["small"], after, "reduce_wait_small")
    device = (2 * shard + lax.axis_index("c")).astype(jnp.int32)
    small_like = dict({n: small[n] for n in SMALL_NAMES}, loss=jnp.zeros((8, _LANES), F32))
    slabs = _adamw(_to_slab(small), _to_slab({n: moments_m[n] for n in SMALL_NAMES}),
                   _to_slab({n: moments_v[n] for n in SMALL_NAMES}), [own_slab, slab_landing], "adamw_small",
                   slot=device)
    for store, slab in zip((grad, delta, new_m, new_v), slabs):
        store.update(_from_slab(slab, small_like))
    loss = grad["loss"][0, 0]
    after = finish(["ffn1_gu"], [slabs[0]])
    finish(["ffn1_down"], after)

    def out(store):
        return [store[n].reshape(given[n].shape) for n in WEIGHT_ORDER]
    return (loss, dx[None], *out(grad), *out(delta), *out(new_m), *out(new_v))
```

```python
import jax
import jax.numpy as jnp
from jax import lax
from jax.experimental import pallas as pl
from jax.experimental.pallas import tpu as pltpu

F32 = jnp.float32
BF16 = jnp.bfloat16

D_MODEL = 1024
D_FF = 2816
N_SHARD = 4
FF_CHUNK = D_FF // 2
POOL_WINDOWS = (2, 4, 8, 16)
POOL_WIDTH = 512
POOL_GROUP = 128
HALO = 16
HEAD_DIM = 64
N_HEADS = 8
N_KV_HEADS = 2
ATTN_WIDTH = 512
KV_WIDTH = 128
IN_WIDTH = 1280
BLOCK = 128
ATTN_BLOCKS = 4
ROT_DIM = 16
ROPE_THETA = 500000.0
EPS = 1e-6
NEG_INF = -1e30
ATTN_SCALE = HEAD_DIM ** -0.5

ADAM_LR = 0.001
ADAM_B1 = 0.9
ADAM_B2 = 0.999
ADAM_EPS = 1e-08
ADAM_WD = 0.01
ADAM_STEP = 10

VMEM_LIMIT = 60 * 1024 * 1024
FFN_UP_TILE = 512
FFN_DOWN_TILE = 1024
FFN_BWD_TILE = 256
MIXER_TILE = 512

MESH = pl.DeviceIdType.MESH


def _params(**kw):
    return pltpu.CompilerParams(vmem_limit_bytes=VMEM_LIMIT, **kw)


def _dot(a, b):
    return jnp.dot(a, b, preferred_element_type=F32)


def _dot_nt(a, b):
    return lax.dot_general(a, b, (((1,), (1,)), ((), ())), preferred_element_type=F32)


def _dot_tn(a, b):
    return lax.dot_general(a, b, (((0,), (0,)), ((), ())), preferred_element_type=F32)


def _rstd(x):
    return lax.rsqrt(jnp.mean(x * x, axis=-1, keepdims=True) + EPS)


def _norm_bwd(dy, xn, r, gain):
    dxn = dy * gain
    return r * (dxn - xn * jnp.mean(dxn * xn, axis=-1, keepdims=True))


def _sigmoid(x):
    return 1.0 / (1.0 + jnp.exp(-x))


def _full(shape):
    return pl.BlockSpec(shape, lambda *_: (0,) * len(shape))


def _rows(tile, width):
    return pl.BlockSpec((tile, width), lambda i: (i, 0))


_ANY = pl.BlockSpec(memory_space=pl.ANY)


_SMEM = pl.BlockSpec(memory_space=pltpu.SMEM)


def _load_once(pairs, sem):
    @pl.when(pl.program_id(0) == 0)
    def _():
        copies = [pltpu.make_async_copy(src, dst, sem.at[n]) for n, (src, dst) in enumerate(pairs)]
        for cp in copies:
            cp.start()
        for cp in copies:
            cp.wait()


def _gathered(land_ref, own_ref, vmem_ref, mine, rows=None):
    def dst(slot):
        if rows is None:
            return vmem_ref.at[slot]
        return vmem_ref.at[pl.ds(pl.multiple_of(slot * rows, 16), rows), :]
    pairs = [(land_ref.at[(mine + d) % N_SHARD], dst((mine + d) % N_SHARD)) for d in range(1, N_SHARD)]
    return pairs + [(own_ref, dst(mine))]


def _ignoring(body, start, count):
    def wrapped(*refs):
        return body(*refs[:start], *refs[start + count:])
    return wrapped


def _ffn_up(x, pre, wgu, mine, deps=()):
    S = x.shape[0]
    tm = FFN_UP_TILE

    def body(x_ref, pre_ref, wgu_land, wgu_own, mine_ref, g_ref, u_ref, a_ref, wgu_v, sem):
        _load_once(_gathered(wgu_land, wgu_own, wgu_v, mine_ref[0]), sem)
        xv = x_ref[...]
        h = ((xv * _rstd(xv)) * pre_ref[...]).astype(BF16)
        for c in range(2):
            cols = slice(c * FF_CHUNK, (c + 1) * FF_CHUNK)
            g = _dot(h, wgu_v[c])
            u = _dot(h, wgu_v[2 + c])
            g_ref[:, cols] = g.astype(BF16)
            u_ref[:, cols] = u.astype(BF16)
            a_ref[:, cols] = ((g * _sigmoid(g)) * u).astype(BF16)

    args = [x, pre, *wgu, mine]
    return pl.pallas_call(
        _ignoring(body, len(args), len(deps)), name="ffn_up", grid=(S // tm,),
        in_specs=[_rows(tm, D_MODEL), _full((1, D_MODEL)), _ANY, _ANY, _SMEM] + [_ANY] * len(deps),
        out_specs=[_rows(tm, D_FF)] * 3,
        out_shape=[jax.ShapeDtypeStruct((S, D_FF), BF16)] * 3,
        scratch_shapes=[pltpu.VMEM((N_SHARD, D_MODEL, FF_CHUNK), BF16), pltpu.SemaphoreType.DMA((N_SHARD,))],
        compiler_params=_params(dimension_semantics=("arbitrary",)),
    )(*args, *deps)


def _ffn_down(x, a, post, wd, mine, target=None, deps=()):
    S = x.shape[0]
    tm = FFN_DOWN_TILE
    with_loss = target is not None

    def body(*refs):
        if with_loss:
            (x_ref, a_ref, post_ref, wd_land, wd_own, mine_ref, tgt_ref,
             out_ref, f_ref, loss_ref, dpost_ref, wd_v, sem) = refs
        else:
            x_ref, a_ref, post_ref, wd_land, wd_own, mine_ref, out_ref, f_ref, wd_v, sem = refs
        _load_once(_gathered(wd_land, wd_own, wd_v, mine_ref[0], rows=D_FF // N_SHARD), sem)
        xv = x_ref[...]
        facc = _dot(a_ref[...], wd_v[...])
        rf = _rstd(facc)
        fn = facc * rf
        out = xv + 0.5 * (fn * post_ref[...])
        if with_loss:
            diff = out - tgt_ref[...]
            dout = diff * (1.0 / D_MODEL)
            out_ref[...] = dout

            @pl.when(pl.program_id(0) == 0)
            def _():
                loss_ref[...] = jnp.zeros_like(loss_ref)
                dpost_ref[...] = jnp.zeros_like(dpost_ref)
            loss_ref[...] += jnp.sum(diff * diff)
            dn = 0.5 * dout
            dpost_ref[...] += jnp.sum(dn * fn, axis=0, keepdims=True)
            f_ref[...] = _norm_bwd(dn, fn, rf, post_ref[...]).astype(BF16)
        else:
            f_ref[...] = facc
            out_ref[...] = out

    in_specs = [_rows(tm, D_MODEL), _rows(tm, D_FF), _full((1, D_MODEL)), _ANY, _ANY, _SMEM]
    args = [x, a, post, *wd, mine]
    out_shape = [jax.ShapeDtypeStruct((S, D_MODEL), F32), jax.ShapeDtypeStruct((S, D_MODEL), F32)]
    out_specs = [_rows(tm, D_MODEL), _rows(tm, D_MODEL)]
    if with_loss:
        in_specs.append(_rows(tm, D_MODEL))
        args.append(target)
        out_shape[1] = jax.ShapeDtypeStruct((S, D_MODEL), BF16)
        out_shape += [jax.ShapeDtypeStruct((8, 128), F32), jax.ShapeDtypeStruct((1, D_MODEL), F32)]
        out_specs += [_full((8, 128)), _full((1, D_MODEL))]
    return pl.pallas_call(
        _ignoring(body, len(args), len(deps)), name="ffn_down_loss" if with_loss else "ffn_down",
        grid=(S // tm,), in_specs=in_specs + [_ANY] * len(deps), out_specs=out_specs, out_shape=out_shape,
        scratch_shapes=[pltpu.VMEM((D_FF, D_MODEL), BF16), pltpu.SemaphoreType.DMA((N_SHARD,))],
        compiler_params=_params(dimension_semantics=("arbitrary",)),
    )(*args, *deps)


def _ffn_bwd(dout, x, f, g, u, pre, post, wgu, wd, mine, df_known=False, deps=()):
    S = x.shape[0]
    tm = FFN_BWD_TILE

    def body(*refs):
        if df_known:
            (dout_ref, x_ref, f_ref, g_ref, u_ref, pre_ref, post_ref, wgu_land, wgu_own, wd_land, wd_own, mine_ref,
             dx_ref, h_ref, dgu_ref, dpre_ref, wgu_v, wd_v, sem) = refs
        else:
            (dout_ref, x_ref, f_ref, g_ref, u_ref, pre_ref, post_ref, wgu_land, wgu_own, wd_land, wd_own, mine_ref,
             dx_ref, h_ref, dgu_ref, df_ref, dpre_ref, dpost_ref, wgu_v, wd_v, sem) = refs
        _load_once(_gathered(wgu_land, wgu_own, wgu_v, mine_ref[0])
                   + _gathered(wd_land, wd_own, wd_v, mine_ref[0], rows=D_FF // N_SHARD), sem)

        @pl.when(pl.program_id(0) == 0)
        def _():
            dpre_ref[...] = jnp.zeros_like(dpre_ref)
            if not df_known:
                dpost_ref[...] = jnp.zeros_like(dpost_ref)

        dout_v = dout_ref[...]
        if df_known:
            df = f_ref[...]
        else:
            dn = 0.5 * dout_v
            fv = f_ref[...]
            rf = _rstd(fv)
            fn = fv * rf
            dpost_ref[...] += jnp.sum(dn * fn, axis=0, keepdims=True)
            df = _norm_bwd(dn, fn, rf, post_ref[...]).astype(BF16)
            df_ref[...] = df
        dh = jnp.zeros((tm, D_MODEL), F32)
        for c in range(2):
            cols = slice(c * FF_CHUNK, (c + 1) * FF_CHUNK)
            da = _dot_nt(df, wd_v[cols, :])
            gv = g_ref[:, cols].astype(F32)
            uv = u_ref[:, cols].astype(F32)
            sg = _sigmoid(gv)
            silu = gv * sg
            dg = ((da * uv) * (sg * (1.0 + gv * (1.0 - sg)))).astype(BF16)
            du = (da * silu).astype(BF16)
            dgu_ref[:, cols] = dg
            dgu_ref[:, 2 * FF_CHUNK + c * FF_CHUNK:2 * FF_CHUNK + (c + 1) * FF_CHUNK] = du
            dh = dh + _dot_nt(dg, wgu_v[c]) + _dot_nt(du, wgu_v[2 + c])
        xv = x_ref[...]
        rx = _rstd(xv)
        xn = xv * rx
        h_ref[...] = (xn * pre_ref[...]).astype(BF16)
        dpre_ref[...] += jnp.sum(dh * xn, axis=0, keepdims=True)
        dx_ref[...] = dout_v + _norm_bwd(dh, xn, rx, pre_ref[...])

    args = [dout, x, f, g, u, pre, post, *wgu, *wd, mine]
    out_specs = [_rows(tm, D_MODEL), _rows(tm, D_MODEL), _rows(tm, 2 * D_FF), _rows(tm, D_MODEL),
                 _full((1, D_MODEL)), _full((1, D_MODEL))]
    out_shape = [jax.ShapeDtypeStruct((S, D_MODEL), F32), jax.ShapeDtypeStruct((S, D_MODEL), BF16),
                 jax.ShapeDtypeStruct((S, 2 * D_FF), BF16), jax.ShapeDtypeStruct((S, D_MODEL), BF16),
                 jax.ShapeDtypeStruct((1, D_MODEL), F32), jax.ShapeDtypeStruct((1, D_MODEL), F32)]
    if df_known:
        out_specs = out_specs[:3] + out_specs[4:5]
        out_shape = out_shape[:3] + out_shape[4:5]
    return pl.pallas_call(
        _ignoring(body, len(args), len(deps)), name="ffn_bwd_from_df" if df_known else "ffn_bwd", grid=(S // tm,),
        in_specs=[_rows(tm, D_MODEL), _rows(tm, D_MODEL), _rows(tm, D_MODEL), _rows(tm, D_FF), _rows(tm, D_FF),
                  _full((1, D_MODEL)), _full((1, D_MODEL)), _ANY, _ANY, _ANY, _ANY, _SMEM] + [_ANY] * len(deps),
        out_specs=out_specs, out_shape=out_shape,
        scratch_shapes=[pltpu.VMEM((N_SHARD, D_MODEL, FF_CHUNK), BF16), pltpu.VMEM((D_FF, D_MODEL), BF16),
                        pltpu.SemaphoreType.DMA((2 * N_SHARD,))],
        compiler_params=_params(dimension_semantics=("arbitrary",)),
    )(*args, *deps)


def _wgrad(lhs, rhs, m_block, n_block, name, column_shards=False, tk=2048, deps=()):
    S, M = lhs.shape
    N = rhs.shape[1]
    k_steps = S // tk

    def body(lhs_ref, rhs_ref, out_ref, out16_ref):
        k = pl.program_id(2)

        @pl.when(k == 0)
        def _():
            out_ref[...] = jnp.zeros_like(out_ref)
        out_ref[...] += _dot_tn(lhs_ref[...], rhs_ref[...])

        @pl.when(k == k_steps - 1)
        def _():
            out16_ref[...] = out_ref[...].astype(BF16)

    if column_shards:
        assert N == N_SHARD * n_block
        shape = (N_SHARD, M, n_block)
        out_spec = pl.BlockSpec((None, m_block, n_block), lambda i, j, k: (j, i, 0))
    else:
        shape = (M, N)
        out_spec = pl.BlockSpec((m_block, n_block), lambda i, j, k: (i, j))
    out, out16 = pl.pallas_call(
        _ignoring(body, 2, len(deps)), name=name, grid=(M // m_block, N // n_block, k_steps),
        in_specs=[pl.BlockSpec((tk, m_block), lambda i, j, k: (k, i)),
                  pl.BlockSpec((tk, n_block), lambda i, j, k: (k, j))] + [_ANY] * len(deps),
        out_specs=[out_spec, out_spec],
        out_shape=[jax.ShapeDtypeStruct(shape, F32), jax.ShapeDtypeStruct(shape, BF16)],
        compiler_params=_params(dimension_semantics=("arbitrary", "arbitrary", "arbitrary")),
    )(lhs, rhs, *deps)
    if not column_shards:
        out = out.reshape(N_SHARD, M // N_SHARD, N)
        out16 = out16.reshape(N_SHARD, M // N_SHARD, N)
    return out, out16


def _rope_tables(pos, invf):
    S = pos.shape[0]
    tm = MIXER_TILE

    def body(pos_ref, invf_ref, out_ref):
        ang = pos_ref[...].astype(F32) * invf_ref[...]
        cos, sin = jnp.cos(ang), jnp.sin(ang)
        lane = lax.broadcasted_iota(jnp.int32, ang.shape, 1) % HEAD_DIM
        first = lane < ROT_DIM // 2
        second = (lane >= ROT_DIM // 2) & (lane < ROT_DIM)
        out_ref[0] = jnp.where(lane < ROT_DIM, cos, 1.0)
        out_ref[1] = jnp.where(first, sin, 0.0)
        out_ref[2] = jnp.where(second, sin, 0.0)

    return pl.pallas_call(
        body, name="rope_tables", grid=(S // tm,),
        in_specs=[_rows(tm, 1), _full((1, _LANES))],
        out_specs=pl.BlockSpec((3, tm, _LANES), lambda i: (0, i, 0)),
        out_shape=jax.ShapeDtypeStruct((3, S, _LANES), F32),
        compiler_params=_params(dimension_semantics=("arbitrary",)),
    )(pos, invf)


def _table_spec(tm):
    return pl.BlockSpec((3, tm, _LANES), lambda i: (0, i, 0))


_HALF = ROT_DIM // 2
_LANES = 128


def _rope(t, tables):
    c, s_first, s_second = tables
    return t * c - pltpu.roll(t, _LANES - _HALF, axis=1) * s_first + pltpu.roll(t, _HALF, axis=1) * s_second


def _rope_transposed(t, tables):
    c, s_first, s_second = tables
    return t * c - pltpu.roll(t * s_first, _HALF, axis=1) + pltpu.roll(t * s_second, _LANES - _HALF, axis=1)


def _store_head_variants(ref, t):
    rolled = pltpu.roll(t, HEAD_DIM, axis=1)
    low = lax.broadcasted_iota(jnp.int32, t.shape, 1) < HEAD_DIM
    zero = jnp.zeros_like(t)
    ref[0] = jnp.where(low, t, zero).astype(BF16)
    ref[1] = jnp.where(low, zero, rolled).astype(BF16)
    ref[2] = jnp.where(low, rolled, zero).astype(BF16)
    ref[3] = jnp.where(low, zero, t).astype(BF16)


def _mixer_in_fwd(x, pre, w_in_t, mine, rope, deps=()):
    S = x.shape[0]
    tm = MIXER_TILE

    def body(x_ref, pre_ref, w_land, w_own, mine_ref, rope_ref, u_ref, q_ref, k_ref, v_ref, w_v, sem):
        _load_once(_gathered(w_land, w_own, w_v, mine_ref[0], rows=IN_WIDTH // N_SHARD), sem)
        xv = x_ref[...]
        h = ((xv * _rstd(xv)) * pre_ref[...]).astype(BF16)
        z = _dot_nt(h, w_v[...])
        tables = (rope_ref[0], rope_ref[1], rope_ref[2])
        u_ref[...] = z[:, :POOL_WIDTH]
        for t in range(ATTN_WIDTH // _LANES):
            lo = POOL_WIDTH + t * _LANES
            q_ref[:, t * _LANES:(t + 1) * _LANES] = (_rope(z[:, lo:lo + _LANES], tables) * ATTN_SCALE).astype(BF16)
        kv = POOL_WIDTH + ATTN_WIDTH
        _store_head_variants(k_ref, _rope(z[:, kv:kv + KV_WIDTH], tables))
        _store_head_variants(v_ref, z[:, kv + KV_WIDTH:])

    args = [x, pre, *w_in_t, mine, rope]
    variants = pl.BlockSpec((2 * N_KV_HEADS, tm, KV_WIDTH), lambda i: (0, i, 0))
    return pl.pallas_call(
        _ignoring(body, len(args), len(deps)), name="mixer_in_fwd", grid=(S // tm,),
        in_specs=[_rows(tm, D_MODEL), _full((1, D_MODEL)), _ANY, _ANY, _SMEM, _table_spec(tm)] + [_ANY] * len(deps),
        out_specs=[_rows(tm, POOL_WIDTH), _rows(tm, ATTN_WIDTH), variants, variants],
        out_shape=[jax.ShapeDtypeStruct((S, POOL_WIDTH), F32), jax.ShapeDtypeStruct((S, ATTN_WIDTH), BF16),
                   jax.ShapeDtypeStruct((2 * N_KV_HEADS, S, KV_WIDTH), BF16),
                   jax.ShapeDtypeStruct((2 * N_KV_HEADS, S, KV_WIDTH), BF16)],
        scratch_shapes=[pltpu.VMEM((IN_WIDTH, D_MODEL), BF16), pltpu.SemaphoreType.DMA((N_SHARD,))],
        compiler_params=_params(dimension_semantics=("arbitrary",)),
    )(*args, *deps)


def _mixer_in_bwd(dres, x, pre, w_in_t, mine, du, dq, dk, dv, dk_next, dv_next, rope, deps=()):
    S = x.shape[0]
    tm = MIXER_TILE

    def body(dres_ref, x_ref, pre_ref, w_land, w_own, mine_ref, du_ref, dq_ref, dk_ref, dv_ref, dkx_ref, dvx_ref,
             rope_ref,
             dx_ref, dz_ref, h_ref, dpre_ref, w_v, sem):
        _load_once(_gathered(w_land, w_own, w_v, mine_ref[0], rows=IN_WIDTH // N_SHARD), sem)

        @pl.when(pl.program_id(0) == 0)
        def _():
            dpre_ref[...] = jnp.zeros_like(dpre_ref)

        tables = (rope_ref[0], rope_ref[1], rope_ref[2])
        dz_ref[:, :POOL_WIDTH] = du_ref[...]
        for t in range(ATTN_WIDTH // _LANES):
            lo = POOL_WIDTH + t * _LANES
            dz_ref[:, lo:lo + _LANES] = _rope_transposed(dq_ref[:, t * _LANES:(t + 1) * _LANES], tables).astype(BF16)
        kv = POOL_WIDTH + ATTN_WIDTH
        has_next = pl.program_id(0) + 1 < steps
        pad = jnp.zeros((tm - BLOCK, KV_WIDTH), F32)
        dk_tile = dk_ref[...] + jnp.concatenate([pad, jnp.where(has_next, dkx_ref[...], 0.0)], axis=0)
        dv_tile = dv_ref[...] + jnp.concatenate([pad, jnp.where(has_next, dvx_ref[...], 0.0)], axis=0)
        dz_ref[:, kv:kv + KV_WIDTH] = _rope_transposed(dk_tile, tables).astype(BF16)
        dz_ref[:, kv + KV_WIDTH:] = dv_tile.astype(BF16)
        dh = _dot(dz_ref[...], w_v[...])
        xv = x_ref[...]
        rx = _rstd(xv)
        xn = xv * rx
        h_ref[...] = (xn * pre_ref[...]).astype(BF16)
        dpre_ref[...] += jnp.sum(dh * xn, axis=0, keepdims=True)
        dx_ref[...] = dres_ref[...] + _norm_bwd(dh, xn, rx, pre_ref[...])

    assert tm == ATTN_BLOCKS * BLOCK
    steps = S // tm
    nxt = pl.BlockSpec((None, BLOCK, KV_WIDTH), lambda i: (jnp.minimum(i + 1, steps - 1), 0, 0))
    args = [dres, x, pre, *w_in_t, mine, du, dq, dk, dv, dk_next, dv_next, rope]
    return pl.pallas_call(
        _ignoring(body, len(args), len(deps)), name="mixer_in_bwd", grid=(S // tm,),
        in_specs=[_rows(tm, D_MODEL), _rows(tm, D_MODEL), _full((1, D_MODEL)), _ANY, _ANY, _SMEM,
                  _rows(tm, POOL_WIDTH), _rows(tm, ATTN_WIDTH), _rows(tm, KV_WIDTH), _rows(tm, KV_WIDTH), nxt, nxt,
                  _table_spec(tm)] + [_ANY] * len(deps),
        out_specs=[_rows(tm, D_MODEL), _rows(tm, IN_WIDTH), _rows(tm, D_MODEL), _full((1, D_MODEL))],
        out_shape=[jax.ShapeDtypeStruct((S, D_MODEL), F32), jax.ShapeDtypeStruct((S, IN_WIDTH), BF16),
                   jax.ShapeDtypeStruct((S, D_MODEL), BF16), jax.ShapeDtypeStruct((1, D_MODEL), F32)],
        scratch_shapes=[pltpu.VMEM((IN_WIDTH, D_MODEL), BF16), pltpu.SemaphoreType.DMA((N_SHARD,))],
        compiler_params=_params(dimension_semantics=("arbitrary",)),
    )(*args, *deps)


def _pool_counts(tile_index, tm, width):
    t = tile_index * tm + lax.broadcasted_iota(jnp.int32, (tm, 1), 0)
    return jnp.minimum(t + 1, width).astype(F32)


def _pool_features(ext, u_tile, tile_index, tm):
    ds = []
    for gi, width in enumerate(POOL_WINDOWS):
        lanes = slice(gi * POOL_GROUP, (gi + 1) * POOL_GROUP)
        s = ext[:, lanes]
        shift = 1
        while shift < width:
            s = s + pltpu.roll(s, shift, axis=0)
            shift *= 2
        ds.append(s[HALO:, :] / _pool_counts(tile_index, tm, width) - u_tile[:, lanes])
    return ds


def _pool_fwd(u, w_pool, pool_scale, g_pool):
    S = u.shape[0]
    tm = MIXER_TILE

    def body(u_ref, w_ref, scale_ref, gain_ref, y_ref, ext_ref):
        i = pl.program_id(0)

        @pl.when(i == 0)
        def _():
            ext_ref[:HALO, :] = jnp.zeros((HALO, POOL_WIDTH), F32)

        u_tile = u_ref[...]
        ext_ref[HALO:, :] = u_tile
        ds = _pool_features(ext_ref[...], u_tile, i, tm)
        ext_ref[:HALO, :] = u_tile[tm - HALO:, :]
        ys = [_dot(ds[gi].astype(BF16), w_ref[gi].astype(BF16)) for gi in range(len(POOL_WINDOWS))]
        po = jnp.concatenate(ys, axis=1) * scale_ref[...]
        y_ref[...] = ((po * _rstd(po)) * gain_ref[...]).astype(BF16)

    return pl.pallas_call(
        body, name="pool_fwd", grid=(S // tm,),
        in_specs=[_rows(tm, POOL_WIDTH), _full((len(POOL_WINDOWS), POOL_GROUP, POOL_GROUP)),
                  _full((1, POOL_WIDTH)), _full((1, POOL_WIDTH))],
        out_specs=_rows(tm, POOL_WIDTH),
        out_shape=jax.ShapeDtypeStruct((S, POOL_WIDTH), BF16),
        scratch_shapes=[pltpu.VMEM((HALO + tm, POOL_WIDTH), F32)],
        compiler_params=_params(dimension_semantics=("arbitrary",)),
    )(u, w_pool, pool_scale, g_pool)


def _pool_bwd(dy, u, w_pool, pool_scale, g_pool):
    S = u.shape[0]
    tm = MIXER_TILE
    n_tiles = S // tm
    halo_blocks = tm // HALO

    def body(dy_ref, u_ref, uprev_ref, w_ref, scale_ref, gain_ref,
             du_ref, dw_ref, dscale_ref, dgain_ref, ext_ref, nxt_ref):
        i = pl.program_id(0)
        tile = n_tiles - 1 - i

        @pl.when(i == 0)
        def _():
            dw_ref[...] = jnp.zeros_like(dw_ref)
            dscale_ref[...] = jnp.zeros_like(dscale_ref)
            dgain_ref[...] = jnp.zeros_like(dgain_ref)
            nxt_ref[...] = jnp.zeros_like(nxt_ref)

        u_tile = u_ref[...]
        ext_ref[:HALO, :] = jnp.where(tile > 0, uprev_ref[...], 0.0)
        ext_ref[HALO:, :] = u_tile
        ds = _pool_features(ext_ref[...], u_tile, tile, tm)
        dsb = [d.astype(BF16) for d in ds]
        wb = [w_ref[gi].astype(BF16) for gi in range(len(POOL_WINDOWS))]
        yraw = jnp.concatenate([_dot(dsb[gi], wb[gi]) for gi in range(len(POOL_WINDOWS))], axis=1)
        po = yraw * scale_ref[...]
        r = _rstd(po)
        pn = po * r
        dyv = dy_ref[...]
        dgain_ref[...] += jnp.sum(dyv * pn, axis=0, keepdims=True)
        dpo = _norm_bwd(dyv, pn, r, gain_ref[...])
        dscale_ref[...] += jnp.sum(dpo * yraw, axis=0, keepdims=True)
        dyraw = (dpo * scale_ref[...]).astype(BF16)
        for gi, width in enumerate(POOL_WINDOWS):
            lanes = slice(gi * POOL_GROUP, (gi + 1) * POOL_GROUP)
            dw_ref[gi] += _dot_tn(dsb[gi], dyraw[:, lanes])
            dd = _dot_nt(dyraw[:, lanes], wb[gi])
            ddc = dd / _pool_counts(tile, tm, width)
            ext_ref[:tm, lanes] = ddc
            ext_ref[tm:, lanes] = nxt_ref[:, lanes]
            s = ext_ref[:, lanes]
            shift = 1
            while shift < width:
                s = s + pltpu.roll(s, HALO + tm - shift, axis=0)
                shift *= 2
            du_ref[:, lanes] = (s[:tm, :] - dd).astype(BF16)
            nxt_ref[:, lanes] = ddc[:HALO, :]

    return pl.pallas_call(
        body, name="pool_bwd", grid=(n_tiles,),
        in_specs=[pl.BlockSpec((tm, POOL_WIDTH), lambda i: (n_tiles - 1 - i, 0)),
                  pl.BlockSpec((tm, POOL_WIDTH), lambda i: (n_tiles - 1 - i, 0)),
                  pl.BlockSpec((HALO, POOL_WIDTH), lambda i: (jnp.maximum((n_tiles - 1 - i) * halo_blocks - 1, 0), 0)),
                  _full((len(POOL_WINDOWS), POOL_GROUP, POOL_GROUP)), _full((1, POOL_WIDTH)), _full((1, POOL_WIDTH))],
        out_specs=[pl.BlockSpec((tm, POOL_WIDTH), lambda i: (n_tiles - 1 - i, 0)),
                   _full((len(POOL_WINDOWS), POOL_GROUP, POOL_GROUP)), _full((1, POOL_WIDTH)), _full((1, POOL_WIDTH))],
        out_shape=[jax.ShapeDtypeStruct((S, POOL_WIDTH), BF16),
                   jax.ShapeDtypeStruct((len(POOL_WINDOWS), POOL_GROUP, POOL_GROUP), F32),
                   jax.ShapeDtypeStruct((1, POOL_WIDTH), F32), jax.ShapeDtypeStruct((1, POOL_WIDTH), F32)],
        scratch_shapes=[pltpu.VMEM((HALO + tm, POOL_WIDTH), F32), pltpu.VMEM((HALO, POOL_WIDTH), F32)],
        compiler_params=_params(dimension_semantics=("arbitrary",)),
    )(dy, u, u, w_pool, pool_scale, g_pool)


def _variant(head):
    return 2 * (head // (N_HEADS // N_KV_HEADS)) + head % 2


def _own_block(shape=(BLOCK, BLOCK)):
    r = lax.broadcasted_iota(jnp.int32, shape, 0)
    i = lax.broadcasted_iota(jnp.int32, shape, 1)
    return r <= i


def _fold_band(own, from_own, from_prev):
    return jnp.where(own, from_own, from_prev)


def _scores_by_head(own_tiles, prev_tiles, q_tiles):
    stacks = [jnp.concatenate(q_tiles[:2], axis=0), jnp.concatenate(q_tiles[2:], axis=0)]
    by_var = [_dot_nt(jnp.concatenate([own_tiles[v], prev_tiles[v]], axis=0), stacks[v // 2])
              for v in range(2 * N_KV_HEADS)]
    quadrant = lambda h, rows: by_var[_variant(h)][rows * BLOCK:(rows + 1) * BLOCK,
                                                   ((h // 2) % 2) * BLOCK:((h // 2) % 2 + 1) * BLOCK]
    return [quadrant(h, 0) for h in range(N_HEADS)], [quadrant(h, 1) for h in range(N_HEADS)]


def _softmax_t(s, sink):
    m = jnp.maximum(jnp.max(s, axis=0, keepdims=True), sink)
    p = jnp.exp(s - m)
    p_sink = jnp.exp(sink - m)
    inv = 1.0 / (jnp.sum(p, axis=0, keepdims=True) + p_sink)
    return p * inv, p_sink * inv


def _attn_fwd(q, kz, vz, sinks, g_attn, y_pool, x, w_out, mine, post):
    S = q.shape[0]
    tq = ATTN_BLOCKS * BLOCK
    n_var = 2 * N_KV_HEADS

    def body(q_ref, kp_ref, kc_ref, vp_ref, vc_ref, sinks_ref, gain_ref, yp_ref, x_ref, w_land, w_own, mine_ref,
             post_ref, o_ref, out_ref, m_ref, y_ref, w_v, sem):
        _load_once(_gathered(w_land, w_own, w_v, mine_ref[0], rows=D_MODEL // N_SHARD), sem)
        step = pl.program_id(0)
        own = _own_block()
        zero = jnp.zeros((BLOCK, BLOCK), F32)

        def tiles(cur_ref, prev_ref, j):
            rows = lambda jj: slice(jj * BLOCK, (jj + 1) * BLOCK)
            return ([cur_ref[v, rows(j), :] for v in range(n_var)],
                    [prev_ref[v] if j == 0 else cur_ref[v, rows(j - 1), :] for v in range(n_var)])

        scores = []
        for j in range(ATTN_BLOCKS):
            q_pairs = [q_ref[j * BLOCK:(j + 1) * BLOCK, i * _LANES:(i + 1) * _LANES] for i in range(N_HEADS // 2)]
            scores.append(_scores_by_head(*tiles(kc_ref, kp_ref, j), q_pairs))
        probs = []
        for j in range(ATTN_BLOCKS):
            s_own, s_prev = scores[j]
            no_prev = jnp.where(step > 0, 0.0, NEG_INF) if j == 0 else 0.0
            p_own, p_prev = [], []
            for h in range(N_HEADS):
                p, _ = _softmax_t(_fold_band(own, s_own[h], s_prev[h] + no_prev), sinks_ref[0, h])
                p_own.append(jnp.where(own, p, zero).astype(BF16))
                p_prev.append(jnp.where(own, zero, p).astype(BF16))
            probs.append((p_own, p_prev))
        blocks = []
        for j in range(ATTN_BLOCKS):
            p_own, p_prev = probs[j]
            v_own, v_prev = tiles(vc_ref, vp_ref, j)
            pairs = []
            for i in range(N_HEADS // 2):
                acc = None
                for h in (2 * i, 2 * i + 1):
                    part = _dot_tn(p_own[h], v_own[_variant(h)]) + _dot_tn(p_prev[h], v_prev[_variant(h)])
                    acc = part if acc is None else acc + part
                pairs.append(acc)
            blocks.append(jnp.concatenate(pairs, axis=1))
        o = jnp.concatenate(blocks, axis=0)
        o_ref[...] = o
        y_ref[:, :POOL_WIDTH] = yp_ref[...]
        y_ref[:, POOL_WIDTH:] = ((o * _rstd(o)) * gain_ref[...]).astype(BF16)
        m = _dot(y_ref[...], w_v[...])
        m_ref[...] = m
        out_ref[...] = x_ref[...] + (m * _rstd(m)) * post_ref[...]

    prev = pl.BlockSpec((n_var, BLOCK, KV_WIDTH), lambda g: (0, jnp.maximum(g * ATTN_BLOCKS - 1, 0), 0))
    cur = pl.BlockSpec((n_var, tq, KV_WIDTH), lambda g: (0, g, 0))
    return pl.pallas_call(
        body, name="attn_fwd", grid=(S // tq,),
        in_specs=[_rows(tq, ATTN_WIDTH), prev, cur, prev, cur,
                  pl.BlockSpec(memory_space=pltpu.SMEM), _full((1, ATTN_WIDTH)),
                  _rows(tq, POOL_WIDTH), _rows(tq, D_MODEL), _ANY, _ANY, _SMEM, _full((1, D_MODEL))],
        out_specs=[_rows(tq, ATTN_WIDTH), _rows(tq, D_MODEL), _rows(tq, D_MODEL), _rows(tq, D_MODEL)],
        out_shape=[jax.ShapeDtypeStruct((S, ATTN_WIDTH), F32), jax.ShapeDtypeStruct((S, D_MODEL), F32),
                   jax.ShapeDtypeStruct((S, D_MODEL), F32), jax.ShapeDtypeStruct((S, D_MODEL), BF16)],
        scratch_shapes=[pltpu.VMEM((D_MODEL, D_MODEL), BF16), pltpu.SemaphoreType.DMA((N_SHARD,))],
        compiler_params=_params(dimension_semantics=("arbitrary",)),
    )(q, kz, kz, vz, vz, sinks, g_attn, y_pool, x, *w_out, mine, post)


def _attn_bwd(dout, m, w_out, mine, post, o, q, kz, vz, sinks, g_attn, deps=()):
    S = q.shape[0]
    tq = ATTN_BLOCKS * BLOCK
    n_var = 2 * N_KV_HEADS

    def body(dout_ref, m_ref, w_land, w_own, mine_ref, post_ref, o_ref, q_ref, kp_ref, kc_ref, vp_ref, vc_ref,
             sinks_ref, gain_ref, dyp_ref, dm_ref, dpost_ref, dq_ref, dk_ref, dv_ref, dkx_ref, dvx_ref, dsink_ref,
             dgain_ref, w_v, sem):
        _load_once(_gathered(w_land, w_own, w_v, mine_ref[0], rows=D_MODEL // N_SHARD), sem)
        step = pl.program_id(0)

        @pl.when(step == 0)
        def _():
            dpost_ref[...] = jnp.zeros_like(dpost_ref)
            dsink_ref[...] = jnp.zeros_like(dsink_ref)
            dgain_ref[...] = jnp.zeros_like(dgain_ref)

        mv = m_ref[...]
        rm = _rstd(mv)
        mn = mv * rm
        dres = dout_ref[...]
        dpost_ref[...] += jnp.sum(dres * mn, axis=0, keepdims=True)
        dm = _norm_bwd(dres, mn, rm, post_ref[...]).astype(BF16)
        dm_ref[...] = dm
        dy = _dot_nt(dm, w_v[...])
        dyp_ref[...] = dy[:, :POOL_WIDTH]
        ov = o_ref[...]
        r = _rstd(ov)
        on = ov * r
        dyv = dy[:, POOL_WIDTH:]
        dgain_ref[...] += jnp.sum(dyv * on, axis=0, keepdims=True)
        do = _norm_bwd(dyv, on, r, gain_ref[...]).astype(BF16)
        own = _own_block()
        zero = jnp.zeros((BLOCK, BLOCK), F32)
        split = lambda t: (jnp.where(own, t, zero).astype(BF16), jnp.where(own, zero, t).astype(BF16))
        rows = lambda j: slice(j * BLOCK, (j + 1) * BLOCK)
        heads = range(N_HEADS)

        def tiles(cur_ref, prev_ref, j):
            return ([cur_ref[v, rows(j), :] for v in range(n_var)],
                    [prev_ref[v] if j == 0 else cur_ref[v, rows(j - 1), :] for v in range(n_var)])

        q_pairs = [[q_ref[rows(j), i * _LANES:(i + 1) * _LANES] for i in range(N_HEADS // 2)] for j in range(ATTN_BLOCKS)]
        do_pairs = [[do[rows(j), i * _LANES:(i + 1) * _LANES] for i in range(N_HEADS // 2)] for j in range(ATTN_BLOCKS)]
        scores = [(_scores_by_head(*tiles(kc_ref, kp_ref, j), q_pairs[j]),
                   _scores_by_head(*tiles(vc_ref, vp_ref, j), do_pairs[j])) for j in range(ATTN_BLOCKS)]
        parts, sink_sum = [], None
        for j in range(ATTN_BLOCKS):
            (s_own, s_prev), (dp_own, dp_prev) = scores[j]
            no_prev = jnp.where(step > 0, 0.0, NEG_INF) if j == 0 else 0.0
            ds_parts, p_parts, sink_rows = [], [], []
            for h in heads:
                p, p_sink = _softmax_t(_fold_band(own, s_own[h], s_prev[h] + no_prev), sinks_ref[0, h])
                dp = _fold_band(own, dp_own[h], dp_prev[h])
                delta = jnp.sum(p * dp, axis=0, keepdims=True)
                ds_parts.append(split(p * (dp - delta)))
                p_parts.append(split(p))
                sink_rows.append(jnp.zeros((1, _LANES), F32) - jnp.sum(p_sink * delta))
            block_sinks = jnp.concatenate(sink_rows, axis=0)
            sink_sum = block_sinks if sink_sum is None else sink_sum + block_sinks
            parts.append((ds_parts, p_parts))
        dsink_ref[...] += sink_sum
        low = lax.broadcasted_iota(jnp.int32, (BLOCK, _LANES), 1) < HEAD_DIM

        def merge(acc):
            return jnp.where(low, acc[0] + pltpu.roll(acc[1], HEAD_DIM, axis=1),
                             acc[3] + pltpu.roll(acc[2], HEAD_DIM, axis=1))
        add = lambda acc, var, t: acc.__setitem__(var, t if acc[var] is None else acc[var] + t)
        k_own, k_prev, v_own, v_prev = [], [], [], []
        for j in range(ATTN_BLOCKS):
            ds_parts, p_parts = parts[j]
            kt_own, kt_prev = tiles(kc_ref, kp_ref, j)
            dk_own, dk_prev, dv_own, dv_prev = ([None] * n_var for _ in range(4))
            for i in range(N_HEADS // 2):
                dq_pair = None
                for h in (2 * i, 2 * i + 1):
                    var = _variant(h)
                    (ds_o, ds_p), (p_o, p_p) = ds_parts[h], p_parts[h]
                    part = _dot_tn(ds_o, kt_own[var]) + _dot_tn(ds_p, kt_prev[var])
                    dq_pair = part if dq_pair is None else dq_pair + part
                    add(dk_own, var, _dot(ds_o, q_pairs[j][i]))
                    add(dk_prev, var, _dot(ds_p, q_pairs[j][i]))
                    add(dv_own, var, _dot(p_o, do_pairs[j][i]))
                    add(dv_prev, var, _dot(p_p, do_pairs[j][i]))
                dq_ref[rows(j), i * _LANES:(i + 1) * _LANES] = dq_pair * ATTN_SCALE
            k_own.append(merge(dk_own))
            k_prev.append(merge(dk_prev))
            v_own.append(merge(dv_own))
            v_prev.append(merge(dv_prev))
        for j in range(ATTN_BLOCKS):
            last = j == ATTN_BLOCKS - 1
            dk_ref[rows(j), :] = k_own[j] if last else k_own[j] + k_prev[j + 1]
            dv_ref[rows(j), :] = v_own[j] if last else v_own[j] + v_prev[j + 1]
        dkx_ref[...] = k_prev[0]
        dvx_ref[...] = v_prev[0]

    steps = S // tq
    prev = pl.BlockSpec((n_var, BLOCK, KV_WIDTH), lambda g: (0, jnp.maximum(g * ATTN_BLOCKS - 1, 0), 0))
    cur = pl.BlockSpec((n_var, tq, KV_WIDTH), lambda g: (0, g, 0))
    nxt = pl.BlockSpec((None, BLOCK, KV_WIDTH), lambda g: (g, 0, 0))
    args = [dout, m, *w_out, mine, post, o, q, kz, kz, vz, vz, sinks, g_attn]
    return pl.pallas_call(
        _ignoring(body, len(args), len(deps)), name="attn_bwd", grid=(steps,),
        in_specs=[_rows(tq, D_MODEL), _rows(tq, D_MODEL), _ANY, _ANY, _SMEM, _full((1, D_MODEL)),
                  _rows(tq, ATTN_WIDTH), _rows(tq, ATTN_WIDTH), prev, cur, prev, cur,
                  pl.BlockSpec(memory_space=pltpu.SMEM), _full((1, ATTN_WIDTH))] + [_ANY] * len(deps),
        out_specs=[_rows(tq, POOL_WIDTH), _rows(tq, D_MODEL), _full((1, D_MODEL)),
                   _rows(tq, ATTN_WIDTH), _rows(tq, KV_WIDTH), _rows(tq, KV_WIDTH), nxt, nxt,
                   _full((N_HEADS, _LANES)), _full((1, ATTN_WIDTH))],
        out_shape=[jax.ShapeDtypeStruct((S, POOL_WIDTH), F32), jax.ShapeDtypeStruct((S, D_MODEL), BF16),
                   jax.ShapeDtypeStruct((1, D_MODEL), F32),
                   jax.ShapeDtypeStruct((S, ATTN_WIDTH), F32), jax.ShapeDtypeStruct((S, KV_WIDTH), F32),
                   jax.ShapeDtypeStruct((S, KV_WIDTH), F32),
                   jax.ShapeDtypeStruct((steps, BLOCK, KV_WIDTH), F32), jax.ShapeDtypeStruct((steps, BLOCK, KV_WIDTH), F32),
                   jax.ShapeDtypeStruct((N_HEADS, _LANES), F32), jax.ShapeDtypeStruct((1, ATTN_WIDTH), F32)],
        scratch_shapes=[pltpu.VMEM((D_MODEL, D_MODEL), BF16), pltpu.SemaphoreType.DMA((N_SHARD,))],
        compiler_params=_params(dimension_semantics=("arbitrary",)),
    )(*args, *deps)


def _inv_freq_row():
    inv_freq = ROPE_THETA ** (-jnp.arange(0, ROT_DIM, 2, dtype=F32) / ROT_DIM)
    per_head = jnp.concatenate([inv_freq, inv_freq, jnp.zeros((HEAD_DIM - ROT_DIM,), F32)])
    return jnp.tile(per_head, _LANES // HEAD_DIM).reshape(1, _LANES)


def _local_step(x, pos, target, small, mine, weights_of, grads_ready):
    rope = _rope_tables(pos, _inv_freq_row())
    (wgu1,) = weights_of("ffn1_up", (rope,))
    g1, u1, a1 = _ffn_up(x, small["ffn1_pre"], wgu1, mine)
    (wd1,) = weights_of("ffn1_down", (a1,))
    x1, f1 = _ffn_down(x, a1, small["ffn1_post"], wd1, mine)
    w_in_t, w_out = weights_of("mixer", (x1,))
    u, q, kz, vz = _mixer_in_fwd(x1, small["mix_pre"], w_in_t, mine, rope)
    y_pool = _pool_fwd(u, small["w_pool"], small["pool_scale"], small["g_pool"])
    o, x2, m, y = _attn_fwd(q, kz, vz, small["sinks"], small["g_attn"], y_pool, x1, w_out, mine, small["mix_post"])
    wgu2, wd2 = weights_of("ffn2", (x2,))
    g2, u2, a2 = _ffn_up(x2, small["ffn2_pre"], wgu2, mine)
    dx3, df2, loss_acc, dpost2 = _ffn_down(x2, a2, small["ffn2_post"], wd2, mine, target=target)
    grads = {"loss": loss_acc * (0.5 / D_MODEL), "ffn2_post": dpost2}
    dx2, h3, dgu2, grads["ffn2_pre"] = _ffn_bwd(
        dx3, x2, df2, g2, u2, small["ffn2_pre"], small["ffn2_post"], wgu2, wd2, mine, df_known=True)
    dwgu2 = _wgrad(h3, dgu2, D_MODEL, FF_CHUNK, "wgrad_gu2", column_shards=True)
    dwd2 = _wgrad(a2, df2, FF_CHUNK, D_MODEL, "wgrad_down2")
    deps = grads_ready("ffn2", {"ffn2_w_gu": dwgu2, "ffn2_w_down": dwd2})
    dy_pool, dm, grads["mix_post"], dq, dk, dv, dk_next, dv_next, dsinks, grads["g_attn"] = _attn_bwd(
        dx2, m, w_out, mine, small["mix_post"], o, q, kz, vz, small["sinks"], small["g_attn"], deps=deps)
    dw_out = _wgrad(y, dm, D_MODEL, D_MODEL, "wgrad_out")
    grads["sinks"] = dsinks[:, 0].reshape(1, N_HEADS)
    du, grads["w_pool"], grads["pool_scale"], grads["g_pool"] = _pool_bwd(
        dy_pool, u, small["w_pool"], small["pool_scale"], small["g_pool"])
    dx1, dz, h2, grads["mix_pre"] = _mixer_in_bwd(dx2, x1, small["mix_pre"], w_in_t, mine, du, dq, dk, dv, dk_next, dv_next, rope)
    dw_in_t = _wgrad(dz, h2, IN_WIDTH, D_MODEL, "wgrad_in")
    deps = grads_ready("mixer", {"w_in": dw_in_t, "w_out": dw_out})
    dx, h1, dgu1, df1, grads["ffn1_pre"], grads["ffn1_post"] = _ffn_bwd(
        dx1, x, f1, g1, u1, small["ffn1_pre"], small["ffn1_post"], wgu1, wd1, mine, deps=deps)
    dwgu1 = _wgrad(h1, dgu1, D_MODEL, FF_CHUNK, "wgrad_gu1", column_shards=True)
    deps = grads_ready("ffn1_gu", {"ffn1_w_gu": dwgu1}, small=grads)
    dwd1 = _wgrad(a1, df1, FF_CHUNK, D_MODEL, "wgrad_down1", deps=deps)
    grads_ready("ffn1_down", {"ffn1_w_down": dwd1})
    return dx


def _place():
    return lax.axis_index("x"), lax.axis_index("y"), lax.axis_index("c")


def _other_chips(x, y):
    return [(1 - x, y), (x, 1 - y), (1 - x, 1 - y)]


_HBM = pl.BlockSpec(memory_space=pltpu.HBM)
_SEM = pl.BlockSpec(memory_space=pltpu.SEMAPHORE)
_EFFECT = pltpu.SideEffectType.DATAFLOW_SIDE_EFFECTING
GATHER, GATHER_HALF, REDUCE, BROADCAST = "gather", "gather_half", "reduce", "broadcast"
N_DEVICES = 8


def _in_hbm(a):
    return pltpu.with_memory_space_constraint(a, pltpu.HBM)


def _core_half(rows, c):
    return pl.ds(pl.multiple_of(c * (rows // 2), 16), rows // 2)


def _chip_copies(kind, srcs, lands, send_sems, recv_sems):
    x, y, c = _place()
    mine = 2 * x + y
    copies = []
    for w in range(len(srcs)):
        if kind == BROADCAST:
            peers = [(x ^ (k >> 2), y ^ ((k >> 1) & 1), c ^ (k & 1)) for k in range(1, N_DEVICES)]
        else:
            peers = [(px, py, c) for px, py in _other_chips(x, y)]
        for k, (px, py, pc) in enumerate(peers):
            if kind == GATHER:
                src, dst = srcs[w], lands[w].at[mine]
            elif kind == GATHER_HALF:
                half = _core_half(srcs[w].shape[0], c)
                src, dst = srcs[w].at[half, :], lands[w].at[mine, half, :]
            elif kind == BROADCAST:
                src, dst = srcs[w], lands[w].at[2 * mine + c]
            else:
                src, dst = srcs[w].at[2 * px + py], lands[w].at[k]
            pair = len(peers) * w + k
            copies.append(pltpu.make_async_remote_copy(
                src_ref=src, dst_ref=dst, send_sem=send_sems.at[pair], recv_sem=recv_sems.at[pair],
                device_id=(px, py, pc), device_id_type=MESH))
    return copies


def _landing_shape(kind, src):
    if kind == REDUCE:
        return (N_SHARD - 1,) + src.shape[1:]
    return ((N_DEVICES if kind == BROADCAST else N_SHARD),) + src.shape


def _peer_count(kind):
    return N_DEVICES - 1 if kind == BROADCAST else N_SHARD - 1


def _exchange_start(kinds, groups, name):
    sizes = [len(g) for g in groups]
    flat = [s for g in groups for s in g]
    n, ng = len(flat), len(groups)

    def body(*refs):
        srcs, lands = refs[:n], refs[n:2 * n]
        sems = refs[2 * n:2 * n + 2 * ng]
        token = refs[-1]
        start = 0
        for gi, size in enumerate(sizes):
            for cp in _chip_copies(kinds[gi], srcs[start:start + size], lands[start:start + size],
                                   sems[2 * gi], sems[2 * gi + 1]):
                cp.start()
            start += size
        token[...] = jnp.zeros_like(token)

    landings = [lax.empty(_landing_shape(kind, s), s.dtype) for kind, g in zip(kinds, groups) for s in g]
    sem_shapes = [pltpu.SemaphoreType.DMA((size * _peer_count(kind),)) for kind, size in zip(kinds, sizes)
                  for _ in range(2)]
    outs = pl.pallas_call(
        body, name=name,
        in_specs=[_HBM] * (2 * n),
        out_specs=[_SEM] * (2 * ng) + [_HBM] * (2 * n) + [pl.BlockSpec(memory_space=pltpu.VMEM)],
        out_shape=sem_shapes + [pltpu.HBM(a.shape, a.dtype) for a in flat + landings]
        + [jax.ShapeDtypeStruct((8, _LANES), F32)],
        input_output_aliases={i: 2 * ng + i for i in range(2 * n)},
        compiler_params=pltpu.CompilerParams(has_side_effects=_EFFECT),
    )(*[_in_hbm(a) for a in flat + landings])
    sems, srcs, lands, token = outs[:2 * ng], outs[2 * ng:2 * ng + n], outs[2 * ng + n:2 * ng + 2 * n], outs[-1]
    handles, start = [], 0
    for gi, size in enumerate(sizes):
        handles.append((sems[2 * gi], sems[2 * gi + 1], srcs[start:start + size], lands[start:start + size]))
        start += size
    return handles, token


def _exchange_wait(kind, handle, after, name):
    send_sems, recv_sems, srcs, lands = handle
    n = len(srcs)

    def body(*refs):
        copies = _chip_copies(kind, refs[:n], refs[n:2 * n], refs[2 * n], refs[2 * n + 1])
        for cp in copies:
            cp.wait_send()
        for cp in copies:
            cp.wait_recv()

    outs = pl.pallas_call(
        body, name=name,
        in_specs=[_HBM] * (2 * n) + [_SEM, _SEM] + [_ANY] * len(after),
        out_specs=[_HBM] * (2 * n),
        out_shape=[pltpu.HBM(a.shape, a.dtype) for a in list(srcs) + list(lands)],
        input_output_aliases={i: i for i in range(2 * n)},
        compiler_params=pltpu.CompilerParams(has_side_effects=_EFFECT),
    )(*srcs, *lands, send_sems, recv_sems, *after)
    return outs[:n], outs[n:]


def _swap_gathered_halves(lands, name):
    n = len(lands)

    def body(*refs):
        bufs = refs[n:2 * n]
        send_sems, recv_sems = refs[2 * n:]
        x, y, c = _place()
        mine = 2 * x + y
        sends, arrivals = [], []
        for w in range(n):
            rows = bufs[w].shape[1]
            for d in range(1, N_SHARD):
                slot = (mine + d) % N_SHARD
                sems = dict(send_sem=send_sems.at[(N_SHARD - 1) * w + d - 1],
                            recv_sem=recv_sems.at[(N_SHARD - 1) * w + d - 1],
                            device_id=(x, y, 1 - c), device_id_type=MESH)
                fetched = bufs[w].at[slot, _core_half(rows, c), :]
                missing = bufs[w].at[slot, _core_half(rows, 1 - c), :]
                sends.append(pltpu.make_async_remote_copy(src_ref=fetched, dst_ref=fetched, **sems))
                arrivals.append(pltpu.make_async_remote_copy(src_ref=missing, dst_ref=missing, **sems))
        for cp in sends:
            cp.start()
        for cp in arrivals:
            cp.wait_recv()
        for cp in sends:
            cp.wait_send()

    return pl.pallas_call(
        body, name=name, in_specs=[_ANY] * n, out_specs=[_ANY] * n,
        out_shape=[jax.ShapeDtypeStruct(a.shape, a.dtype) for a in lands],
        input_output_aliases={i: i for i in range(n)},
        scratch_shapes=[pltpu.SemaphoreType.DMA((n * (N_SHARD - 1),)), pltpu.SemaphoreType.DMA((n * (N_SHARD - 1),))],
        compiler_params=pltpu.CompilerParams(has_side_effects=True),
    )(*lands)


def _swap_with_sibling(partials, name):
    n = len(partials)

    def body(*refs):
        ins, outs = refs[:n], refs[n:2 * n]
        send_sems, recv_sems = refs[2 * n:]
        x, y, c = _place()
        sends = [pltpu.make_async_remote_copy(
            src_ref=ins[w], dst_ref=outs[w], send_sem=send_sems.at[w], recv_sem=recv_sems.at[w],
            device_id=(x, y, 1 - c), device_id_type=MESH) for w in range(n)]
        for cp in sends:
            cp.start()
        for cp in sends:
            cp.wait_recv()
        for cp in sends:
            cp.wait_send()

    return pl.pallas_call(
        body, name=name,
        in_specs=[_ANY] * n, out_specs=[_ANY] * n,
        out_shape=[jax.ShapeDtypeStruct(p.shape, p.dtype) for p in partials],
        scratch_shapes=[pltpu.SemaphoreType.DMA((n,)), pltpu.SemaphoreType.DMA((n,))],
        compiler_params=pltpu.CompilerParams(has_side_effects=True),
    )(*partials)


def _row_block(rows, cap):
    best = None
    for cand in range(16, min(rows, cap) + 1, 16):
        if rows % cand == 0:
            best = cand
    assert best is not None, rows
    return best


def _chip_partial(own, received, shard, name):
    _, R, C = own.shape
    rb = _row_block(R, 512)

    def body(shard_ref, own_ref, rec_ref, out_ref):
        acc = own_ref[...]
        for k in range(3):
            acc = acc + rec_ref[k].astype(F32)
        out_ref[...] = acc.astype(BF16)

    return pl.pallas_call(
        body, name=name,
        grid_spec=pltpu.PrefetchScalarGridSpec(
            num_scalar_prefetch=1, grid=(R // rb,),
            in_specs=[pl.BlockSpec((None, rb, C), lambda i, s: (s[0], i, 0)),
                      pl.BlockSpec((3, rb, C), lambda i, s: (0, i, 0))],
            out_specs=pl.BlockSpec((rb, C), lambda i, s: (i, 0))),
        out_shape=jax.ShapeDtypeStruct((R, C), BF16),
        compiler_params=_params(dimension_semantics=("arbitrary",)),
    )(shard, own, received)


def _adamw(w, m, v, g_parts, name, slot=None):
    R, C = w.shape
    by_device = slot is not None
    rb = _row_block(R, 512) if R % 16 == 0 else R

    def body(w_ref, m_ref, v_ref, *refs):
        g_refs, (grad_ref, delta_ref, m_out, v_out) = refs[:-4], refs[-4:]
        if by_device:
            own_ref, land_ref, slot_ref = g_refs
            part = lambda d: jnp.where(slot_ref[0] == d, own_ref[...], land_ref[d])
            g = part(0)
            for d in range(1, N_DEVICES):
                g = g + part(d)
        else:
            g = g_refs[0][...].astype(F32)
            for g_ref in g_refs[1:]:
                g = g + g_ref[...].astype(F32)
        grad_ref[...] = g
        new_m = ADAM_B1 * m_ref[...] + (1.0 - ADAM_B1) * g
        new_v = ADAM_B2 * v_ref[...] + (1.0 - ADAM_B2) * (g * g)
        m_hat = new_m / (1.0 - ADAM_B1 ** ADAM_STEP)
        v_hat = new_v / (1.0 - ADAM_B2 ** ADAM_STEP)
        delta_ref[...] = -ADAM_LR * (m_hat / (jnp.sqrt(v_hat) + ADAM_EPS) + ADAM_WD * w_ref[...])
        m_out[...] = new_m
        v_out[...] = new_v

    spec = pl.BlockSpec((rb, C), lambda i: (i, 0))
    if by_device:
        g_specs = [spec, pl.BlockSpec((N_DEVICES, rb, C), lambda i: (0, i, 0)), _SMEM]
        g_parts = list(g_parts) + [slot]
    else:
        g_specs = [spec] * len(g_parts)
    return pl.pallas_call(
        body, name=name, grid=(R // rb,),
        in_specs=[spec, spec, spec] + g_specs,
        out_specs=[spec] * 4,
        out_shape=[jax.ShapeDtypeStruct((R, C), F32)] * 4,
        compiler_params=_params(dimension_semantics=("arbitrary",)),
    )(w, m, v, *g_parts)


SMALL_NAMES = ("ffn1_pre", "ffn1_post", "mix_pre", "pool_scale", "sinks", "g_pool", "g_attn", "mix_post",
               "ffn2_pre", "ffn2_post", "w_pool")
_SLAB_PART = 8 * _LANES


SLAB_NAMES = SMALL_NAMES + ("loss",)


def _to_slab(parts):
    rows = []
    for name in SLAB_NAMES:
        flat = parts[name].reshape(-1) if name in parts else jnp.zeros((_SLAB_PART,), F32)
        padded = -(-flat.shape[0] // _SLAB_PART) * _SLAB_PART
        rows.append(jnp.pad(flat, (0, padded - flat.shape[0])).reshape(-1, _LANES))
    return jnp.concatenate(rows, axis=0)


def _from_slab(slab, like):
    out, row = {}, 0
    for name in SLAB_NAMES:
        size = like[name].size
        rows = -(-size // _SLAB_PART) * (_SLAB_PART // _LANES)
        out[name] = slab[row:row + rows].reshape(-1)[:size].reshape(like[name].shape)
        row += rows
    return out


BIG_NAMES = ("ffn1_w_gu", "ffn1_w_down", "w_in", "w_out", "ffn2_w_gu", "ffn2_w_down")
WEIGHT_ORDER = ("ffn1_pre", "ffn1_w_gu", "ffn1_w_down", "ffn1_post", "mix_pre", "w_in", "w_pool", "pool_scale",
                "sinks", "g_pool", "g_attn", "w_out", "mix_post", "ffn2_pre", "ffn2_w_gu", "ffn2_w_down", "ffn2_post")


def kernel(x, positions, ffn1_pre, ffn1_w_gu, ffn1_w_down, ffn1_post, mix_pre, w_in, w_pool, pool_scale, sinks, g_pool, g_attn, w_out, mix_post, ffn2_pre, ffn2_w_gu, ffn2_w_down, ffn2_post, loss_target, m_ffn1_pre, m_ffn1_w_gu, m_ffn1_w_down, m_ffn1_post, m_mix_pre, m_w_in, m_w_pool, m_pool_scale, m_sinks, m_g_pool, m_g_attn, m_w_out, m_mix_post, m_ffn2_pre, m_ffn2_w_gu, m_ffn2_w_down, m_ffn2_post, v_ffn1_pre, v_ffn1_w_gu, v_ffn1_w_down, v_ffn1_post, v_mix_pre, v_w_in, v_w_pool, v_pool_scale, v_sinks, v_g_pool, v_g_attn, v_w_out, v_mix_post, v_ffn2_pre, v_ffn2_w_gu, v_ffn2_w_down, v_ffn2_post):
    given = dict(locals())
    weights = {n: given[n][0] for n in WEIGHT_ORDER}
    moments_m = {n: given["m_" + n][0] for n in WEIGHT_ORDER}
    moments_v = {n: given["v_" + n][0] for n in WEIGHT_ORDER}
    S = x.shape[1]
    shard = (2 * lax.axis_index("x") + lax.axis_index("y")).astype(jnp.int32).reshape(1)

    local16 = {n: weights[n].astype(BF16) for n in BIG_NAMES if n != "w_in"}
    local16["w_in"] = weights["w_in"].T.astype(BF16)
    gather_groups = {"ffn1_up": ("ffn1_w_gu",), "ffn1_down": ("ffn1_w_down",), "mixer": ("w_in", "w_out"),
                     "ffn2": ("ffn2_w_gu", "ffn2_w_down")}
    gather_kinds = {"ffn1_up": GATHER_HALF, "ffn1_down": GATHER, "mixer": GATHER, "ffn2": GATHER}
    handles, _ = _exchange_start(list(gather_kinds.values()),
                                 [[local16[n] for n in names] for names in gather_groups.values()], "gather_start")
    gather_handles = dict(zip(gather_groups, handles))

    def weights_of(group, after):
        kind = gather_kinds[group]
        owns, lands = _exchange_wait(kind, gather_handles[group], list(after), "gather_wait_" + group)
        if kind == GATHER_HALF:
            lands = _swap_gathered_halves(lands, "swap_gathered_" + group)
        return list(zip(lands, owns))

    pending, last_token = {}, []

    def grads_ready(group, grads, small=None):
        names = list(grads)
        kinds, sources = [REDUCE], [[grads[n][1] for n in names]]
        if small is not None:
            kinds, sources = kinds + [BROADCAST], sources + [[_to_slab(small)]]
        handles, token = _exchange_start(kinds, sources, "reduce_start_" + group)
        handle = handles[0]
        if small is not None:
            pending["small"] = handles[1]
        pending[group] = (names, handle, [grads[n][0] for n in names])
        last_token[:] = [token]
        return [token]

    small = {n: (weights[n] if weights[n].ndim > 1 else weights[n].reshape(1, -1)) for n in SMALL_NAMES}
    dx = _local_step(x[0], positions.reshape(S, 1), loss_target[0], small, shard, weights_of, grads_ready)

    grad, delta, new_m, new_v = {}, {}, {}, {}

    def finish(groups, after):
        names, partials = [], []
        for group in groups:
            group_names, handle, own32 = pending[group]
            _, received = _exchange_wait(REDUCE, handle, after, "reduce_wait_" + group)
            names += group_names
            partials += [_chip_partial(g32, rec, shard, "chip_partial_" + n)
                         for n, g32, rec in zip(group_names, own32, received)]
        siblings = _swap_with_sibling(partials, "swap_" + groups[0])
        for name, mine, theirs in zip(names, partials, siblings):
            if name == "w_in":
                mine, theirs = mine.T, theirs.T
            grad[name], delta[name], new_m[name], new_v[name] = _adamw(
                weights[name], moments_m[name], moments_v[name], [mine, theirs], "adamw_" + name)
        return [grad[names[-1]]]

    after = finish(["ffn2"], last_token)
    after = finish(["mixer"], after)
    (own_slab,), (slab_landing,) = _exchange_wait(BROADCAST, pending["small"], after, "reduce_wait_small")
    device = (2 * shard + lax.axis_index("c")).astype(jnp.int32)
    small_like = dict({n: small[n] for n in SMALL_NAMES}, loss=jnp.zeros((8, _LANES), F32))
    slabs = _adamw(_to_slab(small), _to_slab({n: moments_m[n] for n in SMALL_NAMES}),
                   _to_slab({n: moments_v[n] for n in SMALL_NAMES}), [own_slab, slab_landing], "adamw_small",
                   slot=device)
    for store, slab in zip((grad, delta, new_m, new_v), slabs):
        store.update(_from_slab(slab, small_like))
    loss = grad["loss"][0, 0]
    after = finish(["ffn1_gu"], [slabs[0]])
    finish(["ffn1_down"], after)

    def out(store):
        return [store[n].reshape(given[n].shape) for n in WEIGHT_ORDER]
    return (loss, dx[None], *out(grad), *out(delta), *out(new_m), *out(new_v))
```

```python
import jax
import jax.numpy as jnp
from jax import lax
from jax.experimental import pallas as pl
from jax.experimental.pallas import tpu as pltpu

F32 = jnp.float32
BF16 = jnp.bfloat16

D_MODEL = 1024
D_FF = 2816
N_SHARD = 4
FF_CHUNK = D_FF // 2
POOL_WINDOWS = (2, 4, 8, 16)
POOL_WIDTH = 512
POOL_GROUP = 128
HALO = 16
HEAD_DIM = 64
N_HEADS = 8
N_KV_HEADS = 2
ATTN_WIDTH = 512
KV_WIDTH = 128
IN_WIDTH = 1280
BLOCK = 128
ATTN_BLOCKS = 4
ROT_DIM = 16
ROPE_THETA = 500000.0
EPS = 1e-6
NEG_INF = -1e30
ATTN_SCALE = HEAD_DIM ** -0.5

ADAM_LR = 0.001
ADAM_B1 = 0.9
ADAM_B2 = 0.999
ADAM_EPS = 1e-08
ADAM_WD = 0.01
ADAM_STEP = 10

VMEM_LIMIT = 60 * 1024 * 1024
FFN_UP_TILE = 512
FFN_DOWN_TILE = 1024
FFN_BWD_TILE = 256
MIXER_TILE = 512

MESH = pl.DeviceIdType.MESH


def _params(**kw):
    return pltpu.CompilerParams(vmem_limit_bytes=VMEM_LIMIT, **kw)


def _dot(a, b):
    return jnp.dot(a, b, preferred_element_type=F32)


def _dot_nt(a, b):
    return lax.dot_general(a, b, (((1,), (1,)), ((), ())), preferred_element_type=F32)


def _dot_tn(a, b):
    return lax.dot_general(a, b, (((0,), (0,)), ((), ())), preferred_element_type=F32)


def _rstd(x):
    return lax.rsqrt(jnp.mean(x * x, axis=-1, keepdims=True) + EPS)


def _norm_bwd(dy, xn, r, gain):
    dxn = dy * gain
    return r * (dxn - xn * jnp.mean(dxn * xn, axis=-1, keepdims=True))


def _sigmoid(x):
    return 1.0 / (1.0 + jnp.exp(-x))


def _full(shape):
    return pl.BlockSpec(shape, lambda *_: (0,) * len(shape))


def _rows(tile, width):
    return pl.BlockSpec((tile, width), lambda i: (i, 0))


_ANY = pl.BlockSpec(memory_space=pl.ANY)


_SMEM = pl.BlockSpec(memory_space=pltpu.SMEM)


def _load_once(pairs, sem):
    @pl.when(pl.program_id(0) == 0)
    def _():
        copies = [pltpu.make_async_copy(src, dst, sem.at[n]) for n, (src, dst) in enumerate(pairs)]
        for cp in copies:
            cp.start()
        for cp in copies:
            cp.wait()


def _gathered(land_ref, own_ref, vmem_ref, mine, rows=None):
    def dst(slot):
        if rows is None:
            return vmem_ref.at[slot]
        return vmem_ref.at[pl.ds(pl.multiple_of(slot * rows, 16), rows), :]
    pairs = [(land_ref.at[(mine + d) % N_SHARD], dst((mine + d) % N_SHARD)) for d in range(1, N_SHARD)]
    return pairs + [(own_ref, dst(mine))]


def _ignoring(body, start, count):
    def wrapped(*refs):
        return body(*refs[:start], *refs[start + count:])
    return wrapped


def _ffn_up(x, pre, wgu, mine, deps=()):
    S = x.shape[0]
    tm = FFN_UP_TILE

    def body(x_ref, pre_ref, wgu_land, wgu_own, mine_ref, g_ref, u_ref, a_ref, wgu_v, sem):
        _load_once(_gathered(wgu_land, wgu_own, wgu_v, mine_ref[0]), sem)
        xv = x_ref[...]
        h = ((xv * _rstd(xv)) * pre_ref[...]).astype(BF16)
        for c in range(2):
            cols = slice(c * FF_CHUNK, (c + 1) * FF_CHUNK)
            g = _dot(h, wgu_v[c])
            u = _dot(h, wgu_v[2 + c])
            g_ref[:, cols] = g.astype(BF16)
            u_ref[:, cols] = u.astype(BF16)
            a_ref[:, cols] = ((g * _sigmoid(g)) * u).astype(BF16)

    args = [x, pre, *wgu, mine]
    return pl.pallas_call(
        _ignoring(body, len(args), len(deps)), name="ffn_up", grid=(S // tm,),
        in_specs=[_rows(tm, D_MODEL), _full((1, D_MODEL)), _ANY, _ANY, _SMEM] + [_ANY] * len(deps),
        out_specs=[_rows(tm, D_FF)] * 3,
        out_shape=[jax.ShapeDtypeStruct((S, D_FF), BF16)] * 3,
        scratch_shapes=[pltpu.VMEM((N_SHARD, D_MODEL, FF_CHUNK), BF16), pltpu.SemaphoreType.DMA((N_SHARD,))],
        compiler_params=_params(dimension_semantics=("arbitrary",)),
    )(*args, *deps)


def _ffn_down(x, a, post, wd, mine, target=None, deps=()):
    S = x.shape[0]
    tm = FFN_DOWN_TILE
    with_loss = target is not None

    def body(*refs):
        if with_loss:
            (x_ref, a_ref, post_ref, wd_land, wd_own, mine_ref, tgt_ref,
             out_ref, f_ref, loss_ref, dpost_ref, wd_v, sem) = refs
        else:
            x_ref, a_ref, post_ref, wd_land, wd_own, mine_ref, out_ref, f_ref, wd_v, sem = refs
        _load_once(_gathered(wd_land, wd_own, wd_v, mine_ref[0], rows=D_FF // N_SHARD), sem)
        xv = x_ref[...]
        facc = _dot(a_ref[...], wd_v[...])
        rf = _rstd(facc)
        fn = facc * rf
        out = xv + 0.5 * (fn * post_ref[...])
        if with_loss:
            diff = out - tgt_ref[...]
            dout = diff * (1.0 / D_MODEL)
            out_ref[...] = dout

            @pl.when(pl.program_id(0) == 0)
            def _():
                loss_ref[...] = jnp.zeros_like(loss_ref)
                dpost_ref[...] = jnp.zeros_like(dpost_ref)
            loss_ref[...] += jnp.sum(diff * diff)
            dn = 0.5 * dout
            dpost_ref[...] += jnp.sum(dn * fn, axis=0, keepdims=True)
            f_ref[...] = _norm_bwd(dn, fn, rf, post_ref[...]).astype(BF16)
        else:
            f_ref[...] = facc
            out_ref[...] = out

    in_specs = [_rows(tm, D_MODEL), _rows(tm, D_FF), _full((1, D_MODEL)), _ANY, _ANY, _SMEM]
    args = [x, a, post, *wd, mine]
    out_shape = [jax.ShapeDtypeStruct((S, D_MODEL), F32), jax.ShapeDtypeStruct((S, D_MODEL), F32)]
    out_specs = [_rows(tm, D_MODEL), _rows(tm, D_MODEL)]
    if with_loss:
        in_specs.append(_rows(tm, D_MODEL))
        args.append(target)
        out_shape[1] = jax.ShapeDtypeStruct((S, D_MODEL), BF16)
        out_shape += [jax.ShapeDtypeStruct((8, 128), F32), jax.ShapeDtypeStruct((1, D_MODEL), F32)]
        out_specs += [_full((8, 128)), _full((1, D_MODEL))]
    return pl.pallas_call(
        _ignoring(body, len(args), len(deps)), name="ffn_down_loss" if with_loss else "ffn_down",
        grid=(S // tm,), in_specs=in_specs + [_ANY] * len(deps), out_specs=out_specs, out_shape=out_shape,
        scratch_shapes=[pltpu.VMEM((D_FF, D_MODEL), BF16), pltpu.SemaphoreType.DMA((N_SHARD,))],
        compiler_params=_params(dimension_semantics=("arbitrary",)),
    )(*args, *deps)


def _ffn_bwd(dout, x, f, g, u, pre, post, wgu, wd, mine, df_known=False, deps=()):
    S = x.shape[0]
    tm = FFN_BWD_TILE

    def body(*refs):
        if df_known:
            (dout_ref, x_ref, f_ref, g_ref, u_ref, pre_ref, post_ref, wgu_land, wgu_own, wd_land, wd_own, mine_ref,
             dx_ref, h_ref, dgu_ref, dpre_ref, wgu_v, wd_v, sem) = refs
        else:
            (dout_ref, x_ref, f_ref, g_ref, u_ref, pre_ref, post_ref, wgu_land, wgu_own, wd_land, wd_own, mine_ref,
             dx_ref, h_ref, dgu_ref, df_ref, dpre_ref, dpost_ref, wgu_v, wd_v, sem) = refs
        _load_once(_gathered(wgu_land, wgu_own, wgu_v, mine_ref[0])
                   + _gathered(wd_land, wd_own, wd_v, mine_ref[0], rows=D_FF // N_SHARD), sem)

        @pl.when(pl.program_id(0) == 0)
        def _():
            dpre_ref[...] = jnp.zeros_like(dpre_ref)
            if not df_known:
                dpost_ref[...] = jnp.zeros_like(dpost_ref)

        dout_v = dout_ref[...]
        if df_known:
            df = f_ref[...]
        else:
            dn = 0.5 * dout_v
            fv = f_ref[...]
            rf = _rstd(fv)
            fn = fv * rf
            dpost_ref[...] += jnp.sum(dn * fn, axis=0, keepdims=True)
            df = _norm_bwd(dn, fn, rf, post_ref[...]).astype(BF16)
            df_ref[...] = df
        dh = jnp.zeros((tm, D_MODEL), F32)
        for c in range(2):
            cols = slice(c * FF_CHUNK, (c + 1) * FF_CHUNK)
            da = _dot_nt(df, wd_v[cols, :])
            gv = g_ref[:, cols].astype(F32)
            uv = u_ref[:, cols].astype(F32)
            sg = _sigmoid(gv)
            silu = gv * sg
            dg = ((da * uv) * (sg * (1.0 + gv * (1.0 - sg)))).astype(BF16)
            du = (da * silu).astype(BF16)
            dgu_ref[:, cols] = dg
            dgu_ref[:, 2 * FF_CHUNK + c * FF_CHUNK:2 * FF_CHUNK + (c + 1) * FF_CHUNK] = du
            dh = dh + _dot_nt(dg, wgu_v[c]) + _dot_nt(du, wgu_v[2 + c])
        xv = x_ref[...]
        rx = _rstd(xv)
        xn = xv * rx
        h_ref[...] = (xn * pre_ref[...]).astype(BF16)
        dpre_ref[...] += jnp.sum(dh * xn, axis=0, keepdims=True)
        dx_ref[...] = dout_v + _norm_bwd(dh, xn, rx, pre_ref[...])

    args = [dout, x, f, g, u, pre, post, *wgu, *wd, mine]
    out_specs = [_rows(tm, D_MODEL), _rows(tm, D_MODEL), _rows(tm, 2 * D_FF), _rows(tm, D_MODEL),
                 _full((1, D_MODEL)), _full((1, D_MODEL))]
    out_shape = [jax.ShapeDtypeStruct((S, D_MODEL), F32), jax.ShapeDtypeStruct((S, D_MODEL), BF16),
                 jax.ShapeDtypeStruct((S, 2 * D_FF), BF16), jax.ShapeDtypeStruct((S, D_MODEL), BF16),
                 jax.ShapeDtypeStruct((1, D_MODEL), F32), jax.ShapeDtypeStruct((1, D_MODEL), F32)]
    if df_known:
        out_specs = out_specs[:3] + out_specs[4:5]
        out_shape = out_shape[:3] + out_shape[4:5]
    return pl.pallas_call(
        _ignoring(body, len(args), len(deps)), name="ffn_bwd_from_df" if df_known else "ffn_bwd", grid=(S // tm,),
        in_specs=[_rows(tm, D_MODEL), _rows(tm, D_MODEL), _rows(tm, D_MODEL), _rows(tm, D_FF), _rows(tm, D_FF),
                  _full((1, D_MODEL)), _full((1, D_MODEL)), _ANY, _ANY, _ANY, _ANY, _SMEM] + [_ANY] * len(deps),
        out_specs=out_specs, out_shape=out_shape,
        scratch_shapes=[pltpu.VMEM((N_SHARD, D_MODEL, FF_CHUNK), BF16), pltpu.VMEM((D_FF, D_MODEL), BF16),
                        pltpu.SemaphoreType.DMA((2 * N_SHARD,))],
        compiler_params=_params(dimension_semantics=("arbitrary",)),
    )(*args, *deps)


def _wgrad(lhs, rhs, m_block, n_block, name, column_shards=False, tk=2048, deps=()):
    S, M = lhs.shape
    N = rhs.shape[1]
    k_steps = S // tk

    def body(lhs_ref, rhs_ref, out_ref, out16_ref):
        k = pl.program_id(2)

        @pl.when(k == 0)
        def _():
            out_ref[...] = jnp.zeros_like(out_ref)
        out_ref[...] += _dot_tn(lhs_ref[...], rhs_ref[...])

        @pl.when(k == k_steps - 1)
        def _():
            out16_ref[...] = out_ref[...].astype(BF16)

    if column_shards:
        assert N == N_SHARD * n_block
        shape = (N_SHARD, M, n_block)
        out_spec = pl.BlockSpec((None, m_block, n_block), lambda i, j, k: (j, i, 0))
    else:
        shape = (M, N)
        out_spec = pl.BlockSpec((m_block, n_block), lambda i, j, k: (i, j))
    out, out16 = pl.pallas_call(
        _ignoring(body, 2, len(deps)), name=name, grid=(M // m_block, N // n_block, k_steps),
        in_specs=[pl.BlockSpec((tk, m_block), lambda i, j, k: (k, i)),
                  pl.BlockSpec((tk, n_block), lambda i, j, k: (k, j))] + [_ANY] * len(deps),
        out_specs=[out_spec, out_spec],
        out_shape=[jax.ShapeDtypeStruct(shape, F32), jax.ShapeDtypeStruct(shape, BF16)],
        compiler_params=_params(dimension_semantics=("arbitrary", "arbitrary", "arbitrary")),
    )(lhs, rhs, *deps)
    if not column_shards:
        out = out.reshape(N_SHARD, M // N_SHARD, N)
        out16 = out16.reshape(N_SHARD, M // N_SHARD, N)
    return out, out16


def _rope_tables(pos, invf):
    S = pos.shape[0]
    tm = MIXER_TILE

    def body(pos_ref, invf_ref, out_ref):
        ang = pos_ref[...].astype(F32) * invf_ref[...]
        cos, sin = jnp.cos(ang), jnp.sin(ang)
        lane = lax.broadcasted_iota(jnp.int32, ang.shape, 1) % HEAD_DIM
        first = lane < ROT_DIM // 2
        second = (lane >= ROT_DIM // 2) & (lane < ROT_DIM)
        out_ref[0] = jnp.where(lane < ROT_DIM, cos, 1.0)
        out_ref[1] = jnp.where(first, sin, 0.0)
        out_ref[2] = jnp.where(second, sin, 0.0)

    return pl.pallas_call(
        body, name="rope_tables", grid=(S // tm,),
        in_specs=[_rows(tm, 1), _full((1, _LANES))],
        out_specs=pl.BlockSpec((3, tm, _LANES), lambda i: (0, i, 0)),
        out_shape=jax.ShapeDtypeStruct((3, S, _LANES), F32),
        compiler_params=_params(dimension_semantics=("arbitrary",)),
    )(pos, invf)


def _table_spec(tm):
    return pl.BlockSpec((3, tm, _LANES), lambda i: (0, i, 0))


_HALF = ROT_DIM // 2
_LANES = 128


def _rope(t, tables):
    c, s_first, s_second = tables
    return t * c - pltpu.roll(t, _LANES - _HALF, axis=1) * s_first + pltpu.roll(t, _HALF, axis=1) * s_second


def _rope_transposed(t, tables):
    c, s_first, s_second = tables
    return t * c - pltpu.roll(t * s_first, _HALF, axis=1) + pltpu.roll(t * s_second, _LANES - _HALF, axis=1)


def _store_head_variants(ref, t):
    rolled = pltpu.roll(t, HEAD_DIM, axis=1)
    low = lax.broadcasted_iota(jnp.int32, t.shape, 1) < HEAD_DIM
    zero = jnp.zeros_like(t)
    ref[0] = jnp.where(low, t, zero).astype(BF16)
    ref[1] = jnp.where(low, zero, rolled).astype(BF16)
    ref[2] = jnp.where(low, rolled, zero).astype(BF16)
    ref[3] = jnp.where(low, zero, t).astype(BF16)


def _mixer_in_fwd(x, pre, w_in_t, mine, rope, deps=()):
    S = x.shape[0]
    tm = MIXER_TILE

    def body(x_ref, pre_ref, w_land, w_own, mine_ref, rope_ref, u_ref, q_ref, k_ref, v_ref, w_v, sem):
        _load_once(_gathered(w_land, w_own, w_v, mine_ref[0], rows=IN_WIDTH // N_SHARD), sem)
        xv = x_ref[...]
        h = ((xv * _rstd(xv)) * pre_ref[...]).astype(BF16)
        z = _dot_nt(h, w_v[...])
        tables = (rope_ref[0], rope_ref[1], rope_ref[2])
        u_ref[...] = z[:, :POOL_WIDTH]
        for t in range(ATTN_WIDTH // _LANES):
            lo = POOL_WIDTH + t * _LANES
            q_ref[:, t * _LANES:(t + 1) * _LANES] = (_rope(z[:, lo:lo + _LANES], tables) * ATTN_SCALE).astype(BF16)
        kv = POOL_WIDTH + ATTN_WIDTH
        _store_head_variants(k_ref, _rope(z[:, kv:kv + KV_WIDTH], tables))
        _store_head_variants(v_ref, z[:, kv + KV_WIDTH:])

    args = [x, pre, *w_in_t, mine, rope]
    variants = pl.BlockSpec((2 * N_KV_HEADS, tm, KV_WIDTH), lambda i: (0, i, 0))
    return pl.pallas_call(
        _ignoring(body, len(args), len(deps)), name="mixer_in_fwd", grid=(S // tm,),
        in_specs=[_rows(tm, D_MODEL), _full((1, D_MODEL)), _ANY, _ANY, _SMEM, _table_spec(tm)] + [_ANY] * len(deps),
        out_specs=[_rows(tm, POOL_WIDTH), _rows(tm, ATTN_WIDTH), variants, variants],
        out_shape=[jax.ShapeDtypeStruct((S, POOL_WIDTH), F32), jax.ShapeDtypeStruct((S, ATTN_WIDTH), BF16),
                   jax.ShapeDtypeStruct((2 * N_KV_HEADS, S, KV_WIDTH), BF16),
                   jax.ShapeDtypeStruct((2 * N_KV_HEADS, S, KV_WIDTH), BF16)],
        scratch_shapes=[pltpu.VMEM((IN_WIDTH, D_MODEL), BF16), pltpu.SemaphoreType.DMA((N_SHARD,))],
        compiler_params=_params(dimension_semantics=("arbitrary",)),
    )(*args, *deps)


def _mixer_in_bwd(dres, x, pre, w_in_t, mine, du, dq, dk, dv, dk_next, dv_next, rope, deps=()):
    S = x.shape[0]
    tm = MIXER_TILE

    def body(dres_ref, x_ref, pre_ref, w_land, w_own, mine_ref, du_ref, dq_ref, dk_ref, dv_ref, dkx_ref, dvx_ref,
             rope_ref,
             dx_ref, dz_ref, h_ref, dpre_ref, w_v, sem):
        _load_once(_gathered(w_land, w_own, w_v, mine_ref[0], rows=IN_WIDTH // N_SHARD), sem)

        @pl.when(pl.program_id(0) == 0)
        def _():
            dpre_ref[...] = jnp.zeros_like(dpre_ref)

        tables = (rope_ref[0], rope_ref[1], rope_ref[2])
        dz_ref[:, :POOL_WIDTH] = du_ref[...]
        for t in range(ATTN_WIDTH // _LANES):
            lo = POOL_WIDTH + t * _LANES
            dz_ref[:, lo:lo + _LANES] = _rope_transposed(dq_ref[:, t * _LANES:(t + 1) * _LANES], tables).astype(BF16)
        kv = POOL_WIDTH + ATTN_WIDTH
        has_next = pl.program_id(0) + 1 < steps
        pad = jnp.zeros((tm - BLOCK, KV_WIDTH), F32)
        dk_tile = dk_ref[...] + jnp.concatenate([pad, jnp.where(has_next, dkx_ref[...], 0.0)], axis=0)
        dv_tile = dv_ref[...] + jnp.concatenate([pad, jnp.where(has_next, dvx_ref[...], 0.0)], axis=0)
        dz_ref[:, kv:kv + KV_WIDTH] = _rope_transposed(dk_tile, tables).astype(BF16)
        dz_ref[:, kv + KV_WIDTH:] = dv_tile.astype(BF16)
        dh = _dot(dz_ref[...], w_v[...])
        xv = x_ref[...]
        rx = _rstd(xv)
        xn = xv * rx
        h_ref[...] = (xn * pre_ref[...]).astype(BF16)
        dpre_ref[...] += jnp.sum(dh * xn, axis=0, keepdims=True)
        dx_ref[...] = dres_ref[...] + _norm_bwd(dh, xn, rx, pre_ref[...])

    assert tm == ATTN_BLOCKS * BLOCK
    steps = S // tm
    nxt = pl.BlockSpec((None, BLOCK, KV_WIDTH), lambda i: (jnp.minimum(i + 1, steps - 1), 0, 0))
    args = [dres, x, pre, *w_in_t, mine, du, dq, dk, dv, dk_next, dv_next, rope]
    return pl.pallas_call(
        _ignoring(body, len(args), len(deps)), name="mixer_in_bwd", grid=(S // tm,),
        in_specs=[_rows(tm, D_MODEL), _rows(tm, D_MODEL), _full((1, D_MODEL)), _ANY, _ANY, _SMEM,
                  _rows(tm, POOL_WIDTH), _rows(tm, ATTN_WIDTH), _rows(tm, KV_WIDTH), _rows(tm, KV_WIDTH), nxt, nxt,
                  _table_spec(tm)] + [_ANY] * len(deps),
        out_specs=[_rows(tm, D_MODEL), _rows(tm, IN_WIDTH), _rows(tm, D_MODEL), _full((1, D_MODEL))],
        out_shape=[jax.ShapeDtypeStruct((S, D_MODEL), F32), jax.ShapeDtypeStruct((S, IN_WIDTH), BF16),
                   jax.ShapeDtypeStruct((S, D_MODEL), BF16), jax.ShapeDtypeStruct((1, D_MODEL), F32)],
        scratch_shapes=[pltpu.VMEM((IN_WIDTH, D_MODEL), BF16), pltpu.SemaphoreType.DMA((N_SHARD,))],
        compiler_params=_params(dimension_semantics=("arbitrary",)),
    )(*args, *deps)


def _pool_counts(tile_index, tm, width):
    t = tile_index * tm + lax.broadcasted_iota(jnp.int32, (tm, 1), 0)
    return jnp.minimum(t + 1, width).astype(F32)


def _pool_features(ext, u_tile, tile_index, tm):
    ds = []
    for gi, width in enumerate(POOL_WINDOWS):
        lanes = slice(gi * POOL_GROUP, (gi + 1) * POOL_GROUP)
        s = ext[:, lanes]
        shift = 1
        while shift < width:
            s = s + pltpu.roll(s, shift, axis=0)
            shift *= 2
        ds.append(s[HALO:, :] / _pool_counts(tile_index, tm, width) - u_tile[:, lanes])
    return ds


def _pool_fwd(u, w_pool, pool_scale, g_pool):
    S = u.shape[0]
    tm = MIXER_TILE

    def body(u_ref, w_ref, scale_ref, gain_ref, y_ref, ext_ref):
        i = pl.program_id(0)

        @pl.when(i == 0)
        def _():
            ext_ref[:HALO, :] = jnp.zeros((HALO, POOL_WIDTH), F32)

        u_tile = u_ref[...]
        ext_ref[HALO:, :] = u_tile
        ds = _pool_features(ext_ref[...], u_tile, i, tm)
        ext_ref[:HALO, :] = u_tile[tm - HALO:, :]
        ys = [_dot(ds[gi].astype(BF16), w_ref[gi].astype(BF16)) for gi in range(len(POOL_WINDOWS))]
        po = jnp.concatenate(ys, axis=1) * scale_ref[...]
        y_ref[...] = ((po * _rstd(po)) * gain_ref[...]).astype(BF16)

    return pl.pallas_call(
        body, name="pool_fwd", grid=(S // tm,),
        in_specs=[_rows(tm, POOL_WIDTH), _full((len(POOL_WINDOWS), POOL_GROUP, POOL_GROUP)),
                  _full((1, POOL_WIDTH)), _full((1, POOL_WIDTH))],
        out_specs=_rows(tm, POOL_WIDTH),
        out_shape=jax.ShapeDtypeStruct((S, POOL_WIDTH), BF16),
        scratch_shapes=[pltpu.VMEM((HALO + tm, POOL_WIDTH), F32)],
        compiler_params=_params(dimension_semantics=("arbitrary",)),
    )(u, w_pool, pool_scale, g_pool)


def _pool_bwd(dy, u, w_pool, pool_scale, g_pool):
    S = u.shape[0]
    tm = MIXER_TILE
    n_tiles = S // tm
    halo_blocks = tm // HALO

    def body(dy_ref, u_ref, uprev_ref, w_ref, scale_ref, gain_ref,
             du_ref, dw_ref, dscale_ref, dgain_ref, ext_ref, nxt_ref):
        i = pl.program_id(0)
        tile = n_tiles - 1 - i

        @pl.when(i == 0)
        def _():
            dw_ref[...] = jnp.zeros_like(dw_ref)
            dscale_ref[...] = jnp.zeros_like(dscale_ref)
            dgain_ref[...] = jnp.zeros_like(dgain_ref)
            nxt_ref[...] = jnp.zeros_like(nxt_ref)

        u_tile = u_ref[...]
        ext_ref[:HALO, :] = jnp.where(tile > 0, uprev_ref[...], 0.0)
        ext_ref[HALO:, :] = u_tile
        ds = _pool_features(ext_ref[...], u_tile, tile, tm)
        dsb = [d.astype(BF16) for d in ds]
        wb = [w_ref[gi].astype(BF16) for gi in range(len(POOL_WINDOWS))]
        yraw = jnp.concatenate([_dot(dsb[gi], wb[gi]) for gi in range(len(POOL_WINDOWS))], axis=1)
        po = yraw * scale_ref[...]
        r = _rstd(po)
        pn = po * r
        dyv = dy_ref[...]
        dgain_ref[...] += jnp.sum(dyv * pn, axis=0, keepdims=True)
        dpo = _norm_bwd(dyv, pn, r, gain_ref[...])
        dscale_ref[...] += jnp.sum(dpo * yraw, axis=0, keepdims=True)
        dyraw = (dpo * scale_ref[...]).astype(BF16)
        for gi, width in enumerate(POOL_WINDOWS):
            lanes = slice(gi * POOL_GROUP, (gi + 1) * POOL_GROUP)
            dw_ref[gi] += _dot_tn(dsb[gi], dyraw[:, lanes])
            dd = _dot_nt(dyraw[:, lanes], wb[gi])
            ddc = dd / _pool_counts(tile, tm, width)
            ext_ref[:tm, lanes] = ddc
            ext_ref[tm:, lanes] = nxt_ref[:, lanes]
            s = ext_ref[:, lanes]
            shift = 1
            while shift < width:
                s = s + pltpu.roll(s, HALO + tm - shift, axis=0)
                shift *= 2
            du_ref[:, lanes] = (s[:tm, :] - dd).astype(BF16)
            nxt_ref[:, lanes] = ddc[:HALO, :]

    return pl.pallas_call(
        body, name="pool_bwd", grid=(n_tiles,),
        in_specs=[pl.BlockSpec((tm, POOL_WIDTH), lambda i: (n_tiles - 1 - i, 0)),
                  pl.BlockSpec((tm, POOL_WIDTH), lambda i: (n_tiles - 1 - i, 0)),
                  pl.BlockSpec((HALO, POOL_WIDTH), lambda i: (jnp.maximum((n_tiles - 1 - i) * halo_blocks - 1, 0), 0)),
                  _full((len(POOL_WINDOWS), POOL_GROUP, POOL_GROUP)), _full((1, POOL_WIDTH)), _full((1, POOL_WIDTH))],
        out_specs=[pl.BlockSpec((tm, POOL_WIDTH), lambda i: (n_tiles - 1 - i, 0)),
                   _full((len(POOL_WINDOWS), POOL_GROUP, POOL_GROUP)), _full((1, POOL_WIDTH)), _full((1, POOL_WIDTH))],
        out_shape=[jax.ShapeDtypeStruct((S, POOL_WIDTH), BF16),
                   jax.ShapeDtypeStruct((len(POOL_WINDOWS), POOL_GROUP, POOL_GROUP), F32),
                   jax.ShapeDtypeStruct((1, POOL_WIDTH), F32), jax.ShapeDtypeStruct((1, POOL_WIDTH), F32)],
        scratch_shapes=[pltpu.VMEM((HALO + tm, POOL_WIDTH), F32), pltpu.VMEM((HALO, POOL_WIDTH), F32)],
        compiler_params=_params(dimension_semantics=("arbitrary",)),
    )(dy, u, u, w_pool, pool_scale, g_pool)


def _variant(head):
    return 2 * (head // (N_HEADS // N_KV_HEADS)) + head % 2


def _own_block(shape=(BLOCK, BLOCK)):
    r = lax.broadcasted_iota(jnp.int32, shape, 0)
    i = lax.broadcasted_iota(jnp.int32, shape, 1)
    return r <= i


def _fold_band(own, from_own, from_prev):
    return jnp.where(own, from_own, from_prev)


def _scores_by_head(own_tiles, prev_tiles, q_tiles):
    stacks = [jnp.concatenate(q_tiles[:2], axis=0), jnp.concatenate(q_tiles[2:], axis=0)]
    by_var = [_dot_nt(jnp.concatenate([own_tiles[v], prev_tiles[v]], axis=0), stacks[v // 2])
              for v in range(2 * N_KV_HEADS)]
    quadrant = lambda h, rows: by_var[_variant(h)][rows * BLOCK:(rows + 1) * BLOCK,
                                                   ((h // 2) % 2) * BLOCK:((h // 2) % 2 + 1) * BLOCK]
    return [quadrant(h, 0) for h in range(N_HEADS)], [quadrant(h, 1) for h in range(N_HEADS)]


def _softmax_t(s, sink):
    m = jnp.maximum(jnp.max(s, axis=0, keepdims=True), sink)
    p = jnp.exp(s - m)
    p_sink = jnp.exp(sink - m)
    inv = 1.0 / (jnp.sum(p, axis=0, keepdims=True) + p_sink)
    return p * inv, p_sink * inv


def _attn_fwd(q, kz, vz, sinks, g_attn, y_pool, x, w_out, mine, post):
    S = q.shape[0]
    tq = ATTN_BLOCKS * BLOCK
    n_var = 2 * N_KV_HEADS

    def body(q_ref, kp_ref, kc_ref, vp_ref, vc_ref, sinks_ref, gain_ref, yp_ref, x_ref, w_land, w_own, mine_ref,
             post_ref, o_ref, out_ref, m_ref, y_ref, w_v, sem):
        _load_once(_gathered(w_land, w_own, w_v, mine_ref[0], rows=D_MODEL // N_SHARD), sem)
        step = pl.program_id(0)
        own = _own_block()
        zero = jnp.zeros((BLOCK, BLOCK), F32)

        def tiles(cur_ref, prev_ref, j):
            rows = lambda jj: slice(jj * BLOCK, (jj + 1) * BLOCK)
            return ([cur_ref[v, rows(j), :] for v in range(n_var)],
                    [prev_ref[v] if j == 0 else cur_ref[v, rows(j - 1), :] for v in range(n_var)])

        scores = []
        for j in range(ATTN_BLOCKS):
            q_pairs = [q_ref[j * BLOCK:(j + 1) * BLOCK, i * _LANES:(i + 1) * _LANES] for i in range(N_HEADS // 2)]
            scores.append(_scores_by_head(*tiles(kc_ref, kp_ref, j), q_pairs))
        probs = []
        for j in range(ATTN_BLOCKS):
            s_own, s_prev = scores[j]
            no_prev = jnp.where(step > 0, 0.0, NEG_INF) if j == 0 else 0.0
            p_own, p_prev = [], []
            for h in range(N_HEADS):
                p, _ = _softmax_t(_fold_band(own, s_own[h], s_prev[h] + no_prev), sinks_ref[0, h])
                p_own.append(jnp.where(own, p, zero).astype(BF16))
                p_prev.append(jnp.where(own, zero, p).astype(BF16))
            probs.append((p_own, p_prev))
        blocks = []
        for j in range(ATTN_BLOCKS):
            p_own, p_prev = probs[j]
            v_own, v_prev = tiles(vc_ref, vp_ref, j)
            pairs = []
            for i in range(N_HEADS // 2):
                acc = None
                for h in (2 * i, 2 * i + 1):
                    part = _dot_tn(p_own[h], v_own[_variant(h)]) + _dot_tn(p_prev[h], v_prev[_variant(h)])
                    acc = part if acc is None else acc + part
                pairs.append(acc)
            blocks.append(jnp.concatenate(pairs, axis=1))
        o = jnp.concatenate(blocks, axis=0)
        o_ref[...] = o
        y_ref[:, :POOL_WIDTH] = yp_ref[...]
        y_ref[:, POOL_WIDTH:] = ((o * _rstd(o)) * gain_ref[...]).astype(BF16)
        m = _dot(y_ref[...], w_v[...])
        m_ref[...] = m
        out_ref[...] = x_ref[...] + (m * _rstd(m)) * post_ref[...]

    prev = pl.BlockSpec((n_var, BLOCK, KV_WIDTH), lambda g: (0, jnp.maximum(g * ATTN_BLOCKS - 1, 0), 0))
    cur = pl.BlockSpec((n_var, tq, KV_WIDTH), lambda g: (0, g, 0))
    return pl.pallas_call(
        body, name="attn_fwd", grid=(S // tq,),
        in_specs=[_rows(tq, ATTN_WIDTH), prev, cur, prev, cur,
                  pl.BlockSpec(memory_space=pltpu.SMEM), _full((1, ATTN_WIDTH)),
                  _rows(tq, POOL_WIDTH), _rows(tq, D_MODEL), _ANY, _ANY, _SMEM, _full((1, D_MODEL))],
        out_specs=[_rows(tq, ATTN_WIDTH), _rows(tq, D_MODEL), _rows(tq, D_MODEL), _rows(tq, D_MODEL)],
        out_shape=[jax.ShapeDtypeStruct((S, ATTN_WIDTH), F32), jax.ShapeDtypeStruct((S, D_MODEL), F32),
                   jax.ShapeDtypeStruct((S, D_MODEL), F32), jax.ShapeDtypeStruct((S, D_MODEL), BF16)],
        scratch_shapes=[pltpu.VMEM((D_MODEL, D_MODEL), BF16), pltpu.SemaphoreType.DMA((N_SHARD,))],
        compiler_params=_params(dimension_semantics=("arbitrary",)),
    )(q, kz, kz, vz, vz, sinks, g_attn, y_pool, x, *w_out, mine, post)


def _attn_bwd(dout, m, w_out, mine, post, o, q, kz, vz, sinks, g_attn, deps=()):
    S = q.shape[0]
    tq = ATTN_BLOCKS * BLOCK
    n_var = 2 * N_KV_HEADS

    def body(dout_ref, m_ref, w_land, w_own, mine_ref, post_ref, o_ref, q_ref, kp_ref, kc_ref, vp_ref, vc_ref,
             sinks_ref, gain_ref, dyp_ref, dm_ref, dpost_ref, dq_ref, dk_ref, dv_ref, dkx_ref, dvx_ref, dsink_ref,
             dgain_ref, w_v, sem):
        _load_once(_gathered(w_land, w_own, w_v, mine_ref[0], rows=D_MODEL // N_SHARD), sem)
        step = pl.program_id(0)

        @pl.when(step == 0)
        def _():
            dpost_ref[...] = jnp.zeros_like(dpost_ref)
            dsink_ref[...] = jnp.zeros_like(dsink_ref)
            dgain_ref[...] = jnp.zeros_like(dgain_ref)

        mv = m_ref[...]
        rm = _rstd(mv)
        mn = mv * rm
        dres = dout_ref[...]
        dpost_ref[...] += jnp.sum(dres * mn, axis=0, keepdims=True)
        dm = _norm_bwd(dres, mn, rm, post_ref[...]).astype(BF16)
        dm_ref[...] = dm
        dy = _dot_nt(dm, w_v[...])
        dyp_ref[...] = dy[:, :POOL_WIDTH]
        ov = o_ref[...]
        r = _rstd(ov)
        on = ov * r
        dyv = dy[:, POOL_WIDTH:]
        dgain_ref[...] += jnp.sum(dyv * on, axis=0, keepdims=True)
        do = _norm_bwd(dyv, on, r, gain_ref[...]).astype(BF16)
        own = _own_block()
        zero = jnp.zeros((BLOCK, BLOCK), F32)
        split = lambda t: (jnp.where(own, t, zero).astype(BF16), jnp.where(own, zero, t).astype(BF16))
        rows = lambda j: slice(j * BLOCK, (j + 1) * BLOCK)
        heads = range(N_HEADS)

        def tiles(cur_ref, prev_ref, j):
            return ([cur_ref[v, rows(j), :] for v in range(n_var)],
                    [prev_ref[v] if j == 0 else cur_ref[v, rows(j - 1), :] for v in range(n_var)])

        q_pairs = [[q_ref[rows(j), i * _LANES:(i + 1) * _LANES] for i in range(N_HEADS // 2)] for j in range(ATTN_BLOCKS)]
        do_pairs = [[do[rows(j), i * _LANES:(i + 1) * _LANES] for i in range(N_HEADS // 2)] for j in range(ATTN_BLOCKS)]
        scores = [(_scores_by_head(*tiles(kc_ref, kp_ref, j), q_pairs[j]),
                   _scores_by_head(*tiles(vc_ref, vp_ref, j), do_pairs[j])) for j in range(ATTN_BLOCKS)]
        parts, sink_sum = [], None
        for j in range(ATTN_BLOCKS):
            (s_own, s_prev), (dp_own, dp_prev) = scores[j]
            no_prev = jnp.where(step > 0, 0.0, NEG_INF) if j == 0 else 0.0
            ds_parts, p_parts, sink_rows = [], [], []
            for h in heads:
                p, p_sink = _softmax_t(_fold_band(own, s_own[h], s_prev[h] + no_prev), sinks_ref[0, h])
                dp = _fold_band(own, dp_own[h], dp_prev[h])
                delta = jnp.sum(p * dp, axis=0, keepdims=True)
                ds_parts.append(split(p * (dp - delta)))
                p_parts.append(split(p))
                sink_rows.append(jnp.zeros((1, _LANES), F32) - jnp.sum(p_sink * delta))
            block_sinks = jnp.concatenate(sink_rows, axis=0)
            sink_sum = block_sinks if sink_sum is None else sink_sum + block_sinks
            parts.append((ds_parts, p_parts))
        dsink_ref[...] += sink_sum
        low = lax.broadcasted_iota(jnp.int32, (BLOCK, _LANES), 1) < HEAD_DIM

        def merge(acc):
            return jnp.where(low, acc[0] + pltpu.roll(acc[1], HEAD_DIM, axis=1),
                             acc[3] + pltpu.roll(acc[2], HEAD_DIM, axis=1))
        add = lambda acc, var, t: acc.__setitem__(var, t if acc[var] is None else acc[var] + t)
        k_own, k_prev, v_own, v_prev = [], [], [], []
        for j in range(ATTN_BLOCKS):
            ds_parts, p_parts = parts[j]
            kt_own, kt_prev = tiles(kc_ref, kp_ref, j)
            dk_own, dk_prev, dv_own, dv_prev = ([None] * n_var for _ in range(4))
            for i in range(N_HEADS // 2):
                dq_pair = None
                for h in (2 * i, 2 * i + 1):
                    var = _variant(h)
                    (ds_o, ds_p), (p_o, p_p) = ds_parts[h], p_parts[h]
                    part = _dot_tn(ds_o, kt_own[var]) + _dot_tn(ds_p, kt_prev[var])
                    dq_pair = part if dq_pair is None else dq_pair + part
                    add(dk_own, var, _dot(ds_o, q_pairs[j][i]))
                    add(dk_prev, var, _dot(ds_p, q_pairs[j][i]))
                    add(dv_own, var, _dot(p_o, do_pairs[j][i]))
                    add(dv_prev, var, _dot(p_p, do_pairs[j][i]))
                dq_ref[rows(j), i * _LANES:(i + 1) * _LANES] = dq_pair * ATTN_SCALE
            k_own.append(merge(dk_own))
            k_prev.append(merge(dk_prev))
            v_own.append(merge(dv_own))
            v_prev.append(merge(dv_prev))
        for j in range(ATTN_BLOCKS):
            last = j == ATTN_BLOCKS - 1
            dk_ref[rows(j), :] = k_own[j] if last else k_own[j] + k_prev[j + 1]
            dv_ref[rows(j), :] = v_own[j] if last else v_own[j] + v_prev[j + 1]
        dkx_ref[...] = k_prev[0]
        dvx_ref[...] = v_prev[0]

    steps = S // tq
    prev = pl.BlockSpec((n_var, BLOCK, KV_WIDTH), lambda g: (0, jnp.maximum(g * ATTN_BLOCKS - 1, 0), 0))
    cur = pl.BlockSpec((n_var, tq, KV_WIDTH), lambda g: (0, g, 0))
    nxt = pl.BlockSpec((None, BLOCK, KV_WIDTH), lambda g: (g, 0, 0))
    args = [dout, m, *w_out, mine, post, o, q, kz, kz, vz, vz, sinks, g_attn]
    return pl.pallas_call(
        _ignoring(body, len(args), len(deps)), name="attn_bwd", grid=(steps,),
        in_specs=[_rows(tq, D_MODEL), _rows(tq, D_MODEL), _ANY, _ANY, _SMEM, _full((1, D_MODEL)),
                  _rows(tq, ATTN_WIDTH), _rows(tq, ATTN_WIDTH), prev, cur, prev, cur,
                  pl.BlockSpec(memory_space=pltpu.SMEM), _full((1, ATTN_WIDTH))] + [_ANY] * len(deps),
        out_specs=[_rows(tq, POOL_WIDTH), _rows(tq, D_MODEL), _full((1, D_MODEL)),
                   _rows(tq, ATTN_WIDTH), _rows(tq, KV_WIDTH), _rows(tq, KV_WIDTH), nxt, nxt,
                   _full((N_HEADS, _LANES)), _full((1, ATTN_WIDTH))],
        out_shape=[jax.ShapeDtypeStruct((S, POOL_WIDTH), F32), jax.ShapeDtypeStruct((S, D_MODEL), BF16),
                   jax.ShapeDtypeStruct((1, D_MODEL), F32),
                   jax.ShapeDtypeStruct((S, ATTN_WIDTH), F32), jax.ShapeDtypeStruct((S, KV_WIDTH), F32),
                   jax.ShapeDtypeStruct((S, KV_WIDTH), F32),
                   jax.ShapeDtypeStruct((steps, BLOCK, KV_WIDTH), F32), jax.ShapeDtypeStruct((steps, BLOCK, KV_WIDTH), F32),
                   jax.ShapeDtypeStruct((N_HEADS, _LANES), F32), jax.ShapeDtypeStruct((1, ATTN_WIDTH), F32)],
        scratch_shapes=[pltpu.VMEM((D_MODEL, D_MODEL), BF16), pltpu.SemaphoreType.DMA((N_SHARD,))],
        compiler_params=_params(dimension_semantics=("arbitrary",)),
    )(*args, *deps)


def _inv_freq_row():
    inv_freq = ROPE_THETA ** (-jnp.arange(0, ROT_DIM, 2, dtype=F32) / ROT_DIM)
    per_head = jnp.concatenate([inv_freq, inv_freq, jnp.zeros((HEAD_DIM - ROT_DIM,), F32)])
    return jnp.tile(per_head, _LANES // HEAD_DIM).reshape(1, _LANES)


def _local_step(x, pos, target, small, mine, weights_of, grads_ready):
    rope = _rope_tables(pos, _inv_freq_row())
    (wgu1,) = weights_of("ffn1_up", (rope,))
    g1, u1, a1 = _ffn_up(x, small["ffn1_pre"], wgu1, mine)
    (wd1,) = weights_of("ffn1_down", (a1,))
    x1, f1 = _ffn_down(x, a1, small["ffn1_post"], wd1, mine)
    w_in_t, w_out = weights_of("mixer", (x1,))
    u, q, kz, vz = _mixer_in_fwd(x1, small["mix_pre"], w_in_t, mine, rope)
    y_pool = _pool_fwd(u, small["w_pool"], small["pool_scale"], small["g_pool"])
    o, x2, m, y = _attn_fwd(q, kz, vz, small["sinks"], small["g_attn"], y_pool, x1, w_out, mine, small["mix_post"])
    wgu2, wd2 = weights_of("ffn2", (x2,))
    g2, u2, a2 = _ffn_up(x2, small["ffn2_pre"], wgu2, mine)
    dx3, df2, loss_acc, dpost2 = _ffn_down(x2, a2, small["ffn2_post"], wd2, mine, target=target)
    grads = {"loss": loss_acc * (0.5 / D_MODEL), "ffn2_post": dpost2}
    dx2, h3, dgu2, grads["ffn2_pre"] = _ffn_bwd(
        dx3, x2, df2, g2, u2, small["ffn2_pre"], small["ffn2_post"], wgu2, wd2, mine, df_known=True)
    dwgu2 = _wgrad(h3, dgu2, D_MODEL, FF_CHUNK, "wgrad_gu2", column_shards=True)
    dwd2 = _wgrad(a2, df2, FF_CHUNK, D_MODEL, "wgrad_down2")
    deps = grads_ready("ffn2", {"ffn2_w_gu": dwgu2, "ffn2_w_down": dwd2})
    dy_pool, dm, grads["mix_post"], dq, dk, dv, dk_next, dv_next, dsinks, grads["g_attn"] = _attn_bwd(
        dx2, m, w_out, mine, small["mix_post"], o, q, kz, vz, small["sinks"], small["g_attn"], deps=deps)
    dw_out = _wgrad(y, dm, D_MODEL, D_MODEL, "wgrad_out")
    grads["sinks"] = dsinks[:, 0].reshape(1, N_HEADS)
    du, grads["w_pool"], grads["pool_scale"], grads["g_pool"] = _pool_bwd(
        dy_pool, u, small["w_pool"], small["pool_scale"], small["g_pool"])
    dx1, dz, h2, grads["mix_pre"] = _mixer_in_bwd(dx2, x1, small["mix_pre"], w_in_t, mine, du, dq, dk, dv, dk_next, dv_next, rope)
    dw_in_t = _wgrad(dz, h2, IN_WIDTH, D_MODEL, "wgrad_in")
    deps = grads_ready("mixer", {"w_in": dw_in_t, "w_out": dw_out})
    dx, h1, dgu1, df1, grads["ffn1_pre"], grads["ffn1_post"] = _ffn_bwd(
        dx1, x, f1, g1, u1, small["ffn1_pre"], small["ffn1_post"], wgu1, wd1, mine, deps=deps)
    dwgu1 = _wgrad(h1, dgu1, D_MODEL, FF_CHUNK, "wgrad_gu1", column_shards=True)
    deps = grads_ready("ffn1_gu", {"ffn1_w_gu": dwgu1}, small=grads)
    dwd1 = _wgrad(a1, df1, FF_CHUNK, D_MODEL, "wgrad_down1", deps=deps)
    grads_ready("ffn1_down", {"ffn1_w_down": dwd1})
    return dx


def _place():
    return lax.axis_index("x"), lax.axis_index("y"), lax.axis_index("c")


def _other_chips(x, y):
    return [(1 - x, y), (x, 1 - y), (1 - x, 1 - y)]


_HBM = pl.BlockSpec(memory_space=pltpu.HBM)
_SEM = pl.BlockSpec(memory_space=pltpu.SEMAPHORE)
_EFFECT = pltpu.SideEffectType.DATAFLOW_SIDE_EFFECTING
GATHER, GATHER_HALF, REDUCE, BROADCAST = "gather", "gather_half", "reduce", "broadcast"
N_DEVICES = 8


def _in_hbm(a):
    return pltpu.with_memory_space_constraint(a, pltpu.HBM)


def _core_half(rows, c):
    return pl.ds(pl.multiple_of(c * (rows // 2), 16), rows // 2)


def _chip_copies(kind, srcs, lands, send_sems, recv_sems):
    x, y, c = _place()
    mine = 2 * x + y
    copies = []
    for w in range(len(srcs)):
        if kind == BROADCAST:
            peers = [(x ^ (k >> 2), y ^ ((k >> 1) & 1), c ^ (k & 1)) for k in range(1, N_DEVICES)]
        else:
            peers = [(px, py, c) for px, py in _other_chips(x, y)]
        for k, (px, py, pc) in enumerate(peers):
            if kind == GATHER:
                src, dst = srcs[w], lands[w].at[mine]
            elif kind == GATHER_HALF:
                half = _core_half(srcs[w].shape[0], c)
                src, dst = srcs[w].at[half, :], lands[w].at[mine, half, :]
            elif kind == BROADCAST:
                src, dst = srcs[w], lands[w].at[2 * mine + c]
            else:
                src, dst = srcs[w].at[2 * px + py], lands[w].at[k]
            pair = len(peers) * w + k
            copies.append(pltpu.make_async_remote_copy(
                src_ref=src, dst_ref=dst, send_sem=send_sems.at[pair], recv_sem=recv_sems.at[pair],
                device_id=(px, py, pc), device_id_type=MESH))
    return copies


def _landing_shape(kind, src):
    if kind == REDUCE:
        return (N_SHARD - 1,) + src.shape[1:]
    return ((N_DEVICES if kind == BROADCAST else N_SHARD),) + src.shape


def _peer_count(kind):
    return N_DEVICES - 1 if kind == BROADCAST else N_SHARD - 1


def _exchange_start(kinds, groups, name):
    sizes = [len(g) for g in groups]
    flat = [s for g in groups for s in g]
    n, ng = len(flat), len(groups)

    def body(*refs):
        srcs, lands = refs[:n], refs[n:2 * n]
        sems = refs[2 * n:2 * n + 2 * ng]
        token = refs[-1]
        start = 0
        for gi, size in enumerate(sizes):
            for cp in _chip_copies(kinds[gi], srcs[start:start + size], lands[start:start + size],
                                   sems[2 * gi], sems[2 * gi + 1]):
                cp.start()
            start += size
        token[...] = jnp.zeros_like(token)

    landings = [lax.empty(_landing_shape(kind, s), s.dtype) for kind, g in zip(kinds, groups) for s in g]
    sem_shapes = [pltpu.SemaphoreType.DMA((size * _peer_count(kind),)) for kind, size in zip(kinds, sizes)
                  for _ in range(2)]
    outs = pl.pallas_call(
        body, name=name,
        in_specs=[_HBM] * (2 * n),
        out_specs=[_SEM] * (2 * ng) + [_HBM] * (2 * n) + [pl.BlockSpec(memory_space=pltpu.VMEM)],
        out_shape=sem_shapes + [pltpu.HBM(a.shape, a.dtype) for a in flat + landings]
        + [jax.ShapeDtypeStruct((8, _LANES), F32)],
        input_output_aliases={i: 2 * ng + i for i in range(2 * n)},
        compiler_params=pltpu.CompilerParams(has_side_effects=_EFFECT),
    )(*[_in_hbm(a) for a in flat + landings])
    sems, srcs, lands, token = outs[:2 * ng], outs[2 * ng:2 * ng + n], outs[2 * ng + n:2 * ng + 2 * n], outs[-1]
    handles, start = [], 0
    for gi, size in enumerate(sizes):
        handles.append((sems[2 * gi], sems[2 * gi + 1], srcs[start:start + size], lands[start:start + size]))
        start += size
    return handles, token


def _exchange_wait(kind, handle, after, name):
    send_sems, recv_sems, srcs, lands = handle
    n = len(srcs)

    def body(*refs):
        copies = _chip_copies(kind, refs[:n], refs[n:2 * n], refs[2 * n], refs[2 * n + 1])
        for cp in copies:
            cp.wait_send()
        for cp in copies:
            cp.wait_recv()

    outs = pl.pallas_call(
        body, name=name,
        in_specs=[_HBM] * (2 * n) + [_SEM, _SEM] + [_ANY] * len(after),
        out_specs=[_HBM] * (2 * n),
        out_shape=[pltpu.HBM(a.shape, a.dtype) for a in list(srcs) + list(lands)],
        input_output_aliases={i: i for i in range(2 * n)},
        compiler_params=pltpu.CompilerParams(has_side_effects=_EFFECT),
    )(*srcs, *lands, send_sems, recv_sems, *after)
    return outs[:n], outs[n:]


def _swap_gathered_halves(lands, name):
    n = len(lands)

    def body(*refs):
        bufs = refs[n:2 * n]
        send_sems, recv_sems = refs[2 * n:]
        x, y, c = _place()
        mine = 2 * x + y
        sends, arrivals = [], []
        for w in range(n):
            rows = bufs[w].shape[1]
            for d in range(1, N_SHARD):
                slot = (mine + d) % N_SHARD
                sems = dict(send_sem=send_sems.at[(N_SHARD - 1) * w + d - 1],
                            recv_sem=recv_sems.at[(N_SHARD - 1) * w + d - 1],
                            device_id=(x, y, 1 - c), device_id_type=MESH)
                fetched = bufs[w].at[slot, _core_half(rows, c), :]
                missing = bufs[w].at[slot, _core_half(rows, 1 - c), :]
                sends.append(pltpu.make_async_remote_copy(src_ref=fetched, dst_ref=fetched, **sems))
                arrivals.append(pltpu.make_async_remote_copy(src_ref=missing, dst_ref=missing, **sems))
        for cp in sends:
            cp.start()
        for cp in arrivals:
            cp.wait_recv()
        for cp in sends:
            cp.wait_send()

    return pl.pallas_call(
        body, name=name, in_specs=[_ANY] * n, out_specs=[_ANY] * n,
        out_shape=[jax.ShapeDtypeStruct(a.shape, a.dtype) for a in lands],
        input_output_aliases={i: i for i in range(n)},
        scratch_shapes=[pltpu.SemaphoreType.DMA((n * (N_SHARD - 1),)), pltpu.SemaphoreType.DMA((n * (N_SHARD - 1),))],
        compiler_params=pltpu.CompilerParams(has_side_effects=True),
    )(*lands)


def _swap_with_sibling(partials, name):
    n = len(partials)

    def body(*refs):
        ins, outs = refs[:n], refs[n:2 * n]
        send_sems, recv_sems = refs[2 * n:]
        x, y, c = _place()
        sends = [pltpu.make_async_remote_copy(
            src_ref=ins[w], dst_ref=outs[w], send_sem=send_sems.at[w], recv_sem=recv_sems.at[w],
            device_id=(x, y, 1 - c), device_id_type=MESH) for w in range(n)]
        for cp in sends:
            cp.start()
        for cp in sends:
            cp.wait_recv()
        for cp in sends:
            cp.wait_send()

    return pl.pallas_call(
        body, name=name,
        in_specs=[_ANY] * n, out_specs=[_ANY] * n,
        out_shape=[jax.ShapeDtypeStruct(p.shape, p.dtype) for p in partials],
        scratch_shapes=[pltpu.SemaphoreType.DMA((n,)), pltpu.SemaphoreType.DMA((n,))],
        compiler_params=pltpu.CompilerParams(has_side_effects=True),
    )(*partials)


def _row_block(rows, cap):
    best = None
    for cand in range(16, min(rows, cap) + 1, 16):
        if rows % cand == 0:
            best = cand
    assert best is not None, rows
    return best


def _chip_partial(own, received, shard, name):
    _, R, C = own.shape
    rb = _row_block(R, 512)

    def body(shard_ref, own_ref, rec_ref, out_ref):
        acc = own_ref[...]
        for k in range(3):
            acc = acc + rec_ref[k].astype(F32)
        out_ref[...] = acc.astype(BF16)

    return pl.pallas_call(
        body, name=name,
        grid_spec=pltpu.PrefetchScalarGridSpec(
            num_scalar_prefetch=1, grid=(R // rb,),
            in_specs=[pl.BlockSpec((None, rb, C), lambda i, s: (s[0], i, 0)),
                      pl.BlockSpec((3, rb, C), lambda i, s: (0, i, 0))],
            out_specs=pl.BlockSpec((rb, C), lambda i, s: (i, 0))),
        out_shape=jax.ShapeDtypeStruct((R, C), BF16),
        compiler_params=_params(dimension_semantics=("arbitrary",)),
    )(shard, own, received)


def _adamw(w, m, v, g_parts, name, slot=None):
    R, C = w.shape
    by_device = slot is not None
    rb = _row_block(R, 512) if R % 16 == 0 else R

    def body(w_ref, m_ref, v_ref, *refs):
        g_refs, (grad_ref, delta_ref, m_out, v_out) = refs[:-4], refs[-4:]
        if by_device:
            own_ref, land_ref, slot_ref = g_refs
            part = lambda d: jnp.where(slot_ref[0] == d, own_ref[...], land_ref[d])
            g = part(0)
            for d in range(1, N_DEVICES):
                g = g + part(d)
        else:
            g = g_refs[0][...].astype(F32)
            for g_ref in g_refs[1:]:
                g = g + g_ref[...].astype(F32)
        grad_ref[...] = g
        new_m = ADAM_B1 * m_ref[...] + (1.0 - ADAM_B1) * g
        new_v = ADAM_B2 * v_ref[...] + (1.0 - ADAM_B2) * (g * g)
        m_hat = new_m / (1.0 - ADAM_B1 ** ADAM_STEP)
        v_hat = new_v / (1.0 - ADAM_B2 ** ADAM_STEP)
        delta_ref[...] = -ADAM_LR * (m_hat / (jnp.sqrt(v_hat) + ADAM_EPS) + ADAM_WD * w_ref[...])
        m_out[...] = new_m
        v_out[...] = new_v

    spec = pl.BlockSpec((rb, C), lambda i: (i, 0))
    if by_device:
        g_specs = [spec, pl.BlockSpec((N_DEVICES, rb, C), lambda i: (0, i, 0)), _SMEM]
        g_parts = list(g_parts) + [slot]
    else:
        g_specs = [spec] * len(g_parts)
    return pl.pallas_call(
        body, name=name, grid=(R // rb,),
        in_specs=[spec, spec, spec] + g_specs,
        out_specs=[spec] * 4,
        out_shape=[jax.ShapeDtypeStruct((R, C), F32)] * 4,
        compiler_params=_params(dimension_semantics=("arbitrary",)),
    )(w, m, v, *g_parts)


SMALL_NAMES = ("ffn1_pre", "ffn1_post", "mix_pre", "pool_scale", "sinks", "g_pool", "g_attn", "mix_post",
               "ffn2_pre", "ffn2_post", "w_pool")
_SLAB_PART = 8 * _LANES


SLAB_NAMES = SMALL_NAMES + ("loss",)


def _to_slab(parts):
    rows = []
    for name in SLAB_NAMES:
        flat = parts[name].reshape(-1) if name in parts else jnp.zeros((_SLAB_PART,), F32)
        padded = -(-flat.shape[0] // _SLAB_PART) * _SLAB_PART
        rows.append(jnp.pad(flat, (0, padded - flat.shape[0])).reshape(-1, _LANES))
    return jnp.concatenate(rows, axis=0)


def _from_slab(slab, like):
    out, row = {}, 0
    for name in SLAB_NAMES:
        size = like[name].size
        rows = -(-size // _SLAB_PART) * (_SLAB_PART // _LANES)
        out[name] = slab[row:row + rows].reshape(-1)[:size].reshape(like[name].shape)
        row += rows
    return out


BIG_NAMES = ("ffn1_w_gu", "ffn1_w_down", "w_in", "w_out", "ffn2_w_gu", "ffn2_w_down")
WEIGHT_ORDER = ("ffn1_pre", "ffn1_w_gu", "ffn1_w_down", "ffn1_post", "mix_pre", "w_in", "w_pool", "pool_scale",
                "sinks", "g_pool", "g_attn", "w_out", "mix_post", "ffn2_pre", "ffn2_w_gu", "ffn2_w_down", "ffn2_post")


def kernel(x, positions, ffn1_pre, ffn1_w_gu, ffn1_w_down, ffn1_post, mix_pre, w_in, w_pool, pool_scale, sinks, g_pool, g_attn, w_out, mix_post, ffn2_pre, ffn2_w_gu, ffn2_w_down, ffn2_post, loss_target, m_ffn1_pre, m_ffn1_w_gu, m_ffn1_w_down, m_ffn1_post, m_mix_pre, m_w_in, m_w_pool, m_pool_scale, m_sinks, m_g_pool, m_g_attn, m_w_out, m_mix_post, m_ffn2_pre, m_ffn2_w_gu, m_ffn2_w_down, m_ffn2_post, v_ffn1_pre, v_ffn1_w_gu, v_ffn1_w_down, v_ffn1_post, v_mix_pre, v_w_in, v_w_pool, v_pool_scale, v_sinks, v_g_pool, v_g_attn, v_w_out, v_mix_post, v_ffn2_pre, v_ffn2_w_gu, v_ffn2_w_down, v_ffn2_post):
    given = dict(locals())
    weights = {n: given[n][0] for n in WEIGHT_ORDER}
    moments_m = {n: given["m_" + n][0] for n in WEIGHT_ORDER}
    moments_v = {n: given["v_" + n][0] for n in WEIGHT_ORDER}
    S = x.shape[1]
    shard = (2 * lax.axis_index("x") + lax.axis_index("y")).astype(jnp.int32).reshape(1)

    gather_groups = {"ffn1_up": ("ffn1_w_gu",), "ffn1_down": ("ffn1_w_down",), "mixer": ("w_in", "w_out"),
                     "ffn2": ("ffn2_w_gu", "ffn2_w_down")}
    gather_kinds = {"ffn1_up": GATHER_HALF, "ffn1_down": GATHER, "mixer": GATHER, "ffn2": GATHER}
    first, rest = "ffn1_up", ("ffn1_down", "mixer", "ffn2")
    (first_handle,), token = _exchange_start(
        [gather_kinds[first]], [[weights[n].astype(BF16) for n in gather_groups[first]]], "gather_start_" + first)
    after_first = 1.0 + token[0, 0]
    local16 = {n: (weights[n] * after_first).astype(BF16) for n in BIG_NAMES if n != "w_in"}
    local16["w_in"] = (weights["w_in"] * after_first).T.astype(BF16)
    handles, _ = _exchange_start([gather_kinds[g] for g in rest],
                                 [[local16[n] for n in gather_groups[g]] for g in rest], "gather_start_rest")
    gather_handles = dict(zip(rest, handles), **{first: first_handle})

    def weights_of(group, after):
        kind = gather_kinds[group]
        owns, lands = _exchange_wait(kind, gather_handles[group], list(after), "gather_wait_" + group)
        if kind == GATHER_HALF:
            lands = _swap_gathered_halves(lands, "swap_gathered_" + group)
        return list(zip(lands, owns))

    pending, last_token = {}, []

    def grads_ready(group, grads, small=None):
        names = list(grads)
        kinds, sources = [REDUCE], [[grads[n][1] for n in names]]
        if small is not None:
            kinds, sources = kinds + [BROADCAST], sources + [[_to_slab(small)]]
        handles, token = _exchange_start(kinds, sources, "reduce_start_" + group)
        handle = handles[0]
        if small is not None:
            pending["small"] = handles[1]
        pending[group] = (names, handle, [grads[n][0] for n in names])
        last_token[:] = [token]
        return [token]

    small = {n: (weights[n] if weights[n].ndim > 1 else weights[n].reshape(1, -1)) for n in SMALL_NAMES}
    dx = _local_step(x[0], positions.reshape(S, 1), loss_target[0], small, shard, weights_of, grads_ready)

    grad, delta, new_m, new_v = {}, {}, {}, {}

    def finish(groups, after):
        names, partials = [], []
        for group in groups:
            group_names, handle, own32 = pending[group]
            _, received = _exchange_wait(REDUCE, handle, after, "reduce_wait_" + group)
            names += group_names
            partials += [_chip_partial(g32, rec, shard, "chip_partial_" + n)
                         for n, g32, rec in zip(group_names, own32, received)]
        siblings = _swap_with_sibling(partials, "swap_" + groups[0])
        for name, mine, theirs in zip(names, partials, siblings):
            if name == "w_in":
                mine, theirs = mine.T, theirs.T
            grad[name], delta[name], new_m[name], new_v[name] = _adamw(
                weights[name], moments_m[name], moments_v[name], [mine, theirs], "adamw_" + name)
        return [grad[names[-1]]]

    after = finish(["ffn2"], last_token)
    after = finish(["mixer"], after)
    (own_slab,), (slab_landing,) = _exchange_wait(BROADCAST, pending["small"], after, "reduce_wait_small")
    device = (2 * shard + lax.axis_index("c")).astype(jnp.int32)
    small_like = dict({n: small[n] for n in SMALL_NAMES}, loss=jnp.zeros((8, _LANES), F32))
    slabs = _adamw(_to_slab(small), _to_slab({n: moments_m[n] for n in SMALL_NAMES}),
                   _to_slab({n: moments_v[n] for n in SMALL_NAMES}), [own_slab, slab_landing], "adamw_small",
                   slot=device)
    for store, slab in zip((grad, delta, new_m, new_v), slabs):
        store.update(_from_slab(slab, small_like))
    loss = grad["loss"][0, 0]
    after = finish(["ffn1_gu"], [slabs[0]])
    finish(["ffn1_down"], after)

    def out(store):
        return [store[n].reshape(given[n].shape) for n in WEIGHT_ORDER]
    return (loss, dx[None], *out(grad), *out(delta), *out(new_m), *out(new_v))
```

```python
import jax
import jax.numpy as jnp
from jax import lax
from jax.experimental import pallas as pl
from jax.experimental.pallas import tpu as pltpu

F32 = jnp.float32
BF16 = jnp.bfloat16

D_MODEL = 1024
D_FF = 2816
N_SHARD = 4
FF_CHUNK = D_FF // 2
POOL_WINDOWS = (2, 4, 8, 16)
POOL_WIDTH = 512
POOL_GROUP = 128
HALO = 16
HEAD_DIM = 64
N_HEADS = 8
N_KV_HEADS = 2
ATTN_WIDTH = 512
KV_WIDTH = 128
IN_WIDTH = 1280
BLOCK = 128
ATTN_BLOCKS = 4
ROT_DIM = 16
ROPE_THETA = 500000.0
EPS = 1e-6
NEG_INF = -1e30
ATTN_SCALE = HEAD_DIM ** -0.5

ADAM_LR = 0.001
ADAM_B1 = 0.9
ADAM_B2 = 0.999
ADAM_EPS = 1e-08
ADAM_WD = 0.01
ADAM_STEP = 10

VMEM_LIMIT = 60 * 1024 * 1024
FFN_UP_TILE = 512
FFN_DOWN_TILE = 1024
FFN_BWD_TILE = 256
MIXER_TILE = 512

MESH = pl.DeviceIdType.MESH


def _params(**kw):
    return pltpu.CompilerParams(vmem_limit_bytes=VMEM_LIMIT, **kw)


def _dot(a, b):
    return jnp.dot(a, b, preferred_element_type=F32)


def _dot_nt(a, b):
    return lax.dot_general(a, b, (((1,), (1,)), ((), ())), preferred_element_type=F32)


def _dot_tn(a, b):
    return lax.dot_general(a, b, (((0,), (0,)), ((), ())), preferred_element_type=F32)


def _rstd(x):
    return lax.rsqrt(jnp.mean(x * x, axis=-1, keepdims=True) + EPS)


def _norm_bwd(dy, xn, r, gain):
    dxn = dy * gain
    return r * (dxn - xn * jnp.mean(dxn * xn, axis=-1, keepdims=True))


def _sigmoid(x):
    return 1.0 / (1.0 + jnp.exp(-x))


def _full(shape):
    return pl.BlockSpec(shape, lambda *_: (0,) * len(shape))


def _rows(tile, width):
    return pl.BlockSpec((tile, width), lambda i: (i, 0))


_ANY = pl.BlockSpec(memory_space=pl.ANY)


_SMEM = pl.BlockSpec(memory_space=pltpu.SMEM)


def _load_once(pairs, sem):
    @pl.when(pl.program_id(0) == 0)
    def _():
        copies = [pltpu.make_async_copy(src, dst, sem.at[n]) for n, (src, dst) in enumerate(pairs)]
        for cp in copies:
            cp.start()
        for cp in copies:
            cp.wait()


def _gathered(land_ref, own_ref, vmem_ref, mine, rows=None):
    def dst(slot):
        if rows is None:
            return vmem_ref.at[slot]
        return vmem_ref.at[pl.ds(pl.multiple_of(slot * rows, 16), rows), :]
    pairs = [(land_ref.at[(mine + d) % N_SHARD], dst((mine + d) % N_SHARD)) for d in range(1, N_SHARD)]
    return pairs + [(own_ref, dst(mine))]


def _ignoring(body, start, count):
    def wrapped(*refs):
        return body(*refs[:start], *refs[start + count:])
    return wrapped


def _ffn_up(x, pre, wgu, mine, deps=()):
    S = x.shape[0]
    tm = FFN_UP_TILE

    def body(x_ref, pre_ref, wgu_land, wgu_own, mine_ref, g_ref, u_ref, a_ref, wgu_v, sem):
        _load_once(_gathered(wgu_land, wgu_own, wgu_v, mine_ref[0]), sem)
        xv = x_ref[...]
        h = ((xv * _rstd(xv)) * pre_ref[...]).astype(BF16)
        for c in range(2):
            cols = slice(c * FF_CHUNK, (c + 1) * FF_CHUNK)
            g = _dot(h, wgu_v[c])
            u = _dot(h, wgu_v[2 + c])
            g_ref[:, cols] = g.astype(BF16)
            u_ref[:, cols] = u.astype(BF16)
            a_ref[:, cols] = ((g * _sigmoid(g)) * u).astype(BF16)

    args = [x, pre, *wgu, mine]
    return pl.pallas_call(
        _ignoring(body, len(args), len(deps)), name="ffn_up", grid=(S // tm,),
        in_specs=[_rows(tm, D_MODEL), _full((1, D_MODEL)), _ANY, _ANY, _SMEM] + [_ANY] * len(deps),
        out_specs=[_rows(tm, D_FF)] * 3,
        out_shape=[jax.ShapeDtypeStruct((S, D_FF), BF16)] * 3,
        scratch_shapes=[pltpu.VMEM((N_SHARD, D_MODEL, FF_CHUNK), BF16), pltpu.SemaphoreType.DMA((N_SHARD,))],
        compiler_params=_params(dimension_semantics=("arbitrary",)),
    )(*args, *deps)


def _ffn_down(x, a, post, wd, mine, target=None, deps=()):
    S = x.shape[0]
    tm = FFN_DOWN_TILE
    with_loss = target is not None

    def body(*refs):
        if with_loss:
            (x_ref, a_ref, post_ref, wd_land, wd_own, mine_ref, tgt_ref,
             out_ref, f_ref, loss_ref, dpost_ref, wd_v, sem) = refs
        else:
            x_ref, a_ref, post_ref, wd_land, wd_own, mine_ref, out_ref, f_ref, wd_v, sem = refs
        _load_once(_gathered(wd_land, wd_own, wd_v, mine_ref[0], rows=D_FF // N_SHARD), sem)
        xv = x_ref[...]
        facc = _dot(a_ref[...], wd_v[...])
        rf = _rstd(facc)
        fn = facc * rf
        out = xv + 0.5 * (fn * post_ref[...])
        if with_loss:
            diff = out - tgt_ref[...]
            dout = diff * (1.0 / D_MODEL)
            out_ref[...] = dout

            @pl.when(pl.program_id(0) == 0)
            def _():
                loss_ref[...] = jnp.zeros_like(loss_ref)
                dpost_ref[...] = jnp.zeros_like(dpost_ref)
            loss_ref[...] += jnp.sum(diff * diff)
            dn = 0.5 * dout
            dpost_ref[...] += jnp.sum(dn * fn, axis=0, keepdims=True)
            f_ref[...] = _norm_bwd(dn, fn, rf, post_ref[...]).astype(BF16)
        else:
            f_ref[...] = facc
            out_ref[...] = out

    in_specs = [_rows(tm, D_MODEL), _rows(tm, D_FF), _full((1, D_MODEL)), _ANY, _ANY, _SMEM]
    args = [x, a, post, *wd, mine]
    out_shape = [jax.ShapeDtypeStruct((S, D_MODEL), F32), jax.ShapeDtypeStruct((S, D_MODEL), F32)]
    out_specs = [_rows(tm, D_MODEL), _rows(tm, D_MODEL)]
    if with_loss:
        in_specs.append(_rows(tm, D_MODEL))
        args.append(target)
        out_shape[1] = jax.ShapeDtypeStruct((S, D_MODEL), BF16)
        out_shape += [jax.ShapeDtypeStruct((8, 128), F32), jax.ShapeDtypeStruct((1, D_MODEL), F32)]
        out_specs += [_full((8, 128)), _full((1, D_MODEL))]
    return pl.pallas_call(
        _ignoring(body, len(args), len(deps)), name="ffn_down_loss" if with_loss else "ffn_down",
        grid=(S // tm,), in_specs=in_specs + [_ANY] * len(deps), out_specs=out_specs, out_shape=out_shape,
        scratch_shapes=[pltpu.VMEM((D_FF, D_MODEL), BF16), pltpu.SemaphoreType.DMA((N_SHARD,))],
        compiler_params=_params(dimension_semantics=("arbitrary",)),
    )(*args, *deps)


def _ffn_bwd(dout, x, f, g, u, pre, post, wgu, wd, mine, df_known=False, deps=()):
    S = x.shape[0]
    tm = FFN_BWD_TILE

    def body(*refs):
        if df_known:
            (dout_ref, x_ref, f_ref, g_ref, u_ref, pre_ref, post_ref, wgu_land, wgu_own, wd_land, wd_own, mine_ref,
             dx_ref, h_ref, dgu_ref, dpre_ref, wgu_v, wd_v, sem) = refs
        else:
            (dout_ref, x_ref, f_ref, g_ref, u_ref, pre_ref, post_ref, wgu_land, wgu_own, wd_land, wd_own, mine_ref,
             dx_ref, h_ref, dgu_ref, df_ref, dpre_ref, dpost_ref, wgu_v, wd_v, sem) = refs
        _load_once(_gathered(wgu_land, wgu_own, wgu_v, mine_ref[0])
                   + _gathered(wd_land, wd_own, wd_v, mine_ref[0], rows=D_FF // N_SHARD), sem)

        @pl.when(pl.program_id(0) == 0)
        def _():
            dpre_ref[...] = jnp.zeros_like(dpre_ref)
            if not df_known:
                dpost_ref[...] = jnp.zeros_like(dpost_ref)

        dout_v = dout_ref[...]
        if df_known:
            df = f_ref[...]
        else:
            dn = 0.5 * dout_v
            fv = f_ref[...]
            rf = _rstd(fv)
            fn = fv * rf
            dpost_ref[...] += jnp.sum(dn * fn, axis=0, keepdims=True)
            df = _norm_bwd(dn, fn, rf, post_ref[...]).astype(BF16)
            df_ref[...] = df
        dh = jnp.zeros((tm, D_MODEL), F32)
        for c in range(2):
            cols = slice(c * FF_CHUNK, (c + 1) * FF_CHUNK)
            da = _dot_nt(df, wd_v[cols, :])
            gv = g_ref[:, cols].astype(F32)
            uv = u_ref[:, cols].astype(F32)
            sg = _sigmoid(gv)
            silu = gv * sg
            dg = ((da * uv) * (sg * (1.0 + gv * (1.0 - sg)))).astype(BF16)
            du = (da * silu).astype(BF16)
            dgu_ref[:, cols] = dg
            dgu_ref[:, 2 * FF_CHUNK + c * FF_CHUNK:2 * FF_CHUNK + (c + 1) * FF_CHUNK] = du
            dh = dh + _dot_nt(dg, wgu_v[c]) + _dot_nt(du, wgu_v[2 + c])
        xv = x_ref[...]
        rx = _rstd(xv)
        xn = xv * rx
        h_ref[...] = (xn * pre_ref[...]).astype(BF16)
        dpre_ref[...] += jnp.sum(dh * xn, axis=0, keepdims=True)
        dx_ref[...] = dout_v + _norm_bwd(dh, xn, rx, pre_ref[...])

    args = [dout, x, f, g, u, pre, post, *wgu, *wd, mine]
    out_specs = [_rows(tm, D_MODEL), _rows(tm, D_MODEL), _rows(tm, 2 * D_FF), _rows(tm, D_MODEL),
                 _full((1, D_MODEL)), _full((1, D_MODEL))]
    out_shape = [jax.ShapeDtypeStruct((S, D_MODEL), F32), jax.ShapeDtypeStruct((S, D_MODEL), BF16),
                 jax.ShapeDtypeStruct((S, 2 * D_FF), BF16), jax.ShapeDtypeStruct((S, D_MODEL), BF16),
                 jax.ShapeDtypeStruct((1, D_MODEL), F32), jax.ShapeDtypeStruct((1, D_MODEL), F32)]
    if df_known:
        out_specs = out_specs[:3] + out_specs[4:5]
        out_shape = out_shape[:3] + out_shape[4:5]
    return pl.pallas_call(
        _ignoring(body, len(args), len(deps)), name="ffn_bwd_from_df" if df_known else "ffn_bwd", grid=(S // tm,),
        in_specs=[_rows(tm, D_MODEL), _rows(tm, D_MODEL), _rows(tm, D_MODEL), _rows(tm, D_FF), _rows(tm, D_FF),
                  _full((1, D_MODEL)), _full((1, D_MODEL)), _ANY, _ANY, _ANY, _ANY, _SMEM] + [_ANY] * len(deps),
        out_specs=out_specs, out_shape=out_shape,
        scratch_shapes=[pltpu.VMEM((N_SHARD, D_MODEL, FF_CHUNK), BF16), pltpu.VMEM((D_FF, D_MODEL), BF16),
                        pltpu.SemaphoreType.DMA((2 * N_SHARD,))],
        compiler_params=_params(dimension_semantics=("arbitrary",)),
    )(*args, *deps)


def _wgrad(lhs, rhs, m_block, n_block, name, column_shards=False, tk=2048, deps=()):
    S, M = lhs.shape
    N = rhs.shape[1]
    k_steps = S // tk

    def body(lhs_ref, rhs_ref, out_ref, out16_ref):
        k = pl.program_id(2)

        @pl.when(k == 0)
        def _():
            out_ref[...] = jnp.zeros_like(out_ref)
        out_ref[...] += _dot_tn(lhs_ref[...], rhs_ref[...])

        @pl.when(k == k_steps - 1)
        def _():
            out16_ref[...] = out_ref[...].astype(BF16)

    if column_shards:
        assert N == N_SHARD * n_block
        shape = (N_SHARD, M, n_block)
        out_spec = pl.BlockSpec((None, m_block, n_block), lambda i, j, k: (j, i, 0))
    else:
        shape = (M, N)
        out_spec = pl.BlockSpec((m_block, n_block), lambda i, j, k: (i, j))
    out, out16 = pl.pallas_call(
        _ignoring(body, 2, len(deps)), name=name, grid=(M // m_block, N // n_block, k_steps),
        in_specs=[pl.BlockSpec((tk, m_block), lambda i, j, k: (k, i)),
                  pl.BlockSpec((tk, n_block), lambda i, j, k: (k, j))] + [_ANY] * len(deps),
        out_specs=[out_spec, out_spec],
        out_shape=[jax.ShapeDtypeStruct(shape, F32), jax.ShapeDtypeStruct(shape, BF16)],
        compiler_params=_params(dimension_semantics=("arbitrary", "arbitrary", "arbitrary")),
    )(lhs, rhs, *deps)
    if not column_shards:
        out = out.reshape(N_SHARD, M // N_SHARD, N)
        out16 = out16.reshape(N_SHARD, M // N_SHARD, N)
    return out, out16


def _rope_tables(pos, invf):
    S = pos.shape[0]
    tm = MIXER_TILE

    def body(pos_ref, invf_ref, out_ref):
        ang = pos_ref[...].astype(F32) * invf_ref[...]
        cos, sin = jnp.cos(ang), jnp.sin(ang)
        lane = lax.broadcasted_iota(jnp.int32, ang.shape, 1) % HEAD_DIM
        first = lane < ROT_DIM // 2
        second = (lane >= ROT_DIM // 2) & (lane < ROT_DIM)
        out_ref[0] = jnp.where(lane < ROT_DIM, cos, 1.0)
        out_ref[1] = jnp.where(first, sin, 0.0)
        out_ref[2] = jnp.where(second, sin, 0.0)

    return pl.pallas_call(
        body, name="rope_tables", grid=(S // tm,),
        in_specs=[_rows(tm, 1), _full((1, _LANES))],
        out_specs=pl.BlockSpec((3, tm, _LANES), lambda i: (0, i, 0)),
        out_shape=jax.ShapeDtypeStruct((3, S, _LANES), F32),
        compiler_params=_params(dimension_semantics=("arbitrary",)),
    )(pos, invf)


def _table_spec(tm):
    return pl.BlockSpec((3, tm, _LANES), lambda i: (0, i, 0))


_HALF = ROT_DIM // 2
_LANES = 128


def _rope(t, tables):
    c, s_first, s_second = tables
    return t * c - pltpu.roll(t, _LANES - _HALF, axis=1) * s_first + pltpu.roll(t, _HALF, axis=1) * s_second


def _rope_transposed(t, tables):
    c, s_first, s_second = tables
    return t * c - pltpu.roll(t * s_first, _HALF, axis=1) + pltpu.roll(t * s_second, _LANES - _HALF, axis=1)


def _store_head_variants(ref, t):
    rolled = pltpu.roll(t, HEAD_DIM, axis=1)
    low = lax.broadcasted_iota(jnp.int32, t.shape, 1) < HEAD_DIM
    zero = jnp.zeros_like(t)
    ref[0] = jnp.where(low, t, zero).astype(BF16)
    ref[1] = jnp.where(low, zero, rolled).astype(BF16)
    ref[2] = jnp.where(low, rolled, zero).astype(BF16)
    ref[3] = jnp.where(low, zero, t).astype(BF16)


def _mixer_in_fwd(x, pre, w_in_t, mine, rope, deps=()):
    S = x.shape[0]
    tm = MIXER_TILE

    def body(x_ref, pre_ref, w_land, w_own, mine_ref, rope_ref, u_ref, q_ref, k_ref, v_ref, w_v, sem):
        _load_once(_gathered(w_land, w_own, w_v, mine_ref[0], rows=IN_WIDTH // N_SHARD), sem)
        xv = x_ref[...]
        h = ((xv * _rstd(xv)) * pre_ref[...]).astype(BF16)
        z = _dot_nt(h, w_v[...])
        tables = (rope_ref[0], rope_ref[1], rope_ref[2])
        u_ref[...] = z[:, :POOL_WIDTH]
        for t in range(ATTN_WIDTH // _LANES):
            lo = POOL_WIDTH + t * _LANES
            q_ref[:, t * _LANES:(t + 1) * _LANES] = (_rope(z[:, lo:lo + _LANES], tables) * ATTN_SCALE).astype(BF16)
        kv = POOL_WIDTH + ATTN_WIDTH
        _store_head_variants(k_ref, _rope(z[:, kv:kv + KV_WIDTH], tables))
        _store_head_variants(v_ref, z[:, kv + KV_WIDTH:])

    args = [x, pre, *w_in_t, mine, rope]
    variants = pl.BlockSpec((2 * N_KV_HEADS, tm, KV_WIDTH), lambda i: (0, i, 0))
    return pl.pallas_call(
        _ignoring(body, len(args), len(deps)), name="mixer_in_fwd", grid=(S // tm,),
        in_specs=[_rows(tm, D_MODEL), _full((1, D_MODEL)), _ANY, _ANY, _SMEM, _table_spec(tm)] + [_ANY] * len(deps),
        out_specs=[_rows(tm, POOL_WIDTH), _rows(tm, ATTN_WIDTH), variants, variants],
        out_shape=[jax.ShapeDtypeStruct((S, POOL_WIDTH), F32), jax.ShapeDtypeStruct((S, ATTN_WIDTH), BF16),
                   jax.ShapeDtypeStruct((2 * N_KV_HEADS, S, KV_WIDTH), BF16),
                   jax.ShapeDtypeStruct((2 * N_KV_HEADS, S, KV_WIDTH), BF16)],
        scratch_shapes=[pltpu.VMEM((IN_WIDTH, D_MODEL), BF16), pltpu.SemaphoreType.DMA((N_SHARD,))],
        compiler_params=_params(dimension_semantics=("arbitrary",)),
    )(*args, *deps)


def _mixer_in_bwd(dres, x, pre, w_in_t, mine, du, dq, dk, dv, dk_next, dv_next, rope, deps=()):
    S = x.shape[0]
    tm = MIXER_TILE

    def body(dres_ref, x_ref, pre_ref, w_land, w_own, mine_ref, du_ref, dq_ref, dk_ref, dv_ref, dkx_ref, dvx_ref,
             rope_ref,
             dx_ref, dz_ref, h_ref, dpre_ref, w_v, sem):
        _load_once(_gathered(w_land, w_own, w_v, mine_ref[0], rows=IN_WIDTH // N_SHARD), sem)

        @pl.when(pl.program_id(0) == 0)
        def _():
            dpre_ref[...] = jnp.zeros_like(dpre_ref)

        tables = (rope_ref[0], rope_ref[1], rope_ref[2])
        dz_ref[:, :POOL_WIDTH] = du_ref[...]
        for t in range(ATTN_WIDTH // _LANES):
            lo = POOL_WIDTH + t * _LANES
            dz_ref[:, lo:lo + _LANES] = _rope_transposed(dq_ref[:, t * _LANES:(t + 1) * _LANES], tables).astype(BF16)
        kv = POOL_WIDTH + ATTN_WIDTH
        has_next = pl.program_id(0) + 1 < steps
        pad = jnp.zeros((tm - BLOCK, KV_WIDTH), F32)
        dk_tile = dk_ref[...] + jnp.concatenate([pad, jnp.where(has_next, dkx_ref[...], 0.0)], axis=0)
        dv_tile = dv_ref[...] + jnp.concatenate([pad, jnp.where(has_next, dvx_ref[...], 0.0)], axis=0)
        dz_ref[:, kv:kv + KV_WIDTH] = _rope_transposed(dk_tile, tables).astype(BF16)
        dz_ref[:, kv + KV_WIDTH:] = dv_tile.astype(BF16)
        dh = _dot(dz_ref[...], w_v[...])
        xv = x_ref[...]
        rx = _rstd(xv)
        xn = xv * rx
        h_ref[...] = (xn * pre_ref[...]).astype(BF16)
        dpre_ref[...] += jnp.sum(dh * xn, axis=0, keepdims=True)
        dx_ref[...] = dres_ref[...] + _norm_bwd(dh, xn, rx, pre_ref[...])

    assert tm == ATTN_BLOCKS * BLOCK
    steps = S // tm
    nxt = pl.BlockSpec((None, BLOCK, KV_WIDTH), lambda i: (jnp.minimum(i + 1, steps - 1), 0, 0))
    args = [dres, x, pre, *w_in_t, mine, du, dq, dk, dv, dk_next, dv_next, rope]
    return pl.pallas_call(
        _ignoring(body, len(args), len(deps)), name="mixer_in_bwd", grid=(S // tm,),
        in_specs=[_rows(tm, D_MODEL), _rows(tm, D_MODEL), _full((1, D_MODEL)), _ANY, _ANY, _SMEM,
                  _rows(tm, POOL_WIDTH), _rows(tm, ATTN_WIDTH), _rows(tm, KV_WIDTH), _rows(tm, KV_WIDTH), nxt, nxt,
                  _table_spec(tm)] + [_ANY] * len(deps),
        out_specs=[_rows(tm, D_MODEL), _rows(tm, IN_WIDTH), _rows(tm, D_MODEL), _full((1, D_MODEL))],
        out_shape=[jax.ShapeDtypeStruct((S, D_MODEL), F32), jax.ShapeDtypeStruct((S, IN_WIDTH), BF16),
                   jax.ShapeDtypeStruct((S, D_MODEL), BF16), jax.ShapeDtypeStruct((1, D_MODEL), F32)],
        scratch_shapes=[pltpu.VMEM((IN_WIDTH, D_MODEL), BF16), pltpu.SemaphoreType.DMA((N_SHARD,))],
        compiler_params=_params(dimension_semantics=("arbitrary",)),
    )(*args, *deps)


def _pool_counts(tile_index, tm, width):
    t = tile_index * tm + lax.broadcasted_iota(jnp.int32, (tm, 1), 0)
    return jnp.minimum(t + 1, width).astype(F32)


def _pool_features(ext, u_tile, tile_index, tm):
    ds = []
    for gi, width in enumerate(POOL_WINDOWS):
        lanes = slice(gi * POOL_GROUP, (gi + 1) * POOL_GROUP)
        s = ext[:, lanes]
        shift = 1
        while shift < width:
            s = s + pltpu.roll(s, shift, axis=0)
            shift *= 2
        ds.append(s[HALO:, :] / _pool_counts(tile_index, tm, width) - u_tile[:, lanes])
    return ds


def _pool_fwd(u, w_pool, pool_scale, g_pool):
    S = u.shape[0]
    tm = MIXER_TILE

    def body(u_ref, w_ref, scale_ref, gain_ref, y_ref, ext_ref):
        i = pl.program_id(0)

        @pl.when(i == 0)
        def _():
            ext_ref[:HALO, :] = jnp.zeros((HALO, POOL_WIDTH), F32)

        u_tile = u_ref[...]
        ext_ref[HALO:, :] = u_tile
        ds = _pool_features(ext_ref[...], u_tile, i, tm)
        ext_ref[:HALO, :] = u_tile[tm - HALO:, :]
        ys = [_dot(ds[gi].astype(BF16), w_ref[gi].astype(BF16)) for gi in range(len(POOL_WINDOWS))]
        po = jnp.concatenate(ys, axis=1) * scale_ref[...]
        y_ref[...] = ((po * _rstd(po)) * gain_ref[...]).astype(BF16)

    return pl.pallas_call(
        body, name="pool_fwd", grid=(S // tm,),
        in_specs=[_rows(tm, POOL_WIDTH), _full((len(POOL_WINDOWS), POOL_GROUP, POOL_GROUP)),
                  _full((1, POOL_WIDTH)), _full((1, POOL_WIDTH))],
        out_specs=_rows(tm, POOL_WIDTH),
        out_shape=jax.ShapeDtypeStruct((S, POOL_WIDTH), BF16),
        scratch_shapes=[pltpu.VMEM((HALO + tm, POOL_WIDTH), F32)],
        compiler_params=_params(dimension_semantics=("arbitrary",)),
    )(u, w_pool, pool_scale, g_pool)


def _pool_bwd(dy, u, w_pool, pool_scale, g_pool):
    S = u.shape[0]
    tm = MIXER_TILE
    n_tiles = S // tm
    halo_blocks = tm // HALO

    def body(dy_ref, u_ref, uprev_ref, w_ref, scale_ref, gain_ref,
             du_ref, dw_ref, dscale_ref, dgain_ref, ext_ref, nxt_ref):
        i = pl.program_id(0)
        tile = n_tiles - 1 - i

        @pl.when(i == 0)
        def _():
            dw_ref[...] = jnp.zeros_like(dw_ref)
            dscale_ref[...] = jnp.zeros_like(dscale_ref)
            dgain_ref[...] = jnp.zeros_like(dgain_ref)
            nxt_ref[...] = jnp.zeros_like(nxt_ref)

        u_tile = u_ref[...]
        ext_ref[:HALO, :] = jnp.where(tile > 0, uprev_ref[...], 0.0)
        ext_ref[HALO:, :] = u_tile
        ds = _pool_features(ext_ref[...], u_tile, tile, tm)
        dsb = [d.astype(BF16) for d in ds]
        wb = [w_ref[gi].astype(BF16) for gi in range(len(POOL_WINDOWS))]
        yraw = jnp.concatenate([_dot(dsb[gi], wb[gi]) for gi in range(len(POOL_WINDOWS))], axis=1)
        po = yraw * scale_ref[...]
        r = _rstd(po)
        pn = po * r
        dyv = dy_ref[...]
        dgain_ref[...] += jnp.sum(dyv * pn, axis=0, keepdims=True)
        dpo = _norm_bwd(dyv, pn, r, gain_ref[...])
        dscale_ref[...] += jnp.sum(dpo * yraw, axis=0, keepdims=True)
        dyraw = (dpo * scale_ref[...]).astype(BF16)
        for gi, width in enumerate(POOL_WINDOWS):
            lanes = slice(gi * POOL_GROUP, (gi + 1) * POOL_GROUP)
            dw_ref[gi] += _dot_tn(dsb[gi], dyraw[:, lanes])
            dd = _dot_nt(dyraw[:, lanes], wb[gi])
            ddc = dd / _pool_counts(tile, tm, width)
            ext_ref[:tm, lanes] = ddc
            ext_ref[tm:, lanes] = nxt_ref[:, lanes]
            s = ext_ref[:, lanes]
            shift = 1
            while shift < width:
                s = s + pltpu.roll(s, HALO + tm - shift, axis=0)
                shift *= 2
            du_ref[:, lanes] = (s[:tm, :] - dd).astype(BF16)
            nxt_ref[:, lanes] = ddc[:HALO, :]

    return pl.pallas_call(
        body, name="pool_bwd", grid=(n_tiles,),
        in_specs=[pl.BlockSpec((tm, POOL_WIDTH), lambda i: (n_tiles - 1 - i, 0)),
                  pl.BlockSpec((tm, POOL_WIDTH), lambda i: (n_tiles - 1 - i, 0)),
                  pl.BlockSpec((HALO, POOL_WIDTH), lambda i: (jnp.maximum((n_tiles - 1 - i) * halo_blocks - 1, 0), 0)),
                  _full((len(POOL_WINDOWS), POOL_GROUP, POOL_GROUP)), _full((1, POOL_WIDTH)), _full((1, POOL_WIDTH))],
        out_specs=[pl.BlockSpec((tm, POOL_WIDTH), lambda i: (n_tiles - 1 - i, 0)),
                   _full((len(POOL_WINDOWS), POOL_GROUP, POOL_GROUP)), _full((1, POOL_WIDTH)), _full((1, POOL_WIDTH))],
        out_shape=[jax.ShapeDtypeStruct((S, POOL_WIDTH), BF16),
                   jax.ShapeDtypeStruct((len(POOL_WINDOWS), POOL_GROUP, POOL_GROUP), F32),
                   jax.ShapeDtypeStruct((1, POOL_WIDTH), F32), jax.ShapeDtypeStruct((1, POOL_WIDTH), F32)],
        scratch_shapes=[pltpu.VMEM((HALO + tm, POOL_WIDTH), F32), pltpu.VMEM((HALO, POOL_WIDTH), F32)],
        compiler_params=_params(dimension_semantics=("arbitrary",)),
    )(dy, u, u, w_pool, pool_scale, g_pool)


def _variant(head):
    return 2 * (head // (N_HEADS // N_KV_HEADS)) + head % 2


def _own_block(shape=(BLOCK, BLOCK)):
    r = lax.broadcasted_iota(jnp.int32, shape, 0)
    i = lax.broadcasted_iota(jnp.int32, shape, 1)
    return r <= i


def _fold_band(own, from_own, from_prev):
    return jnp.where(own, from_own, from_prev)


def _scores_by_head(own_tiles, prev_tiles, q_tiles):
    stacks = [jnp.concatenate(q_tiles[:2], axis=0), jnp.concatenate(q_tiles[2:], axis=0)]
    by_var = [_dot_nt(jnp.concatenate([own_tiles[v], prev_tiles[v]], axis=0), stacks[v // 2])
              for v in range(2 * N_KV_HEADS)]
    quadrant = lambda h, rows: by_var[_variant(h)][rows * BLOCK:(rows + 1) * BLOCK,
                                                   ((h // 2) % 2) * BLOCK:((h // 2) % 2 + 1) * BLOCK]
    return [quadrant(h, 0) for h in range(N_HEADS)], [quadrant(h, 1) for h in range(N_HEADS)]


def _softmax_t(s, sink):
    m = jnp.maximum(jnp.max(s, axis=0, keepdims=True), sink)
    p = jnp.exp(s - m)
    p_sink = jnp.exp(sink - m)
    inv = 1.0 / (jnp.sum(p, axis=0, keepdims=True) + p_sink)
    return p * inv, p_sink * inv


def _attn_fwd(q, kz, vz, sinks, g_attn, y_pool, x, w_out, mine, post):
    S = q.shape[0]
    tq = ATTN_BLOCKS * BLOCK
    n_var = 2 * N_KV_HEADS

    def body(q_ref, kp_ref, kc_ref, vp_ref, vc_ref, sinks_ref, gain_ref, yp_ref, x_ref, w_land, w_own, mine_ref,
             post_ref, o_ref, out_ref, m_ref, y_ref, w_v, sem):
        _load_once(_gathered(w_land, w_own, w_v, mine_ref[0], rows=D_MODEL // N_SHARD), sem)
        step = pl.program_id(0)
        own = _own_block()
        zero = jnp.zeros((BLOCK, BLOCK), F32)

        def tiles(cur_ref, prev_ref, j):
            rows = lambda jj: slice(jj * BLOCK, (jj + 1) * BLOCK)
            return ([cur_ref[v, rows(j), :] for v in range(n_var)],
                    [prev_ref[v] if j == 0 else cur_ref[v, rows(j - 1), :] for v in range(n_var)])

        scores = []
        for j in range(ATTN_BLOCKS):
            q_pairs = [q_ref[j * BLOCK:(j + 1) * BLOCK, i * _LANES:(i + 1) * _LANES] for i in range(N_HEADS // 2)]
            scores.append(_scores_by_head(*tiles(kc_ref, kp_ref, j), q_pairs))
        probs = []
        for j in range(ATTN_BLOCKS):
            s_own, s_prev = scores[j]
            no_prev = jnp.where(step > 0, 0.0, NEG_INF) if j == 0 else 0.0
            p_own, p_prev = [], []
            for h in range(N_HEADS):
                p, _ = _softmax_t(_fold_band(own, s_own[h], s_prev[h] + no_prev), sinks_ref[0, h])
                p_own.append(jnp.where(own, p, zero).astype(BF16))
                p_prev.append(jnp.where(own, zero, p).astype(BF16))
            probs.append((p_own, p_prev))
        blocks = []
        for j in range(ATTN_BLOCKS):
            p_own, p_prev = probs[j]
            v_own, v_prev = tiles(vc_ref, vp_ref, j)
            pairs = []
            for i in range(N_HEADS // 2):
                acc = None
                for h in (2 * i, 2 * i + 1):
                    part = _dot_tn(p_own[h], v_own[_variant(h)]) + _dot_tn(p_prev[h], v_prev[_variant(h)])
                    acc = part if acc is None else acc + part
                pairs.append(acc)
            blocks.append(jnp.concatenate(pairs, axis=1))
        o = jnp.concatenate(blocks, axis=0)
        o_ref[...] = o
        y_ref[:, :POOL_WIDTH] = yp_ref[...]
        y_ref[:, POOL_WIDTH:] = ((o * _rstd(o)) * gain_ref[...]).astype(BF16)
        m = _dot(y_ref[...], w_v[...])
        m_ref[...] = m
        out_ref[...] = x_ref[...] + (m * _rstd(m)) * post_ref[...]

    prev = pl.BlockSpec((n_var, BLOCK, KV_WIDTH), lambda g: (0, jnp.maximum(g * ATTN_BLOCKS - 1, 0), 0))
    cur = pl.BlockSpec((n_var, tq, KV_WIDTH), lambda g: (0, g, 0))
    return pl.pallas_call(
        body, name="attn_fwd", grid=(S // tq,),
        in_specs=[_rows(tq, ATTN_WIDTH), prev, cur, prev, cur,
                  pl.BlockSpec(memory_space=pltpu.SMEM), _full((1, ATTN_WIDTH)),
                  _rows(tq, POOL_WIDTH), _rows(tq, D_MODEL), _ANY, _ANY, _SMEM, _full((1, D_MODEL))],
        out_specs=[_rows(tq, ATTN_WIDTH), _rows(tq, D_MODEL), _rows(tq, D_MODEL), _rows(tq, D_MODEL)],
        out_shape=[jax.ShapeDtypeStruct((S, ATTN_WIDTH), F32), jax.ShapeDtypeStruct((S, D_MODEL), F32),
                   jax.ShapeDtypeStruct((S, D_MODEL), F32), jax.ShapeDtypeStruct((S, D_MODEL), BF16)],
        scratch_shapes=[pltpu.VMEM((D_MODEL, D_MODEL), BF16), pltpu.SemaphoreType.DMA((N_SHARD,))],
        compiler_params=_params(dimension_semantics=("arbitrary",)),
    )(q, kz, kz, vz, vz, sinks, g_attn, y_pool, x, *w_out, mine, post)


def _attn_bwd(dout, m, w_out, mine, post, o, q, kz, vz, sinks, g_attn, deps=()):
    S = q.shape[0]
    tq = ATTN_BLOCKS * BLOCK
    n_var = 2 * N_KV_HEADS

    def body(dout_ref, m_ref, w_land, w_own, mine_ref, post_ref, o_ref, q_ref, kp_ref, kc_ref, vp_ref, vc_ref,
             sinks_ref, gain_ref, dyp_ref, dm_ref, dpost_ref, dq_ref, dk_ref, dv_ref, dkx_ref, dvx_ref, dsink_ref,
             dgain_ref, w_v, sem):
        _load_once(_gathered(w_land, w_own, w_v, mine_ref[0], rows=D_MODEL // N_SHARD), sem)
        step = pl.program_id(0)

        @pl.when(step == 0)
        def _():
            dpost_ref[...] = jnp.zeros_like(dpost_ref)
            dsink_ref[...] = jnp.zeros_like(dsink_ref)
            dgain_ref[...] = jnp.zeros_like(dgain_ref)

        mv = m_ref[...]
        rm = _rstd(mv)
        mn = mv * rm
        dres = dout_ref[...]
        dpost_ref[...] += jnp.sum(dres * mn, axis=0, keepdims=True)
        dm = _norm_bwd(dres, mn, rm, post_ref[...]).astype(BF16)
        dm_ref[...] = dm
        dy = _dot_nt(dm, w_v[...])
        dyp_ref[...] = dy[:, :POOL_WIDTH]
        ov = o_ref[...]
        r = _rstd(ov)
        on = ov * r
        dyv = dy[:, POOL_WIDTH:]
        dgain_ref[...] += jnp.sum(dyv * on, axis=0, keepdims=True)
        do = _norm_bwd(dyv, on, r, gain_ref[...]).astype(BF16)
        own = _own_block()
        zero = jnp.zeros((BLOCK, BLOCK), F32)
        split = lambda t: (jnp.where(own, t, zero).astype(BF16), jnp.where(own, zero, t).astype(BF16))
        rows = lambda j: slice(j * BLOCK, (j + 1) * BLOCK)
        heads = range(N_HEADS)

        def tiles(cur_ref, prev_ref, j):
            return ([cur_ref[v, rows(j), :] for v in range(n_var)],
                    [prev_ref[v] if j == 0 else cur_ref[v, rows(j - 1), :] for v in range(n_var)])

        q_pairs = [[q_ref[rows(j), i * _LANES:(i + 1) * _LANES] for i in range(N_HEADS // 2)] for j in range(ATTN_BLOCKS)]
        do_pairs = [[do[rows(j), i * _LANES:(i + 1) * _LANES] for i in range(N_HEADS // 2)] for j in range(ATTN_BLOCKS)]
        scores = [(_scores_by_head(*tiles(kc_ref, kp_ref, j), q_pairs[j]),
                   _scores_by_head(*tiles(vc_ref, vp_ref, j), do_pairs[j])) for j in range(ATTN_BLOCKS)]
        parts, sink_sum = [], None
        for j in range(ATTN_BLOCKS):
            (s_own, s_prev), (dp_own, dp_prev) = scores[j]
            no_prev = jnp.where(step > 0, 0.0, NEG_INF) if j == 0 else 0.0
            ds_parts, p_parts, sink_rows = [], [], []
            for h in heads:
                p, p_sink = _softmax_t(_fold_band(own, s_own[h], s_prev[h] + no_prev), sinks_ref[0, h])
                dp = _fold_band(own, dp_own[h], dp_prev[h])
                delta = jnp.sum(p * dp, axis=0, keepdims=True)
                ds_parts.append(split(p * (dp - delta)))
                p_parts.append(split(p))
                sink_rows.append(jnp.zeros((1, _LANES), F32) - jnp.sum(p_sink * delta))
            block_sinks = jnp.concatenate(sink_rows, axis=0)
            sink_sum = block_sinks if sink_sum is None else sink_sum + block_sinks
            parts.append((ds_parts, p_parts))
        dsink_ref[...] += sink_sum
        low = lax.broadcasted_iota(jnp.int32, (BLOCK, _LANES), 1) < HEAD_DIM

        def merge(acc):
            return jnp.where(low, acc[0] + pltpu.roll(acc[1], HEAD_DIM, axis=1),
                             acc[3] + pltpu.roll(acc[2], HEAD_DIM, axis=1))
        add = lambda acc, var, t: acc.__setitem__(var, t if acc[var] is None else acc[var] + t)
        k_own, k_prev, v_own, v_prev = [], [], [], []
        for j in range(ATTN_BLOCKS):
            ds_parts, p_parts = parts[j]
            kt_own, kt_prev = tiles(kc_ref, kp_ref, j)
            dk_own, dk_prev, dv_own, dv_prev = ([None] * n_var for _ in range(4))
            for i in range(N_HEADS // 2):
                dq_pair = None
                for h in (2 * i, 2 * i + 1):
                    var = _variant(h)
                    (ds_o, ds_p), (p_o, p_p) = ds_parts[h], p_parts[h]
                    part = _dot_tn(ds_o, kt_own[var]) + _dot_tn(ds_p, kt_prev[var])
                    dq_pair = part if dq_pair is None else dq_pair + part
                    add(dk_own, var, _dot(ds_o, q_pairs[j][i]))
                    add(dk_prev, var, _dot(ds_p, q_pairs[j][i]))
                    add(dv_own, var, _dot(p_o, do_pairs[j][i]))
                    add(dv_prev, var, _dot(p_p, do_pairs[j][i]))
                dq_ref[rows(j), i * _LANES:(i + 1) * _LANES] = dq_pair * ATTN_SCALE
            k_own.append(merge(dk_own))
            k_prev.append(merge(dk_prev))
            v_own.append(merge(dv_own))
            v_prev.append(merge(dv_prev))
        for j in range(ATTN_BLOCKS):
            last = j == ATTN_BLOCKS - 1
            dk_ref[rows(j), :] = k_own[j] if last else k_own[j] + k_prev[j + 1]
            dv_ref[rows(j), :] = v_own[j] if last else v_own[j] + v_prev[j + 1]
        dkx_ref[...] = k_prev[0]
        dvx_ref[...] = v_prev[0]

    steps = S // tq
    prev = pl.BlockSpec((n_var, BLOCK, KV_WIDTH), lambda g: (0, jnp.maximum(g * ATTN_BLOCKS - 1, 0), 0))
    cur = pl.BlockSpec((n_var, tq, KV_WIDTH), lambda g: (0, g, 0))
    nxt = pl.BlockSpec((None, BLOCK, KV_WIDTH), lambda g: (g, 0, 0))
    args = [dout, m, *w_out, mine, post, o, q, kz, kz, vz, vz, sinks, g_attn]
    return pl.pallas_call(
        _ignoring(body, len(args), len(deps)), name="attn_bwd", grid=(steps,),
        in_specs=[_rows(tq, D_MODEL), _rows(tq, D_MODEL), _ANY, _ANY, _SMEM, _full((1, D_MODEL)),
                  _rows(tq, ATTN_WIDTH), _rows(tq, ATTN_WIDTH), prev, cur, prev, cur,
                  pl.BlockSpec(memory_space=pltpu.SMEM), _full((1, ATTN_WIDTH))] + [_ANY] * len(deps),
        out_specs=[_rows(tq, POOL_WIDTH), _rows(tq, D_MODEL), _full((1, D_MODEL)),
                   _rows(tq, ATTN_WIDTH), _rows(tq, KV_WIDTH), _rows(tq, KV_WIDTH), nxt, nxt,
                   _full((N_HEADS, _LANES)), _full((1, ATTN_WIDTH))],
        out_shape=[jax.ShapeDtypeStruct((S, POOL_WIDTH), F32), jax.ShapeDtypeStruct((S, D_MODEL), BF16),
                   jax.ShapeDtypeStruct((1, D_MODEL), F32),
                   jax.ShapeDtypeStruct((S, ATTN_WIDTH), F32), jax.ShapeDtypeStruct((S, KV_WIDTH), F32),
                   jax.ShapeDtypeStruct((S, KV_WIDTH), F32),
                   jax.ShapeDtypeStruct((steps, BLOCK, KV_WIDTH), F32), jax.ShapeDtypeStruct((steps, BLOCK, KV_WIDTH), F32),
                   jax.ShapeDtypeStruct((N_HEADS, _LANES), F32), jax.ShapeDtypeStruct((1, ATTN_WIDTH), F32)],
        scratch_shapes=[pltpu.VMEM((D_MODEL, D_MODEL), BF16), pltpu.SemaphoreType.DMA((N_SHARD,))],
        compiler_params=_params(dimension_semantics=("arbitrary",)),
    )(*args, *deps)


def _inv_freq_row():
    inv_freq = ROPE_THETA ** (-jnp.arange(0, ROT_DIM, 2, dtype=F32) / ROT_DIM)
    per_head = jnp.concatenate([inv_freq, inv_freq, jnp.zeros((HEAD_DIM - ROT_DIM,), F32)])
    return jnp.tile(per_head, _LANES // HEAD_DIM).reshape(1, _LANES)


def _local_step(x, pos, target, small, mine, weights_of, grads_ready):
    rope = _rope_tables(pos, _inv_freq_row())
    (wgu1,) = weights_of("ffn1_up", (rope,))
    g1, u1, a1 = _ffn_up(x, small["ffn1_pre"], wgu1, mine)
    (wd1,) = weights_of("ffn1_down", (a1,))
    x1, f1 = _ffn_down(x, a1, small["ffn1_post"], wd1, mine)
    w_in_t, w_out = weights_of("mixer", (x1,))
    u, q, kz, vz = _mixer_in_fwd(x1, small["mix_pre"], w_in_t, mine, rope)
    y_pool = _pool_fwd(u, small["w_pool"], small["pool_scale"], small["g_pool"])
    o, x2, m, y = _attn_fwd(q, kz, vz, small["sinks"], small["g_attn"], y_pool, x1, w_out, mine, small["mix_post"])
    wgu2, wd2 = weights_of("ffn2", (x2,))
    g2, u2, a2 = _ffn_up(x2, small["ffn2_pre"], wgu2, mine)
    dx3, df2, loss_acc, dpost2 = _ffn_down(x2, a2, small["ffn2_post"], wd2, mine, target=target)
    grads = {"loss": loss_acc * (0.5 / D_MODEL), "ffn2_post": dpost2}
    dx2, h3, dgu2, grads["ffn2_pre"] = _ffn_bwd(
        dx3, x2, df2, g2, u2, small["ffn2_pre"], small["ffn2_post"], wgu2, wd2, mine, df_known=True)
    dwgu2 = _wgrad(h3, dgu2, D_MODEL, FF_CHUNK, "wgrad_gu2", column_shards=True)
    dwd2 = _wgrad(a2, df2, FF_CHUNK, D_MODEL, "wgrad_down2")
    deps = grads_ready("ffn2", {"ffn2_w_gu": dwgu2, "ffn2_w_down": dwd2})
    dy_pool, dm, grads["mix_post"], dq, dk, dv, dk_next, dv_next, dsinks, grads["g_attn"] = _attn_bwd(
        dx2, m, w_out, mine, small["mix_post"], o, q, kz, vz, small["sinks"], small["g_attn"], deps=deps)
    dw_out = _wgrad(y, dm, D_MODEL, D_MODEL, "wgrad_out")
    grads["sinks"] = dsinks[:, 0].reshape(1, N_HEADS)
    du, grads["w_pool"], grads["pool_scale"], grads["g_pool"] = _pool_bwd(
        dy_pool, u, small["w_pool"], small["pool_scale"], small["g_pool"])
    dx1, dz, h2, grads["mix_pre"] = _mixer_in_bwd(dx2, x1, small["mix_pre"], w_in_t, mine, du, dq, dk, dv, dk_next, dv_next, rope)
    dw_in_t = _wgrad(dz, h2, IN_WIDTH, D_MODEL, "wgrad_in")
    deps = grads_ready("mixer", {"w_in": dw_in_t, "w_out": dw_out})
    dx, h1, dgu1, df1, grads["ffn1_pre"], grads["ffn1_post"] = _ffn_bwd(
        dx1, x, f1, g1, u1, small["ffn1_pre"], small["ffn1_post"], wgu1, wd1, mine, deps=deps)
    dwgu1 = _wgrad(h1, dgu1, D_MODEL, FF_CHUNK, "wgrad_gu1", column_shards=True)
    deps = grads_ready("ffn1_gu", {"ffn1_w_gu": dwgu1}, small=grads)
    dwd1 = _wgrad(a1, df1, FF_CHUNK, D_MODEL, "wgrad_down1", deps=deps)
    grads_ready("ffn1_down", {"ffn1_w_down": dwd1})
    return dx


def _place():
    return lax.axis_index("x"), lax.axis_index("y"), lax.axis_index("c")


def _other_chips(x, y):
    return [(1 - x, 1 - y), (1 - x, y), (x, 1 - y)]


_HBM = pl.BlockSpec(memory_space=pltpu.HBM)
_SEM = pl.BlockSpec(memory_space=pltpu.SEMAPHORE)
_EFFECT = pltpu.SideEffectType.DATAFLOW_SIDE_EFFECTING
GATHER, GATHER_HALF, REDUCE, BROADCAST = "gather", "gather_half", "reduce", "broadcast"
N_DEVICES = 8


def _in_hbm(a):
    return pltpu.with_memory_space_constraint(a, pltpu.HBM)


def _core_half(rows, c):
    return pl.ds(pl.multiple_of(c * (rows // 2), 16), rows // 2)


def _chip_copies(kind, srcs, lands, send_sems, recv_sems):
    x, y, c = _place()
    mine = 2 * x + y
    copies = []
    for w in range(len(srcs)):
        if kind == BROADCAST:
            peers = [(x ^ (k >> 2), y ^ ((k >> 1) & 1), c ^ (k & 1)) for k in range(1, N_DEVICES)]
        else:
            peers = [(px, py, c) for px, py in _other_chips(x, y)]
        for k, (px, py, pc) in enumerate(peers):
            if kind == GATHER:
                src, dst = srcs[w], lands[w].at[mine]
            elif kind == GATHER_HALF:
                half = _core_half(srcs[w].shape[0], c)
                src, dst = srcs[w].at[half, :], lands[w].at[mine, half, :]
            elif kind == BROADCAST:
                src, dst = srcs[w], lands[w].at[2 * mine + c]
            else:
                src, dst = srcs[w].at[2 * px + py], lands[w].at[k]
            pair = len(peers) * w + k
            copies.append(pltpu.make_async_remote_copy(
                src_ref=src, dst_ref=dst, send_sem=send_sems.at[pair], recv_sem=recv_sems.at[pair],
                device_id=(px, py, pc), device_id_type=MESH))
    return copies


def _landing_shape(kind, src):
    if kind == REDUCE:
        return (N_SHARD - 1,) + src.shape[1:]
    return ((N_DEVICES if kind == BROADCAST else N_SHARD),) + src.shape


def _peer_count(kind):
    return N_DEVICES - 1 if kind == BROADCAST else N_SHARD - 1


def _exchange_start(kinds, groups, name):
    sizes = [len(g) for g in groups]
    flat = [s for g in groups for s in g]
    n, ng = len(flat), len(groups)

    def body(*refs):
        srcs, lands = refs[:n], refs[n:2 * n]
        sems = refs[2 * n:2 * n + 2 * ng]
        token = refs[-1]
        start = 0
        for gi, size in enumerate(sizes):
            for cp in _chip_copies(kinds[gi], srcs[start:start + size], lands[start:start + size],
                                   sems[2 * gi], sems[2 * gi + 1]):
                cp.start()
            start += size
        token[...] = jnp.zeros_like(token)

    landings = [lax.empty(_landing_shape(kind, s), s.dtype) for kind, g in zip(kinds, groups) for s in g]
    sem_shapes = [pltpu.SemaphoreType.DMA((size * _peer_count(kind),)) for kind, size in zip(kinds, sizes)
                  for _ in range(2)]
    outs = pl.pallas_call(
        body, name=name,
        in_specs=[_HBM] * (2 * n),
        out_specs=[_SEM] * (2 * ng) + [_HBM] * (2 * n) + [pl.BlockSpec(memory_space=pltpu.VMEM)],
        out_shape=sem_shapes + [pltpu.HBM(a.shape, a.dtype) for a in flat + landings]
        + [jax.ShapeDtypeStruct((8, _LANES), F32)],
        input_output_aliases={i: 2 * ng + i for i in range(2 * n)},
        compiler_params=pltpu.CompilerParams(has_side_effects=_EFFECT),
    )(*[_in_hbm(a) for a in flat + landings])
    sems, srcs, lands, token = outs[:2 * ng], outs[2 * ng:2 * ng + n], outs[2 * ng + n:2 * ng + 2 * n], outs[-1]
    handles, start = [], 0
    for gi, size in enumerate(sizes):
        handles.append((sems[2 * gi], sems[2 * gi + 1], srcs[start:start + size], lands[start:start + size]))
        start += size
    return handles, token


def _exchange_wait(kind, handle, after, name):
    send_sems, recv_sems, srcs, lands = handle
    n = len(srcs)

    def body(*refs):
        copies = _chip_copies(kind, refs[:n], refs[n:2 * n], refs[2 * n], refs[2 * n + 1])
        for cp in copies:
            cp.wait_send()
        for cp in copies:
            cp.wait_recv()

    outs = pl.pallas_call(
        body, name=name,
        in_specs=[_HBM] * (2 * n) + [_SEM, _SEM] + [_ANY] * len(after),
        out_specs=[_HBM] * (2 * n),
        out_shape=[pltpu.HBM(a.shape, a.dtype) for a in list(srcs) + list(lands)],
        input_output_aliases={i: i for i in range(2 * n)},
        compiler_params=pltpu.CompilerParams(has_side_effects=_EFFECT),
    )(*srcs, *lands, send_sems, recv_sems, *after)
    return outs[:n], outs[n:]


def _swap_gathered_halves(lands, name):
    n = len(lands)

    def body(*refs):
        bufs = refs[n:2 * n]
        send_sems, recv_sems = refs[2 * n:]
        x, y, c = _place()
        mine = 2 * x + y
        sends, arrivals = [], []
        for w in range(n):
            rows = bufs[w].shape[1]
            for d in range(1, N_SHARD):
                slot = (mine + d) % N_SHARD
                sems = dict(send_sem=send_sems.at[(N_SHARD - 1) * w + d - 1],
                            recv_sem=recv_sems.at[(N_SHARD - 1) * w + d - 1],
                            device_id=(x, y, 1 - c), device_id_type=MESH)
                fetched = bufs[w].at[slot, _core_half(rows, c), :]
                missing = bufs[w].at[slot, _core_half(rows, 1 - c), :]
                sends.append(pltpu.make_async_remote_copy(src_ref=fetched, dst_ref=fetched, **sems))
                arrivals.append(pltpu.make_async_remote_copy(src_ref=missing, dst_ref=missing, **sems))
        for cp in sends:
            cp.start()
        for cp in arrivals:
            cp.wait_recv()
        for cp in sends:
            cp.wait_send()

    return pl.pallas_call(
        body, name=name, in_specs=[_ANY] * n, out_specs=[_ANY] * n,
        out_shape=[jax.ShapeDtypeStruct(a.shape, a.dtype) for a in lands],
        input_output_aliases={i: i for i in range(n)},
        scratch_shapes=[pltpu.SemaphoreType.DMA((n * (N_SHARD - 1),)), pltpu.SemaphoreType.DMA((n * (N_SHARD - 1),))],
        compiler_params=pltpu.CompilerParams(has_side_effects=True),
    )(*lands)


def _swap_with_sibling(partials, name):
    n = len(partials)

    def body(*refs):
        ins, outs = refs[:n], refs[n:2 * n]
        send_sems, recv_sems = refs[2 * n:]
        x, y, c = _place()
        sends = [pltpu.make_async_remote_copy(
            src_ref=ins[w], dst_ref=outs[w], send_sem=send_sems.at[w], recv_sem=recv_sems.at[w],
            device_id=(x, y, 1 - c), device_id_type=MESH) for w in range(n)]
        for cp in sends:
            cp.start()
        for cp in sends:
            cp.wait_recv()
        for cp in sends:
            cp.wait_send()

    return pl.pallas_call(
        body, name=name,
        in_specs=[_ANY] * n, out_specs=[_ANY] * n,
        out_shape=[jax.ShapeDtypeStruct(p.shape, p.dtype) for p in partials],
        scratch_shapes=[pltpu.SemaphoreType.DMA((n,)), pltpu.SemaphoreType.DMA((n,))],
        compiler_params=pltpu.CompilerParams(has_side_effects=True),
    )(*partials)


def _row_block(rows, cap):
    best = None
    for cand in range(16, min(rows, cap) + 1, 16):
        if rows % cand == 0:
            best = cand
    assert best is not None, rows
    return best


def _chip_partial(own, received, shard, name):
    _, R, C = own.shape
    rb = _row_block(R, 512)

    def body(shard_ref, own_ref, rec_ref, out_ref):
        acc = own_ref[...]
        for k in range(3):
            acc = acc + rec_ref[k].astype(F32)
        out_ref[...] = acc.astype(BF16)

    return pl.pallas_call(
        body, name=name,
        grid_spec=pltpu.PrefetchScalarGridSpec(
            num_scalar_prefetch=1, grid=(R // rb,),
            in_specs=[pl.BlockSpec((None, rb, C), lambda i, s: (s[0], i, 0)),
                      pl.BlockSpec((3, rb, C), lambda i, s: (0, i, 0))],
            out_specs=pl.BlockSpec((rb, C), lambda i, s: (i, 0))),
        out_shape=jax.ShapeDtypeStruct((R, C), BF16),
        compiler_params=_params(dimension_semantics=("arbitrary",)),
    )(shard, own, received)


def _adamw(w, m, v, g_parts, name, slot=None):
    R, C = w.shape
    by_device = slot is not None
    rb = _row_block(R, 512) if R % 16 == 0 else R

    def body(w_ref, m_ref, v_ref, *refs):
        g_refs, (grad_ref, delta_ref, m_out, v_out) = refs[:-4], refs[-4:]
        if by_device:
            own_ref, land_ref, slot_ref = g_refs
            part = lambda d: jnp.where(slot_ref[0] == d, own_ref[...], land_ref[d])
            g = part(0)
            for d in range(1, N_DEVICES):
                g = g + part(d)
        else:
            g = g_refs[0][...].astype(F32)
            for g_ref in g_refs[1:]:
                g = g + g_ref[...].astype(F32)
        grad_ref[...] = g
        new_m = ADAM_B1 * m_ref[...] + (1.0 - ADAM_B1) * g
        new_v = ADAM_B2 * v_ref[...] + (1.0 - ADAM_B2) * (g * g)
        m_hat = new_m / (1.0 - ADAM_B1 ** ADAM_STEP)
        v_hat = new_v / (1.0 - ADAM_B2 ** ADAM_STEP)
        delta_ref[...] = -ADAM_LR * (m_hat / (jnp.sqrt(v_hat) + ADAM_EPS) + ADAM_WD * w_ref[...])
        m_out[...] = new_m
        v_out[...] = new_v

    spec = pl.BlockSpec((rb, C), lambda i: (i, 0))
    if by_device:
        g_specs = [spec, pl.BlockSpec((N_DEVICES, rb, C), lambda i: (0, i, 0)), _SMEM]
        g_parts = list(g_parts) + [slot]
    else:
        g_specs = [spec] * len(g_parts)
    return pl.pallas_call(
        body, name=name, grid=(R // rb,),
        in_specs=[spec, spec, spec] + g_specs,
        out_specs=[spec] * 4,
        out_shape=[jax.ShapeDtypeStruct((R, C), F32)] * 4,
        compiler_params=_params(dimension_semantics=("arbitrary",)),
    )(w, m, v, *g_parts)


SMALL_NAMES = ("ffn1_pre", "ffn1_post", "mix_pre", "pool_scale", "sinks", "g_pool", "g_attn", "mix_post",
               "ffn2_pre", "ffn2_post", "w_pool")
_SLAB_PART = 8 * _LANES


SLAB_NAMES = SMALL_NAMES + ("loss",)


def _to_slab(parts):
    rows = []
    for name in SLAB_NAMES:
        flat = parts[name].reshape(-1) if name in parts else jnp.zeros((_SLAB_PART,), F32)
        padded = -(-flat.shape[0] // _SLAB_PART) * _SLAB_PART
        rows.append(jnp.pad(flat, (0, padded - flat.shape[0])).reshape(-1, _LANES))
    return jnp.concatenate(rows, axis=0)


def _from_slab(slab, like):
    out, row = {}, 0
    for name in SLAB_NAMES:
        size = like[name].size
        rows = -(-size // _SLAB_PART) * (_SLAB_PART // _LANES)
        out[name] = slab[row:row + rows].reshape(-1)[:size].reshape(like[name].shape)
        row += rows
    return out


BIG_NAMES = ("ffn1_w_gu", "ffn1_w_down", "w_in", "w_out", "ffn2_w_gu", "ffn2_w_down")
WEIGHT_ORDER = ("ffn1_pre", "ffn1_w_gu", "ffn1_w_down", "ffn1_post", "mix_pre", "w_in", "w_pool", "pool_scale",
                "sinks", "g_pool", "g_attn", "w_out", "mix_post", "ffn2_pre", "ffn2_w_gu", "ffn2_w_down", "ffn2_post")


def kernel(x, positions, ffn1_pre, ffn1_w_gu, ffn1_w_down, ffn1_post, mix_pre, w_in, w_pool, pool_scale, sinks, g_pool, g_attn, w_out, mix_post, ffn2_pre, ffn2_w_gu, ffn2_w_down, ffn2_post, loss_target, m_ffn1_pre, m_ffn1_w_gu, m_ffn1_w_down, m_ffn1_post, m_mix_pre, m_w_in, m_w_pool, m_pool_scale, m_sinks, m_g_pool, m_g_attn, m_w_out, m_mix_post, m_ffn2_pre, m_ffn2_w_gu, m_ffn2_w_down, m_ffn2_post, v_ffn1_pre, v_ffn1_w_gu, v_ffn1_w_down, v_ffn1_post, v_mix_pre, v_w_in, v_w_pool, v_pool_scale, v_sinks, v_g_pool, v_g_attn, v_w_out, v_mix_post, v_ffn2_pre, v_ffn2_w_gu, v_ffn2_w_down, v_ffn2_post):
    given = dict(locals())
    weights = {n: given[n][0] for n in WEIGHT_ORDER}
    moments_m = {n: given["m_" + n][0] for n in WEIGHT_ORDER}
    moments_v = {n: given["v_" + n][0] for n in WEIGHT_ORDER}
    S = x.shape[1]
    shard = (2 * lax.axis_index("x") + lax.axis_index("y")).astype(jnp.int32).reshape(1)

    local16 = {n: weights[n].astype(BF16) for n in BIG_NAMES if n != "w_in"}
    local16["w_in"] = weights["w_in"].T.astype(BF16)
    gather_groups = {"ffn1_up": ("ffn1_w_gu",), "ffn1_down": ("ffn1_w_down",), "mixer": ("w_in", "w_out"),
                     "ffn2": ("ffn2_w_gu", "ffn2_w_down")}
    gather_kinds = {"ffn1_up": GATHER_HALF, "ffn1_down": GATHER, "mixer": GATHER, "ffn2": GATHER}
    handles, _ = _exchange_start(list(gather_kinds.values()),
                                 [[local16[n] for n in names] for names in gather_groups.values()], "gather_start")
    gather_handles = dict(zip(gather_groups, handles))

    def weights_of(group, after):
        kind = gather_kinds[group]
        owns, lands = _exchange_wait(kind, gather_handles[group], list(after), "gather_wait_" + group)
        if kind == GATHER_HALF:
            lands = _swap_gathered_halves(lands, "swap_gathered_" + group)
        return list(zip(lands, owns))

    pending, last_token = {}, []

    def grads_ready(group, grads, small=None):
        names = list(grads)
        kinds, sources = [REDUCE], [[grads[n][1] for n in names]]
        if small is not None:
            kinds, sources = kinds + [BROADCAST], sources + [[_to_slab(small)]]
        handles, token = _exchange_start(kinds, sources, "reduce_start_" + group)
        handle = handles[0]
        if small is not None:
            pending["small"] = handles[1]
        pending[group] = (names, handle, [grads[n][0] for n in names])
        last_token[:] = [token]
        return [token]

    small = {n: (weights[n] if weights[n].ndim > 1 else weights[n].reshape(1, -1)) for n in SMALL_NAMES}
    dx = _local_step(x[0], positions.reshape(S, 1), loss_target[0], small, shard, weights_of, grads_ready)

    grad, delta, new_m, new_v = {}, {}, {}, {}

    def finish(groups, after):
        names, partials = [], []
        for group in groups:
            group_names, handle, own32 = pending[group]
            _, received = _exchange_wait(REDUCE, handle, after, "reduce_wait_" + group)
            names += group_names
            partials += [_chip_partial(g32, rec, shard, "chip_partial_" + n)
                         for n, g32, rec in zip(group_names, own32, received)]
        siblings = _swap_with_sibling(partials, "swap_" + groups[0])
        for name, mine, theirs in zip(names, partials, siblings):
            if name == "w_in":
                mine, theirs = mine.T, theirs.T
            grad[name], delta[name], new_m[name], new_v[name] = _adamw(
                weights[name], moments_m[name], moments_v[name], [mine, theirs], "adamw_" + name)
        return [grad[names[-1]]]

    after = finish(["ffn2"], last_token)
    after = finish(["mixer"], after)
    (own_slab,), (slab_landing,) = _exchange_wait(BROADCAST, pending["small"], after, "reduce_wait_small")
    device = (2 * shard + lax.axis_index("c")).astype(jnp.int32)
    small_like = dict({n: small[n] for n in SMALL_NAMES}, loss=jnp.zeros((8, _LANES), F32))
    slabs = _adamw(_to_slab(small), _to_slab({n: moments_m[n] for n in SMALL_NAMES}),
                   _to_slab({n: moments_v[n] for n in SMALL_NAMES}), [own_slab, slab_landing], "adamw_small",
                   slot=device)
    for store, slab in zip((grad, delta, new_m, new_v), slabs):
        store.update(_from_slab(slab, small_like))
    loss = grad["loss"][0, 0]
    after = finish(["ffn1_gu"], [slabs[0]])
    finish(["ffn1_down"], after)

    def out(store):
        return [store[n].reshape(given[n].shape) for n in WEIGHT_ORDER]
    return (loss, dx[None], *out(grad), *out(delta), *out(new_m), *out(new_v))
```
